```python
import math
import jax, jax.numpy as jnp
from jax import lax
import numpy as np

D_MODEL = 1024
BATCH = 8
SEQ = 4096
DEPTH = 2

N_EVEN = (DEPTH + 1) // 2
N_ODD = DEPTH // 2

RET_HEADS = 4
RET_DK = 128
RET_DV = 256
RET_CHUNK = 128
ROPE_BASE = 10000.0
RET_QK = RET_HEADS * RET_DK
RET_V = RET_HEADS * RET_DV
SSM_HEADS = 16
SSM_HEADDIM = 64
SSM_DINNER = SSM_HEADS * SSM_HEADDIM
SSM_STATE = 128
SSM_GROUPS = 2
SSM_CONV = 4
SSM_CHUNK = 128
SSM_XBC = SSM_DINNER + 2 * SSM_GROUPS * SSM_STATE
EVEN_SPLITS = [RET_QK, RET_QK, RET_V, RET_V, SSM_DINNER, SSM_XBC, SSM_HEADS]
EVEN_IN = sum(EVEN_SPLITS)
EVEN_MIX = RET_V + SSM_DINNER
CONF_DIM = D_MODEL // 2
CONF_KERNEL = 31
S5_DIM = D_MODEL // 2
S5_GROUP = 16
S5_GROUPS = S5_DIM // S5_GROUP
S5_STATE = 64
ODD_IN = 2 * CONF_DIM + S5_DIM
ODD_MIX = CONF_DIM + S5_DIM
D_FF = 2816
FFN_CONV = 3
EPS = 1e-6

kernel_name = "hybrid_retention_ssd_conformer_s5_trunk"

F32 = jnp.float32


def rms_norm(x, g, eps=EPS):
    xf = x.astype(F32)
    y = xf * lax.rsqrt(jnp.mean(xf * xf, axis=-1, keepdims=True) + eps)
    return (y * g.astype(F32)).astype(x.dtype)


def layer_norm(x, g, b, eps=EPS):
    xf = x.astype(F32)
    mu = jnp.mean(xf, axis=-1, keepdims=True)
    xc = xf - mu
    var = jnp.mean(xc * xc, axis=-1, keepdims=True)
    return (xc * lax.rsqrt(var + eps) * g.astype(F32) + b.astype(F32)).astype(x.dtype)


def causal_dwconv(x, w, b):
    k = w.shape[0]
    y = lax.conv_general_dilated(
        x, w[:, None, :].astype(x.dtype), window_strides=(1,),
        padding=((k - 1, 0),), dimension_numbers=("NWC", "WIO", "NWC"),
        feature_group_count=x.shape[-1])
    return y + b.astype(x.dtype)


def split_cols(a, sizes):
    return jnp.split(a, np.cumsum(sizes)[:-1].tolist(), axis=-1)


def rotary(x, pos):
    d = x.shape[-1]
    inv = ROPE_BASE ** (-jnp.arange(0, d, 2, dtype=F32) / d)
    ang = pos.astype(F32)[:, None] * inv[None, :]
    cos = jnp.cos(ang)[None, :, None, :]
    sin = jnp.sin(ang)[None, :, None, :]
    x1, x2 = x[..., : d // 2], x[..., d // 2:]
    return jnp.concatenate([x1 * cos - x2 * sin, x1 * sin + x2 * cos], axis=-1)


def retention_chunkwise(q, k, v):
    b, l, h, dk = q.shape
    dv = v.shape[-1]
    c = RET_CHUNK
    nc = l // c
    log_g = jnp.log1p(-(2.0 ** (-5.0 - jnp.arange(h, dtype=F32))))
    idx = jnp.arange(c, dtype=F32)
    diff = idx[:, None] - idx[None, :]
    intra = jnp.where(diff[None] >= 0,
                      jnp.exp(jnp.maximum(diff, 0.0)[None] * log_g[:, None, None]), 0.0)
    q = q.reshape(b, nc, c, h, dk)
    k = k.reshape(b, nc, c, h, dk) * (dk ** -0.5)
    v = v.reshape(b, nc, c, h, dv)
    s = jnp.einsum("bcihd,bcjhd->bchij", q, k) * intra
    inner = jnp.einsum("bchij,bcjhe->bcihe", s, v)
    zeta = jnp.exp((c - 1 - idx)[:, None] * log_g[None, :])
    kv = jnp.einsum("bcjhd,bcjhe->bchde", k * zeta[None, None, :, :, None], v)
    chunk_decay = jnp.exp(c * log_g)[None, :, None, None]

    def step(state, kv_c):
        return state * chunk_decay + kv_c, state

    _, prev = lax.scan(step, jnp.zeros_like(kv[:, 0]), jnp.moveaxis(kv, 1, 0))
    prev = jnp.moveaxis(prev, 0, 1)
    xi = jnp.exp((idx + 1)[:, None] * log_g[None, :])
    cross = jnp.einsum("bcihd,bchde->bcihe", q, prev) * xi[None, None, :, :, None]
    return (inner + cross).reshape(b, l, h, dv)


def head_group_norm(y, eps=EPS):
    mu = jnp.mean(y, axis=-1, keepdims=True)
    yc = y - mu
    return yc * lax.rsqrt(jnp.mean(yc * yc, axis=-1, keepdims=True) + eps)


def ssd_chunked(xh, dt, a_neg, bm, cm):
    b, l, h, p = xh.shape
    g, n = bm.shape[-2], bm.shape[-1]
    hg = h // g
    c = SSM_CHUNK
    nc = l // c
    X = (xh * dt[..., None]).reshape(b, nc, c, g, hg, p)
    acs = jnp.cumsum((dt * a_neg).reshape(b, nc, c, g, hg), axis=2)
    Bc = bm.reshape(b, nc, c, g, n)
    Cc = cm.reshape(b, nc, c, g, n)
    mask = (jnp.arange(c)[:, None] >= jnp.arange(c)[None, :])[:, :, None, None]
    seg = acs[:, :, :, None] - acs[:, :, None, :]
    lmat = jnp.exp(jnp.where(mask, seg, -jnp.inf))
    cb = jnp.einsum("bcign,bcjgn->bcijg", Cc, Bc)
    y_diag = jnp.einsum("bcijgh,bcjghp->bcighp", cb[..., None] * lmat, X)
    decay = jnp.exp(acs[:, :, -1:] - acs)
    states = jnp.einsum("bcjgn,bcjghp->bcghpn", Bc, X * decay[..., None])
    chunk_decay = jnp.exp(acs[:, :, -1])

    def step(state, inp):
        st, cd = inp
        return state * cd[..., None, None] + st, state

    _, prev = lax.scan(step, jnp.zeros_like(states[:, 0]),
                       (jnp.moveaxis(states, 1, 0), jnp.moveaxis(chunk_decay, 1, 0)))
    prev = jnp.moveaxis(prev, 0, 1)
    y_off = jnp.einsum("bcign,bcghpn->bcighp", Cc, prev) * jnp.exp(acs)[..., None]
    return (y_diag + y_off).reshape(b, l, h, p)


def even_mixer(h, w_in, conv_w, conv_b, dt_bias, a_log, d_skip, ssm_norm_w, w_out):
    b, l, _ = h.shape
    proj = h @ w_in
    q, k, v, g, z, xbc, dtr = split_cols(proj, EVEN_SPLITS)
    pos = jnp.arange(l)
    q = rotary(q.reshape(b, l, RET_HEADS, RET_DK).astype(F32), pos)
    k = rotary(k.reshape(b, l, RET_HEADS, RET_DK).astype(F32), pos)
    v = v.reshape(b, l, RET_HEADS, RET_DV).astype(F32)
    r = head_group_norm(retention_chunkwise(q, k, v)).reshape(b, l, RET_V)
    y_ret = jax.nn.silu(g.astype(F32)) * r
    xbc = jax.nn.silu(causal_dwconv(xbc, conv_w, conv_b))
    xs, bm, cm = split_cols(xbc, [SSM_DINNER, SSM_GROUPS * SSM_STATE, SSM_GROUPS * SSM_STATE])
    xs = xs.reshape(b, l, SSM_HEADS, SSM_HEADDIM).astype(F32)
    dt = jax.nn.softplus(dtr.astype(F32) + dt_bias.astype(F32))
    a_neg = -jnp.exp(a_log.astype(F32))
    y = ssd_chunked(xs, dt, a_neg,
                    bm.reshape(b, l, SSM_GROUPS, SSM_STATE).astype(F32),
                    cm.reshape(b, l, SSM_GROUPS, SSM_STATE).astype(F32))
    y = y + d_skip.astype(F32)[:, None] * xs
    y = y.reshape(b, l, SSM_DINNER) * jax.nn.silu(z.astype(F32))
    y = y.reshape(b, l, SSM_GROUPS, SSM_DINNER // SSM_GROUPS)
    y = y * lax.rsqrt(jnp.mean(y * y, axis=-1, keepdims=True) + EPS)
    y_ssm = y.reshape(b, l, SSM_DINNER) * ssm_norm_w.astype(F32)
    mix = jnp.concatenate([y_ret, y_ssm], axis=-1).astype(h.dtype)
    return mix @ w_out


def s5_ssm(u, a_re, a_im, b_re, b_im, c_re, c_im, d_skip, log_step):
    bsz, l, _ = u.shape
    uf = u.astype(F32)
    ug = jnp.moveaxis(uf.reshape(bsz, l, S5_GROUPS, S5_GROUP), 1, 0)
    step = jnp.exp(log_step.astype(F32))[:, None]
    lr, li = a_re.astype(F32), a_im.astype(F32)
    mag = jnp.exp(lr * step)
    ab_re = mag * jnp.cos(li * step)
    ab_im = mag * jnp.sin(li * step)
    den = lr * lr + li * li
    f_re = ((ab_re - 1.0) * lr + ab_im * li) / den
    f_im = (ab_im * lr - (ab_re - 1.0) * li) / den
    br, bi = b_re.astype(F32), b_im.astype(F32)
    bb_re = f_re[..., None] * br - f_im[..., None] * bi
    bb_im = f_re[..., None] * bi + f_im[..., None] * br
    bu_re = jnp.einsum("lbgc,gnc->lbgn", ug, bb_re)
    bu_im = jnp.einsum("lbgc,gnc->lbgn", ug, bb_im)
    a_re_seq = jnp.broadcast_to(ab_re[None, None], (l, 1) + ab_re.shape)
    a_im_seq = jnp.broadcast_to(ab_im[None, None], (l, 1) + ab_im.shape)

    def combine(e1, e2):
        a1r, a1i, b1r, b1i = e1
        a2r, a2i, b2r, b2i = e2
        return (a2r * a1r - a2i * a1i,
                a2r * a1i + a2i * a1r,
                a2r * b1r - a2i * b1i + b2r,
                a2r * b1i + a2i * b1r + b2i)

    _, _, xr, xi = lax.associative_scan(combine, (a_re_seq, a_im_seq, bu_re, bu_im), axis=0)
    y = (jnp.einsum("lbgn,gcn->lbgc", xr, c_re.astype(F32))
         - jnp.einsum("lbgn,gcn->lbgc", xi, c_im.astype(F32)))
    y = jnp.moveaxis(y, 0, 1).reshape(bsz, l, S5_DIM)
    return y + d_skip.astype(F32) * uf


def odd_mixer(h, w_in, conf_dw_w, conf_dw_b, conf_ln_g, conf_ln_b, s5_a_re, s5_a_im,
              s5_b_re, s5_b_im, s5_c_re, s5_c_im, s5_d, s5_log_step, s5_glu_w, w_out):
    proj = h @ w_in
    ca, cg, u = split_cols(proj, [CONF_DIM, CONF_DIM, S5_DIM])
    c = ca * jax.nn.sigmoid(cg)
    c = causal_dwconv(c, conf_dw_w, conf_dw_b)
    c = jax.nn.silu(layer_norm(c, conf_ln_g, conf_ln_b)).astype(F32)
    s = jax.nn.gelu(s5_ssm(u, s5_a_re, s5_a_im, s5_b_re, s5_b_im, s5_c_re, s5_c_im, s5_d, s5_log_step))
    s = s * jax.nn.sigmoid(s @ s5_glu_w.astype(F32))
    mix = jnp.concatenate([c, s], axis=-1).astype(h.dtype)
    return mix @ w_out


def conv_ffn(h, w_up, dw_w, dw_b, w_down):
    a = causal_dwconv(h @ w_up, dw_w, dw_b)
    gate, up = jnp.split(a, 2, axis=-1)
    return (jax.nn.silu(gate) * up) @ w_down


def _fwd_setup_inputs(seed: int = 0) -> dict:
    key = jax.random.key(seed)
    ks = iter(jax.random.split(key, 48))

    def nrm(shape, scale):
        return jax.random.normal(next(ks), shape, F32) * scale

    def gain(shape):
        return 1.0 + 0.02 * jax.random.normal(next(ks), shape, F32)

    x = jax.random.normal(next(ks), (BATCH, SEQ, D_MODEL), F32)
    mix_norm = gain((DEPTH, D_MODEL))
    e_w_in = nrm((N_EVEN, D_MODEL, EVEN_IN), D_MODEL ** -0.5)
    e_conv_w = nrm((N_EVEN, SSM_CONV, SSM_XBC), SSM_CONV ** -0.5)
    e_conv_b = nrm((N_EVEN, SSM_XBC), 0.02)
    dt0 = jnp.exp(jax.random.uniform(next(ks), (N_EVEN, SSM_HEADS), F32,
                                     math.log(1e-3), math.log(1e-1)))
    e_dt_bias = dt0 + jnp.log(-jnp.expm1(-dt0))
    e_a_log = jnp.log(jax.random.uniform(next(ks), (N_EVEN, SSM_HEADS), F32, 1.0, 16.0))
    e_d = gain((N_EVEN, SSM_HEADS))
    e_ssm_norm = gain((N_EVEN, SSM_DINNER))
    e_w_out = nrm((N_EVEN, EVEN_MIX, D_MODEL), EVEN_MIX ** -0.5)
    o_w_in = nrm((N_ODD, D_MODEL, ODD_IN), D_MODEL ** -0.5)
    o_dw_w = nrm((N_ODD, CONF_KERNEL, CONF_DIM), CONF_KERNEL ** -0.5)
    o_dw_b = nrm((N_ODD, CONF_DIM), 0.02)
    o_ln_g = gain((N_ODD, CONF_DIM))
    o_ln_b = nrm((N_ODD, CONF_DIM), 0.02)
    o_a_re = -0.5 + nrm((N_ODD, S5_GROUPS, S5_STATE), 0.01)
    o_a_im = (math.pi * jnp.arange(S5_STATE, dtype=F32))[None, None, :] + nrm((N_ODD, S5_GROUPS, S5_STATE), 0.01)
    o_b_re = nrm((N_ODD, S5_GROUPS, S5_STATE, S5_GROUP), S5_GROUP ** -0.5)
    o_b_im = nrm((N_ODD, S5_GROUPS, S5_STATE, S5_GROUP), S5_GROUP ** -0.5)
    o_c_re = nrm((N_ODD, S5_GROUPS, S5_GROUP, S5_STATE), S5_STATE ** -0.5)
    o_c_im = nrm((N_ODD, S5_GROUPS, S5_GROUP, S5_STATE), S5_STATE ** -0.5)
    o_d = nrm((N_ODD, S5_DIM), 1.0)
    o_log_step = jax.random.uniform(next(ks), (N_ODD, S5_GROUPS), F32, math.log(1e-3), math.log(1e-1))
    o_glu_w = nrm((N_ODD, S5_DIM, S5_DIM), S5_DIM ** -0.5)
    o_w_out = nrm((N_ODD, ODD_MIX, D_MODEL), ODD_MIX ** -0.5)
    ffn_norm = gain((DEPTH, D_MODEL))
    ffn_w_up = nrm((DEPTH, D_MODEL, 2 * D_FF), D_MODEL ** -0.5)
    ffn_dw_w = nrm((DEPTH, FFN_CONV, 2 * D_FF), FFN_CONV ** -0.5)
    ffn_dw_b = nrm((DEPTH, 2 * D_FF), 0.02)
    ffn_w_down = nrm((DEPTH, D_FF, D_MODEL), D_FF ** -0.5)
    final_norm = gain((D_MODEL,))
    return {
        "x": x, "mix_norm": mix_norm,
        "e_w_in": e_w_in, "e_conv_w": e_conv_w, "e_conv_b": e_conv_b,
        "e_dt_bias": e_dt_bias, "e_a_log": e_a_log, "e_d": e_d,
        "e_ssm_norm": e_ssm_norm, "e_w_out": e_w_out,
        "o_w_in": o_w_in, "o_dw_w": o_dw_w, "o_dw_b": o_dw_b,
        "o_ln_g": o_ln_g, "o_ln_b": o_ln_b, "o_a_re": o_a_re, "o_a_im": o_a_im,
        "o_b_re": o_b_re, "o_b_im": o_b_im, "o_c_re": o_c_re, "o_c_im": o_c_im,
        "o_d": o_d, "o_log_step": o_log_step, "o_glu_w": o_glu_w, "o_w_out": o_w_out,
        "ffn_norm": ffn_norm, "ffn_w_up": ffn_w_up, "ffn_dw_w": ffn_dw_w,
        "ffn_dw_b": ffn_dw_b, "ffn_w_down": ffn_w_down, "final_norm": final_norm,
    }


def _fwd_reference(x, mix_norm, e_w_in, e_conv_w, e_conv_b, e_dt_bias, e_a_log, e_d,
              e_ssm_norm, e_w_out, o_w_in, o_dw_w, o_dw_b, o_ln_g, o_ln_b, o_a_re,
              o_a_im, o_b_re, o_b_im, o_c_re, o_c_im, o_d, o_log_step, o_glu_w,
              o_w_out, ffn_norm, ffn_w_up, ffn_dw_w, ffn_dw_b, ffn_w_down, final_norm):
    for i in range(DEPTH):
        j = i // 2
        hn = rms_norm(x, mix_norm[i])
        if i % 2 == 0:
            m = even_mixer(hn, e_w_in[j], e_conv_w[j], e_conv_b[j], e_dt_bias[j],
                           e_a_log[j], e_d[j], e_ssm_norm[j], e_w_out[j])
        else:
            m = odd_mixer(hn, o_w_in[j], o_dw_w[j], o_dw_b[j], o_ln_g[j], o_ln_b[j],
                          o_a_re[j], o_a_im[j], o_b_re[j], o_b_im[j], o_c_re[j],
                          o_c_im[j], o_d[j], o_log_step[j], o_glu_w[j], o_w_out[j])
        x = x + m.astype(x.dtype)
        f = conv_ffn(rms_norm(x, ffn_norm[i]), ffn_w_up[i], ffn_dw_w[i], ffn_dw_b[i], ffn_w_down[i])
        x = x + f.astype(x.dtype)
    return rms_norm(x, final_norm)


import jax as _jax
import jax.numpy as _jnp

TWIN_FORMAT = 'train_step'
FWD_PARAMS = ['x', 'mix_norm', 'e_w_in', 'e_conv_w', 'e_conv_b', 'e_dt_bias', 'e_a_log', 'e_d', 'e_ssm_norm', 'e_w_out', 'o_w_in', 'o_dw_w', 'o_dw_b', 'o_ln_g', 'o_ln_b', 'o_a_re', 'o_a_im', 'o_b_re', 'o_b_im', 'o_c_re', 'o_c_im', 'o_d', 'o_log_step', 'o_glu_w', 'o_w_out', 'ffn_norm', 'ffn_w_up', 'ffn_dw_w', 'ffn_dw_b', 'ffn_w_down', 'final_norm']
TWIN_WEIGHTS = ['mix_norm', 'e_w_in', 'e_conv_w', 'e_conv_b', 'e_dt_bias', 'e_a_log', 'e_d', 'e_ssm_norm', 'e_w_out', 'o_w_in', 'o_dw_w', 'o_dw_b', 'o_ln_g', 'o_ln_b', 'o_a_re', 'o_a_im', 'o_b_re', 'o_b_im', 'o_c_re', 'o_c_im', 'o_d', 'o_log_step', 'o_glu_w', 'o_w_out', 'ffn_norm', 'ffn_w_up', 'ffn_dw_w', 'ffn_dw_b', 'ffn_w_down', 'final_norm']
TWIN_DIFF_INPUT = 'x'
TWIN_INPUTS = ['x', 'mix_norm', 'e_w_in', 'e_conv_w', 'e_conv_b', 'e_dt_bias', 'e_a_log', 'e_d', 'e_ssm_norm', 'e_w_out', 'o_w_in', 'o_dw_w', 'o_dw_b', 'o_ln_g', 'o_ln_b', 'o_a_re', 'o_a_im', 'o_b_re', 'o_b_im', 'o_c_re', 'o_c_im', 'o_d', 'o_log_step', 'o_glu_w', 'o_w_out', 'ffn_norm', 'ffn_w_up', 'ffn_dw_w', 'ffn_dw_b', 'ffn_w_down', 'final_norm', 'loss_target', 'm_mix_norm', 'm_e_w_in', 'm_e_conv_w', 'm_e_conv_b', 'm_e_dt_bias', 'm_e_a_log', 'm_e_d', 'm_e_ssm_norm', 'm_e_w_out', 'm_o_w_in', 'm_o_dw_w', 'm_o_dw_b', 'm_o_ln_g', 'm_o_ln_b', 'm_o_a_re', 'm_o_a_im', 'm_o_b_re', 'm_o_b_im', 'm_o_c_re', 'm_o_c_im', 'm_o_d', 'm_o_log_step', 'm_o_glu_w', 'm_o_w_out', 'm_ffn_norm', 'm_ffn_w_up', 'm_ffn_dw_w', 'm_ffn_dw_b', 'm_ffn_w_down', 'm_final_norm', 'v_mix_norm', 'v_e_w_in', 'v_e_conv_w', 'v_e_conv_b', 'v_e_dt_bias', 'v_e_a_log', 'v_e_d', 'v_e_ssm_norm', 'v_e_w_out', 'v_o_w_in', 'v_o_dw_w', 'v_o_dw_b', 'v_o_ln_g', 'v_o_ln_b', 'v_o_a_re', 'v_o_a_im', 'v_o_b_re', 'v_o_b_im', 'v_o_c_re', 'v_o_c_im', 'v_o_d', 'v_o_log_step', 'v_o_glu_w', 'v_o_w_out', 'v_ffn_norm', 'v_ffn_w_up', 'v_ffn_dw_w', 'v_ffn_dw_b', 'v_ffn_w_down', 'v_final_norm']
TWIN_OUTPUTS = ['loss', 'grad_x', 'grad_mix_norm', 'grad_e_w_in', 'grad_e_conv_w', 'grad_e_conv_b', 'grad_e_dt_bias', 'grad_e_a_log', 'grad_e_d', 'grad_e_ssm_norm', 'grad_e_w_out', 'grad_o_w_in', 'grad_o_dw_w', 'grad_o_dw_b', 'grad_o_ln_g', 'grad_o_ln_b', 'grad_o_a_re', 'grad_o_a_im', 'grad_o_b_re', 'grad_o_b_im', 'grad_o_c_re', 'grad_o_c_im', 'grad_o_d', 'grad_o_log_step', 'grad_o_glu_w', 'grad_o_w_out', 'grad_ffn_norm', 'grad_ffn_w_up', 'grad_ffn_dw_w', 'grad_ffn_dw_b', 'grad_ffn_w_down', 'grad_final_norm', 'delta_mix_norm', 'delta_e_w_in', 'delta_e_conv_w', 'delta_e_conv_b', 'delta_e_dt_bias', 'delta_e_a_log', 'delta_e_d', 'delta_e_ssm_norm', 'delta_e_w_out', 'delta_o_w_in', 'delta_o_dw_w', 'delta_o_dw_b', 'delta_o_ln_g', 'delta_o_ln_b', 'delta_o_a_re', 'delta_o_a_im', 'delta_o_b_re', 'delta_o_b_im', 'delta_o_c_re', 'delta_o_c_im', 'delta_o_d', 'delta_o_log_step', 'delta_o_glu_w', 'delta_o_w_out', 'delta_ffn_norm', 'delta_ffn_w_up', 'delta_ffn_dw_w', 'delta_ffn_dw_b', 'delta_ffn_w_down', 'delta_final_norm', 'new_m_mix_norm', 'new_m_e_w_in', 'new_m_e_conv_w', 'new_m_e_conv_b', 'new_m_e_dt_bias', 'new_m_e_a_log', 'new_m_e_d', 'new_m_e_ssm_norm', 'new_m_e_w_out', 'new_m_o_w_in', 'new_m_o_dw_w', 'new_m_o_dw_b', 'new_m_o_ln_g', 'new_m_o_ln_b', 'new_m_o_a_re', 'new_m_o_a_im', 'new_m_o_b_re', 'new_m_o_b_im', 'new_m_o_c_re', 'new_m_o_c_im', 'new_m_o_d', 'new_m_o_log_step', 'new_m_o_glu_w', 'new_m_o_w_out', 'new_m_ffn_norm', 'new_m_ffn_w_up', 'new_m_ffn_dw_w', 'new_m_ffn_dw_b', 'new_m_ffn_w_down', 'new_m_final_norm', 'new_v_mix_norm', 'new_v_e_w_in', 'new_v_e_conv_w', 'new_v_e_conv_b', 'new_v_e_dt_bias', 'new_v_e_a_log', 'new_v_e_d', 'new_v_e_ssm_norm', 'new_v_e_w_out', 'new_v_o_w_in', 'new_v_o_dw_w', 'new_v_o_dw_b', 'new_v_o_ln_g', 'new_v_o_ln_b', 'new_v_o_a_re', 'new_v_o_a_im', 'new_v_o_b_re', 'new_v_o_b_im', 'new_v_o_c_re', 'new_v_o_c_im', 'new_v_o_d', 'new_v_o_log_step', 'new_v_o_glu_w', 'new_v_o_w_out', 'new_v_ffn_norm', 'new_v_ffn_w_up', 'new_v_ffn_dw_w', 'new_v_ffn_dw_b', 'new_v_ffn_w_down', 'new_v_final_norm']
TWIN_LEAF_KINDS = {'loss': 'loss', 'grad_x': 'grad_x', 'grad_mix_norm': 'grad_w', 'grad_e_w_in': 'grad_w', 'grad_e_conv_w': 'grad_w', 'grad_e_conv_b': 'grad_w', 'grad_e_dt_bias': 'grad_w', 'grad_e_a_log': 'grad_w', 'grad_e_d': 'grad_w', 'grad_e_ssm_norm': 'grad_w', 'grad_e_w_out': 'grad_w', 'grad_o_w_in': 'grad_w', 'grad_o_dw_w': 'grad_w', 'grad_o_dw_b': 'grad_w', 'grad_o_ln_g': 'grad_w', 'grad_o_ln_b': 'grad_w', 'grad_o_a_re': 'grad_w', 'grad_o_a_im': 'grad_w', 'grad_o_b_re': 'grad_w', 'grad_o_b_im': 'grad_w', 'grad_o_c_re': 'grad_w', 'grad_o_c_im': 'grad_w', 'grad_o_d': 'grad_w', 'grad_o_log_step': 'grad_w', 'grad_o_glu_w': 'grad_w', 'grad_o_w_out': 'grad_w', 'grad_ffn_norm': 'grad_w', 'grad_ffn_w_up': 'grad_w', 'grad_ffn_dw_w': 'grad_w', 'grad_ffn_dw_b': 'grad_w', 'grad_ffn_w_down': 'grad_w', 'grad_final_norm': 'grad_w', 'delta_mix_norm': 'delta_w', 'delta_e_w_in': 'delta_w', 'delta_e_conv_w': 'delta_w', 'delta_e_conv_b': 'delta_w', 'delta_e_dt_bias': 'delta_w', 'delta_e_a_log': 'delta_w', 'delta_e_d': 'delta_w', 'delta_e_ssm_norm': 'delta_w', 'delta_e_w_out': 'delta_w', 'delta_o_w_in': 'delta_w', 'delta_o_dw_w': 'delta_w', 'delta_o_dw_b': 'delta_w', 'delta_o_ln_g': 'delta_w', 'delta_o_ln_b': 'delta_w', 'delta_o_a_re': 'delta_w', 'delta_o_a_im': 'delta_w', 'delta_o_b_re': 'delta_w', 'delta_o_b_im': 'delta_w', 'delta_o_c_re': 'delta_w', 'delta_o_c_im': 'delta_w', 'delta_o_d': 'delta_w', 'delta_o_log_step': 'delta_w', 'delta_o_glu_w': 'delta_w', 'delta_o_w_out': 'delta_w', 'delta_ffn_norm': 'delta_w', 'delta_ffn_w_up': 'delta_w', 'delta_ffn_dw_w': 'delta_w', 'delta_ffn_dw_b': 'delta_w', 'delta_ffn_w_down': 'delta_w', 'delta_final_norm': 'delta_w', 'new_m_mix_norm': 'new_m', 'new_m_e_w_in': 'new_m', 'new_m_e_conv_w': 'new_m', 'new_m_e_conv_b': 'new_m', 'new_m_e_dt_bias': 'new_m', 'new_m_e_a_log': 'new_m', 'new_m_e_d': 'new_m', 'new_m_e_ssm_norm': 'new_m', 'new_m_e_w_out': 'new_m', 'new_m_o_w_in': 'new_m', 'new_m_o_dw_w': 'new_m', 'new_m_o_dw_b': 'new_m', 'new_m_o_ln_g': 'new_m', 'new_m_o_ln_b': 'new_m', 'new_m_o_a_re': 'new_m', 'new_m_o_a_im': 'new_m', 'new_m_o_b_re': 'new_m', 'new_m_o_b_im': 'new_m', 'new_m_o_c_re': 'new_m', 'new_m_o_c_im': 'new_m', 'new_m_o_d': 'new_m', 'new_m_o_log_step': 'new_m', 'new_m_o_glu_w': 'new_m', 'new_m_o_w_out': 'new_m', 'new_m_ffn_norm': 'new_m', 'new_m_ffn_w_up': 'new_m', 'new_m_ffn_dw_w': 'new_m', 'new_m_ffn_dw_b': 'new_m', 'new_m_ffn_w_down': 'new_m', 'new_m_final_norm': 'new_m', 'new_v_mix_norm': 'new_v', 'new_v_e_w_in': 'new_v', 'new_v_e_conv_w': 'new_v', 'new_v_e_conv_b': 'new_v', 'new_v_e_dt_bias': 'new_v', 'new_v_e_a_log': 'new_v', 'new_v_e_d': 'new_v', 'new_v_e_ssm_norm': 'new_v', 'new_v_e_w_out': 'new_v', 'new_v_o_w_in': 'new_v', 'new_v_o_dw_w': 'new_v', 'new_v_o_dw_b': 'new_v', 'new_v_o_ln_g': 'new_v', 'new_v_o_ln_b': 'new_v', 'new_v_o_a_re': 'new_v', 'new_v_o_a_im': 'new_v', 'new_v_o_b_re': 'new_v', 'new_v_o_b_im': 'new_v', 'new_v_o_c_re': 'new_v', 'new_v_o_c_im': 'new_v', 'new_v_o_d': 'new_v', 'new_v_o_log_step': 'new_v', 'new_v_o_glu_w': 'new_v', 'new_v_o_w_out': 'new_v', 'new_v_ffn_norm': 'new_v', 'new_v_ffn_w_up': 'new_v', 'new_v_ffn_dw_w': 'new_v', 'new_v_ffn_dw_b': 'new_v', 'new_v_ffn_w_down': 'new_v', 'new_v_final_norm': 'new_v'}


def _forward(args):
    return _fwd_reference(*[args[k] for k in FWD_PARAMS])


def _output_shape():
    def fwd():
        inp = _fwd_setup_inputs(0)
        return _fwd_reference(*[inp[k] for k in FWD_PARAMS])
    out = _jax.eval_shape(fwd)
    return out.shape, out.dtype

N_MICROBATCH = 1
ADAM_LR = 0.001
ADAM_B1 = 0.9
ADAM_B2 = 0.999
ADAM_EPS = 1e-08
ADAM_WD = 0.01
ADAM_STEP = 10
PER_EXAMPLE_BATCH_AXIS = {'x': 0, 'loss_target': 0}
SHARED_INPUTS = []
_WEIGHT_DTYPES = {'mix_norm': _jnp.float32, 'e_w_in': _jnp.float32, 'e_conv_w': _jnp.float32, 'e_conv_b': _jnp.float32, 'e_dt_bias': _jnp.float32, 'e_a_log': _jnp.float32, 'e_d': _jnp.float32, 'e_ssm_norm': _jnp.float32, 'e_w_out': _jnp.float32, 'o_w_in': _jnp.float32, 'o_dw_w': _jnp.float32, 'o_dw_b': _jnp.float32, 'o_ln_g': _jnp.float32, 'o_ln_b': _jnp.float32, 'o_a_re': _jnp.float32, 'o_a_im': _jnp.float32, 'o_b_re': _jnp.float32, 'o_b_im': _jnp.float32, 'o_c_re': _jnp.float32, 'o_c_im': _jnp.float32, 'o_d': _jnp.float32, 'o_log_step': _jnp.float32, 'o_glu_w': _jnp.float32, 'o_w_out': _jnp.float32, 'ffn_norm': _jnp.float32, 'ffn_w_up': _jnp.float32, 'ffn_dw_w': _jnp.float32, 'ffn_dw_b': _jnp.float32, 'ffn_w_down': _jnp.float32, 'final_norm': _jnp.float32}
MOMENT_SCALE = {'mix_norm': 1.670146e-01, 'e_w_in': 9.559758e-02, 'e_conv_w': 1.037254e-01, 'e_conv_b': 1.311840e-01, 'e_dt_bias': 3.832969e-01, 'e_a_log': 2.401668e-01, 'e_d': 1.162865e+00, 'e_ssm_norm': 1.180095e-01, 'e_w_out': 1.338025e-01, 'o_w_in': 5.512548e-02, 'o_dw_w': 7.763298e-02, 'o_dw_b': 1.543962e-01, 'o_ln_g': 9.306416e-02, 'o_ln_b': 8.471061e-02, 'o_a_re': 4.132953e-03, 'o_a_im': 4.336925e-03, 'o_b_re': 1.992381e-03, 'o_b_im': 2.024929e-03, 'o_c_re': 4.012643e-03, 'o_c_im': 3.929808e-03, 'o_d': 4.757453e-02, 'o_log_step': 3.425611e+00, 'o_glu_w': 1.283417e-02, 'o_w_out': 6.321272e-02, 'ffn_norm': 1.131306e-01, 'ffn_w_up': 4.602053e-02, 'ffn_dw_w': 4.622220e-02, 'ffn_dw_b': 4.490349e-02, 'ffn_w_down': 7.551054e-02, 'final_norm': 3.200744e+01}


def _to_microbatches(a, axis):
    t = _jnp.moveaxis(a, axis, 0)
    t = t.reshape((N_MICROBATCH, t.shape[0] // N_MICROBATCH) + t.shape[1:])
    return _jnp.moveaxis(t, 1, axis + 1)


def setup_inputs(seed: int = 0) -> dict:
    inp = _fwd_setup_inputs(seed)
    key = _jax.random.fold_in(_jax.random.key(seed), 7919)
    shape, _ = _output_shape()
    out = dict(inp)
    out["loss_target"] = _jax.random.normal(_jax.random.fold_in(key, 0), shape, _jnp.float32)
    for i, name in enumerate(TWIN_WEIGHTS):
        w = inp[name].astype(_jnp.float32)
        if MOMENT_SCALE is None:
            s = _jnp.sqrt(_jnp.mean(_jnp.square(w)) + 1e-30)
        else:
            s = MOMENT_SCALE[name]
        km, kv = _jax.random.split(_jax.random.fold_in(key, i + 1))
        out[name] = w
        out["m_" + name] = s * _jax.random.normal(km, w.shape, _jnp.float32)
        out["v_" + name] = (s * s) * _jax.random.uniform(kv, w.shape, _jnp.float32, 0.5, 1.5)
    if N_MICROBATCH > 1:
        for name, axis in PER_EXAMPLE_BATCH_AXIS.items():
            out[name] = _to_microbatches(out[name], axis)
    return {'x': out['x'], 'mix_norm': out['mix_norm'], 'e_w_in': out['e_w_in'], 'e_conv_w': out['e_conv_w'], 'e_conv_b': out['e_conv_b'], 'e_dt_bias': out['e_dt_bias'], 'e_a_log': out['e_a_log'], 'e_d': out['e_d'], 'e_ssm_norm': out['e_ssm_norm'], 'e_w_out': out['e_w_out'], 'o_w_in': out['o_w_in'], 'o_dw_w': out['o_dw_w'], 'o_dw_b': out['o_dw_b'], 'o_ln_g': out['o_ln_g'], 'o_ln_b': out['o_ln_b'], 'o_a_re': out['o_a_re'], 'o_a_im': out['o_a_im'], 'o_b_re': out['o_b_re'], 'o_b_im': out['o_b_im'], 'o_c_re': out['o_c_re'], 'o_c_im': out['o_c_im'], 'o_d': out['o_d'], 'o_log_step': out['o_log_step'], 'o_glu_w': out['o_glu_w'], 'o_w_out': out['o_w_out'], 'ffn_norm': out['ffn_norm'], 'ffn_w_up': out['ffn_w_up'], 'ffn_dw_w': out['ffn_dw_w'], 'ffn_dw_b': out['ffn_dw_b'], 'ffn_w_down': out['ffn_w_down'], 'final_norm': out['final_norm'], 'loss_target': out['loss_target'], 'm_mix_norm': out['m_mix_norm'], 'm_e_w_in': out['m_e_w_in'], 'm_e_conv_w': out['m_e_conv_w'], 'm_e_conv_b': out['m_e_conv_b'], 'm_e_dt_bias': out['m_e_dt_bias'], 'm_e_a_log': out['m_e_a_log'], 'm_e_d': out['m_e_d'], 'm_e_ssm_norm': out['m_e_ssm_norm'], 'm_e_w_out': out['m_e_w_out'], 'm_o_w_in': out['m_o_w_in'], 'm_o_dw_w': out['m_o_dw_w'], 'm_o_dw_b': out['m_o_dw_b'], 'm_o_ln_g': out['m_o_ln_g'], 'm_o_ln_b': out['m_o_ln_b'], 'm_o_a_re': out['m_o_a_re'], 'm_o_a_im': out['m_o_a_im'], 'm_o_b_re': out['m_o_b_re'], 'm_o_b_im': out['m_o_b_im'], 'm_o_c_re': out['m_o_c_re'], 'm_o_c_im': out['m_o_c_im'], 'm_o_d': out['m_o_d'], 'm_o_log_step': out['m_o_log_step'], 'm_o_glu_w': out['m_o_glu_w'], 'm_o_w_out': out['m_o_w_out'], 'm_ffn_norm': out['m_ffn_norm'], 'm_ffn_w_up': out['m_ffn_w_up'], 'm_ffn_dw_w': out['m_ffn_dw_w'], 'm_ffn_dw_b': out['m_ffn_dw_b'], 'm_ffn_w_down': out['m_ffn_w_down'], 'm_final_norm': out['m_final_norm'], 'v_mix_norm': out['v_mix_norm'], 'v_e_w_in': out['v_e_w_in'], 'v_e_conv_w': out['v_e_conv_w'], 'v_e_conv_b': out['v_e_conv_b'], 'v_e_dt_bias': out['v_e_dt_bias'], 'v_e_a_log': out['v_e_a_log'], 'v_e_d': out['v_e_d'], 'v_e_ssm_norm': out['v_e_ssm_norm'], 'v_e_w_out': out['v_e_w_out'], 'v_o_w_in': out['v_o_w_in'], 'v_o_dw_w': out['v_o_dw_w'], 'v_o_dw_b': out['v_o_dw_b'], 'v_o_ln_g': out['v_o_ln_g'], 'v_o_ln_b': out['v_o_ln_b'], 'v_o_a_re': out['v_o_a_re'], 'v_o_a_im': out['v_o_a_im'], 'v_o_b_re': out['v_o_b_re'], 'v_o_b_im': out['v_o_b_im'], 'v_o_c_re': out['v_o_c_re'], 'v_o_c_im': out['v_o_c_im'], 'v_o_d': out['v_o_d'], 'v_o_log_step': out['v_o_log_step'], 'v_o_glu_w': out['v_o_glu_w'], 'v_o_w_out': out['v_o_w_out'], 'v_ffn_norm': out['v_ffn_norm'], 'v_ffn_w_up': out['v_ffn_w_up'], 'v_ffn_dw_w': out['v_ffn_dw_w'], 'v_ffn_dw_b': out['v_ffn_dw_b'], 'v_ffn_w_down': out['v_ffn_w_down'], 'v_final_norm': out['v_final_norm']}


def _loss(weights, diff, rest, loss_target):
    with _jax.named_scope("forward"):
        args = {**rest, TWIN_DIFF_INPUT: diff, **{k: w.astype(_WEIGHT_DTYPES[k]) for k, w in weights.items()}}
        y = _forward(args)
    with _jax.named_scope("loss_head"):
        err = _jnp.square(y.astype(_jnp.float32) - loss_target)
        return 0.5 * _jnp.sum(_jnp.mean(err, axis=-1)) if err.ndim else 0.5 * err


def _adamw(w, g, m, v):
    m = ADAM_B1 * m + (1.0 - ADAM_B1) * g
    v = ADAM_B2 * v + (1.0 - ADAM_B2) * _jnp.square(g)
    m_hat = m / (1.0 - ADAM_B1 ** ADAM_STEP)
    v_hat = v / (1.0 - ADAM_B2 ** ADAM_STEP)
    delta = -ADAM_LR * (m_hat / (_jnp.sqrt(v_hat) + ADAM_EPS) + ADAM_WD * w)
    return delta, m, v


def reference(x, mix_norm, e_w_in, e_conv_w, e_conv_b, e_dt_bias, e_a_log, e_d, e_ssm_norm, e_w_out, o_w_in, o_dw_w, o_dw_b, o_ln_g, o_ln_b, o_a_re, o_a_im, o_b_re, o_b_im, o_c_re, o_c_im, o_d, o_log_step, o_glu_w, o_w_out, ffn_norm, ffn_w_up, ffn_dw_w, ffn_dw_b, ffn_w_down, final_norm, loss_target, m_mix_norm, m_e_w_in, m_e_conv_w, m_e_conv_b, m_e_dt_bias, m_e_a_log, m_e_d, m_e_ssm_norm, m_e_w_out, m_o_w_in, m_o_dw_w, m_o_dw_b, m_o_ln_g, m_o_ln_b, m_o_a_re, m_o_a_im, m_o_b_re, m_o_b_im, m_o_c_re, m_o_c_im, m_o_d, m_o_log_step, m_o_glu_w, m_o_w_out, m_ffn_norm, m_ffn_w_up, m_ffn_dw_w, m_ffn_dw_b, m_ffn_w_down, m_final_norm, v_mix_norm, v_e_w_in, v_e_conv_w, v_e_conv_b, v_e_dt_bias, v_e_a_log, v_e_d, v_e_ssm_norm, v_e_w_out, v_o_w_in, v_o_dw_w, v_o_dw_b, v_o_ln_g, v_o_ln_b, v_o_a_re, v_o_a_im, v_o_b_re, v_o_b_im, v_o_c_re, v_o_c_im, v_o_d, v_o_log_step, v_o_glu_w, v_o_w_out, v_ffn_norm, v_ffn_w_up, v_ffn_dw_w, v_ffn_dw_b, v_ffn_w_down, v_final_norm):
    given = dict(x=x, mix_norm=mix_norm, e_w_in=e_w_in, e_conv_w=e_conv_w, e_conv_b=e_conv_b, e_dt_bias=e_dt_bias, e_a_log=e_a_log, e_d=e_d, e_ssm_norm=e_ssm_norm, e_w_out=e_w_out, o_w_in=o_w_in, o_dw_w=o_dw_w, o_dw_b=o_dw_b, o_ln_g=o_ln_g, o_ln_b=o_ln_b, o_a_re=o_a_re, o_a_im=o_a_im, o_b_re=o_b_re, o_b_im=o_b_im, o_c_re=o_c_re, o_c_im=o_c_im, o_d=o_d, o_log_step=o_log_step, o_glu_w=o_glu_w, o_w_out=o_w_out, ffn_norm=ffn_norm, ffn_w_up=ffn_w_up, ffn_dw_w=ffn_dw_w, ffn_dw_b=ffn_dw_b, ffn_w_down=ffn_w_down, final_norm=final_norm, loss_target=loss_target, m_mix_norm=m_mix_norm, m_e_w_in=m_e_w_in, m_e_conv_w=m_e_conv_w, m_e_conv_b=m_e_conv_b, m_e_dt_bias=m_e_dt_bias, m_e_a_log=m_e_a_log, m_e_d=m_e_d, m_e_ssm_norm=m_e_ssm_norm, m_e_w_out=m_e_w_out, m_o_w_in=m_o_w_in, m_o_dw_w=m_o_dw_w, m_o_dw_b=m_o_dw_b, m_o_ln_g=m_o_ln_g, m_o_ln_b=m_o_ln_b, m_o_a_re=m_o_a_re, m_o_a_im=m_o_a_im, m_o_b_re=m_o_b_re, m_o_b_im=m_o_b_im, m_o_c_re=m_o_c_re, m_o_c_im=m_o_c_im, m_o_d=m_o_d, m_o_log_step=m_o_log_step, m_o_glu_w=m_o_glu_w, m_o_w_out=m_o_w_out, m_ffn_norm=m_ffn_norm, m_ffn_w_up=m_ffn_w_up, m_ffn_dw_w=m_ffn_dw_w, m_ffn_dw_b=m_ffn_dw_b, m_ffn_w_down=m_ffn_w_down, m_final_norm=m_final_norm, v_mix_norm=v_mix_norm, v_e_w_in=v_e_w_in, v_e_conv_w=v_e_conv_w, v_e_conv_b=v_e_conv_b, v_e_dt_bias=v_e_dt_bias, v_e_a_log=v_e_a_log, v_e_d=v_e_d, v_e_ssm_norm=v_e_ssm_norm, v_e_w_out=v_e_w_out, v_o_w_in=v_o_w_in, v_o_dw_w=v_o_dw_w, v_o_dw_b=v_o_dw_b, v_o_ln_g=v_o_ln_g, v_o_ln_b=v_o_ln_b, v_o_a_re=v_o_a_re, v_o_a_im=v_o_a_im, v_o_b_re=v_o_b_re, v_o_b_im=v_o_b_im, v_o_c_re=v_o_c_re, v_o_c_im=v_o_c_im, v_o_d=v_o_d, v_o_log_step=v_o_log_step, v_o_glu_w=v_o_glu_w, v_o_w_out=v_o_w_out, v_ffn_norm=v_ffn_norm, v_ffn_w_up=v_ffn_w_up, v_ffn_dw_w=v_ffn_dw_w, v_ffn_dw_b=v_ffn_dw_b, v_ffn_w_down=v_ffn_w_down, v_final_norm=v_final_norm)
    weights = {n: given[n] for n in TWIN_WEIGHTS}
    shared = {n: given[n] for n in SHARED_INPUTS}
    per_example = {n: given[n] for n in ['x']}
    grad_fn = _jax.value_and_grad(_loss, argnums=(0, 1))

    def one_microbatch(ex, loss_target):
        ex = dict(ex)
        diff = ex.pop(TWIN_DIFF_INPUT)
        return grad_fn(weights, diff, {**shared, **ex}, loss_target)

    if N_MICROBATCH == 1:
        loss, (grad_w, grad_x) = one_microbatch(per_example, given["loss_target"])
    else:
        def body(carry, xs):
            loss_sum, grad_sum = carry
            l_k, (gw_k, gx_k) = one_microbatch(xs[0], xs[1])
            with _jax.named_scope("update"):
                return (loss_sum + l_k, _jax.tree.map(_jnp.add, grad_sum, gw_k)), gx_k

        init = (_jnp.zeros((), _jnp.float32), _jax.tree.map(_jnp.zeros_like, weights))
        (loss, grad_w), grad_x = _jax.lax.scan(body, init, (per_example, given["loss_target"]))
    with _jax.named_scope("update"):
        delta_w, new_m, new_v = {}, {}, {}
        for n in TWIN_WEIGHTS:
            delta_w[n], new_m[n], new_v[n] = _adamw(weights[n], grad_w[n], given["m_" + n], given["v_" + n])
    return (loss, grad_x, *[grad_w[n] for n in TWIN_WEIGHTS], *[delta_w[n] for n in TWIN_WEIGHTS],
            *[new_m[n] for n in TWIN_WEIGHTS], *[new_v[n] for n in TWIN_WEIGHTS])
```

```python
import functools
import math
from typing import NamedTuple

import numpy as np
import jax
import jax.numpy as jnp
from jax import lax
from jax.experimental import pallas as pl
from jax.experimental.pallas import tpu as pltpu

F32 = jnp.float32
BF16 = jnp.bfloat16
HIGHEST = lax.Precision.HIGHEST
MESH = pl.DeviceIdType.MESH

D_MODEL = 1024
EPS = 1e-6
RET_HEADS, RET_DK, RET_DV, CHUNK = 4, 128, 256, 128
ROPE_BASE = 10000.0
SSM_HEADS, SSM_P, SSM_N, SSM_GROUPS = 16, 64, 128, 2
SSM_DINNER = SSM_HEADS * SSM_P
EVEN_IN, EVEN_IN_PAD = 5648, 5760
S5_GROUPS, S5_GROUP, S5_STATE = 32, 16, 64
S5_LANES = S5_GROUPS * S5_STATE
SCAN_SEG = 8
D_FF = 2816
ADAM_LR, ADAM_B1, ADAM_B2, ADAM_EPS, ADAM_WD, ADAM_STEP = 0.001, 0.9, 0.999, 1e-08, 0.01, 10

LANE = 128
VMEM_LIMIT = 56 * 1024 * 1024


def _params(sem=None, **kw):
    return pltpu.CompilerParams(dimension_semantics=sem, vmem_limit_bytes=VMEM_LIMIT, **kw)


def _tile(n, target, unit=LANE):
    if n <= target:
        return n
    t = (target // unit) * unit
    while t >= unit:
        if n % t == 0:
            return t
        t -= unit
    return n


def _silu(x):
    return x * jax.nn.sigmoid(x)


def _mm(a, b):
    return jnp.dot(a.astype(BF16), b.astype(BF16), preferred_element_type=F32)


def _mm_nt(a, b):
    return lax.dot_general(a.astype(BF16), b.astype(BF16), (((1,), (1,)), ((), ())), preferred_element_type=F32)


def _mm_tn(a, b):
    return lax.dot_general(a.astype(BF16), b.astype(BF16), (((0,), (0,)), ((), ())), preferred_element_type=F32)


def _dot_hi(a, b):
    return jnp.dot(a, b, precision=HIGHEST, preferred_element_type=F32)


def _dot_hi_tn(a, b):
    return lax.dot_general(a, b, (((0,), (0,)), ((), ())), precision=HIGHEST, preferred_element_type=F32)


def matmul(a, b, *, ta=False, tb=False, res=None, out_dtype=F32, name, tm=1024, tn=512, tk=1024):
    m, k = (a.shape[1], a.shape[0]) if ta else a.shape
    n = b.shape[0] if tb else b.shape[1]
    assert (b.shape[1] if tb else b.shape[0]) == k, (a.shape, b.shape, ta, tb)
    tm, tn, tk = _tile(m, tm), _tile(n, tn), _tile(k, tk)
    nk = k // tk
    a_spec = pl.BlockSpec((tk, tm), lambda i, j, kk: (kk, i)) if ta else pl.BlockSpec((tm, tk), lambda i, j, kk: (i, kk))
    b_spec = pl.BlockSpec((tn, tk), lambda i, j, kk: (j, kk)) if tb else pl.BlockSpec((tk, tn), lambda i, j, kk: (kk, j))
    o_spec = pl.BlockSpec((tm, tn), lambda i, j, kk: (i, j))
    dims = (((0 if ta else 1,), (1 if tb else 0,)), ((), ()))
    has_res = res is not None

    def body(a_ref, b_ref, *rest):
        if has_res:
            r_ref, o_ref, acc = rest
        else:
            o_ref, acc = rest
        kk = pl.program_id(2)
        p = lax.dot_general(a_ref[...].astype(BF16), b_ref[...].astype(BF16), dims, preferred_element_type=F32)

        @pl.when(kk == 0)
        def _():
            acc[...] = p

        @pl.when(kk > 0)
        def _():
            acc[...] += p

        @pl.when(kk == nk - 1)
        def _():
            out = acc[...]
            if has_res:
                out = out + r_ref[...].astype(F32)
            o_ref[...] = out.astype(o_ref.dtype)

    ins = [a, b] + ([res] if has_res else [])
    specs = [a_spec, b_spec] + ([o_spec] if has_res else [])
    return pl.pallas_call(
        body, name=name, grid=(m // tm, n // tn, nk), in_specs=specs, out_specs=o_spec,
        out_shape=jax.ShapeDtypeStruct((m, n), out_dtype), scratch_shapes=[pltpu.VMEM((tm, tn), F32)],
        compiler_params=_params(("parallel", "parallel", "arbitrary")),
    )(*ins)


class Cols(NamedTuple):
    arr: jax.Array
    w: int
    j: int


def _cols(a):
    return a if isinstance(a, Cols) else Cols(a, a.shape[1], 0)


def _row_spec(c, tl):
    return pl.BlockSpec((tl, c.w), lambda i, j=c.j: (i, j))


def _whole_spec(p):
    return pl.BlockSpec(p.shape, lambda i, nd=p.ndim: (0,) * nd)


def rowwise_fwd(fn, rows, aux, pars, consts, outs, *, name, tl):
    rows = [_cols(r) for r in rows + aux]
    whole = list(pars) + list(consts)
    n_rows = len(rows)
    n_whole = len(whole)
    length = rows[0].arr.shape[0]
    tl = min(tl, length)

    def body(*refs):
        vals = [r[...].astype(F32) for r in refs[:n_rows]] + [r[...] for r in refs[n_rows:n_rows + n_whole]]
        res = fn(*vals)
        for o_ref, v in zip(refs[n_rows + n_whole:], res, strict=True):
            o_ref[...] = v.astype(o_ref.dtype)

    return pl.pallas_call(
        body, name=name, grid=(length // tl,),
        in_specs=[_row_spec(r, tl) for r in rows] + [_whole_spec(p) for p in whole],
        out_specs=[pl.BlockSpec((tl, w), lambda i: (i, 0)) for w, _ in outs],
        out_shape=[jax.ShapeDtypeStruct((length, w), dt) for w, dt in outs],
        compiler_params=_params(("parallel",)),
    )(*[r.arr for r in rows], *whole)


def rowwise_bwd(fn, rows, aux, pars, consts, cots, drow_dtypes, *, name, tl, add=None, merge=False):
    rows = [_cols(r) for r in rows]
    aux = [_cols(r) for r in aux]
    cots = [_cols(r) for r in cots]
    n_r, n_a, n_p, n_c, n_t = len(rows), len(aux), len(pars), len(consts), len(cots)
    length = rows[0].arr.shape[0]
    tl = min(tl, length)
    has_add = add is not None
    widths = [r.w for r in rows]

    def body(*refs):
        pos = 0
        r_vals = [r[...].astype(F32) for r in refs[pos:pos + n_r]]; pos += n_r
        a_vals = [r[...].astype(F32) for r in refs[pos:pos + n_a]]; pos += n_a
        p_vals = [r[...].astype(F32) for r in refs[pos:pos + n_p]]; pos += n_p
        c_vals = [r[...] for r in refs[pos:pos + n_c]]; pos += n_c
        t_vals = [r[...].astype(F32) for r in refs[pos:pos + n_t]]; pos += n_t
        add_val = None
        if has_add:
            add_val = refs[pos][...].astype(F32); pos += 1
        n_dr = 1 if merge else n_r
        dr_refs = refs[pos:pos + n_dr]; pos += n_dr
        dp_refs = refs[pos:pos + n_p]

        def f(*rp):
            return fn(*rp[:n_r], *a_vals, *rp[n_r:], *c_vals)

        _, vjp = jax.vjp(f, *r_vals, *p_vals)
        grads = vjp(tuple(t_vals))
        drows = list(grads[:n_r])
        if has_add:
            drows[0] = drows[0] + add_val
        if merge:
            off = 0
            for w, d in zip(widths, drows):
                dr_refs[0][:, off:off + w] = d.astype(dr_refs[0].dtype)
                off += w
        else:
            for r, d in zip(dr_refs, drows):
                r[...] = d.astype(r.dtype)
        i = pl.program_id(0)
        for r, d in zip(dp_refs, grads[n_r:]):
            @pl.when(i == 0)
            def _(r=r, d=d):
                r[...] = d

            @pl.when(i > 0)
            def _(r=r, d=d):
                r[...] += d

    if merge:
        dr_specs = [pl.BlockSpec((tl, sum(widths)), lambda i: (i, 0))]
        dr_shapes = [jax.ShapeDtypeStruct((length, sum(widths)), drow_dtypes[0])]
    else:
        dr_specs = [pl.BlockSpec((tl, w), lambda i: (i, 0)) for w in widths]
        dr_shapes = [jax.ShapeDtypeStruct((length, w), dt) for w, dt in zip(widths, drow_dtypes)]
    ins = [r.arr for r in rows + aux] + list(pars) + list(consts) + [r.arr for r in cots] + ([add] if has_add else [])
    specs = ([_row_spec(r, tl) for r in rows + aux] + [_whole_spec(p) for p in list(pars) + list(consts)]
             + [_row_spec(r, tl) for r in cots] + ([pl.BlockSpec((tl, add.shape[1]), lambda i: (i, 0))] if has_add else []))
    return pl.pallas_call(
        body, name=name, grid=(length // tl,), in_specs=specs,
        out_specs=dr_specs + [_whole_spec(p) for p in pars],
        out_shape=dr_shapes + [jax.ShapeDtypeStruct(p.shape, F32) for p in pars],
        compiler_params=_params(("arbitrary",)),
    )(*ins)


def whole_fwd(fn, ins, out_shapes, *, name):
    n_in = len(ins)

    def body(*refs):
        res = fn(*[r[...] for r in refs[:n_in]])
        for o_ref, v in zip(refs[n_in:], res, strict=True):
            o_ref[...] = v

    return pl.pallas_call(body, name=name, out_shape=[jax.ShapeDtypeStruct(s, F32) for s in out_shapes],
                          compiler_params=_params())(*ins)


def whole_bwd(fn, ins, n_diff, cots, *, name):
    n_in, n_t = len(ins), len(cots)

    def body(*refs):
        vals = [r[...] for r in refs[:n_in]]
        t_vals = [r[...] for r in refs[n_in:n_in + n_t]]
        _, vjp = jax.vjp(lambda *d: fn(*d, *vals[n_diff:]), *vals[:n_diff])
        for o_ref, g in zip(refs[n_in + n_t:], vjp(tuple(t_vals)), strict=True):
            o_ref[...] = g

    return pl.pallas_call(body, name=name, out_shape=[jax.ShapeDtypeStruct(a.shape, F32) for a in ins[:n_diff]],
                          compiler_params=_params())(*ins, *cots)


CONV_ROWS = 256


def _conv_geometry(x, w, cw, off):
    width = w.shape[1]
    x = Cols(x, width, 0)
    length = x.arr.shape[0]
    taps = w.shape[0]
    pad = -(-(taps - 1) // 8) * 8
    assert off % cw == 0 and width % cw == 0, (off, width, cw)
    return x, length, taps, pad, off // cw


def _conv_taps(xp_ref, w_ref, base, taps, pad, init):
    acc = init
    for k in range(taps):
        acc = acc + w_ref[k:k + 1, :] * xp_ref[pl.ds(base + pad - (taps - 1) + k, init.shape[0]), :]
    return acc


def conv_fwd(x, w, b, *, act, name, off=0, cw=LANE, out_dtype=F32):
    x, length, taps, pad, jb = _conv_geometry(x, w, cw, off)
    rc = min(CONV_ROWS, length)

    def body(x_ref, w_ref, b_ref, o_ref, xp_ref):
        xp_ref[0:pad, :] = jnp.zeros((pad, cw), F32)
        xp_ref[pad:pad + length, :] = x_ref[...].astype(F32)

        def chunk(r, carry):
            base = pl.multiple_of(r * rc, rc)
            acc = _conv_taps(xp_ref, w_ref, base, taps, pad, jnp.broadcast_to(b_ref[...], (rc, cw)))
            if act:
                acc = _silu(acc)
            o_ref[pl.ds(base, rc), :] = acc.astype(o_ref.dtype)
            return carry

        lax.fori_loop(0, length // rc, chunk, 0)

    return pl.pallas_call(
        body, name=name, grid=(x.w // cw,),
        in_specs=[pl.BlockSpec((length, cw), lambda j: (0, jb + j)), pl.BlockSpec((taps, cw), lambda j: (0, j)),
                  pl.BlockSpec((1, cw), lambda j: (0, j))],
        out_specs=pl.BlockSpec((length, cw), lambda j: (0, j)),
        out_shape=jax.ShapeDtypeStruct((length, x.w), out_dtype),
        scratch_shapes=[pltpu.VMEM((pad + length, cw), F32)],
        compiler_params=_params(("parallel",)),
    )(x.arr, w, b)


def conv_bwd(x, w, b, dy, *, act, name, off=0, cw=LANE, dx_dtype=F32):
    x, length, taps, pad, jb = _conv_geometry(x, w, cw, off)
    rc = min(CONV_ROWS, length)

    def body(x_ref, w_ref, b_ref, dy_ref, dx_ref, dw_ref, db_ref, xp_ref, gp_ref):
        xp_ref[0:pad, :] = jnp.zeros((pad, cw), F32)
        xp_ref[pad:pad + length, :] = x_ref[...].astype(F32)
        gp_ref[length:length + pad, :] = jnp.zeros((pad, cw), F32)
        if act:
            def pre_chunk(r, carry):
                base = pl.multiple_of(r * rc, rc)
                pre = _conv_taps(xp_ref, w_ref, base, taps, pad, jnp.broadcast_to(b_ref[...], (rc, cw)))
                sig = jax.nn.sigmoid(pre)
                gp_ref[pl.ds(base, rc), :] = dy_ref[pl.ds(base, rc), :].astype(F32) * (sig * (1.0 + pre * (1.0 - sig)))
                return carry

            lax.fori_loop(0, length // rc, pre_chunk, 0)
        else:
            gp_ref[0:length, :] = dy_ref[...].astype(F32)
        dw_ref[...] = jnp.zeros((taps, cw), F32)
        db_ref[...] = jnp.zeros((1, cw), F32)

        def chunk(r, carry):
            base = pl.multiple_of(r * rc, rc)
            acc = jnp.zeros((rc, cw), F32)
            g = gp_ref[pl.ds(base, rc), :]
            for k in range(taps):
                acc = acc + w_ref[k:k + 1, :] * gp_ref[pl.ds(base + (taps - 1) - k, rc), :]
                xs = xp_ref[pl.ds(base + pad - (taps - 1) + k, rc), :]
                dw_ref[k:k + 1, :] += jnp.sum(g * xs, axis=0, keepdims=True)
            db_ref[...] += jnp.sum(g, axis=0, keepdims=True)
            dx_ref[pl.ds(base, rc), :] = acc.astype(dx_ref.dtype)
            return carry

        lax.fori_loop(0, length // rc, chunk, 0)

    dy = _cols(dy)
    assert dy.j == 0 and dy.w == x.w
    return pl.pallas_call(
        body, name=name, grid=(x.w // cw,),
        in_specs=[pl.BlockSpec((length, cw), lambda j: (0, jb + j)), pl.BlockSpec((taps, cw), lambda j: (0, j)),
                  pl.BlockSpec((1, cw), lambda j: (0, j)), pl.BlockSpec((length, cw), lambda j: (0, j))],
        out_specs=[pl.BlockSpec((length, cw), lambda j: (0, j)), pl.BlockSpec((taps, cw), lambda j: (0, j)),
                   pl.BlockSpec((1, cw), lambda j: (0, j))],
        out_shape=[jax.ShapeDtypeStruct((length, x.w), dx_dtype), jax.ShapeDtypeStruct((taps, x.w), F32),
                   jax.ShapeDtypeStruct((1, x.w), F32)],
        scratch_shapes=[pltpu.VMEM((pad + length, cw), F32), pltpu.VMEM((length + pad, cw), F32)],
        compiler_params=_params(("parallel",)),
    )(x.arr, w, b, dy.arr)


def _retention_consts():
    h = np.arange(RET_HEADS, dtype=np.float32)
    log_g = np.log1p(-(2.0 ** (-5.0 - h))).astype(np.float32)
    idx = np.arange(CHUNK, dtype=np.float32)
    diff = idx[:, None] - idx[None, :]
    intra = np.where(diff[None] >= 0, np.exp(np.maximum(diff, 0.0)[None] * log_g[:, None, None]), 0.0)
    zeta = np.exp((CHUNK - 1 - idx)[None, :] * log_g[:, None])
    xi = np.exp((idx + 1)[None, :] * log_g[:, None])
    decay = np.exp(CHUNK * log_g)
    zeta = np.broadcast_to(zeta[:, :, None], (RET_HEADS, CHUNK, RET_DK))
    xi = np.broadcast_to(xi[:, :, None], (RET_HEADS, CHUNK, RET_DV))
    return (jnp.asarray(intra, F32), jnp.asarray(zeta, F32), jnp.asarray(xi, F32), [float(d) for d in decay])


def _rotary_tables(length):
    inv = ROPE_BASE ** (-jnp.arange(0, RET_DK, 2, dtype=F32) / RET_DK)
    ang = jnp.arange(length).astype(F32)[:, None] * inv[None, :]
    cos, sin = jnp.cos(ang), jnp.sin(ang)
    return jnp.concatenate([cos, cos], axis=1), jnp.concatenate([-sin, sin], axis=1)


def _rot(x, cos2, sin2):
    return x * cos2 + pltpu.roll(x, RET_DK // 2, 1) * sin2


def _rot_t(y, cos2, sin2):
    return y * cos2 + pltpu.roll(y * sin2, RET_DK // 2, 1)


def _head_decay(h, decays):
    d = jnp.float32(decays[-1])
    for i in range(len(decays) - 2, -1, -1):
        d = jnp.where(h == i, jnp.float32(decays[i]), d)
    return d


def _ret_chunk(q, k, v, g, state, intra, zeta, xi, decay):
    s = _mm_nt(q, k) * intra
    kv = _mm_tn(k * zeta, v)
    o = _mm(s, v) + _mm(q, state) * xi
    oc = o - jnp.mean(o, axis=-1, keepdims=True)
    r = oc * lax.rsqrt(jnp.mean(oc * oc, axis=-1, keepdims=True) + EPS)
    return _silu(g) * r, state * decay + kv


def _ret_specs(rev, nc):
    def cidx(c):
        return nc - 1 - c if rev else c
    return [
        pl.BlockSpec((CHUNK, RET_DK), lambda h, c: (cidx(c), h)),
        pl.BlockSpec((CHUNK, RET_DK), lambda h, c: (cidx(c), RET_HEADS + h)),
        pl.BlockSpec((CHUNK, RET_DV), lambda h, c: (cidx(c), 4 + h)),
        pl.BlockSpec((CHUNK, RET_DV), lambda h, c: (cidx(c), 8 + h)),
        pl.BlockSpec((CHUNK, RET_DK), lambda h, c: (cidx(c), 0)),
        pl.BlockSpec((CHUNK, RET_DK), lambda h, c: (cidx(c), 0)),
        pl.BlockSpec((1, CHUNK, CHUNK), lambda h, c: (h, 0, 0)),
        pl.BlockSpec((1, CHUNK, RET_DK), lambda h, c: (h, 0, 0)),
        pl.BlockSpec((1, CHUNK, RET_DV), lambda h, c: (h, 0, 0)),
    ], cidx


def retention_fwd(proj, cos2, sin2):
    length = proj.shape[0]
    nc = length // CHUNK
    intra, zeta, xi, decays = _retention_consts()
    specs, _ = _ret_specs(False, nc)
    scale = RET_DK ** -0.5

    def body(q_ref, k_ref, v_ref, g_ref, cos_ref, sin_ref, intra_ref, zeta_ref, xi_ref, y_ref, st_ref, state):
        h, c = pl.program_id(0), pl.program_id(1)

        @pl.when(c == 0)
        def _():
            state[...] = jnp.zeros_like(state)

        q = _rot(q_ref[...], cos_ref[...], sin_ref[...])
        k = _rot(k_ref[...], cos_ref[...], sin_ref[...]) * scale
        st_ref[0, 0] = state[...]
        y, new_state = _ret_chunk(q, k, v_ref[...], g_ref[...], state[...], intra_ref[0], zeta_ref[0], xi_ref[0],
                                  _head_decay(h, decays))
        y_ref[...] = y.astype(y_ref.dtype)
        state[...] = new_state

    return pl.pallas_call(
        body, name="retention_fwd", grid=(RET_HEADS, nc), in_specs=specs,
        out_specs=[pl.BlockSpec((CHUNK, RET_DV), lambda h, c: (c, h)),
                   pl.BlockSpec((1, 1, RET_DK, RET_DV), lambda h, c: (h, c, 0, 0))],
        out_shape=[jax.ShapeDtypeStruct((length, RET_HEADS * RET_DV), BF16),
                   jax.ShapeDtypeStruct((RET_HEADS, nc, RET_DK, RET_DV), F32)],
        scratch_shapes=[pltpu.VMEM((RET_DK, RET_DV), F32)],
        compiler_params=_params(("parallel", "arbitrary")),
    )(proj, proj, proj, proj, cos2, sin2, intra, zeta, xi)


def retention_bwd(proj, cos2, sin2, states, dmix):
    length = proj.shape[0]
    nc = length // CHUNK
    intra, zeta, xi, decays = _retention_consts()
    specs, cidx = _ret_specs(True, nc)
    scale = RET_DK ** -0.5

    def body(q_ref, k_ref, v_ref, g_ref, cos_ref, sin_ref, intra_ref, zeta_ref, xi_ref, st_ref, dy_ref,
             dq_ref, dk_ref, dv_ref, dg_ref, dstate):
        h, c = pl.program_id(0), pl.program_id(1)

        @pl.when(c == 0)
        def _():
            dstate[...] = jnp.zeros_like(dstate)

        cos2v, sin2v = cos_ref[...], sin_ref[...]
        q = _rot(q_ref[...], cos2v, sin2v)
        k = _rot(k_ref[...], cos2v, sin2v) * scale
        decay = _head_decay(h, decays)
        intra_v, zeta_v, xi_v = intra_ref[0], zeta_ref[0], xi_ref[0]
        _, vjp = jax.vjp(lambda q, k, v, g, s: _ret_chunk(q, k, v, g, s, intra_v, zeta_v, xi_v, decay),
                         q, k, v_ref[...], g_ref[...], st_ref[0, 0])
        dq, dk, dv, dg, ds = vjp((dy_ref[...].astype(F32), dstate[...]))
        dq_ref[...] = _rot_t(dq, cos2v, sin2v).astype(dq_ref.dtype)
        dk_ref[...] = _rot_t(dk * scale, cos2v, sin2v).astype(dk_ref.dtype)
        dv_ref[...] = dv.astype(dv_ref.dtype)
        dg_ref[...] = dg.astype(dg_ref.dtype)
        dstate[...] = ds

    specs = specs + [pl.BlockSpec((1, 1, RET_DK, RET_DV), lambda h, c: (h, cidx(c), 0, 0)),
                     pl.BlockSpec((CHUNK, RET_DV), lambda h, c: (cidx(c), h))]
    return pl.pallas_call(
        body, name="retention_bwd", grid=(RET_HEADS, nc), in_specs=specs,
        out_specs=[pl.BlockSpec((CHUNK, RET_DK), lambda h, c: (cidx(c), h)),
                   pl.BlockSpec((CHUNK, RET_DK), lambda h, c: (cidx(c), h)),
                   pl.BlockSpec((CHUNK, RET_DV), lambda h, c: (cidx(c), h)),
                   pl.BlockSpec((CHUNK, RET_DV), lambda h, c: (cidx(c), h))],
        out_shape=[jax.ShapeDtypeStruct((length, RET_HEADS * RET_DK), BF16),
                   jax.ShapeDtypeStruct((length, RET_HEADS * RET_DK), BF16),
                   jax.ShapeDtypeStruct((length, RET_HEADS * RET_DV), BF16),
                   jax.ShapeDtypeStruct((length, RET_HEADS * RET_DV), BF16)],
        scratch_shapes=[pltpu.VMEM((RET_DK, RET_DV), F32)],
        compiler_params=_params(("parallel", "arbitrary")),
    )(proj, proj, proj, proj, cos2, sin2, intra, zeta, xi, states, dmix)


def _ssd_consts():
    tri = np.tril(np.ones((CHUNK, CHUNK), np.float32))
    expand = np.zeros((LANE, SSM_DINNER), np.float32)
    for h in range(SSM_HEADS):
        expand[h, h * SSM_P:(h + 1) * SSM_P] = 1.0
    return jnp.asarray(tri), jnp.asarray(tri.T.copy()), jnp.asarray(expand)


def _ssd_chunk(xs, bm, cm, dtr, z, state, dt_bias, a_log, d_skip, norm_w, tri, tri_t, expand):
    gw = SSM_DINNER // SSM_GROUPS
    dt = jax.nn.softplus(dtr + dt_bias)
    da = dt * (-jnp.exp(a_log))
    acs = _dot_hi(tri, da)
    acs_t = _dot_hi_tn(da, tri_t)
    dt_x = _dot_hi(dt, expand)
    da_x = _dot_hi(da, expand)
    acs_x = _dot_hi(tri, da_x)
    tot_x = jnp.sum(da_x, axis=0, keepdims=True)
    x_dt = xs * dt_x
    x_dec = x_dt * jnp.exp(tot_x - acs_x)
    e_acs = jnp.exp(acs_x)
    e_tot = jnp.exp(tot_x)
    lane = lax.broadcasted_iota(jnp.int32, (CHUNK, LANE), 1)
    sub = lax.broadcasted_iota(jnp.int32, (CHUNK, LANE), 0)
    causal = sub >= lane
    ys, new_states = [], []
    for g in range(SSM_GROUPS):
        bg = bm[:, g * SSM_N:(g + 1) * SSM_N]
        cg = cm[:, g * SSM_N:(g + 1) * SSM_N]
        sg = state[:, g * gw:(g + 1) * gw]
        cb = _mm_nt(cg, bg)
        y_off = _mm(cg, sg) * e_acs[:, g * gw:(g + 1) * gw]
        new_states.append(sg * e_tot[:, g * gw:(g + 1) * gw] + _mm_tn(bg, x_dec[:, g * gw:(g + 1) * gw]))
        pairs = []
        for p in range(gw // LANE):
            hp = g * (gw // LANE) + p
            xp = x_dt[:, hp * LANE:(hp + 1) * LANE]
            halves = []
            for head in (2 * hp, 2 * hp + 1):
                col = jnp.sum(jnp.where(lane == head, acs, 0.0), axis=1, keepdims=True)
                row = jnp.sum(jnp.where(sub == head, acs_t, 0.0), axis=0, keepdims=True)
                decay = jnp.exp(jnp.where(causal, col - row, -1e30))
                halves.append(_mm(cb * decay, xp))
            pairs.append(jnp.where(lane < SSM_P, halves[0], halves[1]))
        ys.append(jnp.concatenate(pairs, axis=1) + y_off)
    d_x = jnp.mean(_dot_hi(jnp.broadcast_to(d_skip, (8, LANE)), expand), axis=0, keepdims=True)
    y = (jnp.concatenate(ys, axis=1) + d_x * xs) * _silu(z)
    normed = []
    for g in range(SSM_GROUPS):
        yg = y[:, g * gw:(g + 1) * gw]
        normed.append(yg * lax.rsqrt(jnp.mean(yg * yg, axis=-1, keepdims=True) + EPS))
    return jnp.concatenate(normed, axis=1) * norm_w, jnp.concatenate(new_states, axis=1)


XBC = SSM_DINNER + 2 * SSM_GROUPS * SSM_N


def _ssd_specs(rev, nc):
    def cidx(c):
        return nc - 1 - c if rev else c
    row = lambda w, j: pl.BlockSpec((CHUNK, w), lambda c: (cidx(c), j))
    whole = lambda shape: pl.BlockSpec(shape, lambda c: (0,) * len(shape))
    return [row(XBC, 0), row(LANE, 5632 // LANE), row(SSM_DINNER, 3),
            whole((1, LANE)), whole((1, LANE)), whole((1, LANE)), whole((1, SSM_DINNER)),
            whole((CHUNK, CHUNK)), whole((CHUNK, CHUNK)), whole((LANE, SSM_DINNER))], cidx


def ssd_fwd(xbc, proj, dt_bias, a_log, d_skip, norm_w):
    length = proj.shape[0]
    nc = length // CHUNK
    tri, tri_t, expand = _ssd_consts()
    specs, _ = _ssd_specs(False, nc)

    def body(xbc_ref, dt_ref, z_ref, dtb_ref, alog_ref, d_ref, nw_ref, tri_ref, trit_ref, e_ref, y_ref, st_ref, state):
        @pl.when(pl.program_id(0) == 0)
        def _():
            state[...] = jnp.zeros_like(state)

        st_ref[0] = state[...]
        y, new_state = _ssd_chunk(
            xbc_ref[:, 0:SSM_DINNER], xbc_ref[:, SSM_DINNER:SSM_DINNER + 256], xbc_ref[:, SSM_DINNER + 256:XBC],
            dt_ref[...], z_ref[...], state[...], dtb_ref[...], alog_ref[...], d_ref[...], nw_ref[...],
            tri_ref[...], trit_ref[...], e_ref[...])
        y_ref[...] = y.astype(y_ref.dtype)
        state[...] = new_state

    return pl.pallas_call(
        body, name="ssd_fwd", grid=(nc,), in_specs=specs,
        out_specs=[pl.BlockSpec((CHUNK, SSM_DINNER), lambda c: (c, 0)),
                   pl.BlockSpec((1, SSM_N, SSM_DINNER), lambda c: (c, 0, 0))],
        out_shape=[jax.ShapeDtypeStruct((length, SSM_DINNER), BF16),
                   jax.ShapeDtypeStruct((nc, SSM_N, SSM_DINNER), F32)],
        scratch_shapes=[pltpu.VMEM((SSM_N, SSM_DINNER), F32)],
        compiler_params=_params(("arbitrary",)),
    )(xbc, proj, proj, dt_bias, a_log, d_skip, norm_w, tri, tri_t, expand)


def ssd_bwd(xbc, proj, dt_bias, a_log, d_skip, norm_w, states, dmix):
    length = proj.shape[0]
    nc = length // CHUNK
    tri, tri_t, expand = _ssd_consts()
    specs, cidx = _ssd_specs(True, nc)

    def body(xbc_ref, dt_ref, z_ref, dtb_ref, alog_ref, d_ref, nw_ref, tri_ref, trit_ref, e_ref, st_ref, dy_ref,
             dxbc_ref, ddt_ref, dz_ref, ddtb_ref, dalog_ref, dd_ref, dnw_ref, dstate):
        c = pl.program_id(0)

        @pl.when(c == 0)
        def _():
            dstate[...] = jnp.zeros_like(dstate)

        tri_v, trit_v, e_v = tri_ref[...], trit_ref[...], e_ref[...]
        _, vjp = jax.vjp(
            lambda *a: _ssd_chunk(*a, tri_v, trit_v, e_v),
            xbc_ref[:, 0:SSM_DINNER], xbc_ref[:, SSM_DINNER:SSM_DINNER + 256], xbc_ref[:, SSM_DINNER + 256:XBC],
            dt_ref[...], z_ref[...], st_ref[0], dtb_ref[...], alog_ref[...], d_ref[...], nw_ref[...])
        dxs, dbm, dcm, ddt, dz, ds, ddtb, dalog, dd, dnw = vjp((dy_ref[...].astype(F32), dstate[...]))
        dxbc_ref[:, 0:SSM_DINNER] = dxs
        dxbc_ref[:, SSM_DINNER:SSM_DINNER + 256] = dbm
        dxbc_ref[:, SSM_DINNER + 256:XBC] = dcm
        ddt_ref[...] = ddt.astype(ddt_ref.dtype)
        dz_ref[...] = dz.astype(dz_ref.dtype)
        dstate[...] = ds
        for r, d in ((ddtb_ref, ddtb), (dalog_ref, dalog), (dd_ref, dd), (dnw_ref, dnw)):
            @pl.when(c == 0)
            def _(r=r, d=d):
                r[...] = d

            @pl.when(c > 0)
            def _(r=r, d=d):
                r[...] += d

    whole = lambda shape: pl.BlockSpec(shape, lambda c: (0,) * len(shape))
    specs = specs + [pl.BlockSpec((1, SSM_N, SSM_DINNER), lambda c: (cidx(c), 0, 0)),
                     pl.BlockSpec((CHUNK, SSM_DINNER), lambda c: (cidx(c), 1))]
    return pl.pallas_call(
        body, name="ssd_bwd", grid=(nc,), in_specs=specs,
        out_specs=[pl.BlockSpec((CHUNK, XBC), lambda c: (cidx(c), 0)), pl.BlockSpec((CHUNK, LANE), lambda c: (cidx(c), 0)),
                   pl.BlockSpec((CHUNK, SSM_DINNER), lambda c: (cidx(c), 0)),
                   whole((1, LANE)), whole((1, LANE)), whole((1, LANE)), whole((1, SSM_DINNER))],
        out_shape=[jax.ShapeDtypeStruct((length, XBC), F32), jax.ShapeDtypeStruct((length, LANE), BF16),
                   jax.ShapeDtypeStruct((length, SSM_DINNER), BF16),
                   jax.ShapeDtypeStruct((1, LANE), F32), jax.ShapeDtypeStruct((1, LANE), F32),
                   jax.ShapeDtypeStruct((1, LANE), F32), jax.ShapeDtypeStruct((1, SSM_DINNER), F32)],
        scratch_shapes=[pltpu.VMEM((SSM_N, SSM_DINNER), F32)],
        compiler_params=_params(("arbitrary",)),
    )(xbc, proj, proj, dt_bias, a_log, d_skip, norm_w, tri, tri_t, expand, states, dmix)


def _cmul(ar, ai, br, bi):
    return ar * br - ai * bi, ar * bi + ai * br


def s5_scan(b_re, b_im, a_re, a_im, *, reverse=False, states=None, name, lw=256):
    length, lanes = b_re.shape
    nk = length // SCAN_SEG
    with_da = states is not None
    assert reverse or not with_da

    def shift(v):
        sub = lax.broadcasted_iota(jnp.int32, v.shape, 0)
        if reverse:
            return jnp.where(sub == SCAN_SEG - 1, 0.0, pltpu.roll(v, SCAN_SEG - 1, 0))
        return jnp.where(sub == 0, 0.0, pltpu.roll(v, 1, 0))

    def body(*refs):
        if with_da:
            bre_ref, bim_ref, are_ref, aim_ref, sre_ref, sim_ref, xre_ref, xim_ref, dare_ref, daim_ref = refs
        else:
            bre_ref, bim_ref, are_ref, aim_ref, xre_ref, xim_ref = refs
        ar = jnp.broadcast_to(are_ref[...], (SCAN_SEG, lw))
        ai = jnp.broadcast_to(aim_ref[...], (SCAN_SEG, lw))

        def tile(i):
            k = (nk - 1 - i) if reverse else i
            return pl.ds(pl.multiple_of(k * SCAN_SEG, SCAN_SEG), SCAN_SEG)

        def local(i, carry):
            xr, xi, pr, pi = carry
            rows = tile(i)
            mr, mi = _cmul(ar, ai, xr, xi)
            xr, xi = mr + bre_ref[rows, :], mi + bim_ref[rows, :]
            xre_ref[rows, :] = xr
            xim_ref[rows, :] = xi
            pr, pi = _cmul(ar, ai, pr, pi)
            return xr, xi, pr, pi

        zero = jnp.zeros((SCAN_SEG, lw), F32)
        one = jnp.ones((SCAN_SEG, lw), F32)
        er, ei, pr, pi = lax.fori_loop(0, nk, local, (zero, zero, one, zero))
        cr, ci = zero, zero
        for _ in range(SCAN_SEG - 1):
            mr, mi = _cmul(pr, pi, cr, ci)
            cr, ci = shift(er + mr), shift(ei + mi)

        def fix(i, carry):
            pr, pi, dr, di = carry
            rows = tile(i)
            pr, pi = _cmul(ar, ai, pr, pi)
            mr, mi = _cmul(pr, pi, cr, ci)
            xr, xi = xre_ref[rows, :] + mr, xim_ref[rows, :] + mi
            xre_ref[rows, :] = xr
            xim_ref[rows, :] = xi
            if with_da:
                k = nk - 1 - i
                prev = pl.ds(pl.multiple_of(jnp.maximum(k - 1, 0) * SCAN_SEG, SCAN_SEG), SCAN_SEG)
                last = pl.ds((nk - 1) * SCAN_SEG, SCAN_SEG)
                sub = lax.broadcasted_iota(jnp.int32, (SCAN_SEG, lw), 0)
                wr = jnp.where(sub == 0, 0.0, pltpu.roll(sre_ref[last, :], 1, 0))
                wi = jnp.where(sub == 0, 0.0, pltpu.roll(sim_ref[last, :], 1, 0))
                sr = jnp.where(k == 0, wr, sre_ref[prev, :])
                si = jnp.where(k == 0, wi, sim_ref[prev, :])
                dr, di = dr + xr * sr + xi * si, di + xi * sr - xr * si
            return pr, pi, dr, di

        _, _, dr, di = lax.fori_loop(0, nk, fix, (one, zero, zero, zero))
        if with_da:
            dare_ref[...] = jnp.sum(dr, axis=0, keepdims=True)
            daim_ref[...] = jnp.sum(di, axis=0, keepdims=True)

    col = pl.BlockSpec((length, lw), lambda j: (0, j))
    vec = pl.BlockSpec((1, lw), lambda j: (0, j))
    ins = [b_re, b_im, a_re, a_im] + (list(states) if with_da else [])
    in_specs = [col, col, vec, vec] + ([col, col] if with_da else [])
    out_specs = [col, col] + ([vec, vec] if with_da else [])
    out_shape = [jax.ShapeDtypeStruct((length, lanes), F32)] * 2 + ([jax.ShapeDtypeStruct((1, lanes), F32)] * 2 if with_da else [])
    return pl.pallas_call(
        body, name=name, grid=(lanes // lw,), in_specs=in_specs, out_specs=out_specs, out_shape=out_shape,
        compiler_params=_params(("parallel",)),
    )(*ins)


def _seg_interleave(v):
    length = v.shape[0]
    return v.reshape(SCAN_SEG, length // SCAN_SEG, -1).transpose(1, 0, 2).reshape(length, -1)


def _seg_deinterleave(v):
    length = v.shape[0]
    return v.reshape(length // SCAN_SEG, SCAN_SEG, -1).transpose(1, 0, 2).reshape(length, -1)


def _block_diag(m):
    eye = jnp.eye(S5_GROUPS, dtype=m.dtype)
    return (m.reshape(S5_GROUPS, S5_GROUP, 1, S5_STATE) * eye[:, None, :, None]).reshape(S5_GROUPS * S5_GROUP, S5_LANES)


def _block_diag_take(full):
    idx = jnp.arange(S5_GROUPS)
    blocks = full.reshape(S5_GROUPS, S5_GROUP, S5_GROUPS, S5_STATE)[idx, :, idx, :]
    return blocks.reshape(S5_GROUPS * S5_GROUP, S5_STATE)


def _s5_prep(a_re, a_im, log_step, b_re, b_im, rep):
    step = jnp.exp(log_step)
    mag = jnp.exp(a_re * step)
    ab_re = mag * jnp.cos(a_im * step)
    ab_im = mag * jnp.sin(a_im * step)
    den = a_re * a_re + a_im * a_im
    f_re = ((ab_re - 1.0) * a_re + ab_im * a_im) / den
    f_im = (ab_im * a_re - (ab_re - 1.0) * a_im) / den
    fr, fi = _dot_hi(rep, f_re), _dot_hi(rep, f_im)
    return ab_re, ab_im, fr * b_re - fi * b_im, fr * b_im + fi * b_re


def _rms(x, g):
    return (x * lax.rsqrt(jnp.mean(x * x, axis=-1, keepdims=True) + EPS) * g,)


def _ffn_act(gate, up):
    return (_silu(gate) * up,)


def _glu(a, g):
    return (a * jax.nn.sigmoid(g),)


def _ln_silu(x, g, b):
    xc = x - jnp.mean(x, axis=-1, keepdims=True)
    var = jnp.mean(xc * xc, axis=-1, keepdims=True)
    return (_silu(xc * lax.rsqrt(var + EPS) * g + b),)


def _s5_post(y, u, d_skip, glu_w):
    s = jax.nn.gelu(y + d_skip * u)
    return (s * jax.nn.sigmoid(_mm(s, glu_w)),)


def loss_head(x, tgt, g, *, tl=512):
    length, d = x.shape
    tl = min(tl, length)

    def body(x_ref, t_ref, g_ref, loss_ref, dx_ref, dg_ref):
        i = pl.program_id(0)
        y, vjp = jax.vjp(lambda x, g: _rms(x, g)[0], x_ref[...], g_ref[...])
        err = y - t_ref[...]
        dx, dg = vjp(err * (1.0 / d))
        dx_ref[...] = dx
        part = jnp.broadcast_to(0.5 * jnp.sum(jnp.mean(err * err, axis=-1, keepdims=True), axis=0, keepdims=True), (1, LANE))

        @pl.when(i == 0)
        def _():
            loss_ref[...] = part
            dg_ref[...] = dg

        @pl.when(i > 0)
        def _():
            loss_ref[...] += part
            dg_ref[...] += dg

    row = pl.BlockSpec((tl, d), lambda i: (i, 0))
    return pl.pallas_call(
        body, name="loss_head", grid=(length // tl,),
        in_specs=[row, row, pl.BlockSpec((1, d), lambda i: (0, 0))],
        out_specs=[pl.BlockSpec((1, LANE), lambda i: (0, 0)), row, pl.BlockSpec((1, d), lambda i: (0, 0))],
        out_shape=[jax.ShapeDtypeStruct((1, LANE), F32), jax.ShapeDtypeStruct((length, d), F32),
                   jax.ShapeDtypeStruct((1, d), F32)],
        compiler_params=_params(("arbitrary",)),
    )(x, tgt, g)


def _pad_heads(v):
    return jnp.pad(v, ((0, 0), (0, LANE - v.shape[1])))


def local_step(x, tgt, w):
    length = x.shape[0]
    cos2, sin2 = _rotary_tables(length)
    grads = {}

    def rms_fwd(xin, g, name):
        return rowwise_fwd(_rms, [xin], [], [g], [], [(D_MODEL, BF16)], name=name, tl=512)[0]

    def rms_bwd(xin, g, dh, dxo, name):
        return rowwise_bwd(_rms, [xin], [], [g], [], [dh], [F32], name=name, tl=512, add=dxo)

    def ffn_fwd(i, xin):
        hf = rms_fwd(xin, w["ffn_norm"][i:i + 1], f"ffn{i}_norm")
        a = matmul(hf, w["ffn_w_up"][i], name=f"ffn{i}_up")
        ac = conv_fwd(a, w["ffn_dw_w"][i], w["ffn_dw_b"][i:i + 1], act=False, name=f"ffn{i}_conv")
        act = rowwise_fwd(_ffn_act, [Cols(ac, D_FF, 0), Cols(ac, D_FF, 1)], [], [], [], [(D_FF, BF16)],
                          name=f"ffn{i}_act", tl=256)[0]
        return matmul(act, w["ffn_w_down"][i], res=xin, name=f"ffn{i}_down"), (hf, a, ac, act)

    def ffn_bwd(i, xin, saved, dxo):
        hf, a, ac, act = saved
        dact = matmul(dxo, w["ffn_w_down"][i], tb=True, name=f"ffn{i}_down_dx")
        dw_down = matmul(act, dxo, ta=True, name=f"ffn{i}_down_dw")
        dac = rowwise_bwd(_ffn_act, [Cols(ac, D_FF, 0), Cols(ac, D_FF, 1)], [], [], [], [dact], [F32],
                          name=f"ffn{i}_act_bwd", tl=256, merge=True)[0]
        da, ddw_w, ddw_b = conv_bwd(a, w["ffn_dw_w"][i], w["ffn_dw_b"][i:i + 1], dac, act=False,
                                    name=f"ffn{i}_conv_bwd", dx_dtype=BF16)
        dw_up = matmul(hf, da, ta=True, name=f"ffn{i}_up_dw")
        dhf = matmul(da, w["ffn_w_up"][i], tb=True, name=f"ffn{i}_up_dx")
        dxin, dnorm = rms_bwd(xin, w["ffn_norm"][i:i + 1], dhf, dxo, f"ffn{i}_norm_bwd")
        return dxin, dict(ffn_norm=dnorm, ffn_w_up=dw_up, ffn_dw_w=ddw_w, ffn_dw_b=ddw_b, ffn_w_down=dw_down)

    w_in_e = jnp.pad(w["e_w_in"][0], ((0, 0), (0, EVEN_IN_PAD - EVEN_IN)))
    w_out_e = w["e_w_out"][0]
    conv_w_e, conv_b_e = w["e_conv_w"][0], w["e_conv_b"]
    dt_bias, a_log, d_skip = _pad_heads(w["e_dt_bias"]), _pad_heads(w["e_a_log"]), _pad_heads(w["e_d"])
    xbc_off = 4 * D_MODEL

    hn0 = rms_fwd(x, w["mix_norm"][0:1], "mix0_norm")
    proj0 = matmul(hn0, w_in_e, name="even_in")
    y_ret, ret_states = retention_fwd(proj0, cos2, sin2)
    xbc = conv_fwd(proj0, conv_w_e, conv_b_e, act=True, off=xbc_off, name="ssd_conv")
    y_ssm, ssd_states = ssd_fwd(xbc, proj0, dt_bias, a_log, d_skip, w["e_ssm_norm"])
    mix0 = jnp.concatenate([y_ret, y_ssm], axis=1)
    x1 = matmul(mix0, w_out_e, res=x, name="even_out")
    x2, ffn0_saved = ffn_fwd(0, x1)

    w_in_o, w_out_o, glu_w = w["o_w_in"][0], w["o_w_out"][0], w["o_glu_w"][0]
    dw_w_o, dw_b_o, ln_g, ln_b, d_o = w["o_dw_w"][0], w["o_dw_b"], w["o_ln_g"], w["o_ln_b"], w["o_d"]
    rep = jnp.asarray(np.repeat(np.eye(S5_GROUPS, dtype=np.float32), S5_GROUP, axis=0))
    rows_gc = (S5_GROUPS * S5_GROUP, S5_STATE)
    prep_in = [w["o_a_re"][0], w["o_a_im"][0], w["o_log_step"].reshape(S5_GROUPS, 1),
               w["o_b_re"][0].transpose(0, 2, 1).reshape(rows_gc), w["o_b_im"][0].transpose(0, 2, 1).reshape(rows_gc), rep]
    ab_re, ab_im, bb_re, bb_im = whole_fwd(
        _s5_prep, prep_in, [(S5_GROUPS, S5_STATE)] * 2 + [rows_gc] * 2, name="s5_prep")
    a_re_row, a_im_row = ab_re.reshape(1, S5_LANES), ab_im.reshape(1, S5_LANES)
    b_re_bd, b_im_bd = _block_diag(bb_re).astype(BF16), _block_diag(bb_im).astype(BF16)
    c_re_bd = _block_diag(w["o_c_re"][0].reshape(rows_gc)).astype(BF16)
    c_im_neg_bd = _block_diag(-w["o_c_im"][0].reshape(rows_gc)).astype(BF16)

    hn1 = rms_fwd(x2, w["mix_norm"][1:2], "mix1_norm")
    proj1 = matmul(hn1, w_in_o, name="odd_in")
    half = D_MODEL // 2
    c_glu = rowwise_fwd(_glu, [Cols(proj1, half, 0), Cols(proj1, half, 1)], [], [], [], [(half, F32)],
                        name="conf_glu", tl=512)[0]
    c_conv = conv_fwd(c_glu, dw_w_o, dw_b_o, act=False, name="conf_conv")
    c_out = rowwise_fwd(_ln_silu, [c_conv], [], [ln_g, ln_b], [], [(half, BF16)], name="conf_ln", tl=512)[0]
    u_seg = _seg_interleave(proj1[:, 2 * half:])
    bu_re = matmul(u_seg, b_re_bd, name="s5_bu_re")
    bu_im = matmul(u_seg, b_im_bd, name="s5_bu_im")
    xs_re, xs_im = s5_scan(bu_re, bu_im, a_re_row, a_im_row, name="s5_scan")
    y_im = matmul(xs_im, c_im_neg_bd, tb=True, name="s5_y_im")
    y_s5 = _seg_deinterleave(matmul(xs_re, c_re_bd, tb=True, res=y_im, name="s5_y_re"))
    s_out = rowwise_fwd(_s5_post, [y_s5, Cols(proj1, half, 2)], [], [d_o, glu_w], [], [(half, BF16)],
                        name="s5_post", tl=512)[0]
    mix1 = jnp.concatenate([c_out, s_out], axis=1)
    x3 = matmul(mix1, w_out_o, res=x2, name="odd_out")
    x4, ffn1_saved = ffn_fwd(1, x3)

    loss, dx4, dfinal = loss_head(x4, tgt, w["final_norm"].reshape(1, D_MODEL))
    grads["final_norm"] = dfinal.reshape(D_MODEL)

    dx3, g_ffn1 = ffn_bwd(1, x3, ffn1_saved, dx4)
    dmix1 = matmul(dx3, w_out_o, tb=True, name="odd_out_dx")
    grads["o_w_out"] = matmul(mix1, dx3, ta=True, name="odd_out_dw")[None]
    dc_conv, dln_g, dln_b = rowwise_bwd(_ln_silu, [c_conv], [], [ln_g, ln_b], [], [Cols(dmix1, half, 0)], [F32],
                                        name="conf_ln_bwd", tl=512)
    dc_glu, ddw_w_o, ddw_b_o = conv_bwd(c_glu, dw_w_o, dw_b_o, dc_conv, act=False, name="conf_conv_bwd")
    d_cacg = rowwise_bwd(_glu, [Cols(proj1, half, 0), Cols(proj1, half, 1)], [], [], [], [dc_glu], [BF16],
                         name="conf_glu_bwd", tl=512, merge=True)[0]
    dy_s5, du_post, dd_o, dglu_w = rowwise_bwd(
        _s5_post, [y_s5, Cols(proj1, half, 2)], [], [d_o, glu_w], [], [Cols(dmix1, half, 1)], [F32, F32],
        name="s5_post_bwd", tl=512)
    dy_seg = _seg_interleave(dy_s5)
    dxs_re = matmul(dy_seg, c_re_bd, name="s5_dx_re")
    dxs_im = matmul(dy_seg, c_im_neg_bd, name="s5_dx_im")
    dc_re_bd = matmul(dy_seg, xs_re, ta=True, name="s5_dc_re")
    dc_im_neg_bd = matmul(dy_seg, xs_im, ta=True, name="s5_dc_im")
    g_re, g_im, dab_re, dab_im = s5_scan(dxs_re, dxs_im, a_re_row, -a_im_row, reverse=True, states=(xs_re, xs_im),
                                         name="s5_scan_bwd", lw=LANE)
    dbb_re = _block_diag_take(matmul(u_seg, g_re, ta=True, name="s5_db_re"))
    dbb_im = _block_diag_take(matmul(u_seg, g_im, ta=True, name="s5_db_im"))
    du_im = matmul(g_im, b_im_bd, tb=True, name="s5_du_im")
    du = _seg_deinterleave(matmul(g_re, b_re_bd, tb=True, res=du_im, name="s5_du_re")) + du_post
    da_re, da_im, dlog_step, db_re, db_im = whole_bwd(
        _s5_prep, prep_in, 5,
        [dab_re.reshape(S5_GROUPS, S5_STATE), dab_im.reshape(S5_GROUPS, S5_STATE), dbb_re, dbb_im], name="s5_prep_bwd")
    gcn = (S5_GROUPS, S5_GROUP, S5_STATE)
    grads.update(
        o_a_re=da_re[None], o_a_im=da_im[None], o_log_step=dlog_step.reshape(1, S5_GROUPS),
        o_b_re=db_re.reshape(gcn).transpose(0, 2, 1)[None], o_b_im=db_im.reshape(gcn).transpose(0, 2, 1)[None],
        o_c_re=_block_diag_take(dc_re_bd).reshape(gcn)[None], o_c_im=-_block_diag_take(dc_im_neg_bd).reshape(gcn)[None],
        o_d=dd_o, o_glu_w=dglu_w[None], o_dw_w=ddw_w_o[None], o_dw_b=ddw_b_o, o_ln_g=dln_g, o_ln_b=dln_b)
    dproj1 = jnp.concatenate([d_cacg, du.astype(BF16)], axis=1)
    grads["o_w_in"] = matmul(hn1, dproj1, ta=True, name="odd_in_dw")[None]
    dhn1 = matmul(dproj1, w_in_o, tb=True, name="odd_in_dx")
    dx2, dmix_norm1 = rms_bwd(x2, w["mix_norm"][1:2], dhn1, dx3, "mix1_norm_bwd")

    dx1, g_ffn0 = ffn_bwd(0, x1, ffn0_saved, dx2)
    for k in g_ffn0:
        grads[k] = jnp.stack([g_ffn0[k].reshape(g_ffn1[k].shape), g_ffn1[k]]).reshape(w[k].shape)
    dmix0 = matmul(dx1, w_out_e, tb=True, name="even_out_dx")
    grads["e_w_out"] = matmul(mix0, dx1, ta=True, name="even_out_dw")[None]
    dq, dk, dv, dg = retention_bwd(proj0, cos2, sin2, ret_states, dmix0)
    dxbc_c, ddt, dz, ddt_bias, da_log, dd_skip, dssm_norm = ssd_bwd(
        xbc, proj0, dt_bias, a_log, d_skip, w["e_ssm_norm"], ssd_states, dmix0)
    dxbc, dconv_w, dconv_b = conv_bwd(proj0, conv_w_e, conv_b_e, dxbc_c, act=True, off=xbc_off,
                                      name="ssd_conv_bwd", dx_dtype=BF16)
    dproj0 = jnp.concatenate([dq, dk, dv, dg, dz, dxbc, ddt], axis=1)
    grads["e_w_in"] = matmul(hn0, dproj0, ta=True, name="even_in_dw")[:, :EVEN_IN][None]
    dhn0 = matmul(dproj0, w_in_e, tb=True, name="even_in_dx")
    dx, dmix_norm0 = rms_bwd(x, w["mix_norm"][0:1], dhn0, dx1, "mix0_norm_bwd")
    grads.update(
        mix_norm=jnp.concatenate([dmix_norm0, dmix_norm1], axis=0), e_conv_w=dconv_w[None], e_conv_b=dconv_b,
        e_dt_bias=ddt_bias[:, :SSM_HEADS], e_a_log=da_log[:, :SSM_HEADS], e_d=dd_skip[:, :SSM_HEADS],
        e_ssm_norm=dssm_norm)
    return loss, dx, grads


def adamw(w, g, m, v, *, name):
    shape = w.shape
    cols = shape[-1]
    rows = w.size // cols
    tr = _tile(rows, max(8, (512 * 1024 // cols) // 8 * 8), unit=8)

    def body(w_ref, g_ref, m_ref, v_ref, d_ref, nm_ref, nv_ref):
        gv = g_ref[...]
        nm = ADAM_B1 * m_ref[...] + (1.0 - ADAM_B1) * gv
        nv = ADAM_B2 * v_ref[...] + (1.0 - ADAM_B2) * jnp.square(gv)
        m_hat = nm / (1.0 - ADAM_B1 ** ADAM_STEP)
        v_hat = nv / (1.0 - ADAM_B2 ** ADAM_STEP)
        d_ref[...] = -ADAM_LR * (m_hat / (jnp.sqrt(v_hat) + ADAM_EPS) + ADAM_WD * w_ref[...])
        nm_ref[...] = nm
        nv_ref[...] = nv

    spec = pl.BlockSpec((tr, cols), lambda i: (i, 0))
    outs = pl.pallas_call(
        body, name=name, grid=(rows // tr,), in_specs=[spec] * 4, out_specs=[spec] * 3,
        out_shape=[jax.ShapeDtypeStruct((rows, cols), F32)] * 3, compiler_params=_params(("parallel",)),
    )(*[t.reshape(rows, cols) for t in (w, g, m, v)])
    return [o.reshape(shape) for o in outs]


OTHER_CHIPS = ((1, 0), (0, 1), (1, 1))
ANY = pl.BlockSpec(memory_space=pl.ANY)


def _position():
    return lax.axis_index("x"), lax.axis_index("y"), lax.axis_index("c")


def _flip(v, f):
    return 1 - v if f else v


def _remote(src, dst, send_sem, recv_sem, device):
    return pltpu.make_async_remote_copy(src_ref=src, dst_ref=dst, send_sem=send_sem, recv_sem=recv_sem,
                                        device_id=device, device_id_type=MESH)


def gather_shards(big, small):
    def body(big_ref, small_ref, obig_ref, osmall_ref, send_sems, recv_sems, local_sems):
        x, y, c = _position()
        mine = 2 * x + y
        pairs = ((big_ref, obig_ref), (small_ref, osmall_ref))
        local = [pltpu.make_async_copy(src, dst.at[mine], local_sems.at[i]) for i, (src, dst) in enumerate(pairs)]
        for cp in local:
            cp.start()
        sends = []
        for j, (fx, fy) in enumerate(OTHER_CHIPS):
            peer = (_flip(x, fx), _flip(y, fy), c)
            for i, (src, dst) in enumerate(pairs):
                sends.append(_remote(src, dst.at[mine], send_sems.at[j, i], recv_sems.at[j, i], peer))
        for cp in sends:
            cp.start()
        for j, (fx, fy) in enumerate(OTHER_CHIPS):
            px, py = _flip(x, fx), _flip(y, fy)
            for i, (src, dst) in enumerate(pairs):
                _remote(src, dst.at[2 * px + py], send_sems.at[j, i], recv_sems.at[j, i], (px, py, c)).wait_recv()
        for cp in sends:
            cp.wait_send()
        for cp in local:
            cp.wait()

    return pl.pallas_call(
        body, name="gather_shards", in_specs=[ANY, ANY], out_specs=[ANY, ANY],
        out_shape=[jax.ShapeDtypeStruct((4,) + big.shape, big.dtype), jax.ShapeDtypeStruct((4,) + small.shape, small.dtype)],
        scratch_shapes=[pltpu.SemaphoreType.DMA((3, 2)), pltpu.SemaphoreType.DMA((3, 2)), pltpu.SemaphoreType.DMA((2,))],
        compiler_params=_params(),
    )(big, small)


def allreduce_small(pack):
    rows = pack.shape[0]

    def body(p_ref, o_ref, slots, send_sems, recv_sems):
        x, y, c = _position()
        me = 4 * x + 2 * y + c
        slots[me] = p_ref[...]
        flips = [((k >> 2) & 1, (k >> 1) & 1, k & 1) for k in range(1, 8)]
        sends = []
        for k, (fx, fy, fc) in enumerate(flips):
            peer = (_flip(x, fx), _flip(y, fy), _flip(c, fc))
            sends.append(_remote(p_ref, slots.at[me], send_sems.at[k], recv_sems.at[k], peer))
        for cp in sends:
            cp.start()
        for k, (fx, fy, fc) in enumerate(flips):
            px, py, pc = _flip(x, fx), _flip(y, fy), _flip(c, fc)
            _remote(p_ref, slots.at[4 * px + 2 * py + pc], send_sems.at[k], recv_sems.at[k], (px, py, pc)).wait_recv()
        for cp in sends:
            cp.wait_send()
        acc = slots[0]
        for d in range(1, 8):
            acc = acc + slots[d]
        o_ref[...] = acc

    vmem = pl.BlockSpec(memory_space=pltpu.VMEM)
    return pl.pallas_call(
        body, name="allreduce_small", in_specs=[vmem], out_specs=vmem,
        out_shape=jax.ShapeDtypeStruct(pack.shape, F32),
        scratch_shapes=[pltpu.VMEM((8, rows, LANE), F32), pltpu.SemaphoreType.DMA((7,)), pltpu.SemaphoreType.DMA((7,))],
        compiler_params=_params(),
    )(pack)


def exchange_halves(g):
    def body(g_ref, o_ref, send_sems, recv_sems):
        x, y, c = _position()
        copies = [_remote(g_ref.at[s, 1 - c], o_ref.at[s], send_sems.at[s], recv_sems.at[s], (x, y, 1 - c)) for s in range(4)]
        for cp in copies:
            cp.start()
        for cp in copies:
            cp.wait()

    return pl.pallas_call(
        body, name="exchange_halves", in_specs=[ANY], out_specs=ANY,
        out_shape=jax.ShapeDtypeStruct((4,) + g.shape[2:], g.dtype),
        scratch_shapes=[pltpu.SemaphoreType.DMA((4,)), pltpu.SemaphoreType.DMA((4,))],
        compiler_params=_params(),
    )(g)


def scatter_to_chips(a):
    def body(a_ref, o_ref, send_sems, recv_sems):
        x, y, c = _position()
        copies = []
        for j, (fx, fy) in enumerate(OTHER_CHIPS):
            px, py = _flip(x, fx), _flip(y, fy)
            copies.append(_remote(a_ref.at[2 * px + py], o_ref.at[j], send_sems.at[j], recv_sems.at[j], (px, py, c)))
        for cp in copies:
            cp.start()
        for cp in copies:
            cp.wait()

    return pl.pallas_call(
        body, name="scatter_to_chips", in_specs=[ANY], out_specs=ANY,
        out_shape=jax.ShapeDtypeStruct((3,) + a.shape[1:], a.dtype),
        scratch_shapes=[pltpu.SemaphoreType.DMA((3,)), pltpu.SemaphoreType.DMA((3,))],
        compiler_params=_params(),
    )(a)


def share_halves(r):
    def body(r_ref, o_ref, send_sem, recv_sem, local_sem):
        x, y, c = _position()
        local = pltpu.make_async_copy(r_ref, o_ref.at[c], local_sem)
        local.start()
        send = _remote(r_ref, o_ref.at[c], send_sem, recv_sem, (x, y, 1 - c))
        send.start()
        _remote(r_ref, o_ref.at[1 - c], send_sem, recv_sem, (x, y, 1 - c)).wait_recv()
        send.wait_send()
        local.wait()

    return pl.pallas_call(
        body, name="share_halves", in_specs=[ANY], out_specs=ANY,
        out_shape=jax.ShapeDtypeStruct((2,) + r.shape, r.dtype),
        scratch_shapes=[pltpu.SemaphoreType.DMA, pltpu.SemaphoreType.DMA, pltpu.SemaphoreType.DMA],
        compiler_params=_params(),
    )(r)


REDUCE_ROWS = 4568


def add_own_half(g, r, c_idx):
    m_rows = g.shape[2]
    tr = _tile(m_rows, REDUCE_ROWS, unit=8)

    def body(c_ref, g_ref, r_ref, o_ref):
        o_ref[...] = g_ref[0] + r_ref[...]

    return pl.pallas_call(
        body, name="add_own_half",
        grid_spec=pltpu.PrefetchScalarGridSpec(
            num_scalar_prefetch=1, grid=(4, m_rows // tr),
            in_specs=[pl.BlockSpec((1, 1, tr, LANE), lambda s, i, c: (s, c[0], i, 0)),
                      pl.BlockSpec((1, tr, LANE), lambda s, i, c: (s, i, 0))],
            out_specs=pl.BlockSpec((1, tr, LANE), lambda s, i, c: (s, i, 0))),
        out_shape=jax.ShapeDtypeStruct(r.shape, F32), compiler_params=_params(("parallel", "parallel")),
    )(c_idx, g, r)


def add_chip_parts(a, parts, chip_idx):
    m_rows = a.shape[1]
    tr = _tile(m_rows, REDUCE_ROWS, unit=8)

    def body(s_ref, a_ref, p0_ref, p1_ref, p2_ref, o_ref):
        o_ref[...] = ((a_ref[0] + p0_ref[0]) + p1_ref[0]) + p2_ref[0]

    part = lambda j: pl.BlockSpec((1, tr, LANE), lambda i, s, j=j: (j, i, 0))
    return pl.pallas_call(
        body, name="add_chip_parts",
        grid_spec=pltpu.PrefetchScalarGridSpec(
            num_scalar_prefetch=1, grid=(m_rows // tr,),
            in_specs=[pl.BlockSpec((1, tr, LANE), lambda i, s: (s[0], i, 0)), part(0), part(1), part(2)],
            out_specs=pl.BlockSpec((tr, LANE), lambda i, s: (i, 0))),
        out_shape=jax.ShapeDtypeStruct(a.shape[1:], F32), compiler_params=_params(("parallel",)),
    )(chip_idx, a, parts, parts, parts)


WEIGHTS = ("mix_norm", "e_w_in", "e_conv_w", "e_conv_b", "e_dt_bias", "e_a_log", "e_d", "e_ssm_norm", "e_w_out",
           "o_w_in", "o_dw_w", "o_dw_b", "o_ln_g", "o_ln_b", "o_a_re", "o_a_im", "o_b_re", "o_b_im", "o_c_re",
           "o_c_im", "o_d", "o_log_step", "o_glu_w", "o_w_out", "ffn_norm", "ffn_w_up", "ffn_dw_w", "ffn_dw_b",
           "ffn_w_down", "final_norm")
BIG = (("e_w_in", 2), ("e_w_out", 1), ("o_w_in", 2), ("o_glu_w", 1), ("o_w_out", 1), ("ffn_w_up", 2), ("ffn_w_down", 1))
SMALL_SHARDED = (("e_conv_w", 2), ("o_dw_w", 2), ("o_dw_b", 1), ("o_ln_g", 1), ("o_ln_b", 1), ("o_d", 1), ("ffn_dw_w", 2))
REPLICATED = tuple(n for n in WEIGHTS if n not in dict(BIG + SMALL_SHARDED))
PACK_ROWS = 8


def _pack(arrays, dtype, row_unit=PACK_ROWS):
    flat = jnp.concatenate([a.astype(dtype).reshape(-1) for a in arrays])
    rows = -(-flat.size // (LANE * row_unit)) * row_unit
    return jnp.pad(flat, (0, rows * LANE - flat.size)).reshape(rows, LANE)


def _unpack(flat, shapes, lead=()):
    out, off = [], 0
    for shape in shapes:
        size = int(np.prod(shape))
        out.append(flat[..., off:off + size].reshape(lead + tuple(shape)))
        off += size
    return out


def _join_shards(parts, axis):
    return jnp.concatenate([parts[s] for s in range(4)], axis=axis)


def _split_shards(full, axis):
    return jnp.stack(jnp.split(full, 4, axis=axis)).reshape(4, -1)


def kernel(x, mix_norm, e_w_in, e_conv_w, e_conv_b, e_dt_bias, e_a_log, e_d, e_ssm_norm, e_w_out, o_w_in, o_dw_w, o_dw_b, o_ln_g, o_ln_b, o_a_re, o_a_im, o_b_re, o_b_im, o_c_re, o_c_im, o_d, o_log_step, o_glu_w, o_w_out, ffn_norm, ffn_w_up, ffn_dw_w, ffn_dw_b, ffn_w_down, final_norm, loss_target, m_mix_norm, m_e_w_in, m_e_conv_w, m_e_conv_b, m_e_dt_bias, m_e_a_log, m_e_d, m_e_ssm_norm, m_e_w_out, m_o_w_in, m_o_dw_w, m_o_dw_b, m_o_ln_g, m_o_ln_b, m_o_a_re, m_o_a_im, m_o_b_re, m_o_b_im, m_o_c_re, m_o_c_im, m_o_d, m_o_log_step, m_o_glu_w, m_o_w_out, m_ffn_norm, m_ffn_w_up, m_ffn_dw_w, m_ffn_dw_b, m_ffn_w_down, m_final_norm, v_mix_norm, v_e_w_in, v_e_conv_w, v_e_conv_b, v_e_dt_bias, v_e_a_log, v_e_d, v_e_ssm_norm, v_e_w_out, v_o_w_in, v_o_dw_w, v_o_dw_b, v_o_ln_g, v_o_ln_b, v_o_a_re, v_o_a_im, v_o_b_re, v_o_b_im, v_o_c_re, v_o_c_im, v_o_d, v_o_log_step, v_o_glu_w, v_o_w_out, v_ffn_norm, v_ffn_w_up, v_ffn_dw_w, v_ffn_dw_b, v_ffn_w_down, v_final_norm):
    given = dict(locals())
    chip = 2 * lax.axis_index("x") + lax.axis_index("y")
    core = lax.axis_index("c")

    big_pack = _pack([given[n] for n, _ in BIG], BF16, row_unit=16)
    small_pack = _pack([given[n] for n, _ in SMALL_SHARDED], F32)
    all_big, all_small = gather_shards(big_pack, small_pack)
    w = {n: given[n] for n in REPLICATED}
    for names, packs in ((BIG, all_big), (SMALL_SHARDED, all_small)):
        parts = _unpack(packs.reshape(4, -1), [given[n].shape for n, _ in names], lead=(4,))
        for (n, axis), p in zip(names, parts):
            w[n] = _join_shards(p, axis)

    loss, dx, grads = local_step(x[0], loss_target[0], w)

    small_names = REPLICATED + tuple(n for n, _ in SMALL_SHARDED)
    small_sum = allreduce_small(_pack([grads[n] for n in small_names], F32))
    reduced = dict(zip(small_names, _unpack(small_sum.reshape(-1), [grads[n].shape for n in small_names])))
    for n, axis in SMALL_SHARDED:
        width = given[n].shape[axis]
        reduced[n] = lax.dynamic_slice_in_dim(reduced[n], chip * width, width, axis=axis)

    g_pack = jnp.concatenate([_split_shards(grads[n], axis) for n, axis in BIG], axis=1)
    g_pack = g_pack.reshape(4, 2, -1, LANE)
    core_sum = add_own_half(g_pack, exchange_halves(g_pack), core.reshape(1).astype(jnp.int32))
    mine = add_chip_parts(core_sum, scatter_to_chips(core_sum), chip.reshape(1).astype(jnp.int32))
    shard = share_halves(mine).reshape(-1)
    reduced.update(zip([n for n, _ in BIG], _unpack(shard, [given[n].shape for n, _ in BIG])))

    delta, new_m, new_v = {}, {}, {}
    for n, _ in BIG:
        delta[n], new_m[n], new_v[n] = adamw(given[n], reduced[n], given["m_" + n], given["v_" + n], name="adamw_" + n)
    shapes = [given[n].shape for n in small_names]
    packed = [_pack([src[n] for n in small_names], F32)
              for src in (given, reduced, {n: given["m_" + n] for n in small_names}, {n: given["v_" + n] for n in small_names})]
    for dst, res in zip((delta, new_m, new_v), adamw(*packed, name="adamw_small")):
        dst.update(zip(small_names, _unpack(res.reshape(-1), shapes)))

    total = lax.psum(loss[0, 0], ("x", "y", "c"))
    return (total, dx[None], *[reduced[n] for n in WEIGHTS], *[delta[n] for n in WEIGHTS],
            *[new_m[n] for n in WEIGHTS], *[new_v[n] for n in WEIGHTS])
```

```python
import functools
import math
from typing import NamedTuple

import numpy as np
import jax
import jax.numpy as jnp
from jax import lax
from jax.experimental import pallas as pl
from jax.experimental.pallas import tpu as pltpu

F32 = jnp.float32
BF16 = jnp.bfloat16
HIGHEST = lax.Precision.HIGHEST
MESH = pl.DeviceIdType.MESH

D_MODEL = 1024
EPS = 1e-6
RET_HEADS, RET_DK, RET_DV, CHUNK = 4, 128, 256, 128
ROPE_BASE = 10000.0
SSM_HEADS, SSM_P, SSM_N, SSM_GROUPS = 16, 64, 128, 2
SSM_DINNER = SSM_HEADS * SSM_P
EVEN_IN, EVEN_IN_PAD = 5648, 5760
S5_GROUPS, S5_GROUP, S5_STATE = 32, 16, 64
S5_LANES = S5_GROUPS * S5_STATE
SCAN_SEG = 8
D_FF = 2816
ADAM_LR, ADAM_B1, ADAM_B2, ADAM_EPS, ADAM_WD, ADAM_STEP = 0.001, 0.9, 0.999, 1e-08, 0.01, 10

LANE = 128
VMEM_LIMIT = 56 * 1024 * 1024


def _params(sem=None, **kw):
    return pltpu.CompilerParams(dimension_semantics=sem, vmem_limit_bytes=VMEM_LIMIT, **kw)


def _tile(n, target, unit=LANE):
    if n <= target:
        return n
    t = (target // unit) * unit
    while t >= unit:
        if n % t == 0:
            return t
        t -= unit
    return n


def _silu(x):
    return x * jax.nn.sigmoid(x)


def _mm(a, b):
    return jnp.dot(a.astype(BF16), b.astype(BF16), preferred_element_type=F32)


def _mm_nt(a, b):
    return lax.dot_general(a.astype(BF16), b.astype(BF16), (((1,), (1,)), ((), ())), preferred_element_type=F32)


def _mm_tn(a, b):
    return lax.dot_general(a.astype(BF16), b.astype(BF16), (((0,), (0,)), ((), ())), preferred_element_type=F32)


def _dot_hi(a, b):
    return jnp.dot(a, b, precision=HIGHEST, preferred_element_type=F32)


def _dot_hi_tn(a, b):
    return lax.dot_general(a, b, (((0,), (0,)), ((), ())), precision=HIGHEST, preferred_element_type=F32)


MATMUL_VMEM = 44 * 1024 * 1024


def matmul(a, b, *, ta=False, tb=False, res=None, out_dtype=F32, name):
    m, k = (a.shape[1], a.shape[0]) if ta else a.shape
    n = b.shape[0] if tb else b.shape[1]
    assert (b.shape[1] if tb else b.shape[0]) == k, (a.shape, b.shape, ta, tb)
    tm = _tile(m, 1536)
    tn = _tile(n, 640)
    if tn < 384:
        tn = _tile(n, 1536)
    res_bytes = 0 if res is None else res.dtype.itemsize

    def vmem(tm, tn):
        return 2 * (tm * k * a.dtype.itemsize + tn * k * b.dtype.itemsize + tm * tn * (jnp.dtype(out_dtype).itemsize + res_bytes))

    while vmem(tm, tn) > MATMUL_VMEM and tm % (2 * LANE) == 0:
        tm //= 2
    assert vmem(tm, tn) <= MATMUL_VMEM, (name, tm, tn, k)
    a_spec = pl.BlockSpec((k, tm), lambda i, j: (0, i)) if ta else pl.BlockSpec((tm, k), lambda i, j: (i, 0))
    b_spec = pl.BlockSpec((tn, k), lambda i, j: (j, 0)) if tb else pl.BlockSpec((k, tn), lambda i, j: (0, j))
    o_spec = pl.BlockSpec((tm, tn), lambda i, j: (i, j))
    dims = (((0 if ta else 1,), (1 if tb else 0,)), ((), ()))
    has_res = res is not None

    def body(a_ref, b_ref, *rest):
        o_ref = rest[-1]
        out = lax.dot_general(a_ref[...].astype(BF16), b_ref[...].astype(BF16), dims, preferred_element_type=F32)
        if has_res:
            out = out + rest[0][...].astype(F32)
        o_ref[...] = out.astype(o_ref.dtype)

    ins = [a, b] + ([res] if has_res else [])
    specs = [a_spec, b_spec] + ([o_spec] if has_res else [])
    return pl.pallas_call(
        body, name=name, grid=(m // tm, n // tn), in_specs=specs, out_specs=o_spec,
        out_shape=jax.ShapeDtypeStruct((m, n), out_dtype), compiler_params=_params(("parallel", "parallel")),
    )(*ins)


class Cols(NamedTuple):
    arr: jax.Array
    w: int
    j: int


def _cols(a):
    return a if isinstance(a, Cols) else Cols(a, a.shape[1], 0)


def _row_spec(c, tl):
    return pl.BlockSpec((tl, c.w), lambda i, j=c.j: (i, j))


def _whole_spec(p):
    return pl.BlockSpec(p.shape, lambda i, nd=p.ndim: (0,) * nd)


def rowwise_fwd(fn, rows, aux, pars, consts, outs, *, name, tl):
    rows = [_cols(r) for r in rows + aux]
    whole = list(pars) + list(consts)
    n_rows = len(rows)
    n_whole = len(whole)
    length = rows[0].arr.shape[0]
    tl = min(tl, length)

    def body(*refs):
        vals = [r[...].astype(F32) for r in refs[:n_rows]] + [r[...] for r in refs[n_rows:n_rows + n_whole]]
        res = fn(*vals)
        for o_ref, v in zip(refs[n_rows + n_whole:], res, strict=True):
            o_ref[...] = v.astype(o_ref.dtype)

    return pl.pallas_call(
        body, name=name, grid=(length // tl,),
        in_specs=[_row_spec(r, tl) for r in rows] + [_whole_spec(p) for p in whole],
        out_specs=[pl.BlockSpec((tl, w), lambda i: (i, 0)) for w, _ in outs],
        out_shape=[jax.ShapeDtypeStruct((length, w), dt) for w, dt in outs],
        compiler_params=_params(("parallel",)),
    )(*[r.arr for r in rows], *whole)


def rowwise_bwd(fn, rows, aux, pars, consts, cots, drow_dtypes, *, name, tl, add=None, merge=False):
    rows = [_cols(r) for r in rows]
    aux = [_cols(r) for r in aux]
    cots = [_cols(r) for r in cots]
    n_r, n_a, n_p, n_c, n_t = len(rows), len(aux), len(pars), len(consts), len(cots)
    length = rows[0].arr.shape[0]
    tl = min(tl, length)
    has_add = add is not None
    widths = [r.w for r in rows]

    def body(*refs):
        pos = 0
        r_vals = [r[...].astype(F32) for r in refs[pos:pos + n_r]]; pos += n_r
        a_vals = [r[...].astype(F32) for r in refs[pos:pos + n_a]]; pos += n_a
        p_vals = [r[...].astype(F32) for r in refs[pos:pos + n_p]]; pos += n_p
        c_vals = [r[...] for r in refs[pos:pos + n_c]]; pos += n_c
        t_vals = [r[...].astype(F32) for r in refs[pos:pos + n_t]]; pos += n_t
        add_val = None
        if has_add:
            add_val = refs[pos][...].astype(F32); pos += 1
        n_dr = 1 if merge else n_r
        dr_refs = refs[pos:pos + n_dr]; pos += n_dr
        dp_refs = refs[pos:pos + n_p]

        def f(*rp):
            return fn(*rp[:n_r], *a_vals, *rp[n_r:], *c_vals)

        _, vjp = jax.vjp(f, *r_vals, *p_vals)
        grads = vjp(tuple(t_vals))
        drows = list(grads[:n_r])
        if has_add:
            drows[0] = drows[0] + add_val
        if merge:
            off = 0
            for w, d in zip(widths, drows):
                dr_refs[0][:, off:off + w] = d.astype(dr_refs[0].dtype)
                off += w
        else:
            for r, d in zip(dr_refs, drows):
                r[...] = d.astype(r.dtype)
        i = pl.program_id(0)
        for r, d in zip(dp_refs, grads[n_r:]):
            @pl.when(i == 0)
            def _(r=r, d=d):
                r[...] = d

            @pl.when(i > 0)
            def _(r=r, d=d):
                r[...] += d

    if merge:
        dr_specs = [pl.BlockSpec((tl, sum(widths)), lambda i: (i, 0))]
        dr_shapes = [jax.ShapeDtypeStruct((length, sum(widths)), drow_dtypes[0])]
    else:
        dr_specs = [pl.BlockSpec((tl, w), lambda i: (i, 0)) for w in widths]
        dr_shapes = [jax.ShapeDtypeStruct((length, w), dt) for w, dt in zip(widths, drow_dtypes)]
    ins = [r.arr for r in rows + aux] + list(pars) + list(consts) + [r.arr for r in cots] + ([add] if has_add else [])
    specs = ([_row_spec(r, tl) for r in rows + aux] + [_whole_spec(p) for p in list(pars) + list(consts)]
             + [_row_spec(r, tl) for r in cots] + ([pl.BlockSpec((tl, add.shape[1]), lambda i: (i, 0))] if has_add else []))
    return pl.pallas_call(
        body, name=name, grid=(length // tl,), in_specs=specs,
        out_specs=dr_specs + [_whole_spec(p) for p in pars],
        out_shape=dr_shapes + [jax.ShapeDtypeStruct(p.shape, F32) for p in pars],
        compiler_params=_params(("arbitrary",)),
    )(*ins)


def whole_fwd(fn, ins, out_shapes, *, name):
    n_in = len(ins)

    def body(*refs):
        res = fn(*[r[...] for r in refs[:n_in]])
        for o_ref, v in zip(refs[n_in:], res, strict=True):
            o_ref[...] = v

    return pl.pallas_call(body, name=name, out_shape=[jax.ShapeDtypeStruct(s, F32) for s in out_shapes],
                          compiler_params=_params())(*ins)


def whole_bwd(fn, ins, n_diff, cots, *, name):
    n_in, n_t = len(ins), len(cots)

    def body(*refs):
        vals = [r[...] for r in refs[:n_in]]
        t_vals = [r[...] for r in refs[n_in:n_in + n_t]]
        _, vjp = jax.vjp(lambda *d: fn(*d, *vals[n_diff:]), *vals[:n_diff])
        for o_ref, g in zip(refs[n_in + n_t:], vjp(tuple(t_vals)), strict=True):
            o_ref[...] = g

    return pl.pallas_call(body, name=name, out_shape=[jax.ShapeDtypeStruct(a.shape, F32) for a in ins[:n_diff]],
                          compiler_params=_params())(*ins, *cots)


CONV_ROWS = 256


def _conv_geometry(x, w, cw, off):
    width = w.shape[1]
    x = Cols(x, width, 0)
    length = x.arr.shape[0]
    taps = w.shape[0]
    pad = -(-(taps - 1) // 8) * 8
    assert off % cw == 0 and width % cw == 0, (off, width, cw)
    return x, length, taps, pad, off // cw


def _conv_taps(xp_ref, w_ref, base, taps, pad, init):
    acc = init
    for k in range(taps):
        acc = acc + w_ref[k:k + 1, :] * xp_ref[pl.ds(base + pad - (taps - 1) + k, init.shape[0]), :]
    return acc


def conv_fwd(x, w, b, *, act, name, off=0, cw=LANE, out_dtype=F32):
    x, length, taps, pad, jb = _conv_geometry(x, w, cw, off)
    rc = min(CONV_ROWS, length)

    def body(x_ref, w_ref, b_ref, o_ref, xp_ref):
        xp_ref[0:pad, :] = jnp.zeros((pad, cw), F32)
        xp_ref[pad:pad + length, :] = x_ref[...].astype(F32)

        def chunk(r, carry):
            base = pl.multiple_of(r * rc, rc)
            acc = _conv_taps(xp_ref, w_ref, base, taps, pad, jnp.broadcast_to(b_ref[...], (rc, cw)))
            if act:
                acc = _silu(acc)
            o_ref[pl.ds(base, rc), :] = acc.astype(o_ref.dtype)
            return carry

        lax.fori_loop(0, length // rc, chunk, 0)

    return pl.pallas_call(
        body, name=name, grid=(x.w // cw,),
        in_specs=[pl.BlockSpec((length, cw), lambda j: (0, jb + j)), pl.BlockSpec((taps, cw), lambda j: (0, j)),
                  pl.BlockSpec((1, cw), lambda j: (0, j))],
        out_specs=pl.BlockSpec((length, cw), lambda j: (0, j)),
        out_shape=jax.ShapeDtypeStruct((length, x.w), out_dtype),
        scratch_shapes=[pltpu.VMEM((pad + length, cw), F32)],
        compiler_params=_params(("parallel",)),
    )(x.arr, w, b)


def conv_bwd(x, w, b, dy, *, act, name, off=0, cw=LANE, dx_dtype=F32):
    x, length, taps, pad, jb = _conv_geometry(x, w, cw, off)
    rc = min(CONV_ROWS, length)

    def body(x_ref, w_ref, b_ref, dy_ref, dx_ref, dw_ref, db_ref, xp_ref, gp_ref):
        xp_ref[0:pad, :] = jnp.zeros((pad, cw), F32)
        xp_ref[pad:pad + length, :] = x_ref[...].astype(F32)
        gp_ref[length:length + pad, :] = jnp.zeros((pad, cw), F32)
        if act:
            def pre_chunk(r, carry):
                base = pl.multiple_of(r * rc, rc)
                pre = _conv_taps(xp_ref, w_ref, base, taps, pad, jnp.broadcast_to(b_ref[...], (rc, cw)))
                sig = jax.nn.sigmoid(pre)
                gp_ref[pl.ds(base, rc), :] = dy_ref[pl.ds(base, rc), :].astype(F32) * (sig * (1.0 + pre * (1.0 - sig)))
                return carry

            lax.fori_loop(0, length // rc, pre_chunk, 0)
        else:
            gp_ref[0:length, :] = dy_ref[...].astype(F32)
        dw_ref[...] = jnp.zeros((taps, cw), F32)
        db_ref[...] = jnp.zeros((1, cw), F32)

        def chunk(r, carry):
            base = pl.multiple_of(r * rc, rc)
            acc = jnp.zeros((rc, cw), F32)
            g = gp_ref[pl.ds(base, rc), :]
            for k in range(taps):
                acc = acc + w_ref[k:k + 1, :] * gp_ref[pl.ds(base + (taps - 1) - k, rc), :]
                xs = xp_ref[pl.ds(base + pad - (taps - 1) + k, rc), :]
                dw_ref[k:k + 1, :] += jnp.sum(g * xs, axis=0, keepdims=True)
            db_ref[...] += jnp.sum(g, axis=0, keepdims=True)
            dx_ref[pl.ds(base, rc), :] = acc.astype(dx_ref.dtype)
            return carry

        lax.fori_loop(0, length // rc, chunk, 0)

    dy = _cols(dy)
    assert dy.j == 0 and dy.w == x.w
    return pl.pallas_call(
        body, name=name, grid=(x.w // cw,),
        in_specs=[pl.BlockSpec((length, cw), lambda j: (0, jb + j)), pl.BlockSpec((taps, cw), lambda j: (0, j)),
                  pl.BlockSpec((1, cw), lambda j: (0, j)), pl.BlockSpec((length, cw), lambda j: (0, j))],
        out_specs=[pl.BlockSpec((length, cw), lambda j: (0, j)), pl.BlockSpec((taps, cw), lambda j: (0, j)),
                   pl.BlockSpec((1, cw), lambda j: (0, j))],
        out_shape=[jax.ShapeDtypeStruct((length, x.w), dx_dtype), jax.ShapeDtypeStruct((taps, x.w), F32),
                   jax.ShapeDtypeStruct((1, x.w), F32)],
        scratch_shapes=[pltpu.VMEM((pad + length, cw), F32), pltpu.VMEM((length + pad, cw), F32)],
        compiler_params=_params(("parallel",)),
    )(x.arr, w, b, dy.arr)


def _retention_consts():
    h = np.arange(RET_HEADS, dtype=np.float32)
    log_g = np.log1p(-(2.0 ** (-5.0 - h))).astype(np.float32)
    idx = np.arange(CHUNK, dtype=np.float32)
    diff = idx[:, None] - idx[None, :]
    intra = np.where(diff[None] >= 0, np.exp(np.maximum(diff, 0.0)[None] * log_g[:, None, None]), 0.0)
    zeta = np.exp((CHUNK - 1 - idx)[None, :] * log_g[:, None])
    xi = np.exp((idx + 1)[None, :] * log_g[:, None])
    decay = np.exp(CHUNK * log_g)
    zeta = np.broadcast_to(zeta[:, :, None], (RET_HEADS, CHUNK, RET_DK))
    xi = np.broadcast_to(xi[:, :, None], (RET_HEADS, CHUNK, RET_DV))
    return (jnp.asarray(intra, F32), jnp.asarray(zeta, F32), jnp.asarray(xi, F32), [float(d) for d in decay])


def _rotary_tables(length):
    inv = ROPE_BASE ** (-jnp.arange(0, RET_DK, 2, dtype=F32) / RET_DK)
    ang = jnp.arange(length).astype(F32)[:, None] * inv[None, :]
    cos, sin = jnp.cos(ang), jnp.sin(ang)
    return jnp.concatenate([cos, cos], axis=1), jnp.concatenate([-sin, sin], axis=1)


def _rot(x, cos2, sin2):
    return x * cos2 + pltpu.roll(x, RET_DK // 2, 1) * sin2


def _rot_t(y, cos2, sin2):
    return y * cos2 + pltpu.roll(y * sin2, RET_DK // 2, 1)


def _head_decay(h, decays):
    d = jnp.float32(decays[-1])
    for i in range(len(decays) - 2, -1, -1):
        d = jnp.where(h == i, jnp.float32(decays[i]), d)
    return d


def _ret_chunk(q, k, v, g, state, intra, zeta, xi, decay):
    s = _mm_nt(q, k) * intra
    kv = _mm_tn(k * zeta, v)
    o = _mm(s, v) + _mm(q, state) * xi
    oc = o - jnp.mean(o, axis=-1, keepdims=True)
    r = oc * lax.rsqrt(jnp.mean(oc * oc, axis=-1, keepdims=True) + EPS)
    return _silu(g) * r, state * decay + kv


def _ret_specs(rev, nc):
    def cidx(c):
        return nc - 1 - c if rev else c
    return [
        pl.BlockSpec((CHUNK, RET_DK), lambda h, c: (cidx(c), h)),
        pl.BlockSpec((CHUNK, RET_DK), lambda h, c: (cidx(c), RET_HEADS + h)),
        pl.BlockSpec((CHUNK, RET_DV), lambda h, c: (cidx(c), 4 + h)),
        pl.BlockSpec((CHUNK, RET_DV), lambda h, c: (cidx(c), 8 + h)),
        pl.BlockSpec((CHUNK, RET_DK), lambda h, c: (cidx(c), 0)),
        pl.BlockSpec((CHUNK, RET_DK), lambda h, c: (cidx(c), 0)),
        pl.BlockSpec((1, CHUNK, CHUNK), lambda h, c: (h, 0, 0)),
        pl.BlockSpec((1, CHUNK, RET_DK), lambda h, c: (h, 0, 0)),
        pl.BlockSpec((1, CHUNK, RET_DV), lambda h, c: (h, 0, 0)),
    ], cidx


def retention_fwd(proj, cos2, sin2):
    length = proj.shape[0]
    nc = length // CHUNK
    intra, zeta, xi, decays = _retention_consts()
    specs, _ = _ret_specs(False, nc)
    scale = RET_DK ** -0.5

    def body(q_ref, k_ref, v_ref, g_ref, cos_ref, sin_ref, intra_ref, zeta_ref, xi_ref, y_ref, st_ref, state):
        h, c = pl.program_id(0), pl.program_id(1)

        @pl.when(c == 0)
        def _():
            state[...] = jnp.zeros_like(state)

        q = _rot(q_ref[...], cos_ref[...], sin_ref[...])
        k = _rot(k_ref[...], cos_ref[...], sin_ref[...]) * scale
        st_ref[0, 0] = state[...]
        y, new_state = _ret_chunk(q, k, v_ref[...], g_ref[...], state[...], intra_ref[0], zeta_ref[0], xi_ref[0],
                                  _head_decay(h, decays))
        y_ref[...] = y.astype(y_ref.dtype)
        state[...] = new_state

    return pl.pallas_call(
        body, name="retention_fwd", grid=(RET_HEADS, nc), in_specs=specs,
        out_specs=[pl.BlockSpec((CHUNK, RET_DV), lambda h, c: (c, h)),
                   pl.BlockSpec((1, 1, RET_DK, RET_DV), lambda h, c: (h, c, 0, 0))],
        out_shape=[jax.ShapeDtypeStruct((length, RET_HEADS * RET_DV), BF16),
                   jax.ShapeDtypeStruct((RET_HEADS, nc, RET_DK, RET_DV), F32)],
        scratch_shapes=[pltpu.VMEM((RET_DK, RET_DV), F32)],
        compiler_params=_params(("parallel", "arbitrary")),
    )(proj, proj, proj, proj, cos2, sin2, intra, zeta, xi)


def retention_bwd(proj, cos2, sin2, states, dmix):
    length = proj.shape[0]
    nc = length // CHUNK
    intra, zeta, xi, decays = _retention_consts()
    specs, cidx = _ret_specs(True, nc)
    scale = RET_DK ** -0.5

    def body(q_ref, k_ref, v_ref, g_ref, cos_ref, sin_ref, intra_ref, zeta_ref, xi_ref, st_ref, dy_ref,
             dq_ref, dk_ref, dv_ref, dg_ref, dstate):
        h, c = pl.program_id(0), pl.program_id(1)

        @pl.when(c == 0)
        def _():
            dstate[...] = jnp.zeros_like(dstate)

        cos2v, sin2v = cos_ref[...], sin_ref[...]
        q = _rot(q_ref[...], cos2v, sin2v)
        k = _rot(k_ref[...], cos2v, sin2v) * scale
        decay = _head_decay(h, decays)
        intra_v, zeta_v, xi_v = intra_ref[0], zeta_ref[0], xi_ref[0]
        _, vjp = jax.vjp(lambda q, k, v, g, s: _ret_chunk(q, k, v, g, s, intra_v, zeta_v, xi_v, decay),
                         q, k, v_ref[...], g_ref[...], st_ref[0, 0])
        dq, dk, dv, dg, ds = vjp((dy_ref[...].astype(F32), dstate[...]))
        dq_ref[...] = _rot_t(dq, cos2v, sin2v).astype(dq_ref.dtype)
        dk_ref[...] = _rot_t(dk * scale, cos2v, sin2v).astype(dk_ref.dtype)
        dv_ref[...] = dv.astype(dv_ref.dtype)
        dg_ref[...] = dg.astype(dg_ref.dtype)
        dstate[...] = ds

    specs = specs + [pl.BlockSpec((1, 1, RET_DK, RET_DV), lambda h, c: (h, cidx(c), 0, 0)),
                     pl.BlockSpec((CHUNK, RET_DV), lambda h, c: (cidx(c), h))]
    return pl.pallas_call(
        body, name="retention_bwd", grid=(RET_HEADS, nc), in_specs=specs,
        out_specs=[pl.BlockSpec((CHUNK, RET_DK), lambda h, c: (cidx(c), h)),
                   pl.BlockSpec((CHUNK, RET_DK), lambda h, c: (cidx(c), h)),
                   pl.BlockSpec((CHUNK, RET_DV), lambda h, c: (cidx(c), h)),
                   pl.BlockSpec((CHUNK, RET_DV), lambda h, c: (cidx(c), h))],
        out_shape=[jax.ShapeDtypeStruct((length, RET_HEADS * RET_DK), BF16),
                   jax.ShapeDtypeStruct((length, RET_HEADS * RET_DK), BF16),
                   jax.ShapeDtypeStruct((length, RET_HEADS * RET_DV), BF16),
                   jax.ShapeDtypeStruct((length, RET_HEADS * RET_DV), BF16)],
        scratch_shapes=[pltpu.VMEM((RET_DK, RET_DV), F32)],
        compiler_params=_params(("parallel", "arbitrary")),
    )(proj, proj, proj, proj, cos2, sin2, intra, zeta, xi, states, dmix)


def _ssd_consts():
    tri = np.tril(np.ones((CHUNK, CHUNK), np.float32))
    expand = np.zeros((LANE, SSM_DINNER), np.float32)
    for h in range(SSM_HEADS):
        expand[h, h * SSM_P:(h + 1) * SSM_P] = 1.0
    return jnp.asarray(tri), jnp.asarray(tri.T.copy()), jnp.asarray(expand)


def _ssd_chunk(xs, bm, cm, dtr, z, state, dt_bias, a_log, d_skip, norm_w, tri, tri_t, expand):
    gw = SSM_DINNER // SSM_GROUPS
    dt = jax.nn.softplus(dtr + dt_bias)
    da = dt * (-jnp.exp(a_log))
    acs = _dot_hi(tri, da)
    acs_t = _dot_hi_tn(da, tri_t)
    dt_x = _dot_hi(dt, expand)
    da_x = _dot_hi(da, expand)
    acs_x = _dot_hi(tri, da_x)
    tot_x = jnp.sum(da_x, axis=0, keepdims=True)
    x_dt = xs * dt_x
    x_dec = x_dt * jnp.exp(tot_x - acs_x)
    e_acs = jnp.exp(acs_x)
    e_tot = jnp.exp(tot_x)
    lane = lax.broadcasted_iota(jnp.int32, (CHUNK, LANE), 1)
    sub = lax.broadcasted_iota(jnp.int32, (CHUNK, LANE), 0)
    causal = sub >= lane
    ys, new_states = [], []
    for g in range(SSM_GROUPS):
        bg = bm[:, g * SSM_N:(g + 1) * SSM_N]
        cg = cm[:, g * SSM_N:(g + 1) * SSM_N]
        sg = state[:, g * gw:(g + 1) * gw]
        cb = _mm_nt(cg, bg)
        y_off = _mm(cg, sg) * e_acs[:, g * gw:(g + 1) * gw]
        new_states.append(sg * e_tot[:, g * gw:(g + 1) * gw] + _mm_tn(bg, x_dec[:, g * gw:(g + 1) * gw]))
        pairs = []
        for p in range(gw // LANE):
            hp = g * (gw // LANE) + p
            xp = x_dt[:, hp * LANE:(hp + 1) * LANE]
            halves = []
            for head in (2 * hp, 2 * hp + 1):
                col = jnp.sum(jnp.where(lane == head, acs, 0.0), axis=1, keepdims=True)
                row = jnp.sum(jnp.where(sub == head, acs_t, 0.0), axis=0, keepdims=True)
                decay = jnp.exp(jnp.where(causal, col - row, -1e30))
                halves.append(_mm(cb * decay, xp))
            pairs.append(jnp.where(lane < SSM_P, halves[0], halves[1]))
        ys.append(jnp.concatenate(pairs, axis=1) + y_off)
    d_x = jnp.mean(_dot_hi(jnp.broadcast_to(d_skip, (8, LANE)), expand), axis=0, keepdims=True)
    y = (jnp.concatenate(ys, axis=1) + d_x * xs) * _silu(z)
    normed = []
    for g in range(SSM_GROUPS):
        yg = y[:, g * gw:(g + 1) * gw]
        normed.append(yg * lax.rsqrt(jnp.mean(yg * yg, axis=-1, keepdims=True) + EPS))
    return jnp.concatenate(normed, axis=1) * norm_w, jnp.concatenate(new_states, axis=1)


XBC = SSM_DINNER + 2 * SSM_GROUPS * SSM_N


def _ssd_specs(rev, nc):
    def cidx(c):
        return nc - 1 - c if rev else c
    row = lambda w, j: pl.BlockSpec((CHUNK, w), lambda c: (cidx(c), j))
    whole = lambda shape: pl.BlockSpec(shape, lambda c: (0,) * len(shape))
    return [row(XBC, 0), row(LANE, 5632 // LANE), row(SSM_DINNER, 3),
            whole((1, LANE)), whole((1, LANE)), whole((1, LANE)), whole((1, SSM_DINNER)),
            whole((CHUNK, CHUNK)), whole((CHUNK, CHUNK)), whole((LANE, SSM_DINNER))], cidx


def ssd_fwd(xbc, proj, dt_bias, a_log, d_skip, norm_w):
    length = proj.shape[0]
    nc = length // CHUNK
    tri, tri_t, expand = _ssd_consts()
    specs, _ = _ssd_specs(False, nc)

    def body(xbc_ref, dt_ref, z_ref, dtb_ref, alog_ref, d_ref, nw_ref, tri_ref, trit_ref, e_ref, y_ref, st_ref, state):
        @pl.when(pl.program_id(0) == 0)
        def _():
            state[...] = jnp.zeros_like(state)

        st_ref[0] = state[...]
        y, new_state = _ssd_chunk(
            xbc_ref[:, 0:SSM_DINNER], xbc_ref[:, SSM_DINNER:SSM_DINNER + 256], xbc_ref[:, SSM_DINNER + 256:XBC],
            dt_ref[...], z_ref[...], state[...], dtb_ref[...], alog_ref[...], d_ref[...], nw_ref[...],
            tri_ref[...], trit_ref[...], e_ref[...])
        y_ref[...] = y.astype(y_ref.dtype)
        state[...] = new_state

    return pl.pallas_call(
        body, name="ssd_fwd", grid=(nc,), in_specs=specs,
        out_specs=[pl.BlockSpec((CHUNK, SSM_DINNER), lambda c: (c, 0)),
                   pl.BlockSpec((1, SSM_N, SSM_DINNER), lambda c: (c, 0, 0))],
        out_shape=[jax.ShapeDtypeStruct((length, SSM_DINNER), BF16),
                   jax.ShapeDtypeStruct((nc, SSM_N, SSM_DINNER), F32)],
        scratch_shapes=[pltpu.VMEM((SSM_N, SSM_DINNER), F32)],
        compiler_params=_params(("arbitrary",)),
    )(xbc, proj, proj, dt_bias, a_log, d_skip, norm_w, tri, tri_t, expand)


def ssd_bwd(xbc, proj, dt_bias, a_log, d_skip, norm_w, states, dmix):
    length = proj.shape[0]
    nc = length // CHUNK
    tri, tri_t, expand = _ssd_consts()
    specs, cidx = _ssd_specs(True, nc)

    def body(xbc_ref, dt_ref, z_ref, dtb_ref, alog_ref, d_ref, nw_ref, tri_ref, trit_ref, e_ref, st_ref, dy_ref,
             dxbc_ref, ddt_ref, dz_ref, ddtb_ref, dalog_ref, dd_ref, dnw_ref, dstate):
        c = pl.program_id(0)

        @pl.when(c == 0)
        def _():
            dstate[...] = jnp.zeros_like(dstate)

        tri_v, trit_v, e_v = tri_ref[...], trit_ref[...], e_ref[...]
        _, vjp = jax.vjp(
            lambda *a: _ssd_chunk(*a, tri_v, trit_v, e_v),
            xbc_ref[:, 0:SSM_DINNER], xbc_ref[:, SSM_DINNER:SSM_DINNER + 256], xbc_ref[:, SSM_DINNER + 256:XBC],
            dt_ref[...], z_ref[...], st_ref[0], dtb_ref[...], alog_ref[...], d_ref[...], nw_ref[...])
        dxs, dbm, dcm, ddt, dz, ds, ddtb, dalog, dd, dnw = vjp((dy_ref[...].astype(F32), dstate[...]))
        dxbc_ref[:, 0:SSM_DINNER] = dxs
        dxbc_ref[:, SSM_DINNER:SSM_DINNER + 256] = dbm
        dxbc_ref[:, SSM_DINNER + 256:XBC] = dcm
        ddt_ref[...] = ddt.astype(ddt_ref.dtype)
        dz_ref[...] = dz.astype(dz_ref.dtype)
        dstate[...] = ds
        for r, d in ((ddtb_ref, ddtb), (dalog_ref, dalog), (dd_ref, dd), (dnw_ref, dnw)):
            @pl.when(c == 0)
            def _(r=r, d=d):
                r[...] = d

            @pl.when(c > 0)
            def _(r=r, d=d):
                r[...] += d

    whole = lambda shape: pl.BlockSpec(shape, lambda c: (0,) * len(shape))
    specs = specs + [pl.BlockSpec((1, SSM_N, SSM_DINNER), lambda c: (cidx(c), 0, 0)),
                     pl.BlockSpec((CHUNK, SSM_DINNER), lambda c: (cidx(c), 1))]
    return pl.pallas_call(
        body, name="ssd_bwd", grid=(nc,), in_specs=specs,
        out_specs=[pl.BlockSpec((CHUNK, XBC), lambda c: (cidx(c), 0)), pl.BlockSpec((CHUNK, LANE), lambda c: (cidx(c), 0)),
                   pl.BlockSpec((CHUNK, SSM_DINNER), lambda c: (cidx(c), 0)),
                   whole((1, LANE)), whole((1, LANE)), whole((1, LANE)), whole((1, SSM_DINNER))],
        out_shape=[jax.ShapeDtypeStruct((length, XBC), F32), jax.ShapeDtypeStruct((length, LANE), BF16),
                   jax.ShapeDtypeStruct((length, SSM_DINNER), BF16),
                   jax.ShapeDtypeStruct((1, LANE), F32), jax.ShapeDtypeStruct((1, LANE), F32),
                   jax.ShapeDtypeStruct((1, LANE), F32), jax.ShapeDtypeStruct((1, SSM_DINNER), F32)],
        scratch_shapes=[pltpu.VMEM((SSM_N, SSM_DINNER), F32)],
        compiler_params=_params(("arbitrary",)),
    )(xbc, proj, proj, dt_bias, a_log, d_skip, norm_w, tri, tri_t, expand, states, dmix)


def _cmul(ar, ai, br, bi):
    return ar * br - ai * bi, ar * bi + ai * br


def s5_scan(b_re, b_im, a_re, a_im, *, reverse=False, states=None, name, lw=256):
    length, lanes = b_re.shape
    nk = length // SCAN_SEG
    with_da = states is not None
    assert reverse or not with_da

    def shift(v):
        sub = lax.broadcasted_iota(jnp.int32, v.shape, 0)
        if reverse:
            return jnp.where(sub == SCAN_SEG - 1, 0.0, pltpu.roll(v, SCAN_SEG - 1, 0))
        return jnp.where(sub == 0, 0.0, pltpu.roll(v, 1, 0))

    def body(*refs):
        if with_da:
            bre_ref, bim_ref, are_ref, aim_ref, sre_ref, sim_ref, xre_ref, xim_ref, dare_ref, daim_ref = refs
        else:
            bre_ref, bim_ref, are_ref, aim_ref, xre_ref, xim_ref = refs
        ar = jnp.broadcast_to(are_ref[...], (SCAN_SEG, lw))
        ai = jnp.broadcast_to(aim_ref[...], (SCAN_SEG, lw))

        def tile(i):
            k = (nk - 1 - i) if reverse else i
            return pl.ds(pl.multiple_of(k * SCAN_SEG, SCAN_SEG), SCAN_SEG)

        def local(i, carry):
            xr, xi, pr, pi = carry
            rows = tile(i)
            mr, mi = _cmul(ar, ai, xr, xi)
            xr, xi = mr + bre_ref[rows, :], mi + bim_ref[rows, :]
            xre_ref[rows, :] = xr
            xim_ref[rows, :] = xi
            pr, pi = _cmul(ar, ai, pr, pi)
            return xr, xi, pr, pi

        zero = jnp.zeros((SCAN_SEG, lw), F32)
        one = jnp.ones((SCAN_SEG, lw), F32)
        er, ei, pr, pi = lax.fori_loop(0, nk, local, (zero, zero, one, zero))
        cr, ci = zero, zero
        for _ in range(SCAN_SEG - 1):
            mr, mi = _cmul(pr, pi, cr, ci)
            cr, ci = shift(er + mr), shift(ei + mi)

        def fix(i, carry):
            pr, pi, dr, di = carry
            rows = tile(i)
            pr, pi = _cmul(ar, ai, pr, pi)
            mr, mi = _cmul(pr, pi, cr, ci)
            xr, xi = xre_ref[rows, :] + mr, xim_ref[rows, :] + mi
            xre_ref[rows, :] = xr
            xim_ref[rows, :] = xi
            if with_da:
                k = nk - 1 - i
                prev = pl.ds(pl.multiple_of(jnp.maximum(k - 1, 0) * SCAN_SEG, SCAN_SEG), SCAN_SEG)
                last = pl.ds((nk - 1) * SCAN_SEG, SCAN_SEG)
                sub = lax.broadcasted_iota(jnp.int32, (SCAN_SEG, lw), 0)
                wr = jnp.where(sub == 0, 0.0, pltpu.roll(sre_ref[last, :], 1, 0))
                wi = jnp.where(sub == 0, 0.0, pltpu.roll(sim_ref[last, :], 1, 0))
                sr = jnp.where(k == 0, wr, sre_ref[prev, :])
                si = jnp.where(k == 0, wi, sim_ref[prev, :])
                dr, di = dr + xr * sr + xi * si, di + xi * sr - xr * si
            return pr, pi, dr, di

        _, _, dr, di = lax.fori_loop(0, nk, fix, (one, zero, zero, zero))
        if with_da:
            dare_ref[...] = jnp.sum(dr, axis=0, keepdims=True)
            daim_ref[...] = jnp.sum(di, axis=0, keepdims=True)

    col = pl.BlockSpec((length, lw), lambda j: (0, j))
    vec = pl.BlockSpec((1, lw), lambda j: (0, j))
    ins = [b_re, b_im, a_re, a_im] + (list(states) if with_da else [])
    in_specs = [col, col, vec, vec] + ([col, col] if with_da else [])
    out_specs = [col, col] + ([vec, vec] if with_da else [])
    out_shape = [jax.ShapeDtypeStruct((length, lanes), F32)] * 2 + ([jax.ShapeDtypeStruct((1, lanes), F32)] * 2 if with_da else [])
    return pl.pallas_call(
        body, name=name, grid=(lanes // lw,), in_specs=in_specs, out_specs=out_specs, out_shape=out_shape,
        compiler_params=_params(("parallel",)),
    )(*ins)


def _seg_interleave(v):
    length = v.shape[0]
    return v.reshape(SCAN_SEG, length // SCAN_SEG, -1).transpose(1, 0, 2).reshape(length, -1)


def _seg_deinterleave(v):
    length = v.shape[0]
    return v.reshape(length // SCAN_SEG, SCAN_SEG, -1).transpose(1, 0, 2).reshape(length, -1)


def _block_diag(m):
    eye = jnp.eye(S5_GROUPS, dtype=m.dtype)
    return (m.reshape(S5_GROUPS, S5_GROUP, 1, S5_STATE) * eye[:, None, :, None]).reshape(S5_GROUPS * S5_GROUP, S5_LANES)


def _block_diag_take(full):
    idx = jnp.arange(S5_GROUPS)
    blocks = full.reshape(S5_GROUPS, S5_GROUP, S5_GROUPS, S5_STATE)[idx, :, idx, :]
    return blocks.reshape(S5_GROUPS * S5_GROUP, S5_STATE)


def _s5_prep(a_re, a_im, log_step, b_re, b_im, rep):
    step = jnp.exp(log_step)
    mag = jnp.exp(a_re * step)
    ab_re = mag * jnp.cos(a_im * step)
    ab_im = mag * jnp.sin(a_im * step)
    den = a_re * a_re + a_im * a_im
    f_re = ((ab_re - 1.0) * a_re + ab_im * a_im) / den
    f_im = (ab_im * a_re - (ab_re - 1.0) * a_im) / den
    fr, fi = _dot_hi(rep, f_re), _dot_hi(rep, f_im)
    return ab_re, ab_im, fr * b_re - fi * b_im, fr * b_im + fi * b_re


def _rms(x, g):
    return (x * lax.rsqrt(jnp.mean(x * x, axis=-1, keepdims=True) + EPS) * g,)


def _ffn_act(gate, up):
    return (_silu(gate) * up,)


def _glu(a, g):
    return (a * jax.nn.sigmoid(g),)


def _ln_silu(x, g, b):
    xc = x - jnp.mean(x, axis=-1, keepdims=True)
    var = jnp.mean(xc * xc, axis=-1, keepdims=True)
    return (_silu(xc * lax.rsqrt(var + EPS) * g + b),)


def _s5_post(y, u, d_skip, glu_w):
    s = jax.nn.gelu(y + d_skip * u)
    return (s * jax.nn.sigmoid(_mm(s, glu_w)),)


def loss_head(x, tgt, g, *, tl=512):
    length, d = x.shape
    tl = min(tl, length)

    def body(x_ref, t_ref, g_ref, loss_ref, dx_ref, dg_ref):
        i = pl.program_id(0)
        y, vjp = jax.vjp(lambda x, g: _rms(x, g)[0], x_ref[...], g_ref[...])
        err = y - t_ref[...]
        dx, dg = vjp(err * (1.0 / d))
        dx_ref[...] = dx
        part = jnp.broadcast_to(0.5 * jnp.sum(jnp.mean(err * err, axis=-1, keepdims=True), axis=0, keepdims=True), (1, LANE))

        @pl.when(i == 0)
        def _():
            loss_ref[...] = part
            dg_ref[...] = dg

        @pl.when(i > 0)
        def _():
            loss_ref[...] += part
            dg_ref[...] += dg

    row = pl.BlockSpec((tl, d), lambda i: (i, 0))
    return pl.pallas_call(
        body, name="loss_head", grid=(length // tl,),
        in_specs=[row, row, pl.BlockSpec((1, d), lambda i: (0, 0))],
        out_specs=[pl.BlockSpec((1, LANE), lambda i: (0, 0)), row, pl.BlockSpec((1, d), lambda i: (0, 0))],
        out_shape=[jax.ShapeDtypeStruct((1, LANE), F32), jax.ShapeDtypeStruct((length, d), F32),
                   jax.ShapeDtypeStruct((1, d), F32)],
        compiler_params=_params(("arbitrary",)),
    )(x, tgt, g)


def _pad_heads(v):
    return jnp.pad(v, ((0, 0), (0, LANE - v.shape[1])))


def local_step(x, tgt, w):
    length = x.shape[0]
    cos2, sin2 = _rotary_tables(length)
    grads = {}

    def rms_fwd(xin, g, name):
        return rowwise_fwd(_rms, [xin], [], [g], [], [(D_MODEL, BF16)], name=name, tl=512)[0]

    def rms_bwd(xin, g, dh, dxo, name):
        return rowwise_bwd(_rms, [xin], [], [g], [], [dh], [F32], name=name, tl=512, add=dxo)

    def ffn_fwd(i, xin):
        hf = rms_fwd(xin, w["ffn_norm"][i:i + 1], f"ffn{i}_norm")
        a = matmul(hf, w["ffn_w_up"][i], name=f"ffn{i}_up")
        ac = conv_fwd(a, w["ffn_dw_w"][i], w["ffn_dw_b"][i:i + 1], act=False, name=f"ffn{i}_conv")
        act = rowwise_fwd(_ffn_act, [Cols(ac, D_FF, 0), Cols(ac, D_FF, 1)], [], [], [], [(D_FF, BF16)],
                          name=f"ffn{i}_act", tl=256)[0]
        return matmul(act, w["ffn_w_down"][i], res=xin, name=f"ffn{i}_down"), (hf, a, ac, act)

    def ffn_bwd(i, xin, saved, dxo):
        hf, a, ac, act = saved
        dact = matmul(dxo, w["ffn_w_down"][i], tb=True, name=f"ffn{i}_down_dx")
        dw_down = matmul(act, dxo, ta=True, name=f"ffn{i}_down_dw")
        dac = rowwise_bwd(_ffn_act, [Cols(ac, D_FF, 0), Cols(ac, D_FF, 1)], [], [], [], [dact], [F32],
                          name=f"ffn{i}_act_bwd", tl=256, merge=True)[0]
        da, ddw_w, ddw_b = conv_bwd(a, w["ffn_dw_w"][i], w["ffn_dw_b"][i:i + 1], dac, act=False,
                                    name=f"ffn{i}_conv_bwd", dx_dtype=BF16)
        dw_up = matmul(hf, da, ta=True, name=f"ffn{i}_up_dw")
        dhf = matmul(da, w["ffn_w_up"][i], tb=True, name=f"ffn{i}_up_dx")
        dxin, dnorm = rms_bwd(xin, w["ffn_norm"][i:i + 1], dhf, dxo, f"ffn{i}_norm_bwd")
        return dxin, dict(ffn_norm=dnorm, ffn_w_up=dw_up, ffn_dw_w=ddw_w, ffn_dw_b=ddw_b, ffn_w_down=dw_down)

    w_in_e = jnp.pad(w["e_w_in"][0], ((0, 0), (0, EVEN_IN_PAD - EVEN_IN)))
    w_out_e = w["e_w_out"][0]
    conv_w_e, conv_b_e = w["e_conv_w"][0], w["e_conv_b"]
    dt_bias, a_log, d_skip = _pad_heads(w["e_dt_bias"]), _pad_heads(w["e_a_log"]), _pad_heads(w["e_d"])
    xbc_off = 4 * D_MODEL

    hn0 = rms_fwd(x, w["mix_norm"][0:1], "mix0_norm")
    proj0 = matmul(hn0, w_in_e, name="even_in")
    y_ret, ret_states = retention_fwd(proj0, cos2, sin2)
    xbc = conv_fwd(proj0, conv_w_e, conv_b_e, act=True, off=xbc_off, name="ssd_conv")
    y_ssm, ssd_states = ssd_fwd(xbc, proj0, dt_bias, a_log, d_skip, w["e_ssm_norm"])
    mix0 = jnp.concatenate([y_ret, y_ssm], axis=1)
    x1 = matmul(mix0, w_out_e, res=x, name="even_out")
    x2, ffn0_saved = ffn_fwd(0, x1)

    w_in_o, w_out_o, glu_w = w["o_w_in"][0], w["o_w_out"][0], w["o_glu_w"][0]
    dw_w_o, dw_b_o, ln_g, ln_b, d_o = w["o_dw_w"][0], w["o_dw_b"], w["o_ln_g"], w["o_ln_b"], w["o_d"]
    rep = jnp.asarray(np.repeat(np.eye(S5_GROUPS, dtype=np.float32), S5_GROUP, axis=0))
    rows_gc = (S5_GROUPS * S5_GROUP, S5_STATE)
    prep_in = [w["o_a_re"][0], w["o_a_im"][0], w["o_log_step"].reshape(S5_GROUPS, 1),
               w["o_b_re"][0].transpose(0, 2, 1).reshape(rows_gc), w["o_b_im"][0].transpose(0, 2, 1).reshape(rows_gc), rep]
    ab_re, ab_im, bb_re, bb_im = whole_fwd(
        _s5_prep, prep_in, [(S5_GROUPS, S5_STATE)] * 2 + [rows_gc] * 2, name="s5_prep")
    a_re_row, a_im_row = ab_re.reshape(1, S5_LANES), ab_im.reshape(1, S5_LANES)
    b_re_bd, b_im_bd = _block_diag(bb_re).astype(BF16), _block_diag(bb_im).astype(BF16)
    c_re_bd = _block_diag(w["o_c_re"][0].reshape(rows_gc)).astype(BF16)
    c_im_neg_bd = _block_diag(-w["o_c_im"][0].reshape(rows_gc)).astype(BF16)

    hn1 = rms_fwd(x2, w["mix_norm"][1:2], "mix1_norm")
    proj1 = matmul(hn1, w_in_o, name="odd_in")
    half = D_MODEL // 2
    c_glu = rowwise_fwd(_glu, [Cols(proj1, half, 0), Cols(proj1, half, 1)], [], [], [], [(half, F32)],
                        name="conf_glu", tl=512)[0]
    c_conv = conv_fwd(c_glu, dw_w_o, dw_b_o, act=False, name="conf_conv")
    c_out = rowwise_fwd(_ln_silu, [c_conv], [], [ln_g, ln_b], [], [(half, BF16)], name="conf_ln", tl=512)[0]
    u_seg = _seg_interleave(proj1[:, 2 * half:])
    bu_re = matmul(u_seg, b_re_bd, name="s5_bu_re")
    bu_im = matmul(u_seg, b_im_bd, name="s5_bu_im")
    xs_re, xs_im = s5_scan(bu_re, bu_im, a_re_row, a_im_row, name="s5_scan")
    y_im = matmul(xs_im, c_im_neg_bd, tb=True, name="s5_y_im")
    y_s5 = _seg_deinterleave(matmul(xs_re, c_re_bd, tb=True, res=y_im, name="s5_y_re"))
    s_out = rowwise_fwd(_s5_post, [y_s5, Cols(proj1, half, 2)], [], [d_o, glu_w], [], [(half, BF16)],
                        name="s5_post", tl=512)[0]
    mix1 = jnp.concatenate([c_out, s_out], axis=1)
    x3 = matmul(mix1, w_out_o, res=x2, name="odd_out")
    x4, ffn1_saved = ffn_fwd(1, x3)

    loss, dx4, dfinal = loss_head(x4, tgt, w["final_norm"].reshape(1, D_MODEL))
    grads["final_norm"] = dfinal.reshape(D_MODEL)

    dx3, g_ffn1 = ffn_bwd(1, x3, ffn1_saved, dx4)
    dmix1 = matmul(dx3, w_out_o, tb=True, name="odd_out_dx")
    grads["o_w_out"] = matmul(mix1, dx3, ta=True, name="odd_out_dw")[None]
    dc_conv, dln_g, dln_b = rowwise_bwd(_ln_silu, [c_conv], [], [ln_g, ln_b], [], [Cols(dmix1, half, 0)], [F32],
                                        name="conf_ln_bwd", tl=512)
    dc_glu, ddw_w_o, ddw_b_o = conv_bwd(c_glu, dw_w_o, dw_b_o, dc_conv, act=False, name="conf_conv_bwd")
    d_cacg = rowwise_bwd(_glu, [Cols(proj1, half, 0), Cols(proj1, half, 1)], [], [], [], [dc_glu], [BF16],
                         name="conf_glu_bwd", tl=512, merge=True)[0]
    dy_s5, du_post, dd_o, dglu_w = rowwise_bwd(
        _s5_post, [y_s5, Cols(proj1, half, 2)], [], [d_o, glu_w], [], [Cols(dmix1, half, 1)], [F32, F32],
        name="s5_post_bwd", tl=512)
    dy_seg = _seg_interleave(dy_s5)
    dxs_re = matmul(dy_seg, c_re_bd, name="s5_dx_re")
    dxs_im = matmul(dy_seg, c_im_neg_bd, name="s5_dx_im")
    dc_re_bd = matmul(dy_seg, xs_re, ta=True, name="s5_dc_re")
    dc_im_neg_bd = matmul(dy_seg, xs_im, ta=True, name="s5_dc_im")
    g_re, g_im, dab_re, dab_im = s5_scan(dxs_re, dxs_im, a_re_row, -a_im_row, reverse=True, states=(xs_re, xs_im),
                                         name="s5_scan_bwd", lw=LANE)
    dbb_re = _block_diag_take(matmul(u_seg, g_re, ta=True, name="s5_db_re"))
    dbb_im = _block_diag_take(matmul(u_seg, g_im, ta=True, name="s5_db_im"))
    du_im = matmul(g_im, b_im_bd, tb=True, name="s5_du_im")
    du = _seg_deinterleave(matmul(g_re, b_re_bd, tb=True, res=du_im, name="s5_du_re")) + du_post
    da_re, da_im, dlog_step, db_re, db_im = whole_bwd(
        _s5_prep, prep_in, 5,
        [dab_re.reshape(S5_GROUPS, S5_STATE), dab_im.reshape(S5_GROUPS, S5_STATE), dbb_re, dbb_im], name="s5_prep_bwd")
    gcn = (S5_GROUPS, S5_GROUP, S5_STATE)
    grads.update(
        o_a_re=da_re[None], o_a_im=da_im[None], o_log_step=dlog_step.reshape(1, S5_GROUPS),
        o_b_re=db_re.reshape(gcn).transpose(0, 2, 1)[None], o_b_im=db_im.reshape(gcn).transpose(0, 2, 1)[None],
        o_c_re=_block_diag_take(dc_re_bd).reshape(gcn)[None], o_c_im=-_block_diag_take(dc_im_neg_bd).reshape(gcn)[None],
        o_d=dd_o, o_glu_w=dglu_w[None], o_dw_w=ddw_w_o[None], o_dw_b=ddw_b_o, o_ln_g=dln_g, o_ln_b=dln_b)
    dproj1 = jnp.concatenate([d_cacg, du.astype(BF16)], axis=1)
    grads["o_w_in"] = matmul(hn1, dproj1, ta=True, name="odd_in_dw")[None]
    dhn1 = matmul(dproj1, w_in_o, tb=True, name="odd_in_dx")
    dx2, dmix_norm1 = rms_bwd(x2, w["mix_norm"][1:2], dhn1, dx3, "mix1_norm_bwd")

    dx1, g_ffn0 = ffn_bwd(0, x1, ffn0_saved, dx2)
    for k in g_ffn0:
        grads[k] = jnp.stack([g_ffn0[k].reshape(g_ffn1[k].shape), g_ffn1[k]]).reshape(w[k].shape)
    dmix0 = matmul(dx1, w_out_e, tb=True, name="even_out_dx")
    grads["e_w_out"] = matmul(mix0, dx1, ta=True, name="even_out_dw")[None]
    dq, dk, dv, dg = retention_bwd(proj0, cos2, sin2, ret_states, dmix0)
    dxbc_c, ddt, dz, ddt_bias, da_log, dd_skip, dssm_norm = ssd_bwd(
        xbc, proj0, dt_bias, a_log, d_skip, w["e_ssm_norm"], ssd_states, dmix0)
    dxbc, dconv_w, dconv_b = conv_bwd(proj0, conv_w_e, conv_b_e, dxbc_c, act=True, off=xbc_off,
                                      name="ssd_conv_bwd", dx_dtype=BF16)
    dproj0 = jnp.concatenate([dq, dk, dv, dg, dz, dxbc, ddt], axis=1)
    grads["e_w_in"] = matmul(hn0, dproj0, ta=True, name="even_in_dw")[:, :EVEN_IN][None]
    dhn0 = matmul(dproj0, w_in_e, tb=True, name="even_in_dx")
    dx, dmix_norm0 = rms_bwd(x, w["mix_norm"][0:1], dhn0, dx1, "mix0_norm_bwd")
    grads.update(
        mix_norm=jnp.concatenate([dmix_norm0, dmix_norm1], axis=0), e_conv_w=dconv_w[None], e_conv_b=dconv_b,
        e_dt_bias=ddt_bias[:, :SSM_HEADS], e_a_log=da_log[:, :SSM_HEADS], e_d=dd_skip[:, :SSM_HEADS],
        e_ssm_norm=dssm_norm)
    return loss, dx, grads


def adamw(w, g, m, v, *, name):
    shape = w.shape
    cols = shape[-1]
    rows = w.size // cols
    tr = _tile(rows, max(8, (512 * 1024 // cols) // 8 * 8), unit=8)

    def body(w_ref, g_ref, m_ref, v_ref, d_ref, nm_ref, nv_ref):
        gv = g_ref[...]
        nm = ADAM_B1 * m_ref[...] + (1.0 - ADAM_B1) * gv
        nv = ADAM_B2 * v_ref[...] + (1.0 - ADAM_B2) * jnp.square(gv)
        m_hat = nm / (1.0 - ADAM_B1 ** ADAM_STEP)
        v_hat = nv / (1.0 - ADAM_B2 ** ADAM_STEP)
        d_ref[...] = -ADAM_LR * (m_hat / (jnp.sqrt(v_hat) + ADAM_EPS) + ADAM_WD * w_ref[...])
        nm_ref[...] = nm
        nv_ref[...] = nv

    spec = pl.BlockSpec((tr, cols), lambda i: (i, 0))
    outs = pl.pallas_call(
        body, name=name, grid=(rows // tr,), in_specs=[spec] * 4, out_specs=[spec] * 3,
        out_shape=[jax.ShapeDtypeStruct((rows, cols), F32)] * 3, compiler_params=_params(("parallel",)),
    )(*[t.reshape(rows, cols) for t in (w, g, m, v)])
    return [o.reshape(shape) for o in outs]


OTHER_CHIPS = ((1, 0), (0, 1), (1, 1))
ANY = pl.BlockSpec(memory_space=pl.ANY)


def _position():
    return lax.axis_index("x"), lax.axis_index("y"), lax.axis_index("c")


def _flip(v, f):
    return 1 - v if f else v


def _remote(src, dst, send_sem, recv_sem, device):
    return pltpu.make_async_remote_copy(src_ref=src, dst_ref=dst, send_sem=send_sem, recv_sem=recv_sem,
                                        device_id=device, device_id_type=MESH)


def gather_shards(big, small):
    n_big, n_small = len(big), len(small)
    halves = [a.shape[0] // 2 for a in big]

    def body(*refs):
        big_refs, small_refs = refs[:n_big], refs[n_big:n_big + n_small]
        obig_refs = refs[n_big + n_small:2 * n_big + n_small]
        osmall_refs = refs[2 * n_big + n_small:2 * (n_big + n_small)]
        ici_send, ici_recv, d2d_send, d2d_recv, small_send, small_recv, local_sems = refs[2 * (n_big + n_small):]
        x, y, c = _position()
        mine = 2 * x + y
        local = [pltpu.make_async_copy(src, dst.at[mine], local_sems.at[i])
                 for i, (src, dst) in enumerate(zip(big_refs + small_refs, obig_refs + osmall_refs))]
        for cp in local:
            cp.start()

        def half(k, core):
            return pl.ds(pl.multiple_of(core * halves[k], 16), halves[k])

        sends = []
        for j, (fx, fy) in enumerate(OTHER_CHIPS):
            peer = (_flip(x, fx), _flip(y, fy), c)
            for k in range(n_big):
                sends.append(_remote(big_refs[k].at[half(k, c)], obig_refs[k].at[mine, half(k, c)],
                                     ici_send.at[j, k], ici_recv.at[j, k], peer))
            for k in range(n_small):
                sends.append(_remote(small_refs[k], osmall_refs[k].at[mine], small_send.at[j, k], small_recv.at[j, k], peer))
        for cp in sends:
            cp.start()
        for j, (fx, fy) in enumerate(OTHER_CHIPS):
            px, py = _flip(x, fx), _flip(y, fy)
            src_chip = 2 * px + py
            for k in range(n_big):
                landed = obig_refs[k].at[src_chip, half(k, c)]
                _remote(landed, landed, ici_send.at[j, k], ici_recv.at[j, k], (px, py, c)).wait_recv()
                fwd = _remote(landed, landed, d2d_send.at[j, k], d2d_recv.at[j, k], (x, y, 1 - c))
                fwd.start()
                sends.append(fwd)
        for j, (fx, fy) in enumerate(OTHER_CHIPS):
            px, py = _flip(x, fx), _flip(y, fy)
            src_chip = 2 * px + py
            for k in range(n_big):
                other = obig_refs[k].at[src_chip, half(k, 1 - c)]
                _remote(other, other, d2d_send.at[j, k], d2d_recv.at[j, k], (x, y, 1 - c)).wait_recv()
            for k in range(n_small):
                dst = osmall_refs[k].at[src_chip]
                _remote(small_refs[k], dst, small_send.at[j, k], small_recv.at[j, k], (px, py, c)).wait_recv()
        for cp in sends:
            cp.wait_send()
        for cp in local:
            cp.wait()

    arrays = list(big) + list(small)
    dma = pltpu.SemaphoreType.DMA
    return pl.pallas_call(
        body, name="gather_shards", in_specs=[ANY] * len(arrays), out_specs=[ANY] * len(arrays),
        out_shape=[jax.ShapeDtypeStruct((4,) + a.shape, a.dtype) for a in arrays],
        scratch_shapes=[dma((3, n_big)), dma((3, n_big)), dma((3, n_big)), dma((3, n_big)),
                        dma((3, n_small)), dma((3, n_small)), dma((n_big + n_small,))],
        compiler_params=_params(),
    )(*arrays)


def allreduce_small(pack):
    rows = pack.shape[0]

    def body(p_ref, o_ref, slots, send_sems, recv_sems):
        x, y, c = _position()
        me = 4 * x + 2 * y + c
        slots[me] = p_ref[...]
        flips = [((k >> 2) & 1, (k >> 1) & 1, k & 1) for k in range(1, 8)]
        sends = []
        for k, (fx, fy, fc) in enumerate(flips):
            peer = (_flip(x, fx), _flip(y, fy), _flip(c, fc))
            sends.append(_remote(p_ref, slots.at[me], send_sems.at[k], recv_sems.at[k], peer))
        for cp in sends:
            cp.start()
        for k, (fx, fy, fc) in enumerate(flips):
            px, py, pc = _flip(x, fx), _flip(y, fy), _flip(c, fc)
            _remote(p_ref, slots.at[4 * px + 2 * py + pc], send_sems.at[k], recv_sems.at[k], (px, py, pc)).wait_recv()
        for cp in sends:
            cp.wait_send()
        acc = slots[0]
        for d in range(1, 8):
            acc = acc + slots[d]
        o_ref[...] = acc

    vmem = pl.BlockSpec(memory_space=pltpu.VMEM)
    return pl.pallas_call(
        body, name="allreduce_small", in_specs=[vmem], out_specs=vmem,
        out_shape=jax.ShapeDtypeStruct(pack.shape, F32),
        scratch_shapes=[pltpu.VMEM((8, rows, LANE), F32), pltpu.SemaphoreType.DMA((7,)), pltpu.SemaphoreType.DMA((7,))],
        compiler_params=_params(),
    )(pack)


def exchange_halves(gs):
    n = len(gs)

    def body(*refs):
        g_refs, o_refs, (send_sems, recv_sems) = refs[:n], refs[n:2 * n], refs[2 * n:]
        x, y, c = _position()
        copies = [_remote(g_refs[k].at[:, 1 - c], o_refs[k], send_sems.at[k], recv_sems.at[k], (x, y, 1 - c)) for k in range(n)]
        for cp in copies:
            cp.start()
        for cp in copies:
            cp.wait()

    return pl.pallas_call(
        body, name="exchange_halves", in_specs=[ANY] * n, out_specs=[ANY] * n,
        out_shape=[jax.ShapeDtypeStruct((4,) + g.shape[2:], g.dtype) for g in gs],
        scratch_shapes=[pltpu.SemaphoreType.DMA((n,)), pltpu.SemaphoreType.DMA((n,))],
        compiler_params=_params(),
    )(*gs)


def scatter_to_chips(parts):
    n = len(parts)

    def body(*refs):
        a_refs, o_refs, (send_sems, recv_sems) = refs[:n], refs[n:2 * n], refs[2 * n:]
        x, y, c = _position()
        copies = []
        for j, (fx, fy) in enumerate(OTHER_CHIPS):
            px, py = _flip(x, fx), _flip(y, fy)
            for k in range(n):
                copies.append(_remote(a_refs[k].at[2 * px + py], o_refs[k].at[j], send_sems.at[j, k], recv_sems.at[j, k], (px, py, c)))
        for cp in copies:
            cp.start()
        for cp in copies:
            cp.wait()

    return pl.pallas_call(
        body, name="scatter_to_chips", in_specs=[ANY] * n, out_specs=[ANY] * n,
        out_shape=[jax.ShapeDtypeStruct((3,) + a.shape[1:], a.dtype) for a in parts],
        scratch_shapes=[pltpu.SemaphoreType.DMA((3, n)), pltpu.SemaphoreType.DMA((3, n))],
        compiler_params=_params(),
    )(*parts)


def share_halves(rs):
    n = len(rs)

    def body(*refs):
        r_refs, o_refs, (send_sems, recv_sems, local_sems) = refs[:n], refs[n:2 * n], refs[2 * n:]
        x, y, c = _position()
        local = [pltpu.make_async_copy(r_refs[k], o_refs[k].at[c], local_sems.at[k]) for k in range(n)]
        sends = [_remote(r_refs[k], o_refs[k].at[c], send_sems.at[k], recv_sems.at[k], (x, y, 1 - c)) for k in range(n)]
        for cp in local + sends:
            cp.start()
        for k in range(n):
            _remote(r_refs[k], o_refs[k].at[1 - c], send_sems.at[k], recv_sems.at[k], (x, y, 1 - c)).wait_recv()
        for cp in sends:
            cp.wait_send()
        for cp in local:
            cp.wait()

    dma = pltpu.SemaphoreType.DMA
    return pl.pallas_call(
        body, name="share_halves", in_specs=[ANY] * n, out_specs=[ANY] * n,
        out_shape=[jax.ShapeDtypeStruct((2,) + r.shape, r.dtype) for r in rs],
        scratch_shapes=[dma((n,)), dma((n,)), dma((n,))],
        compiler_params=_params(),
    )(*rs)


def add_own_half(g, r, c_idx, *, name):
    _, _, h, cols = g.shape

    def body(c_ref, g_ref, r_ref, o_ref):
        o_ref[...] = (g_ref[0] + r_ref[...]).astype(o_ref.dtype)

    return pl.pallas_call(
        body, name=name,
        grid_spec=pltpu.PrefetchScalarGridSpec(
            num_scalar_prefetch=1, grid=(4,),
            in_specs=[pl.BlockSpec((1, 1, h, cols), lambda s, c: (s, c[0], 0, 0)),
                      pl.BlockSpec((1, h, cols), lambda s, c: (s, 0, 0))],
            out_specs=pl.BlockSpec((1, h, cols), lambda s, c: (s, 0, 0))),
        out_shape=jax.ShapeDtypeStruct(r.shape, BF16), compiler_params=_params(("parallel",)),
    )(c_idx, g, r)


def add_chip_parts(a, parts, chip_idx, *, name):
    _, h, cols = a.shape
    th = h // 2

    def body(s_ref, a_ref, p0_ref, p1_ref, p2_ref, o_ref):
        f = lambda r: r[0].astype(F32)
        o_ref[...] = ((f(a_ref) + f(p0_ref)) + f(p1_ref)) + f(p2_ref)

    part = lambda j: pl.BlockSpec((1, th, cols), lambda i, s, j=j: (j, i, 0))
    return pl.pallas_call(
        body, name=name,
        grid_spec=pltpu.PrefetchScalarGridSpec(
            num_scalar_prefetch=1, grid=(2,),
            in_specs=[pl.BlockSpec((1, th, cols), lambda i, s: (s[0], i, 0)), part(0), part(1), part(2)],
            out_specs=pl.BlockSpec((th, cols), lambda i, s: (i, 0))),
        out_shape=jax.ShapeDtypeStruct((h, cols), F32), compiler_params=_params(("parallel",)),
    )(chip_idx, a, parts, parts, parts)


WEIGHTS = ("mix_norm", "e_w_in", "e_conv_w", "e_conv_b", "e_dt_bias", "e_a_log", "e_d", "e_ssm_norm", "e_w_out",
           "o_w_in", "o_dw_w", "o_dw_b", "o_ln_g", "o_ln_b", "o_a_re", "o_a_im", "o_b_re", "o_b_im", "o_c_re",
           "o_c_im", "o_d", "o_log_step", "o_glu_w", "o_w_out", "ffn_norm", "ffn_w_up", "ffn_dw_w", "ffn_dw_b",
           "ffn_w_down", "final_norm")
BIG = (("e_w_in", 2), ("e_w_out", 1), ("o_w_in", 2), ("o_glu_w", 1), ("o_w_out", 1), ("ffn_w_up", 2), ("ffn_w_down", 1))
SMALL_SHARDED = (("e_conv_w", 2), ("o_dw_w", 2), ("o_dw_b", 1), ("o_ln_g", 1), ("o_ln_b", 1), ("o_d", 1), ("ffn_dw_w", 2))
REPLICATED = tuple(n for n in WEIGHTS if n not in dict(BIG + SMALL_SHARDED))
PACK_ROWS = 8


def _pack(arrays, dtype, row_unit=PACK_ROWS):
    flat = jnp.concatenate([a.astype(dtype).reshape(-1) for a in arrays])
    rows = -(-flat.size // (LANE * row_unit)) * row_unit
    return jnp.pad(flat, (0, rows * LANE - flat.size)).reshape(rows, LANE)


def _unpack(flat, shapes, lead=()):
    out, off = [], 0
    for shape in shapes:
        size = int(np.prod(shape))
        out.append(flat[..., off:off + size].reshape(lead + tuple(shape)))
        off += size
    return out


def _join_shards(parts, axis):
    return jnp.concatenate([parts[s] for s in range(4)], axis=axis)


def _split_shards(full, axis):
    return jnp.stack(jnp.split(full, 4, axis=axis))


def _rows2d(a):
    return a.reshape(-1, a.shape[-1])


def kernel(x, mix_norm, e_w_in, e_conv_w, e_conv_b, e_dt_bias, e_a_log, e_d, e_ssm_norm, e_w_out, o_w_in, o_dw_w, o_dw_b, o_ln_g, o_ln_b, o_a_re, o_a_im, o_b_re, o_b_im, o_c_re, o_c_im, o_d, o_log_step, o_glu_w, o_w_out, ffn_norm, ffn_w_up, ffn_dw_w, ffn_dw_b, ffn_w_down, final_norm, loss_target, m_mix_norm, m_e_w_in, m_e_conv_w, m_e_conv_b, m_e_dt_bias, m_e_a_log, m_e_d, m_e_ssm_norm, m_e_w_out, m_o_w_in, m_o_dw_w, m_o_dw_b, m_o_ln_g, m_o_ln_b, m_o_a_re, m_o_a_im, m_o_b_re, m_o_b_im, m_o_c_re, m_o_c_im, m_o_d, m_o_log_step, m_o_glu_w, m_o_w_out, m_ffn_norm, m_ffn_w_up, m_ffn_dw_w, m_ffn_dw_b, m_ffn_w_down, m_final_norm, v_mix_norm, v_e_w_in, v_e_conv_w, v_e_conv_b, v_e_dt_bias, v_e_a_log, v_e_d, v_e_ssm_norm, v_e_w_out, v_o_w_in, v_o_dw_w, v_o_dw_b, v_o_ln_g, v_o_ln_b, v_o_a_re, v_o_a_im, v_o_b_re, v_o_b_im, v_o_c_re, v_o_c_im, v_o_d, v_o_log_step, v_o_glu_w, v_o_w_out, v_ffn_norm, v_ffn_w_up, v_ffn_dw_w, v_ffn_dw_b, v_ffn_w_down, v_final_norm):
    given = dict(locals())
    chip = 2 * lax.axis_index("x") + lax.axis_index("y")
    core = lax.axis_index("c")

    gathered = gather_shards([_rows2d(given[n]).astype(BF16) for n, _ in BIG],
                             [_rows2d(given[n]) for n, _ in SMALL_SHARDED])
    w = {n: given[n] for n in REPLICATED}
    for (n, axis), parts in zip(BIG + SMALL_SHARDED, gathered):
        w[n] = _join_shards(parts.reshape((4,) + given[n].shape), axis)

    loss, dx, grads = local_step(x[0], loss_target[0], w)

    small_names = REPLICATED + tuple(n for n, _ in SMALL_SHARDED)
    small_sum = allreduce_small(_pack([grads[n] for n in small_names], F32))
    reduced = dict(zip(small_names, _unpack(small_sum.reshape(-1), [grads[n].shape for n in small_names])))
    for n, axis in SMALL_SHARDED:
        width = given[n].shape[axis]
        reduced[n] = lax.dynamic_slice_in_dim(reduced[n], chip * width, width, axis=axis)

    keys, parts = [], []
    for n, axis in BIG:
        shards = _split_shards(grads[n], axis)
        for layer in range(shards.shape[1]):
            keys.append((n, layer))
            parts.append(shards[:, layer].reshape(4, 2, shards.shape[2] // 2, shards.shape[3]))
    core_idx, chip_idx = core.reshape(1).astype(jnp.int32), chip.reshape(1).astype(jnp.int32)
    tags = [f"{n}{layer}" for n, layer in keys]
    core_sums = [add_own_half(g, r, core_idx, name="add_own_half_" + t) for g, r, t in zip(parts, exchange_halves(parts), tags)]
    mine = [add_chip_parts(a, p, chip_idx, name="add_chip_parts_" + t) for a, p, t in zip(core_sums, scatter_to_chips(core_sums), tags)]
    layers = {}
    for (n, layer), both in zip(keys, share_halves(mine)):
        layers.setdefault(n, []).append(both.reshape(given[n].shape[1:]))
    for n, _ in BIG:
        reduced[n] = jnp.stack(layers[n])

    delta, new_m, new_v = {}, {}, {}
    for n, _ in BIG:
        delta[n], new_m[n], new_v[n] = adamw(given[n], reduced[n], given["m_" + n], given["v_" + n], name="adamw_" + n)
    shapes = [given[n].shape for n in small_names]
    packed = [_pack([src[n] for n in small_names], F32)
              for src in (given, reduced, {n: given["m_" + n] for n in small_names}, {n: given["v_" + n] for n in small_names})]
    for dst, res in zip((delta, new_m, new_v), adamw(*packed, name="adamw_small")):
        dst.update(zip(small_names, _unpack(res.reshape(-1), shapes)))

    total = lax.psum(loss[0, 0], ("x", "y", "c"))
    return (total, dx[None], *[reduced[n] for n in WEIGHTS], *[delta[n] for n in WEIGHTS],
            *[new_m[n] for n in WEIGHTS], *[new_v[n] for n in WEIGHTS])
```

```python
import functools
import math
from typing import NamedTuple

import numpy as np
import jax
import jax.numpy as jnp
from jax import lax
from jax.experimental import pallas as pl
from jax.experimental.pallas import tpu as pltpu

F32 = jnp.float32
BF16 = jnp.bfloat16
HIGHEST = lax.Precision.HIGHEST
MESH = pl.DeviceIdType.MESH

D_MODEL = 1024
EPS = 1e-6
RET_HEADS, RET_DK, RET_DV, CHUNK = 4, 128, 256, 128
ROPE_BASE = 10000.0
SSM_HEADS, SSM_P, SSM_N, SSM_GROUPS = 16, 64, 128, 2
SSM_DINNER = SSM_HEADS * SSM_P
EVEN_IN, EVEN_IN_PAD = 5648, 5760
S5_GROUPS, S5_GROUP, S5_STATE = 32, 16, 64
S5_LANES = S5_GROUPS * S5_STATE
SCAN_SEG = 8
D_FF = 2816
ADAM_LR, ADAM_B1, ADAM_B2, ADAM_EPS, ADAM_WD, ADAM_STEP = 0.001, 0.9, 0.999, 1e-08, 0.01, 10

LANE = 128
VMEM_LIMIT = 56 * 1024 * 1024


def _params(sem=None, **kw):
    return pltpu.CompilerParams(dimension_semantics=sem, vmem_limit_bytes=VMEM_LIMIT, **kw)


def _tile(n, target, unit=LANE):
    if n <= target:
        return n
    t = (target // unit) * unit
    while t >= unit:
        if n % t == 0:
            return t
        t -= unit
    return n


def _silu(x):
    return x * jax.nn.sigmoid(x)


def _mm(a, b):
    return jnp.dot(a.astype(BF16), b.astype(BF16), preferred_element_type=F32)


def _mm_nt(a, b):
    return lax.dot_general(a.astype(BF16), b.astype(BF16), (((1,), (1,)), ((), ())), preferred_element_type=F32)


def _mm_tn(a, b):
    return lax.dot_general(a.astype(BF16), b.astype(BF16), (((0,), (0,)), ((), ())), preferred_element_type=F32)


def _dot_hi(a, b):
    return jnp.dot(a, b, precision=HIGHEST, preferred_element_type=F32)


def _dot_hi_tn(a, b):
    return lax.dot_general(a, b, (((0,), (0,)), ((), ())), precision=HIGHEST, preferred_element_type=F32)


MATMUL_VMEM = 44 * 1024 * 1024


def matmul(a, b, *, ta=False, tb=False, res=None, out_dtype=F32, name):
    m, k = (a.shape[1], a.shape[0]) if ta else a.shape
    n = b.shape[0] if tb else b.shape[1]
    assert (b.shape[1] if tb else b.shape[0]) == k, (a.shape, b.shape, ta, tb)
    tm = _tile(m, 1536)
    tn = _tile(n, 640)
    if tn < 384:
        tn = _tile(n, 1536)
    res_bytes = 0 if res is None else res.dtype.itemsize

    def vmem(tm, tn):
        return 2 * (tm * k * a.dtype.itemsize + tn * k * b.dtype.itemsize + tm * tn * (jnp.dtype(out_dtype).itemsize + res_bytes))

    while vmem(tm, tn) > MATMUL_VMEM and tm % (2 * LANE) == 0:
        tm //= 2
    assert vmem(tm, tn) <= MATMUL_VMEM, (name, tm, tn, k)
    a_spec = pl.BlockSpec((k, tm), lambda i, j: (0, i)) if ta else pl.BlockSpec((tm, k), lambda i, j: (i, 0))
    b_spec = pl.BlockSpec((tn, k), lambda i, j: (j, 0)) if tb else pl.BlockSpec((k, tn), lambda i, j: (0, j))
    o_spec = pl.BlockSpec((tm, tn), lambda i, j: (i, j))
    dims = (((0 if ta else 1,), (1 if tb else 0,)), ((), ()))
    has_res = res is not None

    def body(a_ref, b_ref, *rest):
        o_ref = rest[-1]
        out = lax.dot_general(a_ref[...].astype(BF16), b_ref[...].astype(BF16), dims, preferred_element_type=F32)
        if has_res:
            out = out + rest[0][...].astype(F32)
        o_ref[...] = out.astype(o_ref.dtype)

    ins = [a, b] + ([res] if has_res else [])
    specs = [a_spec, b_spec] + ([o_spec] if has_res else [])
    return pl.pallas_call(
        body, name=name, grid=(m // tm, n // tn), in_specs=specs, out_specs=o_spec,
        out_shape=jax.ShapeDtypeStruct((m, n), out_dtype), compiler_params=_params(("parallel", "parallel")),
    )(*ins)


class Cols(NamedTuple):
    arr: jax.Array
    w: int
    j: int


def _cols(a):
    return a if isinstance(a, Cols) else Cols(a, a.shape[1], 0)


def _row_spec(c, tl):
    return pl.BlockSpec((tl, c.w), lambda i, j=c.j: (i, j))


def _whole_spec(p):
    return pl.BlockSpec(p.shape, lambda i, nd=p.ndim: (0,) * nd)


def rowwise_fwd(fn, rows, aux, pars, consts, outs, *, name, tl):
    rows = [_cols(r) for r in rows + aux]
    whole = list(pars) + list(consts)
    n_rows = len(rows)
    n_whole = len(whole)
    length = rows[0].arr.shape[0]
    tl = min(tl, length)

    def body(*refs):
        vals = [r[...].astype(F32) for r in refs[:n_rows]] + [r[...] for r in refs[n_rows:n_rows + n_whole]]
        res = fn(*vals)
        for o_ref, v in zip(refs[n_rows + n_whole:], res, strict=True):
            o_ref[...] = v.astype(o_ref.dtype)

    return pl.pallas_call(
        body, name=name, grid=(length // tl,),
        in_specs=[_row_spec(r, tl) for r in rows] + [_whole_spec(p) for p in whole],
        out_specs=[pl.BlockSpec((tl, w), lambda i: (i, 0)) for w, _ in outs],
        out_shape=[jax.ShapeDtypeStruct((length, w), dt) for w, dt in outs],
        compiler_params=_params(("parallel",)),
    )(*[r.arr for r in rows], *whole)


def rowwise_bwd(fn, rows, aux, pars, consts, cots, drow_dtypes, *, name, tl, add=None, merge=False):
    rows = [_cols(r) for r in rows]
    aux = [_cols(r) for r in aux]
    cots = [_cols(r) for r in cots]
    n_r, n_a, n_p, n_c, n_t = len(rows), len(aux), len(pars), len(consts), len(cots)
    length = rows[0].arr.shape[0]
    tl = min(tl, length)
    has_add = add is not None
    widths = [r.w for r in rows]

    def body(*refs):
        pos = 0
        r_vals = [r[...].astype(F32) for r in refs[pos:pos + n_r]]; pos += n_r
        a_vals = [r[...].astype(F32) for r in refs[pos:pos + n_a]]; pos += n_a
        p_vals = [r[...].astype(F32) for r in refs[pos:pos + n_p]]; pos += n_p
        c_vals = [r[...] for r in refs[pos:pos + n_c]]; pos += n_c
        t_vals = [r[...].astype(F32) for r in refs[pos:pos + n_t]]; pos += n_t
        add_val = None
        if has_add:
            add_val = refs[pos][...].astype(F32); pos += 1
        n_dr = 1 if merge else n_r
        dr_refs = refs[pos:pos + n_dr]; pos += n_dr
        dp_refs = refs[pos:pos + n_p]

        def f(*rp):
            return fn(*rp[:n_r], *a_vals, *rp[n_r:], *c_vals)

        _, vjp = jax.vjp(f, *r_vals, *p_vals)
        grads = vjp(tuple(t_vals))
        drows = list(grads[:n_r])
        if has_add:
            drows[0] = drows[0] + add_val
        if merge:
            off = 0
            for w, d in zip(widths, drows):
                dr_refs[0][:, off:off + w] = d.astype(dr_refs[0].dtype)
                off += w
        else:
            for r, d in zip(dr_refs, drows):
                r[...] = d.astype(r.dtype)
        i = pl.program_id(0)
        for r, d in zip(dp_refs, grads[n_r:]):
            @pl.when(i == 0)
            def _(r=r, d=d):
                r[...] = d

            @pl.when(i > 0)
            def _(r=r, d=d):
                r[...] += d

    if merge:
        dr_specs = [pl.BlockSpec((tl, sum(widths)), lambda i: (i, 0))]
        dr_shapes = [jax.ShapeDtypeStruct((length, sum(widths)), drow_dtypes[0])]
    else:
        dr_specs = [pl.BlockSpec((tl, w), lambda i: (i, 0)) for w in widths]
        dr_shapes = [jax.ShapeDtypeStruct((length, w), dt) for w, dt in zip(widths, drow_dtypes)]
    ins = [r.arr for r in rows + aux] + list(pars) + list(consts) + [r.arr for r in cots] + ([add] if has_add else [])
    specs = ([_row_spec(r, tl) for r in rows + aux] + [_whole_spec(p) for p in list(pars) + list(consts)]
             + [_row_spec(r, tl) for r in cots] + ([pl.BlockSpec((tl, add.shape[1]), lambda i: (i, 0))] if has_add else []))
    return pl.pallas_call(
        body, name=name, grid=(length // tl,), in_specs=specs,
        out_specs=dr_specs + [_whole_spec(p) for p in pars],
        out_shape=dr_shapes + [jax.ShapeDtypeStruct(p.shape, F32) for p in pars],
        compiler_params=_params(("arbitrary",)),
    )(*ins)


def whole_fwd(fn, ins, out_shapes, *, name):
    n_in = len(ins)

    def body(*refs):
        res = fn(*[r[...] for r in refs[:n_in]])
        for o_ref, v in zip(refs[n_in:], res, strict=True):
            o_ref[...] = v

    return pl.pallas_call(body, name=name, out_shape=[jax.ShapeDtypeStruct(s, F32) for s in out_shapes],
                          compiler_params=_params())(*ins)


def whole_bwd(fn, ins, n_diff, cots, *, name):
    n_in, n_t = len(ins), len(cots)

    def body(*refs):
        vals = [r[...] for r in refs[:n_in]]
        t_vals = [r[...] for r in refs[n_in:n_in + n_t]]
        _, vjp = jax.vjp(lambda *d: fn(*d, *vals[n_diff:]), *vals[:n_diff])
        for o_ref, g in zip(refs[n_in + n_t:], vjp(tuple(t_vals)), strict=True):
            o_ref[...] = g

    return pl.pallas_call(body, name=name, out_shape=[jax.ShapeDtypeStruct(a.shape, F32) for a in ins[:n_diff]],
                          compiler_params=_params())(*ins, *cots)


CONV_ROWS = 256


def _conv_geometry(x, w, cw, off):
    width = w.shape[1]
    x = Cols(x, width, 0)
    length = x.arr.shape[0]
    taps = w.shape[0]
    pad = -(-(taps - 1) // 8) * 8
    assert off % cw == 0 and width % cw == 0, (off, width, cw)
    return x, length, taps, pad, off // cw


def _conv_taps(xp_ref, w_ref, base, taps, pad, init):
    acc = init
    for k in range(taps):
        acc = acc + w_ref[k:k + 1, :] * xp_ref[pl.ds(base + pad - (taps - 1) + k, init.shape[0]), :]
    return acc


def conv_fwd(x, w, b, *, act, name, off=0, cw=LANE, out_dtype=F32):
    x, length, taps, pad, jb = _conv_geometry(x, w, cw, off)
    rc = min(CONV_ROWS, length)

    def body(x_ref, w_ref, b_ref, o_ref, xp_ref):
        xp_ref[0:pad, :] = jnp.zeros((pad, cw), F32)
        xp_ref[pad:pad + length, :] = x_ref[...].astype(F32)

        def chunk(r, carry):
            base = pl.multiple_of(r * rc, rc)
            acc = _conv_taps(xp_ref, w_ref, base, taps, pad, jnp.broadcast_to(b_ref[...], (rc, cw)))
            if act:
                acc = _silu(acc)
            o_ref[pl.ds(base, rc), :] = acc.astype(o_ref.dtype)
            return carry

        lax.fori_loop(0, length // rc, chunk, 0)

    return pl.pallas_call(
        body, name=name, grid=(x.w // cw,),
        in_specs=[pl.BlockSpec((length, cw), lambda j: (0, jb + j)), pl.BlockSpec((taps, cw), lambda j: (0, j)),
                  pl.BlockSpec((1, cw), lambda j: (0, j))],
        out_specs=pl.BlockSpec((length, cw), lambda j: (0, j)),
        out_shape=jax.ShapeDtypeStruct((length, x.w), out_dtype),
        scratch_shapes=[pltpu.VMEM((pad + length, cw), F32)],
        compiler_params=_params(("parallel",)),
    )(x.arr, w, b)


def conv_bwd(x, w, b, dy, *, act, name, off=0, cw=LANE, dx_dtype=F32):
    x, length, taps, pad, jb = _conv_geometry(x, w, cw, off)
    rc = min(CONV_ROWS, length)

    def body(x_ref, w_ref, b_ref, dy_ref, dx_ref, dw_ref, db_ref, xp_ref, gp_ref):
        xp_ref[0:pad, :] = jnp.zeros((pad, cw), F32)
        xp_ref[pad:pad + length, :] = x_ref[...].astype(F32)
        gp_ref[length:length + pad, :] = jnp.zeros((pad, cw), F32)
        if act:
            def pre_chunk(r, carry):
                base = pl.multiple_of(r * rc, rc)
                pre = _conv_taps(xp_ref, w_ref, base, taps, pad, jnp.broadcast_to(b_ref[...], (rc, cw)))
                sig = jax.nn.sigmoid(pre)
                gp_ref[pl.ds(base, rc), :] = dy_ref[pl.ds(base, rc), :].astype(F32) * (sig * (1.0 + pre * (1.0 - sig)))
                return carry

            lax.fori_loop(0, length // rc, pre_chunk, 0)
        else:
            gp_ref[0:length, :] = dy_ref[...].astype(F32)
        dw_ref[...] = jnp.zeros((taps, cw), F32)
        db_ref[...] = jnp.zeros((1, cw), F32)

        def chunk(r, carry):
            base = pl.multiple_of(r * rc, rc)
            acc = jnp.zeros((rc, cw), F32)
            g = gp_ref[pl.ds(base, rc), :]
            for k in range(taps):
                acc = acc + w_ref[k:k + 1, :] * gp_ref[pl.ds(base + (taps - 1) - k, rc), :]
                xs = xp_ref[pl.ds(base + pad - (taps - 1) + k, rc), :]
                dw_ref[k:k + 1, :] += jnp.sum(g * xs, axis=0, keepdims=True)
            db_ref[...] += jnp.sum(g, axis=0, keepdims=True)
            dx_ref[pl.ds(base, rc), :] = acc.astype(dx_ref.dtype)
            return carry

        lax.fori_loop(0, length // rc, chunk, 0)

    dy = _cols(dy)
    assert dy.j == 0 and dy.w == x.w
    return pl.pallas_call(
        body, name=name, grid=(x.w // cw,),
        in_specs=[pl.BlockSpec((length, cw), lambda j: (0, jb + j)), pl.BlockSpec((taps, cw), lambda j: (0, j)),
                  pl.BlockSpec((1, cw), lambda j: (0, j)), pl.BlockSpec((length, cw), lambda j: (0, j))],
        out_specs=[pl.BlockSpec((length, cw), lambda j: (0, j)), pl.BlockSpec((taps, cw), lambda j: (0, j)),
                   pl.BlockSpec((1, cw), lambda j: (0, j))],
        out_shape=[jax.ShapeDtypeStruct((length, x.w), dx_dtype), jax.ShapeDtypeStruct((taps, x.w), F32),
                   jax.ShapeDtypeStruct((1, x.w), F32)],
        scratch_shapes=[pltpu.VMEM((pad + length, cw), F32), pltpu.VMEM((length + pad, cw), F32)],
        compiler_params=_params(("parallel",)),
    )(x.arr, w, b, dy.arr)


def _conv_transpose(xp_ref, gp_ref, w_ref, dx_ref, dw_ref, db_ref, length, taps, pad, rc):
    dw_ref[...] = jnp.zeros(dw_ref.shape, F32)
    db_ref[...] = jnp.zeros(db_ref.shape, F32)

    def chunk(r, carry):
        base = pl.multiple_of(r * rc, rc)
        acc = jnp.zeros((rc, LANE), F32)
        g = gp_ref[pl.ds(base, rc), :]
        for k in range(taps):
            acc = acc + w_ref[k:k + 1, :] * gp_ref[pl.ds(base + (taps - 1) - k, rc), :]
            xs = xp_ref[pl.ds(base + pad - (taps - 1) + k, rc), :]
            dw_ref[k:k + 1, :] += jnp.sum(g * xs, axis=0, keepdims=True)
        db_ref[...] += jnp.sum(g, axis=0, keepdims=True)
        dx_ref[pl.ds(base, rc), :] = acc.astype(dx_ref.dtype)
        return carry

    lax.fori_loop(0, length // rc, chunk, 0)


def ffn_conv_act(a, w, b, *, name):
    length, width = a.shape
    nb = width // 2 // LANE
    taps = w.shape[0]
    pad = -(-(taps - 1) // 8) * 8
    rc = min(CONV_ROWS, length)

    def body(ag_ref, au_ref, wg_ref, wu_ref, bg_ref, bu_ref, o_ref, xg_ref, xu_ref):
        for xp_ref, src in ((xg_ref, ag_ref), (xu_ref, au_ref)):
            xp_ref[0:pad, :] = jnp.zeros((pad, LANE), F32)
            xp_ref[pad:pad + length, :] = src[...]

        def chunk(r, carry):
            base = pl.multiple_of(r * rc, rc)
            gate = _conv_taps(xg_ref, wg_ref, base, taps, pad, jnp.broadcast_to(bg_ref[...], (rc, LANE)))
            up = _conv_taps(xu_ref, wu_ref, base, taps, pad, jnp.broadcast_to(bu_ref[...], (rc, LANE)))
            o_ref[pl.ds(base, rc), :] = (_silu(gate) * up).astype(o_ref.dtype)
            return carry

        lax.fori_loop(0, length // rc, chunk, 0)

    blk = lambda rows, up: pl.BlockSpec((rows, LANE), (lambda j: (0, nb + j)) if up else (lambda j: (0, j)))
    return pl.pallas_call(
        body, name=name, grid=(nb,),
        in_specs=[blk(length, False), blk(length, True), blk(taps, False), blk(taps, True), blk(1, False), blk(1, True)],
        out_specs=blk(length, False), out_shape=jax.ShapeDtypeStruct((length, width // 2), BF16),
        scratch_shapes=[pltpu.VMEM((pad + length, LANE), F32), pltpu.VMEM((pad + length, LANE), F32)],
        compiler_params=_params(("parallel",)),
    )(a, a, w, w, b, b)


def ffn_conv_act_bwd(a, w, b, dact, *, name):
    length, width = a.shape
    nb = width // 2 // LANE
    taps = w.shape[0]
    pad = -(-(taps - 1) // 8) * 8
    rc = min(CONV_ROWS, length)

    def body(ag_ref, au_ref, wg_ref, wu_ref, bg_ref, bu_ref, dy_ref, da_ref, dw_ref, db_ref, xg_ref, xu_ref, gp_ref):
        for xp_ref, src in ((xg_ref, ag_ref), (xu_ref, au_ref)):
            xp_ref[0:pad, :] = jnp.zeros((pad, LANE), F32)
            xp_ref[pad:pad + length, :] = src[...]
        gp_ref[length:length + pad, :] = jnp.zeros((pad, LANE), F32)
        is_gate = pl.program_id(0) == 0

        def pre_chunk(r, carry):
            base = pl.multiple_of(r * rc, rc)
            gate = _conv_taps(xg_ref, wg_ref, base, taps, pad, jnp.broadcast_to(bg_ref[...], (rc, LANE)))
            up = _conv_taps(xu_ref, wu_ref, base, taps, pad, jnp.broadcast_to(bu_ref[...], (rc, LANE)))
            sig = jax.nn.sigmoid(gate)
            d_gate = up * (sig * (1.0 + gate * (1.0 - sig)))
            gp_ref[pl.ds(base, rc), :] = dy_ref[pl.ds(base, rc), :] * jnp.where(is_gate, d_gate, gate * sig)
            return carry

        lax.fori_loop(0, length // rc, pre_chunk, 0)

        @pl.when(is_gate)
        def _():
            _conv_transpose(xg_ref, gp_ref, wg_ref, da_ref, dw_ref, db_ref, length, taps, pad, rc)

        @pl.when(jnp.logical_not(is_gate))
        def _():
            _conv_transpose(xu_ref, gp_ref, wu_ref, da_ref, dw_ref, db_ref, length, taps, pad, rc)

    blk = lambda rows, up: pl.BlockSpec((rows, LANE), (lambda h, j: (0, nb + j)) if up else (lambda h, j: (0, j)))
    out = lambda rows: pl.BlockSpec((rows, LANE), lambda h, j: (0, h * nb + j))
    return pl.pallas_call(
        body, name=name, grid=(2, nb),
        in_specs=[blk(length, False), blk(length, True), blk(taps, False), blk(taps, True), blk(1, False), blk(1, True),
                  blk(length, False)],
        out_specs=[out(length), out(taps), out(1)],
        out_shape=[jax.ShapeDtypeStruct((length, width), BF16), jax.ShapeDtypeStruct((taps, width), F32),
                   jax.ShapeDtypeStruct((1, width), F32)],
        scratch_shapes=[pltpu.VMEM((pad + length, LANE), F32), pltpu.VMEM((pad + length, LANE), F32),
                        pltpu.VMEM((length + pad, LANE), F32)],
        compiler_params=_params(("parallel", "parallel")),
    )(a, a, w, w, b, b, dact)


def _retention_consts():
    h = np.arange(RET_HEADS, dtype=np.float32)
    log_g = np.log1p(-(2.0 ** (-5.0 - h))).astype(np.float32)
    idx = np.arange(CHUNK, dtype=np.float32)
    diff = idx[:, None] - idx[None, :]
    intra = np.where(diff[None] >= 0, np.exp(np.maximum(diff, 0.0)[None] * log_g[:, None, None]), 0.0)
    zeta = np.exp((CHUNK - 1 - idx)[None, :] * log_g[:, None])
    xi = np.exp((idx + 1)[None, :] * log_g[:, None])
    decay = np.exp(CHUNK * log_g)
    zeta = np.broadcast_to(zeta[:, :, None], (RET_HEADS, CHUNK, RET_DK))
    xi = np.broadcast_to(xi[:, :, None], (RET_HEADS, CHUNK, RET_DV))
    return (jnp.asarray(intra, F32), jnp.asarray(zeta, F32), jnp.asarray(xi, F32), [float(d) for d in decay])


def _rotary_tables(length):
    inv = ROPE_BASE ** (-jnp.arange(0, RET_DK, 2, dtype=F32) / RET_DK)
    ang = jnp.arange(length).astype(F32)[:, None] * inv[None, :]
    cos, sin = jnp.cos(ang), jnp.sin(ang)
    return jnp.concatenate([cos, cos], axis=1), jnp.concatenate([-sin, sin], axis=1)


def _rot(x, cos2, sin2):
    return x * cos2 + pltpu.roll(x, RET_DK // 2, 1) * sin2


def _rot_t(y, cos2, sin2):
    return y * cos2 + pltpu.roll(y * sin2, RET_DK // 2, 1)


def _head_decay(h, decays):
    d = jnp.float32(decays[-1])
    for i in range(len(decays) - 2, -1, -1):
        d = jnp.where(h == i, jnp.float32(decays[i]), d)
    return d


def _ret_chunk(q, k, v, g, state, intra, zeta, xi, decay):
    s = _mm_nt(q, k) * intra
    kv = _mm_tn(k * zeta, v)
    o = _mm(s, v) + _mm(q, state) * xi
    oc = o - jnp.mean(o, axis=-1, keepdims=True)
    r = oc * lax.rsqrt(jnp.mean(oc * oc, axis=-1, keepdims=True) + EPS)
    return _silu(g) * r, state * decay + kv


def _ret_specs(rev, nc):
    def cidx(c):
        return nc - 1 - c if rev else c
    return [
        pl.BlockSpec((CHUNK, RET_DK), lambda h, c: (cidx(c), h)),
        pl.BlockSpec((CHUNK, RET_DK), lambda h, c: (cidx(c), RET_HEADS + h)),
        pl.BlockSpec((CHUNK, RET_DV), lambda h, c: (cidx(c), 4 + h)),
        pl.BlockSpec((CHUNK, RET_DV), lambda h, c: (cidx(c), 8 + h)),
        pl.BlockSpec((CHUNK, RET_DK), lambda h, c: (cidx(c), 0)),
        pl.BlockSpec((CHUNK, RET_DK), lambda h, c: (cidx(c), 0)),
        pl.BlockSpec((1, CHUNK, CHUNK), lambda h, c: (h, 0, 0)),
        pl.BlockSpec((1, CHUNK, RET_DK), lambda h, c: (h, 0, 0)),
        pl.BlockSpec((1, CHUNK, RET_DV), lambda h, c: (h, 0, 0)),
    ], cidx


def retention_fwd(proj, cos2, sin2):
    length = proj.shape[0]
    nc = length // CHUNK
    intra, zeta, xi, decays = _retention_consts()
    specs, _ = _ret_specs(False, nc)
    scale = RET_DK ** -0.5

    def body(q_ref, k_ref, v_ref, g_ref, cos_ref, sin_ref, intra_ref, zeta_ref, xi_ref, y_ref, st_ref, state):
        h, c = pl.program_id(0), pl.program_id(1)

        @pl.when(c == 0)
        def _():
            state[...] = jnp.zeros_like(state)

        q = _rot(q_ref[...], cos_ref[...], sin_ref[...])
        k = _rot(k_ref[...], cos_ref[...], sin_ref[...]) * scale
        st_ref[0, 0] = state[...]
        y, new_state = _ret_chunk(q, k, v_ref[...], g_ref[...], state[...], intra_ref[0], zeta_ref[0], xi_ref[0],
                                  _head_decay(h, decays))
        y_ref[...] = y.astype(y_ref.dtype)
        state[...] = new_state

    return pl.pallas_call(
        body, name="retention_fwd", grid=(RET_HEADS, nc), in_specs=specs,
        out_specs=[pl.BlockSpec((CHUNK, RET_DV), lambda h, c: (c, h)),
                   pl.BlockSpec((1, 1, RET_DK, RET_DV), lambda h, c: (h, c, 0, 0))],
        out_shape=[jax.ShapeDtypeStruct((length, RET_HEADS * RET_DV), BF16),
                   jax.ShapeDtypeStruct((RET_HEADS, nc, RET_DK, RET_DV), F32)],
        scratch_shapes=[pltpu.VMEM((RET_DK, RET_DV), F32)],
        compiler_params=_params(("parallel", "arbitrary")),
    )(proj, proj, proj, proj, cos2, sin2, intra, zeta, xi)


def retention_bwd(proj, cos2, sin2, states, dmix):
    length = proj.shape[0]
    nc = length // CHUNK
    intra, zeta, xi, decays = _retention_consts()
    specs, cidx = _ret_specs(True, nc)
    scale = RET_DK ** -0.5

    def body(q_ref, k_ref, v_ref, g_ref, cos_ref, sin_ref, intra_ref, zeta_ref, xi_ref, st_ref, dy_ref,
             dq_ref, dk_ref, dv_ref, dg_ref, dstate):
        h, c = pl.program_id(0), pl.program_id(1)

        @pl.when(c == 0)
        def _():
            dstate[...] = jnp.zeros_like(dstate)

        cos2v, sin2v = cos_ref[...], sin_ref[...]
        q = _rot(q_ref[...], cos2v, sin2v)
        k = _rot(k_ref[...], cos2v, sin2v) * scale
        decay = _head_decay(h, decays)
        intra_v, zeta_v, xi_v = intra_ref[0], zeta_ref[0], xi_ref[0]
        _, vjp = jax.vjp(lambda q, k, v, g, s: _ret_chunk(q, k, v, g, s, intra_v, zeta_v, xi_v, decay),
                         q, k, v_ref[...], g_ref[...], st_ref[0, 0])
        dq, dk, dv, dg, ds = vjp((dy_ref[...].astype(F32), dstate[...]))
        dq_ref[...] = _rot_t(dq, cos2v, sin2v).astype(dq_ref.dtype)
        dk_ref[...] = _rot_t(dk * scale, cos2v, sin2v).astype(dk_ref.dtype)
        dv_ref[...] = dv.astype(dv_ref.dtype)
        dg_ref[...] = dg.astype(dg_ref.dtype)
        dstate[...] = ds

    specs = specs + [pl.BlockSpec((1, 1, RET_DK, RET_DV), lambda h, c: (h, cidx(c), 0, 0)),
                     pl.BlockSpec((CHUNK, RET_DV), lambda h, c: (cidx(c), h))]
    return pl.pallas_call(
        body, name="retention_bwd", grid=(RET_HEADS, nc), in_specs=specs,
        out_specs=[pl.BlockSpec((CHUNK, RET_DK), lambda h, c: (cidx(c), h)),
                   pl.BlockSpec((CHUNK, RET_DK), lambda h, c: (cidx(c), h)),
                   pl.BlockSpec((CHUNK, RET_DV), lambda h, c: (cidx(c), h)),
                   pl.BlockSpec((CHUNK, RET_DV), lambda h, c: (cidx(c), h))],
        out_shape=[jax.ShapeDtypeStruct((length, RET_HEADS * RET_DK), BF16),
                   jax.ShapeDtypeStruct((length, RET_HEADS * RET_DK), BF16),
                   jax.ShapeDtypeStruct((length, RET_HEADS * RET_DV), BF16),
                   jax.ShapeDtypeStruct((length, RET_HEADS * RET_DV), BF16)],
        scratch_shapes=[pltpu.VMEM((RET_DK, RET_DV), F32)],
        compiler_params=_params(("parallel", "arbitrary")),
    )(proj, proj, proj, proj, cos2, sin2, intra, zeta, xi, states, dmix)


def _ssd_consts():
    tri = np.tril(np.ones((CHUNK, CHUNK), np.float32))
    expand = np.zeros((LANE, SSM_DINNER), np.float32)
    for h in range(SSM_HEADS):
        expand[h, h * SSM_P:(h + 1) * SSM_P] = 1.0
    return jnp.asarray(tri), jnp.asarray(tri.T.copy()), jnp.asarray(expand)


def _ssd_chunk(xs, bm, cm, dtr, z, state, dt_bias, a_log, d_skip, norm_w, tri, tri_t, expand):
    gw = SSM_DINNER // SSM_GROUPS
    dt = jax.nn.softplus(dtr + dt_bias)
    da = dt * (-jnp.exp(a_log))
    acs = _dot_hi(tri, da)
    acs_t = _dot_hi_tn(da, tri_t)
    dt_x = _dot_hi(dt, expand)
    da_x = _dot_hi(da, expand)
    acs_x = _dot_hi(tri, da_x)
    tot_x = jnp.sum(da_x, axis=0, keepdims=True)
    x_dt = xs * dt_x
    x_dec = x_dt * jnp.exp(tot_x - acs_x)
    e_acs = jnp.exp(acs_x)
    e_tot = jnp.exp(tot_x)
    lane = lax.broadcasted_iota(jnp.int32, (CHUNK, LANE), 1)
    sub = lax.broadcasted_iota(jnp.int32, (CHUNK, LANE), 0)
    causal = sub >= lane
    ys, new_states = [], []
    for g in range(SSM_GROUPS):
        bg = bm[:, g * SSM_N:(g + 1) * SSM_N]
        cg = cm[:, g * SSM_N:(g + 1) * SSM_N]
        sg = state[:, g * gw:(g + 1) * gw]
        cb = _mm_nt(cg, bg)
        y_off = _mm(cg, sg) * e_acs[:, g * gw:(g + 1) * gw]
        new_states.append(sg * e_tot[:, g * gw:(g + 1) * gw] + _mm_tn(bg, x_dec[:, g * gw:(g + 1) * gw]))
        pairs = []
        for p in range(gw // LANE):
            hp = g * (gw // LANE) + p
            xp = x_dt[:, hp * LANE:(hp + 1) * LANE]
            halves = []
            for head in (2 * hp, 2 * hp + 1):
                col = jnp.sum(jnp.where(lane == head, acs, 0.0), axis=1, keepdims=True)
                row = jnp.sum(jnp.where(sub == head, acs_t, 0.0), axis=0, keepdims=True)
                decay = jnp.exp(jnp.where(causal, col - row, -1e30))
                halves.append(_mm(cb * decay, xp))
            pairs.append(jnp.where(lane < SSM_P, halves[0], halves[1]))
        ys.append(jnp.concatenate(pairs, axis=1) + y_off)
    d_x = jnp.mean(_dot_hi(jnp.broadcast_to(d_skip, (8, LANE)), expand), axis=0, keepdims=True)
    y = (jnp.concatenate(ys, axis=1) + d_x * xs) * _silu(z)
    normed = []
    for g in range(SSM_GROUPS):
        yg = y[:, g * gw:(g + 1) * gw]
        normed.append(yg * lax.rsqrt(jnp.mean(yg * yg, axis=-1, keepdims=True) + EPS))
    return jnp.concatenate(normed, axis=1) * norm_w, jnp.concatenate(new_states, axis=1)


XBC = SSM_DINNER + 2 * SSM_GROUPS * SSM_N


def _ssd_specs(rev, nc):
    def cidx(c):
        return nc - 1 - c if rev else c
    row = lambda w, j: pl.BlockSpec((CHUNK, w), lambda c: (cidx(c), j))
    whole = lambda shape: pl.BlockSpec(shape, lambda c: (0,) * len(shape))
    return [row(XBC, 0), row(LANE, 5632 // LANE), row(SSM_DINNER, 3),
            whole((1, LANE)), whole((1, LANE)), whole((1, LANE)), whole((1, SSM_DINNER)),
            whole((CHUNK, CHUNK)), whole((CHUNK, CHUNK)), whole((LANE, SSM_DINNER))], cidx


def ssd_fwd(xbc, proj, dt_bias, a_log, d_skip, norm_w):
    length = proj.shape[0]
    nc = length // CHUNK
    tri, tri_t, expand = _ssd_consts()
    specs, _ = _ssd_specs(False, nc)

    def body(xbc_ref, dt_ref, z_ref, dtb_ref, alog_ref, d_ref, nw_ref, tri_ref, trit_ref, e_ref, y_ref, st_ref, state):
        @pl.when(pl.program_id(0) == 0)
        def _():
            state[...] = jnp.zeros_like(state)

        st_ref[0] = state[...]
        y, new_state = _ssd_chunk(
            xbc_ref[:, 0:SSM_DINNER], xbc_ref[:, SSM_DINNER:SSM_DINNER + 256], xbc_ref[:, SSM_DINNER + 256:XBC],
            dt_ref[...], z_ref[...], state[...], dtb_ref[...], alog_ref[...], d_ref[...], nw_ref[...],
            tri_ref[...], trit_ref[...], e_ref[...])
        y_ref[...] = y.astype(y_ref.dtype)
        state[...] = new_state

    return pl.pallas_call(
        body, name="ssd_fwd", grid=(nc,), in_specs=specs,
        out_specs=[pl.BlockSpec((CHUNK, SSM_DINNER), lambda c: (c, 0)),
                   pl.BlockSpec((1, SSM_N, SSM_DINNER), lambda c: (c, 0, 0))],
        out_shape=[jax.ShapeDtypeStruct((length, SSM_DINNER), BF16),
                   jax.ShapeDtypeStruct((nc, SSM_N, SSM_DINNER), F32)],
        scratch_shapes=[pltpu.VMEM((SSM_N, SSM_DINNER), F32)],
        compiler_params=_params(("arbitrary",)),
    )(xbc, proj, proj, dt_bias, a_log, d_skip, norm_w, tri, tri_t, expand)


def ssd_bwd(xbc, proj, dt_bias, a_log, d_skip, norm_w, states, dmix):
    length = proj.shape[0]
    nc = length // CHUNK
    tri, tri_t, expand = _ssd_consts()
    specs, cidx = _ssd_specs(True, nc)

    def body(xbc_ref, dt_ref, z_ref, dtb_ref, alog_ref, d_ref, nw_ref, tri_ref, trit_ref, e_ref, st_ref, dy_ref,
             dxbc_ref, ddt_ref, dz_ref, ddtb_ref, dalog_ref, dd_ref, dnw_ref, dstate):
        c = pl.program_id(0)

        @pl.when(c == 0)
        def _():
            dstate[...] = jnp.zeros_like(dstate)

        tri_v, trit_v, e_v = tri_ref[...], trit_ref[...], e_ref[...]
        _, vjp = jax.vjp(
            lambda *a: _ssd_chunk(*a, tri_v, trit_v, e_v),
            xbc_ref[:, 0:SSM_DINNER], xbc_ref[:, SSM_DINNER:SSM_DINNER + 256], xbc_ref[:, SSM_DINNER + 256:XBC],
            dt_ref[...], z_ref[...], st_ref[0], dtb_ref[...], alog_ref[...], d_ref[...], nw_ref[...])
        dxs, dbm, dcm, ddt, dz, ds, ddtb, dalog, dd, dnw = vjp((dy_ref[...].astype(F32), dstate[...]))
        dxbc_ref[:, 0:SSM_DINNER] = dxs
        dxbc_ref[:, SSM_DINNER:SSM_DINNER + 256] = dbm
        dxbc_ref[:, SSM_DINNER + 256:XBC] = dcm
        ddt_ref[...] = ddt.astype(ddt_ref.dtype)
        dz_ref[...] = dz.astype(dz_ref.dtype)
        dstate[...] = ds
        for r, d in ((ddtb_ref, ddtb), (dalog_ref, dalog), (dd_ref, dd), (dnw_ref, dnw)):
            @pl.when(c == 0)
            def _(r=r, d=d):
                r[...] = d

            @pl.when(c > 0)
            def _(r=r, d=d):
                r[...] += d

    whole = lambda shape: pl.BlockSpec(shape, lambda c: (0,) * len(shape))
    specs = specs + [pl.BlockSpec((1, SSM_N, SSM_DINNER), lambda c: (cidx(c), 0, 0)),
                     pl.BlockSpec((CHUNK, SSM_DINNER), lambda c: (cidx(c), 1))]
    return pl.pallas_call(
        body, name="ssd_bwd", grid=(nc,), in_specs=specs,
        out_specs=[pl.BlockSpec((CHUNK, XBC), lambda c: (cidx(c), 0)), pl.BlockSpec((CHUNK, LANE), lambda c: (cidx(c), 0)),
                   pl.BlockSpec((CHUNK, SSM_DINNER), lambda c: (cidx(c), 0)),
                   whole((1, LANE)), whole((1, LANE)), whole((1, LANE)), whole((1, SSM_DINNER))],
        out_shape=[jax.ShapeDtypeStruct((length, XBC), F32), jax.ShapeDtypeStruct((length, LANE), BF16),
                   jax.ShapeDtypeStruct((length, SSM_DINNER), BF16),
                   jax.ShapeDtypeStruct((1, LANE), F32), jax.ShapeDtypeStruct((1, LANE), F32),
                   jax.ShapeDtypeStruct((1, LANE), F32), jax.ShapeDtypeStruct((1, SSM_DINNER), F32)],
        scratch_shapes=[pltpu.VMEM((SSM_N, SSM_DINNER), F32)],
        compiler_params=_params(("arbitrary",)),
    )(xbc, proj, proj, dt_bias, a_log, d_skip, norm_w, tri, tri_t, expand, states, dmix)


def _cmul(ar, ai, br, bi):
    return ar * br - ai * bi, ar * bi + ai * br


def s5_scan(b_re, b_im, a_re, a_im, *, reverse=False, states=None, name, lw=256):
    length, lanes = b_re.shape
    nk = length // SCAN_SEG
    with_da = states is not None
    assert reverse or not with_da

    def shift(v):
        sub = lax.broadcasted_iota(jnp.int32, v.shape, 0)
        if reverse:
            return jnp.where(sub == SCAN_SEG - 1, 0.0, pltpu.roll(v, SCAN_SEG - 1, 0))
        return jnp.where(sub == 0, 0.0, pltpu.roll(v, 1, 0))

    def body(*refs):
        if with_da:
            bre_ref, bim_ref, are_ref, aim_ref, sre_ref, sim_ref, xre_ref, xim_ref, dare_ref, daim_ref = refs
        else:
            bre_ref, bim_ref, are_ref, aim_ref, xre_ref, xim_ref = refs
        ar = jnp.broadcast_to(are_ref[...], (SCAN_SEG, lw))
        ai = jnp.broadcast_to(aim_ref[...], (SCAN_SEG, lw))

        def tile(i):
            k = (nk - 1 - i) if reverse else i
            return pl.ds(pl.multiple_of(k * SCAN_SEG, SCAN_SEG), SCAN_SEG)

        def local(i, carry):
            xr, xi, pr, pi = carry
            rows = tile(i)
            mr, mi = _cmul(ar, ai, xr, xi)
            xr, xi = mr + bre_ref[rows, :], mi + bim_ref[rows, :]
            xre_ref[rows, :] = xr
            xim_ref[rows, :] = xi
            pr, pi = _cmul(ar, ai, pr, pi)
            return xr, xi, pr, pi

        zero = jnp.zeros((SCAN_SEG, lw), F32)
        one = jnp.ones((SCAN_SEG, lw), F32)
        er, ei, pr, pi = lax.fori_loop(0, nk, local, (zero, zero, one, zero))
        cr, ci = zero, zero
        for _ in range(SCAN_SEG - 1):
            mr, mi = _cmul(pr, pi, cr, ci)
            cr, ci = shift(er + mr), shift(ei + mi)

        def fix(i, carry):
            pr, pi, dr, di = carry
            rows = tile(i)
            pr, pi = _cmul(ar, ai, pr, pi)
            mr, mi = _cmul(pr, pi, cr, ci)
            xr, xi = xre_ref[rows, :] + mr, xim_ref[rows, :] + mi
            xre_ref[rows, :] = xr
            xim_ref[rows, :] = xi
            if with_da:
                k = nk - 1 - i
                prev = pl.ds(pl.multiple_of(jnp.maximum(k - 1, 0) * SCAN_SEG, SCAN_SEG), SCAN_SEG)
                last = pl.ds((nk - 1) * SCAN_SEG, SCAN_SEG)
                sub = lax.broadcasted_iota(jnp.int32, (SCAN_SEG, lw), 0)
                wr = jnp.where(sub == 0, 0.0, pltpu.roll(sre_ref[last, :], 1, 0))
                wi = jnp.where(sub == 0, 0.0, pltpu.roll(sim_ref[last, :], 1, 0))
                sr = jnp.where(k == 0, wr, sre_ref[prev, :])
                si = jnp.where(k == 0, wi, sim_ref[prev, :])
                dr, di = dr + xr * sr + xi * si, di + xi * sr - xr * si
            return pr, pi, dr, di

        _, _, dr, di = lax.fori_loop(0, nk, fix, (one, zero, zero, zero))
        if with_da:
            dare_ref[...] = jnp.sum(dr, axis=0, keepdims=True)
            daim_ref[...] = jnp.sum(di, axis=0, keepdims=True)

    col = pl.BlockSpec((length, lw), lambda j: (0, j))
    vec = pl.BlockSpec((1, lw), lambda j: (0, j))
    ins = [b_re, b_im, a_re, a_im] + (list(states) if with_da else [])
    in_specs = [col, col, vec, vec] + ([col, col] if with_da else [])
    out_specs = [col, col] + ([vec, vec] if with_da else [])
    out_shape = [jax.ShapeDtypeStruct((length, lanes), F32)] * 2 + ([jax.ShapeDtypeStruct((1, lanes), F32)] * 2 if with_da else [])
    return pl.pallas_call(
        body, name=name, grid=(lanes // lw,), in_specs=in_specs, out_specs=out_specs, out_shape=out_shape,
        compiler_params=_params(("parallel",)),
    )(*ins)


def _seg_interleave(v):
    length = v.shape[0]
    return v.reshape(SCAN_SEG, length // SCAN_SEG, -1).transpose(1, 0, 2).reshape(length, -1)


def _seg_deinterleave(v):
    length = v.shape[0]
    return v.reshape(length // SCAN_SEG, SCAN_SEG, -1).transpose(1, 0, 2).reshape(length, -1)


def _block_diag(m):
    eye = jnp.eye(S5_GROUPS, dtype=m.dtype)
    return (m.reshape(S5_GROUPS, S5_GROUP, 1, S5_STATE) * eye[:, None, :, None]).reshape(S5_GROUPS * S5_GROUP, S5_LANES)


def _block_diag_take(full):
    idx = jnp.arange(S5_GROUPS)
    blocks = full.reshape(S5_GROUPS, S5_GROUP, S5_GROUPS, S5_STATE)[idx, :, idx, :]
    return blocks.reshape(S5_GROUPS * S5_GROUP, S5_STATE)


def _s5_prep(a_re, a_im, log_step, b_re, b_im, rep):
    step = jnp.exp(log_step)
    mag = jnp.exp(a_re * step)
    ab_re = mag * jnp.cos(a_im * step)
    ab_im = mag * jnp.sin(a_im * step)
    den = a_re * a_re + a_im * a_im
    f_re = ((ab_re - 1.0) * a_re + ab_im * a_im) / den
    f_im = (ab_im * a_re - (ab_re - 1.0) * a_im) / den
    fr, fi = _dot_hi(rep, f_re), _dot_hi(rep, f_im)
    return ab_re, ab_im, fr * b_re - fi * b_im, fr * b_im + fi * b_re


def _rms(x, g):
    return (x * lax.rsqrt(jnp.mean(x * x, axis=-1, keepdims=True) + EPS) * g,)


def _ffn_act(gate, up):
    return (_silu(gate) * up,)


def _glu(a, g):
    return (a * jax.nn.sigmoid(g),)


def _ln_silu(x, g, b):
    xc = x - jnp.mean(x, axis=-1, keepdims=True)
    var = jnp.mean(xc * xc, axis=-1, keepdims=True)
    return (_silu(xc * lax.rsqrt(var + EPS) * g + b),)


def _s5_post(y, u, d_skip, glu_w):
    s = jax.nn.gelu(y + d_skip * u)
    return (s * jax.nn.sigmoid(_mm(s, glu_w)),)


def loss_head(x, tgt, g, *, tl=512):
    length, d = x.shape
    tl = min(tl, length)

    def body(x_ref, t_ref, g_ref, loss_ref, dx_ref, dg_ref):
        i = pl.program_id(0)
        y, vjp = jax.vjp(lambda x, g: _rms(x, g)[0], x_ref[...], g_ref[...])
        err = y - t_ref[...]
        dx, dg = vjp(err * (1.0 / d))
        dx_ref[...] = dx
        part = jnp.broadcast_to(0.5 * jnp.sum(jnp.mean(err * err, axis=-1, keepdims=True), axis=0, keepdims=True), (1, LANE))

        @pl.when(i == 0)
        def _():
            loss_ref[...] = part
            dg_ref[...] = dg

        @pl.when(i > 0)
        def _():
            loss_ref[...] += part
            dg_ref[...] += dg

    row = pl.BlockSpec((tl, d), lambda i: (i, 0))
    return pl.pallas_call(
        body, name="loss_head", grid=(length // tl,),
        in_specs=[row, row, pl.BlockSpec((1, d), lambda i: (0, 0))],
        out_specs=[pl.BlockSpec((1, LANE), lambda i: (0, 0)), row, pl.BlockSpec((1, d), lambda i: (0, 0))],
        out_shape=[jax.ShapeDtypeStruct((1, LANE), F32), jax.ShapeDtypeStruct((length, d), F32),
                   jax.ShapeDtypeStruct((1, d), F32)],
        compiler_params=_params(("arbitrary",)),
    )(x, tgt, g)


def _pad_heads(v):
    return jnp.pad(v, ((0, 0), (0, LANE - v.shape[1])))


def local_step(x, tgt, w):
    length = x.shape[0]
    cos2, sin2 = _rotary_tables(length)
    grads = {}

    def rms_fwd(xin, g, name):
        return rowwise_fwd(_rms, [xin], [], [g], [], [(D_MODEL, BF16)], name=name, tl=512)[0]

    def rms_bwd(xin, g, dh, dxo, name):
        return rowwise_bwd(_rms, [xin], [], [g], [], [dh], [F32], name=name, tl=512, add=dxo)

    def ffn_fwd(i, xin):
        hf = rms_fwd(xin, w["ffn_norm"][i:i + 1], f"ffn{i}_norm")
        a = matmul(hf, w["ffn_w_up"][i], name=f"ffn{i}_up")
        act = ffn_conv_act(a, w["ffn_dw_w"][i], w["ffn_dw_b"][i:i + 1], name=f"ffn{i}_conv_act")
        return matmul(act, w["ffn_w_down"][i], res=xin, name=f"ffn{i}_down"), (hf, a, act)

    def ffn_bwd(i, xin, saved, dxo):
        hf, a, act = saved
        dact = matmul(dxo, w["ffn_w_down"][i], tb=True, name=f"ffn{i}_down_dx")
        dw_down = matmul(act, dxo, ta=True, name=f"ffn{i}_down_dw")
        da, ddw_w, ddw_b = ffn_conv_act_bwd(a, w["ffn_dw_w"][i], w["ffn_dw_b"][i:i + 1], dact,
                                            name=f"ffn{i}_conv_act_bwd")
        dw_up = matmul(hf, da, ta=True, name=f"ffn{i}_up_dw")
        dhf = matmul(da, w["ffn_w_up"][i], tb=True, name=f"ffn{i}_up_dx")
        dxin, dnorm = rms_bwd(xin, w["ffn_norm"][i:i + 1], dhf, dxo, f"ffn{i}_norm_bwd")
        return dxin, dict(ffn_norm=dnorm, ffn_w_up=dw_up, ffn_dw_w=ddw_w, ffn_dw_b=ddw_b, ffn_w_down=dw_down)

    w_in_e = jnp.pad(w["e_w_in"][0], ((0, 0), (0, EVEN_IN_PAD - EVEN_IN)))
    w_out_e = w["e_w_out"][0]
    conv_w_e, conv_b_e = w["e_conv_w"][0], w["e_conv_b"]
    dt_bias, a_log, d_skip = _pad_heads(w["e_dt_bias"]), _pad_heads(w["e_a_log"]), _pad_heads(w["e_d"])
    xbc_off = 4 * D_MODEL

    hn0 = rms_fwd(x, w["mix_norm"][0:1], "mix0_norm")
    proj0 = matmul(hn0, w_in_e, name="even_in")
    y_ret, ret_states = retention_fwd(proj0, cos2, sin2)
    xbc = conv_fwd(proj0, conv_w_e, conv_b_e, act=True, off=xbc_off, name="ssd_conv")
    y_ssm, ssd_states = ssd_fwd(xbc, proj0, dt_bias, a_log, d_skip, w["e_ssm_norm"])
    mix0 = jnp.concatenate([y_ret, y_ssm], axis=1)
    x1 = matmul(mix0, w_out_e, res=x, name="even_out")
    x2, ffn0_saved = ffn_fwd(0, x1)

    w_in_o, w_out_o, glu_w = w["o_w_in"][0], w["o_w_out"][0], w["o_glu_w"][0]
    dw_w_o, dw_b_o, ln_g, ln_b, d_o = w["o_dw_w"][0], w["o_dw_b"], w["o_ln_g"], w["o_ln_b"], w["o_d"]
    rep = jnp.asarray(np.repeat(np.eye(S5_GROUPS, dtype=np.float32), S5_GROUP, axis=0))
    rows_gc = (S5_GROUPS * S5_GROUP, S5_STATE)
    prep_in = [w["o_a_re"][0], w["o_a_im"][0], w["o_log_step"].reshape(S5_GROUPS, 1),
               w["o_b_re"][0].transpose(0, 2, 1).reshape(rows_gc), w["o_b_im"][0].transpose(0, 2, 1).reshape(rows_gc), rep]
    ab_re, ab_im, bb_re, bb_im = whole_fwd(
        _s5_prep, prep_in, [(S5_GROUPS, S5_STATE)] * 2 + [rows_gc] * 2, name="s5_prep")
    a_re_row, a_im_row = ab_re.reshape(1, S5_LANES), ab_im.reshape(1, S5_LANES)
    b_re_bd, b_im_bd = _block_diag(bb_re).astype(BF16), _block_diag(bb_im).astype(BF16)
    c_re_bd = _block_diag(w["o_c_re"][0].reshape(rows_gc)).astype(BF16)
    c_im_neg_bd = _block_diag(-w["o_c_im"][0].reshape(rows_gc)).astype(BF16)

    hn1 = rms_fwd(x2, w["mix_norm"][1:2], "mix1_norm")
    proj1 = matmul(hn1, w_in_o, name="odd_in")
    half = D_MODEL // 2
    c_glu = rowwise_fwd(_glu, [Cols(proj1, half, 0), Cols(proj1, half, 1)], [], [], [], [(half, F32)],
                        name="conf_glu", tl=512)[0]
    c_conv = conv_fwd(c_glu, dw_w_o, dw_b_o, act=False, name="conf_conv")
    c_out = rowwise_fwd(_ln_silu, [c_conv], [], [ln_g, ln_b], [], [(half, BF16)], name="conf_ln", tl=512)[0]
    u_seg = _seg_interleave(proj1[:, 2 * half:])
    bu_re = matmul(u_seg, b_re_bd, name="s5_bu_re")
    bu_im = matmul(u_seg, b_im_bd, name="s5_bu_im")
    xs_re, xs_im = s5_scan(bu_re, bu_im, a_re_row, a_im_row, name="s5_scan")
    y_im = matmul(xs_im, c_im_neg_bd, tb=True, name="s5_y_im")
    y_s5 = _seg_deinterleave(matmul(xs_re, c_re_bd, tb=True, res=y_im, name="s5_y_re"))
    s_out = rowwise_fwd(_s5_post, [y_s5, Cols(proj1, half, 2)], [], [d_o, glu_w], [], [(half, BF16)],
                        name="s5_post", tl=512)[0]
    mix1 = jnp.concatenate([c_out, s_out], axis=1)
    x3 = matmul(mix1, w_out_o, res=x2, name="odd_out")
    x4, ffn1_saved = ffn_fwd(1, x3)

    loss, dx4, dfinal = loss_head(x4, tgt, w["final_norm"].reshape(1, D_MODEL))
    grads["final_norm"] = dfinal.reshape(D_MODEL)

    dx3, g_ffn1 = ffn_bwd(1, x3, ffn1_saved, dx4)
    dmix1 = matmul(dx3, w_out_o, tb=True, name="odd_out_dx")
    grads["o_w_out"] = [matmul(mix1, dx3, ta=True, name="odd_out_dw")]
    dc_conv, dln_g, dln_b = rowwise_bwd(_ln_silu, [c_conv], [], [ln_g, ln_b], [], [Cols(dmix1, half, 0)], [F32],
                                        name="conf_ln_bwd", tl=512)
    dc_glu, ddw_w_o, ddw_b_o = conv_bwd(c_glu, dw_w_o, dw_b_o, dc_conv, act=False, name="conf_conv_bwd")
    d_cacg = rowwise_bwd(_glu, [Cols(proj1, half, 0), Cols(proj1, half, 1)], [], [], [], [dc_glu], [BF16],
                         name="conf_glu_bwd", tl=512, merge=True)[0]
    dy_s5, du_post, dd_o, dglu_w = rowwise_bwd(
        _s5_post, [y_s5, Cols(proj1, half, 2)], [], [d_o, glu_w], [], [Cols(dmix1, half, 1)], [F32, F32],
        name="s5_post_bwd", tl=512)
    dy_seg = _seg_interleave(dy_s5)
    dxs_re = matmul(dy_seg, c_re_bd, name="s5_dx_re")
    dxs_im = matmul(dy_seg, c_im_neg_bd, name="s5_dx_im")
    dc_re_bd = matmul(dy_seg, xs_re, ta=True, name="s5_dc_re")
    dc_im_neg_bd = matmul(dy_seg, xs_im, ta=True, name="s5_dc_im")
    g_re, g_im, dab_re, dab_im = s5_scan(dxs_re, dxs_im, a_re_row, -a_im_row, reverse=True, states=(xs_re, xs_im),
                                         name="s5_scan_bwd", lw=LANE)
    dbb_re = _block_diag_take(matmul(u_seg, g_re, ta=True, name="s5_db_re"))
    dbb_im = _block_diag_take(matmul(u_seg, g_im, ta=True, name="s5_db_im"))
    du_im = matmul(g_im, b_im_bd, tb=True, name="s5_du_im")
    du = _seg_deinterleave(matmul(g_re, b_re_bd, tb=True, res=du_im, name="s5_du_re")) + du_post
    da_re, da_im, dlog_step, db_re, db_im = whole_bwd(
        _s5_prep, prep_in, 5,
        [dab_re.reshape(S5_GROUPS, S5_STATE), dab_im.reshape(S5_GROUPS, S5_STATE), dbb_re, dbb_im], name="s5_prep_bwd")
    gcn = (S5_GROUPS, S5_GROUP, S5_STATE)
    grads.update(
        o_a_re=da_re[None], o_a_im=da_im[None], o_log_step=dlog_step.reshape(1, S5_GROUPS),
        o_b_re=db_re.reshape(gcn).transpose(0, 2, 1)[None], o_b_im=db_im.reshape(gcn).transpose(0, 2, 1)[None],
        o_c_re=_block_diag_take(dc_re_bd).reshape(gcn)[None], o_c_im=-_block_diag_take(dc_im_neg_bd).reshape(gcn)[None],
        o_d=dd_o, o_glu_w=[dglu_w], o_dw_w=ddw_w_o[None], o_dw_b=ddw_b_o, o_ln_g=dln_g, o_ln_b=dln_b)
    dproj1 = jnp.concatenate([d_cacg, du.astype(BF16)], axis=1)
    grads["o_w_in"] = [matmul(hn1, dproj1, ta=True, name="odd_in_dw")]
    dhn1 = matmul(dproj1, w_in_o, tb=True, name="odd_in_dx")
    dx2, dmix_norm1 = rms_bwd(x2, w["mix_norm"][1:2], dhn1, dx3, "mix1_norm_bwd")

    dx1, g_ffn0 = ffn_bwd(0, x1, ffn0_saved, dx2)
    for k in g_ffn0:
        per_layer = [g_ffn0[k], g_ffn1[k]]
        grads[k] = per_layer if k in ("ffn_w_up", "ffn_w_down") else jnp.stack(per_layer).reshape(w[k].shape)
    dmix0 = matmul(dx1, w_out_e, tb=True, name="even_out_dx")
    grads["e_w_out"] = [matmul(mix0, dx1, ta=True, name="even_out_dw")]
    dq, dk, dv, dg = retention_bwd(proj0, cos2, sin2, ret_states, dmix0)
    dxbc_c, ddt, dz, ddt_bias, da_log, dd_skip, dssm_norm = ssd_bwd(
        xbc, proj0, dt_bias, a_log, d_skip, w["e_ssm_norm"], ssd_states, dmix0)
    dxbc, dconv_w, dconv_b = conv_bwd(proj0, conv_w_e, conv_b_e, dxbc_c, act=True, off=xbc_off,
                                      name="ssd_conv_bwd", dx_dtype=BF16)
    dproj0 = jnp.concatenate([dq, dk, dv, dg, dz, dxbc, ddt], axis=1)
    grads["e_w_in"] = [matmul(hn0, dproj0, ta=True, name="even_in_dw")[:, :EVEN_IN]]
    dhn0 = matmul(dproj0, w_in_e, tb=True, name="even_in_dx")
    dx, dmix_norm0 = rms_bwd(x, w["mix_norm"][0:1], dhn0, dx1, "mix0_norm_bwd")
    grads.update(
        mix_norm=jnp.concatenate([dmix_norm0, dmix_norm1], axis=0), e_conv_w=dconv_w[None], e_conv_b=dconv_b,
        e_dt_bias=ddt_bias[:, :SSM_HEADS], e_a_log=da_log[:, :SSM_HEADS], e_d=dd_skip[:, :SSM_HEADS],
        e_ssm_norm=dssm_norm)
    return loss, dx, grads


def adamw(w, g, m, v, *, name):
    shape = w.shape
    cols = shape[-1]
    rows = w.size // cols
    tr = _tile(rows, max(8, (512 * 1024 // cols) // 8 * 8), unit=8)

    def body(w_ref, g_ref, m_ref, v_ref, d_ref, nm_ref, nv_ref):
        gv = g_ref[...]
        nm = ADAM_B1 * m_ref[...] + (1.0 - ADAM_B1) * gv
        nv = ADAM_B2 * v_ref[...] + (1.0 - ADAM_B2) * jnp.square(gv)
        m_hat = nm / (1.0 - ADAM_B1 ** ADAM_STEP)
        v_hat = nv / (1.0 - ADAM_B2 ** ADAM_STEP)
        d_ref[...] = -ADAM_LR * (m_hat / (jnp.sqrt(v_hat) + ADAM_EPS) + ADAM_WD * w_ref[...])
        nm_ref[...] = nm
        nv_ref[...] = nv

    spec = pl.BlockSpec((tr, cols), lambda i: (i, 0))
    outs = pl.pallas_call(
        body, name=name, grid=(rows // tr,), in_specs=[spec] * 4, out_specs=[spec] * 3,
        out_shape=[jax.ShapeDtypeStruct((rows, cols), F32)] * 3, compiler_params=_params(("parallel",)),
    )(*[t.reshape(rows, cols) for t in (w, g, m, v)])
    return [o.reshape(shape) for o in outs]


OTHER_CHIPS = ((1, 0), (0, 1), (1, 1))
ANY = pl.BlockSpec(memory_space=pl.ANY)


def _position():
    return lax.axis_index("x"), lax.axis_index("y"), lax.axis_index("c")


def _flip(v, f):
    return 1 - v if f else v


def _remote(src, dst, send_sem, recv_sem, device):
    return pltpu.make_async_remote_copy(src_ref=src, dst_ref=dst, send_sem=send_sem, recv_sem=recv_sem,
                                        device_id=device, device_id_type=MESH)


def gather_shards(big, small):
    n_big, n_small = len(big), len(small)
    halves = [a.shape[0] // 2 for a in big]

    def body(*refs):
        big_refs, small_refs = refs[:n_big], refs[n_big:n_big + n_small]
        obig_refs = refs[n_big + n_small:2 * n_big + n_small]
        osmall_refs = refs[2 * n_big + n_small:2 * (n_big + n_small)]
        ici_send, ici_recv, d2d_send, d2d_recv, small_send, small_recv, local_sems = refs[2 * (n_big + n_small):]
        x, y, c = _position()
        mine = 2 * x + y
        local = [pltpu.make_async_copy(src, dst.at[mine], local_sems.at[i])
                 for i, (src, dst) in enumerate(zip(big_refs + small_refs, obig_refs + osmall_refs))]
        for cp in local:
            cp.start()

        def half(k, core):
            return pl.ds(pl.multiple_of(core * halves[k], 16), halves[k])

        sends = []
        for j, (fx, fy) in enumerate(OTHER_CHIPS):
            peer = (_flip(x, fx), _flip(y, fy), c)
            for k in range(n_big):
                sends.append(_remote(big_refs[k].at[half(k, c)], obig_refs[k].at[mine, half(k, c)],
                                     ici_send.at[j, k], ici_recv.at[j, k], peer))
            for k in range(n_small):
                sends.append(_remote(small_refs[k], osmall_refs[k].at[mine], small_send.at[j, k], small_recv.at[j, k], peer))
        for cp in sends:
            cp.start()
        for j, (fx, fy) in enumerate(OTHER_CHIPS):
            px, py = _flip(x, fx), _flip(y, fy)
            src_chip = 2 * px + py
            for k in range(n_big):
                landed = obig_refs[k].at[src_chip, half(k, c)]
                _remote(landed, landed, ici_send.at[j, k], ici_recv.at[j, k], (px, py, c)).wait_recv()
                fwd = _remote(landed, landed, d2d_send.at[j, k], d2d_recv.at[j, k], (x, y, 1 - c))
                fwd.start()
                sends.append(fwd)
        for j, (fx, fy) in enumerate(OTHER_CHIPS):
            px, py = _flip(x, fx), _flip(y, fy)
            src_chip = 2 * px + py
            for k in range(n_big):
                other = obig_refs[k].at[src_chip, half(k, 1 - c)]
                _remote(other, other, d2d_send.at[j, k], d2d_recv.at[j, k], (x, y, 1 - c)).wait_recv()
            for k in range(n_small):
                dst = osmall_refs[k].at[src_chip]
                _remote(small_refs[k], dst, small_send.at[j, k], small_recv.at[j, k], (px, py, c)).wait_recv()
        for cp in sends:
            cp.wait_send()
        for cp in local:
            cp.wait()

    arrays = list(big) + list(small)
    dma = pltpu.SemaphoreType.DMA
    return pl.pallas_call(
        body, name="gather_shards", in_specs=[ANY] * len(arrays), out_specs=[ANY] * len(arrays),
        out_shape=[jax.ShapeDtypeStruct((4,) + a.shape, a.dtype) for a in arrays],
        scratch_shapes=[dma((3, n_big)), dma((3, n_big)), dma((3, n_big)), dma((3, n_big)),
                        dma((3, n_small)), dma((3, n_small)), dma((n_big + n_small,))],
        compiler_params=_params(),
    )(*arrays)


def allreduce_small(pack):
    rows = pack.shape[0]

    def body(p_ref, o_ref, slots, send_sems, recv_sems):
        x, y, c = _position()
        me = 4 * x + 2 * y + c
        slots[me] = p_ref[...]
        flips = [((k >> 2) & 1, (k >> 1) & 1, k & 1) for k in range(1, 8)]
        sends = []
        for k, (fx, fy, fc) in enumerate(flips):
            peer = (_flip(x, fx), _flip(y, fy), _flip(c, fc))
            sends.append(_remote(p_ref, slots.at[me], send_sems.at[k], recv_sems.at[k], peer))
        for cp in sends:
            cp.start()
        for k, (fx, fy, fc) in enumerate(flips):
            px, py, pc = _flip(x, fx), _flip(y, fy), _flip(c, fc)
            _remote(p_ref, slots.at[4 * px + 2 * py + pc], send_sems.at[k], recv_sems.at[k], (px, py, pc)).wait_recv()
        for cp in sends:
            cp.wait_send()
        acc = slots[0]
        for d in range(1, 8):
            acc = acc + slots[d]
        o_ref[...] = acc

    vmem = pl.BlockSpec(memory_space=pltpu.VMEM)
    return pl.pallas_call(
        body, name="allreduce_small", in_specs=[vmem], out_specs=vmem,
        out_shape=jax.ShapeDtypeStruct(pack.shape, F32),
        scratch_shapes=[pltpu.VMEM((8, rows, LANE), F32), pltpu.SemaphoreType.DMA((7,)), pltpu.SemaphoreType.DMA((7,))],
        compiler_params=_params(),
    )(pack)


def exchange_halves(gs):
    n = len(gs)

    def body(*refs):
        g_refs, o_refs, (send_sems, recv_sems) = refs[:n], refs[n:2 * n], refs[2 * n:]
        x, y, c = _position()
        copies = [_remote(g_refs[k].at[:, 1 - c], o_refs[k], send_sems.at[k], recv_sems.at[k], (x, y, 1 - c)) for k in range(n)]
        for cp in copies:
            cp.start()
        for cp in copies:
            cp.wait()

    return pl.pallas_call(
        body, name="exchange_halves", in_specs=[ANY] * n, out_specs=[ANY] * n,
        out_shape=[jax.ShapeDtypeStruct((4,) + g.shape[2:], g.dtype) for g in gs],
        scratch_shapes=[pltpu.SemaphoreType.DMA((n,)), pltpu.SemaphoreType.DMA((n,))],
        compiler_params=_params(),
    )(*gs)


def scatter_to_chips(parts):
    n = len(parts)

    def body(*refs):
        a_refs, o_refs, (send_sems, recv_sems) = refs[:n], refs[n:2 * n], refs[2 * n:]
        x, y, c = _position()
        copies = []
        for j, (fx, fy) in enumerate(OTHER_CHIPS):
            px, py = _flip(x, fx), _flip(y, fy)
            for k in range(n):
                copies.append(_remote(a_refs[k].at[2 * px + py], o_refs[k].at[j], send_sems.at[j, k], recv_sems.at[j, k], (px, py, c)))
        for cp in copies:
            cp.start()
        for cp in copies:
            cp.wait()

    return pl.pallas_call(
        body, name="scatter_to_chips", in_specs=[ANY] * n, out_specs=[ANY] * n,
        out_shape=[jax.ShapeDtypeStruct((3,) + a.shape[1:], a.dtype) for a in parts],
        scratch_shapes=[pltpu.SemaphoreType.DMA((3, n)), pltpu.SemaphoreType.DMA((3, n))],
        compiler_params=_params(),
    )(*parts)


def share_halves(rs):
    n = len(rs)

    def body(*refs):
        r_refs, o_refs, (send_sems, recv_sems, local_sems) = refs[:n], refs[n:2 * n], refs[2 * n:]
        x, y, c = _position()
        local = [pltpu.make_async_copy(r_refs[k], o_refs[k].at[c], local_sems.at[k]) for k in range(n)]
        sends = [_remote(r_refs[k], o_refs[k].at[c], send_sems.at[k], recv_sems.at[k], (x, y, 1 - c)) for k in range(n)]
        for cp in local + sends:
            cp.start()
        for k in range(n):
            _remote(r_refs[k], o_refs[k].at[1 - c], send_sems.at[k], recv_sems.at[k], (x, y, 1 - c)).wait_recv()
        for cp in sends:
            cp.wait_send()
        for cp in local:
            cp.wait()

    dma = pltpu.SemaphoreType.DMA
    return pl.pallas_call(
        body, name="share_halves", in_specs=[ANY] * n, out_specs=[ANY] * n,
        out_shape=[jax.ShapeDtypeStruct((2,) + r.shape, r.dtype) for r in rs],
        scratch_shapes=[dma((n,)), dma((n,)), dma((n,))],
        compiler_params=_params(),
    )(*rs)


def add_own_half(g, r, c_idx, *, name):
    _, _, h, cols = g.shape

    def body(c_ref, g_ref, r_ref, o_ref):
        o_ref[...] = (g_ref[0] + r_ref[...]).astype(o_ref.dtype)

    return pl.pallas_call(
        body, name=name,
        grid_spec=pltpu.PrefetchScalarGridSpec(
            num_scalar_prefetch=1, grid=(4,),
            in_specs=[pl.BlockSpec((1, 1, h, cols), lambda s, c: (s, c[0], 0, 0)),
                      pl.BlockSpec((1, h, cols), lambda s, c: (s, 0, 0))],
            out_specs=pl.BlockSpec((1, h, cols), lambda s, c: (s, 0, 0))),
        out_shape=jax.ShapeDtypeStruct(r.shape, BF16), compiler_params=_params(("parallel",)),
    )(c_idx, g, r)


def add_chip_parts(a, parts, chip_idx, *, name):
    _, h, cols = a.shape
    th = h // 2

    def body(s_ref, a_ref, p0_ref, p1_ref, p2_ref, o_ref):
        f = lambda r: r[0].astype(F32)
        o_ref[...] = ((f(a_ref) + f(p0_ref)) + f(p1_ref)) + f(p2_ref)

    part = lambda j: pl.BlockSpec((1, th, cols), lambda i, s, j=j: (j, i, 0))
    return pl.pallas_call(
        body, name=name,
        grid_spec=pltpu.PrefetchScalarGridSpec(
            num_scalar_prefetch=1, grid=(2,),
            in_specs=[pl.BlockSpec((1, th, cols), lambda i, s: (s[0], i, 0)), part(0), part(1), part(2)],
            out_specs=pl.BlockSpec((th, cols), lambda i, s: (i, 0))),
        out_shape=jax.ShapeDtypeStruct((h, cols), F32), compiler_params=_params(("parallel",)),
    )(chip_idx, a, parts, parts, parts)


WEIGHTS = ("mix_norm", "e_w_in", "e_conv_w", "e_conv_b", "e_dt_bias", "e_a_log", "e_d", "e_ssm_norm", "e_w_out",
           "o_w_in", "o_dw_w", "o_dw_b", "o_ln_g", "o_ln_b", "o_a_re", "o_a_im", "o_b_re", "o_b_im", "o_c_re",
           "o_c_im", "o_d", "o_log_step", "o_glu_w", "o_w_out", "ffn_norm", "ffn_w_up", "ffn_dw_w", "ffn_dw_b",
           "ffn_w_down", "final_norm")
BIG = (("e_w_in", 2), ("e_w_out", 1), ("o_w_in", 2), ("o_glu_w", 1), ("o_w_out", 1), ("ffn_w_up", 2), ("ffn_w_down", 1))
SMALL_SHARDED = (("e_conv_w", 2), ("o_dw_w", 2), ("o_dw_b", 1), ("o_ln_g", 1), ("o_ln_b", 1), ("o_d", 1), ("ffn_dw_w", 2))
REPLICATED = tuple(n for n in WEIGHTS if n not in dict(BIG + SMALL_SHARDED))
PACK_ROWS = 8


def _pack(arrays, dtype, row_unit=PACK_ROWS):
    flat = jnp.concatenate([a.astype(dtype).reshape(-1) for a in arrays])
    rows = -(-flat.size // (LANE * row_unit)) * row_unit
    return jnp.pad(flat, (0, rows * LANE - flat.size)).reshape(rows, LANE)


def _unpack(flat, shapes, lead=()):
    out, off = [], 0
    for shape in shapes:
        size = int(np.prod(shape))
        out.append(flat[..., off:off + size].reshape(lead + tuple(shape)))
        off += size
    return out


def _join_shards(parts, axis):
    return jnp.concatenate([parts[s] for s in range(4)], axis=axis)


def _split_shards(full, axis):
    return jnp.stack(jnp.split(full, 4, axis=axis))


def _rows2d(a):
    return a.reshape(-1, a.shape[-1])


def _layer_shards(g, axis):
    rows, cols = g.shape
    if axis == 0:
        return g.reshape(4, 2, rows // 8, cols)
    return g.reshape(rows, 4, cols // 4).transpose(1, 0, 2).reshape(4, 2, rows // 2, cols // 4)


def kernel(x, mix_norm, e_w_in, e_conv_w, e_conv_b, e_dt_bias, e_a_log, e_d, e_ssm_norm, e_w_out, o_w_in, o_dw_w, o_dw_b, o_ln_g, o_ln_b, o_a_re, o_a_im, o_b_re, o_b_im, o_c_re, o_c_im, o_d, o_log_step, o_glu_w, o_w_out, ffn_norm, ffn_w_up, ffn_dw_w, ffn_dw_b, ffn_w_down, final_norm, loss_target, m_mix_norm, m_e_w_in, m_e_conv_w, m_e_conv_b, m_e_dt_bias, m_e_a_log, m_e_d, m_e_ssm_norm, m_e_w_out, m_o_w_in, m_o_dw_w, m_o_dw_b, m_o_ln_g, m_o_ln_b, m_o_a_re, m_o_a_im, m_o_b_re, m_o_b_im, m_o_c_re, m_o_c_im, m_o_d, m_o_log_step, m_o_glu_w, m_o_w_out, m_ffn_norm, m_ffn_w_up, m_ffn_dw_w, m_ffn_dw_b, m_ffn_w_down, m_final_norm, v_mix_norm, v_e_w_in, v_e_conv_w, v_e_conv_b, v_e_dt_bias, v_e_a_log, v_e_d, v_e_ssm_norm, v_e_w_out, v_o_w_in, v_o_dw_w, v_o_dw_b, v_o_ln_g, v_o_ln_b, v_o_a_re, v_o_a_im, v_o_b_re, v_o_b_im, v_o_c_re, v_o_c_im, v_o_d, v_o_log_step, v_o_glu_w, v_o_w_out, v_ffn_norm, v_ffn_w_up, v_ffn_dw_w, v_ffn_dw_b, v_ffn_w_down, v_final_norm):
    given = dict(locals())
    chip = 2 * lax.axis_index("x") + lax.axis_index("y")
    core = lax.axis_index("c")

    gathered = gather_shards([_rows2d(given[n]).astype(BF16) for n, _ in BIG],
                             [_rows2d(given[n]) for n, _ in SMALL_SHARDED])
    w = {n: given[n] for n in REPLICATED}
    for (n, axis), parts in zip(BIG + SMALL_SHARDED, gathered):
        w[n] = _join_shards(parts.reshape((4,) + given[n].shape), axis)

    loss, dx, grads = local_step(x[0], loss_target[0], w)

    small_names = REPLICATED + tuple(n for n, _ in SMALL_SHARDED)
    small_sum = allreduce_small(_pack([grads[n] for n in small_names], F32))
    reduced = dict(zip(small_names, _unpack(small_sum.reshape(-1), [grads[n].shape for n in small_names])))
    for n, axis in SMALL_SHARDED:
        width = given[n].shape[axis]
        reduced[n] = lax.dynamic_slice_in_dim(reduced[n], chip * width, width, axis=axis)

    keys, parts = [], []
    for n, axis in BIG:
        for layer, g in enumerate(grads[n]):
            keys.append((n, layer))
            parts.append(_layer_shards(g, axis - 1))
    core_idx, chip_idx = core.reshape(1).astype(jnp.int32), chip.reshape(1).astype(jnp.int32)
    tags = [f"{n}{layer}" for n, layer in keys]
    core_sums = [add_own_half(g, r, core_idx, name="add_own_half_" + t) for g, r, t in zip(parts, exchange_halves(parts), tags)]
    mine = [add_chip_parts(a, p, chip_idx, name="add_chip_parts_" + t) for a, p, t in zip(core_sums, scatter_to_chips(core_sums), tags)]
    layers = {}
    for (n, layer), both in zip(keys, share_halves(mine)):
        layers.setdefault(n, []).append(both.reshape(given[n].shape[1:]))
    for n, _ in BIG:
        reduced[n] = jnp.stack(layers[n])

    delta, new_m, new_v = {}, {}, {}
    for n, _ in BIG:
        delta[n], new_m[n], new_v[n] = adamw(given[n], reduced[n], given["m_" + n], given["v_" + n], name="adamw_" + n)
    shapes = [given[n].shape for n in small_names]
    packed = [_pack([src[n] for n in small_names], F32)
              for src in (given, reduced, {n: given["m_" + n] for n in small_names}, {n: given["v_" + n] for n in small_names})]
    for dst, res in zip((delta, new_m, new_v), adamw(*packed, name="adamw_small")):
        dst.update(zip(small_names, _unpack(res.reshape(-1), shapes)))

    total = lax.psum(loss[0, 0], ("x", "y", "c"))
    return (total, dx[None], *[reduced[n] for n in WEIGHTS], *[delta[n] for n in WEIGHTS],
            *[new_m[n] for n in WEIGHTS], *[new_v[n] for n in WEIGHTS])
```

```python
import functools
import math
from typing import NamedTuple

import numpy as np
import jax
import jax.numpy as jnp
from jax import lax
from jax.experimental import pallas as pl
from jax.experimental.pallas import tpu as pltpu

F32 = jnp.float32
BF16 = jnp.bfloat16
HIGHEST = lax.Precision.HIGHEST
MESH = pl.DeviceIdType.MESH

D_MODEL = 1024
EPS = 1e-6
RET_HEADS, RET_DK, RET_DV, CHUNK = 4, 128, 256, 128
ROPE_BASE = 10000.0
SSM_HEADS, SSM_P, SSM_N, SSM_GROUPS = 16, 64, 128, 2
SSM_DINNER = SSM_HEADS * SSM_P
EVEN_IN, EVEN_IN_PAD = 5648, 5760
S5_GROUPS, S5_GROUP, S5_STATE = 32, 16, 64
S5_LANES = S5_GROUPS * S5_STATE
SCAN_SEG = 8
D_FF = 2816
ADAM_LR, ADAM_B1, ADAM_B2, ADAM_EPS, ADAM_WD, ADAM_STEP = 0.001, 0.9, 0.999, 1e-08, 0.01, 10

LANE = 128
VMEM_LIMIT = 56 * 1024 * 1024


def _params(sem=None, **kw):
    return pltpu.CompilerParams(dimension_semantics=sem, vmem_limit_bytes=VMEM_LIMIT, **kw)


def _tile(n, target, unit=LANE):
    if n <= target:
        return n
    t = (target // unit) * unit
    while t >= unit:
        if n % t == 0:
            return t
        t -= unit
    return n


def _silu(x):
    return x * jax.nn.sigmoid(x)


def _mm(a, b):
    return jnp.dot(a.astype(BF16), b.astype(BF16), preferred_element_type=F32)


def _mm_nt(a, b):
    return lax.dot_general(a.astype(BF16), b.astype(BF16), (((1,), (1,)), ((), ())), preferred_element_type=F32)


def _mm_tn(a, b):
    return lax.dot_general(a.astype(BF16), b.astype(BF16), (((0,), (0,)), ((), ())), preferred_element_type=F32)


def _dot_hi(a, b):
    return jnp.dot(a, b, precision=HIGHEST, preferred_element_type=F32)


def _dot_hi_tn(a, b):
    return lax.dot_general(a, b, (((0,), (0,)), ((), ())), precision=HIGHEST, preferred_element_type=F32)


MATMUL_VMEM = 44 * 1024 * 1024


def matmul(a, b, *, ta=False, tb=False, res=None, out_dtype=F32, name):
    m, k = (a.shape[1], a.shape[0]) if ta else a.shape
    n = b.shape[0] if tb else b.shape[1]
    assert (b.shape[1] if tb else b.shape[0]) == k, (a.shape, b.shape, ta, tb)
    tm = _tile(m, 1536)
    tn = _tile(n, 640)
    if tn < 384:
        tn = _tile(n, 1536)
    res_bytes = 0 if res is None else res.dtype.itemsize

    def vmem(tm, tn):
        return 2 * (tm * k * a.dtype.itemsize + tn * k * b.dtype.itemsize + tm * tn * (jnp.dtype(out_dtype).itemsize + res_bytes))

    while vmem(tm, tn) > MATMUL_VMEM and tm % (2 * LANE) == 0:
        tm //= 2
    assert vmem(tm, tn) <= MATMUL_VMEM, (name, tm, tn, k)
    a_spec = pl.BlockSpec((k, tm), lambda i, j: (0, i)) if ta else pl.BlockSpec((tm, k), lambda i, j: (i, 0))
    b_spec = pl.BlockSpec((tn, k), lambda i, j: (j, 0)) if tb else pl.BlockSpec((k, tn), lambda i, j: (0, j))
    o_spec = pl.BlockSpec((tm, tn), lambda i, j: (i, j))
    dims = (((0 if ta else 1,), (1 if tb else 0,)), ((), ()))
    has_res = res is not None

    def body(a_ref, b_ref, *rest):
        o_ref = rest[-1]
        out = lax.dot_general(a_ref[...].astype(BF16), b_ref[...].astype(BF16), dims, preferred_element_type=F32)
        if has_res:
            out = out + rest[0][...].astype(F32)
        o_ref[...] = out.astype(o_ref.dtype)

    ins = [a, b] + ([res] if has_res else [])
    specs = [a_spec, b_spec] + ([o_spec] if has_res else [])
    return pl.pallas_call(
        body, name=name, grid=(m // tm, n // tn), in_specs=specs, out_specs=o_spec,
        out_shape=jax.ShapeDtypeStruct((m, n), out_dtype), compiler_params=_params(("parallel", "parallel")),
    )(*ins)


class Cols(NamedTuple):
    arr: jax.Array
    w: int
    j: int


def _cols(a):
    return a if isinstance(a, Cols) else Cols(a, a.shape[1], 0)


def _row_spec(c, tl):
    return pl.BlockSpec((tl, c.w), lambda i, j=c.j: (i, j))


def _whole_spec(p):
    return pl.BlockSpec(p.shape, lambda i, nd=p.ndim: (0,) * nd)


def rowwise_fwd(fn, rows, aux, pars, consts, outs, *, name, tl):
    rows = [_cols(r) for r in rows + aux]
    whole = list(pars) + list(consts)
    n_rows = len(rows)
    n_whole = len(whole)
    length = rows[0].arr.shape[0]
    tl = min(tl, length)

    def body(*refs):
        vals = [r[...].astype(F32) for r in refs[:n_rows]] + [r[...] for r in refs[n_rows:n_rows + n_whole]]
        res = fn(*vals)
        for o_ref, v in zip(refs[n_rows + n_whole:], res, strict=True):
            o_ref[...] = v.astype(o_ref.dtype)

    return pl.pallas_call(
        body, name=name, grid=(length // tl,),
        in_specs=[_row_spec(r, tl) for r in rows] + [_whole_spec(p) for p in whole],
        out_specs=[pl.BlockSpec((tl, w), lambda i: (i, 0)) for w, _ in outs],
        out_shape=[jax.ShapeDtypeStruct((length, w), dt) for w, dt in outs],
        compiler_params=_params(("parallel",)),
    )(*[r.arr for r in rows], *whole)


def rowwise_bwd(fn, rows, aux, pars, consts, cots, drow_dtypes, *, name, tl, add=None, merge=False):
    rows = [_cols(r) for r in rows]
    aux = [_cols(r) for r in aux]
    cots = [_cols(r) for r in cots]
    n_r, n_a, n_p, n_c, n_t = len(rows), len(aux), len(pars), len(consts), len(cots)
    length = rows[0].arr.shape[0]
    tl = min(tl, length)
    has_add = add is not None
    widths = [r.w for r in rows]

    def body(*refs):
        pos = 0
        r_vals = [r[...].astype(F32) for r in refs[pos:pos + n_r]]; pos += n_r
        a_vals = [r[...].astype(F32) for r in refs[pos:pos + n_a]]; pos += n_a
        p_vals = [r[...].astype(F32) for r in refs[pos:pos + n_p]]; pos += n_p
        c_vals = [r[...] for r in refs[pos:pos + n_c]]; pos += n_c
        t_vals = [r[...].astype(F32) for r in refs[pos:pos + n_t]]; pos += n_t
        add_val = None
        if has_add:
            add_val = refs[pos][...].astype(F32); pos += 1
        n_dr = 1 if merge else n_r
        dr_refs = refs[pos:pos + n_dr]; pos += n_dr
        dp_refs = refs[pos:pos + n_p]

        def f(*rp):
            return fn(*rp[:n_r], *a_vals, *rp[n_r:], *c_vals)

        _, vjp = jax.vjp(f, *r_vals, *p_vals)
        grads = vjp(tuple(t_vals))
        drows = list(grads[:n_r])
        if has_add:
            drows[0] = drows[0] + add_val
        if merge:
            off = 0
            for w, d in zip(widths, drows):
                dr_refs[0][:, off:off + w] = d.astype(dr_refs[0].dtype)
                off += w
        else:
            for r, d in zip(dr_refs, drows):
                r[...] = d.astype(r.dtype)
        i = pl.program_id(0)
        for r, d in zip(dp_refs, grads[n_r:]):
            @pl.when(i == 0)
            def _(r=r, d=d):
                r[...] = d

            @pl.when(i > 0)
            def _(r=r, d=d):
                r[...] += d

    if merge:
        dr_specs = [pl.BlockSpec((tl, sum(widths)), lambda i: (i, 0))]
        dr_shapes = [jax.ShapeDtypeStruct((length, sum(widths)), drow_dtypes[0])]
    else:
        dr_specs = [pl.BlockSpec((tl, w), lambda i: (i, 0)) for w in widths]
        dr_shapes = [jax.ShapeDtypeStruct((length, w), dt) for w, dt in zip(widths, drow_dtypes)]
    ins = [r.arr for r in rows + aux] + list(pars) + list(consts) + [r.arr for r in cots] + ([add] if has_add else [])
    specs = ([_row_spec(r, tl) for r in rows + aux] + [_whole_spec(p) for p in list(pars) + list(consts)]
             + [_row_spec(r, tl) for r in cots] + ([pl.BlockSpec((tl, add.shape[1]), lambda i: (i, 0))] if has_add else []))
    return pl.pallas_call(
        body, name=name, grid=(length // tl,), in_specs=specs,
        out_specs=dr_specs + [_whole_spec(p) for p in pars],
        out_shape=dr_shapes + [jax.ShapeDtypeStruct(p.shape, F32) for p in pars],
        compiler_params=_params(("arbitrary",)),
    )(*ins)


def whole_fwd(fn, ins, out_shapes, *, name):
    n_in = len(ins)

    def body(*refs):
        res = fn(*[r[...] for r in refs[:n_in]])
        for o_ref, v in zip(refs[n_in:], res, strict=True):
            o_ref[...] = v

    return pl.pallas_call(body, name=name, out_shape=[jax.ShapeDtypeStruct(s, F32) for s in out_shapes],
                          compiler_params=_params())(*ins)


def whole_bwd(fn, ins, n_diff, cots, *, name):
    n_in, n_t = len(ins), len(cots)

    def body(*refs):
        vals = [r[...] for r in refs[:n_in]]
        t_vals = [r[...] for r in refs[n_in:n_in + n_t]]
        _, vjp = jax.vjp(lambda *d: fn(*d, *vals[n_diff:]), *vals[:n_diff])
        for o_ref, g in zip(refs[n_in + n_t:], vjp(tuple(t_vals)), strict=True):
            o_ref[...] = g

    return pl.pallas_call(body, name=name, out_shape=[jax.ShapeDtypeStruct(a.shape, F32) for a in ins[:n_diff]],
                          compiler_params=_params())(*ins, *cots)


CONV_ROWS = 256


def _conv_geometry(x, w, cw, off):
    width = w.shape[1]
    x = Cols(x, width, 0)
    length = x.arr.shape[0]
    taps = w.shape[0]
    pad = -(-(taps - 1) // 8) * 8
    assert off % cw == 0 and width % cw == 0, (off, width, cw)
    return x, length, taps, pad, off // cw


def _conv_taps(xp_ref, w_ref, base, taps, pad, init):
    acc = init
    for k in range(taps):
        acc = acc + w_ref[k:k + 1, :] * xp_ref[pl.ds(base + pad - (taps - 1) + k, init.shape[0]), :]
    return acc


def conv_fwd(x, w, b, *, act, name, off=0, cw=LANE, out_dtype=F32):
    x, length, taps, pad, jb = _conv_geometry(x, w, cw, off)
    rc = min(CONV_ROWS, length)

    def body(x_ref, w_ref, b_ref, o_ref, xp_ref):
        xp_ref[0:pad, :] = jnp.zeros((pad, cw), F32)
        xp_ref[pad:pad + length, :] = x_ref[...].astype(F32)

        def chunk(r, carry):
            base = pl.multiple_of(r * rc, rc)
            acc = _conv_taps(xp_ref, w_ref, base, taps, pad, jnp.broadcast_to(b_ref[...], (rc, cw)))
            if act:
                acc = _silu(acc)
            o_ref[pl.ds(base, rc), :] = acc.astype(o_ref.dtype)
            return carry

        lax.fori_loop(0, length // rc, chunk, 0)

    return pl.pallas_call(
        body, name=name, grid=(x.w // cw,),
        in_specs=[pl.BlockSpec((length, cw), lambda j: (0, jb + j)), pl.BlockSpec((taps, cw), lambda j: (0, j)),
                  pl.BlockSpec((1, cw), lambda j: (0, j))],
        out_specs=pl.BlockSpec((length, cw), lambda j: (0, j)),
        out_shape=jax.ShapeDtypeStruct((length, x.w), out_dtype),
        scratch_shapes=[pltpu.VMEM((pad + length, cw), F32)],
        compiler_params=_params(("parallel",)),
    )(x.arr, w, b)


def conv_bwd(x, w, b, dy, *, act, name, off=0, cw=LANE, dx_dtype=F32):
    x, length, taps, pad, jb = _conv_geometry(x, w, cw, off)
    rc = min(CONV_ROWS, length)

    def body(x_ref, w_ref, b_ref, dy_ref, dx_ref, dw_ref, db_ref, xp_ref, gp_ref):
        xp_ref[0:pad, :] = jnp.zeros((pad, cw), F32)
        xp_ref[pad:pad + length, :] = x_ref[...].astype(F32)
        gp_ref[length:length + pad, :] = jnp.zeros((pad, cw), F32)
        if act:
            def pre_chunk(r, carry):
                base = pl.multiple_of(r * rc, rc)
                pre = _conv_taps(xp_ref, w_ref, base, taps, pad, jnp.broadcast_to(b_ref[...], (rc, cw)))
                sig = jax.nn.sigmoid(pre)
                gp_ref[pl.ds(base, rc), :] = dy_ref[pl.ds(base, rc), :].astype(F32) * (sig * (1.0 + pre * (1.0 - sig)))
                return carry

            lax.fori_loop(0, length // rc, pre_chunk, 0)
        else:
            gp_ref[0:length, :] = dy_ref[...].astype(F32)
        dw_ref[...] = jnp.zeros((taps, cw), F32)
        db_ref[...] = jnp.zeros((1, cw), F32)

        def chunk(r, carry):
            base = pl.multiple_of(r * rc, rc)
            acc = jnp.zeros((rc, cw), F32)
            g = gp_ref[pl.ds(base, rc), :]
            for k in range(taps):
                acc = acc + w_ref[k:k + 1, :] * gp_ref[pl.ds(base + (taps - 1) - k, rc), :]
                xs = xp_ref[pl.ds(base + pad - (taps - 1) + k, rc), :]
                dw_ref[k:k + 1, :] += jnp.sum(g * xs, axis=0, keepdims=True)
            db_ref[...] += jnp.sum(g, axis=0, keepdims=True)
            dx_ref[pl.ds(base, rc), :] = acc.astype(dx_ref.dtype)
            return carry

        lax.fori_loop(0, length // rc, chunk, 0)

    dy = _cols(dy)
    assert dy.j == 0 and dy.w == x.w
    return pl.pallas_call(
        body, name=name, grid=(x.w // cw,),
        in_specs=[pl.BlockSpec((length, cw), lambda j: (0, jb + j)), pl.BlockSpec((taps, cw), lambda j: (0, j)),
                  pl.BlockSpec((1, cw), lambda j: (0, j)), pl.BlockSpec((length, cw), lambda j: (0, j))],
        out_specs=[pl.BlockSpec((length, cw), lambda j: (0, j)), pl.BlockSpec((taps, cw), lambda j: (0, j)),
                   pl.BlockSpec((1, cw), lambda j: (0, j))],
        out_shape=[jax.ShapeDtypeStruct((length, x.w), dx_dtype), jax.ShapeDtypeStruct((taps, x.w), F32),
                   jax.ShapeDtypeStruct((1, x.w), F32)],
        scratch_shapes=[pltpu.VMEM((pad + length, cw), F32), pltpu.VMEM((length + pad, cw), F32)],
        compiler_params=_params(("parallel",)),
    )(x.arr, w, b, dy.arr)


def _conv_transpose(xp_ref, gp_ref, w_ref, dx_ref, dw_ref, db_ref, length, taps, pad, rc):
    dw_ref[...] = jnp.zeros(dw_ref.shape, F32)
    db_ref[...] = jnp.zeros(db_ref.shape, F32)

    def chunk(r, carry):
        base = pl.multiple_of(r * rc, rc)
        acc = jnp.zeros((rc, LANE), F32)
        g = gp_ref[pl.ds(base, rc), :]
        for k in range(taps):
            acc = acc + w_ref[k:k + 1, :] * gp_ref[pl.ds(base + (taps - 1) - k, rc), :]
            xs = xp_ref[pl.ds(base + pad - (taps - 1) + k, rc), :]
            dw_ref[k:k + 1, :] += jnp.sum(g * xs, axis=0, keepdims=True)
        db_ref[...] += jnp.sum(g, axis=0, keepdims=True)
        dx_ref[pl.ds(base, rc), :] = acc.astype(dx_ref.dtype)
        return carry

    lax.fori_loop(0, length // rc, chunk, 0)


def ffn_conv_act(a, w, b, *, name):
    length, width = a.shape
    nb = width // 2 // LANE
    taps = w.shape[0]
    pad = -(-(taps - 1) // 8) * 8
    rc = min(CONV_ROWS, length)

    def body(ag_ref, au_ref, wg_ref, wu_ref, bg_ref, bu_ref, o_ref, xg_ref, xu_ref):
        for xp_ref, src in ((xg_ref, ag_ref), (xu_ref, au_ref)):
            xp_ref[0:pad, :] = jnp.zeros((pad, LANE), F32)
            xp_ref[pad:pad + length, :] = src[...]

        def chunk(r, carry):
            base = pl.multiple_of(r * rc, rc)
            gate = _conv_taps(xg_ref, wg_ref, base, taps, pad, jnp.broadcast_to(bg_ref[...], (rc, LANE)))
            up = _conv_taps(xu_ref, wu_ref, base, taps, pad, jnp.broadcast_to(bu_ref[...], (rc, LANE)))
            o_ref[pl.ds(base, rc), :] = (_silu(gate) * up).astype(o_ref.dtype)
            return carry

        lax.fori_loop(0, length // rc, chunk, 0)

    blk = lambda rows, up: pl.BlockSpec((rows, LANE), (lambda j: (0, nb + j)) if up else (lambda j: (0, j)))
    return pl.pallas_call(
        body, name=name, grid=(nb,),
        in_specs=[blk(length, False), blk(length, True), blk(taps, False), blk(taps, True), blk(1, False), blk(1, True)],
        out_specs=blk(length, False), out_shape=jax.ShapeDtypeStruct((length, width // 2), BF16),
        scratch_shapes=[pltpu.VMEM((pad + length, LANE), F32), pltpu.VMEM((pad + length, LANE), F32)],
        compiler_params=_params(("parallel",)),
    )(a, a, w, w, b, b)


def ffn_conv_act_bwd(a, w, b, dact, *, name):
    length, width = a.shape
    nb = width // 2 // LANE
    taps = w.shape[0]
    pad = -(-(taps - 1) // 8) * 8
    rc = min(CONV_ROWS, length)

    def body(ag_ref, au_ref, wg_ref, wu_ref, bg_ref, bu_ref, dy_ref, da_ref, dw_ref, db_ref, xg_ref, xu_ref, gp_ref):
        for xp_ref, src in ((xg_ref, ag_ref), (xu_ref, au_ref)):
            xp_ref[0:pad, :] = jnp.zeros((pad, LANE), F32)
            xp_ref[pad:pad + length, :] = src[...]
        gp_ref[length:length + pad, :] = jnp.zeros((pad, LANE), F32)
        is_gate = pl.program_id(0) == 0

        def pre_chunk(r, carry):
            base = pl.multiple_of(r * rc, rc)
            gate = _conv_taps(xg_ref, wg_ref, base, taps, pad, jnp.broadcast_to(bg_ref[...], (rc, LANE)))
            up = _conv_taps(xu_ref, wu_ref, base, taps, pad, jnp.broadcast_to(bu_ref[...], (rc, LANE)))
            sig = jax.nn.sigmoid(gate)
            d_gate = up * (sig * (1.0 + gate * (1.0 - sig)))
            gp_ref[pl.ds(base, rc), :] = dy_ref[pl.ds(base, rc), :] * jnp.where(is_gate, d_gate, gate * sig)
            return carry

        lax.fori_loop(0, length // rc, pre_chunk, 0)

        @pl.when(is_gate)
        def _():
            _conv_transpose(xg_ref, gp_ref, wg_ref, da_ref, dw_ref, db_ref, length, taps, pad, rc)

        @pl.when(jnp.logical_not(is_gate))
        def _():
            _conv_transpose(xu_ref, gp_ref, wu_ref, da_ref, dw_ref, db_ref, length, taps, pad, rc)

    blk = lambda rows, up: pl.BlockSpec((rows, LANE), (lambda h, j: (0, nb + j)) if up else (lambda h, j: (0, j)))
    out = lambda rows: pl.BlockSpec((rows, LANE), lambda h, j: (0, h * nb + j))
    return pl.pallas_call(
        body, name=name, grid=(2, nb),
        in_specs=[blk(length, False), blk(length, True), blk(taps, False), blk(taps, True), blk(1, False), blk(1, True),
                  blk(length, False)],
        out_specs=[out(length), out(taps), out(1)],
        out_shape=[jax.ShapeDtypeStruct((length, width), BF16), jax.ShapeDtypeStruct((taps, width), F32),
                   jax.ShapeDtypeStruct((1, width), F32)],
        scratch_shapes=[pltpu.VMEM((pad + length, LANE), F32), pltpu.VMEM((pad + length, LANE), F32),
                        pltpu.VMEM((length + pad, LANE), F32)],
        compiler_params=_params(("parallel", "parallel")),
    )(a, a, w, w, b, b, dact)


def _retention_consts():
    h = np.arange(RET_HEADS, dtype=np.float32)
    log_g = np.log1p(-(2.0 ** (-5.0 - h))).astype(np.float32)
    idx = np.arange(CHUNK, dtype=np.float32)
    diff = idx[:, None] - idx[None, :]
    intra = np.where(diff[None] >= 0, np.exp(np.maximum(diff, 0.0)[None] * log_g[:, None, None]), 0.0)
    zeta = np.exp((CHUNK - 1 - idx)[None, :] * log_g[:, None])
    xi = np.exp((idx + 1)[None, :] * log_g[:, None])
    decay = np.exp(CHUNK * log_g)
    zeta = np.broadcast_to(zeta[:, :, None], (RET_HEADS, CHUNK, RET_DK))
    xi = np.broadcast_to(xi[:, :, None], (RET_HEADS, CHUNK, RET_DV))
    return (jnp.asarray(intra, F32), jnp.asarray(zeta, F32), jnp.asarray(xi, F32), [float(d) for d in decay])


def _rotary_tables(length):
    inv = ROPE_BASE ** (-jnp.arange(0, RET_DK, 2, dtype=F32) / RET_DK)
    ang = jnp.arange(length).astype(F32)[:, None] * inv[None, :]
    cos, sin = jnp.cos(ang), jnp.sin(ang)
    return jnp.concatenate([cos, cos], axis=1), jnp.concatenate([-sin, sin], axis=1)


def _rot(x, cos2, sin2):
    return x * cos2 + pltpu.roll(x, RET_DK // 2, 1) * sin2


def _rot_t(y, cos2, sin2):
    return y * cos2 + pltpu.roll(y * sin2, RET_DK // 2, 1)


def _head_decay(h, decays):
    d = jnp.float32(decays[-1])
    for i in range(len(decays) - 2, -1, -1):
        d = jnp.where(h == i, jnp.float32(decays[i]), d)
    return d


def _ret_chunk(q, k, v, g, state, intra, zeta, xi, decay):
    s = _mm_nt(q, k) * intra
    kv = _mm_tn(k * zeta, v)
    o = _mm(s, v) + _mm(q, state) * xi
    oc = o - jnp.mean(o, axis=-1, keepdims=True)
    r = oc * lax.rsqrt(jnp.mean(oc * oc, axis=-1, keepdims=True) + EPS)
    return _silu(g) * r, state * decay + kv


def _ret_specs(rev, nc):
    def cidx(c):
        return nc - 1 - c if rev else c
    return [
        pl.BlockSpec((CHUNK, RET_DK), lambda h, c: (cidx(c), h)),
        pl.BlockSpec((CHUNK, RET_DK), lambda h, c: (cidx(c), RET_HEADS + h)),
        pl.BlockSpec((CHUNK, RET_DV), lambda h, c: (cidx(c), 4 + h)),
        pl.BlockSpec((CHUNK, RET_DV), lambda h, c: (cidx(c), 8 + h)),
        pl.BlockSpec((CHUNK, RET_DK), lambda h, c: (cidx(c), 0)),
        pl.BlockSpec((CHUNK, RET_DK), lambda h, c: (cidx(c), 0)),
        pl.BlockSpec((1, CHUNK, CHUNK), lambda h, c: (h, 0, 0)),
        pl.BlockSpec((1, CHUNK, RET_DK), lambda h, c: (h, 0, 0)),
        pl.BlockSpec((1, CHUNK, RET_DV), lambda h, c: (h, 0, 0)),
    ], cidx


def retention_fwd(proj, cos2, sin2):
    length = proj.shape[0]
    nc = length // CHUNK
    intra, zeta, xi, decays = _retention_consts()
    specs, _ = _ret_specs(False, nc)
    scale = RET_DK ** -0.5

    def body(q_ref, k_ref, v_ref, g_ref, cos_ref, sin_ref, intra_ref, zeta_ref, xi_ref, y_ref, st_ref, state):
        h, c = pl.program_id(0), pl.program_id(1)

        @pl.when(c == 0)
        def _():
            state[...] = jnp.zeros_like(state)

        q = _rot(q_ref[...], cos_ref[...], sin_ref[...])
        k = _rot(k_ref[...], cos_ref[...], sin_ref[...]) * scale
        st_ref[0, 0] = state[...]
        y, new_state = _ret_chunk(q, k, v_ref[...], g_ref[...], state[...], intra_ref[0], zeta_ref[0], xi_ref[0],
                                  _head_decay(h, decays))
        y_ref[...] = y.astype(y_ref.dtype)
        state[...] = new_state

    return pl.pallas_call(
        body, name="retention_fwd", grid=(RET_HEADS, nc), in_specs=specs,
        out_specs=[pl.BlockSpec((CHUNK, RET_DV), lambda h, c: (c, h)),
                   pl.BlockSpec((1, 1, RET_DK, RET_DV), lambda h, c: (h, c, 0, 0))],
        out_shape=[jax.ShapeDtypeStruct((length, RET_HEADS * RET_DV), BF16),
                   jax.ShapeDtypeStruct((RET_HEADS, nc, RET_DK, RET_DV), F32)],
        scratch_shapes=[pltpu.VMEM((RET_DK, RET_DV), F32)],
        compiler_params=_params(("parallel", "arbitrary")),
    )(proj, proj, proj, proj, cos2, sin2, intra, zeta, xi)


def retention_bwd(proj, cos2, sin2, states, dmix):
    length = proj.shape[0]
    nc = length // CHUNK
    intra, zeta, xi, decays = _retention_consts()
    specs, cidx = _ret_specs(True, nc)
    scale = RET_DK ** -0.5

    def body(q_ref, k_ref, v_ref, g_ref, cos_ref, sin_ref, intra_ref, zeta_ref, xi_ref, st_ref, dy_ref,
             dq_ref, dk_ref, dv_ref, dg_ref, dstate):
        h, c = pl.program_id(0), pl.program_id(1)

        @pl.when(c == 0)
        def _():
            dstate[...] = jnp.zeros_like(dstate)

        cos2v, sin2v = cos_ref[...], sin_ref[...]
        q = _rot(q_ref[...], cos2v, sin2v)
        k = _rot(k_ref[...], cos2v, sin2v) * scale
        decay = _head_decay(h, decays)
        intra_v, zeta_v, xi_v = intra_ref[0], zeta_ref[0], xi_ref[0]
        _, vjp = jax.vjp(lambda q, k, v, g, s: _ret_chunk(q, k, v, g, s, intra_v, zeta_v, xi_v, decay),
                         q, k, v_ref[...], g_ref[...], st_ref[0, 0])
        dq, dk, dv, dg, ds = vjp((dy_ref[...].astype(F32), dstate[...]))
        dq_ref[...] = _rot_t(dq, cos2v, sin2v).astype(dq_ref.dtype)
        dk_ref[...] = _rot_t(dk * scale, cos2v, sin2v).astype(dk_ref.dtype)
        dv_ref[...] = dv.astype(dv_ref.dtype)
        dg_ref[...] = dg.astype(dg_ref.dtype)
        dstate[...] = ds

    specs = specs + [pl.BlockSpec((1, 1, RET_DK, RET_DV), lambda h, c: (h, cidx(c), 0, 0)),
                     pl.BlockSpec((CHUNK, RET_DV), lambda h, c: (cidx(c), h))]
    return pl.pallas_call(
        body, name="retention_bwd", grid=(RET_HEADS, nc), in_specs=specs,
        out_specs=[pl.BlockSpec((CHUNK, RET_DK), lambda h, c: (cidx(c), h)),
                   pl.BlockSpec((CHUNK, RET_DK), lambda h, c: (cidx(c), h)),
                   pl.BlockSpec((CHUNK, RET_DV), lambda h, c: (cidx(c), h)),
                   pl.BlockSpec((CHUNK, RET_DV), lambda h, c: (cidx(c), h))],
        out_shape=[jax.ShapeDtypeStruct((length, RET_HEADS * RET_DK), BF16),
                   jax.ShapeDtypeStruct((length, RET_HEADS * RET_DK), BF16),
                   jax.ShapeDtypeStruct((length, RET_HEADS * RET_DV), BF16),
                   jax.ShapeDtypeStruct((length, RET_HEADS * RET_DV), BF16)],
        scratch_shapes=[pltpu.VMEM((RET_DK, RET_DV), F32)],
        compiler_params=_params(("parallel", "arbitrary")),
    )(proj, proj, proj, proj, cos2, sin2, intra, zeta, xi, states, dmix)


def _ssd_consts():
    tri = np.tril(np.ones((CHUNK, CHUNK), np.float32))
    expand = np.zeros((LANE, SSM_DINNER), np.float32)
    for h in range(SSM_HEADS):
        expand[h, h * SSM_P:(h + 1) * SSM_P] = 1.0
    return jnp.asarray(tri), jnp.asarray(tri.T.copy()), jnp.asarray(expand)


def _ssd_chunk(xs, bm, cm, dtr, z, state, dt_bias, a_log, d_skip, norm_w, tri, tri_t, expand):
    gw = SSM_DINNER // SSM_GROUPS
    dt = jax.nn.softplus(dtr + dt_bias)
    da = dt * (-jnp.exp(a_log))
    acs = _dot_hi(tri, da)
    acs_t = _dot_hi_tn(da, tri_t)
    dt_x = _dot_hi(dt, expand)
    da_x = _dot_hi(da, expand)
    acs_x = _dot_hi(tri, da_x)
    tot_x = jnp.sum(da_x, axis=0, keepdims=True)
    x_dt = xs * dt_x
    x_dec = x_dt * jnp.exp(tot_x - acs_x)
    e_acs = jnp.exp(acs_x)
    e_tot = jnp.exp(tot_x)
    lane = lax.broadcasted_iota(jnp.int32, (CHUNK, LANE), 1)
    sub = lax.broadcasted_iota(jnp.int32, (CHUNK, LANE), 0)
    causal = sub >= lane
    ys, new_states = [], []
    for g in range(SSM_GROUPS):
        bg = bm[:, g * SSM_N:(g + 1) * SSM_N]
        cg = cm[:, g * SSM_N:(g + 1) * SSM_N]
        sg = state[:, g * gw:(g + 1) * gw]
        cb = _mm_nt(cg, bg)
        y_off = _mm(cg, sg) * e_acs[:, g * gw:(g + 1) * gw]
        new_states.append(sg * e_tot[:, g * gw:(g + 1) * gw] + _mm_tn(bg, x_dec[:, g * gw:(g + 1) * gw]))
        pairs = []
        for p in range(gw // LANE):
            hp = g * (gw // LANE) + p
            xp = x_dt[:, hp * LANE:(hp + 1) * LANE]
            halves = []
            for head in (2 * hp, 2 * hp + 1):
                col = jnp.sum(jnp.where(lane == head, acs, 0.0), axis=1, keepdims=True)
                row = jnp.sum(jnp.where(sub == head, acs_t, 0.0), axis=0, keepdims=True)
                decay = jnp.exp(jnp.where(causal, col - row, -1e30))
                halves.append(_mm(cb * decay, xp))
            pairs.append(jnp.where(lane < SSM_P, halves[0], halves[1]))
        ys.append(jnp.concatenate(pairs, axis=1) + y_off)
    d_x = jnp.mean(_dot_hi(jnp.broadcast_to(d_skip, (8, LANE)), expand), axis=0, keepdims=True)
    y = (jnp.concatenate(ys, axis=1) + d_x * xs) * _silu(z)
    normed = []
    for g in range(SSM_GROUPS):
        yg = y[:, g * gw:(g + 1) * gw]
        normed.append(yg * lax.rsqrt(jnp.mean(yg * yg, axis=-1, keepdims=True) + EPS))
    return jnp.concatenate(normed, axis=1) * norm_w, jnp.concatenate(new_states, axis=1)


XBC = SSM_DINNER + 2 * SSM_GROUPS * SSM_N


def _ssd_specs(rev, nc):
    def cidx(c):
        return nc - 1 - c if rev else c
    row = lambda w, j: pl.BlockSpec((CHUNK, w), lambda c: (cidx(c), j))
    whole = lambda shape: pl.BlockSpec(shape, lambda c: (0,) * len(shape))
    return [row(XBC, 0), row(LANE, 5632 // LANE), row(SSM_DINNER, 3),
            whole((1, LANE)), whole((1, LANE)), whole((1, LANE)), whole((1, SSM_DINNER)),
            whole((CHUNK, CHUNK)), whole((CHUNK, CHUNK)), whole((LANE, SSM_DINNER))], cidx


def ssd_fwd(xbc, proj, dt_bias, a_log, d_skip, norm_w):
    length = proj.shape[0]
    nc = length // CHUNK
    tri, tri_t, expand = _ssd_consts()
    specs, _ = _ssd_specs(False, nc)

    def body(xbc_ref, dt_ref, z_ref, dtb_ref, alog_ref, d_ref, nw_ref, tri_ref, trit_ref, e_ref, y_ref, st_ref, state):
        @pl.when(pl.program_id(0) == 0)
        def _():
            state[...] = jnp.zeros_like(state)

        st_ref[0] = state[...]
        y, new_state = _ssd_chunk(
            xbc_ref[:, 0:SSM_DINNER], xbc_ref[:, SSM_DINNER:SSM_DINNER + 256], xbc_ref[:, SSM_DINNER + 256:XBC],
            dt_ref[...], z_ref[...], state[...], dtb_ref[...], alog_ref[...], d_ref[...], nw_ref[...],
            tri_ref[...], trit_ref[...], e_ref[...])
        y_ref[...] = y.astype(y_ref.dtype)
        state[...] = new_state

    return pl.pallas_call(
        body, name="ssd_fwd", grid=(nc,), in_specs=specs,
        out_specs=[pl.BlockSpec((CHUNK, SSM_DINNER), lambda c: (c, 0)),
                   pl.BlockSpec((1, SSM_N, SSM_DINNER), lambda c: (c, 0, 0))],
        out_shape=[jax.ShapeDtypeStruct((length, SSM_DINNER), BF16),
                   jax.ShapeDtypeStruct((nc, SSM_N, SSM_DINNER), F32)],
        scratch_shapes=[pltpu.VMEM((SSM_N, SSM_DINNER), F32)],
        compiler_params=_params(("arbitrary",)),
    )(xbc, proj, proj, dt_bias, a_log, d_skip, norm_w, tri, tri_t, expand)


def ssd_bwd(xbc, proj, dt_bias, a_log, d_skip, norm_w, states, dmix):
    length = proj.shape[0]
    nc = length // CHUNK
    tri, tri_t, expand = _ssd_consts()
    specs, cidx = _ssd_specs(True, nc)

    def body(xbc_ref, dt_ref, z_ref, dtb_ref, alog_ref, d_ref, nw_ref, tri_ref, trit_ref, e_ref, st_ref, dy_ref,
             dxbc_ref, ddt_ref, dz_ref, ddtb_ref, dalog_ref, dd_ref, dnw_ref, dstate):
        c = pl.program_id(0)

        @pl.when(c == 0)
        def _():
            dstate[...] = jnp.zeros_like(dstate)

        tri_v, trit_v, e_v = tri_ref[...], trit_ref[...], e_ref[...]
        _, vjp = jax.vjp(
            lambda *a: _ssd_chunk(*a, tri_v, trit_v, e_v),
            xbc_ref[:, 0:SSM_DINNER], xbc_ref[:, SSM_DINNER:SSM_DINNER + 256], xbc_ref[:, SSM_DINNER + 256:XBC],
            dt_ref[...], z_ref[...], st_ref[0], dtb_ref[...], alog_ref[...], d_ref[...], nw_ref[...])
        dxs, dbm, dcm, ddt, dz, ds, ddtb, dalog, dd, dnw = vjp((dy_ref[...].astype(F32), dstate[...]))
        dxbc_ref[:, 0:SSM_DINNER] = dxs
        dxbc_ref[:, SSM_DINNER:SSM_DINNER + 256] = dbm
        dxbc_ref[:, SSM_DINNER + 256:XBC] = dcm
        ddt_ref[...] = ddt.astype(ddt_ref.dtype)
        dz_ref[...] = dz.astype(dz_ref.dtype)
        dstate[...] = ds
        for r, d in ((ddtb_ref, ddtb), (dalog_ref, dalog), (dd_ref, dd), (dnw_ref, dnw)):
            @pl.when(c == 0)
            def _(r=r, d=d):
                r[...] = d

            @pl.when(c > 0)
            def _(r=r, d=d):
                r[...] += d

    whole = lambda shape: pl.BlockSpec(shape, lambda c: (0,) * len(shape))
    specs = specs + [pl.BlockSpec((1, SSM_N, SSM_DINNER), lambda c: (cidx(c), 0, 0)),
                     pl.BlockSpec((CHUNK, SSM_DINNER), lambda c: (cidx(c), 1))]
    return pl.pallas_call(
        body, name="ssd_bwd", grid=(nc,), in_specs=specs,
        out_specs=[pl.BlockSpec((CHUNK, XBC), lambda c: (cidx(c), 0)), pl.BlockSpec((CHUNK, LANE), lambda c: (cidx(c), 0)),
                   pl.BlockSpec((CHUNK, SSM_DINNER), lambda c: (cidx(c), 0)),
                   whole((1, LANE)), whole((1, LANE)), whole((1, LANE)), whole((1, SSM_DINNER))],
        out_shape=[jax.ShapeDtypeStruct((length, XBC), F32), jax.ShapeDtypeStruct((length, LANE), BF16),
                   jax.ShapeDtypeStruct((length, SSM_DINNER), BF16),
                   jax.ShapeDtypeStruct((1, LANE), F32), jax.ShapeDtypeStruct((1, LANE), F32),
                   jax.ShapeDtypeStruct((1, LANE), F32), jax.ShapeDtypeStruct((1, SSM_DINNER), F32)],
        scratch_shapes=[pltpu.VMEM((SSM_N, SSM_DINNER), F32)],
        compiler_params=_params(("arbitrary",)),
    )(xbc, proj, proj, dt_bias, a_log, d_skip, norm_w, tri, tri_t, expand, states, dmix)


def _cmul(ar, ai, br, bi):
    return ar * br - ai * bi, ar * bi + ai * br


def s5_scan(b_re, b_im, a_re, a_im, *, reverse=False, states=None, name, lw=256):
    length, lanes = b_re.shape
    nk = length // SCAN_SEG
    with_da = states is not None
    assert reverse or not with_da

    def shift(v):
        sub = lax.broadcasted_iota(jnp.int32, v.shape, 0)
        if reverse:
            return jnp.where(sub == SCAN_SEG - 1, 0.0, pltpu.roll(v, SCAN_SEG - 1, 0))
        return jnp.where(sub == 0, 0.0, pltpu.roll(v, 1, 0))

    def body(*refs):
        if with_da:
            bre_ref, bim_ref, are_ref, aim_ref, sre_ref, sim_ref, xre_ref, xim_ref, dare_ref, daim_ref = refs
        else:
            bre_ref, bim_ref, are_ref, aim_ref, xre_ref, xim_ref = refs
        ar = jnp.broadcast_to(are_ref[...], (SCAN_SEG, lw))
        ai = jnp.broadcast_to(aim_ref[...], (SCAN_SEG, lw))

        def tile(i):
            k = (nk - 1 - i) if reverse else i
            return pl.ds(pl.multiple_of(k * SCAN_SEG, SCAN_SEG), SCAN_SEG)

        def local(i, carry):
            xr, xi, pr, pi = carry
            rows = tile(i)
            mr, mi = _cmul(ar, ai, xr, xi)
            xr, xi = mr + bre_ref[rows, :], mi + bim_ref[rows, :]
            xre_ref[rows, :] = xr
            xim_ref[rows, :] = xi
            pr, pi = _cmul(ar, ai, pr, pi)
            return xr, xi, pr, pi

        zero = jnp.zeros((SCAN_SEG, lw), F32)
        one = jnp.ones((SCAN_SEG, lw), F32)
        er, ei, pr, pi = lax.fori_loop(0, nk, local, (zero, zero, one, zero))
        cr, ci = zero, zero
        for _ in range(SCAN_SEG - 1):
            mr, mi = _cmul(pr, pi, cr, ci)
            cr, ci = shift(er + mr), shift(ei + mi)

        def fix(i, carry):
            pr, pi, dr, di = carry
            rows = tile(i)
            pr, pi = _cmul(ar, ai, pr, pi)
            mr, mi = _cmul(pr, pi, cr, ci)
            xr, xi = xre_ref[rows, :] + mr, xim_ref[rows, :] + mi
            xre_ref[rows, :] = xr
            xim_ref[rows, :] = xi
            if with_da:
                k = nk - 1 - i
                prev = pl.ds(pl.multiple_of(jnp.maximum(k - 1, 0) * SCAN_SEG, SCAN_SEG), SCAN_SEG)
                last = pl.ds((nk - 1) * SCAN_SEG, SCAN_SEG)
                sub = lax.broadcasted_iota(jnp.int32, (SCAN_SEG, lw), 0)
                wr = jnp.where(sub == 0, 0.0, pltpu.roll(sre_ref[last, :], 1, 0))
                wi = jnp.where(sub == 0, 0.0, pltpu.roll(sim_ref[last, :], 1, 0))
                sr = jnp.where(k == 0, wr, sre_ref[prev, :])
                si = jnp.where(k == 0, wi, sim_ref[prev, :])
                dr, di = dr + xr * sr + xi * si, di + xi * sr - xr * si
            return pr, pi, dr, di

        _, _, dr, di = lax.fori_loop(0, nk, fix, (one, zero, zero, zero))
        if with_da:
            dare_ref[...] = jnp.sum(dr, axis=0, keepdims=True)
            daim_ref[...] = jnp.sum(di, axis=0, keepdims=True)

    col = pl.BlockSpec((length, lw), lambda j: (0, j))
    vec = pl.BlockSpec((1, lw), lambda j: (0, j))
    ins = [b_re, b_im, a_re, a_im] + (list(states) if with_da else [])
    in_specs = [col, col, vec, vec] + ([col, col] if with_da else [])
    out_specs = [col, col] + ([vec, vec] if with_da else [])
    out_shape = [jax.ShapeDtypeStruct((length, lanes), F32)] * 2 + ([jax.ShapeDtypeStruct((1, lanes), F32)] * 2 if with_da else [])
    return pl.pallas_call(
        body, name=name, grid=(lanes // lw,), in_specs=in_specs, out_specs=out_specs, out_shape=out_shape,
        compiler_params=_params(("parallel",)),
    )(*ins)


def _seg_interleave(v):
    length = v.shape[0]
    return v.reshape(SCAN_SEG, length // SCAN_SEG, -1).transpose(1, 0, 2).reshape(length, -1)


def _seg_deinterleave(v):
    length = v.shape[0]
    return v.reshape(length // SCAN_SEG, SCAN_SEG, -1).transpose(1, 0, 2).reshape(length, -1)


def _block_diag(m):
    eye = jnp.eye(S5_GROUPS, dtype=m.dtype)
    return (m.reshape(S5_GROUPS, S5_GROUP, 1, S5_STATE) * eye[:, None, :, None]).reshape(S5_GROUPS * S5_GROUP, S5_LANES)


def _block_diag_take(full):
    idx = jnp.arange(S5_GROUPS)
    blocks = full.reshape(S5_GROUPS, S5_GROUP, S5_GROUPS, S5_STATE)[idx, :, idx, :]
    return blocks.reshape(S5_GROUPS * S5_GROUP, S5_STATE)


def _s5_prep(a_re, a_im, log_step, b_re, b_im, rep):
    step = jnp.exp(log_step)
    mag = jnp.exp(a_re * step)
    ab_re = mag * jnp.cos(a_im * step)
    ab_im = mag * jnp.sin(a_im * step)
    den = a_re * a_re + a_im * a_im
    f_re = ((ab_re - 1.0) * a_re + ab_im * a_im) / den
    f_im = (ab_im * a_re - (ab_re - 1.0) * a_im) / den
    fr, fi = _dot_hi(rep, f_re), _dot_hi(rep, f_im)
    return ab_re, ab_im, fr * b_re - fi * b_im, fr * b_im + fi * b_re


def _rms(x, g):
    return (x * lax.rsqrt(jnp.mean(x * x, axis=-1, keepdims=True) + EPS) * g,)


def _ffn_act(gate, up):
    return (_silu(gate) * up,)


def _glu(a, g):
    return (a * jax.nn.sigmoid(g),)


def _ln_silu(x, g, b):
    xc = x - jnp.mean(x, axis=-1, keepdims=True)
    var = jnp.mean(xc * xc, axis=-1, keepdims=True)
    return (_silu(xc * lax.rsqrt(var + EPS) * g + b),)


def _s5_post(y, u, d_skip, glu_w):
    s = jax.nn.gelu(y + d_skip * u)
    return (s * jax.nn.sigmoid(_mm(s, glu_w)),)


def loss_head(x, tgt, g, *, tl=512):
    length, d = x.shape
    tl = min(tl, length)

    def body(x_ref, t_ref, g_ref, loss_ref, dx_ref, dg_ref):
        i = pl.program_id(0)
        y, vjp = jax.vjp(lambda x, g: _rms(x, g)[0], x_ref[...], g_ref[...])
        err = y - t_ref[...]
        dx, dg = vjp(err * (1.0 / d))
        dx_ref[...] = dx
        part = jnp.broadcast_to(0.5 * jnp.sum(jnp.mean(err * err, axis=-1, keepdims=True), axis=0, keepdims=True), (1, LANE))

        @pl.when(i == 0)
        def _():
            loss_ref[...] = part
            dg_ref[...] = dg

        @pl.when(i > 0)
        def _():
            loss_ref[...] += part
            dg_ref[...] += dg

    row = pl.BlockSpec((tl, d), lambda i: (i, 0))
    return pl.pallas_call(
        body, name="loss_head", grid=(length // tl,),
        in_specs=[row, row, pl.BlockSpec((1, d), lambda i: (0, 0))],
        out_specs=[pl.BlockSpec((1, LANE), lambda i: (0, 0)), row, pl.BlockSpec((1, d), lambda i: (0, 0))],
        out_shape=[jax.ShapeDtypeStruct((1, LANE), F32), jax.ShapeDtypeStruct((length, d), F32),
                   jax.ShapeDtypeStruct((1, d), F32)],
        compiler_params=_params(("arbitrary",)),
    )(x, tgt, g)


def _pad_heads(v):
    return jnp.pad(v, ((0, 0), (0, LANE - v.shape[1])))


def local_step(x, tgt, w, late_weights=None, early_reduce=None):
    length = x.shape[0]
    cos2, sin2 = _rotary_tables(length)
    grads = {}
    w = dict(w)

    def rms_fwd(xin, g, name):
        return rowwise_fwd(_rms, [xin], [], [g], [], [(D_MODEL, BF16)], name=name, tl=512)[0]

    def rms_bwd(xin, g, dh, dxo, name):
        return rowwise_bwd(_rms, [xin], [], [g], [], [dh], [F32], name=name, tl=512, add=dxo)

    def ffn_fwd(i, xin):
        hf = rms_fwd(xin, w["ffn_norm"][i:i + 1], f"ffn{i}_norm")
        a = matmul(hf, w["ffn_w_up"][i], name=f"ffn{i}_up")
        act = ffn_conv_act(a, w["ffn_dw_w"][i], w["ffn_dw_b"][i:i + 1], name=f"ffn{i}_conv_act")
        return matmul(act, w["ffn_w_down"][i], res=xin, name=f"ffn{i}_down"), (hf, a, act)

    def ffn_bwd(i, xin, saved, dxo):
        hf, a, act = saved
        dact = matmul(dxo, w["ffn_w_down"][i], tb=True, name=f"ffn{i}_down_dx")
        dw_down = matmul(act, dxo, ta=True, name=f"ffn{i}_down_dw")
        da, ddw_w, ddw_b = ffn_conv_act_bwd(a, w["ffn_dw_w"][i], w["ffn_dw_b"][i:i + 1], dact,
                                            name=f"ffn{i}_conv_act_bwd")
        dw_up = matmul(hf, da, ta=True, name=f"ffn{i}_up_dw")
        dhf = matmul(da, w["ffn_w_up"][i], tb=True, name=f"ffn{i}_up_dx")
        dxin, dnorm = rms_bwd(xin, w["ffn_norm"][i:i + 1], dhf, dxo, f"ffn{i}_norm_bwd")
        return dxin, dict(ffn_norm=dnorm, ffn_w_up=dw_up, ffn_dw_w=ddw_w, ffn_dw_b=ddw_b, ffn_w_down=dw_down)

    w_in_e = jnp.pad(w["e_w_in"][0], ((0, 0), (0, EVEN_IN_PAD - EVEN_IN)))
    conv_w_e, conv_b_e = w["e_conv_w"][0], w["e_conv_b"]
    dt_bias, a_log, d_skip = _pad_heads(w["e_dt_bias"]), _pad_heads(w["e_a_log"]), _pad_heads(w["e_d"])
    xbc_off = 4 * D_MODEL

    hn0 = rms_fwd(x, w["mix_norm"][0:1], "mix0_norm")
    proj0 = matmul(hn0, w_in_e, name="even_in")
    y_ret, ret_states = retention_fwd(proj0, cos2, sin2)
    xbc = conv_fwd(proj0, conv_w_e, conv_b_e, act=True, off=xbc_off, name="ssd_conv")
    y_ssm, ssd_states = ssd_fwd(xbc, proj0, dt_bias, a_log, d_skip, w["e_ssm_norm"])
    mix0 = jnp.concatenate([y_ret, y_ssm], axis=1)
    if late_weights is not None:
        w.update(late_weights(y_ssm))
    w_out_e = w["e_w_out"][0]
    x1 = matmul(mix0, w_out_e, res=x, name="even_out")
    x2, ffn0_saved = ffn_fwd(0, x1)

    w_in_o, w_out_o, glu_w = w["o_w_in"][0], w["o_w_out"][0], w["o_glu_w"][0]
    dw_w_o, dw_b_o, ln_g, ln_b, d_o = w["o_dw_w"][0], w["o_dw_b"], w["o_ln_g"], w["o_ln_b"], w["o_d"]
    rep = jnp.asarray(np.repeat(np.eye(S5_GROUPS, dtype=np.float32), S5_GROUP, axis=0))
    rows_gc = (S5_GROUPS * S5_GROUP, S5_STATE)
    prep_in = [w["o_a_re"][0], w["o_a_im"][0], w["o_log_step"].reshape(S5_GROUPS, 1),
               w["o_b_re"][0].transpose(0, 2, 1).reshape(rows_gc), w["o_b_im"][0].transpose(0, 2, 1).reshape(rows_gc), rep]
    ab_re, ab_im, bb_re, bb_im = whole_fwd(
        _s5_prep, prep_in, [(S5_GROUPS, S5_STATE)] * 2 + [rows_gc] * 2, name="s5_prep")
    a_re_row, a_im_row = ab_re.reshape(1, S5_LANES), ab_im.reshape(1, S5_LANES)
    b_re_bd, b_im_bd = _block_diag(bb_re).astype(BF16), _block_diag(bb_im).astype(BF16)
    c_re_bd = _block_diag(w["o_c_re"][0].reshape(rows_gc)).astype(BF16)
    c_im_neg_bd = _block_diag(-w["o_c_im"][0].reshape(rows_gc)).astype(BF16)

    hn1 = rms_fwd(x2, w["mix_norm"][1:2], "mix1_norm")
    proj1 = matmul(hn1, w_in_o, name="odd_in")
    half = D_MODEL // 2
    c_glu = rowwise_fwd(_glu, [Cols(proj1, half, 0), Cols(proj1, half, 1)], [], [], [], [(half, F32)],
                        name="conf_glu", tl=512)[0]
    c_conv = conv_fwd(c_glu, dw_w_o, dw_b_o, act=False, name="conf_conv")
    c_out = rowwise_fwd(_ln_silu, [c_conv], [], [ln_g, ln_b], [], [(half, BF16)], name="conf_ln", tl=512)[0]
    u_seg = _seg_interleave(proj1[:, 2 * half:])
    bu_re = matmul(u_seg, b_re_bd, name="s5_bu_re")
    bu_im = matmul(u_seg, b_im_bd, name="s5_bu_im")
    xs_re, xs_im = s5_scan(bu_re, bu_im, a_re_row, a_im_row, name="s5_scan")
    y_im = matmul(xs_im, c_im_neg_bd, tb=True, name="s5_y_im")
    y_s5 = _seg_deinterleave(matmul(xs_re, c_re_bd, tb=True, res=y_im, name="s5_y_re"))
    s_out = rowwise_fwd(_s5_post, [y_s5, Cols(proj1, half, 2)], [], [d_o, glu_w], [], [(half, BF16)],
                        name="s5_post", tl=512)[0]
    mix1 = jnp.concatenate([c_out, s_out], axis=1)
    x3 = matmul(mix1, w_out_o, res=x2, name="odd_out")
    x4, ffn1_saved = ffn_fwd(1, x3)

    loss, dx4, dfinal = loss_head(x4, tgt, w["final_norm"].reshape(1, D_MODEL))
    grads["final_norm"] = dfinal.reshape(D_MODEL)

    dx3, g_ffn1 = ffn_bwd(1, x3, ffn1_saved, dx4)
    dmix1 = matmul(dx3, w_out_o, tb=True, name="odd_out_dx")
    grads["o_w_out"] = [matmul(mix1, dx3, ta=True, name="odd_out_dw")]
    dc_conv, dln_g, dln_b = rowwise_bwd(_ln_silu, [c_conv], [], [ln_g, ln_b], [], [Cols(dmix1, half, 0)], [F32],
                                        name="conf_ln_bwd", tl=512)
    dc_glu, ddw_w_o, ddw_b_o = conv_bwd(c_glu, dw_w_o, dw_b_o, dc_conv, act=False, name="conf_conv_bwd")
    d_cacg = rowwise_bwd(_glu, [Cols(proj1, half, 0), Cols(proj1, half, 1)], [], [], [], [dc_glu], [BF16],
                         name="conf_glu_bwd", tl=512, merge=True)[0]
    dy_s5, du_post, dd_o, dglu_w = rowwise_bwd(
        _s5_post, [y_s5, Cols(proj1, half, 2)], [], [d_o, glu_w], [], [Cols(dmix1, half, 1)], [F32, F32],
        name="s5_post_bwd", tl=512)
    dy_seg = _seg_interleave(dy_s5)
    dxs_re = matmul(dy_seg, c_re_bd, name="s5_dx_re")
    dxs_im = matmul(dy_seg, c_im_neg_bd, name="s5_dx_im")
    dc_re_bd = matmul(dy_seg, xs_re, ta=True, name="s5_dc_re")
    dc_im_neg_bd = matmul(dy_seg, xs_im, ta=True, name="s5_dc_im")
    g_re, g_im, dab_re, dab_im = s5_scan(dxs_re, dxs_im, a_re_row, -a_im_row, reverse=True, states=(xs_re, xs_im),
                                         name="s5_scan_bwd", lw=LANE)
    dbb_re = _block_diag_take(matmul(u_seg, g_re, ta=True, name="s5_db_re"))
    dbb_im = _block_diag_take(matmul(u_seg, g_im, ta=True, name="s5_db_im"))
    du_im = matmul(g_im, b_im_bd, tb=True, name="s5_du_im")
    du = _seg_deinterleave(matmul(g_re, b_re_bd, tb=True, res=du_im, name="s5_du_re")) + du_post
    da_re, da_im, dlog_step, db_re, db_im = whole_bwd(
        _s5_prep, prep_in, 5,
        [dab_re.reshape(S5_GROUPS, S5_STATE), dab_im.reshape(S5_GROUPS, S5_STATE), dbb_re, dbb_im], name="s5_prep_bwd")
    gcn = (S5_GROUPS, S5_GROUP, S5_STATE)
    grads.update(
        o_a_re=da_re[None], o_a_im=da_im[None], o_log_step=dlog_step.reshape(1, S5_GROUPS),
        o_b_re=db_re.reshape(gcn).transpose(0, 2, 1)[None], o_b_im=db_im.reshape(gcn).transpose(0, 2, 1)[None],
        o_c_re=_block_diag_take(dc_re_bd).reshape(gcn)[None], o_c_im=-_block_diag_take(dc_im_neg_bd).reshape(gcn)[None],
        o_d=dd_o, o_glu_w=[dglu_w], o_dw_w=ddw_w_o[None], o_dw_b=ddw_b_o, o_ln_g=dln_g, o_ln_b=dln_b)
    dproj1 = jnp.concatenate([d_cacg, du.astype(BF16)], axis=1)
    grads["o_w_in"] = [matmul(hn1, dproj1, ta=True, name="odd_in_dw")]
    dhn1 = matmul(dproj1, w_in_o, tb=True, name="odd_in_dx")
    dx2, dmix_norm1 = rms_bwd(x2, w["mix_norm"][1:2], dhn1, dx3, "mix1_norm_bwd")

    if early_reduce is not None:
        zero = early_reduce({("o_w_in", 0): grads["o_w_in"][0], ("o_glu_w", 0): grads["o_glu_w"][0],
                             ("o_w_out", 0): grads["o_w_out"][0], ("ffn_w_up", 1): g_ffn1["ffn_w_up"],
                             ("ffn_w_down", 1): g_ffn1["ffn_w_down"]})
        w["ffn_dw_b"] = w["ffn_dw_b"] + zero
    dx1, g_ffn0 = ffn_bwd(0, x1, ffn0_saved, dx2)
    for k in g_ffn0:
        per_layer = [g_ffn0[k], g_ffn1[k]]
        grads[k] = per_layer if k in ("ffn_w_up", "ffn_w_down") else jnp.stack(per_layer).reshape(w[k].shape)
    dmix0 = matmul(dx1, w_out_e, tb=True, name="even_out_dx")
    grads["e_w_out"] = [matmul(mix0, dx1, ta=True, name="even_out_dw")]
    dq, dk, dv, dg = retention_bwd(proj0, cos2, sin2, ret_states, dmix0)
    dxbc_c, ddt, dz, ddt_bias, da_log, dd_skip, dssm_norm = ssd_bwd(
        xbc, proj0, dt_bias, a_log, d_skip, w["e_ssm_norm"], ssd_states, dmix0)
    dxbc, dconv_w, dconv_b = conv_bwd(proj0, conv_w_e, conv_b_e, dxbc_c, act=True, off=xbc_off,
                                      name="ssd_conv_bwd", dx_dtype=BF16)
    dproj0 = jnp.concatenate([dq, dk, dv, dg, dz, dxbc, ddt], axis=1)
    grads["e_w_in"] = [matmul(hn0, dproj0, ta=True, name="even_in_dw")[:, :EVEN_IN]]
    dhn0 = matmul(dproj0, w_in_e, tb=True, name="even_in_dx")
    dx, dmix_norm0 = rms_bwd(x, w["mix_norm"][0:1], dhn0, dx1, "mix0_norm_bwd")
    grads.update(
        mix_norm=jnp.concatenate([dmix_norm0, dmix_norm1], axis=0), e_conv_w=dconv_w[None], e_conv_b=dconv_b,
        e_dt_bias=ddt_bias[:, :SSM_HEADS], e_a_log=da_log[:, :SSM_HEADS], e_d=dd_skip[:, :SSM_HEADS],
        e_ssm_norm=dssm_norm)
    return loss, dx, grads


def adamw(w, g, m, v, *, name):
    shape = w.shape
    cols = shape[-1]
    rows = w.size // cols
    tr = _tile(rows, max(8, (512 * 1024 // cols) // 8 * 8), unit=8)

    def body(w_ref, g_ref, m_ref, v_ref, d_ref, nm_ref, nv_ref):
        gv = g_ref[...]
        nm = ADAM_B1 * m_ref[...] + (1.0 - ADAM_B1) * gv
        nv = ADAM_B2 * v_ref[...] + (1.0 - ADAM_B2) * jnp.square(gv)
        m_hat = nm / (1.0 - ADAM_B1 ** ADAM_STEP)
        v_hat = nv / (1.0 - ADAM_B2 ** ADAM_STEP)
        d_ref[...] = -ADAM_LR * (m_hat / (jnp.sqrt(v_hat) + ADAM_EPS) + ADAM_WD * w_ref[...])
        nm_ref[...] = nm
        nv_ref[...] = nv

    spec = pl.BlockSpec((tr, cols), lambda i: (i, 0))
    outs = pl.pallas_call(
        body, name=name, grid=(rows // tr,), in_specs=[spec] * 4, out_specs=[spec] * 3,
        out_shape=[jax.ShapeDtypeStruct((rows, cols), F32)] * 3, compiler_params=_params(("parallel",)),
    )(*[t.reshape(rows, cols) for t in (w, g, m, v)])
    return [o.reshape(shape) for o in outs]


OTHER_CHIPS = ((1, 0), (0, 1), (1, 1))
ANY = pl.BlockSpec(memory_space=pl.ANY)


def _position():
    return lax.axis_index("x"), lax.axis_index("y"), lax.axis_index("c")


def _flip(v, f):
    return 1 - v if f else v


def _remote(src, dst, send_sem, recv_sem, device):
    return pltpu.make_async_remote_copy(src_ref=src, dst_ref=dst, send_sem=send_sem, recv_sem=recv_sem,
                                        device_id=device, device_id_type=MESH)


def gather_shards(big, small):
    n_big, n_small = len(big), len(small)
    halves = [a.shape[0] // 2 for a in big]

    def body(*refs):
        big_refs, small_refs = refs[:n_big], refs[n_big:n_big + n_small]
        obig_refs = refs[n_big + n_small:2 * n_big + n_small]
        osmall_refs = refs[2 * n_big + n_small:2 * (n_big + n_small)]
        ici_send, ici_recv, d2d_send, d2d_recv, small_send, small_recv, local_sems = refs[2 * (n_big + n_small):]
        x, y, c = _position()
        mine = 2 * x + y
        local = [pltpu.make_async_copy(src, dst.at[mine], local_sems.at[i])
                 for i, (src, dst) in enumerate(zip(big_refs + small_refs, obig_refs + osmall_refs))]
        for cp in local:
            cp.start()

        def half(k, core):
            return pl.ds(pl.multiple_of(core * halves[k], 16), halves[k])

        sends = []
        for j, (fx, fy) in enumerate(OTHER_CHIPS):
            peer = (_flip(x, fx), _flip(y, fy), c)
            for k in range(n_big):
                sends.append(_remote(big_refs[k].at[half(k, c)], obig_refs[k].at[mine, half(k, c)],
                                     ici_send.at[j, k], ici_recv.at[j, k], peer))
            for k in range(n_small):
                sends.append(_remote(small_refs[k], osmall_refs[k].at[mine], small_send.at[j, k], small_recv.at[j, k], peer))
        for cp in sends:
            cp.start()
        for j, (fx, fy) in enumerate(OTHER_CHIPS):
            px, py = _flip(x, fx), _flip(y, fy)
            src_chip = 2 * px + py
            for k in range(n_big):
                landed = obig_refs[k].at[src_chip, half(k, c)]
                _remote(landed, landed, ici_send.at[j, k], ici_recv.at[j, k], (px, py, c)).wait_recv()
                fwd = _remote(landed, landed, d2d_send.at[j, k], d2d_recv.at[j, k], (x, y, 1 - c))
                fwd.start()
                sends.append(fwd)
        for j, (fx, fy) in enumerate(OTHER_CHIPS):
            px, py = _flip(x, fx), _flip(y, fy)
            src_chip = 2 * px + py
            for k in range(n_big):
                other = obig_refs[k].at[src_chip, half(k, 1 - c)]
                _remote(other, other, d2d_send.at[j, k], d2d_recv.at[j, k], (x, y, 1 - c)).wait_recv()
            for k in range(n_small):
                dst = osmall_refs[k].at[src_chip]
                _remote(small_refs[k], dst, small_send.at[j, k], small_recv.at[j, k], (px, py, c)).wait_recv()
        for cp in sends:
            cp.wait_send()
        for cp in local:
            cp.wait()

    arrays = list(big) + list(small)
    dma = pltpu.SemaphoreType.DMA
    return pl.pallas_call(
        body, name="gather_shards", in_specs=[ANY] * len(arrays), out_specs=[ANY] * len(arrays),
        out_shape=[jax.ShapeDtypeStruct((4,) + a.shape, a.dtype) for a in arrays],
        scratch_shapes=[dma((3, n_big)), dma((3, n_big)), dma((3, n_big)), dma((3, n_big)),
                        dma((3, n_small)), dma((3, n_small)), dma((n_big + n_small,))],
        compiler_params=_params(),
    )(*arrays)


def allreduce_small(pack):
    rows = pack.shape[0]

    def body(p_ref, o_ref, slots, send_sems, recv_sems):
        x, y, c = _position()
        me = 4 * x + 2 * y + c
        slots[me] = p_ref[...]
        flips = [((k >> 2) & 1, (k >> 1) & 1, k & 1) for k in range(1, 8)]
        sends = []
        for k, (fx, fy, fc) in enumerate(flips):
            peer = (_flip(x, fx), _flip(y, fy), _flip(c, fc))
            sends.append(_remote(p_ref, slots.at[me], send_sems.at[k], recv_sems.at[k], peer))
        for cp in sends:
            cp.start()
        for k, (fx, fy, fc) in enumerate(flips):
            px, py, pc = _flip(x, fx), _flip(y, fy), _flip(c, fc)
            _remote(p_ref, slots.at[4 * px + 2 * py + pc], send_sems.at[k], recv_sems.at[k], (px, py, pc)).wait_recv()
        for cp in sends:
            cp.wait_send()
        acc = slots[0]
        for d in range(1, 8):
            acc = acc + slots[d]
        o_ref[...] = acc

    vmem = pl.BlockSpec(memory_space=pltpu.VMEM)
    return pl.pallas_call(
        body, name="allreduce_small", in_specs=[vmem], out_specs=vmem,
        out_shape=jax.ShapeDtypeStruct(pack.shape, F32),
        scratch_shapes=[pltpu.VMEM((8, rows, LANE), F32), pltpu.SemaphoreType.DMA((7,)), pltpu.SemaphoreType.DMA((7,))],
        compiler_params=_params(),
    )(pack)


def exchange_halves(gs, *, name):
    n = len(gs)

    def body(*refs):
        g_refs, o_refs, (send_sems, recv_sems) = refs[:n], refs[n:2 * n], refs[2 * n:]
        x, y, c = _position()
        copies = [_remote(g_refs[k].at[:, 1 - c], o_refs[k], send_sems.at[k], recv_sems.at[k], (x, y, 1 - c)) for k in range(n)]
        for cp in copies:
            cp.start()
        for cp in copies:
            cp.wait()

    return pl.pallas_call(
        body, name=name, in_specs=[ANY] * n, out_specs=[ANY] * n,
        out_shape=[jax.ShapeDtypeStruct((4,) + g.shape[2:], g.dtype) for g in gs],
        scratch_shapes=[pltpu.SemaphoreType.DMA((n,)), pltpu.SemaphoreType.DMA((n,))],
        compiler_params=_params(),
    )(*gs)


def scatter_to_chips(parts):
    n = len(parts)

    def body(*refs):
        a_refs, o_refs, (send_sems, recv_sems) = refs[:n], refs[n:2 * n], refs[2 * n:]
        x, y, c = _position()
        copies = []
        for j, (fx, fy) in enumerate(OTHER_CHIPS):
            px, py = _flip(x, fx), _flip(y, fy)
            for k in range(n):
                copies.append(_remote(a_refs[k].at[2 * px + py], o_refs[k].at[j], send_sems.at[j, k], recv_sems.at[j, k], (px, py, c)))
        for cp in copies:
            cp.start()
        for cp in copies:
            cp.wait()

    return pl.pallas_call(
        body, name="scatter_to_chips", in_specs=[ANY] * n, out_specs=[ANY] * n,
        out_shape=[jax.ShapeDtypeStruct((3,) + a.shape[1:], a.dtype) for a in parts],
        scratch_shapes=[pltpu.SemaphoreType.DMA((3, n)), pltpu.SemaphoreType.DMA((3, n))],
        compiler_params=_params(),
    )(*parts)


def share_halves(rs):
    n = len(rs)

    def body(*refs):
        r_refs, o_refs, (send_sems, recv_sems, local_sems) = refs[:n], refs[n:2 * n], refs[2 * n:]
        x, y, c = _position()
        local = [pltpu.make_async_copy(r_refs[k], o_refs[k].at[c], local_sems.at[k]) for k in range(n)]
        sends = [_remote(r_refs[k], o_refs[k].at[c], send_sems.at[k], recv_sems.at[k], (x, y, 1 - c)) for k in range(n)]
        for cp in local + sends:
            cp.start()
        for k in range(n):
            _remote(r_refs[k], o_refs[k].at[1 - c], send_sems.at[k], recv_sems.at[k], (x, y, 1 - c)).wait_recv()
        for cp in sends:
            cp.wait_send()
        for cp in local:
            cp.wait()

    dma = pltpu.SemaphoreType.DMA
    return pl.pallas_call(
        body, name="share_halves", in_specs=[ANY] * n, out_specs=[ANY] * n,
        out_shape=[jax.ShapeDtypeStruct((2,) + r.shape, r.dtype) for r in rs],
        scratch_shapes=[dma((n,)), dma((n,)), dma((n,))],
        compiler_params=_params(),
    )(*rs)


HBM = pl.BlockSpec(memory_space=pltpu.HBM)
SEM = pl.BlockSpec(memory_space=pltpu.SEMAPHORE)
SIDE_EFFECT = pltpu.SideEffectType.DATAFLOW_SIDE_EFFECTING


def _gather_plan(halves):
    def plan(v_refs, land_refs, x, y, c):
        copies = []
        for fx, fy in OTHER_CHIPS:
            for k in range(len(v_refs)):
                rows = pl.ds(pl.multiple_of(c * halves[k], 16), halves[k])
                copies.append((v_refs[k].at[rows], land_refs[k].at[2 * x + y, rows], (_flip(x, fx), _flip(y, fy), c)))
        return copies
    return plan


def _scatter_plan(v_refs, land_refs, x, y, c):
    copies = []
    for j, (fx, fy) in enumerate(OTHER_CHIPS):
        px, py = _flip(x, fx), _flip(y, fy)
        for k in range(len(v_refs)):
            copies.append((v_refs[k].at[2 * px + py], land_refs[k].at[j], (px, py, c)))
    return copies


def chip_exchange_start(srcs, land_shapes, plan, after, *, name):
    n = len(srcs)
    n_cp = 3 * n

    def body(*refs):
        v_refs, land_refs = refs[:n], refs[n:2 * n]
        outs = refs[2 * n + 1:]
        sends, recvs, token = outs[:n_cp], outs[n_cp:2 * n_cp], outs[-1]
        x, y, c = _position()
        for (src, dst, device), send, recv in zip(plan(v_refs, land_refs, x, y, c), sends, recvs, strict=True):
            _remote(src, dst, send, recv, device).start()
        token[...] = jnp.zeros_like(token)

    lands = [lax.empty(shape, v.dtype) for shape, v in zip(land_shapes, srcs)]
    arrays = [pltpu.with_memory_space_constraint(a, pltpu.HBM) for a in list(srcs) + lands]
    outs = pl.pallas_call(
        body, name=name,
        out_shape=tuple(pltpu.SemaphoreType.DMA(()) for _ in range(2 * n_cp))
        + tuple(pltpu.HBM(a.shape, a.dtype) for a in arrays) + (jax.ShapeDtypeStruct((8, LANE), F32),),
        in_specs=[HBM] * (2 * n) + [ANY],
        out_specs=(SEM,) * (2 * n_cp) + (HBM,) * (2 * n) + (pl.BlockSpec(memory_space=pltpu.VMEM),),
        input_output_aliases={i: 2 * n_cp + i for i in range(2 * n)},
        compiler_params=pltpu.CompilerParams(has_side_effects=SIDE_EFFECT),
    )(*arrays, after)
    handle = (outs[:n_cp], outs[n_cp:2 * n_cp], outs[2 * n_cp:2 * n_cp + n], outs[2 * n_cp + n:2 * n_cp + 2 * n])
    return handle, outs[-1]


def chip_exchange_wait(handle, plan, after, *, name):
    sends, recvs, v_thru, land_thru = handle
    n = len(v_thru)
    n_cp = 3 * n

    def body(*refs):
        v_refs, land_refs = refs[:n], refs[n:2 * n]
        sends, recvs = refs[2 * n:2 * n + n_cp], refs[2 * n + n_cp:2 * n + 2 * n_cp]
        x, y, c = _position()
        for (src, dst, device), send, recv in zip(plan(v_refs, land_refs, x, y, c), sends, recvs, strict=True):
            copy = _remote(src, dst, send, recv, device)
            copy.wait_send()
            copy.wait_recv()

    outs = pl.pallas_call(
        body, name=name,
        out_shape=tuple(pltpu.HBM(a.shape, a.dtype) for a in list(v_thru) + list(land_thru)),
        in_specs=[HBM] * (2 * n) + [SEM] * (2 * n_cp) + [ANY], out_specs=(HBM,) * (2 * n),
        input_output_aliases={i: i for i in range(2 * n)},
        compiler_params=pltpu.CompilerParams(has_side_effects=SIDE_EFFECT),
    )(*v_thru, *land_thru, *sends, *recvs, after)
    return outs[:n], outs[n:]


def finish_gather(shards, lands):
    n = len(shards)
    halves = [a.shape[0] // 2 for a in shards]

    def body(*refs):
        s_refs, l_refs, o_refs = refs[:n], refs[n:2 * n], refs[2 * n:3 * n]
        send_sems, recv_sems, local_sems = refs[3 * n:]
        x, y, c = _position()

        def half(k, core):
            return pl.ds(pl.multiple_of(core * halves[k], 16), halves[k])

        local = [pltpu.make_async_copy(s_refs[k], o_refs[k].at[2 * x + y], local_sems.at[3, k]) for k in range(n)]
        sends = []
        for j, (fx, fy) in enumerate(OTHER_CHIPS):
            src_chip = 2 * _flip(x, fx) + _flip(y, fy)
            for k in range(n):
                held = l_refs[k].at[src_chip, half(k, c)]
                local.append(pltpu.make_async_copy(held, o_refs[k].at[src_chip, half(k, c)], local_sems.at[j, k]))
                sends.append(_remote(held, o_refs[k].at[src_chip, half(k, c)], send_sems.at[j, k], recv_sems.at[j, k], (x, y, 1 - c)))
        for cp in local + sends:
            cp.start()
        for j, (fx, fy) in enumerate(OTHER_CHIPS):
            src_chip = 2 * _flip(x, fx) + _flip(y, fy)
            for k in range(n):
                other = o_refs[k].at[src_chip, half(k, 1 - c)]
                _remote(other, other, send_sems.at[j, k], recv_sems.at[j, k], (x, y, 1 - c)).wait_recv()
        for cp in sends:
            cp.wait_send()
        for cp in local:
            cp.wait()

    dma = pltpu.SemaphoreType.DMA
    return pl.pallas_call(
        body, name="finish_gather", in_specs=[ANY] * (2 * n), out_specs=[ANY] * n,
        out_shape=[jax.ShapeDtypeStruct(a.shape, a.dtype) for a in lands],
        scratch_shapes=[dma((3, n)), dma((3, n)), dma((4, n))],
        compiler_params=_params(),
    )(*shards, *lands)


def add_own_half(g, r, c_idx, *, name):
    _, _, h, cols = g.shape

    def body(c_ref, g_ref, r_ref, o_ref):
        o_ref[...] = (g_ref[0] + r_ref[...]).astype(o_ref.dtype)

    return pl.pallas_call(
        body, name=name,
        grid_spec=pltpu.PrefetchScalarGridSpec(
            num_scalar_prefetch=1, grid=(4,),
            in_specs=[pl.BlockSpec((1, 1, h, cols), lambda s, c: (s, c[0], 0, 0)),
                      pl.BlockSpec((1, h, cols), lambda s, c: (s, 0, 0))],
            out_specs=pl.BlockSpec((1, h, cols), lambda s, c: (s, 0, 0))),
        out_shape=jax.ShapeDtypeStruct(r.shape, BF16), compiler_params=_params(("parallel",)),
    )(c_idx, g, r)


def add_chip_parts(a, parts, chip_idx, *, name):
    _, h, cols = a.shape
    th = h // 2

    def body(s_ref, a_ref, p0_ref, p1_ref, p2_ref, o_ref):
        f = lambda r: r[0].astype(F32)
        o_ref[...] = ((f(a_ref) + f(p0_ref)) + f(p1_ref)) + f(p2_ref)

    part = lambda j: pl.BlockSpec((1, th, cols), lambda i, s, j=j: (j, i, 0))
    return pl.pallas_call(
        body, name=name,
        grid_spec=pltpu.PrefetchScalarGridSpec(
            num_scalar_prefetch=1, grid=(2,),
            in_specs=[pl.BlockSpec((1, th, cols), lambda i, s: (s[0], i, 0)), part(0), part(1), part(2)],
            out_specs=pl.BlockSpec((th, cols), lambda i, s: (i, 0))),
        out_shape=jax.ShapeDtypeStruct((h, cols), F32), compiler_params=_params(("parallel",)),
    )(chip_idx, a, parts, parts, parts)


WEIGHTS = ("mix_norm", "e_w_in", "e_conv_w", "e_conv_b", "e_dt_bias", "e_a_log", "e_d", "e_ssm_norm", "e_w_out",
           "o_w_in", "o_dw_w", "o_dw_b", "o_ln_g", "o_ln_b", "o_a_re", "o_a_im", "o_b_re", "o_b_im", "o_c_re",
           "o_c_im", "o_d", "o_log_step", "o_glu_w", "o_w_out", "ffn_norm", "ffn_w_up", "ffn_dw_w", "ffn_dw_b",
           "ffn_w_down", "final_norm")
BIG = (("e_w_in", 2), ("e_w_out", 1), ("o_w_in", 2), ("o_glu_w", 1), ("o_w_out", 1), ("ffn_w_up", 2), ("ffn_w_down", 1))
SMALL_SHARDED = (("e_conv_w", 2), ("o_dw_w", 2), ("o_dw_b", 1), ("o_ln_g", 1), ("o_ln_b", 1), ("o_d", 1), ("ffn_dw_w", 2))
REPLICATED = tuple(n for n in WEIGHTS if n not in dict(BIG + SMALL_SHARDED))
PACK_ROWS = 8


def _pack(arrays, dtype, row_unit=PACK_ROWS):
    flat = jnp.concatenate([a.astype(dtype).reshape(-1) for a in arrays])
    rows = -(-flat.size // (LANE * row_unit)) * row_unit
    return jnp.pad(flat, (0, rows * LANE - flat.size)).reshape(rows, LANE)


def _unpack(flat, shapes, lead=()):
    out, off = [], 0
    for shape in shapes:
        size = int(np.prod(shape))
        out.append(flat[..., off:off + size].reshape(lead + tuple(shape)))
        off += size
    return out


def _join_shards(parts, axis):
    return jnp.concatenate([parts[s] for s in range(4)], axis=axis)


def _split_shards(full, axis):
    return jnp.stack(jnp.split(full, 4, axis=axis))


def _rows2d(a):
    return a.reshape(-1, a.shape[-1])


def _layer_shards(g, axis):
    rows, cols = g.shape
    if axis == 0:
        return g.reshape(4, 2, rows // 8, cols)
    return g.reshape(rows, 4, cols // 4).transpose(1, 0, 2).reshape(4, 2, rows // 2, cols // 4)


def kernel(x, mix_norm, e_w_in, e_conv_w, e_conv_b, e_dt_bias, e_a_log, e_d, e_ssm_norm, e_w_out, o_w_in, o_dw_w, o_dw_b, o_ln_g, o_ln_b, o_a_re, o_a_im, o_b_re, o_b_im, o_c_re, o_c_im, o_d, o_log_step, o_glu_w, o_w_out, ffn_norm, ffn_w_up, ffn_dw_w, ffn_dw_b, ffn_w_down, final_norm, loss_target, m_mix_norm, m_e_w_in, m_e_conv_w, m_e_conv_b, m_e_dt_bias, m_e_a_log, m_e_d, m_e_ssm_norm, m_e_w_out, m_o_w_in, m_o_dw_w, m_o_dw_b, m_o_ln_g, m_o_ln_b, m_o_a_re, m_o_a_im, m_o_b_re, m_o_b_im, m_o_c_re, m_o_c_im, m_o_d, m_o_log_step, m_o_glu_w, m_o_w_out, m_ffn_norm, m_ffn_w_up, m_ffn_dw_w, m_ffn_dw_b, m_ffn_w_down, m_final_norm, v_mix_norm, v_e_w_in, v_e_conv_w, v_e_conv_b, v_e_dt_bias, v_e_a_log, v_e_d, v_e_ssm_norm, v_e_w_out, v_o_w_in, v_o_dw_w, v_o_dw_b, v_o_ln_g, v_o_ln_b, v_o_a_re, v_o_a_im, v_o_b_re, v_o_b_im, v_o_c_re, v_o_c_im, v_o_d, v_o_log_step, v_o_glu_w, v_o_w_out, v_ffn_norm, v_ffn_w_up, v_ffn_dw_w, v_ffn_dw_b, v_ffn_w_down, v_final_norm):
    given = dict(locals())
    chip = 2 * lax.axis_index("x") + lax.axis_index("y")
    core = lax.axis_index("c")

    core_idx, chip_idx = core.reshape(1).astype(jnp.int32), chip.reshape(1).astype(jnp.int32)

    def whole(n, axis, parts):
        return _join_shards(parts.reshape((4,) + given[n].shape), axis)

    first, later = BIG[:1], BIG[1:]
    shards = {n: _rows2d(given[n]).astype(BF16) for n, _ in BIG}
    gathered = gather_shards([shards[n] for n, _ in first], [_rows2d(given[n]) for n, _ in SMALL_SHARDED])
    w = {n: given[n] for n in REPLICATED}
    for (n, axis), parts in zip(first + SMALL_SHARDED, gathered):
        w[n] = whole(n, axis, parts)
    later_shards = [shards[n] for n, _ in later]
    gather_plan = _gather_plan([a.shape[0] // 2 for a in later_shards])
    gather_handle, token = chip_exchange_start(later_shards, [(4,) + a.shape for a in later_shards], gather_plan,
                                               gathered[0], name="gather_start")
    w["mix_norm"] = w["mix_norm"] + token[0, 0]

    def late_weights(after):
        kept, lands = chip_exchange_wait(gather_handle, gather_plan, after, name="gather_wait")
        return {n: whole(n, axis, parts) for (n, axis), parts in zip(later, finish_gather(kept, lands))}

    pending = {}

    def early_reduce(layer_grads):
        keys = list(layer_grads)
        parts = [_layer_shards(layer_grads[k], dict(BIG)[k[0]] - 1) for k in keys]
        sums = [add_own_half(g, r, core_idx, name=f"add_own_half_{n}{layer}")
                for g, r, (n, layer) in zip(parts, exchange_halves(parts, name="exchange_halves_early"), keys)]
        handle, zeros = chip_exchange_start(sums, [(3,) + a.shape[1:] for a in sums], _scatter_plan, sums[0],
                                            name="scatter_start")
        pending.update(keys=keys, handle=handle)
        return zeros[0, 0]

    loss, dx, grads = local_step(x[0], loss_target[0], w, late_weights, early_reduce)
    early_sums, early_parts = chip_exchange_wait(pending["handle"], _scatter_plan, dx, name="scatter_wait")

    small_names = REPLICATED + tuple(n for n, _ in SMALL_SHARDED)
    small_sum = allreduce_small(_pack([grads[n] for n in small_names], F32))
    reduced = dict(zip(small_names, _unpack(small_sum.reshape(-1), [grads[n].shape for n in small_names])))
    for n, axis in SMALL_SHARDED:
        width = given[n].shape[axis]
        reduced[n] = lax.dynamic_slice_in_dim(reduced[n], chip * width, width, axis=axis)

    keys, parts = [], []
    for n, axis in BIG:
        for layer, g in enumerate(grads[n]):
            if (n, layer) not in pending["keys"]:
                keys.append((n, layer))
                parts.append(_layer_shards(g, axis - 1))
    core_sums = [add_own_half(g, r, core_idx, name=f"add_own_half_{n}{layer}")
                 for g, r, (n, layer) in zip(parts, exchange_halves(parts, name="exchange_halves_late"), keys)]
    chip_parts = scatter_to_chips(core_sums)
    keys, core_sums, chip_parts = pending["keys"] + keys, list(early_sums) + core_sums, list(early_parts) + list(chip_parts)
    mine = [add_chip_parts(a, p, chip_idx, name=f"add_chip_parts_{n}{layer}")
            for a, p, (n, layer) in zip(core_sums, chip_parts, keys)]
    layers = {}
    for (n, layer), both in zip(keys, share_halves(mine)):
        layers.setdefault(n, {})[layer] = both.reshape(given[n].shape[1:])
    for n, _ in BIG:
        reduced[n] = jnp.stack([layers[n][layer] for layer in sorted(layers[n])])

    delta, new_m, new_v = {}, {}, {}
    for n, _ in BIG:
        delta[n], new_m[n], new_v[n] = adamw(given[n], reduced[n], given["m_" + n], given["v_" + n], name="adamw_" + n)
    shapes = [given[n].shape for n in small_names]
    packed = [_pack([src[n] for n in small_names], F32)
              for src in (given, reduced, {n: given["m_" + n] for n in small_names}, {n: given["v_" + n] for n in small_names})]
    for dst, res in zip((delta, new_m, new_v), adamw(*packed, name="adamw_small")):
        dst.update(zip(small_names, _unpack(res.reshape(-1), shapes)))

    total = lax.psum(loss[0, 0], ("x", "y", "c"))
    return (total, dx[None], *[reduced[n] for n in WEIGHTS], *[delta[n] for n in WEIGHTS],
            *[new_m[n] for n in WEIGHTS], *[new_v[n] for n in WEIGHTS])
```

```python
import functools
import math
from typing import NamedTuple

import numpy as np
import jax
import jax.numpy as jnp
from jax import lax
from jax.experimental import pallas as pl
from jax.experimental.pallas import tpu as pltpu

F32 = jnp.float32
BF16 = jnp.bfloat16
HIGHEST = lax.Precision.HIGHEST
MESH = pl.DeviceIdType.MESH

D_MODEL = 1024
EPS = 1e-6
RET_HEADS, RET_DK, RET_DV, CHUNK = 4, 128, 256, 128
ROPE_BASE = 10000.0
SSM_HEADS, SSM_P, SSM_N, SSM_GROUPS = 16, 64, 128, 2
SSM_DINNER = SSM_HEADS * SSM_P
EVEN_IN, EVEN_IN_PAD = 5648, 5760
S5_GROUPS, S5_GROUP, S5_STATE = 32, 16, 64
S5_LANES = S5_GROUPS * S5_STATE
SCAN_SEG = 8
D_FF = 2816
ADAM_LR, ADAM_B1, ADAM_B2, ADAM_EPS, ADAM_WD, ADAM_STEP = 0.001, 0.9, 0.999, 1e-08, 0.01, 10

LANE = 128
VMEM_LIMIT = 56 * 1024 * 1024


def _params(sem=None, **kw):
    return pltpu.CompilerParams(dimension_semantics=sem, vmem_limit_bytes=VMEM_LIMIT, **kw)


def _tile(n, target, unit=LANE):
    if n <= target:
        return n
    t = (target // unit) * unit
    while t >= unit:
        if n % t == 0:
            return t
        t -= unit
    return n


def _silu(x):
    return x * jax.nn.sigmoid(x)


def _mm(a, b):
    return jnp.dot(a.astype(BF16), b.astype(BF16), preferred_element_type=F32)


def _mm_nt(a, b):
    return lax.dot_general(a.astype(BF16), b.astype(BF16), (((1,), (1,)), ((), ())), preferred_element_type=F32)


def _mm_tn(a, b):
    return lax.dot_general(a.astype(BF16), b.astype(BF16), (((0,), (0,)), ((), ())), preferred_element_type=F32)


def _dot_hi(a, b):
    return jnp.dot(a, b, precision=HIGHEST, preferred_element_type=F32)


def _dot_hi_tn(a, b):
    return lax.dot_general(a, b, (((0,), (0,)), ((), ())), precision=HIGHEST, preferred_element_type=F32)


MATMUL_VMEM = 44 * 1024 * 1024


def matmul(a, b, *, ta=False, tb=False, res=None, out_dtype=F32, name):
    m, k = (a.shape[1], a.shape[0]) if ta else a.shape
    n = b.shape[0] if tb else b.shape[1]
    assert (b.shape[1] if tb else b.shape[0]) == k, (a.shape, b.shape, ta, tb)
    tm = _tile(m, 1536)
    tn = _tile(n, 640)
    if tn < 384:
        tn = _tile(n, 1536)
    res_bytes = 0 if res is None else res.dtype.itemsize

    def vmem(tm, tn):
        return 2 * (tm * k * a.dtype.itemsize + tn * k * b.dtype.itemsize + tm * tn * (jnp.dtype(out_dtype).itemsize + res_bytes))

    while vmem(tm, tn) > MATMUL_VMEM and tm % (2 * LANE) == 0:
        tm //= 2
    assert vmem(tm, tn) <= MATMUL_VMEM, (name, tm, tn, k)
    a_spec = pl.BlockSpec((k, tm), lambda i, j: (0, i)) if ta else pl.BlockSpec((tm, k), lambda i, j: (i, 0))
    b_spec = pl.BlockSpec((tn, k), lambda i, j: (j, 0)) if tb else pl.BlockSpec((k, tn), lambda i, j: (0, j))
    o_spec = pl.BlockSpec((tm, tn), lambda i, j: (i, j))
    dims = (((0 if ta else 1,), (1 if tb else 0,)), ((), ()))
    has_res = res is not None

    def body(a_ref, b_ref, *rest):
        o_ref = rest[-1]
        out = lax.dot_general(a_ref[...].astype(BF16), b_ref[...].astype(BF16), dims, preferred_element_type=F32)
        if has_res:
            out = out + rest[0][...].astype(F32)
        o_ref[...] = out.astype(o_ref.dtype)

    ins = [a, b] + ([res] if has_res else [])
    specs = [a_spec, b_spec] + ([o_spec] if has_res else [])
    return pl.pallas_call(
        body, name=name, grid=(m // tm, n // tn), in_specs=specs, out_specs=o_spec,
        out_shape=jax.ShapeDtypeStruct((m, n), out_dtype), compiler_params=_params(("parallel", "parallel")),
    )(*ins)


class Cols(NamedTuple):
    arr: jax.Array
    w: int
    j: int


def _cols(a):
    return a if isinstance(a, Cols) else Cols(a, a.shape[1], 0)


def _row_spec(c, tl):
    return pl.BlockSpec((tl, c.w), lambda i, j=c.j: (i, j))


def _whole_spec(p):
    return pl.BlockSpec(p.shape, lambda i, nd=p.ndim: (0,) * nd)


def rowwise_fwd(fn, rows, aux, pars, consts, outs, *, name, tl):
    rows = [_cols(r) for r in rows + aux]
    whole = list(pars) + list(consts)
    n_rows = len(rows)
    n_whole = len(whole)
    length = rows[0].arr.shape[0]
    tl = min(tl, length)

    def body(*refs):
        vals = [r[...].astype(F32) for r in refs[:n_rows]] + [r[...] for r in refs[n_rows:n_rows + n_whole]]
        res = fn(*vals)
        for o_ref, v in zip(refs[n_rows + n_whole:], res, strict=True):
            o_ref[...] = v.astype(o_ref.dtype)

    return pl.pallas_call(
        body, name=name, grid=(length // tl,),
        in_specs=[_row_spec(r, tl) for r in rows] + [_whole_spec(p) for p in whole],
        out_specs=[pl.BlockSpec((tl, w), lambda i: (i, 0)) for w, _ in outs],
        out_shape=[jax.ShapeDtypeStruct((length, w), dt) for w, dt in outs],
        compiler_params=_params(("parallel",)),
    )(*[r.arr for r in rows], *whole)


def rowwise_bwd(fn, rows, aux, pars, consts, cots, drow_dtypes, *, name, tl, add=None, merge=False):
    rows = [_cols(r) for r in rows]
    aux = [_cols(r) for r in aux]
    cots = [_cols(r) for r in cots]
    n_r, n_a, n_p, n_c, n_t = len(rows), len(aux), len(pars), len(consts), len(cots)
    length = rows[0].arr.shape[0]
    tl = min(tl, length)
    has_add = add is not None
    widths = [r.w for r in rows]

    def body(*refs):
        pos = 0
        r_vals = [r[...].astype(F32) for r in refs[pos:pos + n_r]]; pos += n_r
        a_vals = [r[...].astype(F32) for r in refs[pos:pos + n_a]]; pos += n_a
        p_vals = [r[...].astype(F32) for r in refs[pos:pos + n_p]]; pos += n_p
        c_vals = [r[...] for r in refs[pos:pos + n_c]]; pos += n_c
        t_vals = [r[...].astype(F32) for r in refs[pos:pos + n_t]]; pos += n_t
        add_val = None
        if has_add:
            add_val = refs[pos][...].astype(F32); pos += 1
        n_dr = 1 if merge else n_r
        dr_refs = refs[pos:pos + n_dr]; pos += n_dr
        dp_refs = refs[pos:pos + n_p]

        def f(*rp):
            return fn(*rp[:n_r], *a_vals, *rp[n_r:], *c_vals)

        _, vjp = jax.vjp(f, *r_vals, *p_vals)
        grads = vjp(tuple(t_vals))
        drows = list(grads[:n_r])
        if has_add:
            drows[0] = drows[0] + add_val
        if merge:
            off = 0
            for w, d in zip(widths, drows):
                dr_refs[0][:, off:off + w] = d.astype(dr_refs[0].dtype)
                off += w
        else:
            for r, d in zip(dr_refs, drows):
                r[...] = d.astype(r.dtype)
        i = pl.program_id(0)
        for r, d in zip(dp_refs, grads[n_r:]):
            @pl.when(i == 0)
            def _(r=r, d=d):
                r[...] = d

            @pl.when(i > 0)
            def _(r=r, d=d):
                r[...] += d

    if merge:
        dr_specs = [pl.BlockSpec((tl, sum(widths)), lambda i: (i, 0))]
        dr_shapes = [jax.ShapeDtypeStruct((length, sum(widths)), drow_dtypes[0])]
    else:
        dr_specs = [pl.BlockSpec((tl, w), lambda i: (i, 0)) for w in widths]
        dr_shapes = [jax.ShapeDtypeStruct((length, w), dt) for w, dt in zip(widths, drow_dtypes)]
    ins = [r.arr for r in rows + aux] + list(pars) + list(consts) + [r.arr for r in cots] + ([add] if has_add else [])
    specs = ([_row_spec(r, tl) for r in rows + aux] + [_whole_spec(p) for p in list(pars) + list(consts)]
             + [_row_spec(r, tl) for r in cots] + ([pl.BlockSpec((tl, add.shape[1]), lambda i: (i, 0))] if has_add else []))
    return pl.pallas_call(
        body, name=name, grid=(length // tl,), in_specs=specs,
        out_specs=dr_specs + [_whole_spec(p) for p in pars],
        out_shape=dr_shapes + [jax.ShapeDtypeStruct(p.shape, F32) for p in pars],
        compiler_params=_params(("arbitrary",)),
    )(*ins)


def whole_fwd(fn, ins, out_shapes, *, name):
    n_in = len(ins)

    def body(*refs):
        res = fn(*[r[...] for r in refs[:n_in]])
        for o_ref, v in zip(refs[n_in:], res, strict=True):
            o_ref[...] = v

    return pl.pallas_call(body, name=name, out_shape=[jax.ShapeDtypeStruct(s, F32) for s in out_shapes],
                          compiler_params=_params())(*ins)


def whole_bwd(fn, ins, n_diff, cots, *, name):
    n_in, n_t = len(ins), len(cots)

    def body(*refs):
        vals = [r[...] for r in refs[:n_in]]
        t_vals = [r[...] for r in refs[n_in:n_in + n_t]]
        _, vjp = jax.vjp(lambda *d: fn(*d, *vals[n_diff:]), *vals[:n_diff])
        for o_ref, g in zip(refs[n_in + n_t:], vjp(tuple(t_vals)), strict=True):
            o_ref[...] = g

    return pl.pallas_call(body, name=name, out_shape=[jax.ShapeDtypeStruct(a.shape, F32) for a in ins[:n_diff]],
                          compiler_params=_params())(*ins, *cots)


CONV_ROWS = 256


def _conv_geometry(x, w, cw, off):
    width = w.shape[1]
    x = Cols(x, width, 0)
    length = x.arr.shape[0]
    taps = w.shape[0]
    pad = -(-(taps - 1) // 8) * 8
    assert off % cw == 0 and width % cw == 0, (off, width, cw)
    return x, length, taps, pad, off // cw


def _conv_taps(xp_ref, w_ref, base, taps, pad, init):
    acc = init
    for k in range(taps):
        acc = acc + w_ref[k:k + 1, :] * xp_ref[pl.ds(base + pad - (taps - 1) + k, init.shape[0]), :]
    return acc


def conv_fwd(x, w, b, *, act, name, off=0, cw=LANE, out_dtype=F32):
    x, length, taps, pad, jb = _conv_geometry(x, w, cw, off)
    rc = min(CONV_ROWS, length)

    def body(x_ref, w_ref, b_ref, o_ref, xp_ref):
        xp_ref[0:pad, :] = jnp.zeros((pad, cw), F32)
        xp_ref[pad:pad + length, :] = x_ref[...].astype(F32)

        def chunk(r, carry):
            base = pl.multiple_of(r * rc, rc)
            acc = _conv_taps(xp_ref, w_ref, base, taps, pad, jnp.broadcast_to(b_ref[...], (rc, cw)))
            if act:
                acc = _silu(acc)
            o_ref[pl.ds(base, rc), :] = acc.astype(o_ref.dtype)
            return carry

        lax.fori_loop(0, length // rc, chunk, 0)

    return pl.pallas_call(
        body, name=name, grid=(x.w // cw,),
        in_specs=[pl.BlockSpec((length, cw), lambda j: (0, jb + j)), pl.BlockSpec((taps, cw), lambda j: (0, j)),
                  pl.BlockSpec((1, cw), lambda j: (0, j))],
        out_specs=pl.BlockSpec((length, cw), lambda j: (0, j)),
        out_shape=jax.ShapeDtypeStruct((length, x.w), out_dtype),
        scratch_shapes=[pltpu.VMEM((pad + length, cw), F32)],
        compiler_params=_params(("parallel",)),
    )(x.arr, w, b)


def conv_bwd(x, w, b, dy, *, act, name, off=0, cw=LANE, dx_dtype=F32):
    x, length, taps, pad, jb = _conv_geometry(x, w, cw, off)
    rc = min(CONV_ROWS, length)

    def body(x_ref, w_ref, b_ref, dy_ref, dx_ref, dw_ref, db_ref, xp_ref, gp_ref):
        xp_ref[0:pad, :] = jnp.zeros((pad, cw), F32)
        xp_ref[pad:pad + length, :] = x_ref[...].astype(F32)
        gp_ref[length:length + pad, :] = jnp.zeros((pad, cw), F32)
        if act:
            def pre_chunk(r, carry):
                base = pl.multiple_of(r * rc, rc)
                pre = _conv_taps(xp_ref, w_ref, base, taps, pad, jnp.broadcast_to(b_ref[...], (rc, cw)))
                sig = jax.nn.sigmoid(pre)
                gp_ref[pl.ds(base, rc), :] = dy_ref[pl.ds(base, rc), :].astype(F32) * (sig * (1.0 + pre * (1.0 - sig)))
                return carry

            lax.fori_loop(0, length // rc, pre_chunk, 0)
        else:
            gp_ref[0:length, :] = dy_ref[...].astype(F32)
        dw_ref[...] = jnp.zeros((taps, cw), F32)
        db_ref[...] = jnp.zeros((1, cw), F32)

        def chunk(r, carry):
            base = pl.multiple_of(r * rc, rc)
            acc = jnp.zeros((rc, cw), F32)
            g = gp_ref[pl.ds(base, rc), :]
            for k in range(taps):
                acc = acc + w_ref[k:k + 1, :] * gp_ref[pl.ds(base + (taps - 1) - k, rc), :]
                xs = xp_ref[pl.ds(base + pad - (taps - 1) + k, rc), :]
                dw_ref[k:k + 1, :] += jnp.sum(g * xs, axis=0, keepdims=True)
            db_ref[...] += jnp.sum(g, axis=0, keepdims=True)
            dx_ref[pl.ds(base, rc), :] = acc.astype(dx_ref.dtype)
            return carry

        lax.fori_loop(0, length // rc, chunk, 0)

    dy = _cols(dy)
    assert dy.j == 0 and dy.w == x.w
    return pl.pallas_call(
        body, name=name, grid=(x.w // cw,),
        in_specs=[pl.BlockSpec((length, cw), lambda j: (0, jb + j)), pl.BlockSpec((taps, cw), lambda j: (0, j)),
                  pl.BlockSpec((1, cw), lambda j: (0, j)), pl.BlockSpec((length, cw), lambda j: (0, j))],
        out_specs=[pl.BlockSpec((length, cw), lambda j: (0, j)), pl.BlockSpec((taps, cw), lambda j: (0, j)),
                   pl.BlockSpec((1, cw), lambda j: (0, j))],
        out_shape=[jax.ShapeDtypeStruct((length, x.w), dx_dtype), jax.ShapeDtypeStruct((taps, x.w), F32),
                   jax.ShapeDtypeStruct((1, x.w), F32)],
        scratch_shapes=[pltpu.VMEM((pad + length, cw), F32), pltpu.VMEM((length + pad, cw), F32)],
        compiler_params=_params(("parallel",)),
    )(x.arr, w, b, dy.arr)


def _conv_transpose(xp_ref, gp_ref, w_ref, dx_ref, dw_ref, db_ref, length, taps, pad, rc):
    dw_ref[...] = jnp.zeros(dw_ref.shape, F32)
    db_ref[...] = jnp.zeros(db_ref.shape, F32)

    def chunk(r, carry):
        base = pl.multiple_of(r * rc, rc)
        acc = jnp.zeros((rc, LANE), F32)
        g = gp_ref[pl.ds(base, rc), :]
        for k in range(taps):
            acc = acc + w_ref[k:k + 1, :] * gp_ref[pl.ds(base + (taps - 1) - k, rc), :]
            xs = xp_ref[pl.ds(base + pad - (taps - 1) + k, rc), :]
            dw_ref[k:k + 1, :] += jnp.sum(g * xs, axis=0, keepdims=True)
        db_ref[...] += jnp.sum(g, axis=0, keepdims=True)
        dx_ref[pl.ds(base, rc), :] = acc.astype(dx_ref.dtype)
        return carry

    lax.fori_loop(0, length // rc, chunk, 0)


def ffn_conv_act(a, w, b, *, name):
    length, width = a.shape
    nb = width // 2 // LANE
    taps = w.shape[0]
    pad = -(-(taps - 1) // 8) * 8
    rc = min(CONV_ROWS, length)

    def body(ag_ref, au_ref, wg_ref, wu_ref, bg_ref, bu_ref, o_ref, xg_ref, xu_ref):
        for xp_ref, src in ((xg_ref, ag_ref), (xu_ref, au_ref)):
            xp_ref[0:pad, :] = jnp.zeros((pad, LANE), F32)
            xp_ref[pad:pad + length, :] = src[...]

        def chunk(r, carry):
            base = pl.multiple_of(r * rc, rc)
            gate = _conv_taps(xg_ref, wg_ref, base, taps, pad, jnp.broadcast_to(bg_ref[...], (rc, LANE)))
            up = _conv_taps(xu_ref, wu_ref, base, taps, pad, jnp.broadcast_to(bu_ref[...], (rc, LANE)))
            o_ref[pl.ds(base, rc), :] = (_silu(gate) * up).astype(o_ref.dtype)
            return carry

        lax.fori_loop(0, length // rc, chunk, 0)

    blk = lambda rows, up: pl.BlockSpec((rows, LANE), (lambda j: (0, nb + j)) if up else (lambda j: (0, j)))
    return pl.pallas_call(
        body, name=name, grid=(nb,),
        in_specs=[blk(length, False), blk(length, True), blk(taps, False), blk(taps, True), blk(1, False), blk(1, True)],
        out_specs=blk(length, False), out_shape=jax.ShapeDtypeStruct((length, width // 2), BF16),
        scratch_shapes=[pltpu.VMEM((pad + length, LANE), F32), pltpu.VMEM((pad + length, LANE), F32)],
        compiler_params=_params(("parallel",)),
    )(a, a, w, w, b, b)


def ffn_conv_act_bwd(a, w, b, dact, *, name):
    length, width = a.shape
    nb = width // 2 // LANE
    taps = w.shape[0]
    pad = -(-(taps - 1) // 8) * 8
    rc = min(CONV_ROWS, length)

    def body(ag_ref, au_ref, wg_ref, wu_ref, bg_ref, bu_ref, dy_ref, da_ref, dw_ref, db_ref, xg_ref, xu_ref, gp_ref):
        for xp_ref, src in ((xg_ref, ag_ref), (xu_ref, au_ref)):
            xp_ref[0:pad, :] = jnp.zeros((pad, LANE), F32)
            xp_ref[pad:pad + length, :] = src[...]
        gp_ref[length:length + pad, :] = jnp.zeros((pad, LANE), F32)
        is_gate = pl.program_id(0) == 0

        def pre_chunk(r, carry):
            base = pl.multiple_of(r * rc, rc)
            gate = _conv_taps(xg_ref, wg_ref, base, taps, pad, jnp.broadcast_to(bg_ref[...], (rc, LANE)))
            up = _conv_taps(xu_ref, wu_ref, base, taps, pad, jnp.broadcast_to(bu_ref[...], (rc, LANE)))
            sig = jax.nn.sigmoid(gate)
            d_gate = up * (sig * (1.0 + gate * (1.0 - sig)))
            gp_ref[pl.ds(base, rc), :] = dy_ref[pl.ds(base, rc), :] * jnp.where(is_gate, d_gate, gate * sig)
            return carry

        lax.fori_loop(0, length // rc, pre_chunk, 0)

        @pl.when(is_gate)
        def _():
            _conv_transpose(xg_ref, gp_ref, wg_ref, da_ref, dw_ref, db_ref, length, taps, pad, rc)

        @pl.when(jnp.logical_not(is_gate))
        def _():
            _conv_transpose(xu_ref, gp_ref, wu_ref, da_ref, dw_ref, db_ref, length, taps, pad, rc)

    blk = lambda rows, up: pl.BlockSpec((rows, LANE), (lambda h, j: (0, nb + j)) if up else (lambda h, j: (0, j)))
    out = lambda rows: pl.BlockSpec((rows, LANE), lambda h, j: (0, h * nb + j))
    return pl.pallas_call(
        body, name=name, grid=(2, nb),
        in_specs=[blk(length, False), blk(length, True), blk(taps, False), blk(taps, True), blk(1, False), blk(1, True),
                  blk(length, False)],
        out_specs=[out(length), out(taps), out(1)],
        out_shape=[jax.ShapeDtypeStruct((length, width), BF16), jax.ShapeDtypeStruct((taps, width), F32),
                   jax.ShapeDtypeStruct((1, width), F32)],
        scratch_shapes=[pltpu.VMEM((pad + length, LANE), F32), pltpu.VMEM((pad + length, LANE), F32),
                        pltpu.VMEM((length + pad, LANE), F32)],
        compiler_params=_params(("parallel", "parallel")),
    )(a, a, w, w, b, b, dact)


def _retention_consts():
    h = np.arange(RET_HEADS, dtype=np.float32)
    log_g = np.log1p(-(2.0 ** (-5.0 - h))).astype(np.float32)
    idx = np.arange(CHUNK, dtype=np.float32)
    diff = idx[:, None] - idx[None, :]
    intra = np.where(diff[None] >= 0, np.exp(np.maximum(diff, 0.0)[None] * log_g[:, None, None]), 0.0)
    zeta = np.exp((CHUNK - 1 - idx)[None, :] * log_g[:, None])
    xi = np.exp((idx + 1)[None, :] * log_g[:, None])
    decay = np.exp(CHUNK * log_g)
    zeta = np.broadcast_to(zeta[:, :, None], (RET_HEADS, CHUNK, RET_DK))
    xi = np.broadcast_to(xi[:, :, None], (RET_HEADS, CHUNK, RET_DV))
    return (jnp.asarray(intra, F32), jnp.asarray(zeta, F32), jnp.asarray(xi, F32), [float(d) for d in decay])


def _rotary_tables(length):
    inv = ROPE_BASE ** (-jnp.arange(0, RET_DK, 2, dtype=F32) / RET_DK)
    ang = jnp.arange(length).astype(F32)[:, None] * inv[None, :]
    cos, sin = jnp.cos(ang), jnp.sin(ang)
    return jnp.concatenate([cos, cos], axis=1), jnp.concatenate([-sin, sin], axis=1)


def _rot(x, cos2, sin2):
    return x * cos2 + pltpu.roll(x, RET_DK // 2, 1) * sin2


def _rot_t(y, cos2, sin2):
    return y * cos2 + pltpu.roll(y * sin2, RET_DK // 2, 1)


def _head_decay(h, decays):
    d = jnp.float32(decays[-1])
    for i in range(len(decays) - 2, -1, -1):
        d = jnp.where(h == i, jnp.float32(decays[i]), d)
    return d


def _ret_chunk(q, k, v, g, state, intra, zeta, xi, decay):
    s = _mm_nt(q, k) * intra
    kv = _mm_tn(k * zeta, v)
    o = _mm(s, v) + _mm(q, state) * xi
    oc = o - jnp.mean(o, axis=-1, keepdims=True)
    r = oc * lax.rsqrt(jnp.mean(oc * oc, axis=-1, keepdims=True) + EPS)
    return _silu(g) * r, state * decay + kv


def _ret_specs(rev, nc):
    def cidx(c):
        return nc - 1 - c if rev else c
    return [
        pl.BlockSpec((CHUNK, RET_DK), lambda h, c: (cidx(c), h)),
        pl.BlockSpec((CHUNK, RET_DK), lambda h, c: (cidx(c), RET_HEADS + h)),
        pl.BlockSpec((CHUNK, RET_DV), lambda h, c: (cidx(c), 4 + h)),
        pl.BlockSpec((CHUNK, RET_DV), lambda h, c: (cidx(c), 8 + h)),
        pl.BlockSpec((CHUNK, RET_DK), lambda h, c: (cidx(c), 0)),
        pl.BlockSpec((CHUNK, RET_DK), lambda h, c: (cidx(c), 0)),
        pl.BlockSpec((1, CHUNK, CHUNK), lambda h, c: (h, 0, 0)),
        pl.BlockSpec((1, CHUNK, RET_DK), lambda h, c: (h, 0, 0)),
        pl.BlockSpec((1, CHUNK, RET_DV), lambda h, c: (h, 0, 0)),
    ], cidx


def retention_fwd(proj, cos2, sin2):
    length = proj.shape[0]
    nc = length // CHUNK
    intra, zeta, xi, decays = _retention_consts()
    specs, _ = _ret_specs(False, nc)
    scale = RET_DK ** -0.5

    def body(q_ref, k_ref, v_ref, g_ref, cos_ref, sin_ref, intra_ref, zeta_ref, xi_ref, y_ref, st_ref, state):
        h, c = pl.program_id(0), pl.program_id(1)

        @pl.when(c == 0)
        def _():
            state[...] = jnp.zeros_like(state)

        q = _rot(q_ref[...], cos_ref[...], sin_ref[...])
        k = _rot(k_ref[...], cos_ref[...], sin_ref[...]) * scale
        st_ref[0, 0] = state[...]
        y, new_state = _ret_chunk(q, k, v_ref[...], g_ref[...], state[...], intra_ref[0], zeta_ref[0], xi_ref[0],
                                  _head_decay(h, decays))
        y_ref[...] = y.astype(y_ref.dtype)
        state[...] = new_state

    return pl.pallas_call(
        body, name="retention_fwd", grid=(RET_HEADS, nc), in_specs=specs,
        out_specs=[pl.BlockSpec((CHUNK, RET_DV), lambda h, c: (c, h)),
                   pl.BlockSpec((1, 1, RET_DK, RET_DV), lambda h, c: (h, c, 0, 0))],
        out_shape=[jax.ShapeDtypeStruct((length, RET_HEADS * RET_DV), BF16),
                   jax.ShapeDtypeStruct((RET_HEADS, nc, RET_DK, RET_DV), F32)],
        scratch_shapes=[pltpu.VMEM((RET_DK, RET_DV), F32)],
        compiler_params=_params(("parallel", "arbitrary")),
    )(proj, proj, proj, proj, cos2, sin2, intra, zeta, xi)


def retention_bwd(proj, cos2, sin2, states, dmix):
    length = proj.shape[0]
    nc = length // CHUNK
    intra, zeta, xi, decays = _retention_consts()
    specs, cidx = _ret_specs(True, nc)
    scale = RET_DK ** -0.5

    def body(q_ref, k_ref, v_ref, g_ref, cos_ref, sin_ref, intra_ref, zeta_ref, xi_ref, st_ref, dy_ref,
             dq_ref, dk_ref, dv_ref, dg_ref, dstate):
        h, c = pl.program_id(0), pl.program_id(1)

        @pl.when(c == 0)
        def _():
            dstate[...] = jnp.zeros_like(dstate)

        cos2v, sin2v = cos_ref[...], sin_ref[...]
        q = _rot(q_ref[...], cos2v, sin2v)
        k = _rot(k_ref[...], cos2v, sin2v) * scale
        decay = _head_decay(h, decays)
        intra_v, zeta_v, xi_v = intra_ref[0], zeta_ref[0], xi_ref[0]
        _, vjp = jax.vjp(lambda q, k, v, g, s: _ret_chunk(q, k, v, g, s, intra_v, zeta_v, xi_v, decay),
                         q, k, v_ref[...], g_ref[...], st_ref[0, 0])
        dq, dk, dv, dg, ds = vjp((dy_ref[...].astype(F32), dstate[...]))
        dq_ref[...] = _rot_t(dq, cos2v, sin2v).astype(dq_ref.dtype)
        dk_ref[...] = _rot_t(dk * scale, cos2v, sin2v).astype(dk_ref.dtype)
        dv_ref[...] = dv.astype(dv_ref.dtype)
        dg_ref[...] = dg.astype(dg_ref.dtype)
        dstate[...] = ds

    specs = specs + [pl.BlockSpec((1, 1, RET_DK, RET_DV), lambda h, c: (h, cidx(c), 0, 0)),
                     pl.BlockSpec((CHUNK, RET_DV), lambda h, c: (cidx(c), h))]
    return pl.pallas_call(
        body, name="retention_bwd", grid=(RET_HEADS, nc), in_specs=specs,
        out_specs=[pl.BlockSpec((CHUNK, RET_DK), lambda h, c: (cidx(c), h)),
                   pl.BlockSpec((CHUNK, RET_DK), lambda h, c: (cidx(c), h)),
                   pl.BlockSpec((CHUNK, RET_DV), lambda h, c: (cidx(c), h)),
                   pl.BlockSpec((CHUNK, RET_DV), lambda h, c: (cidx(c), h))],
        out_shape=[jax.ShapeDtypeStruct((length, RET_HEADS * RET_DK), BF16),
                   jax.ShapeDtypeStruct((length, RET_HEADS * RET_DK), BF16),
                   jax.ShapeDtypeStruct((length, RET_HEADS * RET_DV), BF16),
                   jax.ShapeDtypeStruct((length, RET_HEADS * RET_DV), BF16)],
        scratch_shapes=[pltpu.VMEM((RET_DK, RET_DV), F32)],
        compiler_params=_params(("parallel", "arbitrary")),
    )(proj, proj, proj, proj, cos2, sin2, intra, zeta, xi, states, dmix)


def _ssd_consts():
    tri = np.tril(np.ones((CHUNK, CHUNK), np.float32))
    expand = np.zeros((LANE, SSM_DINNER), np.float32)
    for h in range(SSM_HEADS):
        expand[h, h * SSM_P:(h + 1) * SSM_P] = 1.0
    return jnp.asarray(tri), jnp.asarray(tri.T.copy()), jnp.asarray(expand)


def _ssd_chunk(xs, bm, cm, dtr, z, state, dt_bias, a_log, d_skip, norm_w, tri, tri_t, expand):
    gw = SSM_DINNER // SSM_GROUPS
    dt = jax.nn.softplus(dtr + dt_bias)
    da = dt * (-jnp.exp(a_log))
    acs = _dot_hi(tri, da)
    acs_t = _dot_hi_tn(da, tri_t)
    dt_x = _dot_hi(dt, expand)
    da_x = _dot_hi(da, expand)
    acs_x = _dot_hi(tri, da_x)
    tot_x = jnp.sum(da_x, axis=0, keepdims=True)
    x_dt = xs * dt_x
    x_dec = x_dt * jnp.exp(tot_x - acs_x)
    e_acs = jnp.exp(acs_x)
    e_tot = jnp.exp(tot_x)
    lane = lax.broadcasted_iota(jnp.int32, (CHUNK, LANE), 1)
    sub = lax.broadcasted_iota(jnp.int32, (CHUNK, LANE), 0)
    causal = sub >= lane
    ys, new_states = [], []
    for g in range(SSM_GROUPS):
        bg = bm[:, g * SSM_N:(g + 1) * SSM_N]
        cg = cm[:, g * SSM_N:(g + 1) * SSM_N]
        sg = state[:, g * gw:(g + 1) * gw]
        cb = _mm_nt(cg, bg)
        y_off = _mm(cg, sg) * e_acs[:, g * gw:(g + 1) * gw]
        new_states.append(sg * e_tot[:, g * gw:(g + 1) * gw] + _mm_tn(bg, x_dec[:, g * gw:(g + 1) * gw]))
        pairs = []
        for p in range(gw // LANE):
            hp = g * (gw // LANE) + p
            xp = x_dt[:, hp * LANE:(hp + 1) * LANE]
            halves = []
            for head in (2 * hp, 2 * hp + 1):
                col = jnp.sum(jnp.where(lane == head, acs, 0.0), axis=1, keepdims=True)
                row = jnp.sum(jnp.where(sub == head, acs_t, 0.0), axis=0, keepdims=True)
                decay = jnp.exp(jnp.where(causal, col - row, -1e30))
                halves.append(_mm(cb * decay, xp))
            pairs.append(jnp.where(lane < SSM_P, halves[0], halves[1]))
        ys.append(jnp.concatenate(pairs, axis=1) + y_off)
    d_x = jnp.mean(_dot_hi(jnp.broadcast_to(d_skip, (8, LANE)), expand), axis=0, keepdims=True)
    y = (jnp.concatenate(ys, axis=1) + d_x * xs) * _silu(z)
    normed = []
    for g in range(SSM_GROUPS):
        yg = y[:, g * gw:(g + 1) * gw]
        normed.append(yg * lax.rsqrt(jnp.mean(yg * yg, axis=-1, keepdims=True) + EPS))
    return jnp.concatenate(normed, axis=1) * norm_w, jnp.concatenate(new_states, axis=1)


XBC = SSM_DINNER + 2 * SSM_GROUPS * SSM_N


def _ssd_specs(rev, nc):
    def cidx(c):
        return nc - 1 - c if rev else c
    row = lambda w, j: pl.BlockSpec((CHUNK, w), lambda c: (cidx(c), j))
    whole = lambda shape: pl.BlockSpec(shape, lambda c: (0,) * len(shape))
    return [row(XBC, 0), row(LANE, 5632 // LANE), row(SSM_DINNER, 3),
            whole((1, LANE)), whole((1, LANE)), whole((1, LANE)), whole((1, SSM_DINNER)),
            whole((CHUNK, CHUNK)), whole((CHUNK, CHUNK)), whole((LANE, SSM_DINNER))], cidx


def ssd_fwd(xbc, proj, dt_bias, a_log, d_skip, norm_w):
    length = proj.shape[0]
    nc = length // CHUNK
    tri, tri_t, expand = _ssd_consts()
    specs, _ = _ssd_specs(False, nc)

    def body(xbc_ref, dt_ref, z_ref, dtb_ref, alog_ref, d_ref, nw_ref, tri_ref, trit_ref, e_ref, y_ref, st_ref, state):
        @pl.when(pl.program_id(0) == 0)
        def _():
            state[...] = jnp.zeros_like(state)

        st_ref[0] = state[...]
        y, new_state = _ssd_chunk(
            xbc_ref[:, 0:SSM_DINNER], xbc_ref[:, SSM_DINNER:SSM_DINNER + 256], xbc_ref[:, SSM_DINNER + 256:XBC],
            dt_ref[...], z_ref[...], state[...], dtb_ref[...], alog_ref[...], d_ref[...], nw_ref[...],
            tri_ref[...], trit_ref[...], e_ref[...])
        y_ref[...] = y.astype(y_ref.dtype)
        state[...] = new_state

    return pl.pallas_call(
        body, name="ssd_fwd", grid=(nc,), in_specs=specs,
        out_specs=[pl.BlockSpec((CHUNK, SSM_DINNER), lambda c: (c, 0)),
                   pl.BlockSpec((1, SSM_N, SSM_DINNER), lambda c: (c, 0, 0))],
        out_shape=[jax.ShapeDtypeStruct((length, SSM_DINNER), BF16),
                   jax.ShapeDtypeStruct((nc, SSM_N, SSM_DINNER), F32)],
        scratch_shapes=[pltpu.VMEM((SSM_N, SSM_DINNER), F32)],
        compiler_params=_params(("arbitrary",)),
    )(xbc, proj, proj, dt_bias, a_log, d_skip, norm_w, tri, tri_t, expand)


def ssd_bwd(xbc, proj, dt_bias, a_log, d_skip, norm_w, states, dmix):
    length = proj.shape[0]
    nc = length // CHUNK
    tri, tri_t, expand = _ssd_consts()
    specs, cidx = _ssd_specs(True, nc)

    def body(xbc_ref, dt_ref, z_ref, dtb_ref, alog_ref, d_ref, nw_ref, tri_ref, trit_ref, e_ref, st_ref, dy_ref,
             dxbc_ref, ddt_ref, dz_ref, ddtb_ref, dalog_ref, dd_ref, dnw_ref, dstate):
        c = pl.program_id(0)

        @pl.when(c == 0)
        def _():
            dstate[...] = jnp.zeros_like(dstate)

        tri_v, trit_v, e_v = tri_ref[...], trit_ref[...], e_ref[...]
        _, vjp = jax.vjp(
            lambda *a: _ssd_chunk(*a, tri_v, trit_v, e_v),
            xbc_ref[:, 0:SSM_DINNER], xbc_ref[:, SSM_DINNER:SSM_DINNER + 256], xbc_ref[:, SSM_DINNER + 256:XBC],
            dt_ref[...], z_ref[...], st_ref[0], dtb_ref[...], alog_ref[...], d_ref[...], nw_ref[...])
        dxs, dbm, dcm, ddt, dz, ds, ddtb, dalog, dd, dnw = vjp((dy_ref[...].astype(F32), dstate[...]))
        dxbc_ref[:, 0:SSM_DINNER] = dxs
        dxbc_ref[:, SSM_DINNER:SSM_DINNER + 256] = dbm
        dxbc_ref[:, SSM_DINNER + 256:XBC] = dcm
        ddt_ref[...] = ddt.astype(ddt_ref.dtype)
        dz_ref[...] = dz.astype(dz_ref.dtype)
        dstate[...] = ds
        for r, d in ((ddtb_ref, ddtb), (dalog_ref, dalog), (dd_ref, dd), (dnw_ref, dnw)):
            @pl.when(c == 0)
            def _(r=r, d=d):
                r[...] = d

            @pl.when(c > 0)
            def _(r=r, d=d):
                r[...] += d

    whole = lambda shape: pl.BlockSpec(shape, lambda c: (0,) * len(shape))
    specs = specs + [pl.BlockSpec((1, SSM_N, SSM_DINNER), lambda c: (cidx(c), 0, 0)),
                     pl.BlockSpec((CHUNK, SSM_DINNER), lambda c: (cidx(c), 1))]
    return pl.pallas_call(
        body, name="ssd_bwd", grid=(nc,), in_specs=specs,
        out_specs=[pl.BlockSpec((CHUNK, XBC), lambda c: (cidx(c), 0)), pl.BlockSpec((CHUNK, LANE), lambda c: (cidx(c), 0)),
                   pl.BlockSpec((CHUNK, SSM_DINNER), lambda c: (cidx(c), 0)),
                   whole((1, LANE)), whole((1, LANE)), whole((1, LANE)), whole((1, SSM_DINNER))],
        out_shape=[jax.ShapeDtypeStruct((length, XBC), F32), jax.ShapeDtypeStruct((length, LANE), BF16),
                   jax.ShapeDtypeStruct((length, SSM_DINNER), BF16),
                   jax.ShapeDtypeStruct((1, LANE), F32), jax.ShapeDtypeStruct((1, LANE), F32),
                   jax.ShapeDtypeStruct((1, LANE), F32), jax.ShapeDtypeStruct((1, SSM_DINNER), F32)],
        scratch_shapes=[pltpu.VMEM((SSM_N, SSM_DINNER), F32)],
        compiler_params=_params(("arbitrary",)),
    )(xbc, proj, proj, dt_bias, a_log, d_skip, norm_w, tri, tri_t, expand, states, dmix)


def _cmul(ar, ai, br, bi):
    return ar * br - ai * bi, ar * bi + ai * br


def s5_scan(b_re, b_im, a_re, a_im, *, reverse=False, states=None, name, lw=256):
    length, lanes = b_re.shape
    nk = length // SCAN_SEG
    with_da = states is not None
    assert reverse or not with_da

    def shift(v):
        sub = lax.broadcasted_iota(jnp.int32, v.shape, 0)
        if reverse:
            return jnp.where(sub == SCAN_SEG - 1, 0.0, pltpu.roll(v, SCAN_SEG - 1, 0))
        return jnp.where(sub == 0, 0.0, pltpu.roll(v, 1, 0))

    def body(*refs):
        if with_da:
            bre_ref, bim_ref, are_ref, aim_ref, sre_ref, sim_ref, xre_ref, xim_ref, dare_ref, daim_ref = refs
        else:
            bre_ref, bim_ref, are_ref, aim_ref, xre_ref, xim_ref = refs
        ar = jnp.broadcast_to(are_ref[...], (SCAN_SEG, lw))
        ai = jnp.broadcast_to(aim_ref[...], (SCAN_SEG, lw))

        def tile(i):
            k = (nk - 1 - i) if reverse else i
            return pl.ds(pl.multiple_of(k * SCAN_SEG, SCAN_SEG), SCAN_SEG)

        def local(i, carry):
            xr, xi, pr, pi = carry
            rows = tile(i)
            mr, mi = _cmul(ar, ai, xr, xi)
            xr, xi = mr + bre_ref[rows, :], mi + bim_ref[rows, :]
            xre_ref[rows, :] = xr
            xim_ref[rows, :] = xi
            pr, pi = _cmul(ar, ai, pr, pi)
            return xr, xi, pr, pi

        zero = jnp.zeros((SCAN_SEG, lw), F32)
        one = jnp.ones((SCAN_SEG, lw), F32)
        er, ei, pr, pi = lax.fori_loop(0, nk, local, (zero, zero, one, zero))
        cr, ci = zero, zero
        for _ in range(SCAN_SEG - 1):
            mr, mi = _cmul(pr, pi, cr, ci)
            cr, ci = shift(er + mr), shift(ei + mi)

        def fix(i, carry):
            pr, pi, dr, di = carry
            rows = tile(i)
            pr, pi = _cmul(ar, ai, pr, pi)
            mr, mi = _cmul(pr, pi, cr, ci)
            xr, xi = xre_ref[rows, :] + mr, xim_ref[rows, :] + mi
            xre_ref[rows, :] = xr
            xim_ref[rows, :] = xi
            if with_da:
                k = nk - 1 - i
                prev = pl.ds(pl.multiple_of(jnp.maximum(k - 1, 0) * SCAN_SEG, SCAN_SEG), SCAN_SEG)
                last = pl.ds((nk - 1) * SCAN_SEG, SCAN_SEG)
                sub = lax.broadcasted_iota(jnp.int32, (SCAN_SEG, lw), 0)
                wr = jnp.where(sub == 0, 0.0, pltpu.roll(sre_ref[last, :], 1, 0))
                wi = jnp.where(sub == 0, 0.0, pltpu.roll(sim_ref[last, :], 1, 0))
                sr = jnp.where(k == 0, wr, sre_ref[prev, :])
                si = jnp.where(k == 0, wi, sim_ref[prev, :])
                dr, di = dr + xr * sr + xi * si, di + xi * sr - xr * si
            return pr, pi, dr, di

        _, _, dr, di = lax.fori_loop(0, nk, fix, (one, zero, zero, zero))
        if with_da:
            dare_ref[...] = jnp.sum(dr, axis=0, keepdims=True)
            daim_ref[...] = jnp.sum(di, axis=0, keepdims=True)

    col = pl.BlockSpec((length, lw), lambda j: (0, j))
    vec = pl.BlockSpec((1, lw), lambda j: (0, j))
    ins = [b_re, b_im, a_re, a_im] + (list(states) if with_da else [])
    in_specs = [col, col, vec, vec] + ([col, col] if with_da else [])
    out_specs = [col, col] + ([vec, vec] if with_da else [])
    out_shape = [jax.ShapeDtypeStruct((length, lanes), F32)] * 2 + ([jax.ShapeDtypeStruct((1, lanes), F32)] * 2 if with_da else [])
    return pl.pallas_call(
        body, name=name, grid=(lanes // lw,), in_specs=in_specs, out_specs=out_specs, out_shape=out_shape,
        compiler_params=_params(("parallel",)),
    )(*ins)


def _seg_interleave(v):
    length = v.shape[0]
    return v.reshape(SCAN_SEG, length // SCAN_SEG, -1).transpose(1, 0, 2).reshape(length, -1)


def _seg_deinterleave(v):
    length = v.shape[0]
    return v.reshape(length // SCAN_SEG, SCAN_SEG, -1).transpose(1, 0, 2).reshape(length, -1)


def _block_diag(m):
    eye = jnp.eye(S5_GROUPS, dtype=m.dtype)
    return (m.reshape(S5_GROUPS, S5_GROUP, 1, S5_STATE) * eye[:, None, :, None]).reshape(S5_GROUPS * S5_GROUP, S5_LANES)


def _block_diag_take(full):
    idx = jnp.arange(S5_GROUPS)
    blocks = full.reshape(S5_GROUPS, S5_GROUP, S5_GROUPS, S5_STATE)[idx, :, idx, :]
    return blocks.reshape(S5_GROUPS * S5_GROUP, S5_STATE)


def _s5_prep(a_re, a_im, log_step, b_re, b_im, rep):
    step = jnp.exp(log_step)
    mag = jnp.exp(a_re * step)
    ab_re = mag * jnp.cos(a_im * step)
    ab_im = mag * jnp.sin(a_im * step)
    den = a_re * a_re + a_im * a_im
    f_re = ((ab_re - 1.0) * a_re + ab_im * a_im) / den
    f_im = (ab_im * a_re - (ab_re - 1.0) * a_im) / den
    fr, fi = _dot_hi(rep, f_re), _dot_hi(rep, f_im)
    return ab_re, ab_im, fr * b_re - fi * b_im, fr * b_im + fi * b_re


def _rms(x, g):
    return (x * lax.rsqrt(jnp.mean(x * x, axis=-1, keepdims=True) + EPS) * g,)


def _ffn_act(gate, up):
    return (_silu(gate) * up,)


def _glu(a, g):
    return (a * jax.nn.sigmoid(g),)


def _ln_silu(x, g, b):
    xc = x - jnp.mean(x, axis=-1, keepdims=True)
    var = jnp.mean(xc * xc, axis=-1, keepdims=True)
    return (_silu(xc * lax.rsqrt(var + EPS) * g + b),)


def _s5_post(y, u, d_skip, glu_w):
    s = jax.nn.gelu(y + d_skip * u)
    return (s * jax.nn.sigmoid(_mm(s, glu_w)),)


def loss_head(x, tgt, g, *, tl=512):
    length, d = x.shape
    tl = min(tl, length)

    def body(x_ref, t_ref, g_ref, loss_ref, dx_ref, dg_ref):
        i = pl.program_id(0)
        y, vjp = jax.vjp(lambda x, g: _rms(x, g)[0], x_ref[...], g_ref[...])
        err = y - t_ref[...]
        dx, dg = vjp(err * (1.0 / d))
        dx_ref[...] = dx
        part = jnp.broadcast_to(0.5 * jnp.sum(jnp.mean(err * err, axis=-1, keepdims=True), axis=0, keepdims=True), (1, LANE))

        @pl.when(i == 0)
        def _():
            loss_ref[...] = part
            dg_ref[...] = dg

        @pl.when(i > 0)
        def _():
            loss_ref[...] += part
            dg_ref[...] += dg

    row = pl.BlockSpec((tl, d), lambda i: (i, 0))
    return pl.pallas_call(
        body, name="loss_head", grid=(length // tl,),
        in_specs=[row, row, pl.BlockSpec((1, d), lambda i: (0, 0))],
        out_specs=[pl.BlockSpec((1, LANE), lambda i: (0, 0)), row, pl.BlockSpec((1, d), lambda i: (0, 0))],
        out_shape=[jax.ShapeDtypeStruct((1, LANE), F32), jax.ShapeDtypeStruct((length, d), F32),
                   jax.ShapeDtypeStruct((1, d), F32)],
        compiler_params=_params(("arbitrary",)),
    )(x, tgt, g)


def _pad_heads(v):
    return jnp.pad(v, ((0, 0), (0, LANE - v.shape[1])))


def local_step(x, tgt, w, late_weights=None, early_reduce=None):
    length = x.shape[0]
    cos2, sin2 = _rotary_tables(length)
    grads = {}
    w = dict(w)

    def rms_fwd(xin, g, name):
        return rowwise_fwd(_rms, [xin], [], [g], [], [(D_MODEL, BF16)], name=name, tl=512)[0]

    def rms_bwd(xin, g, dh, dxo, name):
        return rowwise_bwd(_rms, [xin], [], [g], [], [dh], [F32], name=name, tl=512, add=dxo)

    def ffn_fwd(i, xin):
        hf = rms_fwd(xin, w["ffn_norm"][i:i + 1], f"ffn{i}_norm")
        a = matmul(hf, w["ffn_w_up"][i], name=f"ffn{i}_up")
        act = ffn_conv_act(a, w["ffn_dw_w"][i], w["ffn_dw_b"][i:i + 1], name=f"ffn{i}_conv_act")
        return matmul(act, w["ffn_w_down"][i], res=xin, name=f"ffn{i}_down"), (hf, a, act)

    def ffn_bwd(i, xin, saved, dxo):
        hf, a, act = saved
        dact = matmul(dxo, w["ffn_w_down"][i], tb=True, name=f"ffn{i}_down_dx")
        dw_down = matmul(act, dxo, ta=True, name=f"ffn{i}_down_dw")
        da, ddw_w, ddw_b = ffn_conv_act_bwd(a, w["ffn_dw_w"][i], w["ffn_dw_b"][i:i + 1], dact,
                                            name=f"ffn{i}_conv_act_bwd")
        dw_up = matmul(hf, da, ta=True, name=f"ffn{i}_up_dw")
        dhf = matmul(da, w["ffn_w_up"][i], tb=True, name=f"ffn{i}_up_dx")
        dxin, dnorm = rms_bwd(xin, w["ffn_norm"][i:i + 1], dhf, dxo, f"ffn{i}_norm_bwd")
        return dxin, dict(ffn_norm=dnorm, ffn_w_up=dw_up, ffn_dw_w=ddw_w, ffn_dw_b=ddw_b, ffn_w_down=dw_down)

    w_in_e = jnp.pad(w["e_w_in"][0], ((0, 0), (0, EVEN_IN_PAD - EVEN_IN)))
    conv_w_e, conv_b_e = w["e_conv_w"][0], w["e_conv_b"]
    dt_bias, a_log, d_skip = _pad_heads(w["e_dt_bias"]), _pad_heads(w["e_a_log"]), _pad_heads(w["e_d"])
    xbc_off = 4 * D_MODEL

    hn0 = rms_fwd(x, w["mix_norm"][0:1], "mix0_norm")
    proj0 = matmul(hn0, w_in_e, name="even_in")
    y_ret, ret_states = retention_fwd(proj0, cos2, sin2)
    xbc = conv_fwd(proj0, conv_w_e, conv_b_e, act=True, off=xbc_off, name="ssd_conv")
    y_ssm, ssd_states = ssd_fwd(xbc, proj0, dt_bias, a_log, d_skip, w["e_ssm_norm"])
    mix0 = jnp.concatenate([y_ret, y_ssm], axis=1)
    if late_weights is not None:
        w.update(late_weights(y_ssm))
    w_out_e = w["e_w_out"][0]
    x1 = matmul(mix0, w_out_e, res=x, name="even_out")
    x2, ffn0_saved = ffn_fwd(0, x1)

    w_in_o, w_out_o, glu_w = w["o_w_in"][0], w["o_w_out"][0], w["o_glu_w"][0]
    dw_w_o, dw_b_o, ln_g, ln_b, d_o = w["o_dw_w"][0], w["o_dw_b"], w["o_ln_g"], w["o_ln_b"], w["o_d"]
    rep = jnp.asarray(np.repeat(np.eye(S5_GROUPS, dtype=np.float32), S5_GROUP, axis=0))
    rows_gc = (S5_GROUPS * S5_GROUP, S5_STATE)
    prep_in = [w["o_a_re"][0], w["o_a_im"][0], w["o_log_step"].reshape(S5_GROUPS, 1),
               w["o_b_re"][0].transpose(0, 2, 1).reshape(rows_gc), w["o_b_im"][0].transpose(0, 2, 1).reshape(rows_gc), rep]
    ab_re, ab_im, bb_re, bb_im = whole_fwd(
        _s5_prep, prep_in, [(S5_GROUPS, S5_STATE)] * 2 + [rows_gc] * 2, name="s5_prep")
    a_re_row, a_im_row = ab_re.reshape(1, S5_LANES), ab_im.reshape(1, S5_LANES)
    b_re_bd, b_im_bd = _block_diag(bb_re).astype(BF16), _block_diag(bb_im).astype(BF16)
    c_re_bd = _block_diag(w["o_c_re"][0].reshape(rows_gc)).astype(BF16)
    c_im_neg_bd = _block_diag(-w["o_c_im"][0].reshape(rows_gc)).astype(BF16)

    hn1 = rms_fwd(x2, w["mix_norm"][1:2], "mix1_norm")
    proj1 = matmul(hn1, w_in_o, name="odd_in")
    half = D_MODEL // 2
    c_glu = rowwise_fwd(_glu, [Cols(proj1, half, 0), Cols(proj1, half, 1)], [], [], [], [(half, F32)],
                        name="conf_glu", tl=512)[0]
    c_conv = conv_fwd(c_glu, dw_w_o, dw_b_o, act=False, name="conf_conv")
    c_out = rowwise_fwd(_ln_silu, [c_conv], [], [ln_g, ln_b], [], [(half, BF16)], name="conf_ln", tl=512)[0]
    u_seg = _seg_interleave(proj1[:, 2 * half:])
    bu_re = matmul(u_seg, b_re_bd, name="s5_bu_re")
    bu_im = matmul(u_seg, b_im_bd, name="s5_bu_im")
    xs_re, xs_im = s5_scan(bu_re, bu_im, a_re_row, a_im_row, name="s5_scan")
    y_im = matmul(xs_im, c_im_neg_bd, tb=True, name="s5_y_im")
    y_s5 = _seg_deinterleave(matmul(xs_re, c_re_bd, tb=True, res=y_im, name="s5_y_re"))
    s_out = rowwise_fwd(_s5_post, [y_s5, Cols(proj1, half, 2)], [], [d_o, glu_w], [], [(half, BF16)],
                        name="s5_post", tl=512)[0]
    mix1 = jnp.concatenate([c_out, s_out], axis=1)
    x3 = matmul(mix1, w_out_o, res=x2, name="odd_out")
    x4, ffn1_saved = ffn_fwd(1, x3)

    loss, dx4, dfinal = loss_head(x4, tgt, w["final_norm"].reshape(1, D_MODEL))
    grads["final_norm"] = dfinal.reshape(D_MODEL)

    dx3, g_ffn1 = ffn_bwd(1, x3, ffn1_saved, dx4)
    dmix1 = matmul(dx3, w_out_o, tb=True, name="odd_out_dx")
    grads["o_w_out"] = [matmul(mix1, dx3, ta=True, name="odd_out_dw")]
    dc_conv, dln_g, dln_b = rowwise_bwd(_ln_silu, [c_conv], [], [ln_g, ln_b], [], [Cols(dmix1, half, 0)], [F32],
                                        name="conf_ln_bwd", tl=512)
    dc_glu, ddw_w_o, ddw_b_o = conv_bwd(c_glu, dw_w_o, dw_b_o, dc_conv, act=False, name="conf_conv_bwd")
    d_cacg = rowwise_bwd(_glu, [Cols(proj1, half, 0), Cols(proj1, half, 1)], [], [], [], [dc_glu], [BF16],
                         name="conf_glu_bwd", tl=512, merge=True)[0]
    dy_s5, du_post, dd_o, dglu_w = rowwise_bwd(
        _s5_post, [y_s5, Cols(proj1, half, 2)], [], [d_o, glu_w], [], [Cols(dmix1, half, 1)], [F32, F32],
        name="s5_post_bwd", tl=512)
    dy_seg = _seg_interleave(dy_s5)
    dxs_re = matmul(dy_seg, c_re_bd, name="s5_dx_re")
    dxs_im = matmul(dy_seg, c_im_neg_bd, name="s5_dx_im")
    dc_re_bd = matmul(dy_seg, xs_re, ta=True, name="s5_dc_re")
    dc_im_neg_bd = matmul(dy_seg, xs_im, ta=True, name="s5_dc_im")
    g_re, g_im, dab_re, dab_im = s5_scan(dxs_re, dxs_im, a_re_row, -a_im_row, reverse=True, states=(xs_re, xs_im),
                                         name="s5_scan_bwd", lw=LANE)
    dbb_re = _block_diag_take(matmul(u_seg, g_re, ta=True, name="s5_db_re"))
    dbb_im = _block_diag_take(matmul(u_seg, g_im, ta=True, name="s5_db_im"))
    du_im = matmul(g_im, b_im_bd, tb=True, name="s5_du_im")
    du = _seg_deinterleave(matmul(g_re, b_re_bd, tb=True, res=du_im, name="s5_du_re")) + du_post
    da_re, da_im, dlog_step, db_re, db_im = whole_bwd(
        _s5_prep, prep_in, 5,
        [dab_re.reshape(S5_GROUPS, S5_STATE), dab_im.reshape(S5_GROUPS, S5_STATE), dbb_re, dbb_im], name="s5_prep_bwd")
    gcn = (S5_GROUPS, S5_GROUP, S5_STATE)
    grads.update(
        o_a_re=da_re[None], o_a_im=da_im[None], o_log_step=dlog_step.reshape(1, S5_GROUPS),
        o_b_re=db_re.reshape(gcn).transpose(0, 2, 1)[None], o_b_im=db_im.reshape(gcn).transpose(0, 2, 1)[None],
        o_c_re=_block_diag_take(dc_re_bd).reshape(gcn)[None], o_c_im=-_block_diag_take(dc_im_neg_bd).reshape(gcn)[None],
        o_d=dd_o, o_glu_w=[dglu_w], o_dw_w=ddw_w_o[None], o_dw_b=ddw_b_o, o_ln_g=dln_g, o_ln_b=dln_b)
    dproj1 = jnp.concatenate([d_cacg, du.astype(BF16)], axis=1)
    grads["o_w_in"] = [matmul(hn1, dproj1, ta=True, name="odd_in_dw")]
    dhn1 = matmul(dproj1, w_in_o, tb=True, name="odd_in_dx")
    dx2, dmix_norm1 = rms_bwd(x2, w["mix_norm"][1:2], dhn1, dx3, "mix1_norm_bwd")

    if early_reduce is not None:
        zero = early_reduce({("o_w_in", 0): grads["o_w_in"][0], ("o_glu_w", 0): grads["o_glu_w"][0],
                             ("o_w_out", 0): grads["o_w_out"][0], ("ffn_w_up", 1): g_ffn1["ffn_w_up"],
                             ("ffn_w_down", 1): g_ffn1["ffn_w_down"]})
        w["ffn_dw_b"] = w["ffn_dw_b"] + zero
    dx1, g_ffn0 = ffn_bwd(0, x1, ffn0_saved, dx2)
    for k in g_ffn0:
        per_layer = [g_ffn0[k], g_ffn1[k]]
        grads[k] = per_layer if k in ("ffn_w_up", "ffn_w_down") else jnp.stack(per_layer).reshape(w[k].shape)
    dmix0 = matmul(dx1, w_out_e, tb=True, name="even_out_dx")
    grads["e_w_out"] = [matmul(mix0, dx1, ta=True, name="even_out_dw")]
    dq, dk, dv, dg = retention_bwd(proj0, cos2, sin2, ret_states, dmix0)
    dxbc_c, ddt, dz, ddt_bias, da_log, dd_skip, dssm_norm = ssd_bwd(
        xbc, proj0, dt_bias, a_log, d_skip, w["e_ssm_norm"], ssd_states, dmix0)
    dxbc, dconv_w, dconv_b = conv_bwd(proj0, conv_w_e, conv_b_e, dxbc_c, act=True, off=xbc_off,
                                      name="ssd_conv_bwd", dx_dtype=BF16)
    dproj0 = jnp.concatenate([dq, dk, dv, dg, dz, dxbc, ddt], axis=1)
    grads["e_w_in"] = [matmul(hn0, dproj0, ta=True, name="even_in_dw")[:, :EVEN_IN]]
    dhn0 = matmul(dproj0, w_in_e, tb=True, name="even_in_dx")
    dx, dmix_norm0 = rms_bwd(x, w["mix_norm"][0:1], dhn0, dx1, "mix0_norm_bwd")
    grads.update(
        mix_norm=jnp.concatenate([dmix_norm0, dmix_norm1], axis=0), e_conv_w=dconv_w[None], e_conv_b=dconv_b,
        e_dt_bias=ddt_bias[:, :SSM_HEADS], e_a_log=da_log[:, :SSM_HEADS], e_d=dd_skip[:, :SSM_HEADS],
        e_ssm_norm=dssm_norm)
    return loss, dx, grads


def adamw(w, g, m, v, *, name):
    shape = w.shape
    cols = shape[-1]
    rows = w.size // cols
    tr = _tile(rows, max(8, (512 * 1024 // cols) // 8 * 8), unit=8)

    def body(w_ref, g_ref, m_ref, v_ref, d_ref, nm_ref, nv_ref):
        gv = g_ref[...]
        nm = ADAM_B1 * m_ref[...] + (1.0 - ADAM_B1) * gv
        nv = ADAM_B2 * v_ref[...] + (1.0 - ADAM_B2) * jnp.square(gv)
        m_hat = nm / (1.0 - ADAM_B1 ** ADAM_STEP)
        v_hat = nv / (1.0 - ADAM_B2 ** ADAM_STEP)
        d_ref[...] = -ADAM_LR * (m_hat / (jnp.sqrt(v_hat) + ADAM_EPS) + ADAM_WD * w_ref[...])
        nm_ref[...] = nm
        nv_ref[...] = nv

    spec = pl.BlockSpec((tr, cols), lambda i: (i, 0))
    outs = pl.pallas_call(
        body, name=name, grid=(rows // tr,), in_specs=[spec] * 4, out_specs=[spec] * 3,
        out_shape=[jax.ShapeDtypeStruct((rows, cols), F32)] * 3, compiler_params=_params(("parallel",)),
    )(*[t.reshape(rows, cols) for t in (w, g, m, v)])
    return [o.reshape(shape) for o in outs]


OTHER_CHIPS = ((1, 0), (0, 1), (1, 1))
ANY = pl.BlockSpec(memory_space=pl.ANY)


def _position():
    return lax.axis_index("x"), lax.axis_index("y"), lax.axis_index("c")


def _flip(v, f):
    return 1 - v if f else v


def _remote(src, dst, send_sem, recv_sem, device):
    return pltpu.make_async_remote_copy(src_ref=src, dst_ref=dst, send_sem=send_sem, recv_sem=recv_sem,
                                        device_id=device, device_id_type=MESH)


def gather_shards(big, small):
    n_big, n_small = len(big), len(small)
    halves = [a.shape[0] // 2 for a in big]

    def body(*refs):
        big_refs, small_refs = refs[:n_big], refs[n_big:n_big + n_small]
        obig_refs = refs[n_big + n_small:2 * n_big + n_small]
        osmall_refs = refs[2 * n_big + n_small:2 * (n_big + n_small)]
        ici_send, ici_recv, d2d_send, d2d_recv, small_send, small_recv = refs[2 * (n_big + n_small):]
        x, y, c = _position()
        mine = 2 * x + y

        def half(k, core):
            return pl.ds(pl.multiple_of(core * halves[k], 16), halves[k])

        sends = []
        for j, (fx, fy) in enumerate(OTHER_CHIPS):
            peer = (_flip(x, fx), _flip(y, fy), c)
            for k in range(n_big):
                sends.append(_remote(big_refs[k].at[half(k, c)], obig_refs[k].at[mine, half(k, c)],
                                     ici_send.at[j, k], ici_recv.at[j, k], peer))
            for k in range(n_small):
                sends.append(_remote(small_refs[k], osmall_refs[k].at[mine], small_send.at[j, k], small_recv.at[j, k], peer))
        for cp in sends:
            cp.start()
        for j, (fx, fy) in enumerate(OTHER_CHIPS):
            px, py = _flip(x, fx), _flip(y, fy)
            src_chip = 2 * px + py
            for k in range(n_big):
                landed = obig_refs[k].at[src_chip, half(k, c)]
                _remote(landed, landed, ici_send.at[j, k], ici_recv.at[j, k], (px, py, c)).wait_recv()
                fwd = _remote(landed, landed, d2d_send.at[j, k], d2d_recv.at[j, k], (x, y, 1 - c))
                fwd.start()
                sends.append(fwd)
        for j, (fx, fy) in enumerate(OTHER_CHIPS):
            px, py = _flip(x, fx), _flip(y, fy)
            src_chip = 2 * px + py
            for k in range(n_big):
                other = obig_refs[k].at[src_chip, half(k, 1 - c)]
                _remote(other, other, d2d_send.at[j, k], d2d_recv.at[j, k], (x, y, 1 - c)).wait_recv()
            for k in range(n_small):
                dst = osmall_refs[k].at[src_chip]
                _remote(small_refs[k], dst, small_send.at[j, k], small_recv.at[j, k], (px, py, c)).wait_recv()
        for cp in sends:
            cp.wait_send()

    arrays = list(big) + list(small)
    dma = pltpu.SemaphoreType.DMA
    return pl.pallas_call(
        body, name="gather_shards", in_specs=[ANY] * len(arrays), out_specs=[ANY] * len(arrays),
        out_shape=[jax.ShapeDtypeStruct((4,) + a.shape, a.dtype) for a in arrays],
        scratch_shapes=[dma((3, n_big)), dma((3, n_big)), dma((3, n_big)), dma((3, n_big)),
                        dma((3, n_small)), dma((3, n_small))],
        compiler_params=_params(),
    )(*arrays)


def allreduce_small(pack):
    rows = pack.shape[0]

    def body(p_ref, o_ref, slots, send_sems, recv_sems):
        x, y, c = _position()
        me = 4 * x + 2 * y + c
        slots[me] = p_ref[...]
        flips = [((k >> 2) & 1, (k >> 1) & 1, k & 1) for k in range(1, 8)]
        sends = []
        for k, (fx, fy, fc) in enumerate(flips):
            peer = (_flip(x, fx), _flip(y, fy), _flip(c, fc))
            sends.append(_remote(p_ref, slots.at[me], send_sems.at[k], recv_sems.at[k], peer))
        for cp in sends:
            cp.start()
        for k, (fx, fy, fc) in enumerate(flips):
            px, py, pc = _flip(x, fx), _flip(y, fy), _flip(c, fc)
            _remote(p_ref, slots.at[4 * px + 2 * py + pc], send_sems.at[k], recv_sems.at[k], (px, py, pc)).wait_recv()
        for cp in sends:
            cp.wait_send()
        acc = slots[0]
        for d in range(1, 8):
            acc = acc + slots[d]
        o_ref[...] = acc

    vmem = pl.BlockSpec(memory_space=pltpu.VMEM)
    return pl.pallas_call(
        body, name="allreduce_small", in_specs=[vmem], out_specs=vmem,
        out_shape=jax.ShapeDtypeStruct(pack.shape, F32),
        scratch_shapes=[pltpu.VMEM((8, rows, LANE), F32), pltpu.SemaphoreType.DMA((7,)), pltpu.SemaphoreType.DMA((7,))],
        compiler_params=_params(),
    )(pack)


def exchange_halves(gs, *, name):
    n = len(gs)

    def body(*refs):
        g_refs, o_refs, (send_sems, recv_sems) = refs[:n], refs[n:2 * n], refs[2 * n:]
        x, y, c = _position()
        copies = [_remote(g_refs[k].at[:, 1 - c], o_refs[k], send_sems.at[k], recv_sems.at[k], (x, y, 1 - c)) for k in range(n)]
        for cp in copies:
            cp.start()
        for cp in copies:
            cp.wait()

    return pl.pallas_call(
        body, name=name, in_specs=[ANY] * n, out_specs=[ANY] * n,
        out_shape=[jax.ShapeDtypeStruct((4,) + g.shape[2:], g.dtype) for g in gs],
        scratch_shapes=[pltpu.SemaphoreType.DMA((n,)), pltpu.SemaphoreType.DMA((n,))],
        compiler_params=_params(),
    )(*gs)


def scatter_to_chips(parts):
    n = len(parts)

    def body(*refs):
        a_refs, o_refs, (send_sems, recv_sems) = refs[:n], refs[n:2 * n], refs[2 * n:]
        x, y, c = _position()
        copies = []
        for j, (fx, fy) in enumerate(OTHER_CHIPS):
            px, py = _flip(x, fx), _flip(y, fy)
            for k in range(n):
                copies.append(_remote(a_refs[k].at[2 * px + py], o_refs[k].at[j], send_sems.at[j, k], recv_sems.at[j, k], (px, py, c)))
        for cp in copies:
            cp.start()
        for cp in copies:
            cp.wait()

    return pl.pallas_call(
        body, name="scatter_to_chips", in_specs=[ANY] * n, out_specs=[ANY] * n,
        out_shape=[jax.ShapeDtypeStruct((3,) + a.shape[1:], a.dtype) for a in parts],
        scratch_shapes=[pltpu.SemaphoreType.DMA((3, n)), pltpu.SemaphoreType.DMA((3, n))],
        compiler_params=_params(),
    )(*parts)


def swap_halves(rs):
    n = len(rs)

    def body(*refs):
        r_refs, o_refs, (send_sems, recv_sems) = refs[:n], refs[n:2 * n], refs[2 * n:]
        x, y, c = _position()
        copies = [_remote(r_refs[k], o_refs[k], send_sems.at[k], recv_sems.at[k], (x, y, 1 - c)) for k in range(n)]
        for cp in copies:
            cp.start()
        for cp in copies:
            cp.wait()

    dma = pltpu.SemaphoreType.DMA
    return pl.pallas_call(
        body, name="swap_halves", in_specs=[ANY] * n, out_specs=[ANY] * n,
        out_shape=[jax.ShapeDtypeStruct(r.shape, r.dtype) for r in rs],
        scratch_shapes=[dma((n,)), dma((n,))],
        compiler_params=_params(),
    )(*rs)


HBM = pl.BlockSpec(memory_space=pltpu.HBM)
SEM = pl.BlockSpec(memory_space=pltpu.SEMAPHORE)
SIDE_EFFECT = pltpu.SideEffectType.DATAFLOW_SIDE_EFFECTING


def _gather_plan(halves):
    def plan(v_refs, land_refs, x, y, c):
        copies = []
        for fx, fy in OTHER_CHIPS:
            for k in range(len(v_refs)):
                rows = pl.ds(pl.multiple_of(c * halves[k], 16), halves[k])
                copies.append((v_refs[k].at[rows], land_refs[k].at[2 * x + y, rows], (_flip(x, fx), _flip(y, fy), c)))
        return copies
    return plan


def _scatter_plan(v_refs, land_refs, x, y, c):
    copies = []
    for j, (fx, fy) in enumerate(OTHER_CHIPS):
        px, py = _flip(x, fx), _flip(y, fy)
        for k in range(len(v_refs)):
            copies.append((v_refs[k].at[2 * px + py], land_refs[k].at[j], (px, py, c)))
    return copies


def chip_exchange_start(srcs, land_shapes, plan, after, *, name):
    n = len(srcs)
    n_cp = 3 * n

    def body(*refs):
        v_refs, land_refs = refs[:n], refs[n:2 * n]
        outs = refs[2 * n + 1:]
        sends, recvs, token = outs[:n_cp], outs[n_cp:2 * n_cp], outs[-1]
        x, y, c = _position()
        for (src, dst, device), send, recv in zip(plan(v_refs, land_refs, x, y, c), sends, recvs, strict=True):
            _remote(src, dst, send, recv, device).start()
        token[...] = jnp.zeros_like(token)

    lands = [lax.empty(shape, v.dtype) for shape, v in zip(land_shapes, srcs)]
    arrays = [pltpu.with_memory_space_constraint(a, pltpu.HBM) for a in list(srcs) + lands]
    outs = pl.pallas_call(
        body, name=name,
        out_shape=tuple(pltpu.SemaphoreType.DMA(()) for _ in range(2 * n_cp))
        + tuple(pltpu.HBM(a.shape, a.dtype) for a in arrays) + (jax.ShapeDtypeStruct((8, LANE), F32),),
        in_specs=[HBM] * (2 * n) + [ANY],
        out_specs=(SEM,) * (2 * n_cp) + (HBM,) * (2 * n) + (pl.BlockSpec(memory_space=pltpu.VMEM),),
        input_output_aliases={i: 2 * n_cp + i for i in range(2 * n)},
        compiler_params=pltpu.CompilerParams(has_side_effects=SIDE_EFFECT),
    )(*arrays, after)
    handle = (outs[:n_cp], outs[n_cp:2 * n_cp], outs[2 * n_cp:2 * n_cp + n], outs[2 * n_cp + n:2 * n_cp + 2 * n])
    return handle, outs[-1]


def chip_exchange_wait(handle, plan, after, *, name):
    sends, recvs, v_thru, land_thru = handle
    n = len(v_thru)
    n_cp = 3 * n

    def body(*refs):
        v_refs, land_refs = refs[:n], refs[n:2 * n]
        sends, recvs = refs[2 * n:2 * n + n_cp], refs[2 * n + n_cp:2 * n + 2 * n_cp]
        x, y, c = _position()
        for (src, dst, device), send, recv in zip(plan(v_refs, land_refs, x, y, c), sends, recvs, strict=True):
            copy = _remote(src, dst, send, recv, device)
            copy.wait_send()
            copy.wait_recv()

    outs = pl.pallas_call(
        body, name=name,
        out_shape=tuple(pltpu.HBM(a.shape, a.dtype) for a in list(v_thru) + list(land_thru)),
        in_specs=[HBM] * (2 * n) + [SEM] * (2 * n_cp) + [ANY], out_specs=(HBM,) * (2 * n),
        input_output_aliases={i: i for i in range(2 * n)},
        compiler_params=pltpu.CompilerParams(has_side_effects=SIDE_EFFECT),
    )(*v_thru, *land_thru, *sends, *recvs, after)
    return outs[:n], outs[n:]


def finish_gather(lands):
    n = len(lands)
    halves = [a.shape[1] // 2 for a in lands]

    def body(*refs):
        o_refs, (send_sems, recv_sems) = refs[n:2 * n], refs[2 * n:]
        x, y, c = _position()

        def half(k, core):
            return pl.ds(pl.multiple_of(core * halves[k], 16), halves[k])

        sends = []
        for j, (fx, fy) in enumerate(OTHER_CHIPS):
            src_chip = 2 * _flip(x, fx) + _flip(y, fy)
            for k in range(n):
                held = o_refs[k].at[src_chip, half(k, c)]
                sends.append(_remote(held, held, send_sems.at[j, k], recv_sems.at[j, k], (x, y, 1 - c)))
        for cp in sends:
            cp.start()
        for j, (fx, fy) in enumerate(OTHER_CHIPS):
            src_chip = 2 * _flip(x, fx) + _flip(y, fy)
            for k in range(n):
                other = o_refs[k].at[src_chip, half(k, 1 - c)]
                _remote(other, other, send_sems.at[j, k], recv_sems.at[j, k], (x, y, 1 - c)).wait_recv()
        for cp in sends:
            cp.wait_send()

    dma = pltpu.SemaphoreType.DMA
    return pl.pallas_call(
        body, name="finish_gather", in_specs=[ANY] * n, out_specs=[ANY] * n,
        out_shape=[jax.ShapeDtypeStruct(a.shape, a.dtype) for a in lands],
        input_output_aliases={k: k for k in range(n)},
        scratch_shapes=[dma((3, n)), dma((3, n))],
        compiler_params=_params(),
    )(*lands)


def add_own_half(g, r, c_idx, *, name):
    _, _, h, cols = g.shape

    def body(c_ref, g_ref, r_ref, o_ref):
        o_ref[...] = (g_ref[0] + r_ref[...]).astype(o_ref.dtype)

    return pl.pallas_call(
        body, name=name,
        grid_spec=pltpu.PrefetchScalarGridSpec(
            num_scalar_prefetch=1, grid=(4,),
            in_specs=[pl.BlockSpec((1, 1, h, cols), lambda s, c: (s, c[0], 0, 0)),
                      pl.BlockSpec((1, h, cols), lambda s, c: (s, 0, 0))],
            out_specs=pl.BlockSpec((1, h, cols), lambda s, c: (s, 0, 0))),
        out_shape=jax.ShapeDtypeStruct(r.shape, BF16), compiler_params=_params(("parallel",)),
    )(c_idx, g, r)


def add_chip_parts(a, parts, chip_idx, *, name):
    _, h, cols = a.shape
    th = h // 2

    def body(s_ref, a_ref, p0_ref, p1_ref, p2_ref, o_ref):
        f = lambda r: r[0].astype(F32)
        o_ref[...] = ((f(a_ref) + f(p0_ref)) + f(p1_ref)) + f(p2_ref)

    part = lambda j: pl.BlockSpec((1, th, cols), lambda i, s, j=j: (j, i, 0))
    return pl.pallas_call(
        body, name=name,
        grid_spec=pltpu.PrefetchScalarGridSpec(
            num_scalar_prefetch=1, grid=(2,),
            in_specs=[pl.BlockSpec((1, th, cols), lambda i, s: (s[0], i, 0)), part(0), part(1), part(2)],
            out_specs=pl.BlockSpec((th, cols), lambda i, s: (i, 0))),
        out_shape=jax.ShapeDtypeStruct((h, cols), F32), compiler_params=_params(("parallel",)),
    )(chip_idx, a, parts, parts, parts)


WEIGHTS = ("mix_norm", "e_w_in", "e_conv_w", "e_conv_b", "e_dt_bias", "e_a_log", "e_d", "e_ssm_norm", "e_w_out",
           "o_w_in", "o_dw_w", "o_dw_b", "o_ln_g", "o_ln_b", "o_a_re", "o_a_im", "o_b_re", "o_b_im", "o_c_re",
           "o_c_im", "o_d", "o_log_step", "o_glu_w", "o_w_out", "ffn_norm", "ffn_w_up", "ffn_dw_w", "ffn_dw_b",
           "ffn_w_down", "final_norm")
BIG = (("e_w_in", 2), ("e_w_out", 1), ("o_w_in", 2), ("o_glu_w", 1), ("o_w_out", 1), ("ffn_w_up", 2), ("ffn_w_down", 1))
SMALL_SHARDED = (("e_conv_w", 2), ("o_dw_w", 2), ("o_dw_b", 1), ("o_ln_g", 1), ("o_ln_b", 1), ("o_d", 1), ("ffn_dw_w", 2))
REPLICATED = tuple(n for n in WEIGHTS if n not in dict(BIG + SMALL_SHARDED))
PACK_ROWS = 8


def _pack(arrays, dtype, row_unit=PACK_ROWS):
    flat = jnp.concatenate([a.astype(dtype).reshape(-1) for a in arrays])
    rows = -(-flat.size // (LANE * row_unit)) * row_unit
    return jnp.pad(flat, (0, rows * LANE - flat.size)).reshape(rows, LANE)


def _unpack(flat, shapes, lead=()):
    out, off = [], 0
    for shape in shapes:
        size = int(np.prod(shape))
        out.append(flat[..., off:off + size].reshape(lead + tuple(shape)))
        off += size
    return out


def _join_shards(parts, axis):
    return jnp.concatenate([parts[s] for s in range(4)], axis=axis)


def _split_shards(full, axis):
    return jnp.stack(jnp.split(full, 4, axis=axis))


def _rows2d(a):
    return a.reshape(-1, a.shape[-1])


def _layer_shards(g, axis):
    rows, cols = g.shape
    if axis == 0:
        return g.reshape(4, 2, rows // 8, cols)
    return g.reshape(rows, 4, cols // 4).transpose(1, 0, 2).reshape(4, 2, rows // 2, cols // 4)


def kernel(x, mix_norm, e_w_in, e_conv_w, e_conv_b, e_dt_bias, e_a_log, e_d, e_ssm_norm, e_w_out, o_w_in, o_dw_w, o_dw_b, o_ln_g, o_ln_b, o_a_re, o_a_im, o_b_re, o_b_im, o_c_re, o_c_im, o_d, o_log_step, o_glu_w, o_w_out, ffn_norm, ffn_w_up, ffn_dw_w, ffn_dw_b, ffn_w_down, final_norm, loss_target, m_mix_norm, m_e_w_in, m_e_conv_w, m_e_conv_b, m_e_dt_bias, m_e_a_log, m_e_d, m_e_ssm_norm, m_e_w_out, m_o_w_in, m_o_dw_w, m_o_dw_b, m_o_ln_g, m_o_ln_b, m_o_a_re, m_o_a_im, m_o_b_re, m_o_b_im, m_o_c_re, m_o_c_im, m_o_d, m_o_log_step, m_o_glu_w, m_o_w_out, m_ffn_norm, m_ffn_w_up, m_ffn_dw_w, m_ffn_dw_b, m_ffn_w_down, m_final_norm, v_mix_norm, v_e_w_in, v_e_conv_w, v_e_conv_b, v_e_dt_bias, v_e_a_log, v_e_d, v_e_ssm_norm, v_e_w_out, v_o_w_in, v_o_dw_w, v_o_dw_b, v_o_ln_g, v_o_ln_b, v_o_a_re, v_o_a_im, v_o_b_re, v_o_b_im, v_o_c_re, v_o_c_im, v_o_d, v_o_log_step, v_o_glu_w, v_o_w_out, v_ffn_norm, v_ffn_w_up, v_ffn_dw_w, v_ffn_dw_b, v_ffn_w_down, v_final_norm):
    given = dict(locals())
    chip = 2 * lax.axis_index("x") + lax.axis_index("y")
    core = lax.axis_index("c")

    core_idx, chip_idx = core.reshape(1).astype(jnp.int32), chip.reshape(1).astype(jnp.int32)

    def whole(n, axis, parts):
        shape = given[n].shape
        own = given[n].astype(parts.dtype)
        return _join_shards(lax.dynamic_update_index_in_dim(parts.reshape((4,) + shape), own, chip, 0), axis)

    first, later = BIG[:1], BIG[1:]
    shards = {n: _rows2d(given[n]).astype(BF16) for n, _ in BIG}
    gathered = gather_shards([shards[n] for n, _ in first], [_rows2d(given[n]) for n, _ in SMALL_SHARDED])
    w = {n: given[n] for n in REPLICATED}
    for (n, axis), parts in zip(first + SMALL_SHARDED, gathered):
        w[n] = whole(n, axis, parts)
    later_shards = [shards[n] for n, _ in later]
    gather_plan = _gather_plan([a.shape[0] // 2 for a in later_shards])
    gather_handle, token = chip_exchange_start(later_shards, [(4,) + a.shape for a in later_shards], gather_plan,
                                               gathered[0], name="gather_start")
    w["mix_norm"] = w["mix_norm"] + token[0, 0]

    def late_weights(after):
        _, lands = chip_exchange_wait(gather_handle, gather_plan, after, name="gather_wait")
        return {n: whole(n, axis, parts) for (n, axis), parts in zip(later, finish_gather(lands))}

    pending = {}

    def early_reduce(layer_grads):
        keys = list(layer_grads)
        parts = [_layer_shards(layer_grads[k], dict(BIG)[k[0]] - 1) for k in keys]
        sums = [add_own_half(g, r, core_idx, name=f"add_own_half_{n}{layer}")
                for g, r, (n, layer) in zip(parts, exchange_halves(parts, name="exchange_halves_early"), keys)]
        handle, zeros = chip_exchange_start(sums, [(3,) + a.shape[1:] for a in sums], _scatter_plan, sums[0],
                                            name="scatter_start")
        pending.update(keys=keys, handle=handle)
        return zeros[0, 0]

    loss, dx, grads = local_step(x[0], loss_target[0], w, late_weights, early_reduce)
    early_sums, early_parts = chip_exchange_wait(pending["handle"], _scatter_plan, dx, name="scatter_wait")

    small_names = REPLICATED + tuple(n for n, _ in SMALL_SHARDED)
    small_sum = allreduce_small(_pack([grads[n] for n in small_names], F32))
    reduced = dict(zip(small_names, _unpack(small_sum.reshape(-1), [grads[n].shape for n in small_names])))
    for n, axis in SMALL_SHARDED:
        width = given[n].shape[axis]
        reduced[n] = lax.dynamic_slice_in_dim(reduced[n], chip * width, width, axis=axis)

    keys, parts = [], []
    for n, axis in BIG:
        for layer, g in enumerate(grads[n]):
            if (n, layer) not in pending["keys"]:
                keys.append((n, layer))
                parts.append(_layer_shards(g, axis - 1))
    core_sums = [add_own_half(g, r, core_idx, name=f"add_own_half_{n}{layer}")
                 for g, r, (n, layer) in zip(parts, exchange_halves(parts, name="exchange_halves_late"), keys)]
    chip_parts = scatter_to_chips(core_sums)
    keys, core_sums, chip_parts = pending["keys"] + keys, list(early_sums) + core_sums, list(early_parts) + list(chip_parts)
    mine = [add_chip_parts(a, p, chip_idx, name=f"add_chip_parts_{n}{layer}")
            for a, p, (n, layer) in zip(core_sums, chip_parts, keys)]
    layers = {}
    for (n, layer), own, other in zip(keys, mine, swap_halves(mine)):
        both = jnp.where(core == 0, jnp.stack([own, other]), jnp.stack([other, own]))
        layers.setdefault(n, {})[layer] = both.reshape(given[n].shape[1:])
    for n, _ in BIG:
        reduced[n] = jnp.stack([layers[n][layer] for layer in sorted(layers[n])])

    delta, new_m, new_v = {}, {}, {}
    for n, _ in BIG:
        delta[n], new_m[n], new_v[n] = adamw(given[n], reduced[n], given["m_" + n], given["v_" + n], name="adamw_" + n)
    shapes = [given[n].shape for n in small_names]
    packed = [_pack([src[n] for n in small_names], F32)
              for src in (given, reduced, {n: given["m_" + n] for n in small_names}, {n: given["v_" + n] for n in small_names})]
    for dst, res in zip((delta, new_m, new_v), adamw(*packed, name="adamw_small")):
        dst.update(zip(small_names, _unpack(res.reshape(-1), shapes)))

    total = lax.psum(loss[0, 0], ("x", "y", "c"))
    return (total, dx[None], *[reduced[n] for n in WEIGHTS], *[delta[n] for n in WEIGHTS],
            *[new_m[n] for n in WEIGHTS], *[new_v[n] for n in WEIGHTS])
```

```python
import functools
import math
from typing import NamedTuple

import numpy as np
import jax
import jax.numpy as jnp
from jax import lax
from jax.experimental import pallas as pl
from jax.experimental.pallas import tpu as pltpu

F32 = jnp.float32
BF16 = jnp.bfloat16
HIGHEST = lax.Precision.HIGHEST
MESH = pl.DeviceIdType.MESH

D_MODEL = 1024
EPS = 1e-6
RET_HEADS, RET_DK, RET_DV, CHUNK = 4, 128, 256, 128
ROPE_BASE = 10000.0
SSM_HEADS, SSM_P, SSM_N, SSM_GROUPS = 16, 64, 128, 2
SSM_DINNER = SSM_HEADS * SSM_P
EVEN_IN, EVEN_IN_PAD = 5648, 5760
S5_GROUPS, S5_GROUP, S5_STATE = 32, 16, 64
S5_LANES = S5_GROUPS * S5_STATE
SCAN_SEG = 8
D_FF = 2816
ADAM_LR, ADAM_B1, ADAM_B2, ADAM_EPS, ADAM_WD, ADAM_STEP = 0.001, 0.9, 0.999, 1e-08, 0.01, 10

LANE = 128
VMEM_LIMIT = 56 * 1024 * 1024


def _params(sem=None, **kw):
    return pltpu.CompilerParams(dimension_semantics=sem, vmem_limit_bytes=VMEM_LIMIT, **kw)


def _tile(n, target, unit=LANE):
    if n <= target:
        return n
    t = (target // unit) * unit
    while t >= unit:
        if n % t == 0:
            return t
        t -= unit
    return n


def _silu(x):
    return x * jax.nn.sigmoid(x)


def _mm(a, b):
    return jnp.dot(a.astype(BF16), b.astype(BF16), preferred_element_type=F32)


def _mm_nt(a, b):
    return lax.dot_general(a.astype(BF16), b.astype(BF16), (((1,), (1,)), ((), ())), preferred_element_type=F32)


def _mm_tn(a, b):
    return lax.dot_general(a.astype(BF16), b.astype(BF16), (((0,), (0,)), ((), ())), preferred_element_type=F32)


def _dot_hi(a, b):
    return jnp.dot(a, b, precision=HIGHEST, preferred_element_type=F32)


def _dot_hi_tn(a, b):
    return lax.dot_general(a, b, (((0,), (0,)), ((), ())), precision=HIGHEST, preferred_element_type=F32)


MATMUL_VMEM = 44 * 1024 * 1024


def matmul(a, b, *, ta=False, tb=False, res=None, out_dtype=F32, name):
    m, k = (a.shape[1], a.shape[0]) if ta else a.shape
    n = b.shape[0] if tb else b.shape[1]
    assert (b.shape[1] if tb else b.shape[0]) == k, (a.shape, b.shape, ta, tb)
    tm = _tile(m, 1536)
    tn = _tile(n, 640)
    if tn < 384:
        tn = _tile(n, 1536)
    res_bytes = 0 if res is None else res.dtype.itemsize

    def vmem(tm, tn):
        return 2 * (tm * k * a.dtype.itemsize + tn * k * b.dtype.itemsize + tm * tn * (jnp.dtype(out_dtype).itemsize + res_bytes))

    while vmem(tm, tn) > MATMUL_VMEM and tm % (2 * LANE) == 0:
        tm //= 2
    assert vmem(tm, tn) <= MATMUL_VMEM, (name, tm, tn, k)
    a_spec = pl.BlockSpec((k, tm), lambda i, j: (0, i)) if ta else pl.BlockSpec((tm, k), lambda i, j: (i, 0))
    b_spec = pl.BlockSpec((tn, k), lambda i, j: (j, 0)) if tb else pl.BlockSpec((k, tn), lambda i, j: (0, j))
    o_spec = pl.BlockSpec((tm, tn), lambda i, j: (i, j))
    dims = (((0 if ta else 1,), (1 if tb else 0,)), ((), ()))
    has_res = res is not None

    def body(a_ref, b_ref, *rest):
        o_ref = rest[-1]
        out = lax.dot_general(a_ref[...].astype(BF16), b_ref[...].astype(BF16), dims, preferred_element_type=F32)
        if has_res:
            out = out + rest[0][...].astype(F32)
        o_ref[...] = out.astype(o_ref.dtype)

    ins = [a, b] + ([res] if has_res else [])
    specs = [a_spec, b_spec] + ([o_spec] if has_res else [])
    return pl.pallas_call(
        body, name=name, grid=(m // tm, n // tn), in_specs=specs, out_specs=o_spec,
        out_shape=jax.ShapeDtypeStruct((m, n), out_dtype), compiler_params=_params(("parallel", "parallel")),
    )(*ins)


class Cols(NamedTuple):
    arr: jax.Array
    w: int
    j: int


def _cols(a):
    return a if isinstance(a, Cols) else Cols(a, a.shape[1], 0)


def _row_spec(c, tl):
    return pl.BlockSpec((tl, c.w), lambda i, j=c.j: (i, j))


def _whole_spec(p):
    return pl.BlockSpec(p.shape, lambda i, nd=p.ndim: (0,) * nd)


def rowwise_fwd(fn, rows, aux, pars, consts, outs, *, name, tl):
    rows = [_cols(r) for r in rows + aux]
    whole = list(pars) + list(consts)
    n_rows = len(rows)
    n_whole = len(whole)
    length = rows[0].arr.shape[0]
    tl = min(tl, length)

    def body(*refs):
        vals = [r[...].astype(F32) for r in refs[:n_rows]] + [r[...] for r in refs[n_rows:n_rows + n_whole]]
        res = fn(*vals)
        for o_ref, v in zip(refs[n_rows + n_whole:], res, strict=True):
            o_ref[...] = v.astype(o_ref.dtype)

    return pl.pallas_call(
        body, name=name, grid=(length // tl,),
        in_specs=[_row_spec(r, tl) for r in rows] + [_whole_spec(p) for p in whole],
        out_specs=[pl.BlockSpec((tl, w), lambda i: (i, 0)) for w, _ in outs],
        out_shape=[jax.ShapeDtypeStruct((length, w), dt) for w, dt in outs],
        compiler_params=_params(("parallel",)),
    )(*[r.arr for r in rows], *whole)


def rowwise_bwd(fn, rows, aux, pars, consts, cots, drow_dtypes, *, name, tl, add=None, merge=False):
    rows = [_cols(r) for r in rows]
    aux = [_cols(r) for r in aux]
    cots = [_cols(r) for r in cots]
    n_r, n_a, n_p, n_c, n_t = len(rows), len(aux), len(pars), len(consts), len(cots)
    length = rows[0].arr.shape[0]
    tl = min(tl, length)
    has_add = add is not None
    widths = [r.w for r in rows]

    def body(*refs):
        pos = 0
        r_vals = [r[...].astype(F32) for r in refs[pos:pos + n_r]]; pos += n_r
        a_vals = [r[...].astype(F32) for r in refs[pos:pos + n_a]]; pos += n_a
        p_vals = [r[...].astype(F32) for r in refs[pos:pos + n_p]]; pos += n_p
        c_vals = [r[...] for r in refs[pos:pos + n_c]]; pos += n_c
        t_vals = [r[...].astype(F32) for r in refs[pos:pos + n_t]]; pos += n_t
        add_val = None
        if has_add:
            add_val = refs[pos][...].astype(F32); pos += 1
        n_dr = 1 if merge else n_r
        dr_refs = refs[pos:pos + n_dr]; pos += n_dr
        dp_refs = refs[pos:pos + n_p]

        def f(*rp):
            return fn(*rp[:n_r], *a_vals, *rp[n_r:], *c_vals)

        _, vjp = jax.vjp(f, *r_vals, *p_vals)
        grads = vjp(tuple(t_vals))
        drows = list(grads[:n_r])
        if has_add:
            drows[0] = drows[0] + add_val
        if merge:
            off = 0
            for w, d in zip(widths, drows):
                dr_refs[0][:, off:off + w] = d.astype(dr_refs[0].dtype)
                off += w
        else:
            for r, d in zip(dr_refs, drows):
                r[...] = d.astype(r.dtype)
        i = pl.program_id(0)
        for r, d in zip(dp_refs, grads[n_r:]):
            @pl.when(i == 0)
            def _(r=r, d=d):
                r[...] = d

            @pl.when(i > 0)
            def _(r=r, d=d):
                r[...] += d

    if merge:
        dr_specs = [pl.BlockSpec((tl, sum(widths)), lambda i: (i, 0))]
        dr_shapes = [jax.ShapeDtypeStruct((length, sum(widths)), drow_dtypes[0])]
    else:
        dr_specs = [pl.BlockSpec((tl, w), lambda i: (i, 0)) for w in widths]
        dr_shapes = [jax.ShapeDtypeStruct((length, w), dt) for w, dt in zip(widths, drow_dtypes)]
    ins = [r.arr for r in rows + aux] + list(pars) + list(consts) + [r.arr for r in cots] + ([add] if has_add else [])
    specs = ([_row_spec(r, tl) for r in rows + aux] + [_whole_spec(p) for p in list(pars) + list(consts)]
             + [_row_spec(r, tl) for r in cots] + ([pl.BlockSpec((tl, add.shape[1]), lambda i: (i, 0))] if has_add else []))
    return pl.pallas_call(
        body, name=name, grid=(length // tl,), in_specs=specs,
        out_specs=dr_specs + [_whole_spec(p) for p in pars],
        out_shape=dr_shapes + [jax.ShapeDtypeStruct(p.shape, F32) for p in pars],
        compiler_params=_params(("arbitrary",)),
    )(*ins)


def whole_fwd(fn, ins, out_shapes, *, name):
    n_in = len(ins)

    def body(*refs):
        res = fn(*[r[...] for r in refs[:n_in]])
        for o_ref, v in zip(refs[n_in:], res, strict=True):
            o_ref[...] = v

    return pl.pallas_call(body, name=name, out_shape=[jax.ShapeDtypeStruct(s, F32) for s in out_shapes],
                          compiler_params=_params())(*ins)


def whole_bwd(fn, ins, n_diff, cots, *, name):
    n_in, n_t = len(ins), len(cots)

    def body(*refs):
        vals = [r[...] for r in refs[:n_in]]
        t_vals = [r[...] for r in refs[n_in:n_in + n_t]]
        _, vjp = jax.vjp(lambda *d: fn(*d, *vals[n_diff:]), *vals[:n_diff])
        for o_ref, g in zip(refs[n_in + n_t:], vjp(tuple(t_vals)), strict=True):
            o_ref[...] = g

    return pl.pallas_call(body, name=name, out_shape=[jax.ShapeDtypeStruct(a.shape, F32) for a in ins[:n_diff]],
                          compiler_params=_params())(*ins, *cots)


CONV_ROWS = 256


def _conv_geometry(x, w, cw, off):
    width = w.shape[1]
    x = Cols(x, width, 0)
    length = x.arr.shape[0]
    taps = w.shape[0]
    pad = -(-(taps - 1) // 8) * 8
    assert off % cw == 0 and width % cw == 0, (off, width, cw)
    return x, length, taps, pad, off // cw


def _conv_taps(xp_ref, w_ref, base, taps, pad, init, lanes=slice(None)):
    acc = init
    for k in range(taps):
        acc = acc + w_ref[k:k + 1, lanes] * xp_ref[pl.ds(base + pad - (taps - 1) + k, init.shape[0]), :]
    return acc


def conv_fwd(x, w, b, *, act, name, off=0, cw=LANE, out_dtype=F32):
    x, length, taps, pad, jb = _conv_geometry(x, w, cw, off)
    rc = min(CONV_ROWS, length)

    def body(x_ref, w_ref, b_ref, o_ref, xp_ref):
        xp_ref[0:pad, :] = jnp.zeros((pad, cw), F32)
        xp_ref[pad:pad + length, :] = x_ref[...].astype(F32)

        def chunk(r, carry):
            base = pl.multiple_of(r * rc, rc)
            acc = _conv_taps(xp_ref, w_ref, base, taps, pad, jnp.broadcast_to(b_ref[...], (rc, cw)))
            if act:
                acc = _silu(acc)
            o_ref[pl.ds(base, rc), :] = acc.astype(o_ref.dtype)
            return carry

        lax.fori_loop(0, length // rc, chunk, 0)

    return pl.pallas_call(
        body, name=name, grid=(x.w // cw,),
        in_specs=[pl.BlockSpec((length, cw), lambda j: (0, jb + j)), pl.BlockSpec((taps, cw), lambda j: (0, j)),
                  pl.BlockSpec((1, cw), lambda j: (0, j))],
        out_specs=pl.BlockSpec((length, cw), lambda j: (0, j)),
        out_shape=jax.ShapeDtypeStruct((length, x.w), out_dtype),
        scratch_shapes=[pltpu.VMEM((pad + length, cw), F32)],
        compiler_params=_params(("parallel",)),
    )(x.arr, w, b)


def conv_bwd(x, w, b, dy, *, act, name, off=0, cw=LANE, dx_dtype=F32):
    x, length, taps, pad, jb = _conv_geometry(x, w, cw, off)
    rc = min(CONV_ROWS, length)

    def body(x_ref, w_ref, b_ref, dy_ref, dx_ref, dw_ref, db_ref, xp_ref, gp_ref):
        xp_ref[0:pad, :] = jnp.zeros((pad, cw), F32)
        xp_ref[pad:pad + length, :] = x_ref[...].astype(F32)
        gp_ref[length:length + pad, :] = jnp.zeros((pad, cw), F32)
        if act:
            def pre_chunk(r, carry):
                base = pl.multiple_of(r * rc, rc)
                pre = _conv_taps(xp_ref, w_ref, base, taps, pad, jnp.broadcast_to(b_ref[...], (rc, cw)))
                sig = jax.nn.sigmoid(pre)
                gp_ref[pl.ds(base, rc), :] = dy_ref[pl.ds(base, rc), :].astype(F32) * (sig * (1.0 + pre * (1.0 - sig)))
                return carry

            lax.fori_loop(0, length // rc, pre_chunk, 0)
        else:
            gp_ref[0:length, :] = dy_ref[...].astype(F32)
        dw_ref[...] = jnp.zeros((taps, cw), F32)
        db_ref[...] = jnp.zeros((1, cw), F32)

        def chunk(r, carry):
            base = pl.multiple_of(r * rc, rc)
            acc = jnp.zeros((rc, cw), F32)
            g = gp_ref[pl.ds(base, rc), :]
            for k in range(taps):
                acc = acc + w_ref[k:k + 1, :] * gp_ref[pl.ds(base + (taps - 1) - k, rc), :]
                xs = xp_ref[pl.ds(base + pad - (taps - 1) + k, rc), :]
                dw_ref[k:k + 1, :] += jnp.sum(g * xs, axis=0, keepdims=True)
            db_ref[...] += jnp.sum(g, axis=0, keepdims=True)
            dx_ref[pl.ds(base, rc), :] = acc.astype(dx_ref.dtype)
            return carry

        lax.fori_loop(0, length // rc, chunk, 0)

    dy = _cols(dy)
    assert dy.j == 0 and dy.w == x.w
    return pl.pallas_call(
        body, name=name, grid=(x.w // cw,),
        in_specs=[pl.BlockSpec((length, cw), lambda j: (0, jb + j)), pl.BlockSpec((taps, cw), lambda j: (0, j)),
                  pl.BlockSpec((1, cw), lambda j: (0, j)), pl.BlockSpec((length, cw), lambda j: (0, j))],
        out_specs=[pl.BlockSpec((length, cw), lambda j: (0, j)), pl.BlockSpec((taps, cw), lambda j: (0, j)),
                   pl.BlockSpec((1, cw), lambda j: (0, j))],
        out_shape=[jax.ShapeDtypeStruct((length, x.w), dx_dtype), jax.ShapeDtypeStruct((taps, x.w), F32),
                   jax.ShapeDtypeStruct((1, x.w), F32)],
        scratch_shapes=[pltpu.VMEM((pad + length, cw), F32), pltpu.VMEM((length + pad, cw), F32)],
        compiler_params=_params(("parallel",)),
    )(x.arr, w, b, dy.arr)


def _conv_transpose(xp_ref, gp_ref, w_ref, dx_ref, dw_ref, db_ref, lanes, length, taps, pad, rc):
    dw_ref[:, lanes] = jnp.zeros((taps, LANE), F32)
    db_ref[:, lanes] = jnp.zeros((1, LANE), F32)

    def chunk(r, carry):
        base = pl.multiple_of(r * rc, rc)
        acc = jnp.zeros((rc, LANE), F32)
        g = gp_ref[pl.ds(base, rc), :]
        for k in range(taps):
            acc = acc + w_ref[k:k + 1, lanes] * gp_ref[pl.ds(base + (taps - 1) - k, rc), :]
            xs = xp_ref[pl.ds(base + pad - (taps - 1) + k, rc), :]
            dw_ref[k:k + 1, lanes] += jnp.sum(g * xs, axis=0, keepdims=True)
        db_ref[:, lanes] += jnp.sum(g, axis=0, keepdims=True)
        dx_ref[pl.ds(base, rc), lanes] = acc.astype(dx_ref.dtype)
        return carry

    lax.fori_loop(0, length // rc, chunk, 0)


def ffn_interleave(a):
    lead = a.shape[:-1]
    return a.reshape(lead + (2, -1, LANE)).swapaxes(-3, -2).reshape(a.shape)


def ffn_deinterleave(a):
    lead = a.shape[:-1]
    return a.reshape(lead + (-1, 2, LANE)).swapaxes(-3, -2).reshape(a.shape)


GATE, UP = slice(0, LANE), slice(LANE, 2 * LANE)


def _ffn_geometry(a, w):
    length, width = a.shape
    taps = w.shape[0]
    return length, width, width // (2 * LANE), taps, -(-(taps - 1) // 8) * 8, min(CONV_ROWS, length)


def _ffn_pre(xg_ref, xu_ref, w_ref, b_ref, base, taps, pad, rc):
    gate = _conv_taps(xg_ref, w_ref, base, taps, pad, jnp.broadcast_to(b_ref[:, GATE], (rc, LANE)), GATE)
    up = _conv_taps(xu_ref, w_ref, base, taps, pad, jnp.broadcast_to(b_ref[:, UP], (rc, LANE)), UP)
    return gate, up


def ffn_conv_act(a, w, b, *, name):
    length, width, nb, taps, pad, rc = _ffn_geometry(a, w)

    def body(a_ref, w_ref, b_ref, o_ref, xg_ref, xu_ref):
        for xp_ref, lanes in ((xg_ref, GATE), (xu_ref, UP)):
            xp_ref[0:pad, :] = jnp.zeros((pad, LANE), F32)
            xp_ref[pad:pad + length, :] = a_ref[:, lanes]

        def chunk(r, carry):
            base = pl.multiple_of(r * rc, rc)
            gate, up = _ffn_pre(xg_ref, xu_ref, w_ref, b_ref, base, taps, pad, rc)
            o_ref[pl.ds(base, rc), :] = (_silu(gate) * up).astype(o_ref.dtype)
            return carry

        lax.fori_loop(0, length // rc, chunk, 0)

    pair = lambda rows: pl.BlockSpec((rows, 2 * LANE), lambda j: (0, j))
    return pl.pallas_call(
        body, name=name, grid=(nb,), in_specs=[pair(length), pair(taps), pair(1)],
        out_specs=pl.BlockSpec((length, LANE), lambda j: (0, j)),
        out_shape=jax.ShapeDtypeStruct((length, width // 2), BF16),
        scratch_shapes=[pltpu.VMEM((pad + length, LANE), F32), pltpu.VMEM((pad + length, LANE), F32)],
        compiler_params=_params(("parallel",)),
    )(a, w, b)


def ffn_conv_act_bwd(a, w, b, dact, *, name):
    length, width, nb, taps, pad, rc = _ffn_geometry(a, w)

    def body(a_ref, w_ref, b_ref, dy_ref, da_ref, dw_ref, db_ref, xg_ref, xu_ref, gg_ref, gu_ref):
        for xp_ref, lanes in ((xg_ref, GATE), (xu_ref, UP)):
            xp_ref[0:pad, :] = jnp.zeros((pad, LANE), F32)
            xp_ref[pad:pad + length, :] = a_ref[:, lanes]
        for gp_ref in (gg_ref, gu_ref):
            gp_ref[length:length + pad, :] = jnp.zeros((pad, LANE), F32)

        def pre_chunk(r, carry):
            base = pl.multiple_of(r * rc, rc)
            gate, up = _ffn_pre(xg_ref, xu_ref, w_ref, b_ref, base, taps, pad, rc)
            sig = jax.nn.sigmoid(gate)
            dy = dy_ref[pl.ds(base, rc), :]
            gg_ref[pl.ds(base, rc), :] = dy * up * (sig * (1.0 + gate * (1.0 - sig)))
            gu_ref[pl.ds(base, rc), :] = dy * (gate * sig)
            return carry

        lax.fori_loop(0, length // rc, pre_chunk, 0)
        _conv_transpose(xg_ref, gg_ref, w_ref, da_ref, dw_ref, db_ref, GATE, length, taps, pad, rc)
        _conv_transpose(xu_ref, gu_ref, w_ref, da_ref, dw_ref, db_ref, UP, length, taps, pad, rc)

    pair = lambda rows: pl.BlockSpec((rows, 2 * LANE), lambda j: (0, j))
    return pl.pallas_call(
        body, name=name, grid=(nb,),
        in_specs=[pair(length), pair(taps), pair(1), pl.BlockSpec((length, LANE), lambda j: (0, j))],
        out_specs=[pair(length), pair(taps), pair(1)],
        out_shape=[jax.ShapeDtypeStruct((length, width), BF16), jax.ShapeDtypeStruct((taps, width), F32),
                   jax.ShapeDtypeStruct((1, width), F32)],
        scratch_shapes=[pltpu.VMEM((pad + length, LANE), F32), pltpu.VMEM((pad + length, LANE), F32),
                        pltpu.VMEM((length + pad, LANE), F32), pltpu.VMEM((length + pad, LANE), F32)],
        compiler_params=_params(("parallel",)),
    )(a, w, b, dact)


def _retention_consts():
    h = np.arange(RET_HEADS, dtype=np.float32)
    log_g = np.log1p(-(2.0 ** (-5.0 - h))).astype(np.float32)
    idx = np.arange(CHUNK, dtype=np.float32)
    diff = idx[:, None] - idx[None, :]
    intra = np.where(diff[None] >= 0, np.exp(np.maximum(diff, 0.0)[None] * log_g[:, None, None]), 0.0)
    zeta = np.exp((CHUNK - 1 - idx)[None, :] * log_g[:, None])
    xi = np.exp((idx + 1)[None, :] * log_g[:, None])
    decay = np.exp(CHUNK * log_g)
    zeta = np.broadcast_to(zeta[:, :, None], (RET_HEADS, CHUNK, RET_DK))
    xi = np.broadcast_to(xi[:, :, None], (RET_HEADS, CHUNK, RET_DV))
    return (jnp.asarray(intra, F32), jnp.asarray(zeta, F32), jnp.asarray(xi, F32), [float(d) for d in decay])


def _rotary_tables(length):
    inv = ROPE_BASE ** (-jnp.arange(0, RET_DK, 2, dtype=F32) / RET_DK)
    ang = jnp.arange(length).astype(F32)[:, None] * inv[None, :]
    cos, sin = jnp.cos(ang), jnp.sin(ang)
    return jnp.concatenate([cos, cos], axis=1), jnp.concatenate([-sin, sin], axis=1)


def _rot(x, cos2, sin2):
    return x * cos2 + pltpu.roll(x, RET_DK // 2, 1) * sin2


def _rot_t(y, cos2, sin2):
    return y * cos2 + pltpu.roll(y * sin2, RET_DK // 2, 1)


def _head_decay(h, decays):
    d = jnp.float32(decays[-1])
    for i in range(len(decays) - 2, -1, -1):
        d = jnp.where(h == i, jnp.float32(decays[i]), d)
    return d


def _ret_chunk(q, k, v, g, state, intra, zeta, xi, decay):
    s = _mm_nt(q, k) * intra
    kv = _mm_tn(k * zeta, v)
    o = _mm(s, v) + _mm(q, state) * xi
    oc = o - jnp.mean(o, axis=-1, keepdims=True)
    r = oc * lax.rsqrt(jnp.mean(oc * oc, axis=-1, keepdims=True) + EPS)
    return _silu(g) * r, state * decay + kv


def _ret_specs(rev, nc):
    def cidx(c):
        return nc - 1 - c if rev else c
    return [
        pl.BlockSpec((CHUNK, RET_DK), lambda h, c: (cidx(c), h)),
        pl.BlockSpec((CHUNK, RET_DK), lambda h, c: (cidx(c), RET_HEADS + h)),
        pl.BlockSpec((CHUNK, RET_DV), lambda h, c: (cidx(c), 4 + h)),
        pl.BlockSpec((CHUNK, RET_DV), lambda h, c: (cidx(c), 8 + h)),
        pl.BlockSpec((CHUNK, RET_DK), lambda h, c: (cidx(c), 0)),
        pl.BlockSpec((CHUNK, RET_DK), lambda h, c: (cidx(c), 0)),
        pl.BlockSpec((1, CHUNK, CHUNK), lambda h, c: (h, 0, 0)),
        pl.BlockSpec((1, CHUNK, RET_DK), lambda h, c: (h, 0, 0)),
        pl.BlockSpec((1, CHUNK, RET_DV), lambda h, c: (h, 0, 0)),
    ], cidx


def retention_fwd(proj, cos2, sin2):
    length = proj.shape[0]
    nc = length // CHUNK
    intra, zeta, xi, decays = _retention_consts()
    specs, _ = _ret_specs(False, nc)
    scale = RET_DK ** -0.5

    def body(q_ref, k_ref, v_ref, g_ref, cos_ref, sin_ref, intra_ref, zeta_ref, xi_ref, y_ref, st_ref, state):
        h, c = pl.program_id(0), pl.program_id(1)

        @pl.when(c == 0)
        def _():
            state[...] = jnp.zeros_like(state)

        q = _rot(q_ref[...], cos_ref[...], sin_ref[...])
        k = _rot(k_ref[...], cos_ref[...], sin_ref[...]) * scale
        st_ref[0, 0] = state[...]
        y, new_state = _ret_chunk(q, k, v_ref[...], g_ref[...], state[...], intra_ref[0], zeta_ref[0], xi_ref[0],
                                  _head_decay(h, decays))
        y_ref[...] = y.astype(y_ref.dtype)
        state[...] = new_state

    return pl.pallas_call(
        body, name="retention_fwd", grid=(RET_HEADS, nc), in_specs=specs,
        out_specs=[pl.BlockSpec((CHUNK, RET_DV), lambda h, c: (c, h)),
                   pl.BlockSpec((1, 1, RET_DK, RET_DV), lambda h, c: (h, c, 0, 0))],
        out_shape=[jax.ShapeDtypeStruct((length, RET_HEADS * RET_DV), BF16),
                   jax.ShapeDtypeStruct((RET_HEADS, nc, RET_DK, RET_DV), F32)],
        scratch_shapes=[pltpu.VMEM((RET_DK, RET_DV), F32)],
        compiler_params=_params(("parallel", "arbitrary")),
    )(proj, proj, proj, proj, cos2, sin2, intra, zeta, xi)


def retention_bwd(proj, cos2, sin2, states, dmix):
    length = proj.shape[0]
    nc = length // CHUNK
    intra, zeta, xi, decays = _retention_consts()
    specs, cidx = _ret_specs(True, nc)
    scale = RET_DK ** -0.5

    def body(q_ref, k_ref, v_ref, g_ref, cos_ref, sin_ref, intra_ref, zeta_ref, xi_ref, st_ref, dy_ref,
             dq_ref, dk_ref, dv_ref, dg_ref, dstate):
        h, c = pl.program_id(0), pl.program_id(1)

        @pl.when(c == 0)
        def _():
            dstate[...] = jnp.zeros_like(dstate)

        cos2v, sin2v = cos_ref[...], sin_ref[...]
        q = _rot(q_ref[...], cos2v, sin2v)
        k = _rot(k_ref[...], cos2v, sin2v) * scale
        decay = _head_decay(h, decays)
        intra_v, zeta_v, xi_v = intra_ref[0], zeta_ref[0], xi_ref[0]
        _, vjp = jax.vjp(lambda q, k, v, g, s: _ret_chunk(q, k, v, g, s, intra_v, zeta_v, xi_v, decay),
                         q, k, v_ref[...], g_ref[...], st_ref[0, 0])
        dq, dk, dv, dg, ds = vjp((dy_ref[...].astype(F32), dstate[...]))
        dq_ref[...] = _rot_t(dq, cos2v, sin2v).astype(dq_ref.dtype)
        dk_ref[...] = _rot_t(dk * scale, cos2v, sin2v).astype(dk_ref.dtype)
        dv_ref[...] = dv.astype(dv_ref.dtype)
        dg_ref[...] = dg.astype(dg_ref.dtype)
        dstate[...] = ds

    specs = specs + [pl.BlockSpec((1, 1, RET_DK, RET_DV), lambda h, c: (h, cidx(c), 0, 0)),
                     pl.BlockSpec((CHUNK, RET_DV), lambda h, c: (cidx(c), h))]
    return pl.pallas_call(
        body, name="retention_bwd", grid=(RET_HEADS, nc), in_specs=specs,
        out_specs=[pl.BlockSpec((CHUNK, RET_DK), lambda h, c: (cidx(c), h)),
                   pl.BlockSpec((CHUNK, RET_DK), lambda h, c: (cidx(c), h)),
                   pl.BlockSpec((CHUNK, RET_DV), lambda h, c: (cidx(c), h)),
                   pl.BlockSpec((CHUNK, RET_DV), lambda h, c: (cidx(c), h))],
        out_shape=[jax.ShapeDtypeStruct((length, RET_HEADS * RET_DK), BF16),
                   jax.ShapeDtypeStruct((length, RET_HEADS * RET_DK), BF16),
                   jax.ShapeDtypeStruct((length, RET_HEADS * RET_DV), BF16),
                   jax.ShapeDtypeStruct((length, RET_HEADS * RET_DV), BF16)],
        scratch_shapes=[pltpu.VMEM((RET_DK, RET_DV), F32)],
        compiler_params=_params(("parallel", "arbitrary")),
    )(proj, proj, proj, proj, cos2, sin2, intra, zeta, xi, states, dmix)


def _ssd_consts():
    tri = np.tril(np.ones((CHUNK, CHUNK), np.float32))
    expand = np.zeros((LANE, SSM_DINNER), np.float32)
    for h in range(SSM_HEADS):
        expand[h, h * SSM_P:(h + 1) * SSM_P] = 1.0
    return jnp.asarray(tri), jnp.asarray(tri.T.copy()), jnp.asarray(expand)


def _ssd_chunk(xs, bm, cm, dtr, z, state, dt_bias, a_log, d_skip, norm_w, tri, tri_t, expand):
    gw = SSM_DINNER // SSM_GROUPS
    dt = jax.nn.softplus(dtr + dt_bias)
    da = dt * (-jnp.exp(a_log))
    acs = _dot_hi(tri, da)
    acs_t = _dot_hi_tn(da, tri_t)
    dt_x = _dot_hi(dt, expand)
    da_x = _dot_hi(da, expand)
    acs_x = _dot_hi(tri, da_x)
    tot_x = jnp.sum(da_x, axis=0, keepdims=True)
    x_dt = xs * dt_x
    x_dec = x_dt * jnp.exp(tot_x - acs_x)
    e_acs = jnp.exp(acs_x)
    e_tot = jnp.exp(tot_x)
    lane = lax.broadcasted_iota(jnp.int32, (CHUNK, LANE), 1)
    sub = lax.broadcasted_iota(jnp.int32, (CHUNK, LANE), 0)
    causal = sub >= lane
    ys, new_states = [], []
    for g in range(SSM_GROUPS):
        bg = bm[:, g * SSM_N:(g + 1) * SSM_N]
        cg = cm[:, g * SSM_N:(g + 1) * SSM_N]
        sg = state[:, g * gw:(g + 1) * gw]
        cb = _mm_nt(cg, bg)
        y_off = _mm(cg, sg) * e_acs[:, g * gw:(g + 1) * gw]
        new_states.append(sg * e_tot[:, g * gw:(g + 1) * gw] + _mm_tn(bg, x_dec[:, g * gw:(g + 1) * gw]))
        pairs = []
        for p in range(gw // LANE):
            hp = g * (gw // LANE) + p
            xp = x_dt[:, hp * LANE:(hp + 1) * LANE]
            halves = []
            for head in (2 * hp, 2 * hp + 1):
                col = jnp.sum(jnp.where(lane == head, acs, 0.0), axis=1, keepdims=True)
                row = jnp.sum(jnp.where(sub == head, acs_t, 0.0), axis=0, keepdims=True)
                decay = jnp.exp(jnp.where(causal, col - row, -1e30))
                halves.append(_mm(cb * decay, xp))
            pairs.append(jnp.where(lane < SSM_P, halves[0], halves[1]))
        ys.append(jnp.concatenate(pairs, axis=1) + y_off)
    d_x = jnp.mean(_dot_hi(jnp.broadcast_to(d_skip, (8, LANE)), expand), axis=0, keepdims=True)
    y = (jnp.concatenate(ys, axis=1) + d_x * xs) * _silu(z)
    normed = []
    for g in range(SSM_GROUPS):
        yg = y[:, g * gw:(g + 1) * gw]
        normed.append(yg * lax.rsqrt(jnp.mean(yg * yg, axis=-1, keepdims=True) + EPS))
    return jnp.concatenate(normed, axis=1) * norm_w, jnp.concatenate(new_states, axis=1)


XBC = SSM_DINNER + 2 * SSM_GROUPS * SSM_N


def _ssd_specs(rev, nc):
    def cidx(c):
        return nc - 1 - c if rev else c
    row = lambda w, j: pl.BlockSpec((CHUNK, w), lambda c: (cidx(c), j))
    whole = lambda shape: pl.BlockSpec(shape, lambda c: (0,) * len(shape))
    return [row(XBC, 0), row(LANE, 5632 // LANE), row(SSM_DINNER, 3),
            whole((1, LANE)), whole((1, LANE)), whole((1, LANE)), whole((1, SSM_DINNER)),
            whole((CHUNK, CHUNK)), whole((CHUNK, CHUNK)), whole((LANE, SSM_DINNER))], cidx


def ssd_fwd(xbc, proj, dt_bias, a_log, d_skip, norm_w):
    length = proj.shape[0]
    nc = length // CHUNK
    tri, tri_t, expand = _ssd_consts()
    specs, _ = _ssd_specs(False, nc)

    def body(xbc_ref, dt_ref, z_ref, dtb_ref, alog_ref, d_ref, nw_ref, tri_ref, trit_ref, e_ref, y_ref, st_ref, state):
        @pl.when(pl.program_id(0) == 0)
        def _():
            state[...] = jnp.zeros_like(state)

        st_ref[0] = state[...]
        y, new_state = _ssd_chunk(
            xbc_ref[:, 0:SSM_DINNER], xbc_ref[:, SSM_DINNER:SSM_DINNER + 256], xbc_ref[:, SSM_DINNER + 256:XBC],
            dt_ref[...], z_ref[...], state[...], dtb_ref[...], alog_ref[...], d_ref[...], nw_ref[...],
            tri_ref[...], trit_ref[...], e_ref[...])
        y_ref[...] = y.astype(y_ref.dtype)
        state[...] = new_state

    return pl.pallas_call(
        body, name="ssd_fwd", grid=(nc,), in_specs=specs,
        out_specs=[pl.BlockSpec((CHUNK, SSM_DINNER), lambda c: (c, 0)),
                   pl.BlockSpec((1, SSM_N, SSM_DINNER), lambda c: (c, 0, 0))],
        out_shape=[jax.ShapeDtypeStruct((length, SSM_DINNER), BF16),
                   jax.ShapeDtypeStruct((nc, SSM_N, SSM_DINNER), F32)],
        scratch_shapes=[pltpu.VMEM((SSM_N, SSM_DINNER), F32)],
        compiler_params=_params(("arbitrary",)),
    )(xbc, proj, proj, dt_bias, a_log, d_skip, norm_w, tri, tri_t, expand)


def ssd_bwd(xbc, proj, dt_bias, a_log, d_skip, norm_w, states, dmix):
    length = proj.shape[0]
    nc = length // CHUNK
    tri, tri_t, expand = _ssd_consts()
    specs, cidx = _ssd_specs(True, nc)

    def body(xbc_ref, dt_ref, z_ref, dtb_ref, alog_ref, d_ref, nw_ref, tri_ref, trit_ref, e_ref, st_ref, dy_ref,
             dxbc_ref, ddt_ref, dz_ref, ddtb_ref, dalog_ref, dd_ref, dnw_ref, dstate):
        c = pl.program_id(0)

        @pl.when(c == 0)
        def _():
            dstate[...] = jnp.zeros_like(dstate)

        tri_v, trit_v, e_v = tri_ref[...], trit_ref[...], e_ref[...]
        _, vjp = jax.vjp(
            lambda *a: _ssd_chunk(*a, tri_v, trit_v, e_v),
            xbc_ref[:, 0:SSM_DINNER], xbc_ref[:, SSM_DINNER:SSM_DINNER + 256], xbc_ref[:, SSM_DINNER + 256:XBC],
            dt_ref[...], z_ref[...], st_ref[0], dtb_ref[...], alog_ref[...], d_ref[...], nw_ref[...])
        dxs, dbm, dcm, ddt, dz, ds, ddtb, dalog, dd, dnw = vjp((dy_ref[...].astype(F32), dstate[...]))
        dxbc_ref[:, 0:SSM_DINNER] = dxs
        dxbc_ref[:, SSM_DINNER:SSM_DINNER + 256] = dbm
        dxbc_ref[:, SSM_DINNER + 256:XBC] = dcm
        ddt_ref[...] = ddt.astype(ddt_ref.dtype)
        dz_ref[...] = dz.astype(dz_ref.dtype)
        dstate[...] = ds
        for r, d in ((ddtb_ref, ddtb), (dalog_ref, dalog), (dd_ref, dd), (dnw_ref, dnw)):
            @pl.when(c == 0)
            def _(r=r, d=d):
                r[...] = d

            @pl.when(c > 0)
            def _(r=r, d=d):
                r[...] += d

    whole = lambda shape: pl.BlockSpec(shape, lambda c: (0,) * len(shape))
    specs = specs + [pl.BlockSpec((1, SSM_N, SSM_DINNER), lambda c: (cidx(c), 0, 0)),
                     pl.BlockSpec((CHUNK, SSM_DINNER), lambda c: (cidx(c), 1))]
    return pl.pallas_call(
        body, name="ssd_bwd", grid=(nc,), in_specs=specs,
        out_specs=[pl.BlockSpec((CHUNK, XBC), lambda c: (cidx(c), 0)), pl.BlockSpec((CHUNK, LANE), lambda c: (cidx(c), 0)),
                   pl.BlockSpec((CHUNK, SSM_DINNER), lambda c: (cidx(c), 0)),
                   whole((1, LANE)), whole((1, LANE)), whole((1, LANE)), whole((1, SSM_DINNER))],
        out_shape=[jax.ShapeDtypeStruct((length, XBC), F32), jax.ShapeDtypeStruct((length, LANE), BF16),
                   jax.ShapeDtypeStruct((length, SSM_DINNER), BF16),
                   jax.ShapeDtypeStruct((1, LANE), F32), jax.ShapeDtypeStruct((1, LANE), F32),
                   jax.ShapeDtypeStruct((1, LANE), F32), jax.ShapeDtypeStruct((1, SSM_DINNER), F32)],
        scratch_shapes=[pltpu.VMEM((SSM_N, SSM_DINNER), F32)],
        compiler_params=_params(("arbitrary",)),
    )(xbc, proj, proj, dt_bias, a_log, d_skip, norm_w, tri, tri_t, expand, states, dmix)


def _cmul(ar, ai, br, bi):
    return ar * br - ai * bi, ar * bi + ai * br


def s5_scan(b_re, b_im, a_re, a_im, *, reverse=False, states=None, name, lw=256):
    length, lanes = b_re.shape
    nk = length // SCAN_SEG
    with_da = states is not None
    assert reverse or not with_da

    def shift(v):
        sub = lax.broadcasted_iota(jnp.int32, v.shape, 0)
        if reverse:
            return jnp.where(sub == SCAN_SEG - 1, 0.0, pltpu.roll(v, SCAN_SEG - 1, 0))
        return jnp.where(sub == 0, 0.0, pltpu.roll(v, 1, 0))

    def body(*refs):
        if with_da:
            bre_ref, bim_ref, are_ref, aim_ref, sre_ref, sim_ref, xre_ref, xim_ref, dare_ref, daim_ref = refs
        else:
            bre_ref, bim_ref, are_ref, aim_ref, xre_ref, xim_ref = refs
        ar = jnp.broadcast_to(are_ref[...], (SCAN_SEG, lw))
        ai = jnp.broadcast_to(aim_ref[...], (SCAN_SEG, lw))

        def tile(i):
            k = (nk - 1 - i) if reverse else i
            return pl.ds(pl.multiple_of(k * SCAN_SEG, SCAN_SEG), SCAN_SEG)

        def local(i, carry):
            xr, xi, pr, pi = carry
            rows = tile(i)
            mr, mi = _cmul(ar, ai, xr, xi)
            xr, xi = mr + bre_ref[rows, :], mi + bim_ref[rows, :]
            xre_ref[rows, :] = xr
            xim_ref[rows, :] = xi
            pr, pi = _cmul(ar, ai, pr, pi)
            return xr, xi, pr, pi

        zero = jnp.zeros((SCAN_SEG, lw), F32)
        one = jnp.ones((SCAN_SEG, lw), F32)
        er, ei, pr, pi = lax.fori_loop(0, nk, local, (zero, zero, one, zero))
        cr, ci = zero, zero
        for _ in range(SCAN_SEG - 1):
            mr, mi = _cmul(pr, pi, cr, ci)
            cr, ci = shift(er + mr), shift(ei + mi)

        def fix(i, carry):
            pr, pi, dr, di = carry
            rows = tile(i)
            pr, pi = _cmul(ar, ai, pr, pi)
            mr, mi = _cmul(pr, pi, cr, ci)
            xr, xi = xre_ref[rows, :] + mr, xim_ref[rows, :] + mi
            xre_ref[rows, :] = xr
            xim_ref[rows, :] = xi
            if with_da:
                k = nk - 1 - i
                prev = pl.ds(pl.multiple_of(jnp.maximum(k - 1, 0) * SCAN_SEG, SCAN_SEG), SCAN_SEG)
                last = pl.ds((nk - 1) * SCAN_SEG, SCAN_SEG)
                sub = lax.broadcasted_iota(jnp.int32, (SCAN_SEG, lw), 0)
                wr = jnp.where(sub == 0, 0.0, pltpu.roll(sre_ref[last, :], 1, 0))
                wi = jnp.where(sub == 0, 0.0, pltpu.roll(sim_ref[last, :], 1, 0))
                sr = jnp.where(k == 0, wr, sre_ref[prev, :])
                si = jnp.where(k == 0, wi, sim_ref[prev, :])
                dr, di = dr + xr * sr + xi * si, di + xi * sr - xr * si
            return pr, pi, dr, di

        _, _, dr, di = lax.fori_loop(0, nk, fix, (one, zero, zero, zero))
        if with_da:
            dare_ref[...] = jnp.sum(dr, axis=0, keepdims=True)
            daim_ref[...] = jnp.sum(di, axis=0, keepdims=True)

    col = pl.BlockSpec((length, lw), lambda j: (0, j))
    vec = pl.BlockSpec((1, lw), lambda j: (0, j))
    ins = [b_re, b_im, a_re, a_im] + (list(states) if with_da else [])
    in_specs = [col, col, vec, vec] + ([col, col] if with_da else [])
    out_specs = [col, col] + ([vec, vec] if with_da else [])
    out_shape = [jax.ShapeDtypeStruct((length, lanes), F32)] * 2 + ([jax.ShapeDtypeStruct((1, lanes), F32)] * 2 if with_da else [])
    return pl.pallas_call(
        body, name=name, grid=(lanes // lw,), in_specs=in_specs, out_specs=out_specs, out_shape=out_shape,
        compiler_params=_params(("parallel",)),
    )(*ins)


def _seg_interleave(v):
    length = v.shape[0]
    return v.reshape(SCAN_SEG, length // SCAN_SEG, -1).transpose(1, 0, 2).reshape(length, -1)


def _seg_deinterleave(v):
    length = v.shape[0]
    return v.reshape(length // SCAN_SEG, SCAN_SEG, -1).transpose(1, 0, 2).reshape(length, -1)


def _block_diag(m):
    eye = jnp.eye(S5_GROUPS, dtype=m.dtype)
    return (m.reshape(S5_GROUPS, S5_GROUP, 1, S5_STATE) * eye[:, None, :, None]).reshape(S5_GROUPS * S5_GROUP, S5_LANES)


def _block_diag_take(full):
    idx = jnp.arange(S5_GROUPS)
    blocks = full.reshape(S5_GROUPS, S5_GROUP, S5_GROUPS, S5_STATE)[idx, :, idx, :]
    return blocks.reshape(S5_GROUPS * S5_GROUP, S5_STATE)


def _s5_prep(a_re, a_im, log_step, b_re, b_im, rep):
    step = jnp.exp(log_step)
    mag = jnp.exp(a_re * step)
    ab_re = mag * jnp.cos(a_im * step)
    ab_im = mag * jnp.sin(a_im * step)
    den = a_re * a_re + a_im * a_im
    f_re = ((ab_re - 1.0) * a_re + ab_im * a_im) / den
    f_im = (ab_im * a_re - (ab_re - 1.0) * a_im) / den
    fr, fi = _dot_hi(rep, f_re), _dot_hi(rep, f_im)
    return ab_re, ab_im, fr * b_re - fi * b_im, fr * b_im + fi * b_re


def _rms(x, g):
    return (x * lax.rsqrt(jnp.mean(x * x, axis=-1, keepdims=True) + EPS) * g,)


def _ffn_act(gate, up):
    return (_silu(gate) * up,)


def _glu(a, g):
    return (a * jax.nn.sigmoid(g),)


def _ln_silu(x, g, b):
    xc = x - jnp.mean(x, axis=-1, keepdims=True)
    var = jnp.mean(xc * xc, axis=-1, keepdims=True)
    return (_silu(xc * lax.rsqrt(var + EPS) * g + b),)


def _s5_post(y, u, d_skip, glu_w):
    s = jax.nn.gelu(y + d_skip * u)
    return (s * jax.nn.sigmoid(_mm(s, glu_w)),)


def loss_head(x, tgt, g, *, tl=512):
    length, d = x.shape
    tl = min(tl, length)

    def body(x_ref, t_ref, g_ref, loss_ref, dx_ref, dg_ref):
        i = pl.program_id(0)
        y, vjp = jax.vjp(lambda x, g: _rms(x, g)[0], x_ref[...], g_ref[...])
        err = y - t_ref[...]
        dx, dg = vjp(err * (1.0 / d))
        dx_ref[...] = dx
        part = jnp.broadcast_to(0.5 * jnp.sum(jnp.mean(err * err, axis=-1, keepdims=True), axis=0, keepdims=True), (1, LANE))

        @pl.when(i == 0)
        def _():
            loss_ref[...] = part
            dg_ref[...] = dg

        @pl.when(i > 0)
        def _():
            loss_ref[...] += part
            dg_ref[...] += dg

    row = pl.BlockSpec((tl, d), lambda i: (i, 0))
    return pl.pallas_call(
        body, name="loss_head", grid=(length // tl,),
        in_specs=[row, row, pl.BlockSpec((1, d), lambda i: (0, 0))],
        out_specs=[pl.BlockSpec((1, LANE), lambda i: (0, 0)), row, pl.BlockSpec((1, d), lambda i: (0, 0))],
        out_shape=[jax.ShapeDtypeStruct((1, LANE), F32), jax.ShapeDtypeStruct((length, d), F32),
                   jax.ShapeDtypeStruct((1, d), F32)],
        compiler_params=_params(("arbitrary",)),
    )(x, tgt, g)


def _pad_heads(v):
    return jnp.pad(v, ((0, 0), (0, LANE - v.shape[1])))


def local_step(x, tgt, w, late_weights=None, early_reduce=None):
    length = x.shape[0]
    cos2, sin2 = _rotary_tables(length)
    grads = {}
    w = dict(w)

    def rms_fwd(xin, g, name):
        return rowwise_fwd(_rms, [xin], [], [g], [], [(D_MODEL, BF16)], name=name, tl=512)[0]

    def rms_bwd(xin, g, dh, dxo, name):
        return rowwise_bwd(_rms, [xin], [], [g], [], [dh], [F32], name=name, tl=512, add=dxo)

    def ffn_fwd(i, xin):
        hf = rms_fwd(xin, w["ffn_norm"][i:i + 1], f"ffn{i}_norm")
        w_up = ffn_interleave(w["ffn_w_up"][i])
        a = matmul(hf, w_up, name=f"ffn{i}_up")
        act = ffn_conv_act(a, ffn_interleave(w["ffn_dw_w"][i]), ffn_interleave(w["ffn_dw_b"][i:i + 1]),
                           name=f"ffn{i}_conv_act")
        return matmul(act, w["ffn_w_down"][i], res=xin, name=f"ffn{i}_down"), (hf, a, act, w_up)

    def ffn_bwd(i, xin, saved, dxo):
        hf, a, act, w_up = saved
        dact = matmul(dxo, w["ffn_w_down"][i], tb=True, name=f"ffn{i}_down_dx")
        dw_down = matmul(act, dxo, ta=True, name=f"ffn{i}_down_dw")
        da, ddw_w, ddw_b = ffn_conv_act_bwd(a, ffn_interleave(w["ffn_dw_w"][i]), ffn_interleave(w["ffn_dw_b"][i:i + 1]),
                                            dact, name=f"ffn{i}_conv_act_bwd")
        dw_up = ffn_deinterleave(matmul(hf, da, ta=True, name=f"ffn{i}_up_dw"))
        dhf = matmul(da, w_up, tb=True, name=f"ffn{i}_up_dx")
        dxin, dnorm = rms_bwd(xin, w["ffn_norm"][i:i + 1], dhf, dxo, f"ffn{i}_norm_bwd")
        return dxin, dict(ffn_norm=dnorm, ffn_w_up=dw_up, ffn_dw_w=ffn_deinterleave(ddw_w),
                          ffn_dw_b=ffn_deinterleave(ddw_b), ffn_w_down=dw_down)

    w_in_e = jnp.pad(w["e_w_in"][0], ((0, 0), (0, EVEN_IN_PAD - EVEN_IN)))
    conv_w_e, conv_b_e = w["e_conv_w"][0], w["e_conv_b"]
    dt_bias, a_log, d_skip = _pad_heads(w["e_dt_bias"]), _pad_heads(w["e_a_log"]), _pad_heads(w["e_d"])
    xbc_off = 4 * D_MODEL

    hn0 = rms_fwd(x, w["mix_norm"][0:1], "mix0_norm")
    proj0 = matmul(hn0, w_in_e, name="even_in")
    y_ret, ret_states = retention_fwd(proj0, cos2, sin2)
    xbc = conv_fwd(proj0, conv_w_e, conv_b_e, act=True, off=xbc_off, name="ssd_conv")
    y_ssm, ssd_states = ssd_fwd(xbc, proj0, dt_bias, a_log, d_skip, w["e_ssm_norm"])
    mix0 = jnp.concatenate([y_ret, y_ssm], axis=1)
    if late_weights is not None:
        w.update(late_weights(y_ssm))
    w_out_e = w["e_w_out"][0]
    x1 = matmul(mix0, w_out_e, res=x, name="even_out")
    x2, ffn0_saved = ffn_fwd(0, x1)

    w_in_o, w_out_o, glu_w = w["o_w_in"][0], w["o_w_out"][0], w["o_glu_w"][0]
    dw_w_o, dw_b_o, ln_g, ln_b, d_o = w["o_dw_w"][0], w["o_dw_b"], w["o_ln_g"], w["o_ln_b"], w["o_d"]
    rep = jnp.asarray(np.repeat(np.eye(S5_GROUPS, dtype=np.float32), S5_GROUP, axis=0))
    rows_gc = (S5_GROUPS * S5_GROUP, S5_STATE)
    prep_in = [w["o_a_re"][0], w["o_a_im"][0], w["o_log_step"].reshape(S5_GROUPS, 1),
               w["o_b_re"][0].transpose(0, 2, 1).reshape(rows_gc), w["o_b_im"][0].transpose(0, 2, 1).reshape(rows_gc), rep]
    ab_re, ab_im, bb_re, bb_im = whole_fwd(
        _s5_prep, prep_in, [(S5_GROUPS, S5_STATE)] * 2 + [rows_gc] * 2, name="s5_prep")
    a_re_row, a_im_row = ab_re.reshape(1, S5_LANES), ab_im.reshape(1, S5_LANES)
    b_re_bd, b_im_bd = _block_diag(bb_re).astype(BF16), _block_diag(bb_im).astype(BF16)
    c_re_bd = _block_diag(w["o_c_re"][0].reshape(rows_gc)).astype(BF16)
    c_im_neg_bd = _block_diag(-w["o_c_im"][0].reshape(rows_gc)).astype(BF16)

    hn1 = rms_fwd(x2, w["mix_norm"][1:2], "mix1_norm")
    proj1 = matmul(hn1, w_in_o, name="odd_in")
    half = D_MODEL // 2
    c_glu = rowwise_fwd(_glu, [Cols(proj1, half, 0), Cols(proj1, half, 1)], [], [], [], [(half, F32)],
                        name="conf_glu", tl=512)[0]
    c_conv = conv_fwd(c_glu, dw_w_o, dw_b_o, act=False, name="conf_conv")
    c_out = rowwise_fwd(_ln_silu, [c_conv], [], [ln_g, ln_b], [], [(half, BF16)], name="conf_ln", tl=512)[0]
    u_seg = _seg_interleave(proj1[:, 2 * half:])
    bu_re = matmul(u_seg, b_re_bd, name="s5_bu_re")
    bu_im = matmul(u_seg, b_im_bd, name="s5_bu_im")
    xs_re, xs_im = s5_scan(bu_re, bu_im, a_re_row, a_im_row, name="s5_scan")
    y_im = matmul(xs_im, c_im_neg_bd, tb=True, name="s5_y_im")
    y_s5 = _seg_deinterleave(matmul(xs_re, c_re_bd, tb=True, res=y_im, name="s5_y_re"))
    s_out = rowwise_fwd(_s5_post, [y_s5, Cols(proj1, half, 2)], [], [d_o, glu_w], [], [(half, BF16)],
                        name="s5_post", tl=512)[0]
    mix1 = jnp.concatenate([c_out, s_out], axis=1)
    x3 = matmul(mix1, w_out_o, res=x2, name="odd_out")
    x4, ffn1_saved = ffn_fwd(1, x3)

    loss, dx4, dfinal = loss_head(x4, tgt, w["final_norm"].reshape(1, D_MODEL))
    grads["final_norm"] = dfinal.reshape(D_MODEL)

    dx3, g_ffn1 = ffn_bwd(1, x3, ffn1_saved, dx4)
    dmix1 = matmul(dx3, w_out_o, tb=True, name="odd_out_dx")
    grads["o_w_out"] = [matmul(mix1, dx3, ta=True, name="odd_out_dw")]
    dc_conv, dln_g, dln_b = rowwise_bwd(_ln_silu, [c_conv], [], [ln_g, ln_b], [], [Cols(dmix1, half, 0)], [F32],
                                        name="conf_ln_bwd", tl=512)
    dc_glu, ddw_w_o, ddw_b_o = conv_bwd(c_glu, dw_w_o, dw_b_o, dc_conv, act=False, name="conf_conv_bwd")
    d_cacg = rowwise_bwd(_glu, [Cols(proj1, half, 0), Cols(proj1, half, 1)], [], [], [], [dc_glu], [BF16],
                         name="conf_glu_bwd", tl=512, merge=True)[0]
    dy_s5, du_post, dd_o, dglu_w = rowwise_bwd(
        _s5_post, [y_s5, Cols(proj1, half, 2)], [], [d_o, glu_w], [], [Cols(dmix1, half, 1)], [F32, F32],
        name="s5_post_bwd", tl=512)
    dy_seg = _seg_interleave(dy_s5)
    dxs_re = matmul(dy_seg, c_re_bd, name="s5_dx_re")
    dxs_im = matmul(dy_seg, c_im_neg_bd, name="s5_dx_im")
    dc_re_bd = matmul(dy_seg, xs_re, ta=True, name="s5_dc_re")
    dc_im_neg_bd = matmul(dy_seg, xs_im, ta=True, name="s5_dc_im")
    g_re, g_im, dab_re, dab_im = s5_scan(dxs_re, dxs_im, a_re_row, -a_im_row, reverse=True, states=(xs_re, xs_im),
                                         name="s5_scan_bwd", lw=LANE)
    dbb_re = _block_diag_take(matmul(u_seg, g_re, ta=True, name="s5_db_re"))
    dbb_im = _block_diag_take(matmul(u_seg, g_im, ta=True, name="s5_db_im"))
    du_im = matmul(g_im, b_im_bd, tb=True, name="s5_du_im")
    du = _seg_deinterleave(matmul(g_re, b_re_bd, tb=True, res=du_im, name="s5_du_re")) + du_post
    da_re, da_im, dlog_step, db_re, db_im = whole_bwd(
        _s5_prep, prep_in, 5,
        [dab_re.reshape(S5_GROUPS, S5_STATE), dab_im.reshape(S5_GROUPS, S5_STATE), dbb_re, dbb_im], name="s5_prep_bwd")
    gcn = (S5_GROUPS, S5_GROUP, S5_STATE)
    grads.update(
        o_a_re=da_re[None], o_a_im=da_im[None], o_log_step=dlog_step.reshape(1, S5_GROUPS),
        o_b_re=db_re.reshape(gcn).transpose(0, 2, 1)[None], o_b_im=db_im.reshape(gcn).transpose(0, 2, 1)[None],
        o_c_re=_block_diag_take(dc_re_bd).reshape(gcn)[None], o_c_im=-_block_diag_take(dc_im_neg_bd).reshape(gcn)[None],
        o_d=dd_o, o_glu_w=[dglu_w], o_dw_w=ddw_w_o[None], o_dw_b=ddw_b_o, o_ln_g=dln_g, o_ln_b=dln_b)
    dproj1 = jnp.concatenate([d_cacg, du.astype(BF16)], axis=1)
    grads["o_w_in"] = [matmul(hn1, dproj1, ta=True, name="odd_in_dw")]
    dhn1 = matmul(dproj1, w_in_o, tb=True, name="odd_in_dx")
    dx2, dmix_norm1 = rms_bwd(x2, w["mix_norm"][1:2], dhn1, dx3, "mix1_norm_bwd")

    if early_reduce is not None:
        zero = early_reduce({("o_w_in", 0): grads["o_w_in"][0], ("o_glu_w", 0): grads["o_glu_w"][0],
                             ("o_w_out", 0): grads["o_w_out"][0], ("ffn_w_up", 1): g_ffn1["ffn_w_up"],
                             ("ffn_w_down", 1): g_ffn1["ffn_w_down"]})
        w["ffn_dw_b"] = w["ffn_dw_b"] + zero
    dx1, g_ffn0 = ffn_bwd(0, x1, ffn0_saved, dx2)
    if early_reduce is not None:
        dt_bias = dt_bias + early_reduce({("ffn_w_up", 0): g_ffn0["ffn_w_up"], ("ffn_w_down", 0): g_ffn0["ffn_w_down"]})
    for k in g_ffn0:
        per_layer = [g_ffn0[k], g_ffn1[k]]
        grads[k] = per_layer if k in ("ffn_w_up", "ffn_w_down") else jnp.stack(per_layer).reshape(w[k].shape)
    dmix0 = matmul(dx1, w_out_e, tb=True, name="even_out_dx")
    grads["e_w_out"] = [matmul(mix0, dx1, ta=True, name="even_out_dw")]
    dq, dk, dv, dg = retention_bwd(proj0, cos2, sin2, ret_states, dmix0)
    dxbc_c, ddt, dz, ddt_bias, da_log, dd_skip, dssm_norm = ssd_bwd(
        xbc, proj0, dt_bias, a_log, d_skip, w["e_ssm_norm"], ssd_states, dmix0)
    dxbc, dconv_w, dconv_b = conv_bwd(proj0, conv_w_e, conv_b_e, dxbc_c, act=True, off=xbc_off,
                                      name="ssd_conv_bwd", dx_dtype=BF16)
    dproj0 = jnp.concatenate([dq, dk, dv, dg, dz, dxbc, ddt], axis=1)
    grads["e_w_in"] = [matmul(hn0, dproj0, ta=True, name="even_in_dw")[:, :EVEN_IN]]
    dhn0 = matmul(dproj0, w_in_e, tb=True, name="even_in_dx")
    dx, dmix_norm0 = rms_bwd(x, w["mix_norm"][0:1], dhn0, dx1, "mix0_norm_bwd")
    grads.update(
        mix_norm=jnp.concatenate([dmix_norm0, dmix_norm1], axis=0), e_conv_w=dconv_w[None], e_conv_b=dconv_b,
        e_dt_bias=ddt_bias[:, :SSM_HEADS], e_a_log=da_log[:, :SSM_HEADS], e_d=dd_skip[:, :SSM_HEADS],
        e_ssm_norm=dssm_norm)
    return loss, dx, grads


def adamw(w, g, m, v, *, name):
    shape = w.shape
    cols = shape[-1]
    rows = w.size // cols
    tr = _tile(rows, max(8, (512 * 1024 // cols) // 8 * 8), unit=8)

    def body(w_ref, g_ref, m_ref, v_ref, d_ref, nm_ref, nv_ref):
        gv = g_ref[...]
        nm = ADAM_B1 * m_ref[...] + (1.0 - ADAM_B1) * gv
        nv = ADAM_B2 * v_ref[...] + (1.0 - ADAM_B2) * jnp.square(gv)
        m_hat = nm / (1.0 - ADAM_B1 ** ADAM_STEP)
        v_hat = nv / (1.0 - ADAM_B2 ** ADAM_STEP)
        d_ref[...] = -ADAM_LR * (m_hat / (jnp.sqrt(v_hat) + ADAM_EPS) + ADAM_WD * w_ref[...])
        nm_ref[...] = nm
        nv_ref[...] = nv

    spec = pl.BlockSpec((tr, cols), lambda i: (i, 0))
    outs = pl.pallas_call(
        body, name=name, grid=(rows // tr,), in_specs=[spec] * 4, out_specs=[spec] * 3,
        out_shape=[jax.ShapeDtypeStruct((rows, cols), F32)] * 3, compiler_params=_params(("parallel",)),
    )(*[t.reshape(rows, cols) for t in (w, g, m, v)])
    return [o.reshape(shape) for o in outs]


OTHER_CHIPS = ((1, 0), (0, 1), (1, 1))
ANY = pl.BlockSpec(memory_space=pl.ANY)


def _position():
    return lax.axis_index("x"), lax.axis_index("y"), lax.axis_index("c")


def _flip(v, f):
    return 1 - v if f else v


def _remote(src, dst, send_sem, recv_sem, device):
    return pltpu.make_async_remote_copy(src_ref=src, dst_ref=dst, send_sem=send_sem, recv_sem=recv_sem,
                                        device_id=device, device_id_type=MESH)


def gather_shards(big, small):
    n_big, n_small = len(big), len(small)
    halves = [a.shape[0] // 2 for a in big]

    def body(*refs):
        big_refs, small_refs = refs[:n_big], refs[n_big:n_big + n_small]
        obig_refs = refs[n_big + n_small:2 * n_big + n_small]
        osmall_refs = refs[2 * n_big + n_small:2 * (n_big + n_small)]
        ici_send, ici_recv, d2d_send, d2d_recv, small_send, small_recv = refs[2 * (n_big + n_small):]
        x, y, c = _position()
        mine = 2 * x + y

        def half(k, core):
            return pl.ds(pl.multiple_of(core * halves[k], 16), halves[k])

        sends = []
        for j, (fx, fy) in enumerate(OTHER_CHIPS):
            peer = (_flip(x, fx), _flip(y, fy), c)
            for k in range(n_big):
                sends.append(_remote(big_refs[k].at[half(k, c)], obig_refs[k].at[mine, half(k, c)],
                                     ici_send.at[j, k], ici_recv.at[j, k], peer))
            for k in range(n_small):
                sends.append(_remote(small_refs[k], osmall_refs[k].at[mine], small_send.at[j, k], small_recv.at[j, k], peer))
        for cp in sends:
            cp.start()
        for j, (fx, fy) in enumerate(OTHER_CHIPS):
            px, py = _flip(x, fx), _flip(y, fy)
            src_chip = 2 * px + py
            for k in range(n_big):
                landed = obig_refs[k].at[src_chip, half(k, c)]
                _remote(landed, landed, ici_send.at[j, k], ici_recv.at[j, k], (px, py, c)).wait_recv()
                fwd = _remote(landed, landed, d2d_send.at[j, k], d2d_recv.at[j, k], (x, y, 1 - c))
                fwd.start()
                sends.append(fwd)
        for j, (fx, fy) in enumerate(OTHER_CHIPS):
            px, py = _flip(x, fx), _flip(y, fy)
            src_chip = 2 * px + py
            for k in range(n_big):
                other = obig_refs[k].at[src_chip, half(k, 1 - c)]
                _remote(other, other, d2d_send.at[j, k], d2d_recv.at[j, k], (x, y, 1 - c)).wait_recv()
            for k in range(n_small):
                dst = osmall_refs[k].at[src_chip]
                _remote(small_refs[k], dst, small_send.at[j, k], small_recv.at[j, k], (px, py, c)).wait_recv()
        for cp in sends:
            cp.wait_send()

    arrays = list(big) + list(small)
    dma = pltpu.SemaphoreType.DMA
    return pl.pallas_call(
        body, name="gather_shards", in_specs=[ANY] * len(arrays), out_specs=[ANY] * len(arrays),
        out_shape=[jax.ShapeDtypeStruct((4,) + a.shape, a.dtype) for a in arrays],
        scratch_shapes=[dma((3, n_big)), dma((3, n_big)), dma((3, n_big)), dma((3, n_big)),
                        dma((3, n_small)), dma((3, n_small))],
        compiler_params=_params(),
    )(*arrays)


def allreduce_small(pack):
    rows = pack.shape[0]

    def body(p_ref, o_ref, slots, send_sems, recv_sems):
        x, y, c = _position()
        me = 4 * x + 2 * y + c
        slots[me] = p_ref[...]
        flips = [((k >> 2) & 1, (k >> 1) & 1, k & 1) for k in range(1, 8)]
        sends = []
        for k, (fx, fy, fc) in enumerate(flips):
            peer = (_flip(x, fx), _flip(y, fy), _flip(c, fc))
            sends.append(_remote(p_ref, slots.at[me], send_sems.at[k], recv_sems.at[k], peer))
        for cp in sends:
            cp.start()
        for k, (fx, fy, fc) in enumerate(flips):
            px, py, pc = _flip(x, fx), _flip(y, fy), _flip(c, fc)
            _remote(p_ref, slots.at[4 * px + 2 * py + pc], send_sems.at[k], recv_sems.at[k], (px, py, pc)).wait_recv()
        for cp in sends:
            cp.wait_send()
        acc = slots[0]
        for d in range(1, 8):
            acc = acc + slots[d]
        o_ref[...] = acc

    vmem = pl.BlockSpec(memory_space=pltpu.VMEM)
    return pl.pallas_call(
        body, name="allreduce_small", in_specs=[vmem], out_specs=vmem,
        out_shape=jax.ShapeDtypeStruct(pack.shape, F32),
        scratch_shapes=[pltpu.VMEM((8, rows, LANE), F32), pltpu.SemaphoreType.DMA((7,)), pltpu.SemaphoreType.DMA((7,))],
        compiler_params=_params(),
    )(pack)


def exchange_halves(gs, *, name):
    n = len(gs)

    def body(*refs):
        g_refs, o_refs, (send_sems, recv_sems) = refs[:n], refs[n:2 * n], refs[2 * n:]
        x, y, c = _position()
        copies = [_remote(g_refs[k].at[:, 1 - c], o_refs[k], send_sems.at[k], recv_sems.at[k], (x, y, 1 - c)) for k in range(n)]
        for cp in copies:
            cp.start()
        for cp in copies:
            cp.wait()

    return pl.pallas_call(
        body, name=name, in_specs=[ANY] * n, out_specs=[ANY] * n,
        out_shape=[jax.ShapeDtypeStruct((4,) + g.shape[2:], g.dtype) for g in gs],
        scratch_shapes=[pltpu.SemaphoreType.DMA((n,)), pltpu.SemaphoreType.DMA((n,))],
        compiler_params=_params(),
    )(*gs)


def scatter_to_chips(parts):
    n = len(parts)

    def body(*refs):
        a_refs, o_refs, (send_sems, recv_sems) = refs[:n], refs[n:2 * n], refs[2 * n:]
        x, y, c = _position()
        copies = []
        for j, (fx, fy) in enumerate(OTHER_CHIPS):
            px, py = _flip(x, fx), _flip(y, fy)
            for k in range(n):
                copies.append(_remote(a_refs[k].at[2 * px + py], o_refs[k].at[j], send_sems.at[j, k], recv_sems.at[j, k], (px, py, c)))
        for cp in copies:
            cp.start()
        for cp in copies:
            cp.wait()

    return pl.pallas_call(
        body, name="scatter_to_chips", in_specs=[ANY] * n, out_specs=[ANY] * n,
        out_shape=[jax.ShapeDtypeStruct((3,) + a.shape[1:], a.dtype) for a in parts],
        scratch_shapes=[pltpu.SemaphoreType.DMA((3, n)), pltpu.SemaphoreType.DMA((3, n))],
        compiler_params=_params(),
    )(*parts)


def swap_halves(rs):
    n = len(rs)

    def body(*refs):
        r_refs, o_refs, (send_sems, recv_sems) = refs[:n], refs[n:2 * n], refs[2 * n:]
        x, y, c = _position()
        copies = [_remote(r_refs[k], o_refs[k], send_sems.at[k], recv_sems.at[k], (x, y, 1 - c)) for k in range(n)]
        for cp in copies:
            cp.start()
        for cp in copies:
            cp.wait()

    dma = pltpu.SemaphoreType.DMA
    return pl.pallas_call(
        body, name="swap_halves", in_specs=[ANY] * n, out_specs=[ANY] * n,
        out_shape=[jax.ShapeDtypeStruct(r.shape, r.dtype) for r in rs],
        scratch_shapes=[dma((n,)), dma((n,))],
        compiler_params=_params(),
    )(*rs)


HBM = pl.BlockSpec(memory_space=pltpu.HBM)
SEM = pl.BlockSpec(memory_space=pltpu.SEMAPHORE)
SIDE_EFFECT = pltpu.SideEffectType.DATAFLOW_SIDE_EFFECTING


def _gather_plan(halves):
    def plan(v_refs, land_refs, x, y, c):
        copies = []
        for fx, fy in OTHER_CHIPS:
            for k in range(len(v_refs)):
                rows = pl.ds(pl.multiple_of(c * halves[k], 16), halves[k])
                copies.append((v_refs[k].at[rows], land_refs[k].at[2 * x + y, rows], (_flip(x, fx), _flip(y, fy), c)))
        return copies
    return plan


def _scatter_plan(v_refs, land_refs, x, y, c):
    copies = []
    for j, (fx, fy) in enumerate(OTHER_CHIPS):
        px, py = _flip(x, fx), _flip(y, fy)
        for k in range(len(v_refs)):
            copies.append((v_refs[k].at[2 * px + py], land_refs[k].at[j], (px, py, c)))
    return copies


def chip_exchange_start(srcs, land_shapes, plan, after, *, name):
    n = len(srcs)
    n_cp = 3 * n

    def body(*refs):
        v_refs, land_refs = refs[:n], refs[n:2 * n]
        outs = refs[2 * n + 1:]
        sends, recvs, token = outs[:n_cp], outs[n_cp:2 * n_cp], outs[-1]
        x, y, c = _position()
        for (src, dst, device), send, recv in zip(plan(v_refs, land_refs, x, y, c), sends, recvs, strict=True):
            _remote(src, dst, send, recv, device).start()
        token[...] = jnp.zeros_like(token)

    lands = [lax.empty(shape, v.dtype) for shape, v in zip(land_shapes, srcs)]
    arrays = [pltpu.with_memory_space_constraint(a, pltpu.HBM) for a in list(srcs) + lands]
    outs = pl.pallas_call(
        body, name=name,
        out_shape=tuple(pltpu.SemaphoreType.DMA(()) for _ in range(2 * n_cp))
        + tuple(pltpu.HBM(a.shape, a.dtype) for a in arrays) + (jax.ShapeDtypeStruct((8, LANE), F32),),
        in_specs=[HBM] * (2 * n) + [ANY],
        out_specs=(SEM,) * (2 * n_cp) + (HBM,) * (2 * n) + (pl.BlockSpec(memory_space=pltpu.VMEM),),
        input_output_aliases={i: 2 * n_cp + i for i in range(2 * n)},
        compiler_params=pltpu.CompilerParams(has_side_effects=SIDE_EFFECT),
    )(*arrays, after)
    handle = (outs[:n_cp], outs[n_cp:2 * n_cp], outs[2 * n_cp:2 * n_cp + n], outs[2 * n_cp + n:2 * n_cp + 2 * n])
    return handle, outs[-1]


def chip_exchange_wait(handle, plan, after, *, name):
    sends, recvs, v_thru, land_thru = handle
    n = len(v_thru)
    n_cp = 3 * n

    def body(*refs):
        v_refs, land_refs = refs[:n], refs[n:2 * n]
        sends, recvs = refs[2 * n:2 * n + n_cp], refs[2 * n + n_cp:2 * n + 2 * n_cp]
        x, y, c = _position()
        for (src, dst, device), send, recv in zip(plan(v_refs, land_refs, x, y, c), sends, recvs, strict=True):
            copy = _remote(src, dst, send, recv, device)
            copy.wait_send()
            copy.wait_recv()

    outs = pl.pallas_call(
        body, name=name,
        out_shape=tuple(pltpu.HBM(a.shape, a.dtype) for a in list(v_thru) + list(land_thru)),
        in_specs=[HBM] * (2 * n) + [SEM] * (2 * n_cp) + [ANY], out_specs=(HBM,) * (2 * n),
        input_output_aliases={i: i for i in range(2 * n)},
        compiler_params=pltpu.CompilerParams(has_side_effects=SIDE_EFFECT),
    )(*v_thru, *land_thru, *sends, *recvs, after)
    return outs[:n], outs[n:]


def finish_gather(lands):
    n = len(lands)
    halves = [a.shape[1] // 2 for a in lands]

    def body(*refs):
        o_refs, (send_sems, recv_sems) = refs[n:2 * n], refs[2 * n:]
        x, y, c = _position()

        def half(k, core):
            return pl.ds(pl.multiple_of(core * halves[k], 16), halves[k])

        sends = []
        for j, (fx, fy) in enumerate(OTHER_CHIPS):
            src_chip = 2 * _flip(x, fx) + _flip(y, fy)
            for k in range(n):
                held = o_refs[k].at[src_chip, half(k, c)]
                sends.append(_remote(held, held, send_sems.at[j, k], recv_sems.at[j, k], (x, y, 1 - c)))
        for cp in sends:
            cp.start()
        for j, (fx, fy) in enumerate(OTHER_CHIPS):
            src_chip = 2 * _flip(x, fx) + _flip(y, fy)
            for k in range(n):
                other = o_refs[k].at[src_chip, half(k, 1 - c)]
                _remote(other, other, send_sems.at[j, k], recv_sems.at[j, k], (x, y, 1 - c)).wait_recv()
        for cp in sends:
            cp.wait_send()

    dma = pltpu.SemaphoreType.DMA
    return pl.pallas_call(
        body, name="finish_gather", in_specs=[ANY] * n, out_specs=[ANY] * n,
        out_shape=[jax.ShapeDtypeStruct(a.shape, a.dtype) for a in lands],
        input_output_aliases={k: k for k in range(n)},
        scratch_shapes=[dma((3, n)), dma((3, n))],
        compiler_params=_params(),
    )(*lands)


def add_own_half(g, r, c_idx, *, name):
    _, _, h, cols = g.shape

    def body(c_ref, g_ref, r_ref, o_ref):
        o_ref[...] = (g_ref[0] + r_ref[...]).astype(o_ref.dtype)

    return pl.pallas_call(
        body, name=name,
        grid_spec=pltpu.PrefetchScalarGridSpec(
            num_scalar_prefetch=1, grid=(4,),
            in_specs=[pl.BlockSpec((1, 1, h, cols), lambda s, c: (s, c[0], 0, 0)),
                      pl.BlockSpec((1, h, cols), lambda s, c: (s, 0, 0))],
            out_specs=pl.BlockSpec((1, h, cols), lambda s, c: (s, 0, 0))),
        out_shape=jax.ShapeDtypeStruct(r.shape, BF16), compiler_params=_params(("parallel",)),
    )(c_idx, g, r)


def add_chip_parts(a, parts, chip_idx, *, name):
    _, h, cols = a.shape
    th = h // 2

    def body(s_ref, a_ref, p0_ref, p1_ref, p2_ref, o_ref):
        f = lambda r: r[0].astype(F32)
        o_ref[...] = ((f(a_ref) + f(p0_ref)) + f(p1_ref)) + f(p2_ref)

    part = lambda j: pl.BlockSpec((1, th, cols), lambda i, s, j=j: (j, i, 0))
    return pl.pallas_call(
        body, name=name,
        grid_spec=pltpu.PrefetchScalarGridSpec(
            num_scalar_prefetch=1, grid=(2,),
            in_specs=[pl.BlockSpec((1, th, cols), lambda i, s: (s[0], i, 0)), part(0), part(1), part(2)],
            out_specs=pl.BlockSpec((th, cols), lambda i, s: (i, 0))),
        out_shape=jax.ShapeDtypeStruct((h, cols), F32), compiler_params=_params(("parallel",)),
    )(chip_idx, a, parts, parts, parts)


WEIGHTS = ("mix_norm", "e_w_in", "e_conv_w", "e_conv_b", "e_dt_bias", "e_a_log", "e_d", "e_ssm_norm", "e_w_out",
           "o_w_in", "o_dw_w", "o_dw_b", "o_ln_g", "o_ln_b", "o_a_re", "o_a_im", "o_b_re", "o_b_im", "o_c_re",
           "o_c_im", "o_d", "o_log_step", "o_glu_w", "o_w_out", "ffn_norm", "ffn_w_up", "ffn_dw_w", "ffn_dw_b",
           "ffn_w_down", "final_norm")
BIG = (("e_w_in", 2), ("e_w_out", 1), ("o_w_in", 2), ("o_glu_w", 1), ("o_w_out", 1), ("ffn_w_up", 2), ("ffn_w_down", 1))
SMALL_SHARDED = (("e_conv_w", 2), ("o_dw_w", 2), ("o_dw_b", 1), ("o_ln_g", 1), ("o_ln_b", 1), ("o_d", 1), ("ffn_dw_w", 2))
REPLICATED = tuple(n for n in WEIGHTS if n not in dict(BIG + SMALL_SHARDED))
PACK_ROWS = 8


def _pack(arrays, dtype, row_unit=PACK_ROWS):
    flat = jnp.concatenate([a.astype(dtype).reshape(-1) for a in arrays])
    rows = -(-flat.size // (LANE * row_unit)) * row_unit
    return jnp.pad(flat, (0, rows * LANE - flat.size)).reshape(rows, LANE)


def _unpack(flat, shapes, lead=()):
    out, off = [], 0
    for shape in shapes:
        size = int(np.prod(shape))
        out.append(flat[..., off:off + size].reshape(lead + tuple(shape)))
        off += size
    return out


def _join_shards(parts, axis):
    return jnp.concatenate([parts[s] for s in range(4)], axis=axis)


def _split_shards(full, axis):
    return jnp.stack(jnp.split(full, 4, axis=axis))


def _rows2d(a):
    return a.reshape(-1, a.shape[-1])


def _layer_shards(g, axis):
    rows, cols = g.shape
    if axis == 0:
        return g.reshape(4, 2, rows // 8, cols)
    return g.reshape(rows, 4, cols // 4).transpose(1, 0, 2).reshape(4, 2, rows // 2, cols // 4)


def kernel(x, mix_norm, e_w_in, e_conv_w, e_conv_b, e_dt_bias, e_a_log, e_d, e_ssm_norm, e_w_out, o_w_in, o_dw_w, o_dw_b, o_ln_g, o_ln_b, o_a_re, o_a_im, o_b_re, o_b_im, o_c_re, o_c_im, o_d, o_log_step, o_glu_w, o_w_out, ffn_norm, ffn_w_up, ffn_dw_w, ffn_dw_b, ffn_w_down, final_norm, loss_target, m_mix_norm, m_e_w_in, m_e_conv_w, m_e_conv_b, m_e_dt_bias, m_e_a_log, m_e_d, m_e_ssm_norm, m_e_w_out, m_o_w_in, m_o_dw_w, m_o_dw_b, m_o_ln_g, m_o_ln_b, m_o_a_re, m_o_a_im, m_o_b_re, m_o_b_im, m_o_c_re, m_o_c_im, m_o_d, m_o_log_step, m_o_glu_w, m_o_w_out, m_ffn_norm, m_ffn_w_up, m_ffn_dw_w, m_ffn_dw_b, m_ffn_w_down, m_final_norm, v_mix_norm, v_e_w_in, v_e_conv_w, v_e_conv_b, v_e_dt_bias, v_e_a_log, v_e_d, v_e_ssm_norm, v_e_w_out, v_o_w_in, v_o_dw_w, v_o_dw_b, v_o_ln_g, v_o_ln_b, v_o_a_re, v_o_a_im, v_o_b_re, v_o_b_im, v_o_c_re, v_o_c_im, v_o_d, v_o_log_step, v_o_glu_w, v_o_w_out, v_ffn_norm, v_ffn_w_up, v_ffn_dw_w, v_ffn_dw_b, v_ffn_w_down, v_final_norm):
    given = dict(locals())
    chip = 2 * lax.axis_index("x") + lax.axis_index("y")
    core = lax.axis_index("c")

    core_idx, chip_idx = core.reshape(1).astype(jnp.int32), chip.reshape(1).astype(jnp.int32)

    def whole(n, axis, parts):
        shape = given[n].shape
        own = given[n].astype(parts.dtype)
        return _join_shards(lax.dynamic_update_index_in_dim(parts.reshape((4,) + shape), own, chip, 0), axis)

    first, later = BIG[:1], BIG[1:]
    shards = {n: _rows2d(given[n]).astype(BF16) for n, _ in BIG}
    gathered = gather_shards([shards[n] for n, _ in first], [_rows2d(given[n]) for n, _ in SMALL_SHARDED])
    w = {n: given[n] for n in REPLICATED}
    for (n, axis), parts in zip(first + SMALL_SHARDED, gathered):
        w[n] = whole(n, axis, parts)
    later_shards = [shards[n] for n, _ in later]
    gather_plan = _gather_plan([a.shape[0] // 2 for a in later_shards])
    gather_handle, token = chip_exchange_start(later_shards, [(4,) + a.shape for a in later_shards], gather_plan,
                                               gathered[0], name="gather_start")
    w["mix_norm"] = w["mix_norm"] + token[0, 0]

    def late_weights(after):
        _, lands = chip_exchange_wait(gather_handle, gather_plan, after, name="gather_wait")
        return {n: whole(n, axis, parts) for (n, axis), parts in zip(later, finish_gather(lands))}

    groups = []

    def finish_group(after):
        group = groups[-1]
        group["sums"], group["parts"] = chip_exchange_wait(group.pop("handle"), _scatter_plan, after,
                                                           name=f"scatter_wait_{len(groups) - 1}")

    def early_reduce(layer_grads):
        keys = list(layer_grads)
        if groups:
            finish_group(layer_grads[keys[0]])
        tag = len(groups)
        parts = [_layer_shards(layer_grads[k], dict(BIG)[k[0]] - 1) for k in keys]
        sums = [add_own_half(g, r, core_idx, name=f"add_own_half_{n}{layer}")
                for g, r, (n, layer) in zip(parts, exchange_halves(parts, name=f"exchange_halves_{tag}"), keys)]
        handle, zeros = chip_exchange_start(sums, [(3,) + a.shape[1:] for a in sums], _scatter_plan, sums[0],
                                            name=f"scatter_start_{tag}")
        groups.append(dict(keys=keys, handle=handle))
        return zeros[0, 0]

    loss, dx, grads = local_step(x[0], loss_target[0], w, late_weights, early_reduce)
    finish_group(dx)
    early_keys = [k for group in groups for k in group["keys"]]
    early_sums = [a for group in groups for a in group["sums"]]
    early_parts = [a for group in groups for a in group["parts"]]

    small_names = REPLICATED + tuple(n for n, _ in SMALL_SHARDED)
    small_sum = allreduce_small(_pack([grads[n] for n in small_names], F32))
    reduced = dict(zip(small_names, _unpack(small_sum.reshape(-1), [grads[n].shape for n in small_names])))
    for n, axis in SMALL_SHARDED:
        width = given[n].shape[axis]
        reduced[n] = lax.dynamic_slice_in_dim(reduced[n], chip * width, width, axis=axis)

    keys, parts = [], []
    for n, axis in BIG:
        for layer, g in enumerate(grads[n]):
            if (n, layer) not in early_keys:
                keys.append((n, layer))
                parts.append(_layer_shards(g, axis - 1))
    core_sums = [add_own_half(g, r, core_idx, name=f"add_own_half_{n}{layer}")
                 for g, r, (n, layer) in zip(parts, exchange_halves(parts, name="exchange_halves_last"), keys)]
    chip_parts = scatter_to_chips(core_sums)
    keys, core_sums, chip_parts = early_keys + keys, early_sums + core_sums, early_parts + list(chip_parts)
    mine = [add_chip_parts(a, p, chip_idx, name=f"add_chip_parts_{n}{layer}")
            for a, p, (n, layer) in zip(core_sums, chip_parts, keys)]
    layers = {}
    for (n, layer), own, other in zip(keys, mine, swap_halves(mine)):
        both = jnp.where(core == 0, jnp.stack([own, other]), jnp.stack([other, own]))
        layers.setdefault(n, {})[layer] = both.reshape(given[n].shape[1:])
    for n, _ in BIG:
        reduced[n] = jnp.stack([layers[n][layer] for layer in sorted(layers[n])])

    delta, new_m, new_v = {}, {}, {}
    for n, _ in BIG:
        delta[n], new_m[n], new_v[n] = adamw(given[n], reduced[n], given["m_" + n], given["v_" + n], name="adamw_" + n)
    shapes = [given[n].shape for n in small_names]
    packed = [_pack([src[n] for n in small_names], F32)
              for src in (given, reduced, {n: given["m_" + n] for n in small_names}, {n: given["v_" + n] for n in small_names})]
    for dst, res in zip((delta, new_m, new_v), adamw(*packed, name="adamw_small")):
        dst.update(zip(small_names, _unpack(res.reshape(-1), shapes)))

    total = lax.psum(loss[0, 0], ("x", "y", "c"))
    return (total, dx[None], *[reduced[n] for n in WEIGHTS], *[delta[n] for n in WEIGHTS],
            *[new_m[n] for n in WEIGHTS], *[new_v[n] for n in WEIGHTS])
```

```python
import functools
import math
from typing import NamedTuple

import numpy as np
import jax
import jax.numpy as jnp
from jax import lax
from jax.experimental import pallas as pl
from jax.experimental.pallas import tpu as pltpu

F32 = jnp.float32
BF16 = jnp.bfloat16
HIGHEST = lax.Precision.HIGHEST
MESH = pl.DeviceIdType.MESH

D_MODEL = 1024
EPS = 1e-6
RET_HEADS, RET_DK, RET_DV, CHUNK = 4, 128, 256, 128
ROPE_BASE = 10000.0
SSM_HEADS, SSM_P, SSM_N, SSM_GROUPS = 16, 64, 128, 2
SSM_DINNER = SSM_HEADS * SSM_P
EVEN_IN, EVEN_IN_PAD = 5648, 5760
S5_GROUPS, S5_GROUP, S5_STATE = 32, 16, 64
S5_LANES = S5_GROUPS * S5_STATE
SCAN_SEG = 8
D_FF = 2816
ADAM_LR, ADAM_B1, ADAM_B2, ADAM_EPS, ADAM_WD, ADAM_STEP = 0.001, 0.9, 0.999, 1e-08, 0.01, 10

LANE = 128
VMEM_LIMIT = 56 * 1024 * 1024


def _params(sem=None, **kw):
    return pltpu.CompilerParams(dimension_semantics=sem, vmem_limit_bytes=VMEM_LIMIT, **kw)


def _tile(n, target, unit=LANE):
    if n <= target:
        return n
    t = (target // unit) * unit
    while t >= unit:
        if n % t == 0:
            return t
        t -= unit
    return n


def _silu(x):
    return x * jax.nn.sigmoid(x)


def _mm(a, b):
    return jnp.dot(a.astype(BF16), b.astype(BF16), preferred_element_type=F32)


def _mm_nt(a, b):
    return lax.dot_general(a.astype(BF16), b.astype(BF16), (((1,), (1,)), ((), ())), preferred_element_type=F32)


def _mm_tn(a, b):
    return lax.dot_general(a.astype(BF16), b.astype(BF16), (((0,), (0,)), ((), ())), preferred_element_type=F32)


def _dot_hi(a, b):
    return jnp.dot(a, b, precision=HIGHEST, preferred_element_type=F32)


def _dot_hi_tn(a, b):
    return lax.dot_general(a, b, (((0,), (0,)), ((), ())), precision=HIGHEST, preferred_element_type=F32)


MATMUL_VMEM = 44 * 1024 * 1024


def matmul(a, b, *, ta=False, tb=False, res=None, out_dtype=F32, name):
    m, k = (a.shape[1], a.shape[0]) if ta else a.shape
    n = b.shape[0] if tb else b.shape[1]
    assert (b.shape[1] if tb else b.shape[0]) == k, (a.shape, b.shape, ta, tb)
    tm = _tile(m, 1536)
    tn = _tile(n, 640)
    if tn < 384:
        tn = _tile(n, 1536)
    res_bytes = 0 if res is None else res.dtype.itemsize

    def vmem(tm, tn):
        return 2 * (tm * k * a.dtype.itemsize + tn * k * b.dtype.itemsize + tm * tn * (jnp.dtype(out_dtype).itemsize + res_bytes))

    while vmem(tm, tn) > MATMUL_VMEM and tm % (2 * LANE) == 0:
        tm //= 2
    assert vmem(tm, tn) <= MATMUL_VMEM, (name, tm, tn, k)
    a_spec = pl.BlockSpec((k, tm), lambda i, j: (0, i)) if ta else pl.BlockSpec((tm, k), lambda i, j: (i, 0))
    b_spec = pl.BlockSpec((tn, k), lambda i, j: (j, 0)) if tb else pl.BlockSpec((k, tn), lambda i, j: (0, j))
    o_spec = pl.BlockSpec((tm, tn), lambda i, j: (i, j))
    dims = (((0 if ta else 1,), (1 if tb else 0,)), ((), ()))
    has_res = res is not None

    def body(a_ref, b_ref, *rest):
        o_ref = rest[-1]
        out = lax.dot_general(a_ref[...].astype(BF16), b_ref[...].astype(BF16), dims, preferred_element_type=F32)
        if has_res:
            out = out + rest[0][...].astype(F32)
        o_ref[...] = out.astype(o_ref.dtype)

    ins = [a, b] + ([res] if has_res else [])
    specs = [a_spec, b_spec] + ([o_spec] if has_res else [])
    return pl.pallas_call(
        body, name=name, grid=(m // tm, n // tn), in_specs=specs, out_specs=o_spec,
        out_shape=jax.ShapeDtypeStruct((m, n), out_dtype), compiler_params=_params(("parallel", "parallel")),
    )(*ins)


class Cols(NamedTuple):
    arr: jax.Array
    w: int
    j: int


def _cols(a):
    return a if isinstance(a, Cols) else Cols(a, a.shape[1], 0)


def _row_spec(c, tl):
    return pl.BlockSpec((tl, c.w), lambda i, j=c.j: (i, j))


def _whole_spec(p):
    return pl.BlockSpec(p.shape, lambda i, nd=p.ndim: (0,) * nd)


def rowwise_fwd(fn, rows, aux, pars, consts, outs, *, name, tl):
    rows = [_cols(r) for r in rows + aux]
    whole = list(pars) + list(consts)
    n_rows = len(rows)
    n_whole = len(whole)
    length = rows[0].arr.shape[0]
    tl = min(tl, length)

    def body(*refs):
        vals = [r[...].astype(F32) for r in refs[:n_rows]] + [r[...] for r in refs[n_rows:n_rows + n_whole]]
        res = fn(*vals)
        for o_ref, v in zip(refs[n_rows + n_whole:], res, strict=True):
            o_ref[...] = v.astype(o_ref.dtype)

    return pl.pallas_call(
        body, name=name, grid=(length // tl,),
        in_specs=[_row_spec(r, tl) for r in rows] + [_whole_spec(p) for p in whole],
        out_specs=[pl.BlockSpec((tl, w), lambda i: (i, 0)) for w, _ in outs],
        out_shape=[jax.ShapeDtypeStruct((length, w), dt) for w, dt in outs],
        compiler_params=_params(("parallel",)),
    )(*[r.arr for r in rows], *whole)


def rowwise_bwd(fn, rows, aux, pars, consts, cots, drow_dtypes, *, name, tl, add=None, merge=False):
    rows = [_cols(r) for r in rows]
    aux = [_cols(r) for r in aux]
    cots = [_cols(r) for r in cots]
    n_r, n_a, n_p, n_c, n_t = len(rows), len(aux), len(pars), len(consts), len(cots)
    length = rows[0].arr.shape[0]
    tl = min(tl, length)
    has_add = add is not None
    widths = [r.w for r in rows]

    def body(*refs):
        pos = 0
        r_vals = [r[...].astype(F32) for r in refs[pos:pos + n_r]]; pos += n_r
        a_vals = [r[...].astype(F32) for r in refs[pos:pos + n_a]]; pos += n_a
        p_vals = [r[...].astype(F32) for r in refs[pos:pos + n_p]]; pos += n_p
        c_vals = [r[...] for r in refs[pos:pos + n_c]]; pos += n_c
        t_vals = [r[...].astype(F32) for r in refs[pos:pos + n_t]]; pos += n_t
        add_val = None
        if has_add:
            add_val = refs[pos][...].astype(F32); pos += 1
        n_dr = 1 if merge else n_r
        dr_refs = refs[pos:pos + n_dr]; pos += n_dr
        dp_refs = refs[pos:pos + n_p]

        def f(*rp):
            return fn(*rp[:n_r], *a_vals, *rp[n_r:], *c_vals)

        _, vjp = jax.vjp(f, *r_vals, *p_vals)
        grads = vjp(tuple(t_vals))
        drows = list(grads[:n_r])
        if has_add:
            drows[0] = drows[0] + add_val
        if merge:
            off = 0
            for w, d in zip(widths, drows):
                dr_refs[0][:, off:off + w] = d.astype(dr_refs[0].dtype)
                off += w
        else:
            for r, d in zip(dr_refs, drows):
                r[...] = d.astype(r.dtype)
        i = pl.program_id(0)
        for r, d in zip(dp_refs, grads[n_r:]):
            @pl.when(i == 0)
            def _(r=r, d=d):
                r[...] = d

            @pl.when(i > 0)
            def _(r=r, d=d):
                r[...] += d

    if merge:
        dr_specs = [pl.BlockSpec((tl, sum(widths)), lambda i: (i, 0))]
        dr_shapes = [jax.ShapeDtypeStruct((length, sum(widths)), drow_dtypes[0])]
    else:
        dr_specs = [pl.BlockSpec((tl, w), lambda i: (i, 0)) for w in widths]
        dr_shapes = [jax.ShapeDtypeStruct((length, w), dt) for w, dt in zip(widths, drow_dtypes)]
    ins = [r.arr for r in rows + aux] + list(pars) + list(consts) + [r.arr for r in cots] + ([add] if has_add else [])
    specs = ([_row_spec(r, tl) for r in rows + aux] + [_whole_spec(p) for p in list(pars) + list(consts)]
             + [_row_spec(r, tl) for r in cots] + ([pl.BlockSpec((tl, add.shape[1]), lambda i: (i, 0))] if has_add else []))
    return pl.pallas_call(
        body, name=name, grid=(length // tl,), in_specs=specs,
        out_specs=dr_specs + [_whole_spec(p) for p in pars],
        out_shape=dr_shapes + [jax.ShapeDtypeStruct(p.shape, F32) for p in pars],
        compiler_params=_params(("arbitrary",)),
    )(*ins)


def whole_fwd(fn, ins, out_shapes, *, name):
    n_in = len(ins)

    def body(*refs):
        res = fn(*[r[...] for r in refs[:n_in]])
        for o_ref, v in zip(refs[n_in:], res, strict=True):
            o_ref[...] = v

    return pl.pallas_call(body, name=name, out_shape=[jax.ShapeDtypeStruct(s, F32) for s in out_shapes],
                          compiler_params=_params())(*ins)


def whole_bwd(fn, ins, n_diff, cots, *, name):
    n_in, n_t = len(ins), len(cots)

    def body(*refs):
        vals = [r[...] for r in refs[:n_in]]
        t_vals = [r[...] for r in refs[n_in:n_in + n_t]]
        _, vjp = jax.vjp(lambda *d: fn(*d, *vals[n_diff:]), *vals[:n_diff])
        for o_ref, g in zip(refs[n_in + n_t:], vjp(tuple(t_vals)), strict=True):
            o_ref[...] = g

    return pl.pallas_call(body, name=name, out_shape=[jax.ShapeDtypeStruct(a.shape, F32) for a in ins[:n_diff]],
                          compiler_params=_params())(*ins, *cots)


CONV_ROWS = 256


def _conv_geometry(x, w, cw, off):
    width = w.shape[1]
    x = Cols(x, width, 0)
    length = x.arr.shape[0]
    taps = w.shape[0]
    pad = -(-(taps - 1) // 8) * 8
    assert off % cw == 0 and width % cw == 0, (off, width, cw)
    return x, length, taps, pad, off // cw


def _conv_taps(xp_ref, w_ref, base, taps, pad, init, lanes=slice(None)):
    acc = init
    for k in range(taps):
        acc = acc + w_ref[k:k + 1, lanes] * xp_ref[pl.ds(base + pad - (taps - 1) + k, init.shape[0]), :]
    return acc


def conv_fwd(x, w, b, *, act, name, off=0, cw=LANE, out_dtype=F32):
    x, length, taps, pad, jb = _conv_geometry(x, w, cw, off)
    rc = min(CONV_ROWS, length)

    def body(x_ref, w_ref, b_ref, o_ref, xp_ref):
        xp_ref[0:pad, :] = jnp.zeros((pad, cw), F32)
        xp_ref[pad:pad + length, :] = x_ref[...].astype(F32)

        def chunk(r, carry):
            base = pl.multiple_of(r * rc, rc)
            acc = _conv_taps(xp_ref, w_ref, base, taps, pad, jnp.broadcast_to(b_ref[...], (rc, cw)))
            if act:
                acc = _silu(acc)
            o_ref[pl.ds(base, rc), :] = acc.astype(o_ref.dtype)
            return carry

        lax.fori_loop(0, length // rc, chunk, 0)

    return pl.pallas_call(
        body, name=name, grid=(x.w // cw,),
        in_specs=[pl.BlockSpec((length, cw), lambda j: (0, jb + j)), pl.BlockSpec((taps, cw), lambda j: (0, j)),
                  pl.BlockSpec((1, cw), lambda j: (0, j))],
        out_specs=pl.BlockSpec((length, cw), lambda j: (0, j)),
        out_shape=jax.ShapeDtypeStruct((length, x.w), out_dtype),
        scratch_shapes=[pltpu.VMEM((pad + length, cw), F32)],
        compiler_params=_params(("parallel",)),
    )(x.arr, w, b)


def conv_bwd(x, w, b, dy, *, act, name, off=0, cw=LANE, dx_dtype=F32):
    x, length, taps, pad, jb = _conv_geometry(x, w, cw, off)
    rc = min(CONV_ROWS, length)

    def body(x_ref, w_ref, b_ref, dy_ref, dx_ref, dw_ref, db_ref, xp_ref, gp_ref):
        xp_ref[0:pad, :] = jnp.zeros((pad, cw), F32)
        xp_ref[pad:pad + length, :] = x_ref[...].astype(F32)
        gp_ref[length:length + pad, :] = jnp.zeros((pad, cw), F32)
        if act:
            def pre_chunk(r, carry):
                base = pl.multiple_of(r * rc, rc)
                pre = _conv_taps(xp_ref, w_ref, base, taps, pad, jnp.broadcast_to(b_ref[...], (rc, cw)))
                sig = jax.nn.sigmoid(pre)
                gp_ref[pl.ds(base, rc), :] = dy_ref[pl.ds(base, rc), :].astype(F32) * (sig * (1.0 + pre * (1.0 - sig)))
                return carry

            lax.fori_loop(0, length // rc, pre_chunk, 0)
        else:
            gp_ref[0:length, :] = dy_ref[...].astype(F32)
        dw_ref[...] = jnp.zeros((taps, cw), F32)
        db_ref[...] = jnp.zeros((1, cw), F32)

        def chunk(r, carry):
            base = pl.multiple_of(r * rc, rc)
            acc = jnp.zeros((rc, cw), F32)
            g = gp_ref[pl.ds(base, rc), :]
            for k in range(taps):
                acc = acc + w_ref[k:k + 1, :] * gp_ref[pl.ds(base + (taps - 1) - k, rc), :]
                xs = xp_ref[pl.ds(base + pad - (taps - 1) + k, rc), :]
                dw_ref[k:k + 1, :] += jnp.sum(g * xs, axis=0, keepdims=True)
            db_ref[...] += jnp.sum(g, axis=0, keepdims=True)
            dx_ref[pl.ds(base, rc), :] = acc.astype(dx_ref.dtype)
            return carry

        lax.fori_loop(0, length // rc, chunk, 0)

    dy = _cols(dy)
    assert dy.j == 0 and dy.w == x.w
    return pl.pallas_call(
        body, name=name, grid=(x.w // cw,),
        in_specs=[pl.BlockSpec((length, cw), lambda j: (0, jb + j)), pl.BlockSpec((taps, cw), lambda j: (0, j)),
                  pl.BlockSpec((1, cw), lambda j: (0, j)), pl.BlockSpec((length, cw), lambda j: (0, j))],
        out_specs=[pl.BlockSpec((length, cw), lambda j: (0, j)), pl.BlockSpec((taps, cw), lambda j: (0, j)),
                   pl.BlockSpec((1, cw), lambda j: (0, j))],
        out_shape=[jax.ShapeDtypeStruct((length, x.w), dx_dtype), jax.ShapeDtypeStruct((taps, x.w), F32),
                   jax.ShapeDtypeStruct((1, x.w), F32)],
        scratch_shapes=[pltpu.VMEM((pad + length, cw), F32), pltpu.VMEM((length + pad, cw), F32)],
        compiler_params=_params(("parallel",)),
    )(x.arr, w, b, dy.arr)


def _conv_transpose(xp_ref, gp_ref, w_ref, dx_ref, dw_ref, db_ref, lanes, length, taps, pad, rc):
    dw_ref[:, lanes] = jnp.zeros((taps, LANE), F32)
    db_ref[:, lanes] = jnp.zeros((1, LANE), F32)

    def chunk(r, carry):
        base = pl.multiple_of(r * rc, rc)
        acc = jnp.zeros((rc, LANE), F32)
        g = gp_ref[pl.ds(base, rc), :]
        for k in range(taps):
            acc = acc + w_ref[k:k + 1, lanes] * gp_ref[pl.ds(base + (taps - 1) - k, rc), :]
            xs = xp_ref[pl.ds(base + pad - (taps - 1) + k, rc), :]
            dw_ref[k:k + 1, lanes] += jnp.sum(g * xs, axis=0, keepdims=True)
        db_ref[:, lanes] += jnp.sum(g, axis=0, keepdims=True)
        dx_ref[pl.ds(base, rc), lanes] = acc.astype(dx_ref.dtype)
        return carry

    lax.fori_loop(0, length // rc, chunk, 0)


LANE_PAIR_ROWS = 1024


def ffn_interleave(a, name=None):
    rows, width = a.shape
    nb = width // (2 * LANE)
    if rows < LANE_PAIR_ROWS:
        return a.reshape(rows, 2, nb, LANE).swapaxes(1, 2).reshape(a.shape)

    def body(g_ref, u_ref, o_ref):
        o_ref[:, 0:LANE] = g_ref[...]
        o_ref[:, LANE:2 * LANE] = u_ref[...]

    tr = LANE_PAIR_ROWS
    return pl.pallas_call(
        body, name=name, grid=(rows // tr, nb),
        in_specs=[pl.BlockSpec((tr, LANE), lambda i, j: (i, j)), pl.BlockSpec((tr, LANE), lambda i, j: (i, nb + j))],
        out_specs=pl.BlockSpec((tr, 2 * LANE), lambda i, j: (i, j)),
        out_shape=jax.ShapeDtypeStruct(a.shape, a.dtype), compiler_params=_params(("parallel", "parallel")),
    )(a, a)


def ffn_deinterleave(a, name=None):
    rows, width = a.shape
    nb = width // (2 * LANE)
    if rows < LANE_PAIR_ROWS:
        return a.reshape(rows, nb, 2, LANE).swapaxes(1, 2).reshape(a.shape)

    def body(a_ref, o_ref):
        @pl.when(pl.program_id(1) == 0)
        def _():
            o_ref[...] = a_ref[:, 0:LANE]

        @pl.when(pl.program_id(1) == 1)
        def _():
            o_ref[...] = a_ref[:, LANE:2 * LANE]

    tr = LANE_PAIR_ROWS
    return pl.pallas_call(
        body, name=name, grid=(rows // tr, 2, nb),
        in_specs=[pl.BlockSpec((tr, 2 * LANE), lambda i, h, j: (i, j))],
        out_specs=pl.BlockSpec((tr, LANE), lambda i, h, j: (i, h * nb + j)),
        out_shape=jax.ShapeDtypeStruct(a.shape, a.dtype), compiler_params=_params(("parallel", "parallel", "parallel")),
    )(a)


GATE, UP = slice(0, LANE), slice(LANE, 2 * LANE)


def _ffn_geometry(a, w):
    length, width = a.shape
    taps = w.shape[0]
    return length, width, width // (2 * LANE), taps, -(-(taps - 1) // 8) * 8, min(CONV_ROWS, length)


def _ffn_pre(xg_ref, xu_ref, w_ref, b_ref, base, taps, pad, rc):
    gate = _conv_taps(xg_ref, w_ref, base, taps, pad, jnp.broadcast_to(b_ref[:, GATE], (rc, LANE)), GATE)
    up = _conv_taps(xu_ref, w_ref, base, taps, pad, jnp.broadcast_to(b_ref[:, UP], (rc, LANE)), UP)
    return gate, up


def ffn_conv_act(a, w, b, *, name):
    length, width, nb, taps, pad, rc = _ffn_geometry(a, w)

    def body(a_ref, w_ref, b_ref, o_ref, xg_ref, xu_ref):
        for xp_ref, lanes in ((xg_ref, GATE), (xu_ref, UP)):
            xp_ref[0:pad, :] = jnp.zeros((pad, LANE), F32)
            xp_ref[pad:pad + length, :] = a_ref[:, lanes]

        def chunk(r, carry):
            base = pl.multiple_of(r * rc, rc)
            gate, up = _ffn_pre(xg_ref, xu_ref, w_ref, b_ref, base, taps, pad, rc)
            o_ref[pl.ds(base, rc), :] = (_silu(gate) * up).astype(o_ref.dtype)
            return carry

        lax.fori_loop(0, length // rc, chunk, 0)

    pair = lambda rows: pl.BlockSpec((rows, 2 * LANE), lambda j: (0, j))
    return pl.pallas_call(
        body, name=name, grid=(nb,), in_specs=[pair(length), pair(taps), pair(1)],
        out_specs=pl.BlockSpec((length, LANE), lambda j: (0, j)),
        out_shape=jax.ShapeDtypeStruct((length, width // 2), BF16),
        scratch_shapes=[pltpu.VMEM((pad + length, LANE), F32), pltpu.VMEM((pad + length, LANE), F32)],
        compiler_params=_params(("parallel",)),
    )(a, w, b)


def ffn_conv_act_bwd(a, w, b, dact, *, name):
    length, width, nb, taps, pad, rc = _ffn_geometry(a, w)

    def body(a_ref, w_ref, b_ref, dy_ref, da_ref, dw_ref, db_ref, xg_ref, xu_ref, gg_ref, gu_ref):
        for xp_ref, lanes in ((xg_ref, GATE), (xu_ref, UP)):
            xp_ref[0:pad, :] = jnp.zeros((pad, LANE), F32)
            xp_ref[pad:pad + length, :] = a_ref[:, lanes]
        for gp_ref in (gg_ref, gu_ref):
            gp_ref[length:length + pad, :] = jnp.zeros((pad, LANE), F32)

        def pre_chunk(r, carry):
            base = pl.multiple_of(r * rc, rc)
            gate, up = _ffn_pre(xg_ref, xu_ref, w_ref, b_ref, base, taps, pad, rc)
            sig = jax.nn.sigmoid(gate)
            dy = dy_ref[pl.ds(base, rc), :]
            gg_ref[pl.ds(base, rc), :] = dy * up * (sig * (1.0 + gate * (1.0 - sig)))
            gu_ref[pl.ds(base, rc), :] = dy * (gate * sig)
            return carry

        lax.fori_loop(0, length // rc, pre_chunk, 0)
        _conv_transpose(xg_ref, gg_ref, w_ref, da_ref, dw_ref, db_ref, GATE, length, taps, pad, rc)
        _conv_transpose(xu_ref, gu_ref, w_ref, da_ref, dw_ref, db_ref, UP, length, taps, pad, rc)

    pair = lambda rows: pl.BlockSpec((rows, 2 * LANE), lambda j: (0, j))
    return pl.pallas_call(
        body, name=name, grid=(nb,),
        in_specs=[pair(length), pair(taps), pair(1), pl.BlockSpec((length, LANE), lambda j: (0, j))],
        out_specs=[pair(length), pair(taps), pair(1)],
        out_shape=[jax.ShapeDtypeStruct((length, width), BF16), jax.ShapeDtypeStruct((taps, width), F32),
                   jax.ShapeDtypeStruct((1, width), F32)],
        scratch_shapes=[pltpu.VMEM((pad + length, LANE), F32), pltpu.VMEM((pad + length, LANE), F32),
                        pltpu.VMEM((length + pad, LANE), F32), pltpu.VMEM((length + pad, LANE), F32)],
        compiler_params=_params(("parallel",)),
    )(a, w, b, dact)


def _retention_consts():
    h = np.arange(RET_HEADS, dtype=np.float32)
    log_g = np.log1p(-(2.0 ** (-5.0 - h))).astype(np.float32)
    idx = np.arange(CHUNK, dtype=np.float32)
    diff = idx[:, None] - idx[None, :]
    intra = np.where(diff[None] >= 0, np.exp(np.maximum(diff, 0.0)[None] * log_g[:, None, None]), 0.0)
    zeta = np.exp((CHUNK - 1 - idx)[None, :] * log_g[:, None])
    xi = np.exp((idx + 1)[None, :] * log_g[:, None])
    decay = np.exp(CHUNK * log_g)
    zeta = np.broadcast_to(zeta[:, :, None], (RET_HEADS, CHUNK, RET_DK))
    xi = np.broadcast_to(xi[:, :, None], (RET_HEADS, CHUNK, RET_DV))
    return (jnp.asarray(intra, F32), jnp.asarray(zeta, F32), jnp.asarray(xi, F32), [float(d) for d in decay])


def _rotary_tables(length):
    inv = ROPE_BASE ** (-jnp.arange(0, RET_DK, 2, dtype=F32) / RET_DK)
    ang = jnp.arange(length).astype(F32)[:, None] * inv[None, :]
    cos, sin = jnp.cos(ang), jnp.sin(ang)
    return jnp.concatenate([cos, cos], axis=1), jnp.concatenate([-sin, sin], axis=1)


def _rot(x, cos2, sin2):
    return x * cos2 + pltpu.roll(x, RET_DK // 2, 1) * sin2


def _rot_t(y, cos2, sin2):
    return y * cos2 + pltpu.roll(y * sin2, RET_DK // 2, 1)


def _head_decay(h, decays):
    d = jnp.float32(decays[-1])
    for i in range(len(decays) - 2, -1, -1):
        d = jnp.where(h == i, jnp.float32(decays[i]), d)
    return d


def _ret_chunk(q, k, v, g, state, intra, zeta, xi, decay):
    s = _mm_nt(q, k) * intra
    kv = _mm_tn(k * zeta, v)
    o = _mm(s, v) + _mm(q, state) * xi
    oc = o - jnp.mean(o, axis=-1, keepdims=True)
    r = oc * lax.rsqrt(jnp.mean(oc * oc, axis=-1, keepdims=True) + EPS)
    return _silu(g) * r, state * decay + kv


def _ret_specs(rev, nc):
    def cidx(c):
        return nc - 1 - c if rev else c
    return [
        pl.BlockSpec((CHUNK, RET_DK), lambda h, c: (cidx(c), h)),
        pl.BlockSpec((CHUNK, RET_DK), lambda h, c: (cidx(c), RET_HEADS + h)),
        pl.BlockSpec((CHUNK, RET_DV), lambda h, c: (cidx(c), 4 + h)),
        pl.BlockSpec((CHUNK, RET_DV), lambda h, c: (cidx(c), 8 + h)),
        pl.BlockSpec((CHUNK, RET_DK), lambda h, c: (cidx(c), 0)),
        pl.BlockSpec((CHUNK, RET_DK), lambda h, c: (cidx(c), 0)),
        pl.BlockSpec((1, CHUNK, CHUNK), lambda h, c: (h, 0, 0)),
        pl.BlockSpec((1, CHUNK, RET_DK), lambda h, c: (h, 0, 0)),
        pl.BlockSpec((1, CHUNK, RET_DV), lambda h, c: (h, 0, 0)),
    ], cidx


def retention_fwd(proj, cos2, sin2):
    length = proj.shape[0]
    nc = length // CHUNK
    intra, zeta, xi, decays = _retention_consts()
    specs, _ = _ret_specs(False, nc)
    scale = RET_DK ** -0.5

    def body(q_ref, k_ref, v_ref, g_ref, cos_ref, sin_ref, intra_ref, zeta_ref, xi_ref, y_ref, st_ref, state):
        h, c = pl.program_id(0), pl.program_id(1)

        @pl.when(c == 0)
        def _():
            state[...] = jnp.zeros_like(state)

        q = _rot(q_ref[...], cos_ref[...], sin_ref[...])
        k = _rot(k_ref[...], cos_ref[...], sin_ref[...]) * scale
        st_ref[0, 0] = state[...]
        y, new_state = _ret_chunk(q, k, v_ref[...], g_ref[...], state[...], intra_ref[0], zeta_ref[0], xi_ref[0],
                                  _head_decay(h, decays))
        y_ref[...] = y.astype(y_ref.dtype)
        state[...] = new_state

    return pl.pallas_call(
        body, name="retention_fwd", grid=(RET_HEADS, nc), in_specs=specs,
        out_specs=[pl.BlockSpec((CHUNK, RET_DV), lambda h, c: (c, h)),
                   pl.BlockSpec((1, 1, RET_DK, RET_DV), lambda h, c: (h, c, 0, 0))],
        out_shape=[jax.ShapeDtypeStruct((length, RET_HEADS * RET_DV), BF16),
                   jax.ShapeDtypeStruct((RET_HEADS, nc, RET_DK, RET_DV), F32)],
        scratch_shapes=[pltpu.VMEM((RET_DK, RET_DV), F32)],
        compiler_params=_params(("parallel", "arbitrary")),
    )(proj, proj, proj, proj, cos2, sin2, intra, zeta, xi)


def retention_bwd(proj, cos2, sin2, states, dmix):
    length = proj.shape[0]
    nc = length // CHUNK
    intra, zeta, xi, decays = _retention_consts()
    specs, cidx = _ret_specs(True, nc)
    scale = RET_DK ** -0.5

    def body(q_ref, k_ref, v_ref, g_ref, cos_ref, sin_ref, intra_ref, zeta_ref, xi_ref, st_ref, dy_ref,
             dq_ref, dk_ref, dv_ref, dg_ref, dstate):
        h, c = pl.program_id(0), pl.program_id(1)

        @pl.when(c == 0)
        def _():
            dstate[...] = jnp.zeros_like(dstate)

        cos2v, sin2v = cos_ref[...], sin_ref[...]
        q = _rot(q_ref[...], cos2v, sin2v)
        k = _rot(k_ref[...], cos2v, sin2v) * scale
        decay = _head_decay(h, decays)
        intra_v, zeta_v, xi_v = intra_ref[0], zeta_ref[0], xi_ref[0]
        _, vjp = jax.vjp(lambda q, k, v, g, s: _ret_chunk(q, k, v, g, s, intra_v, zeta_v, xi_v, decay),
                         q, k, v_ref[...], g_ref[...], st_ref[0, 0])
        dq, dk, dv, dg, ds = vjp((dy_ref[...].astype(F32), dstate[...]))
        dq_ref[...] = _rot_t(dq, cos2v, sin2v).astype(dq_ref.dtype)
        dk_ref[...] = _rot_t(dk * scale, cos2v, sin2v).astype(dk_ref.dtype)
        dv_ref[...] = dv.astype(dv_ref.dtype)
        dg_ref[...] = dg.astype(dg_ref.dtype)
        dstate[...] = ds

    specs = specs + [pl.BlockSpec((1, 1, RET_DK, RET_DV), lambda h, c: (h, cidx(c), 0, 0)),
                     pl.BlockSpec((CHUNK, RET_DV), lambda h, c: (cidx(c), h))]
    return pl.pallas_call(
        body, name="retention_bwd", grid=(RET_HEADS, nc), in_specs=specs,
        out_specs=[pl.BlockSpec((CHUNK, RET_DK), lambda h, c: (cidx(c), h)),
                   pl.BlockSpec((CHUNK, RET_DK), lambda h, c: (cidx(c), h)),
                   pl.BlockSpec((CHUNK, RET_DV), lambda h, c: (cidx(c), h)),
                   pl.BlockSpec((CHUNK, RET_DV), lambda h, c: (cidx(c), h))],
        out_shape=[jax.ShapeDtypeStruct((length, RET_HEADS * RET_DK), BF16),
                   jax.ShapeDtypeStruct((length, RET_HEADS * RET_DK), BF16),
                   jax.ShapeDtypeStruct((length, RET_HEADS * RET_DV), BF16),
                   jax.ShapeDtypeStruct((length, RET_HEADS * RET_DV), BF16)],
        scratch_shapes=[pltpu.VMEM((RET_DK, RET_DV), F32)],
        compiler_params=_params(("parallel", "arbitrary")),
    )(proj, proj, proj, proj, cos2, sin2, intra, zeta, xi, states, dmix)


def _ssd_consts():
    tri = np.tril(np.ones((CHUNK, CHUNK), np.float32))
    expand = np.zeros((LANE, SSM_DINNER), np.float32)
    for h in range(SSM_HEADS):
        expand[h, h * SSM_P:(h + 1) * SSM_P] = 1.0
    return jnp.asarray(tri), jnp.asarray(tri.T.copy()), jnp.asarray(expand)


def _ssd_chunk(xs, bm, cm, dtr, z, state, dt_bias, a_log, d_skip, norm_w, tri, tri_t, expand):
    gw = SSM_DINNER // SSM_GROUPS
    dt = jax.nn.softplus(dtr + dt_bias)
    da = dt * (-jnp.exp(a_log))
    acs = _dot_hi(tri, da)
    acs_t = _dot_hi_tn(da, tri_t)
    dt_x = _dot_hi(dt, expand)
    da_x = _dot_hi(da, expand)
    acs_x = _dot_hi(tri, da_x)
    tot_x = jnp.sum(da_x, axis=0, keepdims=True)
    x_dt = xs * dt_x
    x_dec = x_dt * jnp.exp(tot_x - acs_x)
    e_acs = jnp.exp(acs_x)
    e_tot = jnp.exp(tot_x)
    lane = lax.broadcasted_iota(jnp.int32, (CHUNK, LANE), 1)
    sub = lax.broadcasted_iota(jnp.int32, (CHUNK, LANE), 0)
    causal = sub >= lane
    ys, new_states = [], []
    for g in range(SSM_GROUPS):
        bg = bm[:, g * SSM_N:(g + 1) * SSM_N]
        cg = cm[:, g * SSM_N:(g + 1) * SSM_N]
        sg = state[:, g * gw:(g + 1) * gw]
        cb = _mm_nt(cg, bg)
        y_off = _mm(cg, sg) * e_acs[:, g * gw:(g + 1) * gw]
        new_states.append(sg * e_tot[:, g * gw:(g + 1) * gw] + _mm_tn(bg, x_dec[:, g * gw:(g + 1) * gw]))
        pairs = []
        for p in range(gw // LANE):
            hp = g * (gw // LANE) + p
            xp = x_dt[:, hp * LANE:(hp + 1) * LANE]
            halves = []
            for head in (2 * hp, 2 * hp + 1):
                col = jnp.sum(jnp.where(lane == head, acs, 0.0), axis=1, keepdims=True)
                row = jnp.sum(jnp.where(sub == head, acs_t, 0.0), axis=0, keepdims=True)
                decay = jnp.exp(jnp.where(causal, col - row, -1e30))
                halves.append(_mm(cb * decay, xp))
            pairs.append(jnp.where(lane < SSM_P, halves[0], halves[1]))
        ys.append(jnp.concatenate(pairs, axis=1) + y_off)
    d_x = jnp.mean(_dot_hi(jnp.broadcast_to(d_skip, (8, LANE)), expand), axis=0, keepdims=True)
    y = (jnp.concatenate(ys, axis=1) + d_x * xs) * _silu(z)
    normed = []
    for g in range(SSM_GROUPS):
        yg = y[:, g * gw:(g + 1) * gw]
        normed.append(yg * lax.rsqrt(jnp.mean(yg * yg, axis=-1, keepdims=True) + EPS))
    return jnp.concatenate(normed, axis=1) * norm_w, jnp.concatenate(new_states, axis=1)


XBC = SSM_DINNER + 2 * SSM_GROUPS * SSM_N


def _ssd_specs(rev, nc):
    def cidx(c):
        return nc - 1 - c if rev else c
    row = lambda w, j: pl.BlockSpec((CHUNK, w), lambda c: (cidx(c), j))
    whole = lambda shape: pl.BlockSpec(shape, lambda c: (0,) * len(shape))
    return [row(XBC, 0), row(LANE, 5632 // LANE), row(SSM_DINNER, 3),
            whole((1, LANE)), whole((1, LANE)), whole((1, LANE)), whole((1, SSM_DINNER)),
            whole((CHUNK, CHUNK)), whole((CHUNK, CHUNK)), whole((LANE, SSM_DINNER))], cidx


def ssd_fwd(xbc, proj, dt_bias, a_log, d_skip, norm_w):
    length = proj.shape[0]
    nc = length // CHUNK
    tri, tri_t, expand = _ssd_consts()
    specs, _ = _ssd_specs(False, nc)

    def body(xbc_ref, dt_ref, z_ref, dtb_ref, alog_ref, d_ref, nw_ref, tri_ref, trit_ref, e_ref, y_ref, st_ref, state):
        @pl.when(pl.program_id(0) == 0)
        def _():
            state[...] = jnp.zeros_like(state)

        st_ref[0] = state[...]
        y, new_state = _ssd_chunk(
            xbc_ref[:, 0:SSM_DINNER], xbc_ref[:, SSM_DINNER:SSM_DINNER + 256], xbc_ref[:, SSM_DINNER + 256:XBC],
            dt_ref[...], z_ref[...], state[...], dtb_ref[...], alog_ref[...], d_ref[...], nw_ref[...],
            tri_ref[...], trit_ref[...], e_ref[...])
        y_ref[...] = y.astype(y_ref.dtype)
        state[...] = new_state

    return pl.pallas_call(
        body, name="ssd_fwd", grid=(nc,), in_specs=specs,
        out_specs=[pl.BlockSpec((CHUNK, SSM_DINNER), lambda c: (c, 0)),
                   pl.BlockSpec((1, SSM_N, SSM_DINNER), lambda c: (c, 0, 0))],
        out_shape=[jax.ShapeDtypeStruct((length, SSM_DINNER), BF16),
                   jax.ShapeDtypeStruct((nc, SSM_N, SSM_DINNER), F32)],
        scratch_shapes=[pltpu.VMEM((SSM_N, SSM_DINNER), F32)],
        compiler_params=_params(("arbitrary",)),
    )(xbc, proj, proj, dt_bias, a_log, d_skip, norm_w, tri, tri_t, expand)


def ssd_bwd(xbc, proj, dt_bias, a_log, d_skip, norm_w, states, dmix):
    length = proj.shape[0]
    nc = length // CHUNK
    tri, tri_t, expand = _ssd_consts()
    specs, cidx = _ssd_specs(True, nc)

    def body(xbc_ref, dt_ref, z_ref, dtb_ref, alog_ref, d_ref, nw_ref, tri_ref, trit_ref, e_ref, st_ref, dy_ref,
             dxbc_ref, ddt_ref, dz_ref, ddtb_ref, dalog_ref, dd_ref, dnw_ref, dstate):
        c = pl.program_id(0)

        @pl.when(c == 0)
        def _():
            dstate[...] = jnp.zeros_like(dstate)

        tri_v, trit_v, e_v = tri_ref[...], trit_ref[...], e_ref[...]
        _, vjp = jax.vjp(
            lambda *a: _ssd_chunk(*a, tri_v, trit_v, e_v),
            xbc_ref[:, 0:SSM_DINNER], xbc_ref[:, SSM_DINNER:SSM_DINNER + 256], xbc_ref[:, SSM_DINNER + 256:XBC],
            dt_ref[...], z_ref[...], st_ref[0], dtb_ref[...], alog_ref[...], d_ref[...], nw_ref[...])
        dxs, dbm, dcm, ddt, dz, ds, ddtb, dalog, dd, dnw = vjp((dy_ref[...].astype(F32), dstate[...]))
        dxbc_ref[:, 0:SSM_DINNER] = dxs
        dxbc_ref[:, SSM_DINNER:SSM_DINNER + 256] = dbm
        dxbc_ref[:, SSM_DINNER + 256:XBC] = dcm
        ddt_ref[...] = ddt.astype(ddt_ref.dtype)
        dz_ref[...] = dz.astype(dz_ref.dtype)
        dstate[...] = ds
        for r, d in ((ddtb_ref, ddtb), (dalog_ref, dalog), (dd_ref, dd), (dnw_ref, dnw)):
            @pl.when(c == 0)
            def _(r=r, d=d):
                r[...] = d

            @pl.when(c > 0)
            def _(r=r, d=d):
                r[...] += d

    whole = lambda shape: pl.BlockSpec(shape, lambda c: (0,) * len(shape))
    specs = specs + [pl.BlockSpec((1, SSM_N, SSM_DINNER), lambda c: (cidx(c), 0, 0)),
                     pl.BlockSpec((CHUNK, SSM_DINNER), lambda c: (cidx(c), 1))]
    return pl.pallas_call(
        body, name="ssd_bwd", grid=(nc,), in_specs=specs,
        out_specs=[pl.BlockSpec((CHUNK, XBC), lambda c: (cidx(c), 0)), pl.BlockSpec((CHUNK, LANE), lambda c: (cidx(c), 0)),
                   pl.BlockSpec((CHUNK, SSM_DINNER), lambda c: (cidx(c), 0)),
                   whole((1, LANE)), whole((1, LANE)), whole((1, LANE)), whole((1, SSM_DINNER))],
        out_shape=[jax.ShapeDtypeStruct((length, XBC), F32), jax.ShapeDtypeStruct((length, LANE), BF16),
                   jax.ShapeDtypeStruct((length, SSM_DINNER), BF16),
                   jax.ShapeDtypeStruct((1, LANE), F32), jax.ShapeDtypeStruct((1, LANE), F32),
                   jax.ShapeDtypeStruct((1, LANE), F32), jax.ShapeDtypeStruct((1, SSM_DINNER), F32)],
        scratch_shapes=[pltpu.VMEM((SSM_N, SSM_DINNER), F32)],
        compiler_params=_params(("arbitrary",)),
    )(xbc, proj, proj, dt_bias, a_log, d_skip, norm_w, tri, tri_t, expand, states, dmix)


def _cmul(ar, ai, br, bi):
    return ar * br - ai * bi, ar * bi + ai * br


def s5_scan(b_re, b_im, a_re, a_im, *, reverse=False, states=None, name, lw=256):
    length, lanes = b_re.shape
    nk = length // SCAN_SEG
    with_da = states is not None
    assert reverse or not with_da

    def shift(v):
        sub = lax.broadcasted_iota(jnp.int32, v.shape, 0)
        if reverse:
            return jnp.where(sub == SCAN_SEG - 1, 0.0, pltpu.roll(v, SCAN_SEG - 1, 0))
        return jnp.where(sub == 0, 0.0, pltpu.roll(v, 1, 0))

    def body(*refs):
        if with_da:
            bre_ref, bim_ref, are_ref, aim_ref, sre_ref, sim_ref, xre_ref, xim_ref, dare_ref, daim_ref = refs
        else:
            bre_ref, bim_ref, are_ref, aim_ref, xre_ref, xim_ref = refs
        ar = jnp.broadcast_to(are_ref[...], (SCAN_SEG, lw))
        ai = jnp.broadcast_to(aim_ref[...], (SCAN_SEG, lw))

        def tile(i):
            k = (nk - 1 - i) if reverse else i
            return pl.ds(pl.multiple_of(k * SCAN_SEG, SCAN_SEG), SCAN_SEG)

        def local(i, carry):
            xr, xi, pr, pi = carry
            rows = tile(i)
            mr, mi = _cmul(ar, ai, xr, xi)
            xr, xi = mr + bre_ref[rows, :], mi + bim_ref[rows, :]
            xre_ref[rows, :] = xr
            xim_ref[rows, :] = xi
            pr, pi = _cmul(ar, ai, pr, pi)
            return xr, xi, pr, pi

        zero = jnp.zeros((SCAN_SEG, lw), F32)
        one = jnp.ones((SCAN_SEG, lw), F32)
        er, ei, pr, pi = lax.fori_loop(0, nk, local, (zero, zero, one, zero))
        cr, ci = zero, zero
        for _ in range(SCAN_SEG - 1):
            mr, mi = _cmul(pr, pi, cr, ci)
            cr, ci = shift(er + mr), shift(ei + mi)

        def fix(i, carry):
            pr, pi, dr, di = carry
            rows = tile(i)
            pr, pi = _cmul(ar, ai, pr, pi)
            mr, mi = _cmul(pr, pi, cr, ci)
            xr, xi = xre_ref[rows, :] + mr, xim_ref[rows, :] + mi
            xre_ref[rows, :] = xr
            xim_ref[rows, :] = xi
            if with_da:
                k = nk - 1 - i
                prev = pl.ds(pl.multiple_of(jnp.maximum(k - 1, 0) * SCAN_SEG, SCAN_SEG), SCAN_SEG)
                last = pl.ds((nk - 1) * SCAN_SEG, SCAN_SEG)
                sub = lax.broadcasted_iota(jnp.int32, (SCAN_SEG, lw), 0)
                wr = jnp.where(sub == 0, 0.0, pltpu.roll(sre_ref[last, :], 1, 0))
                wi = jnp.where(sub == 0, 0.0, pltpu.roll(sim_ref[last, :], 1, 0))
                sr = jnp.where(k == 0, wr, sre_ref[prev, :])
                si = jnp.where(k == 0, wi, sim_ref[prev, :])
                dr, di = dr + xr * sr + xi * si, di + xi * sr - xr * si
            return pr, pi, dr, di

        _, _, dr, di = lax.fori_loop(0, nk, fix, (one, zero, zero, zero))
        if with_da:
            dare_ref[...] = jnp.sum(dr, axis=0, keepdims=True)
            daim_ref[...] = jnp.sum(di, axis=0, keepdims=True)

    col = pl.BlockSpec((length, lw), lambda j: (0, j))
    vec = pl.BlockSpec((1, lw), lambda j: (0, j))
    ins = [b_re, b_im, a_re, a_im] + (list(states) if with_da else [])
    in_specs = [col, col, vec, vec] + ([col, col] if with_da else [])
    out_specs = [col, col] + ([vec, vec] if with_da else [])
    out_shape = [jax.ShapeDtypeStruct((length, lanes), F32)] * 2 + ([jax.ShapeDtypeStruct((1, lanes), F32)] * 2 if with_da else [])
    return pl.pallas_call(
        body, name=name, grid=(lanes // lw,), in_specs=in_specs, out_specs=out_specs, out_shape=out_shape,
        compiler_params=_params(("parallel",)),
    )(*ins)


def _seg_interleave(v):
    length = v.shape[0]
    return v.reshape(SCAN_SEG, length // SCAN_SEG, -1).transpose(1, 0, 2).reshape(length, -1)


def _seg_deinterleave(v):
    length = v.shape[0]
    return v.reshape(length // SCAN_SEG, SCAN_SEG, -1).transpose(1, 0, 2).reshape(length, -1)


def _block_diag(m):
    eye = jnp.eye(S5_GROUPS, dtype=m.dtype)
    return (m.reshape(S5_GROUPS, S5_GROUP, 1, S5_STATE) * eye[:, None, :, None]).reshape(S5_GROUPS * S5_GROUP, S5_LANES)


def _block_diag_take(full):
    idx = jnp.arange(S5_GROUPS)
    blocks = full.reshape(S5_GROUPS, S5_GROUP, S5_GROUPS, S5_STATE)[idx, :, idx, :]
    return blocks.reshape(S5_GROUPS * S5_GROUP, S5_STATE)


def _s5_prep(a_re, a_im, log_step, b_re, b_im, rep):
    step = jnp.exp(log_step)
    mag = jnp.exp(a_re * step)
    ab_re = mag * jnp.cos(a_im * step)
    ab_im = mag * jnp.sin(a_im * step)
    den = a_re * a_re + a_im * a_im
    f_re = ((ab_re - 1.0) * a_re + ab_im * a_im) / den
    f_im = (ab_im * a_re - (ab_re - 1.0) * a_im) / den
    fr, fi = _dot_hi(rep, f_re), _dot_hi(rep, f_im)
    return ab_re, ab_im, fr * b_re - fi * b_im, fr * b_im + fi * b_re


def _rms(x, g):
    return (x * lax.rsqrt(jnp.mean(x * x, axis=-1, keepdims=True) + EPS) * g,)


def _ffn_act(gate, up):
    return (_silu(gate) * up,)


def _glu(a, g):
    return (a * jax.nn.sigmoid(g),)


def _ln_silu(x, g, b):
    xc = x - jnp.mean(x, axis=-1, keepdims=True)
    var = jnp.mean(xc * xc, axis=-1, keepdims=True)
    return (_silu(xc * lax.rsqrt(var + EPS) * g + b),)


def _s5_post(y, u, d_skip, glu_w):
    s = jax.nn.gelu(y + d_skip * u)
    return (s * jax.nn.sigmoid(_mm(s, glu_w)),)


def loss_head(x, tgt, g, *, tl=512):
    length, d = x.shape
    tl = min(tl, length)

    def body(x_ref, t_ref, g_ref, loss_ref, dx_ref, dg_ref):
        i = pl.program_id(0)
        y, vjp = jax.vjp(lambda x, g: _rms(x, g)[0], x_ref[...], g_ref[...])
        err = y - t_ref[...]
        dx, dg = vjp(err * (1.0 / d))
        dx_ref[...] = dx
        part = jnp.broadcast_to(0.5 * jnp.sum(jnp.mean(err * err, axis=-1, keepdims=True), axis=0, keepdims=True), (1, LANE))

        @pl.when(i == 0)
        def _():
            loss_ref[...] = part
            dg_ref[...] = dg

        @pl.when(i > 0)
        def _():
            loss_ref[...] += part
            dg_ref[...] += dg

    row = pl.BlockSpec((tl, d), lambda i: (i, 0))
    return pl.pallas_call(
        body, name="loss_head", grid=(length // tl,),
        in_specs=[row, row, pl.BlockSpec((1, d), lambda i: (0, 0))],
        out_specs=[pl.BlockSpec((1, LANE), lambda i: (0, 0)), row, pl.BlockSpec((1, d), lambda i: (0, 0))],
        out_shape=[jax.ShapeDtypeStruct((1, LANE), F32), jax.ShapeDtypeStruct((length, d), F32),
                   jax.ShapeDtypeStruct((1, d), F32)],
        compiler_params=_params(("arbitrary",)),
    )(x, tgt, g)


def _pad_heads(v):
    return jnp.pad(v, ((0, 0), (0, LANE - v.shape[1])))


def local_step(x, tgt, w, late_weights=None, early_reduce=None):
    length = x.shape[0]
    cos2, sin2 = _rotary_tables(length)
    grads = {}
    w = dict(w)

    def rms_fwd(xin, g, name):
        return rowwise_fwd(_rms, [xin], [], [g], [], [(D_MODEL, BF16)], name=name, tl=512)[0]

    def rms_bwd(xin, g, dh, dxo, name):
        return rowwise_bwd(_rms, [xin], [], [g], [], [dh], [F32], name=name, tl=512, add=dxo)

    def ffn_fwd(i, xin):
        hf = rms_fwd(xin, w["ffn_norm"][i:i + 1], f"ffn{i}_norm")
        w_up = ffn_interleave(w["ffn_w_up"][i], name=f"ffn{i}_up_pairs")
        a = matmul(hf, w_up, name=f"ffn{i}_up")
        act = ffn_conv_act(a, ffn_interleave(w["ffn_dw_w"][i]), ffn_interleave(w["ffn_dw_b"][i:i + 1]),
                           name=f"ffn{i}_conv_act")
        return matmul(act, w["ffn_w_down"][i], res=xin, name=f"ffn{i}_down"), (hf, a, act, w_up)

    def ffn_bwd(i, xin, saved, dxo):
        hf, a, act, w_up = saved
        dact = matmul(dxo, w["ffn_w_down"][i], tb=True, name=f"ffn{i}_down_dx")
        dw_down = matmul(act, dxo, ta=True, name=f"ffn{i}_down_dw")
        da, ddw_w, ddw_b = ffn_conv_act_bwd(a, ffn_interleave(w["ffn_dw_w"][i]), ffn_interleave(w["ffn_dw_b"][i:i + 1]),
                                            dact, name=f"ffn{i}_conv_act_bwd")
        dw_up = ffn_deinterleave(matmul(hf, da, ta=True, name=f"ffn{i}_up_dw"), name=f"ffn{i}_up_dw_plain")
        dhf = matmul(da, w_up, tb=True, name=f"ffn{i}_up_dx")
        dxin, dnorm = rms_bwd(xin, w["ffn_norm"][i:i + 1], dhf, dxo, f"ffn{i}_norm_bwd")
        return dxin, dict(ffn_norm=dnorm, ffn_w_up=dw_up, ffn_dw_w=ffn_deinterleave(ddw_w),
                          ffn_dw_b=ffn_deinterleave(ddw_b), ffn_w_down=dw_down)

    w_in_e = jnp.pad(w["e_w_in"][0], ((0, 0), (0, EVEN_IN_PAD - EVEN_IN)))
    conv_w_e, conv_b_e = w["e_conv_w"][0], w["e_conv_b"]
    dt_bias, a_log, d_skip = _pad_heads(w["e_dt_bias"]), _pad_heads(w["e_a_log"]), _pad_heads(w["e_d"])
    xbc_off = 4 * D_MODEL

    hn0 = rms_fwd(x, w["mix_norm"][0:1], "mix0_norm")
    proj0 = matmul(hn0, w_in_e, name="even_in")
    y_ret, ret_states = retention_fwd(proj0, cos2, sin2)
    xbc = conv_fwd(proj0, conv_w_e, conv_b_e, act=True, off=xbc_off, name="ssd_conv")
    y_ssm, ssd_states = ssd_fwd(xbc, proj0, dt_bias, a_log, d_skip, w["e_ssm_norm"])
    mix0 = jnp.concatenate([y_ret, y_ssm], axis=1)
    if late_weights is not None:
        w.update(late_weights(y_ssm))
    w_out_e = w["e_w_out"][0]
    x1 = matmul(mix0, w_out_e, res=x, name="even_out")
    x2, ffn0_saved = ffn_fwd(0, x1)

    w_in_o, w_out_o, glu_w = w["o_w_in"][0], w["o_w_out"][0], w["o_glu_w"][0]
    dw_w_o, dw_b_o, ln_g, ln_b, d_o = w["o_dw_w"][0], w["o_dw_b"], w["o_ln_g"], w["o_ln_b"], w["o_d"]
    rep = jnp.asarray(np.repeat(np.eye(S5_GROUPS, dtype=np.float32), S5_GROUP, axis=0))
    rows_gc = (S5_GROUPS * S5_GROUP, S5_STATE)
    prep_in = [w["o_a_re"][0], w["o_a_im"][0], w["o_log_step"].reshape(S5_GROUPS, 1),
               w["o_b_re"][0].transpose(0, 2, 1).reshape(rows_gc), w["o_b_im"][0].transpose(0, 2, 1).reshape(rows_gc), rep]
    ab_re, ab_im, bb_re, bb_im = whole_fwd(
        _s5_prep, prep_in, [(S5_GROUPS, S5_STATE)] * 2 + [rows_gc] * 2, name="s5_prep")
    a_re_row, a_im_row = ab_re.reshape(1, S5_LANES), ab_im.reshape(1, S5_LANES)
    b_re_bd, b_im_bd = _block_diag(bb_re).astype(BF16), _block_diag(bb_im).astype(BF16)
    c_re_bd = _block_diag(w["o_c_re"][0].reshape(rows_gc)).astype(BF16)
    c_im_neg_bd = _block_diag(-w["o_c_im"][0].reshape(rows_gc)).astype(BF16)

    hn1 = rms_fwd(x2, w["mix_norm"][1:2], "mix1_norm")
    proj1 = matmul(hn1, w_in_o, name="odd_in")
    half = D_MODEL // 2
    c_glu = rowwise_fwd(_glu, [Cols(proj1, half, 0), Cols(proj1, half, 1)], [], [], [], [(half, F32)],
                        name="conf_glu", tl=512)[0]
    c_conv = conv_fwd(c_glu, dw_w_o, dw_b_o, act=False, name="conf_conv")
    c_out = rowwise_fwd(_ln_silu, [c_conv], [], [ln_g, ln_b], [], [(half, BF16)], name="conf_ln", tl=512)[0]
    u_seg = _seg_interleave(proj1[:, 2 * half:])
    bu_re = matmul(u_seg, b_re_bd, name="s5_bu_re")
    bu_im = matmul(u_seg, b_im_bd, name="s5_bu_im")
    xs_re, xs_im = s5_scan(bu_re, bu_im, a_re_row, a_im_row, name="s5_scan")
    y_im = matmul(xs_im, c_im_neg_bd, tb=True, name="s5_y_im")
    y_s5 = _seg_deinterleave(matmul(xs_re, c_re_bd, tb=True, res=y_im, name="s5_y_re"))
    s_out = rowwise_fwd(_s5_post, [y_s5, Cols(proj1, half, 2)], [], [d_o, glu_w], [], [(half, BF16)],
                        name="s5_post", tl=512)[0]
    mix1 = jnp.concatenate([c_out, s_out], axis=1)
    x3 = matmul(mix1, w_out_o, res=x2, name="odd_out")
    x4, ffn1_saved = ffn_fwd(1, x3)

    loss, dx4, dfinal = loss_head(x4, tgt, w["final_norm"].reshape(1, D_MODEL))
    grads["final_norm"] = dfinal.reshape(D_MODEL)

    dx3, g_ffn1 = ffn_bwd(1, x3, ffn1_saved, dx4)
    dmix1 = matmul(dx3, w_out_o, tb=True, name="odd_out_dx")
    grads["o_w_out"] = [matmul(mix1, dx3, ta=True, name="odd_out_dw")]
    dc_conv, dln_g, dln_b = rowwise_bwd(_ln_silu, [c_conv], [], [ln_g, ln_b], [], [Cols(dmix1, half, 0)], [F32],
                                        name="conf_ln_bwd", tl=512)
    dc_glu, ddw_w_o, ddw_b_o = conv_bwd(c_glu, dw_w_o, dw_b_o, dc_conv, act=False, name="conf_conv_bwd")
    d_cacg = rowwise_bwd(_glu, [Cols(proj1, half, 0), Cols(proj1, half, 1)], [], [], [], [dc_glu], [BF16],
                         name="conf_glu_bwd", tl=512, merge=True)[0]
    dy_s5, du_post, dd_o, dglu_w = rowwise_bwd(
        _s5_post, [y_s5, Cols(proj1, half, 2)], [], [d_o, glu_w], [], [Cols(dmix1, half, 1)], [F32, F32],
        name="s5_post_bwd", tl=512)
    dy_seg = _seg_interleave(dy_s5)
    dxs_re = matmul(dy_seg, c_re_bd, name="s5_dx_re")
    dxs_im = matmul(dy_seg, c_im_neg_bd, name="s5_dx_im")
    dc_re_bd = matmul(dy_seg, xs_re, ta=True, name="s5_dc_re")
    dc_im_neg_bd = matmul(dy_seg, xs_im, ta=True, name="s5_dc_im")
    g_re, g_im, dab_re, dab_im = s5_scan(dxs_re, dxs_im, a_re_row, -a_im_row, reverse=True, states=(xs_re, xs_im),
                                         name="s5_scan_bwd", lw=LANE)
    dbb_re = _block_diag_take(matmul(u_seg, g_re, ta=True, name="s5_db_re"))
    dbb_im = _block_diag_take(matmul(u_seg, g_im, ta=True, name="s5_db_im"))
    du_im = matmul(g_im, b_im_bd, tb=True, name="s5_du_im")
    du = _seg_deinterleave(matmul(g_re, b_re_bd, tb=True, res=du_im, name="s5_du_re")) + du_post
    da_re, da_im, dlog_step, db_re, db_im = whole_bwd(
        _s5_prep, prep_in, 5,
        [dab_re.reshape(S5_GROUPS, S5_STATE), dab_im.reshape(S5_GROUPS, S5_STATE), dbb_re, dbb_im], name="s5_prep_bwd")
    gcn = (S5_GROUPS, S5_GROUP, S5_STATE)
    grads.update(
        o_a_re=da_re[None], o_a_im=da_im[None], o_log_step=dlog_step.reshape(1, S5_GROUPS),
        o_b_re=db_re.reshape(gcn).transpose(0, 2, 1)[None], o_b_im=db_im.reshape(gcn).transpose(0, 2, 1)[None],
        o_c_re=_block_diag_take(dc_re_bd).reshape(gcn)[None], o_c_im=-_block_diag_take(dc_im_neg_bd).reshape(gcn)[None],
        o_d=dd_o, o_glu_w=[dglu_w], o_dw_w=ddw_w_o[None], o_dw_b=ddw_b_o, o_ln_g=dln_g, o_ln_b=dln_b)
    dproj1 = jnp.concatenate([d_cacg, du.astype(BF16)], axis=1)
    grads["o_w_in"] = [matmul(hn1, dproj1, ta=True, name="odd_in_dw")]
    dhn1 = matmul(dproj1, w_in_o, tb=True, name="odd_in_dx")
    dx2, dmix_norm1 = rms_bwd(x2, w["mix_norm"][1:2], dhn1, dx3, "mix1_norm_bwd")

    if early_reduce is not None:
        zero = early_reduce({("o_w_in", 0): grads["o_w_in"][0], ("o_glu_w", 0): grads["o_glu_w"][0],
                             ("o_w_out", 0): grads["o_w_out"][0], ("ffn_w_up", 1): g_ffn1["ffn_w_up"],
                             ("ffn_w_down", 1): g_ffn1["ffn_w_down"]})
        w["ffn_dw_b"] = w["ffn_dw_b"] + zero
    dx1, g_ffn0 = ffn_bwd(0, x1, ffn0_saved, dx2)
    if early_reduce is not None:
        dt_bias = dt_bias + early_reduce({("ffn_w_up", 0): g_ffn0["ffn_w_up"], ("ffn_w_down", 0): g_ffn0["ffn_w_down"]})
    for k in g_ffn0:
        per_layer = [g_ffn0[k], g_ffn1[k]]
        grads[k] = per_layer if k in ("ffn_w_up", "ffn_w_down") else jnp.stack(per_layer).reshape(w[k].shape)
    dmix0 = matmul(dx1, w_out_e, tb=True, name="even_out_dx")
    grads["e_w_out"] = [matmul(mix0, dx1, ta=True, name="even_out_dw")]
    dq, dk, dv, dg = retention_bwd(proj0, cos2, sin2, ret_states, dmix0)
    dxbc_c, ddt, dz, ddt_bias, da_log, dd_skip, dssm_norm = ssd_bwd(
        xbc, proj0, dt_bias, a_log, d_skip, w["e_ssm_norm"], ssd_states, dmix0)
    dxbc, dconv_w, dconv_b = conv_bwd(proj0, conv_w_e, conv_b_e, dxbc_c, act=True, off=xbc_off,
                                      name="ssd_conv_bwd", dx_dtype=BF16)
    dproj0 = jnp.concatenate([dq, dk, dv, dg, dz, dxbc, ddt], axis=1)
    grads["e_w_in"] = [matmul(hn0, dproj0, ta=True, name="even_in_dw")[:, :EVEN_IN]]
    dhn0 = matmul(dproj0, w_in_e, tb=True, name="even_in_dx")
    dx, dmix_norm0 = rms_bwd(x, w["mix_norm"][0:1], dhn0, dx1, "mix0_norm_bwd")
    grads.update(
        mix_norm=jnp.concatenate([dmix_norm0, dmix_norm1], axis=0), e_conv_w=dconv_w[None], e_conv_b=dconv_b,
        e_dt_bias=ddt_bias[:, :SSM_HEADS], e_a_log=da_log[:, :SSM_HEADS], e_d=dd_skip[:, :SSM_HEADS],
        e_ssm_norm=dssm_norm)
    return loss, dx, grads


def adamw(w, g, m, v, *, name):
    shape = w.shape
    cols = shape[-1]
    rows = w.size // cols
    tr = _tile(rows, max(8, (512 * 1024 // cols) // 8 * 8), unit=8)

    def body(w_ref, g_ref, m_ref, v_ref, d_ref, nm_ref, nv_ref):
        gv = g_ref[...]
        nm = ADAM_B1 * m_ref[...] + (1.0 - ADAM_B1) * gv
        nv = ADAM_B2 * v_ref[...] + (1.0 - ADAM_B2) * jnp.square(gv)
        m_hat = nm / (1.0 - ADAM_B1 ** ADAM_STEP)
        v_hat = nv / (1.0 - ADAM_B2 ** ADAM_STEP)
        d_ref[...] = -ADAM_LR * (m_hat / (jnp.sqrt(v_hat) + ADAM_EPS) + ADAM_WD * w_ref[...])
        nm_ref[...] = nm
        nv_ref[...] = nv

    spec = pl.BlockSpec((tr, cols), lambda i: (i, 0))
    outs = pl.pallas_call(
        body, name=name, grid=(rows // tr,), in_specs=[spec] * 4, out_specs=[spec] * 3,
        out_shape=[jax.ShapeDtypeStruct((rows, cols), F32)] * 3, compiler_params=_params(("parallel",)),
    )(*[t.reshape(rows, cols) for t in (w, g, m, v)])
    return [o.reshape(shape) for o in outs]


OTHER_CHIPS = ((1, 0), (0, 1), (1, 1))
ANY = pl.BlockSpec(memory_space=pl.ANY)


def _position():
    return lax.axis_index("x"), lax.axis_index("y"), lax.axis_index("c")


def _flip(v, f):
    return 1 - v if f else v


def _remote(src, dst, send_sem, recv_sem, device):
    return pltpu.make_async_remote_copy(src_ref=src, dst_ref=dst, send_sem=send_sem, recv_sem=recv_sem,
                                        device_id=device, device_id_type=MESH)


def gather_shards(big, small):
    n_big, n_small = len(big), len(small)
    halves = [a.shape[0] // 2 for a in big]

    def body(*refs):
        big_refs, small_refs = refs[:n_big], refs[n_big:n_big + n_small]
        obig_refs = refs[n_big + n_small:2 * n_big + n_small]
        osmall_refs = refs[2 * n_big + n_small:2 * (n_big + n_small)]
        ici_send, ici_recv, d2d_send, d2d_recv, small_send, small_recv = refs[2 * (n_big + n_small):]
        x, y, c = _position()
        mine = 2 * x + y

        def half(k, core):
            return pl.ds(pl.multiple_of(core * halves[k], 16), halves[k])

        sends = []
        for j, (fx, fy) in enumerate(OTHER_CHIPS):
            peer = (_flip(x, fx), _flip(y, fy), c)
            for k in range(n_big):
                sends.append(_remote(big_refs[k].at[half(k, c)], obig_refs[k].at[mine, half(k, c)],
                                     ici_send.at[j, k], ici_recv.at[j, k], peer))
            for k in range(n_small):
                sends.append(_remote(small_refs[k], osmall_refs[k].at[mine], small_send.at[j, k], small_recv.at[j, k], peer))
        for cp in sends:
            cp.start()
        for j, (fx, fy) in enumerate(OTHER_CHIPS):
            px, py = _flip(x, fx), _flip(y, fy)
            src_chip = 2 * px + py
            for k in range(n_big):
                landed = obig_refs[k].at[src_chip, half(k, c)]
                _remote(landed, landed, ici_send.at[j, k], ici_recv.at[j, k], (px, py, c)).wait_recv()
                fwd = _remote(landed, landed, d2d_send.at[j, k], d2d_recv.at[j, k], (x, y, 1 - c))
                fwd.start()
                sends.append(fwd)
        for j, (fx, fy) in enumerate(OTHER_CHIPS):
            px, py = _flip(x, fx), _flip(y, fy)
            src_chip = 2 * px + py
            for k in range(n_big):
                other = obig_refs[k].at[src_chip, half(k, 1 - c)]
                _remote(other, other, d2d_send.at[j, k], d2d_recv.at[j, k], (x, y, 1 - c)).wait_recv()
            for k in range(n_small):
                dst = osmall_refs[k].at[src_chip]
                _remote(small_refs[k], dst, small_send.at[j, k], small_recv.at[j, k], (px, py, c)).wait_recv()
        for cp in sends:
            cp.wait_send()

    arrays = list(big) + list(small)
    dma = pltpu.SemaphoreType.DMA
    return pl.pallas_call(
        body, name="gather_shards", in_specs=[ANY] * len(arrays), out_specs=[ANY] * len(arrays),
        out_shape=[jax.ShapeDtypeStruct((4,) + a.shape, a.dtype) for a in arrays],
        scratch_shapes=[dma((3, n_big)), dma((3, n_big)), dma((3, n_big)), dma((3, n_big)),
                        dma((3, n_small)), dma((3, n_small))],
        compiler_params=_params(),
    )(*arrays)


def allreduce_small(pack):
    rows = pack.shape[0]

    def body(p_ref, o_ref, slots, send_sems, recv_sems):
        x, y, c = _position()
        me = 4 * x + 2 * y + c
        slots[me] = p_ref[...]
        flips = [((k >> 2) & 1, (k >> 1) & 1, k & 1) for k in range(1, 8)]
        sends = []
        for k, (fx, fy, fc) in enumerate(flips):
            peer = (_flip(x, fx), _flip(y, fy), _flip(c, fc))
            sends.append(_remote(p_ref, slots.at[me], send_sems.at[k], recv_sems.at[k], peer))
        for cp in sends:
            cp.start()
        for k, (fx, fy, fc) in enumerate(flips):
            px, py, pc = _flip(x, fx), _flip(y, fy), _flip(c, fc)
            _remote(p_ref, slots.at[4 * px + 2 * py + pc], send_sems.at[k], recv_sems.at[k], (px, py, pc)).wait_recv()
        for cp in sends:
            cp.wait_send()
        acc = slots[0]
        for d in range(1, 8):
            acc = acc + slots[d]
        o_ref[...] = acc

    vmem = pl.BlockSpec(memory_space=pltpu.VMEM)
    return pl.pallas_call(
        body, name="allreduce_small", in_specs=[vmem], out_specs=vmem,
        out_shape=jax.ShapeDtypeStruct(pack.shape, F32),
        scratch_shapes=[pltpu.VMEM((8, rows, LANE), F32), pltpu.SemaphoreType.DMA((7,)), pltpu.SemaphoreType.DMA((7,))],
        compiler_params=_params(),
    )(pack)


def exchange_halves(gs, *, name):
    n = len(gs)

    def body(*refs):
        g_refs, o_refs, (send_sems, recv_sems) = refs[:n], refs[n:2 * n], refs[2 * n:]
        x, y, c = _position()
        copies = [_remote(g_refs[k].at[:, 1 - c], o_refs[k], send_sems.at[k], recv_sems.at[k], (x, y, 1 - c)) for k in range(n)]
        for cp in copies:
            cp.start()
        for cp in copies:
            cp.wait()

    return pl.pallas_call(
        body, name=name, in_specs=[ANY] * n, out_specs=[ANY] * n,
        out_shape=[jax.ShapeDtypeStruct((4,) + g.shape[2:], g.dtype) for g in gs],
        scratch_shapes=[pltpu.SemaphoreType.DMA((n,)), pltpu.SemaphoreType.DMA((n,))],
        compiler_params=_params(),
    )(*gs)


def scatter_to_chips(parts):
    n = len(parts)

    def body(*refs):
        a_refs, o_refs, (send_sems, recv_sems) = refs[:n], refs[n:2 * n], refs[2 * n:]
        x, y, c = _position()
        copies = []
        for j, (fx, fy) in enumerate(OTHER_CHIPS):
            px, py = _flip(x, fx), _flip(y, fy)
            for k in range(n):
                copies.append(_remote(a_refs[k].at[2 * px + py], o_refs[k].at[j], send_sems.at[j, k], recv_sems.at[j, k], (px, py, c)))
        for cp in copies:
            cp.start()
        for cp in copies:
            cp.wait()

    return pl.pallas_call(
        body, name="scatter_to_chips", in_specs=[ANY] * n, out_specs=[ANY] * n,
        out_shape=[jax.ShapeDtypeStruct((3,) + a.shape[1:], a.dtype) for a in parts],
        scratch_shapes=[pltpu.SemaphoreType.DMA((3, n)), pltpu.SemaphoreType.DMA((3, n))],
        compiler_params=_params(),
    )(*parts)


def swap_halves(rs):
    n = len(rs)

    def body(*refs):
        r_refs, o_refs, (send_sems, recv_sems) = refs[:n], refs[n:2 * n], refs[2 * n:]
        x, y, c = _position()
        copies = [_remote(r_refs[k], o_refs[k], send_sems.at[k], recv_sems.at[k], (x, y, 1 - c)) for k in range(n)]
        for cp in copies:
            cp.start()
        for cp in copies:
            cp.wait()

    dma = pltpu.SemaphoreType.DMA
    return pl.pallas_call(
        body, name="swap_halves", in_specs=[ANY] * n, out_specs=[ANY] * n,
        out_shape=[jax.ShapeDtypeStruct(r.shape, r.dtype) for r in rs],
        scratch_shapes=[dma((n,)), dma((n,))],
        compiler_params=_params(),
    )(*rs)


HBM = pl.BlockSpec(memory_space=pltpu.HBM)
SEM = pl.BlockSpec(memory_space=pltpu.SEMAPHORE)
SIDE_EFFECT = pltpu.SideEffectType.DATAFLOW_SIDE_EFFECTING


def _gather_plan(halves):
    def plan(v_refs, land_refs, x, y, c):
        copies = []
        for fx, fy in OTHER_CHIPS:
            for k in range(len(v_refs)):
                rows = pl.ds(pl.multiple_of(c * halves[k], 16), halves[k])
                copies.append((v_refs[k].at[rows], land_refs[k].at[2 * x + y, rows], (_flip(x, fx), _flip(y, fy), c)))
        return copies
    return plan


def _scatter_plan(v_refs, land_refs, x, y, c):
    copies = []
    for j, (fx, fy) in enumerate(OTHER_CHIPS):
        px, py = _flip(x, fx), _flip(y, fy)
        for k in range(len(v_refs)):
            copies.append((v_refs[k].at[2 * px + py], land_refs[k].at[j], (px, py, c)))
    return copies


def chip_exchange_start(srcs, land_shapes, plan, after, *, name):
    n = len(srcs)
    n_cp = 3 * n

    def body(*refs):
        v_refs, land_refs = refs[:n], refs[n:2 * n]
        outs = refs[2 * n + 1:]
        sends, recvs, token = outs[:n_cp], outs[n_cp:2 * n_cp], outs[-1]
        x, y, c = _position()
        for (src, dst, device), send, recv in zip(plan(v_refs, land_refs, x, y, c), sends, recvs, strict=True):
            _remote(src, dst, send, recv, device).start()
        token[...] = jnp.zeros_like(token)

    lands = [lax.empty(shape, v.dtype) for shape, v in zip(land_shapes, srcs)]
    arrays = [pltpu.with_memory_space_constraint(a, pltpu.HBM) for a in list(srcs) + lands]
    outs = pl.pallas_call(
        body, name=name,
        out_shape=tuple(pltpu.SemaphoreType.DMA(()) for _ in range(2 * n_cp))
        + tuple(pltpu.HBM(a.shape, a.dtype) for a in arrays) + (jax.ShapeDtypeStruct((8, LANE), F32),),
        in_specs=[HBM] * (2 * n) + [ANY],
        out_specs=(SEM,) * (2 * n_cp) + (HBM,) * (2 * n) + (pl.BlockSpec(memory_space=pltpu.VMEM),),
        input_output_aliases={i: 2 * n_cp + i for i in range(2 * n)},
        compiler_params=pltpu.CompilerParams(has_side_effects=SIDE_EFFECT),
    )(*arrays, after)
    handle = (outs[:n_cp], outs[n_cp:2 * n_cp], outs[2 * n_cp:2 * n_cp + n], outs[2 * n_cp + n:2 * n_cp + 2 * n])
    return handle, outs[-1]


def chip_exchange_wait(handle, plan, after, *, name):
    sends, recvs, v_thru, land_thru = handle
    n = len(v_thru)
    n_cp = 3 * n

    def body(*refs):
        v_refs, land_refs = refs[:n], refs[n:2 * n]
        sends, recvs = refs[2 * n:2 * n + n_cp], refs[2 * n + n_cp:2 * n + 2 * n_cp]
        x, y, c = _position()
        for (src, dst, device), send, recv in zip(plan(v_refs, land_refs, x, y, c), sends, recvs, strict=True):
            copy = _remote(src, dst, send, recv, device)
            copy.wait_send()
            copy.wait_recv()

    outs = pl.pallas_call(
        body, name=name,
        out_shape=tuple(pltpu.HBM(a.shape, a.dtype) for a in list(v_thru) + list(land_thru)),
        in_specs=[HBM] * (2 * n) + [SEM] * (2 * n_cp) + [ANY], out_specs=(HBM,) * (2 * n),
        input_output_aliases={i: i for i in range(2 * n)},
        compiler_params=pltpu.CompilerParams(has_side_effects=SIDE_EFFECT),
    )(*v_thru, *land_thru, *sends, *recvs, after)
    return outs[:n], outs[n:]


def finish_gather(lands):
    n = len(lands)
    halves = [a.shape[1] // 2 for a in lands]

    def body(*refs):
        o_refs, (send_sems, recv_sems) = refs[n:2 * n], refs[2 * n:]
        x, y, c = _position()

        def half(k, core):
            return pl.ds(pl.multiple_of(core * halves[k], 16), halves[k])

        sends = []
        for j, (fx, fy) in enumerate(OTHER_CHIPS):
            src_chip = 2 * _flip(x, fx) + _flip(y, fy)
            for k in range(n):
                held = o_refs[k].at[src_chip, half(k, c)]
                sends.append(_remote(held, held, send_sems.at[j, k], recv_sems.at[j, k], (x, y, 1 - c)))
        for cp in sends:
            cp.start()
        for j, (fx, fy) in enumerate(OTHER_CHIPS):
            src_chip = 2 * _flip(x, fx) + _flip(y, fy)
            for k in range(n):
                other = o_refs[k].at[src_chip, half(k, 1 - c)]
                _remote(other, other, send_sems.at[j, k], recv_sems.at[j, k], (x, y, 1 - c)).wait_recv()
        for cp in sends:
            cp.wait_send()

    dma = pltpu.SemaphoreType.DMA
    return pl.pallas_call(
        body, name="finish_gather", in_specs=[ANY] * n, out_specs=[ANY] * n,
        out_shape=[jax.ShapeDtypeStruct(a.shape, a.dtype) for a in lands],
        input_output_aliases={k: k for k in range(n)},
        scratch_shapes=[dma((3, n)), dma((3, n))],
        compiler_params=_params(),
    )(*lands)


def add_own_half(g, r, c_idx, *, name):
    _, _, h, cols = g.shape

    def body(c_ref, g_ref, r_ref, o_ref):
        o_ref[...] = (g_ref[0] + r_ref[...]).astype(o_ref.dtype)

    return pl.pallas_call(
        body, name=name,
        grid_spec=pltpu.PrefetchScalarGridSpec(
            num_scalar_prefetch=1, grid=(4,),
            in_specs=[pl.BlockSpec((1, 1, h, cols), lambda s, c: (s, c[0], 0, 0)),
                      pl.BlockSpec((1, h, cols), lambda s, c: (s, 0, 0))],
            out_specs=pl.BlockSpec((1, h, cols), lambda s, c: (s, 0, 0))),
        out_shape=jax.ShapeDtypeStruct(r.shape, BF16), compiler_params=_params(("parallel",)),
    )(c_idx, g, r)


def add_chip_parts(a, parts, chip_idx, *, name):
    _, h, cols = a.shape
    th = h // 2

    def body(s_ref, a_ref, p0_ref, p1_ref, p2_ref, o_ref):
        f = lambda r: r[0].astype(F32)
        o_ref[...] = ((f(a_ref) + f(p0_ref)) + f(p1_ref)) + f(p2_ref)

    part = lambda j: pl.BlockSpec((1, th, cols), lambda i, s, j=j: (j, i, 0))
    return pl.pallas_call(
        body, name=name,
        grid_spec=pltpu.PrefetchScalarGridSpec(
            num_scalar_prefetch=1, grid=(2,),
            in_specs=[pl.BlockSpec((1, th, cols), lambda i, s: (s[0], i, 0)), part(0), part(1), part(2)],
            out_specs=pl.BlockSpec((th, cols), lambda i, s: (i, 0))),
        out_shape=jax.ShapeDtypeStruct((h, cols), F32), compiler_params=_params(("parallel",)),
    )(chip_idx, a, parts, parts, parts)


WEIGHTS = ("mix_norm", "e_w_in", "e_conv_w", "e_conv_b", "e_dt_bias", "e_a_log", "e_d", "e_ssm_norm", "e_w_out",
           "o_w_in", "o_dw_w", "o_dw_b", "o_ln_g", "o_ln_b", "o_a_re", "o_a_im", "o_b_re", "o_b_im", "o_c_re",
           "o_c_im", "o_d", "o_log_step", "o_glu_w", "o_w_out", "ffn_norm", "ffn_w_up", "ffn_dw_w", "ffn_dw_b",
           "ffn_w_down", "final_norm")
BIG = (("e_w_in", 2), ("e_w_out", 1), ("o_w_in", 2), ("o_glu_w", 1), ("o_w_out", 1), ("ffn_w_up", 2), ("ffn_w_down", 1))
SMALL_SHARDED = (("e_conv_w", 2), ("o_dw_w", 2), ("o_dw_b", 1), ("o_ln_g", 1), ("o_ln_b", 1), ("o_d", 1), ("ffn_dw_w", 2))
REPLICATED = tuple(n for n in WEIGHTS if n not in dict(BIG + SMALL_SHARDED))
PACK_ROWS = 8


def _pack(arrays, dtype, row_unit=PACK_ROWS):
    flat = jnp.concatenate([a.astype(dtype).reshape(-1) for a in arrays])
    rows = -(-flat.size // (LANE * row_unit)) * row_unit
    return jnp.pad(flat, (0, rows * LANE - flat.size)).reshape(rows, LANE)


def _unpack(flat, shapes, lead=()):
    out, off = [], 0
    for shape in shapes:
        size = int(np.prod(shape))
        out.append(flat[..., off:off + size].reshape(lead + tuple(shape)))
        off += size
    return out


def _join_shards(parts, axis):
    return jnp.concatenate([parts[s] for s in range(4)], axis=axis)


def _split_shards(full, axis):
    return jnp.stack(jnp.split(full, 4, axis=axis))


def _rows2d(a):
    return a.reshape(-1, a.shape[-1])


def _layer_shards(g, axis):
    rows, cols = g.shape
    if axis == 0:
        return g.reshape(4, 2, rows // 8, cols)
    return g.reshape(rows, 4, cols // 4).transpose(1, 0, 2).reshape(4, 2, rows // 2, cols // 4)


def kernel(x, mix_norm, e_w_in, e_conv_w, e_conv_b, e_dt_bias, e_a_log, e_d, e_ssm_norm, e_w_out, o_w_in, o_dw_w, o_dw_b, o_ln_g, o_ln_b, o_a_re, o_a_im, o_b_re, o_b_im, o_c_re, o_c_im, o_d, o_log_step, o_glu_w, o_w_out, ffn_norm, ffn_w_up, ffn_dw_w, ffn_dw_b, ffn_w_down, final_norm, loss_target, m_mix_norm, m_e_w_in, m_e_conv_w, m_e_conv_b, m_e_dt_bias, m_e_a_log, m_e_d, m_e_ssm_norm, m_e_w_out, m_o_w_in, m_o_dw_w, m_o_dw_b, m_o_ln_g, m_o_ln_b, m_o_a_re, m_o_a_im, m_o_b_re, m_o_b_im, m_o_c_re, m_o_c_im, m_o_d, m_o_log_step, m_o_glu_w, m_o_w_out, m_ffn_norm, m_ffn_w_up, m_ffn_dw_w, m_ffn_dw_b, m_ffn_w_down, m_final_norm, v_mix_norm, v_e_w_in, v_e_conv_w, v_e_conv_b, v_e_dt_bias, v_e_a_log, v_e_d, v_e_ssm_norm, v_e_w_out, v_o_w_in, v_o_dw_w, v_o_dw_b, v_o_ln_g, v_o_ln_b, v_o_a_re, v_o_a_im, v_o_b_re, v_o_b_im, v_o_c_re, v_o_c_im, v_o_d, v_o_log_step, v_o_glu_w, v_o_w_out, v_ffn_norm, v_ffn_w_up, v_ffn_dw_w, v_ffn_dw_b, v_ffn_w_down, v_final_norm):
    given = dict(locals())
    chip = 2 * lax.axis_index("x") + lax.axis_index("y")
    core = lax.axis_index("c")

    core_idx, chip_idx = core.reshape(1).astype(jnp.int32), chip.reshape(1).astype(jnp.int32)

    def whole(n, axis, parts):
        shape = given[n].shape
        own = given[n].astype(parts.dtype)
        return _join_shards(lax.dynamic_update_index_in_dim(parts.reshape((4,) + shape), own, chip, 0), axis)

    first, later = BIG[:1], BIG[1:]
    shards = {n: _rows2d(given[n]).astype(BF16) for n, _ in BIG}
    gathered = gather_shards([shards[n] for n, _ in first], [_rows2d(given[n]) for n, _ in SMALL_SHARDED])
    w = {n: given[n] for n in REPLICATED}
    for (n, axis), parts in zip(first + SMALL_SHARDED, gathered):
        w[n] = whole(n, axis, parts)
    later_shards = [shards[n] for n, _ in later]
    gather_plan = _gather_plan([a.shape[0] // 2 for a in later_shards])
    gather_handle, token = chip_exchange_start(later_shards, [(4,) + a.shape for a in later_shards], gather_plan,
                                               gathered[0], name="gather_start")
    w["mix_norm"] = w["mix_norm"] + token[0, 0]

    def late_weights(after):
        _, lands = chip_exchange_wait(gather_handle, gather_plan, after, name="gather_wait")
        return {n: whole(n, axis, parts) for (n, axis), parts in zip(later, finish_gather(lands))}

    groups = []

    def finish_group(after):
        group = groups[-1]
        group["sums"], group["parts"] = chip_exchange_wait(group.pop("handle"), _scatter_plan, after,
                                                           name=f"scatter_wait_{len(groups) - 1}")

    def early_reduce(layer_grads):
        keys = list(layer_grads)
        if groups:
            finish_group(layer_grads[keys[0]])
        tag = len(groups)
        parts = [_layer_shards(layer_grads[k], dict(BIG)[k[0]] - 1) for k in keys]
        sums = [add_own_half(g, r, core_idx, name=f"add_own_half_{n}{layer}")
                for g, r, (n, layer) in zip(parts, exchange_halves(parts, name=f"exchange_halves_{tag}"), keys)]
        handle, zeros = chip_exchange_start(sums, [(3,) + a.shape[1:] for a in sums], _scatter_plan, sums[0],
                                            name=f"scatter_start_{tag}")
        groups.append(dict(keys=keys, handle=handle))
        return zeros[0, 0]

    loss, dx, grads = local_step(x[0], loss_target[0], w, late_weights, early_reduce)
    finish_group(dx)
    early_keys = [k for group in groups for k in group["keys"]]
    early_sums = [a for group in groups for a in group["sums"]]
    early_parts = [a for group in groups for a in group["parts"]]

    small_names = REPLICATED + tuple(n for n, _ in SMALL_SHARDED)
    small_sum = allreduce_small(_pack([grads[n] for n in small_names], F32))
    reduced = dict(zip(small_names, _unpack(small_sum.reshape(-1), [grads[n].shape for n in small_names])))
    for n, axis in SMALL_SHARDED:
        width = given[n].shape[axis]
        reduced[n] = lax.dynamic_slice_in_dim(reduced[n], chip * width, width, axis=axis)

    keys, parts = [], []
    for n, axis in BIG:
        for layer, g in enumerate(grads[n]):
            if (n, layer) not in early_keys:
                keys.append((n, layer))
                parts.append(_layer_shards(g, axis - 1))
    core_sums = [add_own_half(g, r, core_idx, name=f"add_own_half_{n}{layer}")
                 for g, r, (n, layer) in zip(parts, exchange_halves(parts, name="exchange_halves_last"), keys)]
    chip_parts = scatter_to_chips(core_sums)
    keys, core_sums, chip_parts = early_keys + keys, early_sums + core_sums, early_parts + list(chip_parts)
    mine = [add_chip_parts(a, p, chip_idx, name=f"add_chip_parts_{n}{layer}")
            for a, p, (n, layer) in zip(core_sums, chip_parts, keys)]
    layers = {}
    for (n, layer), own, other in zip(keys, mine, swap_halves(mine)):
        both = jnp.where(core == 0, jnp.stack([own, other]), jnp.stack([other, own]))
        layers.setdefault(n, {})[layer] = both.reshape(given[n].shape[1:])
    for n, _ in BIG:
        reduced[n] = jnp.stack([layers[n][layer] for layer in sorted(layers[n])])

    delta, new_m, new_v = {}, {}, {}
    for n, _ in BIG:
        delta[n], new_m[n], new_v[n] = adamw(given[n], reduced[n], given["m_" + n], given["v_" + n], name="adamw_" + n)
    shapes = [given[n].shape for n in small_names]
    packed = [_pack([src[n] for n in small_names], F32)
              for src in (given, reduced, {n: given["m_" + n] for n in small_names}, {n: given["v_" + n] for n in small_names})]
    for dst, res in zip((delta, new_m, new_v), adamw(*packed, name="adamw_small")):
        dst.update(zip(small_names, _unpack(res.reshape(-1), shapes)))

    total = lax.psum(loss[0, 0], ("x", "y", "c"))
    return (total, dx[None], *[reduced[n] for n in WEIGHTS], *[delta[n] for n in WEIGHTS],
            *[new_m[n] for n in WEIGHTS], *[new_v[n] for n in WEIGHTS])
```

```python
import functools
import math
from typing import NamedTuple

import numpy as np
import jax
import jax.numpy as jnp
from jax import lax
from jax.experimental import pallas as pl
from jax.experimental.pallas import tpu as pltpu

F32 = jnp.float32
BF16 = jnp.bfloat16
HIGHEST = lax.Precision.HIGHEST
MESH = pl.DeviceIdType.MESH

D_MODEL = 1024
EPS = 1e-6
RET_HEADS, RET_DK, RET_DV, CHUNK = 4, 128, 256, 128
ROPE_BASE = 10000.0
SSM_HEADS, SSM_P, SSM_N, SSM_GROUPS = 16, 64, 128, 2
SSM_DINNER = SSM_HEADS * SSM_P
EVEN_IN, EVEN_IN_PAD = 5648, 5760
S5_GROUPS, S5_GROUP, S5_STATE = 32, 16, 64
S5_LANES = S5_GROUPS * S5_STATE
SCAN_SEG = 32
D_FF = 2816
ADAM_LR, ADAM_B1, ADAM_B2, ADAM_EPS, ADAM_WD, ADAM_STEP = 0.001, 0.9, 0.999, 1e-08, 0.01, 10

LANE = 128
VMEM_LIMIT = 56 * 1024 * 1024


def _params(sem=None, **kw):
    return pltpu.CompilerParams(dimension_semantics=sem, vmem_limit_bytes=VMEM_LIMIT, **kw)


def _tile(n, target, unit=LANE):
    if n <= target:
        return n
    t = (target // unit) * unit
    while t >= unit:
        if n % t == 0:
            return t
        t -= unit
    return n


def _silu(x):
    return x * jax.nn.sigmoid(x)


def _mm(a, b):
    return jnp.dot(a.astype(BF16), b.astype(BF16), preferred_element_type=F32)


def _mm_nt(a, b):
    return lax.dot_general(a.astype(BF16), b.astype(BF16), (((1,), (1,)), ((), ())), preferred_element_type=F32)


def _mm_tn(a, b):
    return lax.dot_general(a.astype(BF16), b.astype(BF16), (((0,), (0,)), ((), ())), preferred_element_type=F32)


def _dot_hi(a, b):
    return jnp.dot(a, b, precision=HIGHEST, preferred_element_type=F32)


def _dot_hi_tn(a, b):
    return lax.dot_general(a, b, (((0,), (0,)), ((), ())), precision=HIGHEST, preferred_element_type=F32)


def _bf16_parts(v):
    hi = v.astype(BF16)
    rest = v - hi.astype(F32)
    mid = rest.astype(BF16)
    return hi, mid, (rest - mid.astype(F32)).astype(BF16)


def _dot_parts(v, fixed, dims, v_first):
    fixed = fixed.astype(BF16)
    out = None
    for part in _bf16_parts(v):
        ops = (part, fixed) if v_first else (fixed, part)
        p = lax.dot_general(*ops, (dims, ((), ())), preferred_element_type=F32)
        out = p if out is None else out + p
    return out


@jax.custom_vjp
def _times_01(v, ones):
    return _dot_parts(v, ones, ((1,), (0,)), True)


_times_01.defvjp(lambda v, ones: (_times_01(v, ones), ones),
                 lambda ones, g: (_dot_parts(g, ones, ((1,), (1,)), True), jnp.zeros_like(ones)))


@jax.custom_vjp
def _01_times(ones, v):
    return _dot_parts(v, ones, ((1,), (0,)), False)


_01_times.defvjp(lambda ones, v: (_01_times(ones, v), ones),
                 lambda ones, g: (jnp.zeros_like(ones), _dot_parts(g, ones, ((0,), (0,)), False)))


MATMUL_VMEM = 44 * 1024 * 1024


def matmul(a, b, *, ta=False, tb=False, res=None, out_dtype=F32, name):
    m, k = (a.shape[1], a.shape[0]) if ta else a.shape
    n = b.shape[0] if tb else b.shape[1]
    assert (b.shape[1] if tb else b.shape[0]) == k, (a.shape, b.shape, ta, tb)
    tm = _tile(m, 1536)
    tn = _tile(n, 640)
    if tn < 384:
        tn = _tile(n, 1536)
    res_bytes = 0 if res is None else res.dtype.itemsize

    def vmem(tm, tn):
        return 2 * (tm * k * a.dtype.itemsize + tn * k * b.dtype.itemsize + tm * tn * (jnp.dtype(out_dtype).itemsize + res_bytes))

    while vmem(tm, tn) > MATMUL_VMEM and tm % (2 * LANE) == 0:
        tm //= 2
    assert vmem(tm, tn) <= MATMUL_VMEM, (name, tm, tn, k)
    a_spec = pl.BlockSpec((k, tm), lambda i, j: (0, i)) if ta else pl.BlockSpec((tm, k), lambda i, j: (i, 0))
    b_spec = pl.BlockSpec((tn, k), lambda i, j: (j, 0)) if tb else pl.BlockSpec((k, tn), lambda i, j: (0, j))
    o_spec = pl.BlockSpec((tm, tn), lambda i, j: (i, j))
    dims = (((0 if ta else 1,), (1 if tb else 0,)), ((), ()))
    has_res = res is not None

    def body(a_ref, b_ref, *rest):
        o_ref = rest[-1]
        out = lax.dot_general(a_ref[...].astype(BF16), b_ref[...].astype(BF16), dims, preferred_element_type=F32)
        if has_res:
            out = out + rest[0][...].astype(F32)
        o_ref[...] = out.astype(o_ref.dtype)

    ins = [a, b] + ([res] if has_res else [])
    specs = [a_spec, b_spec] + ([o_spec] if has_res else [])
    return pl.pallas_call(
        body, name=name, grid=(m // tm, n // tn), in_specs=specs, out_specs=o_spec,
        out_shape=jax.ShapeDtypeStruct((m, n), out_dtype), compiler_params=_params(("parallel", "parallel")),
    )(*ins)


class Cols(NamedTuple):
    arr: jax.Array
    w: int
    j: int


def _cols(a):
    return a if isinstance(a, Cols) else Cols(a, a.shape[1], 0)


def _row_spec(c, tl):
    return pl.BlockSpec((tl, c.w), lambda i, j=c.j: (i, j))


def _whole_spec(p):
    return pl.BlockSpec(p.shape, lambda i, nd=p.ndim: (0,) * nd)


def rowwise_fwd(fn, rows, aux, pars, consts, outs, *, name, tl):
    rows = [_cols(r) for r in rows + aux]
    whole = list(pars) + list(consts)
    n_rows = len(rows)
    n_whole = len(whole)
    length = rows[0].arr.shape[0]
    tl = min(tl, length)

    def body(*refs):
        vals = [r[...].astype(F32) for r in refs[:n_rows]] + [r[...] for r in refs[n_rows:n_rows + n_whole]]
        res = fn(*vals)
        for o_ref, v in zip(refs[n_rows + n_whole:], res, strict=True):
            o_ref[...] = v.astype(o_ref.dtype)

    return pl.pallas_call(
        body, name=name, grid=(length // tl,),
        in_specs=[_row_spec(r, tl) for r in rows] + [_whole_spec(p) for p in whole],
        out_specs=[pl.BlockSpec((tl, w), lambda i: (i, 0)) for w, _ in outs],
        out_shape=[jax.ShapeDtypeStruct((length, w), dt) for w, dt in outs],
        compiler_params=_params(("parallel",)),
    )(*[r.arr for r in rows], *whole)


def rowwise_bwd(fn, rows, aux, pars, consts, cots, drow_dtypes, *, name, tl, add=None, merge=False):
    rows = [_cols(r) for r in rows]
    aux = [_cols(r) for r in aux]
    cots = [_cols(r) for r in cots]
    n_r, n_a, n_p, n_c, n_t = len(rows), len(aux), len(pars), len(consts), len(cots)
    length = rows[0].arr.shape[0]
    tl = min(tl, length)
    has_add = add is not None
    widths = [r.w for r in rows]

    def body(*refs):
        pos = 0
        r_vals = [r[...].astype(F32) for r in refs[pos:pos + n_r]]; pos += n_r
        a_vals = [r[...].astype(F32) for r in refs[pos:pos + n_a]]; pos += n_a
        p_vals = [r[...].astype(F32) for r in refs[pos:pos + n_p]]; pos += n_p
        c_vals = [r[...] for r in refs[pos:pos + n_c]]; pos += n_c
        t_vals = [r[...].astype(F32) for r in refs[pos:pos + n_t]]; pos += n_t
        add_val = None
        if has_add:
            add_val = refs[pos][...].astype(F32); pos += 1
        n_dr = 1 if merge else n_r
        dr_refs = refs[pos:pos + n_dr]; pos += n_dr
        dp_refs = refs[pos:pos + n_p]

        def f(*rp):
            return fn(*rp[:n_r], *a_vals, *rp[n_r:], *c_vals)

        _, vjp = jax.vjp(f, *r_vals, *p_vals)
        grads = vjp(tuple(t_vals))
        drows = list(grads[:n_r])
        if has_add:
            drows[0] = drows[0] + add_val
        if merge:
            off = 0
            for w, d in zip(widths, drows):
                dr_refs[0][:, off:off + w] = d.astype(dr_refs[0].dtype)
                off += w
        else:
            for r, d in zip(dr_refs, drows):
                r[...] = d.astype(r.dtype)
        i = pl.program_id(0)
        for r, d in zip(dp_refs, grads[n_r:]):
            @pl.when(i == 0)
            def _(r=r, d=d):
                r[...] = d

            @pl.when(i > 0)
            def _(r=r, d=d):
                r[...] += d

    if merge:
        dr_specs = [pl.BlockSpec((tl, sum(widths)), lambda i: (i, 0))]
        dr_shapes = [jax.ShapeDtypeStruct((length, sum(widths)), drow_dtypes[0])]
    else:
        dr_specs = [pl.BlockSpec((tl, w), lambda i: (i, 0)) for w in widths]
        dr_shapes = [jax.ShapeDtypeStruct((length, w), dt) for w, dt in zip(widths, drow_dtypes)]
    ins = [r.arr for r in rows + aux] + list(pars) + list(consts) + [r.arr for r in cots] + ([add] if has_add else [])
    specs = ([_row_spec(r, tl) for r in rows + aux] + [_whole_spec(p) for p in list(pars) + list(consts)]
             + [_row_spec(r, tl) for r in cots] + ([pl.BlockSpec((tl, add.shape[1]), lambda i: (i, 0))] if has_add else []))
    return pl.pallas_call(
        body, name=name, grid=(length // tl,), in_specs=specs,
        out_specs=dr_specs + [_whole_spec(p) for p in pars],
        out_shape=dr_shapes + [jax.ShapeDtypeStruct(p.shape, F32) for p in pars],
        compiler_params=_params(("arbitrary",)),
    )(*ins)


def whole_fwd(fn, ins, out_shapes, *, name):
    n_in = len(ins)

    def body(*refs):
        res = fn(*[r[...] for r in refs[:n_in]])
        for o_ref, v in zip(refs[n_in:], res, strict=True):
            o_ref[...] = v

    return pl.pallas_call(body, name=name, out_shape=[jax.ShapeDtypeStruct(s, F32) for s in out_shapes],
                          compiler_params=_params())(*ins)


def whole_bwd(fn, ins, n_diff, cots, *, name):
    n_in, n_t = len(ins), len(cots)

    def body(*refs):
        vals = [r[...] for r in refs[:n_in]]
        t_vals = [r[...] for r in refs[n_in:n_in + n_t]]
        _, vjp = jax.vjp(lambda *d: fn(*d, *vals[n_diff:]), *vals[:n_diff])
        for o_ref, g in zip(refs[n_in + n_t:], vjp(tuple(t_vals)), strict=True):
            o_ref[...] = g

    return pl.pallas_call(body, name=name, out_shape=[jax.ShapeDtypeStruct(a.shape, F32) for a in ins[:n_diff]],
                          compiler_params=_params())(*ins, *cots)


CONV_ROWS = 256


def _conv_geometry(x, w, cw, off):
    width = w.shape[1]
    x = Cols(x, width, 0)
    length = x.arr.shape[0]
    taps = w.shape[0]
    pad = -(-(taps - 1) // 8) * 8
    assert off % cw == 0 and width % cw == 0, (off, width, cw)
    return x, length, taps, pad, off // cw


def _conv_taps(xp_ref, w_ref, base, taps, pad, init, lanes=slice(None)):
    acc = init
    for k in range(taps):
        acc = acc + w_ref[k:k + 1, lanes] * xp_ref[pl.ds(base + pad - (taps - 1) + k, init.shape[0]), :]
    return acc


def conv_fwd(x, w, b, *, act, name, off=0, cw=LANE, out_dtype=F32):
    x, length, taps, pad, jb = _conv_geometry(x, w, cw, off)
    rc = min(CONV_ROWS, length)

    def body(x_ref, w_ref, b_ref, o_ref, xp_ref):
        xp_ref[0:pad, :] = jnp.zeros((pad, cw), F32)
        xp_ref[pad:pad + length, :] = x_ref[...].astype(F32)

        def chunk(r, carry):
            base = pl.multiple_of(r * rc, rc)
            acc = _conv_taps(xp_ref, w_ref, base, taps, pad, jnp.broadcast_to(b_ref[...], (rc, cw)))
            if act:
                acc = _silu(acc)
            o_ref[pl.ds(base, rc), :] = acc.astype(o_ref.dtype)
            return carry

        lax.fori_loop(0, length // rc, chunk, 0)

    return pl.pallas_call(
        body, name=name, grid=(x.w // cw,),
        in_specs=[pl.BlockSpec((length, cw), lambda j: (0, jb + j)), pl.BlockSpec((taps, cw), lambda j: (0, j)),
                  pl.BlockSpec((1, cw), lambda j: (0, j))],
        out_specs=pl.BlockSpec((length, cw), lambda j: (0, j)),
        out_shape=jax.ShapeDtypeStruct((length, x.w), out_dtype),
        scratch_shapes=[pltpu.VMEM((pad + length, cw), F32)],
        compiler_params=_params(("parallel",)),
    )(x.arr, w, b)


def conv_bwd(x, w, b, dy, *, act, name, off=0, cw=LANE, dx_dtype=F32):
    x, length, taps, pad, jb = _conv_geometry(x, w, cw, off)
    rc = min(CONV_ROWS, length)

    def body(x_ref, w_ref, b_ref, dy_ref, dx_ref, dw_ref, db_ref, xp_ref, gp_ref):
        xp_ref[0:pad, :] = jnp.zeros((pad, cw), F32)
        xp_ref[pad:pad + length, :] = x_ref[...].astype(F32)
        gp_ref[length:length + pad, :] = jnp.zeros((pad, cw), F32)
        if act:
            def pre_chunk(r, carry):
                base = pl.multiple_of(r * rc, rc)
                pre = _conv_taps(xp_ref, w_ref, base, taps, pad, jnp.broadcast_to(b_ref[...], (rc, cw)))
                sig = jax.nn.sigmoid(pre)
                gp_ref[pl.ds(base, rc), :] = dy_ref[pl.ds(base, rc), :].astype(F32) * (sig * (1.0 + pre * (1.0 - sig)))
                return carry

            lax.fori_loop(0, length // rc, pre_chunk, 0)
        else:
            gp_ref[0:length, :] = dy_ref[...].astype(F32)
        dw_ref[...] = jnp.zeros((taps, cw), F32)
        db_ref[...] = jnp.zeros((1, cw), F32)

        def chunk(r, carry):
            base = pl.multiple_of(r * rc, rc)
            acc = jnp.zeros((rc, cw), F32)
            g = gp_ref[pl.ds(base, rc), :]
            for k in range(taps):
                acc = acc + w_ref[k:k + 1, :] * gp_ref[pl.ds(base + (taps - 1) - k, rc), :]
                xs = xp_ref[pl.ds(base + pad - (taps - 1) + k, rc), :]
                dw_ref[k:k + 1, :] += jnp.sum(g * xs, axis=0, keepdims=True)
            db_ref[...] += jnp.sum(g, axis=0, keepdims=True)
            dx_ref[pl.ds(base, rc), :] = acc.astype(dx_ref.dtype)
            return carry

        lax.fori_loop(0, length // rc, chunk, 0)

    dy = _cols(dy)
    assert dy.j == 0 and dy.w == x.w
    return pl.pallas_call(
        body, name=name, grid=(x.w // cw,),
        in_specs=[pl.BlockSpec((length, cw), lambda j: (0, jb + j)), pl.BlockSpec((taps, cw), lambda j: (0, j)),
                  pl.BlockSpec((1, cw), lambda j: (0, j)), pl.BlockSpec((length, cw), lambda j: (0, j))],
        out_specs=[pl.BlockSpec((length, cw), lambda j: (0, j)), pl.BlockSpec((taps, cw), lambda j: (0, j)),
                   pl.BlockSpec((1, cw), lambda j: (0, j))],
        out_shape=[jax.ShapeDtypeStruct((length, x.w), dx_dtype), jax.ShapeDtypeStruct((taps, x.w), F32),
                   jax.ShapeDtypeStruct((1, x.w), F32)],
        scratch_shapes=[pltpu.VMEM((pad + length, cw), F32), pltpu.VMEM((length + pad, cw), F32)],
        compiler_params=_params(("parallel",)),
    )(x.arr, w, b, dy.arr)


def _conv_transpose(xp_ref, gp_ref, w_ref, dx_ref, dw_ref, db_ref, lanes, length, taps, pad, rc):
    dw_ref[:, lanes] = jnp.zeros((taps, LANE), F32)
    db_ref[:, lanes] = jnp.zeros((1, LANE), F32)

    def chunk(r, carry):
        base = pl.multiple_of(r * rc, rc)
        acc = jnp.zeros((rc, LANE), F32)
        g = gp_ref[pl.ds(base, rc), :]
        for k in range(taps):
            acc = acc + w_ref[k:k + 1, lanes] * gp_ref[pl.ds(base + (taps - 1) - k, rc), :]
            xs = xp_ref[pl.ds(base + pad - (taps - 1) + k, rc), :]
            dw_ref[k:k + 1, lanes] += jnp.sum(g * xs, axis=0, keepdims=True)
        db_ref[:, lanes] += jnp.sum(g, axis=0, keepdims=True)
        dx_ref[pl.ds(base, rc), lanes] = acc.astype(dx_ref.dtype)
        return carry

    lax.fori_loop(0, length // rc, chunk, 0)


LANE_PAIR_ROWS = 1024


def ffn_interleave(a, name=None):
    rows, width = a.shape
    nb = width // (2 * LANE)
    if rows < LANE_PAIR_ROWS:
        return a.reshape(rows, 2, nb, LANE).swapaxes(1, 2).reshape(a.shape)

    def body(g_ref, u_ref, o_ref):
        o_ref[:, 0:LANE] = g_ref[...]
        o_ref[:, LANE:2 * LANE] = u_ref[...]

    tr = LANE_PAIR_ROWS
    return pl.pallas_call(
        body, name=name, grid=(rows // tr, nb),
        in_specs=[pl.BlockSpec((tr, LANE), lambda i, j: (i, j)), pl.BlockSpec((tr, LANE), lambda i, j: (i, nb + j))],
        out_specs=pl.BlockSpec((tr, 2 * LANE), lambda i, j: (i, j)),
        out_shape=jax.ShapeDtypeStruct(a.shape, a.dtype), compiler_params=_params(("parallel", "parallel")),
    )(a, a)


def ffn_deinterleave(a, name=None):
    rows, width = a.shape
    nb = width // (2 * LANE)
    if rows < LANE_PAIR_ROWS:
        return a.reshape(rows, nb, 2, LANE).swapaxes(1, 2).reshape(a.shape)

    def body(a_ref, o_ref):
        @pl.when(pl.program_id(1) == 0)
        def _():
            o_ref[...] = a_ref[:, 0:LANE]

        @pl.when(pl.program_id(1) == 1)
        def _():
            o_ref[...] = a_ref[:, LANE:2 * LANE]

    tr = LANE_PAIR_ROWS
    return pl.pallas_call(
        body, name=name, grid=(rows // tr, 2, nb),
        in_specs=[pl.BlockSpec((tr, 2 * LANE), lambda i, h, j: (i, j))],
        out_specs=pl.BlockSpec((tr, LANE), lambda i, h, j: (i, h * nb + j)),
        out_shape=jax.ShapeDtypeStruct(a.shape, a.dtype), compiler_params=_params(("parallel", "parallel", "parallel")),
    )(a)


GATE, UP = slice(0, LANE), slice(LANE, 2 * LANE)


def _ffn_geometry(a, w):
    length, width = a.shape
    taps = w.shape[0]
    return length, width, width // (2 * LANE), taps, -(-(taps - 1) // 8) * 8, min(CONV_ROWS, length)


def _ffn_pre(xg_ref, xu_ref, w_ref, b_ref, base, taps, pad, rc):
    gate = _conv_taps(xg_ref, w_ref, base, taps, pad, jnp.broadcast_to(b_ref[:, GATE], (rc, LANE)), GATE)
    up = _conv_taps(xu_ref, w_ref, base, taps, pad, jnp.broadcast_to(b_ref[:, UP], (rc, LANE)), UP)
    return gate, up


def ffn_conv_act(a, w, b, *, name):
    length, width, nb, taps, pad, rc = _ffn_geometry(a, w)

    def body(a_ref, w_ref, b_ref, o_ref, xg_ref, xu_ref):
        for xp_ref, lanes in ((xg_ref, GATE), (xu_ref, UP)):
            xp_ref[0:pad, :] = jnp.zeros((pad, LANE), F32)
            xp_ref[pad:pad + length, :] = a_ref[:, lanes]

        def chunk(r, carry):
            base = pl.multiple_of(r * rc, rc)
            gate, up = _ffn_pre(xg_ref, xu_ref, w_ref, b_ref, base, taps, pad, rc)
            o_ref[pl.ds(base, rc), :] = (_silu(gate) * up).astype(o_ref.dtype)
            return carry

        lax.fori_loop(0, length // rc, chunk, 0)

    pair = lambda rows: pl.BlockSpec((rows, 2 * LANE), lambda j: (0, j))
    return pl.pallas_call(
        body, name=name, grid=(nb,), in_specs=[pair(length), pair(taps), pair(1)],
        out_specs=pl.BlockSpec((length, LANE), lambda j: (0, j)),
        out_shape=jax.ShapeDtypeStruct((length, width // 2), BF16),
        scratch_shapes=[pltpu.VMEM((pad + length, LANE), F32), pltpu.VMEM((pad + length, LANE), F32)],
        compiler_params=_params(("parallel",)),
    )(a, w, b)


def ffn_conv_act_bwd(a, w, b, dact, *, name):
    length, width, nb, taps, pad, rc = _ffn_geometry(a, w)

    def body(a_ref, w_ref, b_ref, dy_ref, da_ref, dw_ref, db_ref, xg_ref, xu_ref, gg_ref, gu_ref):
        for xp_ref, lanes in ((xg_ref, GATE), (xu_ref, UP)):
            xp_ref[0:pad, :] = jnp.zeros((pad, LANE), F32)
            xp_ref[pad:pad + length, :] = a_ref[:, lanes]
        for gp_ref in (gg_ref, gu_ref):
            gp_ref[length:length + pad, :] = jnp.zeros((pad, LANE), F32)

        def pre_chunk(r, carry):
            base = pl.multiple_of(r * rc, rc)
            gate, up = _ffn_pre(xg_ref, xu_ref, w_ref, b_ref, base, taps, pad, rc)
            sig = jax.nn.sigmoid(gate)
            dy = dy_ref[pl.ds(base, rc), :]
            gg_ref[pl.ds(base, rc), :] = dy * up * (sig * (1.0 + gate * (1.0 - sig)))
            gu_ref[pl.ds(base, rc), :] = dy * (gate * sig)
            return carry

        lax.fori_loop(0, length // rc, pre_chunk, 0)
        _conv_transpose(xg_ref, gg_ref, w_ref, da_ref, dw_ref, db_ref, GATE, length, taps, pad, rc)
        _conv_transpose(xu_ref, gu_ref, w_ref, da_ref, dw_ref, db_ref, UP, length, taps, pad, rc)

    pair = lambda rows: pl.BlockSpec((rows, 2 * LANE), lambda j: (0, j))
    return pl.pallas_call(
        body, name=name, grid=(nb,),
        in_specs=[pair(length), pair(taps), pair(1), pl.BlockSpec((length, LANE), lambda j: (0, j))],
        out_specs=[pair(length), pair(taps), pair(1)],
        out_shape=[jax.ShapeDtypeStruct((length, width), BF16), jax.ShapeDtypeStruct((taps, width), F32),
                   jax.ShapeDtypeStruct((1, width), F32)],
        scratch_shapes=[pltpu.VMEM((pad + length, LANE), F32), pltpu.VMEM((pad + length, LANE), F32),
                        pltpu.VMEM((length + pad, LANE), F32), pltpu.VMEM((length + pad, LANE), F32)],
        compiler_params=_params(("parallel",)),
    )(a, w, b, dact)


def _retention_consts():
    h = np.arange(RET_HEADS, dtype=np.float32)
    log_g = np.log1p(-(2.0 ** (-5.0 - h))).astype(np.float32)
    idx = np.arange(CHUNK, dtype=np.float32)
    diff = idx[:, None] - idx[None, :]
    intra = np.where(diff[None] >= 0, np.exp(np.maximum(diff, 0.0)[None] * log_g[:, None, None]), 0.0)
    zeta = np.exp((CHUNK - 1 - idx)[None, :] * log_g[:, None])
    xi = np.exp((idx + 1)[None, :] * log_g[:, None])
    decay = np.exp(CHUNK * log_g)
    zeta = np.broadcast_to(zeta[:, :, None], (RET_HEADS, CHUNK, RET_DK))
    xi = np.broadcast_to(xi[:, :, None], (RET_HEADS, CHUNK, RET_DV))
    return (jnp.asarray(intra, F32), jnp.asarray(zeta, F32), jnp.asarray(xi, F32), [float(d) for d in decay])


def _rotary_tables(length):
    inv = ROPE_BASE ** (-jnp.arange(0, RET_DK, 2, dtype=F32) / RET_DK)
    ang = jnp.arange(length).astype(F32)[:, None] * inv[None, :]
    cos, sin = jnp.cos(ang), jnp.sin(ang)
    return jnp.concatenate([cos, cos], axis=1), jnp.concatenate([-sin, sin], axis=1)


def _rot(x, cos2, sin2):
    return x * cos2 + pltpu.roll(x, RET_DK // 2, 1) * sin2


def _rot_t(y, cos2, sin2):
    return y * cos2 + pltpu.roll(y * sin2, RET_DK // 2, 1)


def _head_decay(h, decays):
    d = jnp.float32(decays[-1])
    for i in range(len(decays) - 2, -1, -1):
        d = jnp.where(h == i, jnp.float32(decays[i]), d)
    return d


def _ret_chunk(q, k, v, g, state, intra, zeta, xi, decay):
    s = _mm_nt(q, k) * intra
    kv = _mm_tn(k * zeta, v)
    o = _mm(s, v) + _mm(q, state) * xi
    oc = o - jnp.mean(o, axis=-1, keepdims=True)
    r = oc * lax.rsqrt(jnp.mean(oc * oc, axis=-1, keepdims=True) + EPS)
    return _silu(g) * r, state * decay + kv


def _ret_specs(rev, nc):
    def cidx(c):
        return nc - 1 - c if rev else c
    return [
        pl.BlockSpec((CHUNK, RET_DK), lambda h, c: (cidx(c), h)),
        pl.BlockSpec((CHUNK, RET_DK), lambda h, c: (cidx(c), RET_HEADS + h)),
        pl.BlockSpec((CHUNK, RET_DV), lambda h, c: (cidx(c), 4 + h)),
        pl.BlockSpec((CHUNK, RET_DV), lambda h, c: (cidx(c), 8 + h)),
        pl.BlockSpec((CHUNK, RET_DK), lambda h, c: (cidx(c), 0)),
        pl.BlockSpec((CHUNK, RET_DK), lambda h, c: (cidx(c), 0)),
        pl.BlockSpec((1, CHUNK, CHUNK), lambda h, c: (h, 0, 0)),
        pl.BlockSpec((1, CHUNK, RET_DK), lambda h, c: (h, 0, 0)),
        pl.BlockSpec((1, CHUNK, RET_DV), lambda h, c: (h, 0, 0)),
    ], cidx


def retention_fwd(proj, cos2, sin2):
    length = proj.shape[0]
    nc = length // CHUNK
    intra, zeta, xi, decays = _retention_consts()
    specs, _ = _ret_specs(False, nc)
    scale = RET_DK ** -0.5

    def body(q_ref, k_ref, v_ref, g_ref, cos_ref, sin_ref, intra_ref, zeta_ref, xi_ref, y_ref, st_ref, state):
        h, c = pl.program_id(0), pl.program_id(1)

        @pl.when(c == 0)
        def _():
            state[...] = jnp.zeros_like(state)

        q = _rot(q_ref[...], cos_ref[...], sin_ref[...])
        k = _rot(k_ref[...], cos_ref[...], sin_ref[...]) * scale
        st_ref[0, 0] = state[...]
        y, new_state = _ret_chunk(q, k, v_ref[...], g_ref[...], state[...], intra_ref[0], zeta_ref[0], xi_ref[0],
                                  _head_decay(h, decays))
        y_ref[...] = y.astype(y_ref.dtype)
        state[...] = new_state

    return pl.pallas_call(
        body, name="retention_fwd", grid=(RET_HEADS, nc), in_specs=specs,
        out_specs=[pl.BlockSpec((CHUNK, RET_DV), lambda h, c: (c, h)),
                   pl.BlockSpec((1, 1, RET_DK, RET_DV), lambda h, c: (h, c, 0, 0))],
        out_shape=[jax.ShapeDtypeStruct((length, RET_HEADS * RET_DV), BF16),
                   jax.ShapeDtypeStruct((RET_HEADS, nc, RET_DK, RET_DV), F32)],
        scratch_shapes=[pltpu.VMEM((RET_DK, RET_DV), F32)],
        compiler_params=_params(("parallel", "arbitrary")),
    )(proj, proj, proj, proj, cos2, sin2, intra, zeta, xi)


def retention_bwd(proj, cos2, sin2, states, dmix):
    length = proj.shape[0]
    nc = length // CHUNK
    intra, zeta, xi, decays = _retention_consts()
    specs, cidx = _ret_specs(True, nc)
    scale = RET_DK ** -0.5

    def body(q_ref, k_ref, v_ref, g_ref, cos_ref, sin_ref, intra_ref, zeta_ref, xi_ref, st_ref, dy_ref,
             dq_ref, dk_ref, dv_ref, dg_ref, dstate):
        h, c = pl.program_id(0), pl.program_id(1)

        @pl.when(c == 0)
        def _():
            dstate[...] = jnp.zeros_like(dstate)

        cos2v, sin2v = cos_ref[...], sin_ref[...]
        q = _rot(q_ref[...], cos2v, sin2v)
        k = _rot(k_ref[...], cos2v, sin2v) * scale
        decay = _head_decay(h, decays)
        intra_v, zeta_v, xi_v = intra_ref[0], zeta_ref[0], xi_ref[0]
        _, vjp = jax.vjp(lambda q, k, v, g, s: _ret_chunk(q, k, v, g, s, intra_v, zeta_v, xi_v, decay),
                         q, k, v_ref[...], g_ref[...], st_ref[0, 0])
        dq, dk, dv, dg, ds = vjp((dy_ref[...].astype(F32), dstate[...]))
        dq_ref[...] = _rot_t(dq, cos2v, sin2v).astype(dq_ref.dtype)
        dk_ref[...] = _rot_t(dk * scale, cos2v, sin2v).astype(dk_ref.dtype)
        dv_ref[...] = dv.astype(dv_ref.dtype)
        dg_ref[...] = dg.astype(dg_ref.dtype)
        dstate[...] = ds

    specs = specs + [pl.BlockSpec((1, 1, RET_DK, RET_DV), lambda h, c: (h, cidx(c), 0, 0)),
                     pl.BlockSpec((CHUNK, RET_DV), lambda h, c: (cidx(c), h))]
    return pl.pallas_call(
        body, name="retention_bwd", grid=(RET_HEADS, nc), in_specs=specs,
        out_specs=[pl.BlockSpec((CHUNK, RET_DK), lambda h, c: (cidx(c), h)),
                   pl.BlockSpec((CHUNK, RET_DK), lambda h, c: (cidx(c), h)),
                   pl.BlockSpec((CHUNK, RET_DV), lambda h, c: (cidx(c), h)),
                   pl.BlockSpec((CHUNK, RET_DV), lambda h, c: (cidx(c), h))],
        out_shape=[jax.ShapeDtypeStruct((length, RET_HEADS * RET_DK), BF16),
                   jax.ShapeDtypeStruct((length, RET_HEADS * RET_DK), BF16),
                   jax.ShapeDtypeStruct((length, RET_HEADS * RET_DV), BF16),
                   jax.ShapeDtypeStruct((length, RET_HEADS * RET_DV), BF16)],
        scratch_shapes=[pltpu.VMEM((RET_DK, RET_DV), F32)],
        compiler_params=_params(("parallel", "arbitrary")),
    )(proj, proj, proj, proj, cos2, sin2, intra, zeta, xi, states, dmix)


def _ssd_consts():
    tri = np.tril(np.ones((CHUNK, CHUNK), np.float32))
    expand = np.zeros((LANE, SSM_DINNER), np.float32)
    for h in range(SSM_HEADS):
        expand[h, h * SSM_P:(h + 1) * SSM_P] = 1.0
    return jnp.asarray(tri), jnp.asarray(tri.T.copy()), jnp.asarray(expand)


def _ssd_chunk(xs, bm, cm, dtr, z, state, dt_bias, a_log, d_skip, norm_w, tri, tri_t, expand):
    gw = SSM_DINNER // SSM_GROUPS
    dt = jax.nn.softplus(dtr + dt_bias)
    a_neg = -jnp.exp(a_log)
    da = dt * a_neg
    acs = _dot_hi(tri, da)
    acs_t = _dot_hi_tn(da, tri_t)
    dt_x = _times_01(dt, expand)
    a_x = jnp.mean(_dot_hi(jnp.broadcast_to(a_neg, (8, LANE)), expand), axis=0, keepdims=True)
    da_x = dt_x * a_x
    acs_x = _01_times(tri, da_x)
    tot_x = jnp.sum(da_x, axis=0, keepdims=True)
    x_dt = xs * dt_x
    x_dec = x_dt * jnp.exp(tot_x - acs_x)
    e_acs = jnp.exp(acs_x)
    e_tot = jnp.exp(tot_x)
    lane = lax.broadcasted_iota(jnp.int32, (CHUNK, LANE), 1)
    sub = lax.broadcasted_iota(jnp.int32, (CHUNK, LANE), 0)
    causal = sub >= lane
    ys, new_states = [], []
    for g in range(SSM_GROUPS):
        bg = bm[:, g * SSM_N:(g + 1) * SSM_N]
        cg = cm[:, g * SSM_N:(g + 1) * SSM_N]
        sg = state[:, g * gw:(g + 1) * gw]
        cb = _mm_nt(cg, bg)
        y_off = _mm(cg, sg) * e_acs[:, g * gw:(g + 1) * gw]
        new_states.append(sg * e_tot[:, g * gw:(g + 1) * gw] + _mm_tn(bg, x_dec[:, g * gw:(g + 1) * gw]))
        pairs = []
        for p in range(gw // LANE):
            hp = g * (gw // LANE) + p
            xp = x_dt[:, hp * LANE:(hp + 1) * LANE]
            halves = []
            for head in (2 * hp, 2 * hp + 1):
                col = jnp.sum(jnp.where(lane == head, acs, 0.0), axis=1, keepdims=True)
                row = jnp.sum(jnp.where(sub == head, acs_t, 0.0), axis=0, keepdims=True)
                decay = jnp.exp(jnp.where(causal, col - row, -1e30))
                halves.append(_mm(cb * decay, xp))
            pairs.append(jnp.where(lane < SSM_P, halves[0], halves[1]))
        ys.append(jnp.concatenate(pairs, axis=1) + y_off)
    d_x = jnp.mean(_dot_hi(jnp.broadcast_to(d_skip, (8, LANE)), expand), axis=0, keepdims=True)
    y = (jnp.concatenate(ys, axis=1) + d_x * xs) * _silu(z)
    normed = []
    for g in range(SSM_GROUPS):
        yg = y[:, g * gw:(g + 1) * gw]
        normed.append(yg * lax.rsqrt(jnp.mean(yg * yg, axis=-1, keepdims=True) + EPS))
    return jnp.concatenate(normed, axis=1) * norm_w, jnp.concatenate(new_states, axis=1)


XBC = SSM_DINNER + 2 * SSM_GROUPS * SSM_N


def _ssd_specs(rev, nc):
    def cidx(c):
        return nc - 1 - c if rev else c
    row = lambda w, j: pl.BlockSpec((CHUNK, w), lambda c: (cidx(c), j))
    whole = lambda shape: pl.BlockSpec(shape, lambda c: (0,) * len(shape))
    return [row(XBC, 0), row(LANE, 5632 // LANE), row(SSM_DINNER, 3),
            whole((1, LANE)), whole((1, LANE)), whole((1, LANE)), whole((1, SSM_DINNER)),
            whole((CHUNK, CHUNK)), whole((CHUNK, CHUNK)), whole((LANE, SSM_DINNER))], cidx


def ssd_fwd(xbc, proj, dt_bias, a_log, d_skip, norm_w):
    length = proj.shape[0]
    nc = length // CHUNK
    tri, tri_t, expand = _ssd_consts()
    specs, _ = _ssd_specs(False, nc)

    def body(xbc_ref, dt_ref, z_ref, dtb_ref, alog_ref, d_ref, nw_ref, tri_ref, trit_ref, e_ref, y_ref, st_ref, state):
        @pl.when(pl.program_id(0) == 0)
        def _():
            state[...] = jnp.zeros_like(state)

        st_ref[0] = state[...]
        y, new_state = _ssd_chunk(
            xbc_ref[:, 0:SSM_DINNER], xbc_ref[:, SSM_DINNER:SSM_DINNER + 256], xbc_ref[:, SSM_DINNER + 256:XBC],
            dt_ref[...], z_ref[...], state[...], dtb_ref[...], alog_ref[...], d_ref[...], nw_ref[...],
            tri_ref[...], trit_ref[...], e_ref[...])
        y_ref[...] = y.astype(y_ref.dtype)
        state[...] = new_state

    return pl.pallas_call(
        body, name="ssd_fwd", grid=(nc,), in_specs=specs,
        out_specs=[pl.BlockSpec((CHUNK, SSM_DINNER), lambda c: (c, 0)),
                   pl.BlockSpec((1, SSM_N, SSM_DINNER), lambda c: (c, 0, 0))],
        out_shape=[jax.ShapeDtypeStruct((length, SSM_DINNER), BF16),
                   jax.ShapeDtypeStruct((nc, SSM_N, SSM_DINNER), F32)],
        scratch_shapes=[pltpu.VMEM((SSM_N, SSM_DINNER), F32)],
        compiler_params=_params(("arbitrary",)),
    )(xbc, proj, proj, dt_bias, a_log, d_skip, norm_w, tri, tri_t, expand)


def ssd_bwd(xbc, proj, dt_bias, a_log, d_skip, norm_w, states, dmix):
    length = proj.shape[0]
    nc = length // CHUNK
    tri, tri_t, expand = _ssd_consts()
    specs, cidx = _ssd_specs(True, nc)

    def body(xbc_ref, dt_ref, z_ref, dtb_ref, alog_ref, d_ref, nw_ref, tri_ref, trit_ref, e_ref, st_ref, dy_ref,
             dxbc_ref, ddt_ref, dz_ref, ddtb_ref, dalog_ref, dd_ref, dnw_ref, dstate):
        c = pl.program_id(0)

        @pl.when(c == 0)
        def _():
            dstate[...] = jnp.zeros_like(dstate)

        tri_v, trit_v, e_v = tri_ref[...], trit_ref[...], e_ref[...]
        _, vjp = jax.vjp(
            lambda *a: _ssd_chunk(*a, tri_v, trit_v, e_v),
            xbc_ref[:, 0:SSM_DINNER], xbc_ref[:, SSM_DINNER:SSM_DINNER + 256], xbc_ref[:, SSM_DINNER + 256:XBC],
            dt_ref[...], z_ref[...], st_ref[0], dtb_ref[...], alog_ref[...], d_ref[...], nw_ref[...])
        dxs, dbm, dcm, ddt, dz, ds, ddtb, dalog, dd, dnw = vjp((dy_ref[...].astype(F32), dstate[...]))
        dxbc_ref[:, 0:SSM_DINNER] = dxs
        dxbc_ref[:, SSM_DINNER:SSM_DINNER + 256] = dbm
        dxbc_ref[:, SSM_DINNER + 256:XBC] = dcm
        ddt_ref[...] = ddt.astype(ddt_ref.dtype)
        dz_ref[...] = dz.astype(dz_ref.dtype)
        dstate[...] = ds
        for r, d in ((ddtb_ref, ddtb), (dalog_ref, dalog), (dd_ref, dd), (dnw_ref, dnw)):
            @pl.when(c == 0)
            def _(r=r, d=d):
                r[...] = d

            @pl.when(c > 0)
            def _(r=r, d=d):
                r[...] += d

    whole = lambda shape: pl.BlockSpec(shape, lambda c: (0,) * len(shape))
    specs = specs + [pl.BlockSpec((1, SSM_N, SSM_DINNER), lambda c: (cidx(c), 0, 0)),
                     pl.BlockSpec((CHUNK, SSM_DINNER), lambda c: (cidx(c), 1))]
    return pl.pallas_call(
        body, name="ssd_bwd", grid=(nc,), in_specs=specs,
        out_specs=[pl.BlockSpec((CHUNK, XBC), lambda c: (cidx(c), 0)), pl.BlockSpec((CHUNK, LANE), lambda c: (cidx(c), 0)),
                   pl.BlockSpec((CHUNK, SSM_DINNER), lambda c: (cidx(c), 0)),
                   whole((1, LANE)), whole((1, LANE)), whole((1, LANE)), whole((1, SSM_DINNER))],
        out_shape=[jax.ShapeDtypeStruct((length, XBC), F32), jax.ShapeDtypeStruct((length, LANE), BF16),
                   jax.ShapeDtypeStruct((length, SSM_DINNER), BF16),
                   jax.ShapeDtypeStruct((1, LANE), F32), jax.ShapeDtypeStruct((1, LANE), F32),
                   jax.ShapeDtypeStruct((1, LANE), F32), jax.ShapeDtypeStruct((1, SSM_DINNER), F32)],
        scratch_shapes=[pltpu.VMEM((SSM_N, SSM_DINNER), F32)],
        compiler_params=_params(("arbitrary",)),
    )(xbc, proj, proj, dt_bias, a_log, d_skip, norm_w, tri, tri_t, expand, states, dmix)


def _cmul(ar, ai, br, bi):
    return ar * br - ai * bi, ar * bi + ai * br


def s5_scan(b_re, b_im, a_re, a_im, *, reverse=False, states=None, name, lw=256):
    length, lanes = b_re.shape
    nk = length // SCAN_SEG
    with_da = states is not None
    assert reverse or not with_da

    def shift(v):
        sub = lax.broadcasted_iota(jnp.int32, v.shape, 0)
        if reverse:
            return jnp.where(sub == SCAN_SEG - 1, 0.0, pltpu.roll(v, SCAN_SEG - 1, 0))
        return jnp.where(sub == 0, 0.0, pltpu.roll(v, 1, 0))

    def body(*refs):
        if with_da:
            bre_ref, bim_ref, are_ref, aim_ref, sre_ref, sim_ref, xre_ref, xim_ref, dare_ref, daim_ref = refs
        else:
            bre_ref, bim_ref, are_ref, aim_ref, xre_ref, xim_ref = refs
        ar = jnp.broadcast_to(are_ref[...], (SCAN_SEG, lw))
        ai = jnp.broadcast_to(aim_ref[...], (SCAN_SEG, lw))

        def tile(i):
            k = (nk - 1 - i) if reverse else i
            return pl.ds(pl.multiple_of(k * SCAN_SEG, SCAN_SEG), SCAN_SEG)

        def local(i, carry):
            xr, xi, pr, pi = carry
            rows = tile(i)
            mr, mi = _cmul(ar, ai, xr, xi)
            xr, xi = mr + bre_ref[rows, :], mi + bim_ref[rows, :]
            xre_ref[rows, :] = xr
            xim_ref[rows, :] = xi
            pr, pi = _cmul(ar, ai, pr, pi)
            return xr, xi, pr, pi

        zero = jnp.zeros((SCAN_SEG, lw), F32)
        one = jnp.ones((SCAN_SEG, lw), F32)
        er, ei, pr, pi = lax.fori_loop(0, nk, local, (zero, zero, one, zero))
        cr, ci = zero, zero
        for _ in range(SCAN_SEG - 1):
            mr, mi = _cmul(pr, pi, cr, ci)
            cr, ci = shift(er + mr), shift(ei + mi)

        def fix(i, carry):
            pr, pi, dr, di = carry
            rows = tile(i)
            pr, pi = _cmul(ar, ai, pr, pi)
            mr, mi = _cmul(pr, pi, cr, ci)
            xr, xi = xre_ref[rows, :] + mr, xim_ref[rows, :] + mi
            xre_ref[rows, :] = xr
            xim_ref[rows, :] = xi
            if with_da:
                k = nk - 1 - i
                prev = pl.ds(pl.multiple_of(jnp.maximum(k - 1, 0) * SCAN_SEG, SCAN_SEG), SCAN_SEG)
                last = pl.ds((nk - 1) * SCAN_SEG, SCAN_SEG)
                sub = lax.broadcasted_iota(jnp.int32, (SCAN_SEG, lw), 0)
                wr = jnp.where(sub == 0, 0.0, pltpu.roll(sre_ref[last, :], 1, 0))
                wi = jnp.where(sub == 0, 0.0, pltpu.roll(sim_ref[last, :], 1, 0))
                sr = jnp.where(k == 0, wr, sre_ref[prev, :])
                si = jnp.where(k == 0, wi, sim_ref[prev, :])
                dr, di = dr + xr * sr + xi * si, di + xi * sr - xr * si
            return pr, pi, dr, di

        _, _, dr, di = lax.fori_loop(0, nk, fix, (one, zero, zero, zero))
        if with_da:
            dare_ref[...] = jnp.sum(dr, axis=0, keepdims=True)
            daim_ref[...] = jnp.sum(di, axis=0, keepdims=True)

    col = pl.BlockSpec((length, lw), lambda j: (0, j))
    vec = pl.BlockSpec((1, lw), lambda j: (0, j))
    ins = [b_re, b_im, a_re, a_im] + (list(states) if with_da else [])
    in_specs = [col, col, vec, vec] + ([col, col] if with_da else [])
    out_specs = [col, col] + ([vec, vec] if with_da else [])
    out_shape = [jax.ShapeDtypeStruct((length, lanes), F32)] * 2 + ([jax.ShapeDtypeStruct((1, lanes), F32)] * 2 if with_da else [])
    return pl.pallas_call(
        body, name=name, grid=(lanes // lw,), in_specs=in_specs, out_specs=out_specs, out_shape=out_shape,
        compiler_params=_params(("parallel",)),
    )(*ins)


def _seg_interleave(v):
    length = v.shape[0]
    return v.reshape(SCAN_SEG, length // SCAN_SEG, -1).transpose(1, 0, 2).reshape(length, -1)


def _seg_deinterleave(v):
    length = v.shape[0]
    return v.reshape(length // SCAN_SEG, SCAN_SEG, -1).transpose(1, 0, 2).reshape(length, -1)


def _block_diag(m):
    eye = jnp.eye(S5_GROUPS, dtype=m.dtype)
    return (m.reshape(S5_GROUPS, S5_GROUP, 1, S5_STATE) * eye[:, None, :, None]).reshape(S5_GROUPS * S5_GROUP, S5_LANES)


def _block_diag_take(full):
    idx = jnp.arange(S5_GROUPS)
    blocks = full.reshape(S5_GROUPS, S5_GROUP, S5_GROUPS, S5_STATE)[idx, :, idx, :]
    return blocks.reshape(S5_GROUPS * S5_GROUP, S5_STATE)


def _s5_prep(a_re, a_im, log_step, b_re, b_im, rep):
    step = jnp.exp(log_step)
    mag = jnp.exp(a_re * step)
    ab_re = mag * jnp.cos(a_im * step)
    ab_im = mag * jnp.sin(a_im * step)
    den = a_re * a_re + a_im * a_im
    f_re = ((ab_re - 1.0) * a_re + ab_im * a_im) / den
    f_im = (ab_im * a_re - (ab_re - 1.0) * a_im) / den
    fr, fi = _dot_hi(rep, f_re), _dot_hi(rep, f_im)
    return ab_re, ab_im, fr * b_re - fi * b_im, fr * b_im + fi * b_re


def _rms(x, g):
    return (x * lax.rsqrt(jnp.mean(x * x, axis=-1, keepdims=True) + EPS) * g,)


def _ffn_act(gate, up):
    return (_silu(gate) * up,)


def _glu(a, g):
    return (a * jax.nn.sigmoid(g),)


def _ln_silu(x, g, b):
    xc = x - jnp.mean(x, axis=-1, keepdims=True)
    var = jnp.mean(xc * xc, axis=-1, keepdims=True)
    return (_silu(xc * lax.rsqrt(var + EPS) * g + b),)


def _s5_post(y, u, d_skip, glu_w):
    s = jax.nn.gelu(y + d_skip * u)
    return (s * jax.nn.sigmoid(_mm(s, glu_w)),)


def loss_head(x, tgt, g, *, tl=512):
    length, d = x.shape
    tl = min(tl, length)

    def body(x_ref, t_ref, g_ref, loss_ref, dx_ref, dg_ref):
        i = pl.program_id(0)
        y, vjp = jax.vjp(lambda x, g: _rms(x, g)[0], x_ref[...], g_ref[...])
        err = y - t_ref[...]
        dx, dg = vjp(err * (1.0 / d))
        dx_ref[...] = dx
        part = jnp.broadcast_to(0.5 * jnp.sum(jnp.mean(err * err, axis=-1, keepdims=True), axis=0, keepdims=True), (1, LANE))

        @pl.when(i == 0)
        def _():
            loss_ref[...] = part
            dg_ref[...] = dg

        @pl.when(i > 0)
        def _():
            loss_ref[...] += part
            dg_ref[...] += dg

    row = pl.BlockSpec((tl, d), lambda i: (i, 0))
    return pl.pallas_call(
        body, name="loss_head", grid=(length // tl,),
        in_specs=[row, row, pl.BlockSpec((1, d), lambda i: (0, 0))],
        out_specs=[pl.BlockSpec((1, LANE), lambda i: (0, 0)), row, pl.BlockSpec((1, d), lambda i: (0, 0))],
        out_shape=[jax.ShapeDtypeStruct((1, LANE), F32), jax.ShapeDtypeStruct((length, d), F32),
                   jax.ShapeDtypeStruct((1, d), F32)],
        compiler_params=_params(("arbitrary",)),
    )(x, tgt, g)


def _pad_heads(v):
    return jnp.pad(v, ((0, 0), (0, LANE - v.shape[1])))


def local_step(x, tgt, w, late_weights=None, early_reduce=None):
    length = x.shape[0]
    cos2, sin2 = _rotary_tables(length)
    grads = {}
    w = dict(w)

    def rms_fwd(xin, g, name):
        return rowwise_fwd(_rms, [xin], [], [g], [], [(D_MODEL, BF16)], name=name, tl=512)[0]

    def rms_bwd(xin, g, dh, dxo, name):
        return rowwise_bwd(_rms, [xin], [], [g], [], [dh], [F32], name=name, tl=512, add=dxo)

    def ffn_fwd(i, xin):
        hf = rms_fwd(xin, w["ffn_norm"][i:i + 1], f"ffn{i}_norm")
        w_up = ffn_interleave(w["ffn_w_up"][i], name=f"ffn{i}_up_pairs")
        a = matmul(hf, w_up, name=f"ffn{i}_up")
        act = ffn_conv_act(a, ffn_interleave(w["ffn_dw_w"][i]), ffn_interleave(w["ffn_dw_b"][i:i + 1]),
                           name=f"ffn{i}_conv_act")
        return matmul(act, w["ffn_w_down"][i], res=xin, name=f"ffn{i}_down"), (hf, a, act, w_up)

    def ffn_bwd(i, xin, saved, dxo):
        hf, a, act, w_up = saved
        dact = matmul(dxo, w["ffn_w_down"][i], tb=True, name=f"ffn{i}_down_dx")
        dw_down = matmul(act, dxo, ta=True, name=f"ffn{i}_down_dw")
        da, ddw_w, ddw_b = ffn_conv_act_bwd(a, ffn_interleave(w["ffn_dw_w"][i]), ffn_interleave(w["ffn_dw_b"][i:i + 1]),
                                            dact, name=f"ffn{i}_conv_act_bwd")
        dw_up = ffn_deinterleave(matmul(hf, da, ta=True, name=f"ffn{i}_up_dw"), name=f"ffn{i}_up_dw_plain")
        dhf = matmul(da, w_up, tb=True, name=f"ffn{i}_up_dx")
        dxin, dnorm = rms_bwd(xin, w["ffn_norm"][i:i + 1], dhf, dxo, f"ffn{i}_norm_bwd")
        return dxin, dict(ffn_norm=dnorm, ffn_w_up=dw_up, ffn_dw_w=ffn_deinterleave(ddw_w),
                          ffn_dw_b=ffn_deinterleave(ddw_b), ffn_w_down=dw_down)

    w_in_e = jnp.pad(w["e_w_in"][0], ((0, 0), (0, EVEN_IN_PAD - EVEN_IN)))
    conv_w_e, conv_b_e = w["e_conv_w"][0], w["e_conv_b"]
    dt_bias, a_log, d_skip = _pad_heads(w["e_dt_bias"]), _pad_heads(w["e_a_log"]), _pad_heads(w["e_d"])
    xbc_off = 4 * D_MODEL

    hn0 = rms_fwd(x, w["mix_norm"][0:1], "mix0_norm")
    proj0 = matmul(hn0, w_in_e, name="even_in")
    y_ret, ret_states = retention_fwd(proj0, cos2, sin2)
    xbc = conv_fwd(proj0, conv_w_e, conv_b_e, act=True, off=xbc_off, name="ssd_conv")
    y_ssm, ssd_states = ssd_fwd(xbc, proj0, dt_bias, a_log, d_skip, w["e_ssm_norm"])
    mix0 = jnp.concatenate([y_ret, y_ssm], axis=1)
    if late_weights is not None:
        w.update(late_weights(y_ssm))
    w_out_e = w["e_w_out"][0]
    x1 = matmul(mix0, w_out_e, res=x, name="even_out")
    x2, ffn0_saved = ffn_fwd(0, x1)

    w_in_o, w_out_o, glu_w = w["o_w_in"][0], w["o_w_out"][0], w["o_glu_w"][0]
    dw_w_o, dw_b_o, ln_g, ln_b, d_o = w["o_dw_w"][0], w["o_dw_b"], w["o_ln_g"], w["o_ln_b"], w["o_d"]
    rep = jnp.asarray(np.repeat(np.eye(S5_GROUPS, dtype=np.float32), S5_GROUP, axis=0))
    rows_gc = (S5_GROUPS * S5_GROUP, S5_STATE)
    prep_in = [w["o_a_re"][0], w["o_a_im"][0], w["o_log_step"].reshape(S5_GROUPS, 1),
               w["o_b_re"][0].transpose(0, 2, 1).reshape(rows_gc), w["o_b_im"][0].transpose(0, 2, 1).reshape(rows_gc), rep]
    ab_re, ab_im, bb_re, bb_im = whole_fwd(
        _s5_prep, prep_in, [(S5_GROUPS, S5_STATE)] * 2 + [rows_gc] * 2, name="s5_prep")
    a_re_row, a_im_row = ab_re.reshape(1, S5_LANES), ab_im.reshape(1, S5_LANES)
    b_re_bd, b_im_bd = _block_diag(bb_re).astype(BF16), _block_diag(bb_im).astype(BF16)
    c_re_bd = _block_diag(w["o_c_re"][0].reshape(rows_gc)).astype(BF16)
    c_im_neg_bd = _block_diag(-w["o_c_im"][0].reshape(rows_gc)).astype(BF16)

    hn1 = rms_fwd(x2, w["mix_norm"][1:2], "mix1_norm")
    proj1 = matmul(hn1, w_in_o, name="odd_in")
    half = D_MODEL // 2
    c_glu = rowwise_fwd(_glu, [Cols(proj1, half, 0), Cols(proj1, half, 1)], [], [], [], [(half, F32)],
                        name="conf_glu", tl=512)[0]
    c_conv = conv_fwd(c_glu, dw_w_o, dw_b_o, act=False, name="conf_conv")
    c_out = rowwise_fwd(_ln_silu, [c_conv], [], [ln_g, ln_b], [], [(half, BF16)], name="conf_ln", tl=512)[0]
    u_seg = _seg_interleave(proj1[:, 2 * half:])
    bu_re = matmul(u_seg, b_re_bd, name="s5_bu_re")
    bu_im = matmul(u_seg, b_im_bd, name="s5_bu_im")
    xs_re, xs_im = s5_scan(bu_re, bu_im, a_re_row, a_im_row, name="s5_scan")
    y_im = matmul(xs_im, c_im_neg_bd, tb=True, name="s5_y_im")
    y_s5 = _seg_deinterleave(matmul(xs_re, c_re_bd, tb=True, res=y_im, name="s5_y_re"))
    s_out = rowwise_fwd(_s5_post, [y_s5, Cols(proj1, half, 2)], [], [d_o, glu_w], [], [(half, BF16)],
                        name="s5_post", tl=512)[0]
    mix1 = jnp.concatenate([c_out, s_out], axis=1)
    x3 = matmul(mix1, w_out_o, res=x2, name="odd_out")
    x4, ffn1_saved = ffn_fwd(1, x3)

    loss, dx4, dfinal = loss_head(x4, tgt, w["final_norm"].reshape(1, D_MODEL))
    grads["final_norm"] = dfinal.reshape(D_MODEL)

    dx3, g_ffn1 = ffn_bwd(1, x3, ffn1_saved, dx4)
    dmix1 = matmul(dx3, w_out_o, tb=True, name="odd_out_dx")
    grads["o_w_out"] = [matmul(mix1, dx3, ta=True, name="odd_out_dw")]
    dc_conv, dln_g, dln_b = rowwise_bwd(_ln_silu, [c_conv], [], [ln_g, ln_b], [], [Cols(dmix1, half, 0)], [F32],
                                        name="conf_ln_bwd", tl=512)
    dc_glu, ddw_w_o, ddw_b_o = conv_bwd(c_glu, dw_w_o, dw_b_o, dc_conv, act=False, name="conf_conv_bwd")
    d_cacg = rowwise_bwd(_glu, [Cols(proj1, half, 0), Cols(proj1, half, 1)], [], [], [], [dc_glu], [BF16],
                         name="conf_glu_bwd", tl=512, merge=True)[0]
    dy_s5, du_post, dd_o, dglu_w = rowwise_bwd(
        _s5_post, [y_s5, Cols(proj1, half, 2)], [], [d_o, glu_w], [], [Cols(dmix1, half, 1)], [F32, F32],
        name="s5_post_bwd", tl=512)
    dy_seg = _seg_interleave(dy_s5)
    dxs_re = matmul(dy_seg, c_re_bd, name="s5_dx_re")
    dxs_im = matmul(dy_seg, c_im_neg_bd, name="s5_dx_im")
    dc_re_bd = matmul(dy_seg, xs_re, ta=True, name="s5_dc_re")
    dc_im_neg_bd = matmul(dy_seg, xs_im, ta=True, name="s5_dc_im")
    g_re, g_im, dab_re, dab_im = s5_scan(dxs_re, dxs_im, a_re_row, -a_im_row, reverse=True, states=(xs_re, xs_im),
                                         name="s5_scan_bwd", lw=LANE)
    dbb_re = _block_diag_take(matmul(u_seg, g_re, ta=True, name="s5_db_re"))
    dbb_im = _block_diag_take(matmul(u_seg, g_im, ta=True, name="s5_db_im"))
    du_im = matmul(g_im, b_im_bd, tb=True, name="s5_du_im")
    du = _seg_deinterleave(matmul(g_re, b_re_bd, tb=True, res=du_im, name="s5_du_re")) + du_post
    da_re, da_im, dlog_step, db_re, db_im = whole_bwd(
        _s5_prep, prep_in, 5,
        [dab_re.reshape(S5_GROUPS, S5_STATE), dab_im.reshape(S5_GROUPS, S5_STATE), dbb_re, dbb_im], name="s5_prep_bwd")
    gcn = (S5_GROUPS, S5_GROUP, S5_STATE)
    grads.update(
        o_a_re=da_re[None], o_a_im=da_im[None], o_log_step=dlog_step.reshape(1, S5_GROUPS),
        o_b_re=db_re.reshape(gcn).transpose(0, 2, 1)[None], o_b_im=db_im.reshape(gcn).transpose(0, 2, 1)[None],
        o_c_re=_block_diag_take(dc_re_bd).reshape(gcn)[None], o_c_im=-_block_diag_take(dc_im_neg_bd).reshape(gcn)[None],
        o_d=dd_o, o_glu_w=[dglu_w], o_dw_w=ddw_w_o[None], o_dw_b=ddw_b_o, o_ln_g=dln_g, o_ln_b=dln_b)
    dproj1 = jnp.concatenate([d_cacg, du.astype(BF16)], axis=1)
    grads["o_w_in"] = [matmul(hn1, dproj1, ta=True, name="odd_in_dw")]
    dhn1 = matmul(dproj1, w_in_o, tb=True, name="odd_in_dx")
    dx2, dmix_norm1 = rms_bwd(x2, w["mix_norm"][1:2], dhn1, dx3, "mix1_norm_bwd")

    if early_reduce is not None:
        zero = early_reduce({("o_w_in", 0): grads["o_w_in"][0], ("o_glu_w", 0): grads["o_glu_w"][0],
                             ("o_w_out", 0): grads["o_w_out"][0], ("ffn_w_up", 1): g_ffn1["ffn_w_up"],
                             ("ffn_w_down", 1): g_ffn1["ffn_w_down"]})
        w["ffn_dw_b"] = w["ffn_dw_b"] + zero
    dx1, g_ffn0 = ffn_bwd(0, x1, ffn0_saved, dx2)
    if early_reduce is not None:
        dt_bias = dt_bias + early_reduce({("ffn_w_up", 0): g_ffn0["ffn_w_up"], ("ffn_w_down", 0): g_ffn0["ffn_w_down"]})
    for k in g_ffn0:
        per_layer = [g_ffn0[k], g_ffn1[k]]
        grads[k] = per_layer if k in ("ffn_w_up", "ffn_w_down") else jnp.stack(per_layer).reshape(w[k].shape)
    dmix0 = matmul(dx1, w_out_e, tb=True, name="even_out_dx")
    grads["e_w_out"] = [matmul(mix0, dx1, ta=True, name="even_out_dw")]
    dq, dk, dv, dg = retention_bwd(proj0, cos2, sin2, ret_states, dmix0)
    dxbc_c, ddt, dz, ddt_bias, da_log, dd_skip, dssm_norm = ssd_bwd(
        xbc, proj0, dt_bias, a_log, d_skip, w["e_ssm_norm"], ssd_states, dmix0)
    dxbc, dconv_w, dconv_b = conv_bwd(proj0, conv_w_e, conv_b_e, dxbc_c, act=True, off=xbc_off,
                                      name="ssd_conv_bwd", dx_dtype=BF16)
    dproj0 = jnp.concatenate([dq, dk, dv, dg, dz, dxbc, ddt], axis=1)
    grads["e_w_in"] = [matmul(hn0, dproj0, ta=True, name="even_in_dw")[:, :EVEN_IN]]
    dhn0 = matmul(dproj0, w_in_e, tb=True, name="even_in_dx")
    dx, dmix_norm0 = rms_bwd(x, w["mix_norm"][0:1], dhn0, dx1, "mix0_norm_bwd")
    grads.update(
        mix_norm=jnp.concatenate([dmix_norm0, dmix_norm1], axis=0), e_conv_w=dconv_w[None], e_conv_b=dconv_b,
        e_dt_bias=ddt_bias[:, :SSM_HEADS], e_a_log=da_log[:, :SSM_HEADS], e_d=dd_skip[:, :SSM_HEADS],
        e_ssm_norm=dssm_norm)
    return loss, dx, grads


def adamw(w, g, m, v, *, name):
    shape = w.shape
    cols = shape[-1]
    rows = w.size // cols
    tr = _tile(rows, max(8, (512 * 1024 // cols) // 8 * 8), unit=8)

    def body(w_ref, g_ref, m_ref, v_ref, d_ref, nm_ref, nv_ref):
        gv = g_ref[...]
        nm = ADAM_B1 * m_ref[...] + (1.0 - ADAM_B1) * gv
        nv = ADAM_B2 * v_ref[...] + (1.0 - ADAM_B2) * jnp.square(gv)
        m_hat = nm / (1.0 - ADAM_B1 ** ADAM_STEP)
        v_hat = nv / (1.0 - ADAM_B2 ** ADAM_STEP)
        d_ref[...] = -ADAM_LR * (m_hat / (jnp.sqrt(v_hat) + ADAM_EPS) + ADAM_WD * w_ref[...])
        nm_ref[...] = nm
        nv_ref[...] = nv

    spec = pl.BlockSpec((tr, cols), lambda i: (i, 0))
    outs = pl.pallas_call(
        body, name=name, grid=(rows // tr,), in_specs=[spec] * 4, out_specs=[spec] * 3,
        out_shape=[jax.ShapeDtypeStruct((rows, cols), F32)] * 3, compiler_params=_params(("parallel",)),
    )(*[t.reshape(rows, cols) for t in (w, g, m, v)])
    return [o.reshape(shape) for o in outs]


OTHER_CHIPS = ((1, 0), (0, 1), (1, 1))
ANY = pl.BlockSpec(memory_space=pl.ANY)


def _position():
    return lax.axis_index("x"), lax.axis_index("y"), lax.axis_index("c")


def _flip(v, f):
    return 1 - v if f else v


def _remote(src, dst, send_sem, recv_sem, device):
    return pltpu.make_async_remote_copy(src_ref=src, dst_ref=dst, send_sem=send_sem, recv_sem=recv_sem,
                                        device_id=device, device_id_type=MESH)


def gather_shards(big, small):
    n_big, n_small = len(big), len(small)
    halves = [a.shape[0] // 2 for a in big]

    def body(*refs):
        big_refs, small_refs = refs[:n_big], refs[n_big:n_big + n_small]
        obig_refs = refs[n_big + n_small:2 * n_big + n_small]
        osmall_refs = refs[2 * n_big + n_small:2 * (n_big + n_small)]
        ici_send, ici_recv, d2d_send, d2d_recv, small_send, small_recv = refs[2 * (n_big + n_small):]
        x, y, c = _position()
        mine = 2 * x + y

        def half(k, core):
            return pl.ds(pl.multiple_of(core * halves[k], 16), halves[k])

        sends = []
        for j, (fx, fy) in enumerate(OTHER_CHIPS):
            peer = (_flip(x, fx), _flip(y, fy), c)
            for k in range(n_big):
                sends.append(_remote(big_refs[k].at[half(k, c)], obig_refs[k].at[mine, half(k, c)],
                                     ici_send.at[j, k], ici_recv.at[j, k], peer))
            for k in range(n_small):
                sends.append(_remote(small_refs[k], osmall_refs[k].at[mine], small_send.at[j, k], small_recv.at[j, k], peer))
        for cp in sends:
            cp.start()
        for j, (fx, fy) in enumerate(OTHER_CHIPS):
            px, py = _flip(x, fx), _flip(y, fy)
            src_chip = 2 * px + py
            for k in range(n_big):
                landed = obig_refs[k].at[src_chip, half(k, c)]
                _remote(landed, landed, ici_send.at[j, k], ici_recv.at[j, k], (px, py, c)).wait_recv()
                fwd = _remote(landed, landed, d2d_send.at[j, k], d2d_recv.at[j, k], (x, y, 1 - c))
                fwd.start()
                sends.append(fwd)
        for j, (fx, fy) in enumerate(OTHER_CHIPS):
            px, py = _flip(x, fx), _flip(y, fy)
            src_chip = 2 * px + py
            for k in range(n_big):
                other = obig_refs[k].at[src_chip, half(k, 1 - c)]
                _remote(other, other, d2d_send.at[j, k], d2d_recv.at[j, k], (x, y, 1 - c)).wait_recv()
            for k in range(n_small):
                dst = osmall_refs[k].at[src_chip]
                _remote(small_refs[k], dst, small_send.at[j, k], small_recv.at[j, k], (px, py, c)).wait_recv()
        for cp in sends:
            cp.wait_send()

    arrays = list(big) + list(small)
    dma = pltpu.SemaphoreType.DMA
    return pl.pallas_call(
        body, name="gather_shards", in_specs=[ANY] * len(arrays), out_specs=[ANY] * len(arrays),
        out_shape=[jax.ShapeDtypeStruct((4,) + a.shape, a.dtype) for a in arrays],
        scratch_shapes=[dma((3, n_big)), dma((3, n_big)), dma((3, n_big)), dma((3, n_big)),
                        dma((3, n_small)), dma((3, n_small))],
        compiler_params=_params(),
    )(*arrays)


def allreduce_small(pack):
    rows = pack.shape[0]

    def body(p_ref, o_ref, slots, send_sems, recv_sems):
        x, y, c = _position()
        me = 4 * x + 2 * y + c
        slots[me] = p_ref[...]
        flips = [((k >> 2) & 1, (k >> 1) & 1, k & 1) for k in range(1, 8)]
        sends = []
        for k, (fx, fy, fc) in enumerate(flips):
            peer = (_flip(x, fx), _flip(y, fy), _flip(c, fc))
            sends.append(_remote(p_ref, slots.at[me], send_sems.at[k], recv_sems.at[k], peer))
        for cp in sends:
            cp.start()
        for k, (fx, fy, fc) in enumerate(flips):
            px, py, pc = _flip(x, fx), _flip(y, fy), _flip(c, fc)
            _remote(p_ref, slots.at[4 * px + 2 * py + pc], send_sems.at[k], recv_sems.at[k], (px, py, pc)).wait_recv()
        for cp in sends:
            cp.wait_send()
        acc = slots[0]
        for d in range(1, 8):
            acc = acc + slots[d]
        o_ref[...] = acc

    vmem = pl.BlockSpec(memory_space=pltpu.VMEM)
    return pl.pallas_call(
        body, name="allreduce_small", in_specs=[vmem], out_specs=vmem,
        out_shape=jax.ShapeDtypeStruct(pack.shape, F32),
        scratch_shapes=[pltpu.VMEM((8, rows, LANE), F32), pltpu.SemaphoreType.DMA((7,)), pltpu.SemaphoreType.DMA((7,))],
        compiler_params=_params(),
    )(pack)


def exchange_halves(gs, *, name):
    n = len(gs)

    def body(*refs):
        g_refs, o_refs, (send_sems, recv_sems) = refs[:n], refs[n:2 * n], refs[2 * n:]
        x, y, c = _position()
        copies = [_remote(g_refs[k].at[:, 1 - c], o_refs[k], send_sems.at[k], recv_sems.at[k], (x, y, 1 - c)) for k in range(n)]
        for cp in copies:
            cp.start()
        for cp in copies:
            cp.wait()

    return pl.pallas_call(
        body, name=name, in_specs=[ANY] * n, out_specs=[ANY] * n,
        out_shape=[jax.ShapeDtypeStruct((4,) + g.shape[2:], g.dtype) for g in gs],
        scratch_shapes=[pltpu.SemaphoreType.DMA((n,)), pltpu.SemaphoreType.DMA((n,))],
        compiler_params=_params(),
    )(*gs)


def scatter_to_chips(parts):
    n = len(parts)

    def body(*refs):
        a_refs, o_refs, (send_sems, recv_sems) = refs[:n], refs[n:2 * n], refs[2 * n:]
        x, y, c = _position()
        copies = []
        for j, (fx, fy) in enumerate(OTHER_CHIPS):
            px, py = _flip(x, fx), _flip(y, fy)
            for k in range(n):
                copies.append(_remote(a_refs[k].at[2 * px + py], o_refs[k].at[j], send_sems.at[j, k], recv_sems.at[j, k], (px, py, c)))
        for cp in copies:
            cp.start()
        for cp in copies:
            cp.wait()

    return pl.pallas_call(
        body, name="scatter_to_chips", in_specs=[ANY] * n, out_specs=[ANY] * n,
        out_shape=[jax.ShapeDtypeStruct((3,) + a.shape[1:], a.dtype) for a in parts],
        scratch_shapes=[pltpu.SemaphoreType.DMA((3, n)), pltpu.SemaphoreType.DMA((3, n))],
        compiler_params=_params(),
    )(*parts)


def swap_halves(rs):
    n = len(rs)

    def body(*refs):
        r_refs, o_refs, (send_sems, recv_sems) = refs[:n], refs[n:2 * n], refs[2 * n:]
        x, y, c = _position()
        copies = [_remote(r_refs[k], o_refs[k], send_sems.at[k], recv_sems.at[k], (x, y, 1 - c)) for k in range(n)]
        for cp in copies:
            cp.start()
        for cp in copies:
            cp.wait()

    dma = pltpu.SemaphoreType.DMA
    return pl.pallas_call(
        body, name="swap_halves", in_specs=[ANY] * n, out_specs=[ANY] * n,
        out_shape=[jax.ShapeDtypeStruct(r.shape, r.dtype) for r in rs],
        scratch_shapes=[dma((n,)), dma((n,))],
        compiler_params=_params(),
    )(*rs)


HBM = pl.BlockSpec(memory_space=pltpu.HBM)
SEM = pl.BlockSpec(memory_space=pltpu.SEMAPHORE)
SIDE_EFFECT = pltpu.SideEffectType.DATAFLOW_SIDE_EFFECTING


def _gather_plan(halves):
    def plan(v_refs, land_refs, x, y, c):
        copies = []
        for fx, fy in OTHER_CHIPS:
            for k in range(len(v_refs)):
                rows = pl.ds(pl.multiple_of(c * halves[k], 16), halves[k])
                copies.append((v_refs[k].at[rows], land_refs[k].at[2 * x + y, rows], (_flip(x, fx), _flip(y, fy), c)))
        return copies
    return plan


def _scatter_plan(v_refs, land_refs, x, y, c):
    copies = []
    for j, (fx, fy) in enumerate(OTHER_CHIPS):
        px, py = _flip(x, fx), _flip(y, fy)
        for k in range(len(v_refs)):
            copies.append((v_refs[k].at[2 * px + py], land_refs[k].at[j], (px, py, c)))
    return copies


def chip_exchange_start(srcs, land_shapes, plan, after, *, name):
    n = len(srcs)
    n_cp = 3 * n

    def body(*refs):
        v_refs, land_refs = refs[:n], refs[n:2 * n]
        outs = refs[2 * n + 1:]
        sends, recvs, token = outs[:n_cp], outs[n_cp:2 * n_cp], outs[-1]
        x, y, c = _position()
        for (src, dst, device), send, recv in zip(plan(v_refs, land_refs, x, y, c), sends, recvs, strict=True):
            _remote(src, dst, send, recv, device).start()
        token[...] = jnp.zeros_like(token)

    lands = [lax.empty(shape, v.dtype) for shape, v in zip(land_shapes, srcs)]
    arrays = [pltpu.with_memory_space_constraint(a, pltpu.HBM) for a in list(srcs) + lands]
    outs = pl.pallas_call(
        body, name=name,
        out_shape=tuple(pltpu.SemaphoreType.DMA(()) for _ in range(2 * n_cp))
        + tuple(pltpu.HBM(a.shape, a.dtype) for a in arrays) + (jax.ShapeDtypeStruct((8, LANE), F32),),
        in_specs=[HBM] * (2 * n) + [ANY],
        out_specs=(SEM,) * (2 * n_cp) + (HBM,) * (2 * n) + (pl.BlockSpec(memory_space=pltpu.VMEM),),
        input_output_aliases={i: 2 * n_cp + i for i in range(2 * n)},
        compiler_params=pltpu.CompilerParams(has_side_effects=SIDE_EFFECT),
    )(*arrays, after)
    handle = (outs[:n_cp], outs[n_cp:2 * n_cp], outs[2 * n_cp:2 * n_cp + n], outs[2 * n_cp + n:2 * n_cp + 2 * n])
    return handle, outs[-1]


def chip_exchange_wait(handle, plan, after, *, name):
    sends, recvs, v_thru, land_thru = handle
    n = len(v_thru)
    n_cp = 3 * n

    def body(*refs):
        v_refs, land_refs = refs[:n], refs[n:2 * n]
        sends, recvs = refs[2 * n:2 * n + n_cp], refs[2 * n + n_cp:2 * n + 2 * n_cp]
        x, y, c = _position()
        for (src, dst, device), send, recv in zip(plan(v_refs, land_refs, x, y, c), sends, recvs, strict=True):
            copy = _remote(src, dst, send, recv, device)
            copy.wait_send()
            copy.wait_recv()

    outs = pl.pallas_call(
        body, name=name,
        out_shape=tuple(pltpu.HBM(a.shape, a.dtype) for a in list(v_thru) + list(land_thru)),
        in_specs=[HBM] * (2 * n) + [SEM] * (2 * n_cp) + [ANY], out_specs=(HBM,) * (2 * n),
        input_output_aliases={i: i for i in range(2 * n)},
        compiler_params=pltpu.CompilerParams(has_side_effects=SIDE_EFFECT),
    )(*v_thru, *land_thru, *sends, *recvs, after)
    return outs[:n], outs[n:]


def finish_gather(lands):
    n = len(lands)
    halves = [a.shape[1] // 2 for a in lands]

    def body(*refs):
        o_refs, (send_sems, recv_sems) = refs[n:2 * n], refs[2 * n:]
        x, y, c = _position()

        def half(k, core):
            return pl.ds(pl.multiple_of(core * halves[k], 16), halves[k])

        sends = []
        for j, (fx, fy) in enumerate(OTHER_CHIPS):
            src_chip = 2 * _flip(x, fx) + _flip(y, fy)
            for k in range(n):
                held = o_refs[k].at[src_chip, half(k, c)]
                sends.append(_remote(held, held, send_sems.at[j, k], recv_sems.at[j, k], (x, y, 1 - c)))
        for cp in sends:
            cp.start()
        for j, (fx, fy) in enumerate(OTHER_CHIPS):
            src_chip = 2 * _flip(x, fx) + _flip(y, fy)
            for k in range(n):
                other = o_refs[k].at[src_chip, half(k, 1 - c)]
                _remote(other, other, send_sems.at[j, k], recv_sems.at[j, k], (x, y, 1 - c)).wait_recv()
        for cp in sends:
            cp.wait_send()

    dma = pltpu.SemaphoreType.DMA
    return pl.pallas_call(
        body, name="finish_gather", in_specs=[ANY] * n, out_specs=[ANY] * n,
        out_shape=[jax.ShapeDtypeStruct(a.shape, a.dtype) for a in lands],
        input_output_aliases={k: k for k in range(n)},
        scratch_shapes=[dma((3, n)), dma((3, n))],
        compiler_params=_params(),
    )(*lands)


def add_own_half(g, r, c_idx, *, name):
    _, _, h, cols = g.shape

    def body(c_ref, g_ref, r_ref, o_ref):
        o_ref[...] = (g_ref[0] + r_ref[...]).astype(o_ref.dtype)

    return pl.pallas_call(
        body, name=name,
        grid_spec=pltpu.PrefetchScalarGridSpec(
            num_scalar_prefetch=1, grid=(4,),
            in_specs=[pl.BlockSpec((1, 1, h, cols), lambda s, c: (s, c[0], 0, 0)),
                      pl.BlockSpec((1, h, cols), lambda s, c: (s, 0, 0))],
            out_specs=pl.BlockSpec((1, h, cols), lambda s, c: (s, 0, 0))),
        out_shape=jax.ShapeDtypeStruct(r.shape, BF16), compiler_params=_params(("parallel",)),
    )(c_idx, g, r)


def add_chip_parts(a, parts, chip_idx, *, name):
    _, h, cols = a.shape
    th = h // 2

    def body(s_ref, a_ref, p0_ref, p1_ref, p2_ref, o_ref):
        f = lambda r: r[0].astype(F32)
        o_ref[...] = ((f(a_ref) + f(p0_ref)) + f(p1_ref)) + f(p2_ref)

    part = lambda j: pl.BlockSpec((1, th, cols), lambda i, s, j=j: (j, i, 0))
    return pl.pallas_call(
        body, name=name,
        grid_spec=pltpu.PrefetchScalarGridSpec(
            num_scalar_prefetch=1, grid=(2,),
            in_specs=[pl.BlockSpec((1, th, cols), lambda i, s: (s[0], i, 0)), part(0), part(1), part(2)],
            out_specs=pl.BlockSpec((th, cols), lambda i, s: (i, 0))),
        out_shape=jax.ShapeDtypeStruct((h, cols), F32), compiler_params=_params(("parallel",)),
    )(chip_idx, a, parts, parts, parts)


WEIGHTS = ("mix_norm", "e_w_in", "e_conv_w", "e_conv_b", "e_dt_bias", "e_a_log", "e_d", "e_ssm_norm", "e_w_out",
           "o_w_in", "o_dw_w", "o_dw_b", "o_ln_g", "o_ln_b", "o_a_re", "o_a_im", "o_b_re", "o_b_im", "o_c_re",
           "o_c_im", "o_d", "o_log_step", "o_glu_w", "o_w_out", "ffn_norm", "ffn_w_up", "ffn_dw_w", "ffn_dw_b",
           "ffn_w_down", "final_norm")
BIG = (("e_w_in", 2), ("e_w_out", 1), ("o_w_in", 2), ("o_glu_w", 1), ("o_w_out", 1), ("ffn_w_up", 2), ("ffn_w_down", 1))
SMALL_SHARDED = (("e_conv_w", 2), ("o_dw_w", 2), ("o_dw_b", 1), ("o_ln_g", 1), ("o_ln_b", 1), ("o_d", 1), ("ffn_dw_w", 2))
REPLICATED = tuple(n for n in WEIGHTS if n not in dict(BIG + SMALL_SHARDED))
PACK_ROWS = 8


def _pack(arrays, dtype, row_unit=PACK_ROWS):
    flat = jnp.concatenate([a.astype(dtype).reshape(-1) for a in arrays])
    rows = -(-flat.size // (LANE * row_unit)) * row_unit
    return jnp.pad(flat, (0, rows * LANE - flat.size)).reshape(rows, LANE)


def _unpack(flat, shapes, lead=()):
    out, off = [], 0
    for shape in shapes:
        size = int(np.prod(shape))
        out.append(flat[..., off:off + size].reshape(lead + tuple(shape)))
        off += size
    return out


def _join_shards(parts, axis):
    return jnp.concatenate([parts[s] for s in range(4)], axis=axis)


def _split_shards(full, axis):
    return jnp.stack(jnp.split(full, 4, axis=axis))


def _rows2d(a):
    return a.reshape(-1, a.shape[-1])


def _layer_shards(g, axis):
    rows, cols = g.shape
    if axis == 0:
        return g.reshape(4, 2, rows // 8, cols)
    return g.reshape(rows, 4, cols // 4).transpose(1, 0, 2).reshape(4, 2, rows // 2, cols // 4)


def kernel(x, mix_norm, e_w_in, e_conv_w, e_conv_b, e_dt_bias, e_a_log, e_d, e_ssm_norm, e_w_out, o_w_in, o_dw_w, o_dw_b, o_ln_g, o_ln_b, o_a_re, o_a_im, o_b_re, o_b_im, o_c_re, o_c_im, o_d, o_log_step, o_glu_w, o_w_out, ffn_norm, ffn_w_up, ffn_dw_w, ffn_dw_b, ffn_w_down, final_norm, loss_target, m_mix_norm, m_e_w_in, m_e_conv_w, m_e_conv_b, m_e_dt_bias, m_e_a_log, m_e_d, m_e_ssm_norm, m_e_w_out, m_o_w_in, m_o_dw_w, m_o_dw_b, m_o_ln_g, m_o_ln_b, m_o_a_re, m_o_a_im, m_o_b_re, m_o_b_im, m_o_c_re, m_o_c_im, m_o_d, m_o_log_step, m_o_glu_w, m_o_w_out, m_ffn_norm, m_ffn_w_up, m_ffn_dw_w, m_ffn_dw_b, m_ffn_w_down, m_final_norm, v_mix_norm, v_e_w_in, v_e_conv_w, v_e_conv_b, v_e_dt_bias, v_e_a_log, v_e_d, v_e_ssm_norm, v_e_w_out, v_o_w_in, v_o_dw_w, v_o_dw_b, v_o_ln_g, v_o_ln_b, v_o_a_re, v_o_a_im, v_o_b_re, v_o_b_im, v_o_c_re, v_o_c_im, v_o_d, v_o_log_step, v_o_glu_w, v_o_w_out, v_ffn_norm, v_ffn_w_up, v_ffn_dw_w, v_ffn_dw_b, v_ffn_w_down, v_final_norm):
    given = dict(locals())
    chip = 2 * lax.axis_index("x") + lax.axis_index("y")
    core = lax.axis_index("c")

    core_idx, chip_idx = core.reshape(1).astype(jnp.int32), chip.reshape(1).astype(jnp.int32)

    def whole(n, axis, parts):
        shape = given[n].shape
        own = given[n].astype(parts.dtype)
        return _join_shards(lax.dynamic_update_index_in_dim(parts.reshape((4,) + shape), own, chip, 0), axis)

    first, later = BIG[:1], BIG[1:]
    shards = {n: _rows2d(given[n]).astype(BF16) for n, _ in BIG}
    gathered = gather_shards([shards[n] for n, _ in first], [_rows2d(given[n]) for n, _ in SMALL_SHARDED])
    w = {n: given[n] for n in REPLICATED}
    for (n, axis), parts in zip(first + SMALL_SHARDED, gathered):
        w[n] = whole(n, axis, parts)
    later_shards = [shards[n] for n, _ in later]
    gather_plan = _gather_plan([a.shape[0] // 2 for a in later_shards])
    gather_handle, token = chip_exchange_start(later_shards, [(4,) + a.shape for a in later_shards], gather_plan,
                                               gathered[0], name="gather_start")
    w["mix_norm"] = w["mix_norm"] + token[0, 0]

    def late_weights(after):
        _, lands = chip_exchange_wait(gather_handle, gather_plan, after, name="gather_wait")
        return {n: whole(n, axis, parts) for (n, axis), parts in zip(later, finish_gather(lands))}

    groups = []

    def finish_group(after):
        group = groups[-1]
        group["sums"], group["parts"] = chip_exchange_wait(group.pop("handle"), _scatter_plan, after,
                                                           name=f"scatter_wait_{len(groups) - 1}")

    def early_reduce(layer_grads):
        keys = list(layer_grads)
        if groups:
            finish_group(layer_grads[keys[0]])
        tag = len(groups)
        parts = [_layer_shards(layer_grads[k], dict(BIG)[k[0]] - 1) for k in keys]
        sums = [add_own_half(g, r, core_idx, name=f"add_own_half_{n}{layer}")
                for g, r, (n, layer) in zip(parts, exchange_halves(parts, name=f"exchange_halves_{tag}"), keys)]
        handle, zeros = chip_exchange_start(sums, [(3,) + a.shape[1:] for a in sums], _scatter_plan, sums[0],
                                            name=f"scatter_start_{tag}")
        groups.append(dict(keys=keys, handle=handle))
        return zeros[0, 0]

    loss, dx, grads = local_step(x[0], loss_target[0], w, late_weights, early_reduce)
    finish_group(dx)
    early_keys = [k for group in groups for k in group["keys"]]
    early_sums = [a for group in groups for a in group["sums"]]
    early_parts = [a for group in groups for a in group["parts"]]

    small_names = REPLICATED + tuple(n for n, _ in SMALL_SHARDED)
    small_sum = allreduce_small(_pack([grads[n] for n in small_names], F32))
    reduced = dict(zip(small_names, _unpack(small_sum.reshape(-1), [grads[n].shape for n in small_names])))
    for n, axis in SMALL_SHARDED:
        width = given[n].shape[axis]
        reduced[n] = lax.dynamic_slice_in_dim(reduced[n], chip * width, width, axis=axis)

    keys, parts = [], []
    for n, axis in BIG:
        for layer, g in enumerate(grads[n]):
            if (n, layer) not in early_keys:
                keys.append((n, layer))
                parts.append(_layer_shards(g, axis - 1))
    core_sums = [add_own_half(g, r, core_idx, name=f"add_own_half_{n}{layer}")
                 for g, r, (n, layer) in zip(parts, exchange_halves(parts, name="exchange_halves_last"), keys)]
    chip_parts = scatter_to_chips(core_sums)
    keys, core_sums, chip_parts = early_keys + keys, early_sums + core_sums, early_parts + list(chip_parts)
    mine = [add_chip_parts(a, p, chip_idx, name=f"add_chip_parts_{n}{layer}")
            for a, p, (n, layer) in zip(core_sums, chip_parts, keys)]
    layers = {}
    for (n, layer), own, other in zip(keys, mine, swap_halves(mine)):
        both = jnp.where(core == 0, jnp.stack([own, other]), jnp.stack([other, own]))
        layers.setdefault(n, {})[layer] = both.reshape(given[n].shape[1:])
    for n, _ in BIG:
        reduced[n] = jnp.stack([layers[n][layer] for layer in sorted(layers[n])])

    delta, new_m, new_v = {}, {}, {}
    for n, _ in BIG:
        delta[n], new_m[n], new_v[n] = adamw(given[n], reduced[n], given["m_" + n], given["v_" + n], name="adamw_" + n)
    shapes = [given[n].shape for n in small_names]
    packed = [_pack([src[n] for n in small_names], F32)
              for src in (given, reduced, {n: given["m_" + n] for n in small_names}, {n: given["v_" + n] for n in small_names})]
    for dst, res in zip((delta, new_m, new_v), adamw(*packed, name="adamw_small")):
        dst.update(zip(small_names, _unpack(res.reshape(-1), shapes)))

    total = lax.psum(loss[0, 0], ("x", "y", "c"))
    return (total, dx[None], *[reduced[n] for n in WEIGHTS], *[delta[n] for n in WEIGHTS],
            *[new_m[n] for n in WEIGHTS], *[new_v[n] for n in WEIGHTS])
```

```python
import functools
import math
from typing import NamedTuple

import numpy as np
import jax
import jax.numpy as jnp
from jax import lax
from jax.experimental import pallas as pl
from jax.experimental.pallas import tpu as pltpu

F32 = jnp.float32
BF16 = jnp.bfloat16
HIGHEST = lax.Precision.HIGHEST
MESH = pl.DeviceIdType.MESH

D_MODEL = 1024
EPS = 1e-6
RET_HEADS, RET_DK, RET_DV, CHUNK = 4, 128, 256, 128
ROPE_BASE = 10000.0
SSM_HEADS, SSM_P, SSM_N, SSM_GROUPS = 16, 64, 128, 2
SSM_DINNER = SSM_HEADS * SSM_P
EVEN_IN, EVEN_IN_PAD = 5648, 5760
S5_GROUPS, S5_GROUP, S5_STATE = 32, 16, 64
S5_LANES = S5_GROUPS * S5_STATE
SCAN_SEG = 32
D_FF = 2816
ADAM_LR, ADAM_B1, ADAM_B2, ADAM_EPS, ADAM_WD, ADAM_STEP = 0.001, 0.9, 0.999, 1e-08, 0.01, 10

LANE = 128
VMEM_LIMIT = 56 * 1024 * 1024


def _params(sem=None, **kw):
    return pltpu.CompilerParams(dimension_semantics=sem, vmem_limit_bytes=VMEM_LIMIT, **kw)


def _tile(n, target, unit=LANE):
    if n <= target:
        return n
    t = (target // unit) * unit
    while t >= unit:
        if n % t == 0:
            return t
        t -= unit
    return n


def _silu(x):
    return x * jax.nn.sigmoid(x)


def _mm(a, b):
    return jnp.dot(a.astype(BF16), b.astype(BF16), preferred_element_type=F32)


def _mm_nt(a, b):
    return lax.dot_general(a.astype(BF16), b.astype(BF16), (((1,), (1,)), ((), ())), preferred_element_type=F32)


def _mm_tn(a, b):
    return lax.dot_general(a.astype(BF16), b.astype(BF16), (((0,), (0,)), ((), ())), preferred_element_type=F32)


def _dot_hi(a, b):
    return jnp.dot(a, b, precision=HIGHEST, preferred_element_type=F32)


def _dot_hi_tn(a, b):
    return lax.dot_general(a, b, (((0,), (0,)), ((), ())), precision=HIGHEST, preferred_element_type=F32)


def _bf16_parts(v):
    hi = v.astype(BF16)
    rest = v - hi.astype(F32)
    mid = rest.astype(BF16)
    return hi, mid, (rest - mid.astype(F32)).astype(BF16)


def _dot_parts(v, fixed, dims, v_first):
    fixed = fixed.astype(BF16)
    out = None
    for part in _bf16_parts(v):
        ops = (part, fixed) if v_first else (fixed, part)
        p = lax.dot_general(*ops, (dims, ((), ())), preferred_element_type=F32)
        out = p if out is None else out + p
    return out


@jax.custom_vjp
def _times_01(v, ones):
    return _dot_parts(v, ones, ((1,), (0,)), True)


_times_01.defvjp(lambda v, ones: (_times_01(v, ones), ones),
                 lambda ones, g: (_dot_parts(g, ones, ((1,), (1,)), True), jnp.zeros_like(ones)))


@jax.custom_vjp
def _01_times(ones, v):
    return _dot_parts(v, ones, ((1,), (0,)), False)


_01_times.defvjp(lambda ones, v: (_01_times(ones, v), ones),
                 lambda ones, g: (jnp.zeros_like(ones), _dot_parts(g, ones, ((0,), (0,)), False)))


MATMUL_VMEM = 44 * 1024 * 1024


def matmul(a, b, *, ta=False, tb=False, res=None, out_dtype=F32, name):
    m, k = (a.shape[1], a.shape[0]) if ta else a.shape
    n = b.shape[0] if tb else b.shape[1]
    assert (b.shape[1] if tb else b.shape[0]) == k, (a.shape, b.shape, ta, tb)
    tm = _tile(m, 1536)
    tn = _tile(n, 640)
    if tn < 384:
        tn = _tile(n, 1536)
    res_bytes = 0 if res is None else res.dtype.itemsize

    def vmem(tm, tn):
        return 2 * (tm * k * a.dtype.itemsize + tn * k * b.dtype.itemsize + tm * tn * (jnp.dtype(out_dtype).itemsize + res_bytes))

    while vmem(tm, tn) > MATMUL_VMEM and tm % (2 * LANE) == 0:
        tm //= 2
    assert vmem(tm, tn) <= MATMUL_VMEM, (name, tm, tn, k)
    a_spec = pl.BlockSpec((k, tm), lambda i, j: (0, i)) if ta else pl.BlockSpec((tm, k), lambda i, j: (i, 0))
    b_spec = pl.BlockSpec((tn, k), lambda i, j: (j, 0)) if tb else pl.BlockSpec((k, tn), lambda i, j: (0, j))
    o_spec = pl.BlockSpec((tm, tn), lambda i, j: (i, j))
    dims = (((0 if ta else 1,), (1 if tb else 0,)), ((), ()))
    has_res = res is not None

    def body(a_ref, b_ref, *rest):
        o_ref = rest[-1]
        out = lax.dot_general(a_ref[...].astype(BF16), b_ref[...].astype(BF16), dims, preferred_element_type=F32)
        if has_res:
            out = out + rest[0][...].astype(F32)
        o_ref[...] = out.astype(o_ref.dtype)

    ins = [a, b] + ([res] if has_res else [])
    specs = [a_spec, b_spec] + ([o_spec] if has_res else [])
    return pl.pallas_call(
        body, name=name, grid=(m // tm, n // tn), in_specs=specs, out_specs=o_spec,
        out_shape=jax.ShapeDtypeStruct((m, n), out_dtype), compiler_params=_params(("parallel", "parallel")),
    )(*ins)


class Cols(NamedTuple):
    arr: jax.Array
    w: int
    j: int


def _cols(a):
    return a if isinstance(a, Cols) else Cols(a, a.shape[1], 0)


def _row_spec(c, tl):
    return pl.BlockSpec((tl, c.w), lambda i, j=c.j: (i, j))


def _whole_spec(p):
    return pl.BlockSpec(p.shape, lambda i, nd=p.ndim: (0,) * nd)


def rowwise_fwd(fn, rows, aux, pars, consts, outs, *, name, tl):
    rows = [_cols(r) for r in rows + aux]
    whole = list(pars) + list(consts)
    n_rows = len(rows)
    n_whole = len(whole)
    length = rows[0].arr.shape[0]
    tl = min(tl, length)

    def body(*refs):
        vals = [r[...].astype(F32) for r in refs[:n_rows]] + [r[...] for r in refs[n_rows:n_rows + n_whole]]
        res = fn(*vals)
        for o_ref, v in zip(refs[n_rows + n_whole:], res, strict=True):
            o_ref[...] = v.astype(o_ref.dtype)

    return pl.pallas_call(
        body, name=name, grid=(length // tl,),
        in_specs=[_row_spec(r, tl) for r in rows] + [_whole_spec(p) for p in whole],
        out_specs=[pl.BlockSpec((tl, w), lambda i: (i, 0)) for w, _ in outs],
        out_shape=[jax.ShapeDtypeStruct((length, w), dt) for w, dt in outs],
        compiler_params=_params(("parallel",)),
    )(*[r.arr for r in rows], *whole)


def rowwise_bwd(fn, rows, aux, pars, consts, cots, drow_dtypes, *, name, tl, add=None, merge=False):
    rows = [_cols(r) for r in rows]
    aux = [_cols(r) for r in aux]
    cots = [_cols(r) for r in cots]
    n_r, n_a, n_p, n_c, n_t = len(rows), len(aux), len(pars), len(consts), len(cots)
    length = rows[0].arr.shape[0]
    tl = min(tl, length)
    has_add = add is not None
    widths = [r.w for r in rows]

    def body(*refs):
        pos = 0
        r_vals = [r[...].astype(F32) for r in refs[pos:pos + n_r]]; pos += n_r
        a_vals = [r[...].astype(F32) for r in refs[pos:pos + n_a]]; pos += n_a
        p_vals = [r[...].astype(F32) for r in refs[pos:pos + n_p]]; pos += n_p
        c_vals = [r[...] for r in refs[pos:pos + n_c]]; pos += n_c
        t_vals = [r[...].astype(F32) for r in refs[pos:pos + n_t]]; pos += n_t
        add_val = None
        if has_add:
            add_val = refs[pos][...].astype(F32); pos += 1
        n_dr = 1 if merge else n_r
        dr_refs = refs[pos:pos + n_dr]; pos += n_dr
        dp_refs = refs[pos:pos + n_p]

        def f(*rp):
            return fn(*rp[:n_r], *a_vals, *rp[n_r:], *c_vals)

        _, vjp = jax.vjp(f, *r_vals, *p_vals)
        grads = vjp(tuple(t_vals))
        drows = list(grads[:n_r])
        if has_add:
            drows[0] = drows[0] + add_val
        if merge:
            off = 0
            for w, d in zip(widths, drows):
                dr_refs[0][:, off:off + w] = d.astype(dr_refs[0].dtype)
                off += w
        else:
            for r, d in zip(dr_refs, drows):
                r[...] = d.astype(r.dtype)
        i = pl.program_id(0)
        for r, d in zip(dp_refs, grads[n_r:]):
            @pl.when(i == 0)
            def _(r=r, d=d):
                r[...] = d

            @pl.when(i > 0)
            def _(r=r, d=d):
                r[...] += d

    if merge:
        dr_specs = [pl.BlockSpec((tl, sum(widths)), lambda i: (i, 0))]
        dr_shapes = [jax.ShapeDtypeStruct((length, sum(widths)), drow_dtypes[0])]
    else:
        dr_specs = [pl.BlockSpec((tl, w), lambda i: (i, 0)) for w in widths]
        dr_shapes = [jax.ShapeDtypeStruct((length, w), dt) for w, dt in zip(widths, drow_dtypes)]
    ins = [r.arr for r in rows + aux] + list(pars) + list(consts) + [r.arr for r in cots] + ([add] if has_add else [])
    specs = ([_row_spec(r, tl) for r in rows + aux] + [_whole_spec(p) for p in list(pars) + list(consts)]
             + [_row_spec(r, tl) for r in cots] + ([pl.BlockSpec((tl, add.shape[1]), lambda i: (i, 0))] if has_add else []))
    return pl.pallas_call(
        body, name=name, grid=(length // tl,), in_specs=specs,
        out_specs=dr_specs + [_whole_spec(p) for p in pars],
        out_shape=dr_shapes + [jax.ShapeDtypeStruct(p.shape, F32) for p in pars],
        compiler_params=_params(("arbitrary",)),
    )(*ins)


def whole_fwd(fn, ins, out_shapes, *, name):
    n_in = len(ins)

    def body(*refs):
        res = fn(*[r[...] for r in refs[:n_in]])
        for o_ref, v in zip(refs[n_in:], res, strict=True):
            o_ref[...] = v

    return pl.pallas_call(body, name=name, out_shape=[jax.ShapeDtypeStruct(s, F32) for s in out_shapes],
                          compiler_params=_params())(*ins)


def whole_bwd(fn, ins, n_diff, cots, *, name):
    n_in, n_t = len(ins), len(cots)

    def body(*refs):
        vals = [r[...] for r in refs[:n_in]]
        t_vals = [r[...] for r in refs[n_in:n_in + n_t]]
        _, vjp = jax.vjp(lambda *d: fn(*d, *vals[n_diff:]), *vals[:n_diff])
        for o_ref, g in zip(refs[n_in + n_t:], vjp(tuple(t_vals)), strict=True):
            o_ref[...] = g

    return pl.pallas_call(body, name=name, out_shape=[jax.ShapeDtypeStruct(a.shape, F32) for a in ins[:n_diff]],
                          compiler_params=_params())(*ins, *cots)


CONV_ROWS = 256


def _conv_geometry(x, w, cw, off):
    width = w.shape[1]
    x = Cols(x, width, 0)
    length = x.arr.shape[0]
    taps = w.shape[0]
    pad = -(-(taps - 1) // 8) * 8
    assert off % cw == 0 and width % cw == 0, (off, width, cw)
    return x, length, taps, pad, off // cw


def _conv_taps(xp_ref, w_ref, base, taps, pad, init, lanes=slice(None)):
    acc = init
    for k in range(taps):
        acc = acc + w_ref[k:k + 1, lanes] * xp_ref[pl.ds(base + pad - (taps - 1) + k, init.shape[0]), :]
    return acc


def conv_fwd(x, w, b, *, act, name, off=0, cw=LANE, out_dtype=F32):
    x, length, taps, pad, jb = _conv_geometry(x, w, cw, off)
    rc = min(CONV_ROWS, length)

    def body(x_ref, w_ref, b_ref, o_ref, xp_ref):
        xp_ref[0:pad, :] = jnp.zeros((pad, cw), F32)
        xp_ref[pad:pad + length, :] = x_ref[...].astype(F32)

        def chunk(r, carry):
            base = pl.multiple_of(r * rc, rc)
            acc = _conv_taps(xp_ref, w_ref, base, taps, pad, jnp.broadcast_to(b_ref[...], (rc, cw)))
            if act:
                acc = _silu(acc)
            o_ref[pl.ds(base, rc), :] = acc.astype(o_ref.dtype)
            return carry

        lax.fori_loop(0, length // rc, chunk, 0)

    return pl.pallas_call(
        body, name=name, grid=(x.w // cw,),
        in_specs=[pl.BlockSpec((length, cw), lambda j: (0, jb + j)), pl.BlockSpec((taps, cw), lambda j: (0, j)),
                  pl.BlockSpec((1, cw), lambda j: (0, j))],
        out_specs=pl.BlockSpec((length, cw), lambda j: (0, j)),
        out_shape=jax.ShapeDtypeStruct((length, x.w), out_dtype),
        scratch_shapes=[pltpu.VMEM((pad + length, cw), F32)],
        compiler_params=_params(("parallel",)),
    )(x.arr, w, b)


def conv_bwd(x, w, b, dy, *, act, name, off=0, cw=LANE, dx_dtype=F32):
    x, length, taps, pad, jb = _conv_geometry(x, w, cw, off)
    rc = min(CONV_ROWS, length)

    def body(x_ref, w_ref, b_ref, dy_ref, dx_ref, dw_ref, db_ref, xp_ref, gp_ref):
        xp_ref[0:pad, :] = jnp.zeros((pad, cw), F32)
        xp_ref[pad:pad + length, :] = x_ref[...].astype(F32)
        gp_ref[length:length + pad, :] = jnp.zeros((pad, cw), F32)
        if act:
            def pre_chunk(r, carry):
                base = pl.multiple_of(r * rc, rc)
                pre = _conv_taps(xp_ref, w_ref, base, taps, pad, jnp.broadcast_to(b_ref[...], (rc, cw)))
                sig = jax.nn.sigmoid(pre)
                gp_ref[pl.ds(base, rc), :] = dy_ref[pl.ds(base, rc), :].astype(F32) * (sig * (1.0 + pre * (1.0 - sig)))
                return carry

            lax.fori_loop(0, length // rc, pre_chunk, 0)
        else:
            gp_ref[0:length, :] = dy_ref[...].astype(F32)
        dw_ref[...] = jnp.zeros((taps, cw), F32)
        db_ref[...] = jnp.zeros((1, cw), F32)

        def chunk(r, carry):
            base = pl.multiple_of(r * rc, rc)
            acc = jnp.zeros((rc, cw), F32)
            g = gp_ref[pl.ds(base, rc), :]
            for k in range(taps):
                acc = acc + w_ref[k:k + 1, :] * gp_ref[pl.ds(base + (taps - 1) - k, rc), :]
                xs = xp_ref[pl.ds(base + pad - (taps - 1) + k, rc), :]
                dw_ref[k:k + 1, :] += jnp.sum(g * xs, axis=0, keepdims=True)
            db_ref[...] += jnp.sum(g, axis=0, keepdims=True)
            dx_ref[pl.ds(base, rc), :] = acc.astype(dx_ref.dtype)
            return carry

        lax.fori_loop(0, length // rc, chunk, 0)

    dy = _cols(dy)
    assert dy.j == 0 and dy.w == x.w
    return pl.pallas_call(
        body, name=name, grid=(x.w // cw,),
        in_specs=[pl.BlockSpec((length, cw), lambda j: (0, jb + j)), pl.BlockSpec((taps, cw), lambda j: (0, j)),
                  pl.BlockSpec((1, cw), lambda j: (0, j)), pl.BlockSpec((length, cw), lambda j: (0, j))],
        out_specs=[pl.BlockSpec((length, cw), lambda j: (0, j)), pl.BlockSpec((taps, cw), lambda j: (0, j)),
                   pl.BlockSpec((1, cw), lambda j: (0, j))],
        out_shape=[jax.ShapeDtypeStruct((length, x.w), dx_dtype), jax.ShapeDtypeStruct((taps, x.w), F32),
                   jax.ShapeDtypeStruct((1, x.w), F32)],
        scratch_shapes=[pltpu.VMEM((pad + length, cw), F32), pltpu.VMEM((length + pad, cw), F32)],
        compiler_params=_params(("parallel",)),
    )(x.arr, w, b, dy.arr)


def _conv_transpose(xp_ref, gp_ref, w_ref, dx_ref, dw_ref, db_ref, lanes, length, taps, pad, rc):
    dw_ref[:, lanes] = jnp.zeros((taps, LANE), F32)
    db_ref[:, lanes] = jnp.zeros((1, LANE), F32)

    def chunk(r, carry):
        base = pl.multiple_of(r * rc, rc)
        acc = jnp.zeros((rc, LANE), F32)
        g = gp_ref[pl.ds(base, rc), :]
        for k in range(taps):
            acc = acc + w_ref[k:k + 1, lanes] * gp_ref[pl.ds(base + (taps - 1) - k, rc), :]
            xs = xp_ref[pl.ds(base + pad - (taps - 1) + k, rc), :]
            dw_ref[k:k + 1, lanes] += jnp.sum(g * xs, axis=0, keepdims=True)
        db_ref[:, lanes] += jnp.sum(g, axis=0, keepdims=True)
        dx_ref[pl.ds(base, rc), lanes] = acc.astype(dx_ref.dtype)
        return carry

    lax.fori_loop(0, length // rc, chunk, 0)


LANE_PAIR_ROWS = 1024


def ffn_interleave(a, name=None):
    rows, width = a.shape
    nb = width // (2 * LANE)
    if rows < LANE_PAIR_ROWS:
        return a.reshape(rows, 2, nb, LANE).swapaxes(1, 2).reshape(a.shape)

    def body(g_ref, u_ref, o_ref):
        o_ref[:, 0:LANE] = g_ref[...]
        o_ref[:, LANE:2 * LANE] = u_ref[...]

    tr = LANE_PAIR_ROWS
    return pl.pallas_call(
        body, name=name, grid=(rows // tr, nb),
        in_specs=[pl.BlockSpec((tr, LANE), lambda i, j: (i, j)), pl.BlockSpec((tr, LANE), lambda i, j: (i, nb + j))],
        out_specs=pl.BlockSpec((tr, 2 * LANE), lambda i, j: (i, j)),
        out_shape=jax.ShapeDtypeStruct(a.shape, a.dtype), compiler_params=_params(("parallel", "parallel")),
    )(a, a)


def ffn_deinterleave(a, name=None):
    rows, width = a.shape
    nb = width // (2 * LANE)
    if rows < LANE_PAIR_ROWS:
        return a.reshape(rows, nb, 2, LANE).swapaxes(1, 2).reshape(a.shape)

    def body(a_ref, o_ref):
        @pl.when(pl.program_id(1) == 0)
        def _():
            o_ref[...] = a_ref[:, 0:LANE]

        @pl.when(pl.program_id(1) == 1)
        def _():
            o_ref[...] = a_ref[:, LANE:2 * LANE]

    tr = LANE_PAIR_ROWS
    return pl.pallas_call(
        body, name=name, grid=(rows // tr, 2, nb),
        in_specs=[pl.BlockSpec((tr, 2 * LANE), lambda i, h, j: (i, j))],
        out_specs=pl.BlockSpec((tr, LANE), lambda i, h, j: (i, h * nb + j)),
        out_shape=jax.ShapeDtypeStruct(a.shape, a.dtype), compiler_params=_params(("parallel", "parallel", "parallel")),
    )(a)


GATE, UP = slice(0, LANE), slice(LANE, 2 * LANE)


def _ffn_geometry(a, w):
    length, width = a.shape
    taps = w.shape[0]
    return length, width, width // (2 * LANE), taps, -(-(taps - 1) // 8) * 8, min(CONV_ROWS, length)


def _ffn_pre(xg_ref, xu_ref, w_ref, b_ref, base, taps, pad, rc):
    gate = _conv_taps(xg_ref, w_ref, base, taps, pad, jnp.broadcast_to(b_ref[:, GATE], (rc, LANE)), GATE)
    up = _conv_taps(xu_ref, w_ref, base, taps, pad, jnp.broadcast_to(b_ref[:, UP], (rc, LANE)), UP)
    return gate, up


def ffn_conv_act(a, w, b, *, name):
    length, width, nb, taps, pad, rc = _ffn_geometry(a, w)

    def body(a_ref, w_ref, b_ref, o_ref, xg_ref, xu_ref):
        for xp_ref, lanes in ((xg_ref, GATE), (xu_ref, UP)):
            xp_ref[0:pad, :] = jnp.zeros((pad, LANE), F32)
            xp_ref[pad:pad + length, :] = a_ref[:, lanes].astype(F32)

        def chunk(r, carry):
            base = pl.multiple_of(r * rc, rc)
            gate, up = _ffn_pre(xg_ref, xu_ref, w_ref, b_ref, base, taps, pad, rc)
            o_ref[pl.ds(base, rc), :] = (_silu(gate) * up).astype(o_ref.dtype)
            return carry

        lax.fori_loop(0, length // rc, chunk, 0)

    pair = lambda rows: pl.BlockSpec((rows, 2 * LANE), lambda j: (0, j))
    return pl.pallas_call(
        body, name=name, grid=(nb,), in_specs=[pair(length), pair(taps), pair(1)],
        out_specs=pl.BlockSpec((length, LANE), lambda j: (0, j)),
        out_shape=jax.ShapeDtypeStruct((length, width // 2), BF16),
        scratch_shapes=[pltpu.VMEM((pad + length, LANE), F32), pltpu.VMEM((pad + length, LANE), F32)],
        compiler_params=_params(("parallel",)),
    )(a, w, b)


def ffn_conv_act_bwd(a, w, b, dact, *, name):
    length, width, nb, taps, pad, rc = _ffn_geometry(a, w)

    def body(a_ref, w_ref, b_ref, dy_ref, da_ref, dw_ref, db_ref, xg_ref, xu_ref, gg_ref, gu_ref):
        for xp_ref, lanes in ((xg_ref, GATE), (xu_ref, UP)):
            xp_ref[0:pad, :] = jnp.zeros((pad, LANE), F32)
            xp_ref[pad:pad + length, :] = a_ref[:, lanes].astype(F32)
        for gp_ref in (gg_ref, gu_ref):
            gp_ref[length:length + pad, :] = jnp.zeros((pad, LANE), F32)

        def pre_chunk(r, carry):
            base = pl.multiple_of(r * rc, rc)
            gate, up = _ffn_pre(xg_ref, xu_ref, w_ref, b_ref, base, taps, pad, rc)
            sig = jax.nn.sigmoid(gate)
            dy = dy_ref[pl.ds(base, rc), :]
            gg_ref[pl.ds(base, rc), :] = dy * up * (sig * (1.0 + gate * (1.0 - sig)))
            gu_ref[pl.ds(base, rc), :] = dy * (gate * sig)
            return carry

        lax.fori_loop(0, length // rc, pre_chunk, 0)
        _conv_transpose(xg_ref, gg_ref, w_ref, da_ref, dw_ref, db_ref, GATE, length, taps, pad, rc)
        _conv_transpose(xu_ref, gu_ref, w_ref, da_ref, dw_ref, db_ref, UP, length, taps, pad, rc)

    pair = lambda rows: pl.BlockSpec((rows, 2 * LANE), lambda j: (0, j))
    return pl.pallas_call(
        body, name=name, grid=(nb,),
        in_specs=[pair(length), pair(taps), pair(1), pl.BlockSpec((length, LANE), lambda j: (0, j))],
        out_specs=[pair(length), pair(taps), pair(1)],
        out_shape=[jax.ShapeDtypeStruct((length, width), BF16), jax.ShapeDtypeStruct((taps, width), F32),
                   jax.ShapeDtypeStruct((1, width), F32)],
        scratch_shapes=[pltpu.VMEM((pad + length, LANE), F32), pltpu.VMEM((pad + length, LANE), F32),
                        pltpu.VMEM((length + pad, LANE), F32), pltpu.VMEM((length + pad, LANE), F32)],
        compiler_params=_params(("parallel",)),
    )(a, w, b, dact)


def _retention_consts():
    h = np.arange(RET_HEADS, dtype=np.float32)
    log_g = np.log1p(-(2.0 ** (-5.0 - h))).astype(np.float32)
    idx = np.arange(CHUNK, dtype=np.float32)
    diff = idx[:, None] - idx[None, :]
    intra = np.where(diff[None] >= 0, np.exp(np.maximum(diff, 0.0)[None] * log_g[:, None, None]), 0.0)
    zeta = np.exp((CHUNK - 1 - idx)[None, :] * log_g[:, None])
    xi = np.exp((idx + 1)[None, :] * log_g[:, None])
    decay = np.exp(CHUNK * log_g)
    zeta = np.broadcast_to(zeta[:, :, None], (RET_HEADS, CHUNK, RET_DK))
    xi = np.broadcast_to(xi[:, :, None], (RET_HEADS, CHUNK, RET_DV))
    return (jnp.asarray(intra, F32), jnp.asarray(zeta, F32), jnp.asarray(xi, F32), [float(d) for d in decay])


def _rotary_tables(length):
    inv = ROPE_BASE ** (-jnp.arange(0, RET_DK, 2, dtype=F32) / RET_DK)
    ang = jnp.arange(length).astype(F32)[:, None] * inv[None, :]
    cos, sin = jnp.cos(ang), jnp.sin(ang)
    return jnp.concatenate([cos, cos], axis=1), jnp.concatenate([-sin, sin], axis=1)


def _rot(x, cos2, sin2):
    return x * cos2 + pltpu.roll(x, RET_DK // 2, 1) * sin2


def _rot_t(y, cos2, sin2):
    return y * cos2 + pltpu.roll(y * sin2, RET_DK // 2, 1)


def _ret_chunk(q, k, v, g, state, intra, zeta, xi, decay):
    s = _mm_nt(q, k) * intra
    kv = _mm_tn(k * zeta, v)
    o = _mm(s, v) + _mm(q, state) * xi
    oc = o - jnp.mean(o, axis=-1, keepdims=True)
    r = oc * lax.rsqrt(jnp.mean(oc * oc, axis=-1, keepdims=True) + EPS)
    return _silu(g) * r, state * decay + kv


RET_QK, RET_V = RET_HEADS * RET_DK, RET_HEADS * RET_DV


def _ret_specs(rev, nc):
    def cidx(c):
        return nc - 1 - c if rev else c
    whole = lambda shape: pl.BlockSpec(shape, lambda c: (0,) * len(shape))
    return [
        pl.BlockSpec((CHUNK, RET_QK), lambda c: (cidx(c), 0)),
        pl.BlockSpec((CHUNK, RET_QK), lambda c: (cidx(c), 1)),
        pl.BlockSpec((CHUNK, RET_V), lambda c: (cidx(c), 1)),
        pl.BlockSpec((CHUNK, RET_V), lambda c: (cidx(c), 2)),
        pl.BlockSpec((CHUNK, RET_DK), lambda c: (cidx(c), 0)),
        pl.BlockSpec((CHUNK, RET_DK), lambda c: (cidx(c), 0)),
        whole((RET_HEADS, CHUNK, CHUNK)), whole((RET_HEADS, CHUNK, RET_DK)), whole((RET_HEADS, CHUNK, RET_DV)),
    ], cidx


def _head(ref, h, width):
    return ref[:, h * width:(h + 1) * width]


def retention_fwd(proj, cos2, sin2):
    length = proj.shape[0]
    nc = length // CHUNK
    intra, zeta, xi, decays = _retention_consts()
    specs, _ = _ret_specs(False, nc)
    scale = RET_DK ** -0.5

    def body(q_ref, k_ref, v_ref, g_ref, cos_ref, sin_ref, intra_ref, zeta_ref, xi_ref, y_ref, st_ref, state):
        @pl.when(pl.program_id(0) == 0)
        def _():
            state[...] = jnp.zeros_like(state)

        cos2v, sin2v = cos_ref[...], sin_ref[...]
        for h in range(RET_HEADS):
            q = _rot(_head(q_ref, h, RET_DK), cos2v, sin2v)
            k = _rot(_head(k_ref, h, RET_DK), cos2v, sin2v) * scale
            st_ref[h, 0] = state[h]
            y, new_state = _ret_chunk(q, k, _head(v_ref, h, RET_DV), _head(g_ref, h, RET_DV), state[h],
                                      intra_ref[h], zeta_ref[h], xi_ref[h], decays[h])
            y_ref[:, h * RET_DV:(h + 1) * RET_DV] = y.astype(y_ref.dtype)
            state[h] = new_state

    return pl.pallas_call(
        body, name="retention_fwd", grid=(nc,), in_specs=specs,
        out_specs=[pl.BlockSpec((CHUNK, RET_V), lambda c: (c, 0)),
                   pl.BlockSpec((RET_HEADS, 1, RET_DK, RET_DV), lambda c: (0, c, 0, 0))],
        out_shape=[jax.ShapeDtypeStruct((length, RET_V), BF16),
                   jax.ShapeDtypeStruct((RET_HEADS, nc, RET_DK, RET_DV), F32)],
        scratch_shapes=[pltpu.VMEM((RET_HEADS, RET_DK, RET_DV), F32)],
        compiler_params=_params(("arbitrary",)),
    )(proj, proj, proj, proj, cos2, sin2, intra, zeta, xi)


def retention_bwd(proj, cos2, sin2, states, dmix):
    length = proj.shape[0]
    nc = length // CHUNK
    intra, zeta, xi, decays = _retention_consts()
    specs, cidx = _ret_specs(True, nc)
    scale = RET_DK ** -0.5

    def body(q_ref, k_ref, v_ref, g_ref, cos_ref, sin_ref, intra_ref, zeta_ref, xi_ref, st_ref, dy_ref,
             dq_ref, dk_ref, dv_ref, dg_ref, dstate):
        @pl.when(pl.program_id(0) == 0)
        def _():
            dstate[...] = jnp.zeros_like(dstate)

        cos2v, sin2v = cos_ref[...], sin_ref[...]
        for h in range(RET_HEADS):
            q = _rot(_head(q_ref, h, RET_DK), cos2v, sin2v)
            k = _rot(_head(k_ref, h, RET_DK), cos2v, sin2v) * scale
            intra_v, zeta_v, xi_v, decay = intra_ref[h], zeta_ref[h], xi_ref[h], decays[h]
            _, vjp = jax.vjp(lambda q, k, v, g, s: _ret_chunk(q, k, v, g, s, intra_v, zeta_v, xi_v, decay),
                             q, k, _head(v_ref, h, RET_DV), _head(g_ref, h, RET_DV), st_ref[h, 0])
            dq, dk, dv, dg, ds = vjp((_head(dy_ref, h, RET_DV).astype(F32), dstate[h]))
            dq_ref[:, h * RET_DK:(h + 1) * RET_DK] = _rot_t(dq, cos2v, sin2v).astype(dq_ref.dtype)
            dk_ref[:, h * RET_DK:(h + 1) * RET_DK] = _rot_t(dk * scale, cos2v, sin2v).astype(dk_ref.dtype)
            dv_ref[:, h * RET_DV:(h + 1) * RET_DV] = dv.astype(dv_ref.dtype)
            dg_ref[:, h * RET_DV:(h + 1) * RET_DV] = dg.astype(dg_ref.dtype)
            dstate[h] = ds

    specs = specs + [pl.BlockSpec((RET_HEADS, 1, RET_DK, RET_DV), lambda c: (0, cidx(c), 0, 0)),
                     pl.BlockSpec((CHUNK, RET_V), lambda c: (cidx(c), 0))]
    row = lambda width: pl.BlockSpec((CHUNK, width), lambda c: (cidx(c), 0))
    return pl.pallas_call(
        body, name="retention_bwd", grid=(nc,), in_specs=specs,
        out_specs=[row(RET_QK), row(RET_QK), row(RET_V), row(RET_V)],
        out_shape=[jax.ShapeDtypeStruct((length, RET_QK), BF16), jax.ShapeDtypeStruct((length, RET_QK), BF16),
                   jax.ShapeDtypeStruct((length, RET_V), BF16), jax.ShapeDtypeStruct((length, RET_V), BF16)],
        scratch_shapes=[pltpu.VMEM((RET_HEADS, RET_DK, RET_DV), F32)],
        compiler_params=_params(("arbitrary",)),
    )(proj, proj, proj, proj, cos2, sin2, intra, zeta, xi, states, dmix)


def _ssd_consts():
    tri = np.tril(np.ones((CHUNK, CHUNK), np.float32))
    expand = np.zeros((LANE, SSM_DINNER), np.float32)
    for h in range(SSM_HEADS):
        expand[h, h * SSM_P:(h + 1) * SSM_P] = 1.0
    return jnp.asarray(tri), jnp.asarray(tri.T.copy()), jnp.asarray(expand)


def _ssd_chunk(xs, bm, cm, dtr, z, state, dt_bias, a_log, d_skip, norm_w, tri, tri_t, expand):
    gw = SSM_DINNER // SSM_GROUPS
    dt = jax.nn.softplus(dtr + dt_bias)
    a_neg = -jnp.exp(a_log)
    da = dt * a_neg
    acs = _dot_hi(tri, da)
    acs_t = _dot_hi_tn(da, tri_t)
    dt_x = _times_01(dt, expand)
    a_x = jnp.mean(_dot_hi(jnp.broadcast_to(a_neg, (8, LANE)), expand), axis=0, keepdims=True)
    da_x = dt_x * a_x
    acs_x = _01_times(tri, da_x)
    tot_x = jnp.sum(da_x, axis=0, keepdims=True)
    x_dt = xs * dt_x
    x_dec = x_dt * jnp.exp(tot_x - acs_x)
    e_acs = jnp.exp(acs_x)
    e_tot = jnp.exp(tot_x)
    lane = lax.broadcasted_iota(jnp.int32, (CHUNK, LANE), 1)
    sub = lax.broadcasted_iota(jnp.int32, (CHUNK, LANE), 0)
    causal = sub >= lane
    ys, new_states = [], []
    for g in range(SSM_GROUPS):
        bg = bm[:, g * SSM_N:(g + 1) * SSM_N]
        cg = cm[:, g * SSM_N:(g + 1) * SSM_N]
        sg = state[:, g * gw:(g + 1) * gw]
        cb = _mm_nt(cg, bg)
        y_off = _mm(cg, sg) * e_acs[:, g * gw:(g + 1) * gw]
        new_states.append(sg * e_tot[:, g * gw:(g + 1) * gw] + _mm_tn(bg, x_dec[:, g * gw:(g + 1) * gw]))
        pairs = []
        for p in range(gw // LANE):
            hp = g * (gw // LANE) + p
            xp = x_dt[:, hp * LANE:(hp + 1) * LANE]
            halves = []
            for head in (2 * hp, 2 * hp + 1):
                col = jnp.sum(jnp.where(lane == head, acs, 0.0), axis=1, keepdims=True)
                row = jnp.sum(jnp.where(sub == head, acs_t, 0.0), axis=0, keepdims=True)
                decay = jnp.exp(jnp.where(causal, col - row, -1e30))
                halves.append(_mm(cb * decay, xp))
            pairs.append(jnp.where(lane < SSM_P, halves[0], halves[1]))
        ys.append(jnp.concatenate(pairs, axis=1) + y_off)
    d_x = jnp.mean(_dot_hi(jnp.broadcast_to(d_skip, (8, LANE)), expand), axis=0, keepdims=True)
    y = (jnp.concatenate(ys, axis=1) + d_x * xs) * _silu(z)
    normed = []
    for g in range(SSM_GROUPS):
        yg = y[:, g * gw:(g + 1) * gw]
        normed.append(yg * lax.rsqrt(jnp.mean(yg * yg, axis=-1, keepdims=True) + EPS))
    return jnp.concatenate(normed, axis=1) * norm_w, jnp.concatenate(new_states, axis=1)


XBC = SSM_DINNER + 2 * SSM_GROUPS * SSM_N


def _ssd_specs(rev, nc):
    def cidx(c):
        return nc - 1 - c if rev else c
    row = lambda w, j: pl.BlockSpec((CHUNK, w), lambda c: (cidx(c), j))
    whole = lambda shape: pl.BlockSpec(shape, lambda c: (0,) * len(shape))
    return [row(XBC, 0), row(LANE, 5632 // LANE), row(SSM_DINNER, 3),
            whole((1, LANE)), whole((1, LANE)), whole((1, LANE)), whole((1, SSM_DINNER)),
            whole((CHUNK, CHUNK)), whole((CHUNK, CHUNK)), whole((LANE, SSM_DINNER))], cidx


def ssd_fwd(xbc, proj, dt_bias, a_log, d_skip, norm_w):
    length = proj.shape[0]
    nc = length // CHUNK
    tri, tri_t, expand = _ssd_consts()
    specs, _ = _ssd_specs(False, nc)

    def body(xbc_ref, dt_ref, z_ref, dtb_ref, alog_ref, d_ref, nw_ref, tri_ref, trit_ref, e_ref, y_ref, st_ref, state):
        @pl.when(pl.program_id(0) == 0)
        def _():
            state[...] = jnp.zeros_like(state)

        st_ref[0] = state[...]
        y, new_state = _ssd_chunk(
            xbc_ref[:, 0:SSM_DINNER], xbc_ref[:, SSM_DINNER:SSM_DINNER + 256], xbc_ref[:, SSM_DINNER + 256:XBC],
            dt_ref[...], z_ref[...], state[...], dtb_ref[...], alog_ref[...], d_ref[...], nw_ref[...],
            tri_ref[...], trit_ref[...], e_ref[...])
        y_ref[...] = y.astype(y_ref.dtype)
        state[...] = new_state

    return pl.pallas_call(
        body, name="ssd_fwd", grid=(nc,), in_specs=specs,
        out_specs=[pl.BlockSpec((CHUNK, SSM_DINNER), lambda c: (c, 0)),
                   pl.BlockSpec((1, SSM_N, SSM_DINNER), lambda c: (c, 0, 0))],
        out_shape=[jax.ShapeDtypeStruct((length, SSM_DINNER), BF16),
                   jax.ShapeDtypeStruct((nc, SSM_N, SSM_DINNER), F32)],
        scratch_shapes=[pltpu.VMEM((SSM_N, SSM_DINNER), F32)],
        compiler_params=_params(("arbitrary",)),
    )(xbc, proj, proj, dt_bias, a_log, d_skip, norm_w, tri, tri_t, expand)


def ssd_bwd(xbc, proj, dt_bias, a_log, d_skip, norm_w, states, dmix):
    length = proj.shape[0]
    nc = length // CHUNK
    tri, tri_t, expand = _ssd_consts()
    specs, cidx = _ssd_specs(True, nc)

    def body(xbc_ref, dt_ref, z_ref, dtb_ref, alog_ref, d_ref, nw_ref, tri_ref, trit_ref, e_ref, st_ref, dy_ref,
             dxbc_ref, ddt_ref, dz_ref, ddtb_ref, dalog_ref, dd_ref, dnw_ref, dstate):
        c = pl.program_id(0)

        @pl.when(c == 0)
        def _():
            dstate[...] = jnp.zeros_like(dstate)

        tri_v, trit_v, e_v = tri_ref[...], trit_ref[...], e_ref[...]
        _, vjp = jax.vjp(
            lambda *a: _ssd_chunk(*a, tri_v, trit_v, e_v),
            xbc_ref[:, 0:SSM_DINNER], xbc_ref[:, SSM_DINNER:SSM_DINNER + 256], xbc_ref[:, SSM_DINNER + 256:XBC],
            dt_ref[...], z_ref[...], st_ref[0], dtb_ref[...], alog_ref[...], d_ref[...], nw_ref[...])
        dxs, dbm, dcm, ddt, dz, ds, ddtb, dalog, dd, dnw = vjp((dy_ref[...].astype(F32), dstate[...]))
        dxbc_ref[:, 0:SSM_DINNER] = dxs
        dxbc_ref[:, SSM_DINNER:SSM_DINNER + 256] = dbm
        dxbc_ref[:, SSM_DINNER + 256:XBC] = dcm
        ddt_ref[...] = ddt.astype(ddt_ref.dtype)
        dz_ref[...] = dz.astype(dz_ref.dtype)
        dstate[...] = ds
        for r, d in ((ddtb_ref, ddtb), (dalog_ref, dalog), (dd_ref, dd), (dnw_ref, dnw)):
            @pl.when(c == 0)
            def _(r=r, d=d):
                r[...] = d

            @pl.when(c > 0)
            def _(r=r, d=d):
                r[...] += d

    whole = lambda shape: pl.BlockSpec(shape, lambda c: (0,) * len(shape))
    specs = specs + [pl.BlockSpec((1, SSM_N, SSM_DINNER), lambda c: (cidx(c), 0, 0)),
                     pl.BlockSpec((CHUNK, SSM_DINNER), lambda c: (cidx(c), 1))]
    return pl.pallas_call(
        body, name="ssd_bwd", grid=(nc,), in_specs=specs,
        out_specs=[pl.BlockSpec((CHUNK, XBC), lambda c: (cidx(c), 0)), pl.BlockSpec((CHUNK, LANE), lambda c: (cidx(c), 0)),
                   pl.BlockSpec((CHUNK, SSM_DINNER), lambda c: (cidx(c), 0)),
                   whole((1, LANE)), whole((1, LANE)), whole((1, LANE)), whole((1, SSM_DINNER))],
        out_shape=[jax.ShapeDtypeStruct((length, XBC), F32), jax.ShapeDtypeStruct((length, LANE), BF16),
                   jax.ShapeDtypeStruct((length, SSM_DINNER), BF16),
                   jax.ShapeDtypeStruct((1, LANE), F32), jax.ShapeDtypeStruct((1, LANE), F32),
                   jax.ShapeDtypeStruct((1, LANE), F32), jax.ShapeDtypeStruct((1, SSM_DINNER), F32)],
        scratch_shapes=[pltpu.VMEM((SSM_N, SSM_DINNER), F32)],
        compiler_params=_params(("arbitrary",)),
    )(xbc, proj, proj, dt_bias, a_log, d_skip, norm_w, tri, tri_t, expand, states, dmix)


def _cmul(ar, ai, br, bi):
    return ar * br - ai * bi, ar * bi + ai * br


def s5_scan(b_re, b_im, a_re, a_im, *, reverse=False, states=None, name, lw=256):
    length, lanes = b_re.shape
    nk = length // SCAN_SEG
    with_da = states is not None
    assert reverse or not with_da

    def shift(v):
        sub = lax.broadcasted_iota(jnp.int32, v.shape, 0)
        if reverse:
            return jnp.where(sub == SCAN_SEG - 1, 0.0, pltpu.roll(v, SCAN_SEG - 1, 0))
        return jnp.where(sub == 0, 0.0, pltpu.roll(v, 1, 0))

    def body(*refs):
        if with_da:
            bre_ref, bim_ref, are_ref, aim_ref, sre_ref, sim_ref, xre_ref, xim_ref, dare_ref, daim_ref = refs
        else:
            bre_ref, bim_ref, are_ref, aim_ref, xre_ref, xim_ref = refs
        ar = jnp.broadcast_to(are_ref[...], (SCAN_SEG, lw))
        ai = jnp.broadcast_to(aim_ref[...], (SCAN_SEG, lw))

        def tile(i):
            k = (nk - 1 - i) if reverse else i
            return pl.ds(pl.multiple_of(k * SCAN_SEG, SCAN_SEG), SCAN_SEG)

        def local(i, carry):
            xr, xi, pr, pi = carry
            rows = tile(i)
            mr, mi = _cmul(ar, ai, xr, xi)
            xr, xi = mr + bre_ref[rows, :], mi + bim_ref[rows, :]
            xre_ref[rows, :] = xr
            xim_ref[rows, :] = xi
            pr, pi = _cmul(ar, ai, pr, pi)
            return xr, xi, pr, pi

        zero = jnp.zeros((SCAN_SEG, lw), F32)
        one = jnp.ones((SCAN_SEG, lw), F32)
        er, ei, pr, pi = lax.fori_loop(0, nk, local, (zero, zero, one, zero))
        cr, ci = zero, zero
        for _ in range(SCAN_SEG - 1):
            mr, mi = _cmul(pr, pi, cr, ci)
            cr, ci = shift(er + mr), shift(ei + mi)

        def fix(i, carry):
            pr, pi, dr, di = carry
            rows = tile(i)
            pr, pi = _cmul(ar, ai, pr, pi)
            mr, mi = _cmul(pr, pi, cr, ci)
            xr, xi = xre_ref[rows, :] + mr, xim_ref[rows, :] + mi
            xre_ref[rows, :] = xr
            xim_ref[rows, :] = xi
            if with_da:
                k = nk - 1 - i
                prev = pl.ds(pl.multiple_of(jnp.maximum(k - 1, 0) * SCAN_SEG, SCAN_SEG), SCAN_SEG)
                last = pl.ds((nk - 1) * SCAN_SEG, SCAN_SEG)
                sub = lax.broadcasted_iota(jnp.int32, (SCAN_SEG, lw), 0)
                wr = jnp.where(sub == 0, 0.0, pltpu.roll(sre_ref[last, :], 1, 0))
                wi = jnp.where(sub == 0, 0.0, pltpu.roll(sim_ref[last, :], 1, 0))
                sr = jnp.where(k == 0, wr, sre_ref[prev, :])
                si = jnp.where(k == 0, wi, sim_ref[prev, :])
                dr, di = dr + xr * sr + xi * si, di + xi * sr - xr * si
            return pr, pi, dr, di

        _, _, dr, di = lax.fori_loop(0, nk, fix, (one, zero, zero, zero))
        if with_da:
            dare_ref[...] = jnp.sum(dr, axis=0, keepdims=True)
            daim_ref[...] = jnp.sum(di, axis=0, keepdims=True)

    col = pl.BlockSpec((length, lw), lambda j: (0, j))
    vec = pl.BlockSpec((1, lw), lambda j: (0, j))
    ins = [b_re, b_im, a_re, a_im] + (list(states) if with_da else [])
    in_specs = [col, col, vec, vec] + ([col, col] if with_da else [])
    out_specs = [col, col] + ([vec, vec] if with_da else [])
    out_shape = [jax.ShapeDtypeStruct((length, lanes), F32)] * 2 + ([jax.ShapeDtypeStruct((1, lanes), F32)] * 2 if with_da else [])
    return pl.pallas_call(
        body, name=name, grid=(lanes // lw,), in_specs=in_specs, out_specs=out_specs, out_shape=out_shape,
        compiler_params=_params(("parallel",)),
    )(*ins)


def _seg_interleave(v):
    length = v.shape[0]
    return v.reshape(SCAN_SEG, length // SCAN_SEG, -1).transpose(1, 0, 2).reshape(length, -1)


def _seg_deinterleave(v):
    length = v.shape[0]
    return v.reshape(length // SCAN_SEG, SCAN_SEG, -1).transpose(1, 0, 2).reshape(length, -1)


def _block_diag(m):
    eye = jnp.eye(S5_GROUPS, dtype=m.dtype)
    return (m.reshape(S5_GROUPS, S5_GROUP, 1, S5_STATE) * eye[:, None, :, None]).reshape(S5_GROUPS * S5_GROUP, S5_LANES)


def _block_diag_take(full):
    idx = jnp.arange(S5_GROUPS)
    blocks = full.reshape(S5_GROUPS, S5_GROUP, S5_GROUPS, S5_STATE)[idx, :, idx, :]
    return blocks.reshape(S5_GROUPS * S5_GROUP, S5_STATE)


def _s5_prep(a_re, a_im, log_step, b_re, b_im, rep):
    step = jnp.exp(log_step)
    mag = jnp.exp(a_re * step)
    ab_re = mag * jnp.cos(a_im * step)
    ab_im = mag * jnp.sin(a_im * step)
    den = a_re * a_re + a_im * a_im
    f_re = ((ab_re - 1.0) * a_re + ab_im * a_im) / den
    f_im = (ab_im * a_re - (ab_re - 1.0) * a_im) / den
    fr, fi = _dot_hi(rep, f_re), _dot_hi(rep, f_im)
    return ab_re, ab_im, fr * b_re - fi * b_im, fr * b_im + fi * b_re


def _rms(x, g):
    return (x * lax.rsqrt(jnp.mean(x * x, axis=-1, keepdims=True) + EPS) * g,)


def _ffn_act(gate, up):
    return (_silu(gate) * up,)


def _glu(a, g):
    return (a * jax.nn.sigmoid(g),)


def _ln_silu(x, g, b):
    xc = x - jnp.mean(x, axis=-1, keepdims=True)
    var = jnp.mean(xc * xc, axis=-1, keepdims=True)
    return (_silu(xc * lax.rsqrt(var + EPS) * g + b),)


def _s5_post(y, u, d_skip, glu_w):
    s = jax.nn.gelu(y + d_skip * u)
    return (s * jax.nn.sigmoid(_mm(s, glu_w)),)


def loss_head(x, tgt, g, *, tl=512):
    length, d = x.shape
    tl = min(tl, length)

    def body(x_ref, t_ref, g_ref, loss_ref, dx_ref, dg_ref):
        i = pl.program_id(0)
        y, vjp = jax.vjp(lambda x, g: _rms(x, g)[0], x_ref[...], g_ref[...])
        err = y - t_ref[...]
        dx, dg = vjp(err * (1.0 / d))
        dx_ref[...] = dx
        part = jnp.broadcast_to(0.5 * jnp.sum(jnp.mean(err * err, axis=-1, keepdims=True), axis=0, keepdims=True), (1, LANE))

        @pl.when(i == 0)
        def _():
            loss_ref[...] = part
            dg_ref[...] = dg

        @pl.when(i > 0)
        def _():
            loss_ref[...] += part
            dg_ref[...] += dg

    row = pl.BlockSpec((tl, d), lambda i: (i, 0))
    return pl.pallas_call(
        body, name="loss_head", grid=(length // tl,),
        in_specs=[row, row, pl.BlockSpec((1, d), lambda i: (0, 0))],
        out_specs=[pl.BlockSpec((1, LANE), lambda i: (0, 0)), row, pl.BlockSpec((1, d), lambda i: (0, 0))],
        out_shape=[jax.ShapeDtypeStruct((1, LANE), F32), jax.ShapeDtypeStruct((length, d), F32),
                   jax.ShapeDtypeStruct((1, d), F32)],
        compiler_params=_params(("arbitrary",)),
    )(x, tgt, g)


def _pad_heads(v):
    return jnp.pad(v, ((0, 0), (0, LANE - v.shape[1])))


def local_step(x, tgt, w, late_weights=None, early_reduce=None):
    length = x.shape[0]
    cos2, sin2 = _rotary_tables(length)
    grads = {}
    w = dict(w)

    def rms_fwd(xin, g, name):
        return rowwise_fwd(_rms, [xin], [], [g], [], [(D_MODEL, BF16)], name=name, tl=512)[0]

    def rms_bwd(xin, g, dh, dxo, name):
        return rowwise_bwd(_rms, [xin], [], [g], [], [dh], [F32], name=name, tl=512, add=dxo)

    def ffn_fwd(i, xin):
        hf = rms_fwd(xin, w["ffn_norm"][i:i + 1], f"ffn{i}_norm")
        w_up = ffn_interleave(w["ffn_w_up"][i], name=f"ffn{i}_up_pairs")
        a = matmul(hf, w_up, out_dtype=BF16, name=f"ffn{i}_up")
        act = ffn_conv_act(a, ffn_interleave(w["ffn_dw_w"][i]), ffn_interleave(w["ffn_dw_b"][i:i + 1]),
                           name=f"ffn{i}_conv_act")
        return matmul(act, w["ffn_w_down"][i], res=xin, name=f"ffn{i}_down"), (hf, a, act, w_up)

    def ffn_bwd(i, xin, saved, dxo):
        hf, a, act, w_up = saved
        dact = matmul(dxo, w["ffn_w_down"][i], tb=True, name=f"ffn{i}_down_dx")
        dw_down = matmul(act, dxo, ta=True, name=f"ffn{i}_down_dw")
        da, ddw_w, ddw_b = ffn_conv_act_bwd(a, ffn_interleave(w["ffn_dw_w"][i]), ffn_interleave(w["ffn_dw_b"][i:i + 1]),
                                            dact, name=f"ffn{i}_conv_act_bwd")
        dw_up = ffn_deinterleave(matmul(hf, da, ta=True, name=f"ffn{i}_up_dw"), name=f"ffn{i}_up_dw_plain")
        dhf = matmul(da, w_up, tb=True, name=f"ffn{i}_up_dx")
        dxin, dnorm = rms_bwd(xin, w["ffn_norm"][i:i + 1], dhf, dxo, f"ffn{i}_norm_bwd")
        return dxin, dict(ffn_norm=dnorm, ffn_w_up=dw_up, ffn_dw_w=ffn_deinterleave(ddw_w),
                          ffn_dw_b=ffn_deinterleave(ddw_b), ffn_w_down=dw_down)

    w_in_e = jnp.pad(w["e_w_in"][0], ((0, 0), (0, EVEN_IN_PAD - EVEN_IN)))
    conv_w_e, conv_b_e = w["e_conv_w"][0], w["e_conv_b"]
    dt_bias, a_log, d_skip = _pad_heads(w["e_dt_bias"]), _pad_heads(w["e_a_log"]), _pad_heads(w["e_d"])
    xbc_off = 4 * D_MODEL

    hn0 = rms_fwd(x, w["mix_norm"][0:1], "mix0_norm")
    proj0 = matmul(hn0, w_in_e, name="even_in")
    y_ret, ret_states = retention_fwd(proj0, cos2, sin2)
    xbc = conv_fwd(proj0, conv_w_e, conv_b_e, act=True, off=xbc_off, name="ssd_conv")
    y_ssm, ssd_states = ssd_fwd(xbc, proj0, dt_bias, a_log, d_skip, w["e_ssm_norm"])
    mix0 = jnp.concatenate([y_ret, y_ssm], axis=1)
    if late_weights is not None:
        w.update(late_weights(y_ssm))
    w_out_e = w["e_w_out"][0]
    x1 = matmul(mix0, w_out_e, res=x, name="even_out")
    x2, ffn0_saved = ffn_fwd(0, x1)

    w_in_o, w_out_o, glu_w = w["o_w_in"][0], w["o_w_out"][0], w["o_glu_w"][0]
    dw_w_o, dw_b_o, ln_g, ln_b, d_o = w["o_dw_w"][0], w["o_dw_b"], w["o_ln_g"], w["o_ln_b"], w["o_d"]
    rep = jnp.asarray(np.repeat(np.eye(S5_GROUPS, dtype=np.float32), S5_GROUP, axis=0))
    rows_gc = (S5_GROUPS * S5_GROUP, S5_STATE)
    prep_in = [w["o_a_re"][0], w["o_a_im"][0], w["o_log_step"].reshape(S5_GROUPS, 1),
               w["o_b_re"][0].transpose(0, 2, 1).reshape(rows_gc), w["o_b_im"][0].transpose(0, 2, 1).reshape(rows_gc), rep]
    ab_re, ab_im, bb_re, bb_im = whole_fwd(
        _s5_prep, prep_in, [(S5_GROUPS, S5_STATE)] * 2 + [rows_gc] * 2, name="s5_prep")
    a_re_row, a_im_row = ab_re.reshape(1, S5_LANES), ab_im.reshape(1, S5_LANES)
    b_re_bd, b_im_bd = _block_diag(bb_re).astype(BF16), _block_diag(bb_im).astype(BF16)
    c_re_bd = _block_diag(w["o_c_re"][0].reshape(rows_gc)).astype(BF16)
    c_im_neg_bd = _block_diag(-w["o_c_im"][0].reshape(rows_gc)).astype(BF16)

    hn1 = rms_fwd(x2, w["mix_norm"][1:2], "mix1_norm")
    proj1 = matmul(hn1, w_in_o, name="odd_in")
    half = D_MODEL // 2
    c_glu = rowwise_fwd(_glu, [Cols(proj1, half, 0), Cols(proj1, half, 1)], [], [], [], [(half, F32)],
                        name="conf_glu", tl=512)[0]
    c_conv = conv_fwd(c_glu, dw_w_o, dw_b_o, act=False, name="conf_conv")
    c_out = rowwise_fwd(_ln_silu, [c_conv], [], [ln_g, ln_b], [], [(half, BF16)], name="conf_ln", tl=512)[0]
    u_seg = _seg_interleave(proj1[:, 2 * half:])
    bu_re = matmul(u_seg, b_re_bd, name="s5_bu_re")
    bu_im = matmul(u_seg, b_im_bd, name="s5_bu_im")
    xs_re, xs_im = s5_scan(bu_re, bu_im, a_re_row, a_im_row, name="s5_scan")
    y_im = matmul(xs_im, c_im_neg_bd, tb=True, name="s5_y_im")
    y_s5 = _seg_deinterleave(matmul(xs_re, c_re_bd, tb=True, res=y_im, name="s5_y_re"))
    s_out = rowwise_fwd(_s5_post, [y_s5, Cols(proj1, half, 2)], [], [d_o, glu_w], [], [(half, BF16)],
                        name="s5_post", tl=512)[0]
    mix1 = jnp.concatenate([c_out, s_out], axis=1)
    x3 = matmul(mix1, w_out_o, res=x2, name="odd_out")
    x4, ffn1_saved = ffn_fwd(1, x3)

    loss, dx4, dfinal = loss_head(x4, tgt, w["final_norm"].reshape(1, D_MODEL))
    grads["final_norm"] = dfinal.reshape(D_MODEL)

    dx3, g_ffn1 = ffn_bwd(1, x3, ffn1_saved, dx4)
    dmix1 = matmul(dx3, w_out_o, tb=True, name="odd_out_dx")
    grads["o_w_out"] = [matmul(mix1, dx3, ta=True, name="odd_out_dw")]
    dc_conv, dln_g, dln_b = rowwise_bwd(_ln_silu, [c_conv], [], [ln_g, ln_b], [], [Cols(dmix1, half, 0)], [F32],
                                        name="conf_ln_bwd", tl=512)
    dc_glu, ddw_w_o, ddw_b_o = conv_bwd(c_glu, dw_w_o, dw_b_o, dc_conv, act=False, name="conf_conv_bwd")
    d_cacg = rowwise_bwd(_glu, [Cols(proj1, half, 0), Cols(proj1, half, 1)], [], [], [], [dc_glu], [BF16],
                         name="conf_glu_bwd", tl=512, merge=True)[0]
    dy_s5, du_post, dd_o, dglu_w = rowwise_bwd(
        _s5_post, [y_s5, Cols(proj1, half, 2)], [], [d_o, glu_w], [], [Cols(dmix1, half, 1)], [F32, F32],
        name="s5_post_bwd", tl=512)
    dy_seg = _seg_interleave(dy_s5)
    dxs_re = matmul(dy_seg, c_re_bd, name="s5_dx_re")
    dxs_im = matmul(dy_seg, c_im_neg_bd, name="s5_dx_im")
    dc_re_bd = matmul(dy_seg, xs_re, ta=True, name="s5_dc_re")
    dc_im_neg_bd = matmul(dy_seg, xs_im, ta=True, name="s5_dc_im")
    g_re, g_im, dab_re, dab_im = s5_scan(dxs_re, dxs_im, a_re_row, -a_im_row, reverse=True, states=(xs_re, xs_im),
                                         name="s5_scan_bwd", lw=LANE)
    dbb_re = _block_diag_take(matmul(u_seg, g_re, ta=True, name="s5_db_re"))
    dbb_im = _block_diag_take(matmul(u_seg, g_im, ta=True, name="s5_db_im"))
    du_im = matmul(g_im, b_im_bd, tb=True, name="s5_du_im")
    du = _seg_deinterleave(matmul(g_re, b_re_bd, tb=True, res=du_im, name="s5_du_re")) + du_post
    da_re, da_im, dlog_step, db_re, db_im = whole_bwd(
        _s5_prep, prep_in, 5,
        [dab_re.reshape(S5_GROUPS, S5_STATE), dab_im.reshape(S5_GROUPS, S5_STATE), dbb_re, dbb_im], name="s5_prep_bwd")
    gcn = (S5_GROUPS, S5_GROUP, S5_STATE)
    grads.update(
        o_a_re=da_re[None], o_a_im=da_im[None], o_log_step=dlog_step.reshape(1, S5_GROUPS),
        o_b_re=db_re.reshape(gcn).transpose(0, 2, 1)[None], o_b_im=db_im.reshape(gcn).transpose(0, 2, 1)[None],
        o_c_re=_block_diag_take(dc_re_bd).reshape(gcn)[None], o_c_im=-_block_diag_take(dc_im_neg_bd).reshape(gcn)[None],
        o_d=dd_o, o_glu_w=[dglu_w], o_dw_w=ddw_w_o[None], o_dw_b=ddw_b_o, o_ln_g=dln_g, o_ln_b=dln_b)
    dproj1 = jnp.concatenate([d_cacg, du.astype(BF16)], axis=1)
    grads["o_w_in"] = [matmul(hn1, dproj1, ta=True, name="odd_in_dw")]
    dhn1 = matmul(dproj1, w_in_o, tb=True, name="odd_in_dx")
    dx2, dmix_norm1 = rms_bwd(x2, w["mix_norm"][1:2], dhn1, dx3, "mix1_norm_bwd")

    if early_reduce is not None:
        zero = early_reduce({("o_w_in", 0): grads["o_w_in"][0], ("o_glu_w", 0): grads["o_glu_w"][0],
                             ("o_w_out", 0): grads["o_w_out"][0], ("ffn_w_up", 1): g_ffn1["ffn_w_up"],
                             ("ffn_w_down", 1): g_ffn1["ffn_w_down"]})
        w["ffn_dw_b"] = w["ffn_dw_b"] + zero
    dx1, g_ffn0 = ffn_bwd(0, x1, ffn0_saved, dx2)
    if early_reduce is not None:
        dt_bias = dt_bias + early_reduce({("ffn_w_up", 0): g_ffn0["ffn_w_up"], ("ffn_w_down", 0): g_ffn0["ffn_w_down"]})
    for k in g_ffn0:
        per_layer = [g_ffn0[k], g_ffn1[k]]
        grads[k] = per_layer if k in ("ffn_w_up", "ffn_w_down") else jnp.stack(per_layer).reshape(w[k].shape)
    dmix0 = matmul(dx1, w_out_e, tb=True, name="even_out_dx")
    grads["e_w_out"] = [matmul(mix0, dx1, ta=True, name="even_out_dw")]
    dq, dk, dv, dg = retention_bwd(proj0, cos2, sin2, ret_states, dmix0)
    dxbc_c, ddt, dz, ddt_bias, da_log, dd_skip, dssm_norm = ssd_bwd(
        xbc, proj0, dt_bias, a_log, d_skip, w["e_ssm_norm"], ssd_states, dmix0)
    dxbc, dconv_w, dconv_b = conv_bwd(proj0, conv_w_e, conv_b_e, dxbc_c, act=True, off=xbc_off,
                                      name="ssd_conv_bwd", dx_dtype=BF16)
    dproj0 = jnp.concatenate([dq, dk, dv, dg, dz, dxbc, ddt], axis=1)
    grads["e_w_in"] = [matmul(hn0, dproj0, ta=True, name="even_in_dw")[:, :EVEN_IN]]
    dhn0 = matmul(dproj0, w_in_e, tb=True, name="even_in_dx")
    dx, dmix_norm0 = rms_bwd(x, w["mix_norm"][0:1], dhn0, dx1, "mix0_norm_bwd")
    grads.update(
        mix_norm=jnp.concatenate([dmix_norm0, dmix_norm1], axis=0), e_conv_w=dconv_w[None], e_conv_b=dconv_b,
        e_dt_bias=ddt_bias[:, :SSM_HEADS], e_a_log=da_log[:, :SSM_HEADS], e_d=dd_skip[:, :SSM_HEADS],
        e_ssm_norm=dssm_norm)
    return loss, dx, grads


def adamw(w, g, m, v, *, name):
    shape = w.shape
    cols = shape[-1]
    rows = w.size // cols
    tr = _tile(rows, max(8, (512 * 1024 // cols) // 8 * 8), unit=8)

    def body(w_ref, g_ref, m_ref, v_ref, d_ref, nm_ref, nv_ref):
        gv = g_ref[...]
        nm = ADAM_B1 * m_ref[...] + (1.0 - ADAM_B1) * gv
        nv = ADAM_B2 * v_ref[...] + (1.0 - ADAM_B2) * jnp.square(gv)
        m_hat = nm / (1.0 - ADAM_B1 ** ADAM_STEP)
        v_hat = nv / (1.0 - ADAM_B2 ** ADAM_STEP)
        d_ref[...] = -ADAM_LR * (m_hat / (jnp.sqrt(v_hat) + ADAM_EPS) + ADAM_WD * w_ref[...])
        nm_ref[...] = nm
        nv_ref[...] = nv

    spec = pl.BlockSpec((tr, cols), lambda i: (i, 0))
    outs = pl.pallas_call(
        body, name=name, grid=(rows // tr,), in_specs=[spec] * 4, out_specs=[spec] * 3,
        out_shape=[jax.ShapeDtypeStruct((rows, cols), F32)] * 3, compiler_params=_params(("parallel",)),
    )(*[t.reshape(rows, cols) for t in (w, g, m, v)])
    return [o.reshape(shape) for o in outs]


OTHER_CHIPS = ((1, 0), (0, 1), (1, 1))
ANY = pl.BlockSpec(memory_space=pl.ANY)


def _position():
    return lax.axis_index("x"), lax.axis_index("y"), lax.axis_index("c")


def _flip(v, f):
    return 1 - v if f else v


def _remote(src, dst, send_sem, recv_sem, device):
    return pltpu.make_async_remote_copy(src_ref=src, dst_ref=dst, send_sem=send_sem, recv_sem=recv_sem,
                                        device_id=device, device_id_type=MESH)


def gather_shards(big, small):
    n_big, n_small = len(big), len(small)
    halves = [a.shape[0] // 2 for a in big]

    def body(*refs):
        big_refs, small_refs = refs[:n_big], refs[n_big:n_big + n_small]
        obig_refs = refs[n_big + n_small:2 * n_big + n_small]
        osmall_refs = refs[2 * n_big + n_small:2 * (n_big + n_small)]
        ici_send, ici_recv, d2d_send, d2d_recv, small_send, small_recv = refs[2 * (n_big + n_small):]
        x, y, c = _position()
        mine = 2 * x + y

        def half(k, core):
            return pl.ds(pl.multiple_of(core * halves[k], 16), halves[k])

        sends = []
        for j, (fx, fy) in enumerate(OTHER_CHIPS):
            peer = (_flip(x, fx), _flip(y, fy), c)
            for k in range(n_big):
                sends.append(_remote(big_refs[k].at[half(k, c)], obig_refs[k].at[mine, half(k, c)],
                                     ici_send.at[j, k], ici_recv.at[j, k], peer))
            for k in range(n_small):
                sends.append(_remote(small_refs[k], osmall_refs[k].at[mine], small_send.at[j, k], small_recv.at[j, k], peer))
        for cp in sends:
            cp.start()
        for j, (fx, fy) in enumerate(OTHER_CHIPS):
            px, py = _flip(x, fx), _flip(y, fy)
            src_chip = 2 * px + py
            for k in range(n_big):
                landed = obig_refs[k].at[src_chip, half(k, c)]
                _remote(landed, landed, ici_send.at[j, k], ici_recv.at[j, k], (px, py, c)).wait_recv()
                fwd = _remote(landed, landed, d2d_send.at[j, k], d2d_recv.at[j, k], (x, y, 1 - c))
                fwd.start()
                sends.append(fwd)
        for j, (fx, fy) in enumerate(OTHER_CHIPS):
            px, py = _flip(x, fx), _flip(y, fy)
            src_chip = 2 * px + py
            for k in range(n_big):
                other = obig_refs[k].at[src_chip, half(k, 1 - c)]
                _remote(other, other, d2d_send.at[j, k], d2d_recv.at[j, k], (x, y, 1 - c)).wait_recv()
            for k in range(n_small):
                dst = osmall_refs[k].at[src_chip]
                _remote(small_refs[k], dst, small_send.at[j, k], small_recv.at[j, k], (px, py, c)).wait_recv()
        for cp in sends:
            cp.wait_send()

    arrays = list(big) + list(small)
    dma = pltpu.SemaphoreType.DMA
    return pl.pallas_call(
        body, name="gather_shards", in_specs=[ANY] * len(arrays), out_specs=[ANY] * len(arrays),
        out_shape=[jax.ShapeDtypeStruct((4,) + a.shape, a.dtype) for a in arrays],
        scratch_shapes=[dma((3, n_big)), dma((3, n_big)), dma((3, n_big)), dma((3, n_big)),
                        dma((3, n_small)), dma((3, n_small))],
        compiler_params=_params(),
    )(*arrays)


def allreduce_small(pack):
    rows = pack.shape[0]

    def body(p_ref, o_ref, slots, send_sems, recv_sems):
        x, y, c = _position()
        me = 4 * x + 2 * y + c
        slots[me] = p_ref[...]
        flips = [((k >> 2) & 1, (k >> 1) & 1, k & 1) for k in range(1, 8)]
        sends = []
        for k, (fx, fy, fc) in enumerate(flips):
            peer = (_flip(x, fx), _flip(y, fy), _flip(c, fc))
            sends.append(_remote(p_ref, slots.at[me], send_sems.at[k], recv_sems.at[k], peer))
        for cp in sends:
            cp.start()
        for k, (fx, fy, fc) in enumerate(flips):
            px, py, pc = _flip(x, fx), _flip(y, fy), _flip(c, fc)
            _remote(p_ref, slots.at[4 * px + 2 * py + pc], send_sems.at[k], recv_sems.at[k], (px, py, pc)).wait_recv()
        for cp in sends:
            cp.wait_send()
        acc = slots[0]
        for d in range(1, 8):
            acc = acc + slots[d]
        o_ref[...] = acc

    vmem = pl.BlockSpec(memory_space=pltpu.VMEM)
    return pl.pallas_call(
        body, name="allreduce_small", in_specs=[vmem], out_specs=vmem,
        out_shape=jax.ShapeDtypeStruct(pack.shape, F32),
        scratch_shapes=[pltpu.VMEM((8, rows, LANE), F32), pltpu.SemaphoreType.DMA((7,)), pltpu.SemaphoreType.DMA((7,))],
        compiler_params=_params(),
    )(pack)


def exchange_halves(gs, *, name):
    n = len(gs)

    def body(*refs):
        g_refs, o_refs, (send_sems, recv_sems) = refs[:n], refs[n:2 * n], refs[2 * n:]
        x, y, c = _position()
        copies = [_remote(g_refs[k].at[:, 1 - c], o_refs[k], send_sems.at[k], recv_sems.at[k], (x, y, 1 - c)) for k in range(n)]
        for cp in copies:
            cp.start()
        for cp in copies:
            cp.wait()

    return pl.pallas_call(
        body, name=name, in_specs=[ANY] * n, out_specs=[ANY] * n,
        out_shape=[jax.ShapeDtypeStruct((4,) + g.shape[2:], g.dtype) for g in gs],
        scratch_shapes=[pltpu.SemaphoreType.DMA((n,)), pltpu.SemaphoreType.DMA((n,))],
        compiler_params=_params(),
    )(*gs)


def scatter_to_chips(parts):
    n = len(parts)

    def body(*refs):
        a_refs, o_refs, (send_sems, recv_sems) = refs[:n], refs[n:2 * n], refs[2 * n:]
        x, y, c = _position()
        copies = []
        for j, (fx, fy) in enumerate(OTHER_CHIPS):
            px, py = _flip(x, fx), _flip(y, fy)
            for k in range(n):
                copies.append(_remote(a_refs[k].at[2 * px + py], o_refs[k].at[j], send_sems.at[j, k], recv_sems.at[j, k], (px, py, c)))
        for cp in copies:
            cp.start()
        for cp in copies:
            cp.wait()

    return pl.pallas_call(
        body, name="scatter_to_chips", in_specs=[ANY] * n, out_specs=[ANY] * n,
        out_shape=[jax.ShapeDtypeStruct((3,) + a.shape[1:], a.dtype) for a in parts],
        scratch_shapes=[pltpu.SemaphoreType.DMA((3, n)), pltpu.SemaphoreType.DMA((3, n))],
        compiler_params=_params(),
    )(*parts)


def swap_halves(rs):
    n = len(rs)

    def body(*refs):
        r_refs, o_refs, (send_sems, recv_sems) = refs[:n], refs[n:2 * n], refs[2 * n:]
        x, y, c = _position()
        copies = [_remote(r_refs[k], o_refs[k], send_sems.at[k], recv_sems.at[k], (x, y, 1 - c)) for k in range(n)]
        for cp in copies:
            cp.start()
        for cp in copies:
            cp.wait()

    dma = pltpu.SemaphoreType.DMA
    return pl.pallas_call(
        body, name="swap_halves", in_specs=[ANY] * n, out_specs=[ANY] * n,
        out_shape=[jax.ShapeDtypeStruct(r.shape, r.dtype) for r in rs],
        scratch_shapes=[dma((n,)), dma((n,))],
        compiler_params=_params(),
    )(*rs)


HBM = pl.BlockSpec(memory_space=pltpu.HBM)
SEM = pl.BlockSpec(memory_space=pltpu.SEMAPHORE)
SIDE_EFFECT = pltpu.SideEffectType.DATAFLOW_SIDE_EFFECTING


def _gather_plan(halves):
    def plan(v_refs, land_refs, x, y, c):
        copies = []
        for fx, fy in OTHER_CHIPS:
            for k in range(len(v_refs)):
                rows = pl.ds(pl.multiple_of(c * halves[k], 16), halves[k])
                copies.append((v_refs[k].at[rows], land_refs[k].at[2 * x + y, rows], (_flip(x, fx), _flip(y, fy), c)))
        return copies
    return plan


def _scatter_plan(v_refs, land_refs, x, y, c):
    copies = []
    for j, (fx, fy) in enumerate(OTHER_CHIPS):
        px, py = _flip(x, fx), _flip(y, fy)
        for k in range(len(v_refs)):
            copies.append((v_refs[k].at[2 * px + py], land_refs[k].at[j], (px, py, c)))
    return copies


def chip_exchange_start(srcs, land_shapes, plan, after, *, name):
    n = len(srcs)
    n_cp = 3 * n

    def body(*refs):
        v_refs, land_refs = refs[:n], refs[n:2 * n]
        outs = refs[2 * n + 1:]
        sends, recvs, token = outs[:n_cp], outs[n_cp:2 * n_cp], outs[-1]
        x, y, c = _position()
        for (src, dst, device), send, recv in zip(plan(v_refs, land_refs, x, y, c), sends, recvs, strict=True):
            _remote(src, dst, send, recv, device).start()
        token[...] = jnp.zeros_like(token)

    lands = [lax.empty(shape, v.dtype) for shape, v in zip(land_shapes, srcs)]
    arrays = [pltpu.with_memory_space_constraint(a, pltpu.HBM) for a in list(srcs) + lands]
    outs = pl.pallas_call(
        body, name=name,
        out_shape=tuple(pltpu.SemaphoreType.DMA(()) for _ in range(2 * n_cp))
        + tuple(pltpu.HBM(a.shape, a.dtype) for a in arrays) + (jax.ShapeDtypeStruct((8, LANE), F32),),
        in_specs=[HBM] * (2 * n) + [ANY],
        out_specs=(SEM,) * (2 * n_cp) + (HBM,) * (2 * n) + (pl.BlockSpec(memory_space=pltpu.VMEM),),
        input_output_aliases={i: 2 * n_cp + i for i in range(2 * n)},
        compiler_params=pltpu.CompilerParams(has_side_effects=SIDE_EFFECT),
    )(*arrays, after)
    handle = (outs[:n_cp], outs[n_cp:2 * n_cp], outs[2 * n_cp:2 * n_cp + n], outs[2 * n_cp + n:2 * n_cp + 2 * n])
    return handle, outs[-1]


def chip_exchange_wait(handle, plan, after, *, name):
    sends, recvs, v_thru, land_thru = handle
    n = len(v_thru)
    n_cp = 3 * n

    def body(*refs):
        v_refs, land_refs = refs[:n], refs[n:2 * n]
        sends, recvs = refs[2 * n:2 * n + n_cp], refs[2 * n + n_cp:2 * n + 2 * n_cp]
        x, y, c = _position()
        for (src, dst, device), send, recv in zip(plan(v_refs, land_refs, x, y, c), sends, recvs, strict=True):
            copy = _remote(src, dst, send, recv, device)
            copy.wait_send()
            copy.wait_recv()

    outs = pl.pallas_call(
        body, name=name,
        out_shape=tuple(pltpu.HBM(a.shape, a.dtype) for a in list(v_thru) + list(land_thru)),
        in_specs=[HBM] * (2 * n) + [SEM] * (2 * n_cp) + [ANY], out_specs=(HBM,) * (2 * n),
        input_output_aliases={i: i for i in range(2 * n)},
        compiler_params=pltpu.CompilerParams(has_side_effects=SIDE_EFFECT),
    )(*v_thru, *land_thru, *sends, *recvs, after)
    return outs[:n], outs[n:]


def finish_gather(lands):
    n = len(lands)
    halves = [a.shape[1] // 2 for a in lands]

    def body(*refs):
        o_refs, (send_sems, recv_sems) = refs[n:2 * n], refs[2 * n:]
        x, y, c = _position()

        def half(k, core):
            return pl.ds(pl.multiple_of(core * halves[k], 16), halves[k])

        sends = []
        for j, (fx, fy) in enumerate(OTHER_CHIPS):
            src_chip = 2 * _flip(x, fx) + _flip(y, fy)
            for k in range(n):
                held = o_refs[k].at[src_chip, half(k, c)]
                sends.append(_remote(held, held, send_sems.at[j, k], recv_sems.at[j, k], (x, y, 1 - c)))
        for cp in sends:
            cp.start()
        for j, (fx, fy) in enumerate(OTHER_CHIPS):
            src_chip = 2 * _flip(x, fx) + _flip(y, fy)
            for k in range(n):
                other = o_refs[k].at[src_chip, half(k, 1 - c)]
                _remote(other, other, send_sems.at[j, k], recv_sems.at[j, k], (x, y, 1 - c)).wait_recv()
        for cp in sends:
            cp.wait_send()

    dma = pltpu.SemaphoreType.DMA
    return pl.pallas_call(
        body, name="finish_gather", in_specs=[ANY] * n, out_specs=[ANY] * n,
        out_shape=[jax.ShapeDtypeStruct(a.shape, a.dtype) for a in lands],
        input_output_aliases={k: k for k in range(n)},
        scratch_shapes=[dma((3, n)), dma((3, n))],
        compiler_params=_params(),
    )(*lands)


def add_own_half(g, r, c_idx, *, name):
    _, _, h, cols = g.shape

    def body(c_ref, g_ref, r_ref, o_ref):
        o_ref[...] = (g_ref[0] + r_ref[...]).astype(o_ref.dtype)

    return pl.pallas_call(
        body, name=name,
        grid_spec=pltpu.PrefetchScalarGridSpec(
            num_scalar_prefetch=1, grid=(4,),
            in_specs=[pl.BlockSpec((1, 1, h, cols), lambda s, c: (s, c[0], 0, 0)),
                      pl.BlockSpec((1, h, cols), lambda s, c: (s, 0, 0))],
            out_specs=pl.BlockSpec((1, h, cols), lambda s, c: (s, 0, 0))),
        out_shape=jax.ShapeDtypeStruct(r.shape, BF16), compiler_params=_params(("parallel",)),
    )(c_idx, g, r)


def add_chip_parts(a, parts, chip_idx, *, name):
    _, h, cols = a.shape
    th = h // 2

    def body(s_ref, a_ref, p0_ref, p1_ref, p2_ref, o_ref):
        f = lambda r: r[0].astype(F32)
        o_ref[...] = ((f(a_ref) + f(p0_ref)) + f(p1_ref)) + f(p2_ref)

    part = lambda j: pl.BlockSpec((1, th, cols), lambda i, s, j=j: (j, i, 0))
    return pl.pallas_call(
        body, name=name,
        grid_spec=pltpu.PrefetchScalarGridSpec(
            num_scalar_prefetch=1, grid=(2,),
            in_specs=[pl.BlockSpec((1, th, cols), lambda i, s: (s[0], i, 0)), part(0), part(1), part(2)],
            out_specs=pl.BlockSpec((th, cols), lambda i, s: (i, 0))),
        out_shape=jax.ShapeDtypeStruct((h, cols), F32), compiler_params=_params(("parallel",)),
    )(chip_idx, a, parts, parts, parts)


WEIGHTS = ("mix_norm", "e_w_in", "e_conv_w", "e_conv_b", "e_dt_bias", "e_a_log", "e_d", "e_ssm_norm", "e_w_out",
           "o_w_in", "o_dw_w", "o_dw_b", "o_ln_g", "o_ln_b", "o_a_re", "o_a_im", "o_b_re", "o_b_im", "o_c_re",
           "o_c_im", "o_d", "o_log_step", "o_glu_w", "o_w_out", "ffn_norm", "ffn_w_up", "ffn_dw_w", "ffn_dw_b",
           "ffn_w_down", "final_norm")
BIG = (("e_w_in", 2), ("e_w_out", 1), ("o_w_in", 2), ("o_glu_w", 1), ("o_w_out", 1), ("ffn_w_up", 2), ("ffn_w_down", 1))
SMALL_SHARDED = (("e_conv_w", 2), ("o_dw_w", 2), ("o_dw_b", 1), ("o_ln_g", 1), ("o_ln_b", 1), ("o_d", 1), ("ffn_dw_w", 2))
REPLICATED = tuple(n for n in WEIGHTS if n not in dict(BIG + SMALL_SHARDED))
PACK_ROWS = 8


def _pack(arrays, dtype, row_unit=PACK_ROWS):
    flat = jnp.concatenate([a.astype(dtype).reshape(-1) for a in arrays])
    rows = -(-flat.size // (LANE * row_unit)) * row_unit
    return jnp.pad(flat, (0, rows * LANE - flat.size)).reshape(rows, LANE)


def _unpack(flat, shapes, lead=()):
    out, off = [], 0
    for shape in shapes:
        size = int(np.prod(shape))
        out.append(flat[..., off:off + size].reshape(lead + tuple(shape)))
        off += size
    return out


def _join_shards(parts, axis):
    return jnp.concatenate([parts[s] for s in range(4)], axis=axis)


def _split_shards(full, axis):
    return jnp.stack(jnp.split(full, 4, axis=axis))


def _rows2d(a):
    return a.reshape(-1, a.shape[-1])


def _layer_shards(g, axis):
    rows, cols = g.shape
    if axis == 0:
        return g.reshape(4, 2, rows // 8, cols)
    return g.reshape(rows, 4, cols // 4).transpose(1, 0, 2).reshape(4, 2, rows // 2, cols // 4)


def kernel(x, mix_norm, e_w_in, e_conv_w, e_conv_b, e_dt_bias, e_a_log, e_d, e_ssm_norm, e_w_out, o_w_in, o_dw_w, o_dw_b, o_ln_g, o_ln_b, o_a_re, o_a_im, o_b_re, o_b_im, o_c_re, o_c_im, o_d, o_log_step, o_glu_w, o_w_out, ffn_norm, ffn_w_up, ffn_dw_w, ffn_dw_b, ffn_w_down, final_norm, loss_target, m_mix_norm, m_e_w_in, m_e_conv_w, m_e_conv_b, m_e_dt_bias, m_e_a_log, m_e_d, m_e_ssm_norm, m_e_w_out, m_o_w_in, m_o_dw_w, m_o_dw_b, m_o_ln_g, m_o_ln_b, m_o_a_re, m_o_a_im, m_o_b_re, m_o_b_im, m_o_c_re, m_o_c_im, m_o_d, m_o_log_step, m_o_glu_w, m_o_w_out, m_ffn_norm, m_ffn_w_up, m_ffn_dw_w, m_ffn_dw_b, m_ffn_w_down, m_final_norm, v_mix_norm, v_e_w_in, v_e_conv_w, v_e_conv_b, v_e_dt_bias, v_e_a_log, v_e_d, v_e_ssm_norm, v_e_w_out, v_o_w_in, v_o_dw_w, v_o_dw_b, v_o_ln_g, v_o_ln_b, v_o_a_re, v_o_a_im, v_o_b_re, v_o_b_im, v_o_c_re, v_o_c_im, v_o_d, v_o_log_step, v_o_glu_w, v_o_w_out, v_ffn_norm, v_ffn_w_up, v_ffn_dw_w, v_ffn_dw_b, v_ffn_w_down, v_final_norm):
    given = dict(locals())
    chip = 2 * lax.axis_index("x") + lax.axis_index("y")
    core = lax.axis_index("c")

    core_idx, chip_idx = core.reshape(1).astype(jnp.int32), chip.reshape(1).astype(jnp.int32)

    def whole(n, axis, parts):
        shape = given[n].shape
        own = given[n].astype(parts.dtype)
        return _join_shards(lax.dynamic_update_index_in_dim(parts.reshape((4,) + shape), own, chip, 0), axis)

    first, later = BIG[:1], BIG[1:]
    shards = {n: _rows2d(given[n]).astype(BF16) for n, _ in BIG}
    gathered = gather_shards([shards[n] for n, _ in first], [_rows2d(given[n]) for n, _ in SMALL_SHARDED])
    w = {n: given[n] for n in REPLICATED}
    for (n, axis), parts in zip(first + SMALL_SHARDED, gathered):
        w[n] = whole(n, axis, parts)
    later_shards = [shards[n] for n, _ in later]
    gather_plan = _gather_plan([a.shape[0] // 2 for a in later_shards])
    gather_handle, token = chip_exchange_start(later_shards, [(4,) + a.shape for a in later_shards], gather_plan,
                                               gathered[0], name="gather_start")
    w["mix_norm"] = w["mix_norm"] + token[0, 0]

    def late_weights(after):
        _, lands = chip_exchange_wait(gather_handle, gather_plan, after, name="gather_wait")
        return {n: whole(n, axis, parts) for (n, axis), parts in zip(later, finish_gather(lands))}

    groups = []

    def finish_group(after):
        group = groups[-1]
        group["sums"], group["parts"] = chip_exchange_wait(group.pop("handle"), _scatter_plan, after,
                                                           name=f"scatter_wait_{len(groups) - 1}")

    def early_reduce(layer_grads):
        keys = list(layer_grads)
        if groups:
            finish_group(layer_grads[keys[0]])
        tag = len(groups)
        parts = [_layer_shards(layer_grads[k], dict(BIG)[k[0]] - 1) for k in keys]
        sums = [add_own_half(g, r, core_idx, name=f"add_own_half_{n}{layer}")
                for g, r, (n, layer) in zip(parts, exchange_halves(parts, name=f"exchange_halves_{tag}"), keys)]
        handle, zeros = chip_exchange_start(sums, [(3,) + a.shape[1:] for a in sums], _scatter_plan, sums[0],
                                            name=f"scatter_start_{tag}")
        groups.append(dict(keys=keys, handle=handle))
        return zeros[0, 0]

    loss, dx, grads = local_step(x[0], loss_target[0], w, late_weights, early_reduce)
    finish_group(dx)
    early_keys = [k for group in groups for k in group["keys"]]
    early_sums = [a for group in groups for a in group["sums"]]
    early_parts = [a for group in groups for a in group["parts"]]

    small_names = REPLICATED + tuple(n for n, _ in SMALL_SHARDED)
    small_sum = allreduce_small(_pack([grads[n] for n in small_names], F32))
    reduced = dict(zip(small_names, _unpack(small_sum.reshape(-1), [grads[n].shape for n in small_names])))
    for n, axis in SMALL_SHARDED:
        width = given[n].shape[axis]
        reduced[n] = lax.dynamic_slice_in_dim(reduced[n], chip * width, width, axis=axis)

    keys, parts = [], []
    for n, axis in BIG:
        for layer, g in enumerate(grads[n]):
            if (n, layer) not in early_keys:
                keys.append((n, layer))
                parts.append(_layer_shards(g, axis - 1))
    core_sums = [add_own_half(g, r, core_idx, name=f"add_own_half_{n}{layer}")
                 for g, r, (n, layer) in zip(parts, exchange_halves(parts, name="exchange_halves_last"), keys)]
    chip_parts = scatter_to_chips(core_sums)
    keys, core_sums, chip_parts = early_keys + keys, early_sums + core_sums, early_parts + list(chip_parts)
    mine = [add_chip_parts(a, p, chip_idx, name=f"add_chip_parts_{n}{layer}")
            for a, p, (n, layer) in zip(core_sums, chip_parts, keys)]
    layers = {}
    for (n, layer), own, other in zip(keys, mine, swap_halves(mine)):
        both = jnp.where(core == 0, jnp.stack([own, other]), jnp.stack([other, own]))
        layers.setdefault(n, {})[layer] = both.reshape(given[n].shape[1:])
    for n, _ in BIG:
        reduced[n] = jnp.stack([layers[n][layer] for layer in sorted(layers[n])])

    delta, new_m, new_v = {}, {}, {}
    for n, _ in BIG:
        delta[n], new_m[n], new_v[n] = adamw(given[n], reduced[n], given["m_" + n], given["v_" + n], name="adamw_" + n)
    shapes = [given[n].shape for n in small_names]
    packed = [_pack([src[n] for n in small_names], F32)
              for src in (given, reduced, {n: given["m_" + n] for n in small_names}, {n: given["v_" + n] for n in small_names})]
    for dst, res in zip((delta, new_m, new_v), adamw(*packed, name="adamw_small")):
        dst.update(zip(small_names, _unpack(res.reshape(-1), shapes)))

    total = lax.psum(loss[0, 0], ("x", "y", "c"))
    return (total, dx[None], *[reduced[n] for n in WEIGHTS], *[delta[n] for n in WEIGHTS],
            *[new_m[n] for n in WEIGHTS], *[new_v[n] for n in WEIGHTS])
```

```python
import functools
import math
from typing import NamedTuple

import numpy as np
import jax
import jax.numpy as jnp
from jax import lax
from jax.experimental import pallas as pl
from jax.experimental.pallas import tpu as pltpu

F32 = jnp.float32
BF16 = jnp.bfloat16
HIGHEST = lax.Precision.HIGHEST
MESH = pl.DeviceIdType.MESH

D_MODEL = 1024
EPS = 1e-6
RET_HEADS, RET_DK, RET_DV, CHUNK = 4, 128, 256, 128
ROPE_BASE = 10000.0
SSM_HEADS, SSM_P, SSM_N, SSM_GROUPS = 16, 64, 128, 2
SSM_DINNER = SSM_HEADS * SSM_P
EVEN_IN, EVEN_IN_PAD = 5648, 5760
S5_GROUPS, S5_GROUP, S5_STATE = 32, 16, 64
S5_LANES = S5_GROUPS * S5_STATE
SCAN_SEG = 32
D_FF = 2816
ADAM_LR, ADAM_B1, ADAM_B2, ADAM_EPS, ADAM_WD, ADAM_STEP = 0.001, 0.9, 0.999, 1e-08, 0.01, 10

LANE = 128
VMEM_LIMIT = 56 * 1024 * 1024


def _params(sem=None, **kw):
    return pltpu.CompilerParams(dimension_semantics=sem, vmem_limit_bytes=VMEM_LIMIT, **kw)


def _tile(n, target, unit=LANE):
    if n <= target:
        return n
    t = (target // unit) * unit
    while t >= unit:
        if n % t == 0:
            return t
        t -= unit
    return n


def _silu(x):
    return x * jax.nn.sigmoid(x)


def _mm(a, b):
    return jnp.dot(a.astype(BF16), b.astype(BF16), preferred_element_type=F32)


def _mm_nt(a, b):
    return lax.dot_general(a.astype(BF16), b.astype(BF16), (((1,), (1,)), ((), ())), preferred_element_type=F32)


def _mm_tn(a, b):
    return lax.dot_general(a.astype(BF16), b.astype(BF16), (((0,), (0,)), ((), ())), preferred_element_type=F32)


def _dot_hi(a, b):
    return jnp.dot(a, b, precision=HIGHEST, preferred_element_type=F32)


def _dot_hi_tn(a, b):
    return lax.dot_general(a, b, (((0,), (0,)), ((), ())), precision=HIGHEST, preferred_element_type=F32)


def _bf16_parts(v):
    hi = v.astype(BF16)
    rest = v - hi.astype(F32)
    mid = rest.astype(BF16)
    return hi, mid, (rest - mid.astype(F32)).astype(BF16)


def _dot_parts(v, fixed, dims, v_first):
    fixed = fixed.astype(BF16)
    out = None
    for part in _bf16_parts(v):
        ops = (part, fixed) if v_first else (fixed, part)
        p = lax.dot_general(*ops, (dims, ((), ())), preferred_element_type=F32)
        out = p if out is None else out + p
    return out


@jax.custom_vjp
def _times_01(v, ones):
    return _dot_parts(v, ones, ((1,), (0,)), True)


_times_01.defvjp(lambda v, ones: (_times_01(v, ones), ones),
                 lambda ones, g: (_dot_parts(g, ones, ((1,), (1,)), True), jnp.zeros_like(ones)))


@jax.custom_vjp
def _01_times(ones, v):
    return _dot_parts(v, ones, ((1,), (0,)), False)


_01_times.defvjp(lambda ones, v: (_01_times(ones, v), ones),
                 lambda ones, g: (jnp.zeros_like(ones), _dot_parts(g, ones, ((0,), (0,)), False)))


MATMUL_VMEM = 44 * 1024 * 1024


def matmul(a, b, *, ta=False, tb=False, res=None, out_dtype=F32, name):
    m, k = (a.shape[1], a.shape[0]) if ta else a.shape
    n = b.shape[0] if tb else b.shape[1]
    assert (b.shape[1] if tb else b.shape[0]) == k, (a.shape, b.shape, ta, tb)
    tm = _tile(m, 1536)
    tn = _tile(n, 640)
    if tn < 384:
        tn = _tile(n, 1536)
    res_bytes = 0 if res is None else res.dtype.itemsize

    def vmem(tm, tn):
        return 2 * (tm * k * a.dtype.itemsize + tn * k * b.dtype.itemsize + tm * tn * (jnp.dtype(out_dtype).itemsize + res_bytes))

    while vmem(tm, tn) > MATMUL_VMEM and tm % (2 * LANE) == 0:
        tm //= 2
    assert vmem(tm, tn) <= MATMUL_VMEM, (name, tm, tn, k)
    a_spec = pl.BlockSpec((k, tm), lambda i, j: (0, i)) if ta else pl.BlockSpec((tm, k), lambda i, j: (i, 0))
    b_spec = pl.BlockSpec((tn, k), lambda i, j: (j, 0)) if tb else pl.BlockSpec((k, tn), lambda i, j: (0, j))
    o_spec = pl.BlockSpec((tm, tn), lambda i, j: (i, j))
    dims = (((0 if ta else 1,), (1 if tb else 0,)), ((), ()))
    has_res = res is not None

    def body(a_ref, b_ref, *rest):
        o_ref = rest[-1]
        out = lax.dot_general(a_ref[...].astype(BF16), b_ref[...].astype(BF16), dims, preferred_element_type=F32)
        if has_res:
            out = out + rest[0][...].astype(F32)
        o_ref[...] = out.astype(o_ref.dtype)

    ins = [a, b] + ([res] if has_res else [])
    specs = [a_spec, b_spec] + ([o_spec] if has_res else [])
    return pl.pallas_call(
        body, name=name, grid=(m // tm, n // tn), in_specs=specs, out_specs=o_spec,
        out_shape=jax.ShapeDtypeStruct((m, n), out_dtype), compiler_params=_params(("parallel", "parallel")),
    )(*ins)


class Cols(NamedTuple):
    arr: jax.Array
    w: int
    j: int


def _cols(a):
    return a if isinstance(a, Cols) else Cols(a, a.shape[1], 0)


def _row_spec(c, tl):
    return pl.BlockSpec((tl, c.w), lambda i, j=c.j: (i, j))


def _whole_spec(p):
    return pl.BlockSpec(p.shape, lambda i, nd=p.ndim: (0,) * nd)


def rowwise_fwd(fn, rows, aux, pars, consts, outs, *, name, tl):
    rows = [_cols(r) for r in rows + aux]
    whole = list(pars) + list(consts)
    n_rows = len(rows)
    n_whole = len(whole)
    length = rows[0].arr.shape[0]
    tl = min(tl, length)

    def body(*refs):
        vals = [r[...].astype(F32) for r in refs[:n_rows]] + [r[...] for r in refs[n_rows:n_rows + n_whole]]
        res = fn(*vals)
        for o_ref, v in zip(refs[n_rows + n_whole:], res, strict=True):
            o_ref[...] = v.astype(o_ref.dtype)

    return pl.pallas_call(
        body, name=name, grid=(length // tl,),
        in_specs=[_row_spec(r, tl) for r in rows] + [_whole_spec(p) for p in whole],
        out_specs=[pl.BlockSpec((tl, w), lambda i: (i, 0)) for w, _ in outs],
        out_shape=[jax.ShapeDtypeStruct((length, w), dt) for w, dt in outs],
        compiler_params=_params(("parallel",)),
    )(*[r.arr for r in rows], *whole)


def rowwise_bwd(fn, rows, aux, pars, consts, cots, drow_dtypes, *, name, tl, add=None, merge=False):
    rows = [_cols(r) for r in rows]
    aux = [_cols(r) for r in aux]
    cots = [_cols(r) for r in cots]
    n_r, n_a, n_p, n_c, n_t = len(rows), len(aux), len(pars), len(consts), len(cots)
    length = rows[0].arr.shape[0]
    tl = min(tl, length)
    has_add = add is not None
    widths = [r.w for r in rows]

    def body(*refs):
        pos = 0
        r_vals = [r[...].astype(F32) for r in refs[pos:pos + n_r]]; pos += n_r
        a_vals = [r[...].astype(F32) for r in refs[pos:pos + n_a]]; pos += n_a
        p_vals = [r[...].astype(F32) for r in refs[pos:pos + n_p]]; pos += n_p
        c_vals = [r[...] for r in refs[pos:pos + n_c]]; pos += n_c
        t_vals = [r[...].astype(F32) for r in refs[pos:pos + n_t]]; pos += n_t
        add_val = None
        if has_add:
            add_val = refs[pos][...].astype(F32); pos += 1
        n_dr = 1 if merge else n_r
        dr_refs = refs[pos:pos + n_dr]; pos += n_dr
        dp_refs = refs[pos:pos + n_p]

        def f(*rp):
            return fn(*rp[:n_r], *a_vals, *rp[n_r:], *c_vals)

        _, vjp = jax.vjp(f, *r_vals, *p_vals)
        grads = vjp(tuple(t_vals))
        drows = list(grads[:n_r])
        if has_add:
            drows[0] = drows[0] + add_val
        if merge:
            off = 0
            for w, d in zip(widths, drows):
                dr_refs[0][:, off:off + w] = d.astype(dr_refs[0].dtype)
                off += w
        else:
            for r, d in zip(dr_refs, drows):
                r[...] = d.astype(r.dtype)
        i = pl.program_id(0)
        for r, d in zip(dp_refs, grads[n_r:]):
            @pl.when(i == 0)
            def _(r=r, d=d):
                r[...] = d

            @pl.when(i > 0)
            def _(r=r, d=d):
                r[...] += d

    if merge:
        dr_specs = [pl.BlockSpec((tl, sum(widths)), lambda i: (i, 0))]
        dr_shapes = [jax.ShapeDtypeStruct((length, sum(widths)), drow_dtypes[0])]
    else:
        dr_specs = [pl.BlockSpec((tl, w), lambda i: (i, 0)) for w in widths]
        dr_shapes = [jax.ShapeDtypeStruct((length, w), dt) for w, dt in zip(widths, drow_dtypes)]
    ins = [r.arr for r in rows + aux] + list(pars) + list(consts) + [r.arr for r in cots] + ([add] if has_add else [])
    specs = ([_row_spec(r, tl) for r in rows + aux] + [_whole_spec(p) for p in list(pars) + list(consts)]
             + [_row_spec(r, tl) for r in cots] + ([pl.BlockSpec((tl, add.shape[1]), lambda i: (i, 0))] if has_add else []))
    return pl.pallas_call(
        body, name=name, grid=(length // tl,), in_specs=specs,
        out_specs=dr_specs + [_whole_spec(p) for p in pars],
        out_shape=dr_shapes + [jax.ShapeDtypeStruct(p.shape, F32) for p in pars],
        compiler_params=_params(("arbitrary",)),
    )(*ins)


def whole_fwd(fn, ins, out_shapes, *, name):
    n_in = len(ins)

    def body(*refs):
        res = fn(*[r[...] for r in refs[:n_in]])
        for o_ref, v in zip(refs[n_in:], res, strict=True):
            o_ref[...] = v

    return pl.pallas_call(body, name=name, out_shape=[jax.ShapeDtypeStruct(s, F32) for s in out_shapes],
                          compiler_params=_params())(*ins)


def whole_bwd(fn, ins, n_diff, cots, *, name):
    n_in, n_t = len(ins), len(cots)

    def body(*refs):
        vals = [r[...] for r in refs[:n_in]]
        t_vals = [r[...] for r in refs[n_in:n_in + n_t]]
        _, vjp = jax.vjp(lambda *d: fn(*d, *vals[n_diff:]), *vals[:n_diff])
        for o_ref, g in zip(refs[n_in + n_t:], vjp(tuple(t_vals)), strict=True):
            o_ref[...] = g

    return pl.pallas_call(body, name=name, out_shape=[jax.ShapeDtypeStruct(a.shape, F32) for a in ins[:n_diff]],
                          compiler_params=_params())(*ins, *cots)


CONV_ROWS = 256


def _conv_geometry(x, w, cw, off):
    width = w.shape[1]
    x = Cols(x, width, 0)
    length = x.arr.shape[0]
    taps = w.shape[0]
    pad = -(-(taps - 1) // 8) * 8
    assert off % cw == 0 and width % cw == 0, (off, width, cw)
    return x, length, taps, pad, off // cw


def _conv_taps(xp_ref, w_ref, base, taps, pad, init, lanes=slice(None)):
    acc = init
    for k in range(taps):
        acc = acc + w_ref[k:k + 1, lanes] * xp_ref[pl.ds(base + pad - (taps - 1) + k, init.shape[0]), :]
    return acc


def conv_fwd(x, w, b, *, act, name, off=0, cw=LANE, out_dtype=F32):
    x, length, taps, pad, jb = _conv_geometry(x, w, cw, off)
    rc = min(CONV_ROWS, length)

    def body(x_ref, w_ref, b_ref, o_ref, xp_ref):
        xp_ref[0:pad, :] = jnp.zeros((pad, cw), F32)
        xp_ref[pad:pad + length, :] = x_ref[...].astype(F32)

        def chunk(r, carry):
            base = pl.multiple_of(r * rc, rc)
            acc = _conv_taps(xp_ref, w_ref, base, taps, pad, jnp.broadcast_to(b_ref[...], (rc, cw)))
            if act:
                acc = _silu(acc)
            o_ref[pl.ds(base, rc), :] = acc.astype(o_ref.dtype)
            return carry

        lax.fori_loop(0, length // rc, chunk, 0)

    return pl.pallas_call(
        body, name=name, grid=(x.w // cw,),
        in_specs=[pl.BlockSpec((length, cw), lambda j: (0, jb + j)), pl.BlockSpec((taps, cw), lambda j: (0, j)),
                  pl.BlockSpec((1, cw), lambda j: (0, j))],
        out_specs=pl.BlockSpec((length, cw), lambda j: (0, j)),
        out_shape=jax.ShapeDtypeStruct((length, x.w), out_dtype),
        scratch_shapes=[pltpu.VMEM((pad + length, cw), F32)],
        compiler_params=_params(("parallel",)),
    )(x.arr, w, b)


def conv_bwd(x, w, b, dy, *, act, name, off=0, cw=LANE, dx_dtype=F32):
    x, length, taps, pad, jb = _conv_geometry(x, w, cw, off)
    rc = min(CONV_ROWS, length)

    def body(x_ref, w_ref, b_ref, dy_ref, dx_ref, dw_ref, db_ref, xp_ref, gp_ref):
        xp_ref[0:pad, :] = jnp.zeros((pad, cw), F32)
        xp_ref[pad:pad + length, :] = x_ref[...].astype(F32)
        gp_ref[length:length + pad, :] = jnp.zeros((pad, cw), F32)
        if act:
            def pre_chunk(r, carry):
                base = pl.multiple_of(r * rc, rc)
                pre = _conv_taps(xp_ref, w_ref, base, taps, pad, jnp.broadcast_to(b_ref[...], (rc, cw)))
                sig = jax.nn.sigmoid(pre)
                gp_ref[pl.ds(base, rc), :] = dy_ref[pl.ds(base, rc), :].astype(F32) * (sig * (1.0 + pre * (1.0 - sig)))
                return carry

            lax.fori_loop(0, length // rc, pre_chunk, 0)
        else:
            gp_ref[0:length, :] = dy_ref[...].astype(F32)
        dw_ref[...] = jnp.zeros((taps, cw), F32)
        db_ref[...] = jnp.zeros((1, cw), F32)

        def chunk(r, carry):
            base = pl.multiple_of(r * rc, rc)
            acc = jnp.zeros((rc, cw), F32)
            g = gp_ref[pl.ds(base, rc), :]
            for k in range(taps):
                acc = acc + w_ref[k:k + 1, :] * gp_ref[pl.ds(base + (taps - 1) - k, rc), :]
                xs = xp_ref[pl.ds(base + pad - (taps - 1) + k, rc), :]
                dw_ref[k:k + 1, :] += jnp.sum(g * xs, axis=0, keepdims=True)
            db_ref[...] += jnp.sum(g, axis=0, keepdims=True)
            dx_ref[pl.ds(base, rc), :] = acc.astype(dx_ref.dtype)
            return carry

        lax.fori_loop(0, length // rc, chunk, 0)

    dy = _cols(dy)
    assert dy.j == 0 and dy.w == x.w
    return pl.pallas_call(
        body, name=name, grid=(x.w // cw,),
        in_specs=[pl.BlockSpec((length, cw), lambda j: (0, jb + j)), pl.BlockSpec((taps, cw), lambda j: (0, j)),
                  pl.BlockSpec((1, cw), lambda j: (0, j)), pl.BlockSpec((length, cw), lambda j: (0, j))],
        out_specs=[pl.BlockSpec((length, cw), lambda j: (0, j)), pl.BlockSpec((taps, cw), lambda j: (0, j)),
                   pl.BlockSpec((1, cw), lambda j: (0, j))],
        out_shape=[jax.ShapeDtypeStruct((length, x.w), dx_dtype), jax.ShapeDtypeStruct((taps, x.w), F32),
                   jax.ShapeDtypeStruct((1, x.w), F32)],
        scratch_shapes=[pltpu.VMEM((pad + length, cw), F32), pltpu.VMEM((length + pad, cw), F32)],
        compiler_params=_params(("parallel",)),
    )(x.arr, w, b, dy.arr)


def _conv_transpose(xp_ref, gp_ref, w_ref, dx_ref, dw_ref, db_ref, lanes, length, taps, pad, rc):
    dw_ref[:, lanes] = jnp.zeros((taps, LANE), F32)
    db_ref[:, lanes] = jnp.zeros((1, LANE), F32)

    def chunk(r, carry):
        base = pl.multiple_of(r * rc, rc)
        acc = jnp.zeros((rc, LANE), F32)
        g = gp_ref[pl.ds(base, rc), :]
        for k in range(taps):
            acc = acc + w_ref[k:k + 1, lanes] * gp_ref[pl.ds(base + (taps - 1) - k, rc), :]
            xs = xp_ref[pl.ds(base + pad - (taps - 1) + k, rc), :]
            dw_ref[k:k + 1, lanes] += jnp.sum(g * xs, axis=0, keepdims=True)
        db_ref[:, lanes] += jnp.sum(g, axis=0, keepdims=True)
        dx_ref[pl.ds(base, rc), lanes] = acc.astype(dx_ref.dtype)
        return carry

    lax.fori_loop(0, length // rc, chunk, 0)


LANE_PAIR_ROWS = 1024


def ffn_interleave(a, name=None):
    rows, width = a.shape
    nb = width // (2 * LANE)
    if rows < LANE_PAIR_ROWS:
        return a.reshape(rows, 2, nb, LANE).swapaxes(1, 2).reshape(a.shape)

    def body(g_ref, u_ref, o_ref):
        o_ref[:, 0:LANE] = g_ref[...]
        o_ref[:, LANE:2 * LANE] = u_ref[...]

    tr = LANE_PAIR_ROWS
    return pl.pallas_call(
        body, name=name, grid=(rows // tr, nb),
        in_specs=[pl.BlockSpec((tr, LANE), lambda i, j: (i, j)), pl.BlockSpec((tr, LANE), lambda i, j: (i, nb + j))],
        out_specs=pl.BlockSpec((tr, 2 * LANE), lambda i, j: (i, j)),
        out_shape=jax.ShapeDtypeStruct(a.shape, a.dtype), compiler_params=_params(("parallel", "parallel")),
    )(a, a)


def ffn_deinterleave(a):
    rows, width = a.shape
    return a.reshape(rows, width // (2 * LANE), 2, LANE).swapaxes(1, 2).reshape(a.shape)


def ffn_pairs_to_shards(a, *, name):
    rows, width = a.shape
    nb = width // (2 * LANE)
    per = width // 4 // LANE

    def source(s, t):
        block = s * per + t
        return 0, jnp.where(block < nb, 2 * block, 2 * (block - nb) + 1)

    def body(a_ref, o_ref):
        o_ref[0] = a_ref[...]

    return pl.pallas_call(
        body, name=name, grid=(4, per), in_specs=[pl.BlockSpec((rows, LANE), source)],
        out_specs=pl.BlockSpec((1, rows, LANE), lambda s, t: (s, 0, t)),
        out_shape=jax.ShapeDtypeStruct((4, rows, width // 4), a.dtype), compiler_params=_params(("parallel", "parallel")),
    )(a)


GATE, UP = slice(0, LANE), slice(LANE, 2 * LANE)


def _ffn_geometry(a, w):
    length, width = a.shape
    taps = w.shape[0]
    return length, width, width // (2 * LANE), taps, -(-(taps - 1) // 8) * 8, min(CONV_ROWS, length)


def _ffn_pre(xg_ref, xu_ref, w_ref, b_ref, base, taps, pad, rc):
    gate = _conv_taps(xg_ref, w_ref, base, taps, pad, jnp.broadcast_to(b_ref[:, GATE], (rc, LANE)), GATE)
    up = _conv_taps(xu_ref, w_ref, base, taps, pad, jnp.broadcast_to(b_ref[:, UP], (rc, LANE)), UP)
    return gate, up


def ffn_conv_act(a, w, b, *, name):
    length, width, nb, taps, pad, rc = _ffn_geometry(a, w)

    def body(a_ref, w_ref, b_ref, o_ref, xg_ref, xu_ref):
        for xp_ref, lanes in ((xg_ref, GATE), (xu_ref, UP)):
            xp_ref[0:pad, :] = jnp.zeros((pad, LANE), F32)
            xp_ref[pad:pad + length, :] = a_ref[:, lanes].astype(F32)

        def chunk(r, carry):
            base = pl.multiple_of(r * rc, rc)
            gate, up = _ffn_pre(xg_ref, xu_ref, w_ref, b_ref, base, taps, pad, rc)
            o_ref[pl.ds(base, rc), :] = (_silu(gate) * up).astype(o_ref.dtype)
            return carry

        lax.fori_loop(0, length // rc, chunk, 0)

    pair = lambda rows: pl.BlockSpec((rows, 2 * LANE), lambda j: (0, j))
    return pl.pallas_call(
        body, name=name, grid=(nb,), in_specs=[pair(length), pair(taps), pair(1)],
        out_specs=pl.BlockSpec((length, LANE), lambda j: (0, j)),
        out_shape=jax.ShapeDtypeStruct((length, width // 2), BF16),
        scratch_shapes=[pltpu.VMEM((pad + length, LANE), F32), pltpu.VMEM((pad + length, LANE), F32)],
        compiler_params=_params(("parallel",)),
    )(a, w, b)


def ffn_conv_act_bwd(a, w, b, dact, *, name):
    length, width, nb, taps, pad, rc = _ffn_geometry(a, w)

    def body(a_ref, w_ref, b_ref, dy_ref, da_ref, dw_ref, db_ref, xg_ref, xu_ref, gg_ref, gu_ref):
        for xp_ref, lanes in ((xg_ref, GATE), (xu_ref, UP)):
            xp_ref[0:pad, :] = jnp.zeros((pad, LANE), F32)
            xp_ref[pad:pad + length, :] = a_ref[:, lanes].astype(F32)
        for gp_ref in (gg_ref, gu_ref):
            gp_ref[length:length + pad, :] = jnp.zeros((pad, LANE), F32)

        def pre_chunk(r, carry):
            base = pl.multiple_of(r * rc, rc)
            gate, up = _ffn_pre(xg_ref, xu_ref, w_ref, b_ref, base, taps, pad, rc)
            sig = jax.nn.sigmoid(gate)
            dy = dy_ref[pl.ds(base, rc), :]
            gg_ref[pl.ds(base, rc), :] = dy * up * (sig * (1.0 + gate * (1.0 - sig)))
            gu_ref[pl.ds(base, rc), :] = dy * (gate * sig)
            return carry

        lax.fori_loop(0, length // rc, pre_chunk, 0)
        _conv_transpose(xg_ref, gg_ref, w_ref, da_ref, dw_ref, db_ref, GATE, length, taps, pad, rc)
        _conv_transpose(xu_ref, gu_ref, w_ref, da_ref, dw_ref, db_ref, UP, length, taps, pad, rc)

    pair = lambda rows: pl.BlockSpec((rows, 2 * LANE), lambda j: (0, j))
    return pl.pallas_call(
        body, name=name, grid=(nb,),
        in_specs=[pair(length), pair(taps), pair(1), pl.BlockSpec((length, LANE), lambda j: (0, j))],
        out_specs=[pair(length), pair(taps), pair(1)],
        out_shape=[jax.ShapeDtypeStruct((length, width), BF16), jax.ShapeDtypeStruct((taps, width), F32),
                   jax.ShapeDtypeStruct((1, width), F32)],
        scratch_shapes=[pltpu.VMEM((pad + length, LANE), F32), pltpu.VMEM((pad + length, LANE), F32),
                        pltpu.VMEM((length + pad, LANE), F32), pltpu.VMEM((length + pad, LANE), F32)],
        compiler_params=_params(("parallel",)),
    )(a, w, b, dact)


def _retention_consts():
    h = np.arange(RET_HEADS, dtype=np.float32)
    log_g = np.log1p(-(2.0 ** (-5.0 - h))).astype(np.float32)
    idx = np.arange(CHUNK, dtype=np.float32)
    diff = idx[:, None] - idx[None, :]
    intra = np.where(diff[None] >= 0, np.exp(np.maximum(diff, 0.0)[None] * log_g[:, None, None]), 0.0)
    zeta = np.exp((CHUNK - 1 - idx)[None, :] * log_g[:, None])
    xi = np.exp((idx + 1)[None, :] * log_g[:, None])
    decay = np.exp(CHUNK * log_g)
    zeta = np.broadcast_to(zeta[:, :, None], (RET_HEADS, CHUNK, RET_DK))
    xi = np.broadcast_to(xi[:, :, None], (RET_HEADS, CHUNK, RET_DV))
    return (jnp.asarray(intra, F32), jnp.asarray(zeta, F32), jnp.asarray(xi, F32), [float(d) for d in decay])


def _rotary_tables(length):
    inv = ROPE_BASE ** (-jnp.arange(0, RET_DK, 2, dtype=F32) / RET_DK)
    ang = jnp.arange(length).astype(F32)[:, None] * inv[None, :]
    cos, sin = jnp.cos(ang), jnp.sin(ang)
    return jnp.concatenate([cos, cos], axis=1), jnp.concatenate([-sin, sin], axis=1)


def _rot(x, cos2, sin2):
    return x * cos2 + pltpu.roll(x, RET_DK // 2, 1) * sin2


def _rot_t(y, cos2, sin2):
    return y * cos2 + pltpu.roll(y * sin2, RET_DK // 2, 1)


def _ret_chunk(q, k, v, g, state, intra, zeta, xi, decay):
    s = _mm_nt(q, k) * intra
    kv = _mm_tn(k * zeta, v)
    o = _mm(s, v) + _mm(q, state) * xi
    oc = o - jnp.mean(o, axis=-1, keepdims=True)
    r = oc * lax.rsqrt(jnp.mean(oc * oc, axis=-1, keepdims=True) + EPS)
    return _silu(g) * r, state * decay + kv


RET_QK, RET_V = RET_HEADS * RET_DK, RET_HEADS * RET_DV


def _ret_specs(rev, nc):
    def cidx(c):
        return nc - 1 - c if rev else c
    whole = lambda shape: pl.BlockSpec(shape, lambda c: (0,) * len(shape))
    return [
        pl.BlockSpec((CHUNK, RET_QK), lambda c: (cidx(c), 0)),
        pl.BlockSpec((CHUNK, RET_QK), lambda c: (cidx(c), 1)),
        pl.BlockSpec((CHUNK, RET_V), lambda c: (cidx(c), 1)),
        pl.BlockSpec((CHUNK, RET_V), lambda c: (cidx(c), 2)),
        pl.BlockSpec((CHUNK, RET_DK), lambda c: (cidx(c), 0)),
        pl.BlockSpec((CHUNK, RET_DK), lambda c: (cidx(c), 0)),
        whole((RET_HEADS, CHUNK, CHUNK)), whole((RET_HEADS, CHUNK, RET_DK)), whole((RET_HEADS, CHUNK, RET_DV)),
    ], cidx


def _head(ref, h, width):
    return ref[:, h * width:(h + 1) * width].astype(F32)


def retention_fwd(proj, cos2, sin2):
    length = proj.shape[0]
    nc = length // CHUNK
    intra, zeta, xi, decays = _retention_consts()
    specs, _ = _ret_specs(False, nc)
    scale = RET_DK ** -0.5

    def body(q_ref, k_ref, v_ref, g_ref, cos_ref, sin_ref, intra_ref, zeta_ref, xi_ref, y_ref, st_ref, state):
        @pl.when(pl.program_id(0) == 0)
        def _():
            state[...] = jnp.zeros_like(state)

        cos2v, sin2v = cos_ref[...], sin_ref[...]
        for h in range(RET_HEADS):
            q = _rot(_head(q_ref, h, RET_DK), cos2v, sin2v)
            k = _rot(_head(k_ref, h, RET_DK), cos2v, sin2v) * scale
            st_ref[h, 0] = state[h]
            y, new_state = _ret_chunk(q, k, _head(v_ref, h, RET_DV), _head(g_ref, h, RET_DV), state[h],
                                      intra_ref[h], zeta_ref[h], xi_ref[h], decays[h])
            y_ref[:, h * RET_DV:(h + 1) * RET_DV] = y.astype(y_ref.dtype)
            state[h] = new_state

    return pl.pallas_call(
        body, name="retention_fwd", grid=(nc,), in_specs=specs,
        out_specs=[pl.BlockSpec((CHUNK, RET_V), lambda c: (c, 0)),
                   pl.BlockSpec((RET_HEADS, 1, RET_DK, RET_DV), lambda c: (0, c, 0, 0))],
        out_shape=[jax.ShapeDtypeStruct((length, RET_V), BF16),
                   jax.ShapeDtypeStruct((RET_HEADS, nc, RET_DK, RET_DV), F32)],
        scratch_shapes=[pltpu.VMEM((RET_HEADS, RET_DK, RET_DV), F32)],
        compiler_params=_params(("arbitrary",)),
    )(proj, proj, proj, proj, cos2, sin2, intra, zeta, xi)


def retention_bwd(proj, cos2, sin2, states, dmix):
    length = proj.shape[0]
    nc = length // CHUNK
    intra, zeta, xi, decays = _retention_consts()
    specs, cidx = _ret_specs(True, nc)
    scale = RET_DK ** -0.5

    def body(q_ref, k_ref, v_ref, g_ref, cos_ref, sin_ref, intra_ref, zeta_ref, xi_ref, st_ref, dy_ref,
             dq_ref, dk_ref, dv_ref, dg_ref, dstate):
        @pl.when(pl.program_id(0) == 0)
        def _():
            dstate[...] = jnp.zeros_like(dstate)

        cos2v, sin2v = cos_ref[...], sin_ref[...]
        for h in range(RET_HEADS):
            q = _rot(_head(q_ref, h, RET_DK), cos2v, sin2v)
            k = _rot(_head(k_ref, h, RET_DK), cos2v, sin2v) * scale
            intra_v, zeta_v, xi_v, decay = intra_ref[h], zeta_ref[h], xi_ref[h], decays[h]
            _, vjp = jax.vjp(lambda q, k, v, g, s: _ret_chunk(q, k, v, g, s, intra_v, zeta_v, xi_v, decay),
                             q, k, _head(v_ref, h, RET_DV), _head(g_ref, h, RET_DV), st_ref[h, 0])
            dq, dk, dv, dg, ds = vjp((_head(dy_ref, h, RET_DV).astype(F32), dstate[h]))
            dq_ref[:, h * RET_DK:(h + 1) * RET_DK] = _rot_t(dq, cos2v, sin2v).astype(dq_ref.dtype)
            dk_ref[:, h * RET_DK:(h + 1) * RET_DK] = _rot_t(dk * scale, cos2v, sin2v).astype(dk_ref.dtype)
            dv_ref[:, h * RET_DV:(h + 1) * RET_DV] = dv.astype(dv_ref.dtype)
            dg_ref[:, h * RET_DV:(h + 1) * RET_DV] = dg.astype(dg_ref.dtype)
            dstate[h] = ds

    specs = specs + [pl.BlockSpec((RET_HEADS, 1, RET_DK, RET_DV), lambda c: (0, cidx(c), 0, 0)),
                     pl.BlockSpec((CHUNK, RET_V), lambda c: (cidx(c), 0))]
    row = lambda width: pl.BlockSpec((CHUNK, width), lambda c: (cidx(c), 0))
    return pl.pallas_call(
        body, name="retention_bwd", grid=(nc,), in_specs=specs,
        out_specs=[row(RET_QK), row(RET_QK), row(RET_V), row(RET_V)],
        out_shape=[jax.ShapeDtypeStruct((length, RET_QK), BF16), jax.ShapeDtypeStruct((length, RET_QK), BF16),
                   jax.ShapeDtypeStruct((length, RET_V), BF16), jax.ShapeDtypeStruct((length, RET_V), BF16)],
        scratch_shapes=[pltpu.VMEM((RET_HEADS, RET_DK, RET_DV), F32)],
        compiler_params=_params(("arbitrary",)),
    )(proj, proj, proj, proj, cos2, sin2, intra, zeta, xi, states, dmix)


def _ssd_consts():
    tri = np.tril(np.ones((CHUNK, CHUNK), np.float32))
    expand = np.zeros((LANE, SSM_DINNER), np.float32)
    for h in range(SSM_HEADS):
        expand[h, h * SSM_P:(h + 1) * SSM_P] = 1.0
    return jnp.asarray(tri), jnp.asarray(tri.T.copy()), jnp.asarray(expand)


def _ssd_chunk(xs, bm, cm, dtr, z, state, dt_bias, a_log, d_skip, norm_w, tri, tri_t, expand):
    gw = SSM_DINNER // SSM_GROUPS
    dt = jax.nn.softplus(dtr + dt_bias)
    a_neg = -jnp.exp(a_log)
    da = dt * a_neg
    acs = _dot_hi(tri, da)
    acs_t = _dot_hi_tn(da, tri_t)
    dt_x = _times_01(dt, expand)
    a_x = jnp.mean(_dot_hi(jnp.broadcast_to(a_neg, (8, LANE)), expand), axis=0, keepdims=True)
    da_x = dt_x * a_x
    acs_x = _01_times(tri, da_x)
    tot_x = jnp.sum(da_x, axis=0, keepdims=True)
    x_dt = xs * dt_x
    x_dec = x_dt * jnp.exp(tot_x - acs_x)
    e_acs = jnp.exp(acs_x)
    e_tot = jnp.exp(tot_x)
    lane = lax.broadcasted_iota(jnp.int32, (CHUNK, LANE), 1)
    sub = lax.broadcasted_iota(jnp.int32, (CHUNK, LANE), 0)
    causal = sub >= lane
    ys, new_states = [], []
    for g in range(SSM_GROUPS):
        bg = bm[:, g * SSM_N:(g + 1) * SSM_N]
        cg = cm[:, g * SSM_N:(g + 1) * SSM_N]
        sg = state[:, g * gw:(g + 1) * gw]
        cb = _mm_nt(cg, bg)
        y_off = _mm(cg, sg) * e_acs[:, g * gw:(g + 1) * gw]
        new_states.append(sg * e_tot[:, g * gw:(g + 1) * gw] + _mm_tn(bg, x_dec[:, g * gw:(g + 1) * gw]))
        pairs = []
        for p in range(gw // LANE):
            hp = g * (gw // LANE) + p
            xp = x_dt[:, hp * LANE:(hp + 1) * LANE]
            halves = []
            for head in (2 * hp, 2 * hp + 1):
                col = jnp.sum(jnp.where(lane == head, acs, 0.0), axis=1, keepdims=True)
                row = jnp.sum(jnp.where(sub == head, acs_t, 0.0), axis=0, keepdims=True)
                decay = jnp.exp(jnp.where(causal, col - row, -1e30))
                halves.append(_mm(cb * decay, xp))
            pairs.append(jnp.where(lane < SSM_P, halves[0], halves[1]))
        ys.append(jnp.concatenate(pairs, axis=1) + y_off)
    d_x = jnp.mean(_dot_hi(jnp.broadcast_to(d_skip, (8, LANE)), expand), axis=0, keepdims=True)
    y = (jnp.concatenate(ys, axis=1) + d_x * xs) * _silu(z)
    normed = []
    for g in range(SSM_GROUPS):
        yg = y[:, g * gw:(g + 1) * gw]
        normed.append(yg * lax.rsqrt(jnp.mean(yg * yg, axis=-1, keepdims=True) + EPS))
    return jnp.concatenate(normed, axis=1) * norm_w, jnp.concatenate(new_states, axis=1)


XBC = SSM_DINNER + 2 * SSM_GROUPS * SSM_N


def _ssd_specs(rev, nc):
    def cidx(c):
        return nc - 1 - c if rev else c
    row = lambda w, j: pl.BlockSpec((CHUNK, w), lambda c: (cidx(c), j))
    whole = lambda shape: pl.BlockSpec(shape, lambda c: (0,) * len(shape))
    return [row(XBC, 0), row(LANE, 0), row(SSM_DINNER, 3),
            whole((1, LANE)), whole((1, LANE)), whole((1, LANE)), whole((1, SSM_DINNER)),
            whole((CHUNK, CHUNK)), whole((CHUNK, CHUNK)), whole((LANE, SSM_DINNER))], cidx


def ssd_fwd(xbc, dt_raw, proj, dt_bias, a_log, d_skip, norm_w):
    length = proj.shape[0]
    nc = length // CHUNK
    tri, tri_t, expand = _ssd_consts()
    specs, _ = _ssd_specs(False, nc)

    def body(xbc_ref, dt_ref, z_ref, dtb_ref, alog_ref, d_ref, nw_ref, tri_ref, trit_ref, e_ref, y_ref, st_ref, state):
        @pl.when(pl.program_id(0) == 0)
        def _():
            state[...] = jnp.zeros_like(state)

        st_ref[0] = state[...]
        y, new_state = _ssd_chunk(
            xbc_ref[:, 0:SSM_DINNER], xbc_ref[:, SSM_DINNER:SSM_DINNER + 256], xbc_ref[:, SSM_DINNER + 256:XBC],
            dt_ref[...], z_ref[...].astype(F32), state[...], dtb_ref[...], alog_ref[...], d_ref[...], nw_ref[...],
            tri_ref[...], trit_ref[...], e_ref[...])
        y_ref[...] = y.astype(y_ref.dtype)
        state[...] = new_state

    return pl.pallas_call(
        body, name="ssd_fwd", grid=(nc,), in_specs=specs,
        out_specs=[pl.BlockSpec((CHUNK, SSM_DINNER), lambda c: (c, 0)),
                   pl.BlockSpec((1, SSM_N, SSM_DINNER), lambda c: (c, 0, 0))],
        out_shape=[jax.ShapeDtypeStruct((length, SSM_DINNER), BF16),
                   jax.ShapeDtypeStruct((nc, SSM_N, SSM_DINNER), F32)],
        scratch_shapes=[pltpu.VMEM((SSM_N, SSM_DINNER), F32)],
        compiler_params=_params(("arbitrary",)),
    )(xbc, dt_raw, proj, dt_bias, a_log, d_skip, norm_w, tri, tri_t, expand)


def ssd_bwd(xbc, dt_raw, proj, dt_bias, a_log, d_skip, norm_w, states, dmix):
    length = proj.shape[0]
    nc = length // CHUNK
    tri, tri_t, expand = _ssd_consts()
    specs, cidx = _ssd_specs(True, nc)

    def body(xbc_ref, dt_ref, z_ref, dtb_ref, alog_ref, d_ref, nw_ref, tri_ref, trit_ref, e_ref, st_ref, dy_ref,
             dxbc_ref, ddt_ref, dz_ref, ddtb_ref, dalog_ref, dd_ref, dnw_ref, dstate):
        c = pl.program_id(0)

        @pl.when(c == 0)
        def _():
            dstate[...] = jnp.zeros_like(dstate)

        tri_v, trit_v, e_v = tri_ref[...], trit_ref[...], e_ref[...]
        _, vjp = jax.vjp(
            lambda *a: _ssd_chunk(*a, tri_v, trit_v, e_v),
            xbc_ref[:, 0:SSM_DINNER], xbc_ref[:, SSM_DINNER:SSM_DINNER + 256], xbc_ref[:, SSM_DINNER + 256:XBC],
            dt_ref[...], z_ref[...].astype(F32), st_ref[0], dtb_ref[...], alog_ref[...], d_ref[...], nw_ref[...])
        dxs, dbm, dcm, ddt, dz, ds, ddtb, dalog, dd, dnw = vjp((dy_ref[...].astype(F32), dstate[...]))
        dxbc_ref[:, 0:SSM_DINNER] = dxs
        dxbc_ref[:, SSM_DINNER:SSM_DINNER + 256] = dbm
        dxbc_ref[:, SSM_DINNER + 256:XBC] = dcm
        ddt_ref[...] = ddt.astype(ddt_ref.dtype)
        dz_ref[...] = dz.astype(dz_ref.dtype)
        dstate[...] = ds
        for r, d in ((ddtb_ref, ddtb), (dalog_ref, dalog), (dd_ref, dd), (dnw_ref, dnw)):
            @pl.when(c == 0)
            def _(r=r, d=d):
                r[...] = d

            @pl.when(c > 0)
            def _(r=r, d=d):
                r[...] += d

    whole = lambda shape: pl.BlockSpec(shape, lambda c: (0,) * len(shape))
    specs = specs + [pl.BlockSpec((1, SSM_N, SSM_DINNER), lambda c: (cidx(c), 0, 0)),
                     pl.BlockSpec((CHUNK, SSM_DINNER), lambda c: (cidx(c), 1))]
    return pl.pallas_call(
        body, name="ssd_bwd", grid=(nc,), in_specs=specs,
        out_specs=[pl.BlockSpec((CHUNK, XBC), lambda c: (cidx(c), 0)), pl.BlockSpec((CHUNK, LANE), lambda c: (cidx(c), 0)),
                   pl.BlockSpec((CHUNK, SSM_DINNER), lambda c: (cidx(c), 0)),
                   whole((1, LANE)), whole((1, LANE)), whole((1, LANE)), whole((1, SSM_DINNER))],
        out_shape=[jax.ShapeDtypeStruct((length, XBC), F32), jax.ShapeDtypeStruct((length, LANE), BF16),
                   jax.ShapeDtypeStruct((length, SSM_DINNER), BF16),
                   jax.ShapeDtypeStruct((1, LANE), F32), jax.ShapeDtypeStruct((1, LANE), F32),
                   jax.ShapeDtypeStruct((1, LANE), F32), jax.ShapeDtypeStruct((1, SSM_DINNER), F32)],
        scratch_shapes=[pltpu.VMEM((SSM_N, SSM_DINNER), F32)],
        compiler_params=_params(("arbitrary",)),
    )(xbc, dt_raw, proj, dt_bias, a_log, d_skip, norm_w, tri, tri_t, expand, states, dmix)


def _cmul(ar, ai, br, bi):
    return ar * br - ai * bi, ar * bi + ai * br


def s5_scan(b_re, b_im, a_re, a_im, *, reverse=False, states=None, name, lw=256):
    length, lanes = b_re.shape
    nk = length // SCAN_SEG
    with_da = states is not None
    assert reverse or not with_da

    def shift(v):
        sub = lax.broadcasted_iota(jnp.int32, v.shape, 0)
        if reverse:
            return jnp.where(sub == SCAN_SEG - 1, 0.0, pltpu.roll(v, SCAN_SEG - 1, 0))
        return jnp.where(sub == 0, 0.0, pltpu.roll(v, 1, 0))

    def body(*refs):
        if with_da:
            bre_ref, bim_ref, are_ref, aim_ref, sre_ref, sim_ref, xre_ref, xim_ref, dare_ref, daim_ref = refs
        else:
            bre_ref, bim_ref, are_ref, aim_ref, xre_ref, xim_ref = refs
        ar = jnp.broadcast_to(are_ref[...], (SCAN_SEG, lw))
        ai = jnp.broadcast_to(aim_ref[...], (SCAN_SEG, lw))

        def tile(i):
            k = (nk - 1 - i) if reverse else i
            return pl.ds(pl.multiple_of(k * SCAN_SEG, SCAN_SEG), SCAN_SEG)

        def local(i, carry):
            xr, xi, pr, pi = carry
            rows = tile(i)
            mr, mi = _cmul(ar, ai, xr, xi)
            xr, xi = mr + bre_ref[rows, :], mi + bim_ref[rows, :]
            xre_ref[rows, :] = xr
            xim_ref[rows, :] = xi
            pr, pi = _cmul(ar, ai, pr, pi)
            return xr, xi, pr, pi

        zero = jnp.zeros((SCAN_SEG, lw), F32)
        one = jnp.ones((SCAN_SEG, lw), F32)
        er, ei, pr, pi = lax.fori_loop(0, nk, local, (zero, zero, one, zero))
        cr, ci = zero, zero
        for _ in range(SCAN_SEG - 1):
            mr, mi = _cmul(pr, pi, cr, ci)
            cr, ci = shift(er + mr), shift(ei + mi)

        def fix(i, carry):
            pr, pi, dr, di = carry
            rows = tile(i)
            pr, pi = _cmul(ar, ai, pr, pi)
            mr, mi = _cmul(pr, pi, cr, ci)
            xr, xi = xre_ref[rows, :] + mr, xim_ref[rows, :] + mi
            xre_ref[rows, :] = xr
            xim_ref[rows, :] = xi
            if with_da:
                k = nk - 1 - i
                prev = pl.ds(pl.multiple_of(jnp.maximum(k - 1, 0) * SCAN_SEG, SCAN_SEG), SCAN_SEG)
                last = pl.ds((nk - 1) * SCAN_SEG, SCAN_SEG)
                sub = lax.broadcasted_iota(jnp.int32, (SCAN_SEG, lw), 0)
                wr = jnp.where(sub == 0, 0.0, pltpu.roll(sre_ref[last, :], 1, 0))
                wi = jnp.where(sub == 0, 0.0, pltpu.roll(sim_ref[last, :], 1, 0))
                sr = jnp.where(k == 0, wr, sre_ref[prev, :])
                si = jnp.where(k == 0, wi, sim_ref[prev, :])
                dr, di = dr + xr * sr + xi * si, di + xi * sr - xr * si
            return pr, pi, dr, di

        _, _, dr, di = lax.fori_loop(0, nk, fix, (one, zero, zero, zero))
        if with_da:
            dare_ref[...] = jnp.sum(dr, axis=0, keepdims=True)
            daim_ref[...] = jnp.sum(di, axis=0, keepdims=True)

    col = pl.BlockSpec((length, lw), lambda j: (0, j))
    vec = pl.BlockSpec((1, lw), lambda j: (0, j))
    ins = [b_re, b_im, a_re, a_im] + (list(states) if with_da else [])
    in_specs = [col, col, vec, vec] + ([col, col] if with_da else [])
    out_specs = [col, col] + ([vec, vec] if with_da else [])
    out_shape = [jax.ShapeDtypeStruct((length, lanes), F32)] * 2 + ([jax.ShapeDtypeStruct((1, lanes), F32)] * 2 if with_da else [])
    return pl.pallas_call(
        body, name=name, grid=(lanes // lw,), in_specs=in_specs, out_specs=out_specs, out_shape=out_shape,
        compiler_params=_params(("parallel",)),
    )(*ins)


def _seg_interleave(v):
    length = v.shape[0]
    return v.reshape(SCAN_SEG, length // SCAN_SEG, -1).transpose(1, 0, 2).reshape(length, -1)


def _seg_deinterleave(v):
    length = v.shape[0]
    return v.reshape(length // SCAN_SEG, SCAN_SEG, -1).transpose(1, 0, 2).reshape(length, -1)


def _block_diag(m):
    eye = jnp.eye(S5_GROUPS, dtype=m.dtype)
    return (m.reshape(S5_GROUPS, S5_GROUP, 1, S5_STATE) * eye[:, None, :, None]).reshape(S5_GROUPS * S5_GROUP, S5_LANES)


def _block_diag_take(full):
    idx = jnp.arange(S5_GROUPS)
    blocks = full.reshape(S5_GROUPS, S5_GROUP, S5_GROUPS, S5_STATE)[idx, :, idx, :]
    return blocks.reshape(S5_GROUPS * S5_GROUP, S5_STATE)


def _s5_prep(a_re, a_im, log_step, b_re, b_im, rep):
    step = jnp.exp(log_step)
    mag = jnp.exp(a_re * step)
    ab_re = mag * jnp.cos(a_im * step)
    ab_im = mag * jnp.sin(a_im * step)
    den = a_re * a_re + a_im * a_im
    f_re = ((ab_re - 1.0) * a_re + ab_im * a_im) / den
    f_im = (ab_im * a_re - (ab_re - 1.0) * a_im) / den
    fr, fi = _dot_hi(rep, f_re), _dot_hi(rep, f_im)
    return ab_re, ab_im, fr * b_re - fi * b_im, fr * b_im + fi * b_re


def _rms(x, g):
    return (x * lax.rsqrt(jnp.mean(x * x, axis=-1, keepdims=True) + EPS) * g,)


def _ffn_act(gate, up):
    return (_silu(gate) * up,)


def _glu(a, g):
    return (a * jax.nn.sigmoid(g),)


def _ln_silu(x, g, b):
    xc = x - jnp.mean(x, axis=-1, keepdims=True)
    var = jnp.mean(xc * xc, axis=-1, keepdims=True)
    return (_silu(xc * lax.rsqrt(var + EPS) * g + b),)


def _s5_post(y, u, d_skip, glu_w):
    s = jax.nn.gelu(y + d_skip * u)
    return (s * jax.nn.sigmoid(_mm(s, glu_w)),)


def loss_head(x, tgt, g, *, tl=512):
    length, d = x.shape
    tl = min(tl, length)

    def body(x_ref, t_ref, g_ref, loss_ref, dx_ref, dg_ref):
        i = pl.program_id(0)
        y, vjp = jax.vjp(lambda x, g: _rms(x, g)[0], x_ref[...], g_ref[...])
        err = y - t_ref[...]
        dx, dg = vjp(err * (1.0 / d))
        dx_ref[...] = dx
        part = jnp.broadcast_to(0.5 * jnp.sum(jnp.mean(err * err, axis=-1, keepdims=True), axis=0, keepdims=True), (1, LANE))

        @pl.when(i == 0)
        def _():
            loss_ref[...] = part
            dg_ref[...] = dg

        @pl.when(i > 0)
        def _():
            loss_ref[...] += part
            dg_ref[...] += dg

    row = pl.BlockSpec((tl, d), lambda i: (i, 0))
    return pl.pallas_call(
        body, name="loss_head", grid=(length // tl,),
        in_specs=[row, row, pl.BlockSpec((1, d), lambda i: (0, 0))],
        out_specs=[pl.BlockSpec((1, LANE), lambda i: (0, 0)), row, pl.BlockSpec((1, d), lambda i: (0, 0))],
        out_shape=[jax.ShapeDtypeStruct((1, LANE), F32), jax.ShapeDtypeStruct((length, d), F32),
                   jax.ShapeDtypeStruct((1, d), F32)],
        compiler_params=_params(("arbitrary",)),
    )(x, tgt, g)


def _pad_heads(v):
    return jnp.pad(v, ((0, 0), (0, LANE - v.shape[1])))


def local_step(x, tgt, w, late_weights=None, early_reduce=None):
    length = x.shape[0]
    cos2, sin2 = _rotary_tables(length)
    grads = {}
    w = dict(w)

    def rms_fwd(xin, g, name):
        return rowwise_fwd(_rms, [xin], [], [g], [], [(D_MODEL, BF16)], name=name, tl=512)[0]

    def rms_bwd(xin, g, dh, dxo, name):
        return rowwise_bwd(_rms, [xin], [], [g], [], [dh], [F32], name=name, tl=512, add=dxo)

    def ffn_fwd(i, xin):
        hf = rms_fwd(xin, w["ffn_norm"][i:i + 1], f"ffn{i}_norm")
        w_up = ffn_interleave(w["ffn_w_up"][i], name=f"ffn{i}_up_pairs")
        a = matmul(hf, w_up, out_dtype=BF16, name=f"ffn{i}_up")
        act = ffn_conv_act(a, ffn_interleave(w["ffn_dw_w"][i]), ffn_interleave(w["ffn_dw_b"][i:i + 1]),
                           name=f"ffn{i}_conv_act")
        return matmul(act, w["ffn_w_down"][i], res=xin, name=f"ffn{i}_down"), (hf, a, act, w_up)

    def ffn_bwd(i, xin, saved, dxo):
        hf, a, act, w_up = saved
        dact = matmul(dxo, w["ffn_w_down"][i], tb=True, name=f"ffn{i}_down_dx")
        dw_down = matmul(act, dxo, ta=True, name=f"ffn{i}_down_dw")
        da, ddw_w, ddw_b = ffn_conv_act_bwd(a, ffn_interleave(w["ffn_dw_w"][i]), ffn_interleave(w["ffn_dw_b"][i:i + 1]),
                                            dact, name=f"ffn{i}_conv_act_bwd")
        dw_up = ffn_pairs_to_shards(matmul(hf, da, ta=True, name=f"ffn{i}_up_dw"), name=f"ffn{i}_up_dw_shards")
        dhf = matmul(da, w_up, tb=True, name=f"ffn{i}_up_dx")
        dxin, dnorm = rms_bwd(xin, w["ffn_norm"][i:i + 1], dhf, dxo, f"ffn{i}_norm_bwd")
        return dxin, dict(ffn_norm=dnorm, ffn_w_up=dw_up, ffn_dw_w=ffn_deinterleave(ddw_w),
                          ffn_dw_b=ffn_deinterleave(ddw_b), ffn_w_down=dw_down)

    w_in_e = jnp.pad(w["e_w_in"][0], ((0, 0), (0, EVEN_IN_PAD - EVEN_IN)))
    conv_w_e, conv_b_e = w["e_conv_w"][0], w["e_conv_b"]
    dt_bias, a_log, d_skip = _pad_heads(w["e_dt_bias"]), _pad_heads(w["e_a_log"]), _pad_heads(w["e_d"])
    xbc_off = 4 * D_MODEL

    hn0 = rms_fwd(x, w["mix_norm"][0:1], "mix0_norm")
    proj0 = matmul(hn0, w_in_e, out_dtype=BF16, name="even_in")
    dt_raw = matmul(hn0, w_in_e[:, EVEN_IN_PAD - LANE:], name="even_in_dt")
    y_ret, ret_states = retention_fwd(proj0, cos2, sin2)
    xbc = conv_fwd(proj0, conv_w_e, conv_b_e, act=True, off=xbc_off, name="ssd_conv")
    y_ssm, ssd_states = ssd_fwd(xbc, dt_raw, proj0, dt_bias, a_log, d_skip, w["e_ssm_norm"])
    mix0 = jnp.concatenate([y_ret, y_ssm], axis=1)
    if late_weights is not None:
        w.update(late_weights(y_ssm))
    w_out_e = w["e_w_out"][0]
    x1 = matmul(mix0, w_out_e, res=x, name="even_out")
    x2, ffn0_saved = ffn_fwd(0, x1)

    w_in_o, w_out_o, glu_w = w["o_w_in"][0], w["o_w_out"][0], w["o_glu_w"][0]
    dw_w_o, dw_b_o, ln_g, ln_b, d_o = w["o_dw_w"][0], w["o_dw_b"], w["o_ln_g"], w["o_ln_b"], w["o_d"]
    rep = jnp.asarray(np.repeat(np.eye(S5_GROUPS, dtype=np.float32), S5_GROUP, axis=0))
    rows_gc = (S5_GROUPS * S5_GROUP, S5_STATE)
    prep_in = [w["o_a_re"][0], w["o_a_im"][0], w["o_log_step"].reshape(S5_GROUPS, 1),
               w["o_b_re"][0].transpose(0, 2, 1).reshape(rows_gc), w["o_b_im"][0].transpose(0, 2, 1).reshape(rows_gc), rep]
    ab_re, ab_im, bb_re, bb_im = whole_fwd(
        _s5_prep, prep_in, [(S5_GROUPS, S5_STATE)] * 2 + [rows_gc] * 2, name="s5_prep")
    a_re_row, a_im_row = ab_re.reshape(1, S5_LANES), ab_im.reshape(1, S5_LANES)
    b_re_bd, b_im_bd = _block_diag(bb_re).astype(BF16), _block_diag(bb_im).astype(BF16)
    c_re_bd = _block_diag(w["o_c_re"][0].reshape(rows_gc)).astype(BF16)
    c_im_neg_bd = _block_diag(-w["o_c_im"][0].reshape(rows_gc)).astype(BF16)

    hn1 = rms_fwd(x2, w["mix_norm"][1:2], "mix1_norm")
    proj1 = matmul(hn1, w_in_o, name="odd_in")
    half = D_MODEL // 2
    c_glu = rowwise_fwd(_glu, [Cols(proj1, half, 0), Cols(proj1, half, 1)], [], [], [], [(half, F32)],
                        name="conf_glu", tl=512)[0]
    c_conv = conv_fwd(c_glu, dw_w_o, dw_b_o, act=False, name="conf_conv")
    c_out = rowwise_fwd(_ln_silu, [c_conv], [], [ln_g, ln_b], [], [(half, BF16)], name="conf_ln", tl=512)[0]
    u_seg = _seg_interleave(proj1[:, 2 * half:])
    bu_re = matmul(u_seg, b_re_bd, name="s5_bu_re")
    bu_im = matmul(u_seg, b_im_bd, name="s5_bu_im")
    xs_re, xs_im = s5_scan(bu_re, bu_im, a_re_row, a_im_row, name="s5_scan")
    y_im = matmul(xs_im, c_im_neg_bd, tb=True, name="s5_y_im")
    y_s5 = _seg_deinterleave(matmul(xs_re, c_re_bd, tb=True, res=y_im, name="s5_y_re"))
    s_out = rowwise_fwd(_s5_post, [y_s5, Cols(proj1, half, 2)], [], [d_o, glu_w], [], [(half, BF16)],
                        name="s5_post", tl=512)[0]
    mix1 = jnp.concatenate([c_out, s_out], axis=1)
    x3 = matmul(mix1, w_out_o, res=x2, name="odd_out")
    x4, ffn1_saved = ffn_fwd(1, x3)

    loss, dx4, dfinal = loss_head(x4, tgt, w["final_norm"].reshape(1, D_MODEL))
    grads["final_norm"] = dfinal.reshape(D_MODEL)

    dx3, g_ffn1 = ffn_bwd(1, x3, ffn1_saved, dx4)
    dmix1 = matmul(dx3, w_out_o, tb=True, name="odd_out_dx")
    grads["o_w_out"] = [matmul(mix1, dx3, ta=True, name="odd_out_dw")]
    dc_conv, dln_g, dln_b = rowwise_bwd(_ln_silu, [c_conv], [], [ln_g, ln_b], [], [Cols(dmix1, half, 0)], [F32],
                                        name="conf_ln_bwd", tl=512)
    dc_glu, ddw_w_o, ddw_b_o = conv_bwd(c_glu, dw_w_o, dw_b_o, dc_conv, act=False, name="conf_conv_bwd")
    d_cacg = rowwise_bwd(_glu, [Cols(proj1, half, 0), Cols(proj1, half, 1)], [], [], [], [dc_glu], [BF16],
                         name="conf_glu_bwd", tl=512, merge=True)[0]
    dy_s5, du_post, dd_o, dglu_w = rowwise_bwd(
        _s5_post, [y_s5, Cols(proj1, half, 2)], [], [d_o, glu_w], [], [Cols(dmix1, half, 1)], [F32, F32],
        name="s5_post_bwd", tl=512)
    dy_seg = _seg_interleave(dy_s5)
    dxs_re = matmul(dy_seg, c_re_bd, name="s5_dx_re")
    dxs_im = matmul(dy_seg, c_im_neg_bd, name="s5_dx_im")
    dc_re_bd = matmul(dy_seg, xs_re, ta=True, name="s5_dc_re")
    dc_im_neg_bd = matmul(dy_seg, xs_im, ta=True, name="s5_dc_im")
    g_re, g_im, dab_re, dab_im = s5_scan(dxs_re, dxs_im, a_re_row, -a_im_row, reverse=True, states=(xs_re, xs_im),
                                         name="s5_scan_bwd", lw=LANE)
    dbb_re = _block_diag_take(matmul(u_seg, g_re, ta=True, name="s5_db_re"))
    dbb_im = _block_diag_take(matmul(u_seg, g_im, ta=True, name="s5_db_im"))
    du_im = matmul(g_im, b_im_bd, tb=True, name="s5_du_im")
    du = _seg_deinterleave(matmul(g_re, b_re_bd, tb=True, res=du_im, name="s5_du_re")) + du_post
    da_re, da_im, dlog_step, db_re, db_im = whole_bwd(
        _s5_prep, prep_in, 5,
        [dab_re.reshape(S5_GROUPS, S5_STATE), dab_im.reshape(S5_GROUPS, S5_STATE), dbb_re, dbb_im], name="s5_prep_bwd")
    gcn = (S5_GROUPS, S5_GROUP, S5_STATE)
    grads.update(
        o_a_re=da_re[None], o_a_im=da_im[None], o_log_step=dlog_step.reshape(1, S5_GROUPS),
        o_b_re=db_re.reshape(gcn).transpose(0, 2, 1)[None], o_b_im=db_im.reshape(gcn).transpose(0, 2, 1)[None],
        o_c_re=_block_diag_take(dc_re_bd).reshape(gcn)[None], o_c_im=-_block_diag_take(dc_im_neg_bd).reshape(gcn)[None],
        o_d=dd_o, o_glu_w=[dglu_w], o_dw_w=ddw_w_o[None], o_dw_b=ddw_b_o, o_ln_g=dln_g, o_ln_b=dln_b)
    dproj1 = jnp.concatenate([d_cacg, du.astype(BF16)], axis=1)
    grads["o_w_in"] = [matmul(hn1, dproj1, ta=True, name="odd_in_dw")]
    dhn1 = matmul(dproj1, w_in_o, tb=True, name="odd_in_dx")
    dx2, dmix_norm1 = rms_bwd(x2, w["mix_norm"][1:2], dhn1, dx3, "mix1_norm_bwd")

    if early_reduce is not None:
        zero = early_reduce({("o_w_in", 0): grads["o_w_in"][0], ("o_glu_w", 0): grads["o_glu_w"][0],
                             ("o_w_out", 0): grads["o_w_out"][0], ("ffn_w_up", 1): g_ffn1["ffn_w_up"],
                             ("ffn_w_down", 1): g_ffn1["ffn_w_down"]})
        w["ffn_dw_b"] = w["ffn_dw_b"] + zero
    dx1, g_ffn0 = ffn_bwd(0, x1, ffn0_saved, dx2)
    if early_reduce is not None:
        dt_bias = dt_bias + early_reduce({("ffn_w_up", 0): g_ffn0["ffn_w_up"], ("ffn_w_down", 0): g_ffn0["ffn_w_down"]})
    for k in g_ffn0:
        per_layer = [g_ffn0[k], g_ffn1[k]]
        grads[k] = per_layer if k in ("ffn_w_up", "ffn_w_down") else jnp.stack(per_layer).reshape(w[k].shape)
    dmix0 = matmul(dx1, w_out_e, tb=True, name="even_out_dx")
    grads["e_w_out"] = [matmul(mix0, dx1, ta=True, name="even_out_dw")]
    dq, dk, dv, dg = retention_bwd(proj0, cos2, sin2, ret_states, dmix0)
    dxbc_c, ddt, dz, ddt_bias, da_log, dd_skip, dssm_norm = ssd_bwd(
        xbc, dt_raw, proj0, dt_bias, a_log, d_skip, w["e_ssm_norm"], ssd_states, dmix0)
    dxbc, dconv_w, dconv_b = conv_bwd(proj0, conv_w_e, conv_b_e, dxbc_c, act=True, off=xbc_off,
                                      name="ssd_conv_bwd", dx_dtype=BF16)
    dproj0 = jnp.concatenate([dq, dk, dv, dg, dz, dxbc, ddt], axis=1)
    grads["e_w_in"] = [matmul(hn0, dproj0, ta=True, name="even_in_dw")[:, :EVEN_IN]]
    dhn0 = matmul(dproj0, w_in_e, tb=True, name="even_in_dx")
    dx, dmix_norm0 = rms_bwd(x, w["mix_norm"][0:1], dhn0, dx1, "mix0_norm_bwd")
    grads.update(
        mix_norm=jnp.concatenate([dmix_norm0, dmix_norm1], axis=0), e_conv_w=dconv_w[None], e_conv_b=dconv_b,
        e_dt_bias=ddt_bias[:, :SSM_HEADS], e_a_log=da_log[:, :SSM_HEADS], e_d=dd_skip[:, :SSM_HEADS],
        e_ssm_norm=dssm_norm)
    return loss, dx, grads


def adamw(w, g, m, v, *, name):
    shape = w.shape
    cols = shape[-1]
    rows = w.size // cols
    tr = _tile(rows, max(8, (512 * 1024 // cols) // 8 * 8), unit=8)

    def body(w_ref, g_ref, m_ref, v_ref, d_ref, nm_ref, nv_ref):
        gv = g_ref[...]
        nm = ADAM_B1 * m_ref[...] + (1.0 - ADAM_B1) * gv
        nv = ADAM_B2 * v_ref[...] + (1.0 - ADAM_B2) * jnp.square(gv)
        m_hat = nm / (1.0 - ADAM_B1 ** ADAM_STEP)
        v_hat = nv / (1.0 - ADAM_B2 ** ADAM_STEP)
        d_ref[...] = -ADAM_LR * (m_hat / (jnp.sqrt(v_hat) + ADAM_EPS) + ADAM_WD * w_ref[...])
        nm_ref[...] = nm
        nv_ref[...] = nv

    spec = pl.BlockSpec((tr, cols), lambda i: (i, 0))
    outs = pl.pallas_call(
        body, name=name, grid=(rows // tr,), in_specs=[spec] * 4, out_specs=[spec] * 3,
        out_shape=[jax.ShapeDtypeStruct((rows, cols), F32)] * 3, compiler_params=_params(("parallel",)),
    )(*[t.reshape(rows, cols) for t in (w, g, m, v)])
    return [o.reshape(shape) for o in outs]


OTHER_CHIPS = ((1, 0), (0, 1), (1, 1))
ANY = pl.BlockSpec(memory_space=pl.ANY)


def _position():
    return lax.axis_index("x"), lax.axis_index("y"), lax.axis_index("c")


def _flip(v, f):
    return 1 - v if f else v


def _remote(src, dst, send_sem, recv_sem, device):
    return pltpu.make_async_remote_copy(src_ref=src, dst_ref=dst, send_sem=send_sem, recv_sem=recv_sem,
                                        device_id=device, device_id_type=MESH)


def gather_shards(big, small):
    n_big, n_small = len(big), len(small)
    halves = [a.shape[0] // 2 for a in big]

    def body(*refs):
        big_refs, small_refs = refs[:n_big], refs[n_big:n_big + n_small]
        obig_refs = refs[n_big + n_small:2 * n_big + n_small]
        osmall_refs = refs[2 * n_big + n_small:2 * (n_big + n_small)]
        ici_send, ici_recv, d2d_send, d2d_recv, small_send, small_recv = refs[2 * (n_big + n_small):]
        x, y, c = _position()
        mine = 2 * x + y

        def half(k, core):
            return pl.ds(pl.multiple_of(core * halves[k], 16), halves[k])

        sends = []
        for j, (fx, fy) in enumerate(OTHER_CHIPS):
            peer = (_flip(x, fx), _flip(y, fy), c)
            for k in range(n_big):
                sends.append(_remote(big_refs[k].at[half(k, c)], obig_refs[k].at[mine, half(k, c)],
                                     ici_send.at[j, k], ici_recv.at[j, k], peer))
            for k in range(n_small):
                sends.append(_remote(small_refs[k], osmall_refs[k].at[mine], small_send.at[j, k], small_recv.at[j, k], peer))
        for cp in sends:
            cp.start()
        for j, (fx, fy) in enumerate(OTHER_CHIPS):
            px, py = _flip(x, fx), _flip(y, fy)
            src_chip = 2 * px + py
            for k in range(n_big):
                landed = obig_refs[k].at[src_chip, half(k, c)]
                _remote(landed, landed, ici_send.at[j, k], ici_recv.at[j, k], (px, py, c)).wait_recv()
                fwd = _remote(landed, landed, d2d_send.at[j, k], d2d_recv.at[j, k], (x, y, 1 - c))
                fwd.start()
                sends.append(fwd)
        for j, (fx, fy) in enumerate(OTHER_CHIPS):
            px, py = _flip(x, fx), _flip(y, fy)
            src_chip = 2 * px + py
            for k in range(n_big):
                other = obig_refs[k].at[src_chip, half(k, 1 - c)]
                _remote(other, other, d2d_send.at[j, k], d2d_recv.at[j, k], (x, y, 1 - c)).wait_recv()
            for k in range(n_small):
                dst = osmall_refs[k].at[src_chip]
                _remote(small_refs[k], dst, small_send.at[j, k], small_recv.at[j, k], (px, py, c)).wait_recv()
        for cp in sends:
            cp.wait_send()

    arrays = list(big) + list(small)
    dma = pltpu.SemaphoreType.DMA
    return pl.pallas_call(
        body, name="gather_shards", in_specs=[ANY] * len(arrays), out_specs=[ANY] * len(arrays),
        out_shape=[jax.ShapeDtypeStruct((4,) + a.shape, a.dtype) for a in arrays],
        scratch_shapes=[dma((3, n_big)), dma((3, n_big)), dma((3, n_big)), dma((3, n_big)),
                        dma((3, n_small)), dma((3, n_small))],
        compiler_params=_params(),
    )(*arrays)


def allreduce_small(pack):
    rows = pack.shape[0]

    def body(p_ref, o_ref, slots, send_sems, recv_sems):
        x, y, c = _position()
        me = 4 * x + 2 * y + c
        slots[me] = p_ref[...]
        flips = [((k >> 2) & 1, (k >> 1) & 1, k & 1) for k in range(1, 8)]
        sends = []
        for k, (fx, fy, fc) in enumerate(flips):
            peer = (_flip(x, fx), _flip(y, fy), _flip(c, fc))
            sends.append(_remote(p_ref, slots.at[me], send_sems.at[k], recv_sems.at[k], peer))
        for cp in sends:
            cp.start()
        for k, (fx, fy, fc) in enumerate(flips):
            px, py, pc = _flip(x, fx), _flip(y, fy), _flip(c, fc)
            _remote(p_ref, slots.at[4 * px + 2 * py + pc], send_sems.at[k], recv_sems.at[k], (px, py, pc)).wait_recv()
        for cp in sends:
            cp.wait_send()
        acc = slots[0]
        for d in range(1, 8):
            acc = acc + slots[d]
        o_ref[...] = acc

    vmem = pl.BlockSpec(memory_space=pltpu.VMEM)
    return pl.pallas_call(
        body, name="allreduce_small", in_specs=[vmem], out_specs=vmem,
        out_shape=jax.ShapeDtypeStruct(pack.shape, F32),
        scratch_shapes=[pltpu.VMEM((8, rows, LANE), F32), pltpu.SemaphoreType.DMA((7,)), pltpu.SemaphoreType.DMA((7,))],
        compiler_params=_params(),
    )(pack)


def exchange_halves(gs, *, name):
    n = len(gs)

    def body(*refs):
        g_refs, o_refs, (send_sems, recv_sems) = refs[:n], refs[n:2 * n], refs[2 * n:]
        x, y, c = _position()
        copies = [_remote(g_refs[k].at[:, 1 - c], o_refs[k], send_sems.at[k], recv_sems.at[k], (x, y, 1 - c)) for k in range(n)]
        for cp in copies:
            cp.start()
        for cp in copies:
            cp.wait()

    return pl.pallas_call(
        body, name=name, in_specs=[ANY] * n, out_specs=[ANY] * n,
        out_shape=[jax.ShapeDtypeStruct((4,) + g.shape[2:], g.dtype) for g in gs],
        scratch_shapes=[pltpu.SemaphoreType.DMA((n,)), pltpu.SemaphoreType.DMA((n,))],
        compiler_params=_params(),
    )(*gs)


def scatter_to_chips(parts):
    n = len(parts)

    def body(*refs):
        a_refs, o_refs, (send_sems, recv_sems) = refs[:n], refs[n:2 * n], refs[2 * n:]
        x, y, c = _position()
        copies = []
        for j, (fx, fy) in enumerate(OTHER_CHIPS):
            px, py = _flip(x, fx), _flip(y, fy)
            for k in range(n):
                copies.append(_remote(a_refs[k].at[2 * px + py], o_refs[k].at[j], send_sems.at[j, k], recv_sems.at[j, k], (px, py, c)))
        for cp in copies:
            cp.start()
        for cp in copies:
            cp.wait()

    return pl.pallas_call(
        body, name="scatter_to_chips", in_specs=[ANY] * n, out_specs=[ANY] * n,
        out_shape=[jax.ShapeDtypeStruct((3,) + a.shape[1:], a.dtype) for a in parts],
        scratch_shapes=[pltpu.SemaphoreType.DMA((3, n)), pltpu.SemaphoreType.DMA((3, n))],
        compiler_params=_params(),
    )(*parts)


def swap_halves(rs):
    n = len(rs)

    def body(*refs):
        r_refs, o_refs, (send_sems, recv_sems) = refs[:n], refs[n:2 * n], refs[2 * n:]
        x, y, c = _position()
        copies = [_remote(r_refs[k], o_refs[k], send_sems.at[k], recv_sems.at[k], (x, y, 1 - c)) for k in range(n)]
        for cp in copies:
            cp.start()
        for cp in copies:
            cp.wait()

    dma = pltpu.SemaphoreType.DMA
    return pl.pallas_call(
        body, name="swap_halves", in_specs=[ANY] * n, out_specs=[ANY] * n,
        out_shape=[jax.ShapeDtypeStruct(r.shape, r.dtype) for r in rs],
        scratch_shapes=[dma((n,)), dma((n,))],
        compiler_params=_params(),
    )(*rs)


HBM = pl.BlockSpec(memory_space=pltpu.HBM)
SEM = pl.BlockSpec(memory_space=pltpu.SEMAPHORE)
SIDE_EFFECT = pltpu.SideEffectType.DATAFLOW_SIDE_EFFECTING


def _gather_plan(halves):
    def plan(v_refs, land_refs, x, y, c):
        copies = []
        for fx, fy in OTHER_CHIPS:
            for k in range(len(v_refs)):
                rows = pl.ds(pl.multiple_of(c * halves[k], 16), halves[k])
                copies.append((v_refs[k].at[rows], land_refs[k].at[2 * x + y, rows], (_flip(x, fx), _flip(y, fy), c)))
        return copies
    return plan


def _scatter_plan(v_refs, land_refs, x, y, c):
    copies = []
    for j, (fx, fy) in enumerate(OTHER_CHIPS):
        px, py = _flip(x, fx), _flip(y, fy)
        for k in range(len(v_refs)):
            copies.append((v_refs[k].at[2 * px + py], land_refs[k].at[j], (px, py, c)))
    return copies


def chip_exchange_start(srcs, land_shapes, plan, after, *, name):
    n = len(srcs)
    n_cp = 3 * n

    def body(*refs):
        v_refs, land_refs = refs[:n], refs[n:2 * n]
        outs = refs[2 * n + 1:]
        sends, recvs, token = outs[:n_cp], outs[n_cp:2 * n_cp], outs[-1]
        x, y, c = _position()
        for (src, dst, device), send, recv in zip(plan(v_refs, land_refs, x, y, c), sends, recvs, strict=True):
            _remote(src, dst, send, recv, device).start()
        token[...] = jnp.zeros_like(token)

    lands = [lax.empty(shape, v.dtype) for shape, v in zip(land_shapes, srcs)]
    arrays = [pltpu.with_memory_space_constraint(a, pltpu.HBM) for a in list(srcs) + lands]
    outs = pl.pallas_call(
        body, name=name,
        out_shape=tuple(pltpu.SemaphoreType.DMA(()) for _ in range(2 * n_cp))
        + tuple(pltpu.HBM(a.shape, a.dtype) for a in arrays) + (jax.ShapeDtypeStruct((8, LANE), F32),),
        in_specs=[HBM] * (2 * n) + [ANY],
        out_specs=(SEM,) * (2 * n_cp) + (HBM,) * (2 * n) + (pl.BlockSpec(memory_space=pltpu.VMEM),),
        input_output_aliases={i: 2 * n_cp + i for i in range(2 * n)},
        compiler_params=pltpu.CompilerParams(has_side_effects=SIDE_EFFECT),
    )(*arrays, after)
    handle = (outs[:n_cp], outs[n_cp:2 * n_cp], outs[2 * n_cp:2 * n_cp + n], outs[2 * n_cp + n:2 * n_cp + 2 * n])
    return handle, outs[-1]


def chip_exchange_wait(handle, plan, after, *, name):
    sends, recvs, v_thru, land_thru = handle
    n = len(v_thru)
    n_cp = 3 * n

    def body(*refs):
        v_refs, land_refs = refs[:n], refs[n:2 * n]
        sends, recvs = refs[2 * n:2 * n + n_cp], refs[2 * n + n_cp:2 * n + 2 * n_cp]
        x, y, c = _position()
        for (src, dst, device), send, recv in zip(plan(v_refs, land_refs, x, y, c), sends, recvs, strict=True):
            copy = _remote(src, dst, send, recv, device)
            copy.wait_send()
            copy.wait_recv()

    outs = pl.pallas_call(
        body, name=name,
        out_shape=tuple(pltpu.HBM(a.shape, a.dtype) for a in list(v_thru) + list(land_thru)),
        in_specs=[HBM] * (2 * n) + [SEM] * (2 * n_cp) + [ANY], out_specs=(HBM,) * (2 * n),
        input_output_aliases={i: i for i in range(2 * n)},
        compiler_params=pltpu.CompilerParams(has_side_effects=SIDE_EFFECT),
    )(*v_thru, *land_thru, *sends, *recvs, after)
    return outs[:n], outs[n:]


def finish_gather(lands):
    n = len(lands)
    halves = [a.shape[1] // 2 for a in lands]

    def body(*refs):
        o_refs, (send_sems, recv_sems) = refs[n:2 * n], refs[2 * n:]
        x, y, c = _position()

        def half(k, core):
            return pl.ds(pl.multiple_of(core * halves[k], 16), halves[k])

        sends = []
        for j, (fx, fy) in enumerate(OTHER_CHIPS):
            src_chip = 2 * _flip(x, fx) + _flip(y, fy)
            for k in range(n):
                held = o_refs[k].at[src_chip, half(k, c)]
                sends.append(_remote(held, held, send_sems.at[j, k], recv_sems.at[j, k], (x, y, 1 - c)))
        for cp in sends:
            cp.start()
        for j, (fx, fy) in enumerate(OTHER_CHIPS):
            src_chip = 2 * _flip(x, fx) + _flip(y, fy)
            for k in range(n):
                other = o_refs[k].at[src_chip, half(k, 1 - c)]
                _remote(other, other, send_sems.at[j, k], recv_sems.at[j, k], (x, y, 1 - c)).wait_recv()
        for cp in sends:
            cp.wait_send()

    dma = pltpu.SemaphoreType.DMA
    return pl.pallas_call(
        body, name="finish_gather", in_specs=[ANY] * n, out_specs=[ANY] * n,
        out_shape=[jax.ShapeDtypeStruct(a.shape, a.dtype) for a in lands],
        input_output_aliases={k: k for k in range(n)},
        scratch_shapes=[dma((3, n)), dma((3, n))],
        compiler_params=_params(),
    )(*lands)


def add_own_half(g, r, c_idx, *, name):
    _, _, h, cols = g.shape

    def body(c_ref, g_ref, r_ref, o_ref):
        o_ref[...] = (g_ref[0] + r_ref[...]).astype(o_ref.dtype)

    return pl.pallas_call(
        body, name=name,
        grid_spec=pltpu.PrefetchScalarGridSpec(
            num_scalar_prefetch=1, grid=(4,),
            in_specs=[pl.BlockSpec((1, 1, h, cols), lambda s, c: (s, c[0], 0, 0)),
                      pl.BlockSpec((1, h, cols), lambda s, c: (s, 0, 0))],
            out_specs=pl.BlockSpec((1, h, cols), lambda s, c: (s, 0, 0))),
        out_shape=jax.ShapeDtypeStruct(r.shape, BF16), compiler_params=_params(("parallel",)),
    )(c_idx, g, r)


def add_chip_parts(a, parts, chip_idx, *, name):
    _, h, cols = a.shape
    th = h // 2

    def body(s_ref, a_ref, p0_ref, p1_ref, p2_ref, o_ref):
        f = lambda r: r[0].astype(F32)
        o_ref[...] = ((f(a_ref) + f(p0_ref)) + f(p1_ref)) + f(p2_ref)

    part = lambda j: pl.BlockSpec((1, th, cols), lambda i, s, j=j: (j, i, 0))
    return pl.pallas_call(
        body, name=name,
        grid_spec=pltpu.PrefetchScalarGridSpec(
            num_scalar_prefetch=1, grid=(2,),
            in_specs=[pl.BlockSpec((1, th, cols), lambda i, s: (s[0], i, 0)), part(0), part(1), part(2)],
            out_specs=pl.BlockSpec((th, cols), lambda i, s: (i, 0))),
        out_shape=jax.ShapeDtypeStruct((h, cols), F32), compiler_params=_params(("parallel",)),
    )(chip_idx, a, parts, parts, parts)


WEIGHTS = ("mix_norm", "e_w_in", "e_conv_w", "e_conv_b", "e_dt_bias", "e_a_log", "e_d", "e_ssm_norm", "e_w_out",
           "o_w_in", "o_dw_w", "o_dw_b", "o_ln_g", "o_ln_b", "o_a_re", "o_a_im", "o_b_re", "o_b_im", "o_c_re",
           "o_c_im", "o_d", "o_log_step", "o_glu_w", "o_w_out", "ffn_norm", "ffn_w_up", "ffn_dw_w", "ffn_dw_b",
           "ffn_w_down", "final_norm")
BIG = (("e_w_in", 2), ("e_w_out", 1), ("o_w_in", 2), ("o_glu_w", 1), ("o_w_out", 1), ("ffn_w_up", 2), ("ffn_w_down", 1))
SMALL_SHARDED = (("e_conv_w", 2), ("o_dw_w", 2), ("o_dw_b", 1), ("o_ln_g", 1), ("o_ln_b", 1), ("o_d", 1), ("ffn_dw_w", 2))
REPLICATED = tuple(n for n in WEIGHTS if n not in dict(BIG + SMALL_SHARDED))
PACK_ROWS = 8


def _pack(arrays, dtype, row_unit=PACK_ROWS):
    flat = jnp.concatenate([a.astype(dtype).reshape(-1) for a in arrays])
    rows = -(-flat.size // (LANE * row_unit)) * row_unit
    return jnp.pad(flat, (0, rows * LANE - flat.size)).reshape(rows, LANE)


def _unpack(flat, shapes, lead=()):
    out, off = [], 0
    for shape in shapes:
        size = int(np.prod(shape))
        out.append(flat[..., off:off + size].reshape(lead + tuple(shape)))
        off += size
    return out


def _join_shards(parts, axis):
    return jnp.concatenate([parts[s] for s in range(4)], axis=axis)


def _split_shards(full, axis):
    return jnp.stack(jnp.split(full, 4, axis=axis))


def _rows2d(a):
    return a.reshape(-1, a.shape[-1])


def _layer_shards(g, axis):
    if g.ndim == 3:
        return g.reshape(4, 2, g.shape[1] // 2, g.shape[2])
    rows, cols = g.shape
    if axis == 0:
        return g.reshape(4, 2, rows // 8, cols)
    return g.reshape(rows, 4, cols // 4).transpose(1, 0, 2).reshape(4, 2, rows // 2, cols // 4)


def kernel(x, mix_norm, e_w_in, e_conv_w, e_conv_b, e_dt_bias, e_a_log, e_d, e_ssm_norm, e_w_out, o_w_in, o_dw_w, o_dw_b, o_ln_g, o_ln_b, o_a_re, o_a_im, o_b_re, o_b_im, o_c_re, o_c_im, o_d, o_log_step, o_glu_w, o_w_out, ffn_norm, ffn_w_up, ffn_dw_w, ffn_dw_b, ffn_w_down, final_norm, loss_target, m_mix_norm, m_e_w_in, m_e_conv_w, m_e_conv_b, m_e_dt_bias, m_e_a_log, m_e_d, m_e_ssm_norm, m_e_w_out, m_o_w_in, m_o_dw_w, m_o_dw_b, m_o_ln_g, m_o_ln_b, m_o_a_re, m_o_a_im, m_o_b_re, m_o_b_im, m_o_c_re, m_o_c_im, m_o_d, m_o_log_step, m_o_glu_w, m_o_w_out, m_ffn_norm, m_ffn_w_up, m_ffn_dw_w, m_ffn_dw_b, m_ffn_w_down, m_final_norm, v_mix_norm, v_e_w_in, v_e_conv_w, v_e_conv_b, v_e_dt_bias, v_e_a_log, v_e_d, v_e_ssm_norm, v_e_w_out, v_o_w_in, v_o_dw_w, v_o_dw_b, v_o_ln_g, v_o_ln_b, v_o_a_re, v_o_a_im, v_o_b_re, v_o_b_im, v_o_c_re, v_o_c_im, v_o_d, v_o_log_step, v_o_glu_w, v_o_w_out, v_ffn_norm, v_ffn_w_up, v_ffn_dw_w, v_ffn_dw_b, v_ffn_w_down, v_final_norm):
    given = dict(locals())
    chip = 2 * lax.axis_index("x") + lax.axis_index("y")
    core = lax.axis_index("c")

    core_idx, chip_idx = core.reshape(1).astype(jnp.int32), chip.reshape(1).astype(jnp.int32)

    def whole(n, axis, parts):
        shape = given[n].shape
        own = given[n].astype(parts.dtype)
        return _join_shards(lax.dynamic_update_index_in_dim(parts.reshape((4,) + shape), own, chip, 0), axis)

    first, later = BIG[:1], BIG[1:]
    shards = {n: _rows2d(given[n]).astype(BF16) for n, _ in BIG}
    gathered = gather_shards([shards[n] for n, _ in first], [_rows2d(given[n]) for n, _ in SMALL_SHARDED])
    w = {n: given[n] for n in REPLICATED}
    for (n, axis), parts in zip(first + SMALL_SHARDED, gathered):
        w[n] = whole(n, axis, parts)
    later_shards = [shards[n] for n, _ in later]
    gather_plan = _gather_plan([a.shape[0] // 2 for a in later_shards])
    gather_handle, token = chip_exchange_start(later_shards, [(4,) + a.shape for a in later_shards], gather_plan,
                                               gathered[0], name="gather_start")
    w["mix_norm"] = w["mix_norm"] + token[0, 0]

    def late_weights(after):
        _, lands = chip_exchange_wait(gather_handle, gather_plan, after, name="gather_wait")
        return {n: whole(n, axis, parts) for (n, axis), parts in zip(later, finish_gather(lands))}

    groups = []

    def finish_group(after):
        group = groups[-1]
        group["sums"], group["parts"] = chip_exchange_wait(group.pop("handle"), _scatter_plan, after,
                                                           name=f"scatter_wait_{len(groups) - 1}")

    def early_reduce(layer_grads):
        keys = list(layer_grads)
        if groups:
            finish_group(layer_grads[keys[0]])
        tag = len(groups)
        parts = [_layer_shards(layer_grads[k], dict(BIG)[k[0]] - 1) for k in keys]
        sums = [add_own_half(g, r, core_idx, name=f"add_own_half_{n}{layer}")
                for g, r, (n, layer) in zip(parts, exchange_halves(parts, name=f"exchange_halves_{tag}"), keys)]
        handle, zeros = chip_exchange_start(sums, [(3,) + a.shape[1:] for a in sums], _scatter_plan, sums[0],
                                            name=f"scatter_start_{tag}")
        groups.append(dict(keys=keys, handle=handle))
        return zeros[0, 0]

    loss, dx, grads = local_step(x[0], loss_target[0], w, late_weights, early_reduce)
    finish_group(dx)
    early_keys = [k for group in groups for k in group["keys"]]
    early_sums = [a for group in groups for a in group["sums"]]
    early_parts = [a for group in groups for a in group["parts"]]

    small_names = REPLICATED + tuple(n for n, _ in SMALL_SHARDED)
    small_sum = allreduce_small(_pack([grads[n] for n in small_names], F32))
    reduced = dict(zip(small_names, _unpack(small_sum.reshape(-1), [grads[n].shape for n in small_names])))
    for n, axis in SMALL_SHARDED:
        width = given[n].shape[axis]
        reduced[n] = lax.dynamic_slice_in_dim(reduced[n], chip * width, width, axis=axis)

    keys, parts = [], []
    for n, axis in BIG:
        for layer, g in enumerate(grads[n]):
            if (n, layer) not in early_keys:
                keys.append((n, layer))
                parts.append(_layer_shards(g, axis - 1))
    core_sums = [add_own_half(g, r, core_idx, name=f"add_own_half_{n}{layer}")
                 for g, r, (n, layer) in zip(parts, exchange_halves(parts, name="exchange_halves_last"), keys)]
    chip_parts = scatter_to_chips(core_sums)
    keys, core_sums, chip_parts = early_keys + keys, early_sums + core_sums, early_parts + list(chip_parts)
    mine = [add_chip_parts(a, p, chip_idx, name=f"add_chip_parts_{n}{layer}")
            for a, p, (n, layer) in zip(core_sums, chip_parts, keys)]
    layers = {}
    for (n, layer), own, other in zip(keys, mine, swap_halves(mine)):
        both = jnp.where(core == 0, jnp.stack([own, other]), jnp.stack([other, own]))
        layers.setdefault(n, {})[layer] = both.reshape(given[n].shape[1:])
    for n, _ in BIG:
        reduced[n] = jnp.stack([layers[n][layer] for layer in sorted(layers[n])])

    delta, new_m, new_v = {}, {}, {}
    for n, _ in BIG:
        delta[n], new_m[n], new_v[n] = adamw(given[n], reduced[n], given["m_" + n], given["v_" + n], name="adamw_" + n)
    shapes = [given[n].shape for n in small_names]
    packed = [_pack([src[n] for n in small_names], F32)
              for src in (given, reduced, {n: given["m_" + n] for n in small_names}, {n: given["v_" + n] for n in small_names})]
    for dst, res in zip((delta, new_m, new_v), adamw(*packed, name="adamw_small")):
        dst.update(zip(small_names, _unpack(res.reshape(-1), shapes)))

    total = lax.psum(loss[0, 0], ("x", "y", "c"))
    return (total, dx[None], *[reduced[n] for n in WEIGHTS], *[delta[n] for n in WEIGHTS],
            *[new_m[n] for n in WEIGHTS], *[new_v[n] for n in WEIGHTS])
```

```python
import functools
import math
from typing import NamedTuple

import numpy as np
import jax
import jax.numpy as jnp
from jax import lax
from jax.experimental import pallas as pl
from jax.experimental.pallas import tpu as pltpu

F32 = jnp.float32
BF16 = jnp.bfloat16
HIGHEST = lax.Precision.HIGHEST
MESH = pl.DeviceIdType.MESH

D_MODEL = 1024
EPS = 1e-6
RET_HEADS, RET_DK, RET_DV, CHUNK = 4, 128, 256, 128
ROPE_BASE = 10000.0
SSM_HEADS, SSM_P, SSM_N, SSM_GROUPS = 16, 64, 128, 2
SSM_DINNER = SSM_HEADS * SSM_P
EVEN_IN, EVEN_IN_PAD = 5648, 5760
S5_GROUPS, S5_GROUP, S5_STATE = 32, 16, 64
S5_LANES = S5_GROUPS * S5_STATE
SCAN_SEG = 32
D_FF = 2816
ADAM_LR, ADAM_B1, ADAM_B2, ADAM_EPS, ADAM_WD, ADAM_STEP = 0.001, 0.9, 0.999, 1e-08, 0.01, 10

LANE = 128
VMEM_LIMIT = 56 * 1024 * 1024


def _params(sem=None, **kw):
    return pltpu.CompilerParams(dimension_semantics=sem, vmem_limit_bytes=VMEM_LIMIT, **kw)


def _tile(n, target, unit=LANE):
    if n <= target:
        return n
    t = (target // unit) * unit
    while t >= unit:
        if n % t == 0:
            return t
        t -= unit
    return n


def _silu(x):
    return x * jax.nn.sigmoid(x)


def _mm(a, b):
    return jnp.dot(a.astype(BF16), b.astype(BF16), preferred_element_type=F32)


def _mm_nt(a, b):
    return lax.dot_general(a.astype(BF16), b.astype(BF16), (((1,), (1,)), ((), ())), preferred_element_type=F32)


def _mm_tn(a, b):
    return lax.dot_general(a.astype(BF16), b.astype(BF16), (((0,), (0,)), ((), ())), preferred_element_type=F32)


def _dot_hi(a, b):
    return jnp.dot(a, b, precision=HIGHEST, preferred_element_type=F32)


def _dot_hi_tn(a, b):
    return lax.dot_general(a, b, (((0,), (0,)), ((), ())), precision=HIGHEST, preferred_element_type=F32)


def _bf16_parts(v):
    hi = v.astype(BF16)
    rest = v - hi.astype(F32)
    mid = rest.astype(BF16)
    return hi, mid, (rest - mid.astype(F32)).astype(BF16)


def _dot_parts(v, fixed, dims, v_first):
    fixed = fixed.astype(BF16)
    out = None
    for part in _bf16_parts(v):
        ops = (part, fixed) if v_first else (fixed, part)
        p = lax.dot_general(*ops, (dims, ((), ())), preferred_element_type=F32)
        out = p if out is None else out + p
    return out


@jax.custom_vjp
def _times_01(v, ones):
    return _dot_parts(v, ones, ((1,), (0,)), True)


_times_01.defvjp(lambda v, ones: (_times_01(v, ones), ones),
                 lambda ones, g: (_dot_parts(g, ones, ((1,), (1,)), True), jnp.zeros_like(ones)))


@jax.custom_vjp
def _01_times(ones, v):
    return _dot_parts(v, ones, ((1,), (0,)), False)


_01_times.defvjp(lambda ones, v: (_01_times(ones, v), ones),
                 lambda ones, g: (jnp.zeros_like(ones), _dot_parts(g, ones, ((0,), (0,)), False)))


MATMUL_VMEM = 44 * 1024 * 1024


def matmul(a, b, *, ta=False, tb=False, res=None, out_dtype=F32, name):
    m, k = (a.shape[1], a.shape[0]) if ta else a.shape
    n = b.shape[0] if tb else b.shape[1]
    assert (b.shape[1] if tb else b.shape[0]) == k, (a.shape, b.shape, ta, tb)
    tm = _tile(m, 1536)
    tn = _tile(n, 640)
    if tn < 384:
        tn = _tile(n, 1536)
    res_bytes = 0 if res is None else res.dtype.itemsize

    def vmem(tm, tn):
        return 2 * (tm * k * a.dtype.itemsize + tn * k * b.dtype.itemsize + tm * tn * (jnp.dtype(out_dtype).itemsize + res_bytes))

    while vmem(tm, tn) > MATMUL_VMEM and tm % (2 * LANE) == 0:
        tm //= 2
    assert vmem(tm, tn) <= MATMUL_VMEM, (name, tm, tn, k)
    a_spec = pl.BlockSpec((k, tm), lambda i, j: (0, i)) if ta else pl.BlockSpec((tm, k), lambda i, j: (i, 0))
    b_spec = pl.BlockSpec((tn, k), lambda i, j: (j, 0)) if tb else pl.BlockSpec((k, tn), lambda i, j: (0, j))
    o_spec = pl.BlockSpec((tm, tn), lambda i, j: (i, j))
    dims = (((0 if ta else 1,), (1 if tb else 0,)), ((), ()))
    has_res = res is not None

    def body(a_ref, b_ref, *rest):
        o_ref = rest[-1]
        out = lax.dot_general(a_ref[...].astype(BF16), b_ref[...].astype(BF16), dims, preferred_element_type=F32)
        if has_res:
            out = out + rest[0][...].astype(F32)
        o_ref[...] = out.astype(o_ref.dtype)

    ins = [a, b] + ([res] if has_res else [])
    specs = [a_spec, b_spec] + ([o_spec] if has_res else [])
    return pl.pallas_call(
        body, name=name, grid=(m // tm, n // tn), in_specs=specs, out_specs=o_spec,
        out_shape=jax.ShapeDtypeStruct((m, n), out_dtype), compiler_params=_params(("parallel", "parallel")),
    )(*ins)


class Cols(NamedTuple):
    arr: jax.Array
    w: int
    j: int


def _cols(a):
    return a if isinstance(a, Cols) else Cols(a, a.shape[1], 0)


def _row_spec(c, tl):
    return pl.BlockSpec((tl, c.w), lambda i, j=c.j: (i, j))


def _whole_spec(p):
    return pl.BlockSpec(p.shape, lambda i, nd=p.ndim: (0,) * nd)


def rowwise_fwd(fn, rows, aux, pars, consts, outs, *, name, tl):
    rows = [_cols(r) for r in rows + aux]
    whole = list(pars) + list(consts)
    n_rows = len(rows)
    n_whole = len(whole)
    length = rows[0].arr.shape[0]
    tl = min(tl, length)

    def body(*refs):
        vals = [r[...].astype(F32) for r in refs[:n_rows]] + [r[...] for r in refs[n_rows:n_rows + n_whole]]
        res = fn(*vals)
        for o_ref, v in zip(refs[n_rows + n_whole:], res, strict=True):
            o_ref[...] = v.astype(o_ref.dtype)

    return pl.pallas_call(
        body, name=name, grid=(length // tl,),
        in_specs=[_row_spec(r, tl) for r in rows] + [_whole_spec(p) for p in whole],
        out_specs=[pl.BlockSpec((tl, w), lambda i: (i, 0)) for w, _ in outs],
        out_shape=[jax.ShapeDtypeStruct((length, w), dt) for w, dt in outs],
        compiler_params=_params(("parallel",)),
    )(*[r.arr for r in rows], *whole)


def rowwise_bwd(fn, rows, aux, pars, consts, cots, drow_dtypes, *, name, tl, add=None, merge=False):
    rows = [_cols(r) for r in rows]
    aux = [_cols(r) for r in aux]
    cots = [_cols(r) for r in cots]
    n_r, n_a, n_p, n_c, n_t = len(rows), len(aux), len(pars), len(consts), len(cots)
    length = rows[0].arr.shape[0]
    tl = min(tl, length)
    has_add = add is not None
    widths = [r.w for r in rows]

    def body(*refs):
        pos = 0
        r_vals = [r[...].astype(F32) for r in refs[pos:pos + n_r]]; pos += n_r
        a_vals = [r[...].astype(F32) for r in refs[pos:pos + n_a]]; pos += n_a
        p_vals = [r[...].astype(F32) for r in refs[pos:pos + n_p]]; pos += n_p
        c_vals = [r[...] for r in refs[pos:pos + n_c]]; pos += n_c
        t_vals = [r[...].astype(F32) for r in refs[pos:pos + n_t]]; pos += n_t
        add_val = None
        if has_add:
            add_val = refs[pos][...].astype(F32); pos += 1
        n_dr = 1 if merge else n_r
        dr_refs = refs[pos:pos + n_dr]; pos += n_dr
        dp_refs = refs[pos:pos + n_p]

        def f(*rp):
            return fn(*rp[:n_r], *a_vals, *rp[n_r:], *c_vals)

        _, vjp = jax.vjp(f, *r_vals, *p_vals)
        grads = vjp(tuple(t_vals))
        drows = list(grads[:n_r])
        if has_add:
            drows[0] = drows[0] + add_val
        if merge:
            off = 0
            for w, d in zip(widths, drows):
                dr_refs[0][:, off:off + w] = d.astype(dr_refs[0].dtype)
                off += w
        else:
            for r, d in zip(dr_refs, drows):
                r[...] = d.astype(r.dtype)
        i = pl.program_id(0)
        for r, d in zip(dp_refs, grads[n_r:]):
            @pl.when(i == 0)
            def _(r=r, d=d):
                r[...] = d

            @pl.when(i > 0)
            def _(r=r, d=d):
                r[...] += d

    if merge:
        dr_specs = [pl.BlockSpec((tl, sum(widths)), lambda i: (i, 0))]
        dr_shapes = [jax.ShapeDtypeStruct((length, sum(widths)), drow_dtypes[0])]
    else:
        dr_specs = [pl.BlockSpec((tl, w), lambda i: (i, 0)) for w in widths]
        dr_shapes = [jax.ShapeDtypeStruct((length, w), dt) for w, dt in zip(widths, drow_dtypes)]
    ins = [r.arr for r in rows + aux] + list(pars) + list(consts) + [r.arr for r in cots] + ([add] if has_add else [])
    specs = ([_row_spec(r, tl) for r in rows + aux] + [_whole_spec(p) for p in list(pars) + list(consts)]
             + [_row_spec(r, tl) for r in cots] + ([pl.BlockSpec((tl, add.shape[1]), lambda i: (i, 0))] if has_add else []))
    return pl.pallas_call(
        body, name=name, grid=(length // tl,), in_specs=specs,
        out_specs=dr_specs + [_whole_spec(p) for p in pars],
        out_shape=dr_shapes + [jax.ShapeDtypeStruct(p.shape, F32) for p in pars],
        compiler_params=_params(("arbitrary",)),
    )(*ins)


def whole_fwd(fn, ins, out_shapes, *, name):
    n_in = len(ins)

    def body(*refs):
        res = fn(*[r[...] for r in refs[:n_in]])
        for o_ref, v in zip(refs[n_in:], res, strict=True):
            o_ref[...] = v

    return pl.pallas_call(body, name=name, out_shape=[jax.ShapeDtypeStruct(s, F32) for s in out_shapes],
                          compiler_params=_params())(*ins)


def whole_bwd(fn, ins, n_diff, cots, *, name):
    n_in, n_t = len(ins), len(cots)

    def body(*refs):
        vals = [r[...] for r in refs[:n_in]]
        t_vals = [r[...] for r in refs[n_in:n_in + n_t]]
        _, vjp = jax.vjp(lambda *d: fn(*d, *vals[n_diff:]), *vals[:n_diff])
        for o_ref, g in zip(refs[n_in + n_t:], vjp(tuple(t_vals)), strict=True):
            o_ref[...] = g

    return pl.pallas_call(body, name=name, out_shape=[jax.ShapeDtypeStruct(a.shape, F32) for a in ins[:n_diff]],
                          compiler_params=_params())(*ins, *cots)


CONV_ROWS = 256


def _conv_geometry(x, w, cw, off):
    width = w.shape[1]
    x = Cols(x, width, 0)
    length = x.arr.shape[0]
    taps = w.shape[0]
    pad = -(-(taps - 1) // 8) * 8
    assert off % cw == 0 and width % cw == 0, (off, width, cw)
    return x, length, taps, pad, off // cw


def _conv_taps(xp_ref, w_ref, base, taps, pad, init, lanes=slice(None)):
    acc = init
    for k in range(taps):
        acc = acc + w_ref[k:k + 1, lanes] * xp_ref[pl.ds(base + pad - (taps - 1) + k, init.shape[0]), :]
    return acc


def conv_fwd(x, w, b, *, act, name, off=0, cw=LANE, out_dtype=F32):
    x, length, taps, pad, jb = _conv_geometry(x, w, cw, off)
    rc = min(CONV_ROWS, length)

    def body(x_ref, w_ref, b_ref, o_ref, xp_ref):
        xp_ref[0:pad, :] = jnp.zeros((pad, cw), F32)
        xp_ref[pad:pad + length, :] = x_ref[...].astype(F32)

        def chunk(r, carry):
            base = pl.multiple_of(r * rc, rc)
            acc = _conv_taps(xp_ref, w_ref, base, taps, pad, jnp.broadcast_to(b_ref[...], (rc, cw)))
            if act:
                acc = _silu(acc)
            o_ref[pl.ds(base, rc), :] = acc.astype(o_ref.dtype)
            return carry

        lax.fori_loop(0, length // rc, chunk, 0)

    return pl.pallas_call(
        body, name=name, grid=(x.w // cw,),
        in_specs=[pl.BlockSpec((length, cw), lambda j: (0, jb + j)), pl.BlockSpec((taps, cw), lambda j: (0, j)),
                  pl.BlockSpec((1, cw), lambda j: (0, j))],
        out_specs=pl.BlockSpec((length, cw), lambda j: (0, j)),
        out_shape=jax.ShapeDtypeStruct((length, x.w), out_dtype),
        scratch_shapes=[pltpu.VMEM((pad + length, cw), F32)],
        compiler_params=_params(("parallel",)),
    )(x.arr, w, b)


def conv_bwd(x, w, b, dy, *, act, name, off=0, cw=LANE, dx_dtype=F32):
    x, length, taps, pad, jb = _conv_geometry(x, w, cw, off)
    rc = min(CONV_ROWS, length)

    def body(x_ref, w_ref, b_ref, dy_ref, dx_ref, dw_ref, db_ref, xp_ref, gp_ref):
        xp_ref[0:pad, :] = jnp.zeros((pad, cw), F32)
        xp_ref[pad:pad + length, :] = x_ref[...].astype(F32)
        gp_ref[length:length + pad, :] = jnp.zeros((pad, cw), F32)
        if act:
            def pre_chunk(r, carry):
                base = pl.multiple_of(r * rc, rc)
                pre = _conv_taps(xp_ref, w_ref, base, taps, pad, jnp.broadcast_to(b_ref[...], (rc, cw)))
                sig = jax.nn.sigmoid(pre)
                gp_ref[pl.ds(base, rc), :] = dy_ref[pl.ds(base, rc), :].astype(F32) * (sig * (1.0 + pre * (1.0 - sig)))
                return carry

            lax.fori_loop(0, length // rc, pre_chunk, 0)
        else:
            gp_ref[0:length, :] = dy_ref[...].astype(F32)
        dw_ref[...] = jnp.zeros((taps, cw), F32)
        db_ref[...] = jnp.zeros((1, cw), F32)

        def chunk(r, carry):
            base = pl.multiple_of(r * rc, rc)
            acc = jnp.zeros((rc, cw), F32)
            g = gp_ref[pl.ds(base, rc), :]
            for k in range(taps):
                acc = acc + w_ref[k:k + 1, :] * gp_ref[pl.ds(base + (taps - 1) - k, rc), :]
                xs = xp_ref[pl.ds(base + pad - (taps - 1) + k, rc), :]
                dw_ref[k:k + 1, :] += jnp.sum(g * xs, axis=0, keepdims=True)
            db_ref[...] += jnp.sum(g, axis=0, keepdims=True)
            dx_ref[pl.ds(base, rc), :] = acc.astype(dx_ref.dtype)
            return carry

        lax.fori_loop(0, length // rc, chunk, 0)

    dy = _cols(dy)
    assert dy.j == 0 and dy.w == x.w
    return pl.pallas_call(
        body, name=name, grid=(x.w // cw,),
        in_specs=[pl.BlockSpec((length, cw), lambda j: (0, jb + j)), pl.BlockSpec((taps, cw), lambda j: (0, j)),
                  pl.BlockSpec((1, cw), lambda j: (0, j)), pl.BlockSpec((length, cw), lambda j: (0, j))],
        out_specs=[pl.BlockSpec((length, cw), lambda j: (0, j)), pl.BlockSpec((taps, cw), lambda j: (0, j)),
                   pl.BlockSpec((1, cw), lambda j: (0, j))],
        out_shape=[jax.ShapeDtypeStruct((length, x.w), dx_dtype), jax.ShapeDtypeStruct((taps, x.w), F32),
                   jax.ShapeDtypeStruct((1, x.w), F32)],
        scratch_shapes=[pltpu.VMEM((pad + length, cw), F32), pltpu.VMEM((length + pad, cw), F32)],
        compiler_params=_params(("parallel",)),
    )(x.arr, w, b, dy.arr)


def _conv_transpose(xp_ref, gp_ref, w_ref, dx_ref, dw_ref, db_ref, lanes, length, taps, pad, rc):
    dw_ref[:, lanes] = jnp.zeros((taps, LANE), F32)
    db_ref[:, lanes] = jnp.zeros((1, LANE), F32)

    def chunk(r, carry):
        base = pl.multiple_of(r * rc, rc)
        acc = jnp.zeros((rc, LANE), F32)
        g = gp_ref[pl.ds(base, rc), :]
        for k in range(taps):
            acc = acc + w_ref[k:k + 1, lanes] * gp_ref[pl.ds(base + (taps - 1) - k, rc), :]
            xs = xp_ref[pl.ds(base + pad - (taps - 1) + k, rc), :]
            dw_ref[k:k + 1, lanes] += jnp.sum(g * xs, axis=0, keepdims=True)
        db_ref[:, lanes] += jnp.sum(g, axis=0, keepdims=True)
        dx_ref[pl.ds(base, rc), lanes] = acc.astype(dx_ref.dtype)
        return carry

    lax.fori_loop(0, length // rc, chunk, 0)


LANE_PAIR_ROWS = 1024


def ffn_interleave(a, name=None):
    rows, width = a.shape
    nb = width // (2 * LANE)
    if rows < LANE_PAIR_ROWS:
        return a.reshape(rows, 2, nb, LANE).swapaxes(1, 2).reshape(a.shape)

    def body(g_ref, u_ref, o_ref):
        o_ref[:, 0:LANE] = g_ref[...]
        o_ref[:, LANE:2 * LANE] = u_ref[...]

    tr = LANE_PAIR_ROWS
    return pl.pallas_call(
        body, name=name, grid=(rows // tr, nb),
        in_specs=[pl.BlockSpec((tr, LANE), lambda i, j: (i, j)), pl.BlockSpec((tr, LANE), lambda i, j: (i, nb + j))],
        out_specs=pl.BlockSpec((tr, 2 * LANE), lambda i, j: (i, j)),
        out_shape=jax.ShapeDtypeStruct(a.shape, a.dtype), compiler_params=_params(("parallel", "parallel")),
    )(a, a)


def ffn_deinterleave(a):
    rows, width = a.shape
    return a.reshape(rows, width // (2 * LANE), 2, LANE).swapaxes(1, 2).reshape(a.shape)


def _pair_block(s, t, nb, per):
    block = s * per + t
    return jnp.where(block < nb, 2 * block, 2 * (block - nb) + 1)


def ffn_pairs_to_shards(a, *, name):
    rows, width = a.shape
    nb = width // (2 * LANE)
    per = width // 4 // LANE

    def body(a_ref, o_ref):
        o_ref[0] = a_ref[...]

    return pl.pallas_call(
        body, name=name, grid=(4, per),
        in_specs=[pl.BlockSpec((rows, LANE), lambda s, t: (0, _pair_block(s, t, nb, per)))],
        out_specs=pl.BlockSpec((1, rows, LANE), lambda s, t: (s, 0, t)),
        out_shape=jax.ShapeDtypeStruct((4, rows, width // 4), a.dtype), compiler_params=_params(("parallel", "parallel")),
    )(a)


def ffn_shards_to_pairs(parts, *, name):
    _, rows, cols = parts.shape
    per = cols // LANE
    nb = 2 * per

    def body(p_ref, o_ref):
        o_ref[...] = p_ref[0]

    return pl.pallas_call(
        body, name=name, grid=(4, per), in_specs=[pl.BlockSpec((1, rows, LANE), lambda s, t: (s, 0, t))],
        out_specs=pl.BlockSpec((rows, LANE), lambda s, t: (0, _pair_block(s, t, nb, per))),
        out_shape=jax.ShapeDtypeStruct((rows, 4 * cols), parts.dtype), compiler_params=_params(("parallel", "parallel")),
    )(parts)


GATE, UP = slice(0, LANE), slice(LANE, 2 * LANE)


def _ffn_geometry(a, w):
    length, width = a.shape
    taps = w.shape[0]
    return length, width, width // (2 * LANE), taps, -(-(taps - 1) // 8) * 8, min(CONV_ROWS, length)


def _ffn_pre(xg_ref, xu_ref, w_ref, b_ref, base, taps, pad, rc):
    gate = _conv_taps(xg_ref, w_ref, base, taps, pad, jnp.broadcast_to(b_ref[:, GATE], (rc, LANE)), GATE)
    up = _conv_taps(xu_ref, w_ref, base, taps, pad, jnp.broadcast_to(b_ref[:, UP], (rc, LANE)), UP)
    return gate, up


def ffn_conv_act(a, w, b, *, name):
    length, width, nb, taps, pad, rc = _ffn_geometry(a, w)

    def body(a_ref, w_ref, b_ref, o_ref, xg_ref, xu_ref):
        for xp_ref, lanes in ((xg_ref, GATE), (xu_ref, UP)):
            xp_ref[0:pad, :] = jnp.zeros((pad, LANE), F32)
            xp_ref[pad:pad + length, :] = a_ref[:, lanes].astype(F32)

        def chunk(r, carry):
            base = pl.multiple_of(r * rc, rc)
            gate, up = _ffn_pre(xg_ref, xu_ref, w_ref, b_ref, base, taps, pad, rc)
            o_ref[pl.ds(base, rc), :] = (_silu(gate) * up).astype(o_ref.dtype)
            return carry

        lax.fori_loop(0, length // rc, chunk, 0)

    pair = lambda rows: pl.BlockSpec((rows, 2 * LANE), lambda j: (0, j))
    return pl.pallas_call(
        body, name=name, grid=(nb,), in_specs=[pair(length), pair(taps), pair(1)],
        out_specs=pl.BlockSpec((length, LANE), lambda j: (0, j)),
        out_shape=jax.ShapeDtypeStruct((length, width // 2), BF16),
        scratch_shapes=[pltpu.VMEM((pad + length, LANE), F32), pltpu.VMEM((pad + length, LANE), F32)],
        compiler_params=_params(("parallel",)),
    )(a, w, b)


def ffn_conv_act_bwd(a, w, b, dact, *, name):
    length, width, nb, taps, pad, rc = _ffn_geometry(a, w)

    def body(a_ref, w_ref, b_ref, dy_ref, da_ref, dw_ref, db_ref, xg_ref, xu_ref, gg_ref, gu_ref):
        for xp_ref, lanes in ((xg_ref, GATE), (xu_ref, UP)):
            xp_ref[0:pad, :] = jnp.zeros((pad, LANE), F32)
            xp_ref[pad:pad + length, :] = a_ref[:, lanes].astype(F32)
        for gp_ref in (gg_ref, gu_ref):
            gp_ref[length:length + pad, :] = jnp.zeros((pad, LANE), F32)

        def pre_chunk(r, carry):
            base = pl.multiple_of(r * rc, rc)
            gate, up = _ffn_pre(xg_ref, xu_ref, w_ref, b_ref, base, taps, pad, rc)
            sig = jax.nn.sigmoid(gate)
            dy = dy_ref[pl.ds(base, rc), :]
            gg_ref[pl.ds(base, rc), :] = dy * up * (sig * (1.0 + gate * (1.0 - sig)))
            gu_ref[pl.ds(base, rc), :] = dy * (gate * sig)
            return carry

        lax.fori_loop(0, length // rc, pre_chunk, 0)
        _conv_transpose(xg_ref, gg_ref, w_ref, da_ref, dw_ref, db_ref, GATE, length, taps, pad, rc)
        _conv_transpose(xu_ref, gu_ref, w_ref, da_ref, dw_ref, db_ref, UP, length, taps, pad, rc)

    pair = lambda rows: pl.BlockSpec((rows, 2 * LANE), lambda j: (0, j))
    return pl.pallas_call(
        body, name=name, grid=(nb,),
        in_specs=[pair(length), pair(taps), pair(1), pl.BlockSpec((length, LANE), lambda j: (0, j))],
        out_specs=[pair(length), pair(taps), pair(1)],
        out_shape=[jax.ShapeDtypeStruct((length, width), BF16), jax.ShapeDtypeStruct((taps, width), F32),
                   jax.ShapeDtypeStruct((1, width), F32)],
        scratch_shapes=[pltpu.VMEM((pad + length, LANE), F32), pltpu.VMEM((pad + length, LANE), F32),
                        pltpu.VMEM((length + pad, LANE), F32), pltpu.VMEM((length + pad, LANE), F32)],
        compiler_params=_params(("parallel",)),
    )(a, w, b, dact)


def _retention_consts():
    h = np.arange(RET_HEADS, dtype=np.float32)
    log_g = np.log1p(-(2.0 ** (-5.0 - h))).astype(np.float32)
    idx = np.arange(CHUNK, dtype=np.float32)
    diff = idx[:, None] - idx[None, :]
    intra = np.where(diff[None] >= 0, np.exp(np.maximum(diff, 0.0)[None] * log_g[:, None, None]), 0.0)
    zeta = np.exp((CHUNK - 1 - idx)[None, :] * log_g[:, None])
    xi = np.exp((idx + 1)[None, :] * log_g[:, None])
    decay = np.exp(CHUNK * log_g)
    zeta = np.broadcast_to(zeta[:, :, None], (RET_HEADS, CHUNK, RET_DK))
    xi = np.broadcast_to(xi[:, :, None], (RET_HEADS, CHUNK, RET_DV))
    return (jnp.asarray(intra, F32), jnp.asarray(zeta, F32), jnp.asarray(xi, F32), [float(d) for d in decay])


def _rotary_tables(length):
    inv = ROPE_BASE ** (-jnp.arange(0, RET_DK, 2, dtype=F32) / RET_DK)
    ang = jnp.arange(length).astype(F32)[:, None] * inv[None, :]
    cos, sin = jnp.cos(ang), jnp.sin(ang)
    return jnp.concatenate([cos, cos], axis=1), jnp.concatenate([-sin, sin], axis=1)


def _rot(x, cos2, sin2):
    return x * cos2 + pltpu.roll(x, RET_DK // 2, 1) * sin2


def _rot_t(y, cos2, sin2):
    return y * cos2 + pltpu.roll(y * sin2, RET_DK // 2, 1)


def _ret_chunk(q, k, v, g, state, intra, zeta, xi, decay):
    s = _mm_nt(q, k) * intra
    kv = _mm_tn(k * zeta, v)
    o = _mm(s, v) + _mm(q, state) * xi
    oc = o - jnp.mean(o, axis=-1, keepdims=True)
    r = oc * lax.rsqrt(jnp.mean(oc * oc, axis=-1, keepdims=True) + EPS)
    return _silu(g) * r, state * decay + kv


RET_QK, RET_V = RET_HEADS * RET_DK, RET_HEADS * RET_DV


def _ret_specs(rev, nc):
    def cidx(c):
        return nc - 1 - c if rev else c
    whole = lambda shape: pl.BlockSpec(shape, lambda c: (0,) * len(shape))
    return [
        pl.BlockSpec((CHUNK, RET_QK), lambda c: (cidx(c), 0)),
        pl.BlockSpec((CHUNK, RET_QK), lambda c: (cidx(c), 1)),
        pl.BlockSpec((CHUNK, RET_V), lambda c: (cidx(c), 1)),
        pl.BlockSpec((CHUNK, RET_V), lambda c: (cidx(c), 2)),
        pl.BlockSpec((CHUNK, RET_DK), lambda c: (cidx(c), 0)),
        pl.BlockSpec((CHUNK, RET_DK), lambda c: (cidx(c), 0)),
        whole((RET_HEADS, CHUNK, CHUNK)), whole((RET_HEADS, CHUNK, RET_DK)), whole((RET_HEADS, CHUNK, RET_DV)),
    ], cidx


def _head(ref, h, width):
    return ref[:, h * width:(h + 1) * width].astype(F32)


def retention_fwd(proj, cos2, sin2):
    length = proj.shape[0]
    nc = length // CHUNK
    intra, zeta, xi, decays = _retention_consts()
    specs, _ = _ret_specs(False, nc)
    scale = RET_DK ** -0.5

    def body(q_ref, k_ref, v_ref, g_ref, cos_ref, sin_ref, intra_ref, zeta_ref, xi_ref, y_ref, st_ref, state):
        @pl.when(pl.program_id(0) == 0)
        def _():
            state[...] = jnp.zeros_like(state)

        cos2v, sin2v = cos_ref[...], sin_ref[...]
        for h in range(RET_HEADS):
            q = _rot(_head(q_ref, h, RET_DK), cos2v, sin2v)
            k = _rot(_head(k_ref, h, RET_DK), cos2v, sin2v) * scale
            st_ref[h, 0] = state[h]
            y, new_state = _ret_chunk(q, k, _head(v_ref, h, RET_DV), _head(g_ref, h, RET_DV), state[h],
                                      intra_ref[h], zeta_ref[h], xi_ref[h], decays[h])
            y_ref[:, h * RET_DV:(h + 1) * RET_DV] = y.astype(y_ref.dtype)
            state[h] = new_state

    return pl.pallas_call(
        body, name="retention_fwd", grid=(nc,), in_specs=specs,
        out_specs=[pl.BlockSpec((CHUNK, RET_V), lambda c: (c, 0)),
                   pl.BlockSpec((RET_HEADS, 1, RET_DK, RET_DV), lambda c: (0, c, 0, 0))],
        out_shape=[jax.ShapeDtypeStruct((length, RET_V), BF16),
                   jax.ShapeDtypeStruct((RET_HEADS, nc, RET_DK, RET_DV), F32)],
        scratch_shapes=[pltpu.VMEM((RET_HEADS, RET_DK, RET_DV), F32)],
        compiler_params=_params(("arbitrary",)),
    )(proj, proj, proj, proj, cos2, sin2, intra, zeta, xi)


def retention_bwd(proj, cos2, sin2, states, dmix):
    length = proj.shape[0]
    nc = length // CHUNK
    intra, zeta, xi, decays = _retention_consts()
    specs, cidx = _ret_specs(True, nc)
    scale = RET_DK ** -0.5

    def body(q_ref, k_ref, v_ref, g_ref, cos_ref, sin_ref, intra_ref, zeta_ref, xi_ref, st_ref, dy_ref,
             dq_ref, dk_ref, dv_ref, dg_ref, dstate):
        @pl.when(pl.program_id(0) == 0)
        def _():
            dstate[...] = jnp.zeros_like(dstate)

        cos2v, sin2v = cos_ref[...], sin_ref[...]
        for h in range(RET_HEADS):
            q = _rot(_head(q_ref, h, RET_DK), cos2v, sin2v)
            k = _rot(_head(k_ref, h, RET_DK), cos2v, sin2v) * scale
            intra_v, zeta_v, xi_v, decay = intra_ref[h], zeta_ref[h], xi_ref[h], decays[h]
            _, vjp = jax.vjp(lambda q, k, v, g, s: _ret_chunk(q, k, v, g, s, intra_v, zeta_v, xi_v, decay),
                             q, k, _head(v_ref, h, RET_DV), _head(g_ref, h, RET_DV), st_ref[h, 0])
            dq, dk, dv, dg, ds = vjp((_head(dy_ref, h, RET_DV).astype(F32), dstate[h]))
            dq_ref[:, h * RET_DK:(h + 1) * RET_DK] = _rot_t(dq, cos2v, sin2v).astype(dq_ref.dtype)
            dk_ref[:, h * RET_DK:(h + 1) * RET_DK] = _rot_t(dk * scale, cos2v, sin2v).astype(dk_ref.dtype)
            dv_ref[:, h * RET_DV:(h + 1) * RET_DV] = dv.astype(dv_ref.dtype)
            dg_ref[:, h * RET_DV:(h + 1) * RET_DV] = dg.astype(dg_ref.dtype)
            dstate[h] = ds

    specs = specs + [pl.BlockSpec((RET_HEADS, 1, RET_DK, RET_DV), lambda c: (0, cidx(c), 0, 0)),
                     pl.BlockSpec((CHUNK, RET_V), lambda c: (cidx(c), 0))]
    row = lambda width: pl.BlockSpec((CHUNK, width), lambda c: (cidx(c), 0))
    return pl.pallas_call(
        body, name="retention_bwd", grid=(nc,), in_specs=specs,
        out_specs=[row(RET_QK), row(RET_QK), row(RET_V), row(RET_V)],
        out_shape=[jax.ShapeDtypeStruct((length, RET_QK), BF16), jax.ShapeDtypeStruct((length, RET_QK), BF16),
                   jax.ShapeDtypeStruct((length, RET_V), BF16), jax.ShapeDtypeStruct((length, RET_V), BF16)],
        scratch_shapes=[pltpu.VMEM((RET_HEADS, RET_DK, RET_DV), F32)],
        compiler_params=_params(("arbitrary",)),
    )(proj, proj, proj, proj, cos2, sin2, intra, zeta, xi, states, dmix)


def _ssd_consts():
    tri = np.tril(np.ones((CHUNK, CHUNK), np.float32))
    expand = np.zeros((LANE, SSM_DINNER), np.float32)
    for h in range(SSM_HEADS):
        expand[h, h * SSM_P:(h + 1) * SSM_P] = 1.0
    return jnp.asarray(tri), jnp.asarray(tri.T.copy()), jnp.asarray(expand)


def _ssd_chunk(xs, bm, cm, dtr, z, state, dt_bias, a_log, d_skip, norm_w, tri, tri_t, expand):
    gw = SSM_DINNER // SSM_GROUPS
    dt = jax.nn.softplus(dtr + dt_bias)
    a_neg = -jnp.exp(a_log)
    da = dt * a_neg
    acs = _dot_hi(tri, da)
    acs_t = _dot_hi_tn(da, tri_t)
    dt_x = _times_01(dt, expand)
    a_x = jnp.mean(_dot_hi(jnp.broadcast_to(a_neg, (8, LANE)), expand), axis=0, keepdims=True)
    da_x = dt_x * a_x
    acs_x = _01_times(tri, da_x)
    tot_x = jnp.sum(da_x, axis=0, keepdims=True)
    x_dt = xs * dt_x
    x_dec = x_dt * jnp.exp(tot_x - acs_x)
    e_acs = jnp.exp(acs_x)
    e_tot = jnp.exp(tot_x)
    lane = lax.broadcasted_iota(jnp.int32, (CHUNK, LANE), 1)
    sub = lax.broadcasted_iota(jnp.int32, (CHUNK, LANE), 0)
    causal = sub >= lane
    ys, new_states = [], []
    for g in range(SSM_GROUPS):
        bg = bm[:, g * SSM_N:(g + 1) * SSM_N]
        cg = cm[:, g * SSM_N:(g + 1) * SSM_N]
        sg = state[:, g * gw:(g + 1) * gw]
        cb = _mm_nt(cg, bg)
        y_off = _mm(cg, sg) * e_acs[:, g * gw:(g + 1) * gw]
        new_states.append(sg * e_tot[:, g * gw:(g + 1) * gw] + _mm_tn(bg, x_dec[:, g * gw:(g + 1) * gw]))
        pairs = []
        for p in range(gw // LANE):
            hp = g * (gw // LANE) + p
            xp = x_dt[:, hp * LANE:(hp + 1) * LANE]
            halves = []
            for head in (2 * hp, 2 * hp + 1):
                col = jnp.sum(jnp.where(lane == head, acs, 0.0), axis=1, keepdims=True)
                row = jnp.sum(jnp.where(sub == head, acs_t, 0.0), axis=0, keepdims=True)
                decay = jnp.exp(jnp.where(causal, col - row, -1e30))
                halves.append(_mm(cb * decay, xp))
            pairs.append(jnp.where(lane < SSM_P, halves[0], halves[1]))
        ys.append(jnp.concatenate(pairs, axis=1) + y_off)
    d_x = jnp.mean(_dot_hi(jnp.broadcast_to(d_skip, (8, LANE)), expand), axis=0, keepdims=True)
    y = (jnp.concatenate(ys, axis=1) + d_x * xs) * _silu(z)
    normed = []
    for g in range(SSM_GROUPS):
        yg = y[:, g * gw:(g + 1) * gw]
        normed.append(yg * lax.rsqrt(jnp.mean(yg * yg, axis=-1, keepdims=True) + EPS))
    return jnp.concatenate(normed, axis=1) * norm_w, jnp.concatenate(new_states, axis=1)


XBC = SSM_DINNER + 2 * SSM_GROUPS * SSM_N


def _ssd_specs(rev, nc):
    def cidx(c):
        return nc - 1 - c if rev else c
    row = lambda w, j: pl.BlockSpec((CHUNK, w), lambda c: (cidx(c), j))
    whole = lambda shape: pl.BlockSpec(shape, lambda c: (0,) * len(shape))
    return [row(XBC, 0), row(LANE, 0), row(SSM_DINNER, 3),
            whole((1, LANE)), whole((1, LANE)), whole((1, LANE)), whole((1, SSM_DINNER)),
            whole((CHUNK, CHUNK)), whole((CHUNK, CHUNK)), whole((LANE, SSM_DINNER))], cidx


def ssd_fwd(xbc, dt_raw, proj, dt_bias, a_log, d_skip, norm_w):
    length = proj.shape[0]
    nc = length // CHUNK
    tri, tri_t, expand = _ssd_consts()
    specs, _ = _ssd_specs(False, nc)

    def body(xbc_ref, dt_ref, z_ref, dtb_ref, alog_ref, d_ref, nw_ref, tri_ref, trit_ref, e_ref, y_ref, st_ref, state):
        @pl.when(pl.program_id(0) == 0)
        def _():
            state[...] = jnp.zeros_like(state)

        st_ref[0] = state[...]
        y, new_state = _ssd_chunk(
            xbc_ref[:, 0:SSM_DINNER], xbc_ref[:, SSM_DINNER:SSM_DINNER + 256], xbc_ref[:, SSM_DINNER + 256:XBC],
            dt_ref[...], z_ref[...].astype(F32), state[...], dtb_ref[...], alog_ref[...], d_ref[...], nw_ref[...],
            tri_ref[...], trit_ref[...], e_ref[...])
        y_ref[...] = y.astype(y_ref.dtype)
        state[...] = new_state

    return pl.pallas_call(
        body, name="ssd_fwd", grid=(nc,), in_specs=specs,
        out_specs=[pl.BlockSpec((CHUNK, SSM_DINNER), lambda c: (c, 0)),
                   pl.BlockSpec((1, SSM_N, SSM_DINNER), lambda c: (c, 0, 0))],
        out_shape=[jax.ShapeDtypeStruct((length, SSM_DINNER), BF16),
                   jax.ShapeDtypeStruct((nc, SSM_N, SSM_DINNER), F32)],
        scratch_shapes=[pltpu.VMEM((SSM_N, SSM_DINNER), F32)],
        compiler_params=_params(("arbitrary",)),
    )(xbc, dt_raw, proj, dt_bias, a_log, d_skip, norm_w, tri, tri_t, expand)


def ssd_bwd(xbc, dt_raw, proj, dt_bias, a_log, d_skip, norm_w, states, dmix):
    length = proj.shape[0]
    nc = length // CHUNK
    tri, tri_t, expand = _ssd_consts()
    specs, cidx = _ssd_specs(True, nc)

    def body(xbc_ref, dt_ref, z_ref, dtb_ref, alog_ref, d_ref, nw_ref, tri_ref, trit_ref, e_ref, st_ref, dy_ref,
             dxbc_ref, ddt_ref, dz_ref, ddtb_ref, dalog_ref, dd_ref, dnw_ref, dstate):
        c = pl.program_id(0)

        @pl.when(c == 0)
        def _():
            dstate[...] = jnp.zeros_like(dstate)

        tri_v, trit_v, e_v = tri_ref[...], trit_ref[...], e_ref[...]
        _, vjp = jax.vjp(
            lambda *a: _ssd_chunk(*a, tri_v, trit_v, e_v),
            xbc_ref[:, 0:SSM_DINNER], xbc_ref[:, SSM_DINNER:SSM_DINNER + 256], xbc_ref[:, SSM_DINNER + 256:XBC],
            dt_ref[...], z_ref[...].astype(F32), st_ref[0], dtb_ref[...], alog_ref[...], d_ref[...], nw_ref[...])
        dxs, dbm, dcm, ddt, dz, ds, ddtb, dalog, dd, dnw = vjp((dy_ref[...].astype(F32), dstate[...]))
        dxbc_ref[:, 0:SSM_DINNER] = dxs
        dxbc_ref[:, SSM_DINNER:SSM_DINNER + 256] = dbm
        dxbc_ref[:, SSM_DINNER + 256:XBC] = dcm
        ddt_ref[...] = ddt.astype(ddt_ref.dtype)
        dz_ref[...] = dz.astype(dz_ref.dtype)
        dstate[...] = ds
        for r, d in ((ddtb_ref, ddtb), (dalog_ref, dalog), (dd_ref, dd), (dnw_ref, dnw)):
            @pl.when(c == 0)
            def _(r=r, d=d):
                r[...] = d

            @pl.when(c > 0)
            def _(r=r, d=d):
                r[...] += d

    whole = lambda shape: pl.BlockSpec(shape, lambda c: (0,) * len(shape))
    specs = specs + [pl.BlockSpec((1, SSM_N, SSM_DINNER), lambda c: (cidx(c), 0, 0)),
                     pl.BlockSpec((CHUNK, SSM_DINNER), lambda c: (cidx(c), 1))]
    return pl.pallas_call(
        body, name="ssd_bwd", grid=(nc,), in_specs=specs,
        out_specs=[pl.BlockSpec((CHUNK, XBC), lambda c: (cidx(c), 0)), pl.BlockSpec((CHUNK, LANE), lambda c: (cidx(c), 0)),
                   pl.BlockSpec((CHUNK, SSM_DINNER), lambda c: (cidx(c), 0)),
                   whole((1, LANE)), whole((1, LANE)), whole((1, LANE)), whole((1, SSM_DINNER))],
        out_shape=[jax.ShapeDtypeStruct((length, XBC), F32), jax.ShapeDtypeStruct((length, LANE), BF16),
                   jax.ShapeDtypeStruct((length, SSM_DINNER), BF16),
                   jax.ShapeDtypeStruct((1, LANE), F32), jax.ShapeDtypeStruct((1, LANE), F32),
                   jax.ShapeDtypeStruct((1, LANE), F32), jax.ShapeDtypeStruct((1, SSM_DINNER), F32)],
        scratch_shapes=[pltpu.VMEM((SSM_N, SSM_DINNER), F32)],
        compiler_params=_params(("arbitrary",)),
    )(xbc, dt_raw, proj, dt_bias, a_log, d_skip, norm_w, tri, tri_t, expand, states, dmix)


def _cmul(ar, ai, br, bi):
    return ar * br - ai * bi, ar * bi + ai * br


def s5_scan(b_re, b_im, a_re, a_im, *, reverse=False, states=None, name, lw=256):
    length, lanes = b_re.shape
    nk = length // SCAN_SEG
    with_da = states is not None
    assert reverse or not with_da

    def shift(v):
        sub = lax.broadcasted_iota(jnp.int32, v.shape, 0)
        if reverse:
            return jnp.where(sub == SCAN_SEG - 1, 0.0, pltpu.roll(v, SCAN_SEG - 1, 0))
        return jnp.where(sub == 0, 0.0, pltpu.roll(v, 1, 0))

    def body(*refs):
        if with_da:
            bre_ref, bim_ref, are_ref, aim_ref, sre_ref, sim_ref, xre_ref, xim_ref, dare_ref, daim_ref = refs
        else:
            bre_ref, bim_ref, are_ref, aim_ref, xre_ref, xim_ref = refs
        ar = jnp.broadcast_to(are_ref[...], (SCAN_SEG, lw))
        ai = jnp.broadcast_to(aim_ref[...], (SCAN_SEG, lw))

        def tile(i):
            k = (nk - 1 - i) if reverse else i
            return pl.ds(pl.multiple_of(k * SCAN_SEG, SCAN_SEG), SCAN_SEG)

        def local(i, carry):
            xr, xi, pr, pi = carry
            rows = tile(i)
            mr, mi = _cmul(ar, ai, xr, xi)
            xr, xi = mr + bre_ref[rows, :], mi + bim_ref[rows, :]
            xre_ref[rows, :] = xr
            xim_ref[rows, :] = xi
            pr, pi = _cmul(ar, ai, pr, pi)
            return xr, xi, pr, pi

        zero = jnp.zeros((SCAN_SEG, lw), F32)
        one = jnp.ones((SCAN_SEG, lw), F32)
        er, ei, pr, pi = lax.fori_loop(0, nk, local, (zero, zero, one, zero))
        cr, ci = zero, zero
        for _ in range(SCAN_SEG - 1):
            mr, mi = _cmul(pr, pi, cr, ci)
            cr, ci = shift(er + mr), shift(ei + mi)

        def fix(i, carry):
            pr, pi, dr, di = carry
            rows = tile(i)
            pr, pi = _cmul(ar, ai, pr, pi)
            mr, mi = _cmul(pr, pi, cr, ci)
            xr, xi = xre_ref[rows, :] + mr, xim_ref[rows, :] + mi
            xre_ref[rows, :] = xr
            xim_ref[rows, :] = xi
            if with_da:
                k = nk - 1 - i
                prev = pl.ds(pl.multiple_of(jnp.maximum(k - 1, 0) * SCAN_SEG, SCAN_SEG), SCAN_SEG)
                last = pl.ds((nk - 1) * SCAN_SEG, SCAN_SEG)
                sub = lax.broadcasted_iota(jnp.int32, (SCAN_SEG, lw), 0)
                wr = jnp.where(sub == 0, 0.0, pltpu.roll(sre_ref[last, :], 1, 0))
                wi = jnp.where(sub == 0, 0.0, pltpu.roll(sim_ref[last, :], 1, 0))
                sr = jnp.where(k == 0, wr, sre_ref[prev, :])
                si = jnp.where(k == 0, wi, sim_ref[prev, :])
                dr, di = dr + xr * sr + xi * si, di + xi * sr - xr * si
            return pr, pi, dr, di

        _, _, dr, di = lax.fori_loop(0, nk, fix, (one, zero, zero, zero))
        if with_da:
            dare_ref[...] = jnp.sum(dr, axis=0, keepdims=True)
            daim_ref[...] = jnp.sum(di, axis=0, keepdims=True)

    col = pl.BlockSpec((length, lw), lambda j: (0, j))
    vec = pl.BlockSpec((1, lw), lambda j: (0, j))
    ins = [b_re, b_im, a_re, a_im] + (list(states) if with_da else [])
    in_specs = [col, col, vec, vec] + ([col, col] if with_da else [])
    out_specs = [col, col] + ([vec, vec] if with_da else [])
    out_shape = [jax.ShapeDtypeStruct((length, lanes), F32)] * 2 + ([jax.ShapeDtypeStruct((1, lanes), F32)] * 2 if with_da else [])
    return pl.pallas_call(
        body, name=name, grid=(lanes // lw,), in_specs=in_specs, out_specs=out_specs, out_shape=out_shape,
        compiler_params=_params(("parallel",)),
    )(*ins)


def _seg_interleave(v):
    length = v.shape[0]
    return v.reshape(SCAN_SEG, length // SCAN_SEG, -1).transpose(1, 0, 2).reshape(length, -1)


def _seg_deinterleave(v):
    length = v.shape[0]
    return v.reshape(length // SCAN_SEG, SCAN_SEG, -1).transpose(1, 0, 2).reshape(length, -1)


def _block_diag(m):
    eye = jnp.eye(S5_GROUPS, dtype=m.dtype)
    return (m.reshape(S5_GROUPS, S5_GROUP, 1, S5_STATE) * eye[:, None, :, None]).reshape(S5_GROUPS * S5_GROUP, S5_LANES)


def _block_diag_take(full):
    idx = jnp.arange(S5_GROUPS)
    blocks = full.reshape(S5_GROUPS, S5_GROUP, S5_GROUPS, S5_STATE)[idx, :, idx, :]
    return blocks.reshape(S5_GROUPS * S5_GROUP, S5_STATE)


def _s5_prep(a_re, a_im, log_step, b_re, b_im, rep):
    step = jnp.exp(log_step)
    mag = jnp.exp(a_re * step)
    ab_re = mag * jnp.cos(a_im * step)
    ab_im = mag * jnp.sin(a_im * step)
    den = a_re * a_re + a_im * a_im
    f_re = ((ab_re - 1.0) * a_re + ab_im * a_im) / den
    f_im = (ab_im * a_re - (ab_re - 1.0) * a_im) / den
    fr, fi = _dot_hi(rep, f_re), _dot_hi(rep, f_im)
    return ab_re, ab_im, fr * b_re - fi * b_im, fr * b_im + fi * b_re


def _rms(x, g):
    return (x * lax.rsqrt(jnp.mean(x * x, axis=-1, keepdims=True) + EPS) * g,)


def _ffn_act(gate, up):
    return (_silu(gate) * up,)


def _glu(a, g):
    return (a * jax.nn.sigmoid(g),)


def _ln_silu(x, g, b):
    xc = x - jnp.mean(x, axis=-1, keepdims=True)
    var = jnp.mean(xc * xc, axis=-1, keepdims=True)
    return (_silu(xc * lax.rsqrt(var + EPS) * g + b),)


def _s5_post(y, u, d_skip, glu_w):
    s = jax.nn.gelu(y + d_skip * u)
    return (s * jax.nn.sigmoid(_mm(s, glu_w)),)


def loss_head(x, tgt, g, *, tl=512):
    length, d = x.shape
    tl = min(tl, length)

    def body(x_ref, t_ref, g_ref, loss_ref, dx_ref, dg_ref):
        i = pl.program_id(0)
        y, vjp = jax.vjp(lambda x, g: _rms(x, g)[0], x_ref[...], g_ref[...])
        err = y - t_ref[...]
        dx, dg = vjp(err * (1.0 / d))
        dx_ref[...] = dx
        part = jnp.broadcast_to(0.5 * jnp.sum(jnp.mean(err * err, axis=-1, keepdims=True), axis=0, keepdims=True), (1, LANE))

        @pl.when(i == 0)
        def _():
            loss_ref[...] = part
            dg_ref[...] = dg

        @pl.when(i > 0)
        def _():
            loss_ref[...] += part
            dg_ref[...] += dg

    row = pl.BlockSpec((tl, d), lambda i: (i, 0))
    return pl.pallas_call(
        body, name="loss_head", grid=(length // tl,),
        in_specs=[row, row, pl.BlockSpec((1, d), lambda i: (0, 0))],
        out_specs=[pl.BlockSpec((1, LANE), lambda i: (0, 0)), row, pl.BlockSpec((1, d), lambda i: (0, 0))],
        out_shape=[jax.ShapeDtypeStruct((1, LANE), F32), jax.ShapeDtypeStruct((length, d), F32),
                   jax.ShapeDtypeStruct((1, d), F32)],
        compiler_params=_params(("arbitrary",)),
    )(x, tgt, g)


def _pad_heads(v):
    return jnp.pad(v, ((0, 0), (0, LANE - v.shape[1])))


def local_step(x, tgt, w, late_weights=None, early_reduce=None):
    length = x.shape[0]
    cos2, sin2 = _rotary_tables(length)
    grads = {}
    w = dict(w)

    def rms_fwd(xin, g, name):
        return rowwise_fwd(_rms, [xin], [], [g], [], [(D_MODEL, BF16)], name=name, tl=512)[0]

    def rms_bwd(xin, g, dh, dxo, name):
        return rowwise_bwd(_rms, [xin], [], [g], [], [dh], [F32], name=name, tl=512, add=dxo)

    def ffn_fwd(i, xin):
        hf = rms_fwd(xin, w["ffn_norm"][i:i + 1], f"ffn{i}_norm")
        w_up = w["ffn_w_up_pairs"][i] if "ffn_w_up_pairs" in w else ffn_interleave(w["ffn_w_up"][i], name=f"ffn{i}_up_pairs")
        a = matmul(hf, w_up, out_dtype=BF16, name=f"ffn{i}_up")
        act = ffn_conv_act(a, ffn_interleave(w["ffn_dw_w"][i]), ffn_interleave(w["ffn_dw_b"][i:i + 1]),
                           name=f"ffn{i}_conv_act")
        return matmul(act, w["ffn_w_down"][i], res=xin, name=f"ffn{i}_down"), (hf, a, act, w_up)

    def ffn_bwd(i, xin, saved, dxo):
        hf, a, act, w_up = saved
        dact = matmul(dxo, w["ffn_w_down"][i], tb=True, name=f"ffn{i}_down_dx")
        dw_down = matmul(act, dxo, ta=True, name=f"ffn{i}_down_dw")
        da, ddw_w, ddw_b = ffn_conv_act_bwd(a, ffn_interleave(w["ffn_dw_w"][i]), ffn_interleave(w["ffn_dw_b"][i:i + 1]),
                                            dact, name=f"ffn{i}_conv_act_bwd")
        dw_up = ffn_pairs_to_shards(matmul(hf, da, ta=True, name=f"ffn{i}_up_dw"), name=f"ffn{i}_up_dw_shards")
        dhf = matmul(da, w_up, tb=True, name=f"ffn{i}_up_dx")
        dxin, dnorm = rms_bwd(xin, w["ffn_norm"][i:i + 1], dhf, dxo, f"ffn{i}_norm_bwd")
        return dxin, dict(ffn_norm=dnorm, ffn_w_up=dw_up, ffn_dw_w=ffn_deinterleave(ddw_w),
                          ffn_dw_b=ffn_deinterleave(ddw_b), ffn_w_down=dw_down)

    w_in_e = jnp.pad(w["e_w_in"][0], ((0, 0), (0, EVEN_IN_PAD - EVEN_IN)))
    conv_w_e, conv_b_e = w["e_conv_w"][0], w["e_conv_b"]
    dt_bias, a_log, d_skip = _pad_heads(w["e_dt_bias"]), _pad_heads(w["e_a_log"]), _pad_heads(w["e_d"])
    xbc_off = 4 * D_MODEL

    hn0 = rms_fwd(x, w["mix_norm"][0:1], "mix0_norm")
    proj0 = matmul(hn0, w_in_e, out_dtype=BF16, name="even_in")
    dt_raw = matmul(hn0, w_in_e[:, EVEN_IN_PAD - LANE:], name="even_in_dt")
    y_ret, ret_states = retention_fwd(proj0, cos2, sin2)
    xbc = conv_fwd(proj0, conv_w_e, conv_b_e, act=True, off=xbc_off, name="ssd_conv")
    y_ssm, ssd_states = ssd_fwd(xbc, dt_raw, proj0, dt_bias, a_log, d_skip, w["e_ssm_norm"])
    mix0 = jnp.concatenate([y_ret, y_ssm], axis=1)
    if late_weights is not None:
        w.update(late_weights(y_ssm))
    w_out_e = w["e_w_out"][0]
    x1 = matmul(mix0, w_out_e, res=x, name="even_out")
    x2, ffn0_saved = ffn_fwd(0, x1)

    w_in_o, w_out_o, glu_w = w["o_w_in"][0], w["o_w_out"][0], w["o_glu_w"][0]
    dw_w_o, dw_b_o, ln_g, ln_b, d_o = w["o_dw_w"][0], w["o_dw_b"], w["o_ln_g"], w["o_ln_b"], w["o_d"]
    rep = jnp.asarray(np.repeat(np.eye(S5_GROUPS, dtype=np.float32), S5_GROUP, axis=0))
    rows_gc = (S5_GROUPS * S5_GROUP, S5_STATE)
    prep_in = [w["o_a_re"][0], w["o_a_im"][0], w["o_log_step"].reshape(S5_GROUPS, 1),
               w["o_b_re"][0].transpose(0, 2, 1).reshape(rows_gc), w["o_b_im"][0].transpose(0, 2, 1).reshape(rows_gc), rep]
    ab_re, ab_im, bb_re, bb_im = whole_fwd(
        _s5_prep, prep_in, [(S5_GROUPS, S5_STATE)] * 2 + [rows_gc] * 2, name="s5_prep")
    a_re_row, a_im_row = ab_re.reshape(1, S5_LANES), ab_im.reshape(1, S5_LANES)
    b_re_bd, b_im_bd = _block_diag(bb_re).astype(BF16), _block_diag(bb_im).astype(BF16)
    c_re_bd = _block_diag(w["o_c_re"][0].reshape(rows_gc)).astype(BF16)
    c_im_neg_bd = _block_diag(-w["o_c_im"][0].reshape(rows_gc)).astype(BF16)

    hn1 = rms_fwd(x2, w["mix_norm"][1:2], "mix1_norm")
    proj1 = matmul(hn1, w_in_o, name="odd_in")
    half = D_MODEL // 2
    c_glu = rowwise_fwd(_glu, [Cols(proj1, half, 0), Cols(proj1, half, 1)], [], [], [], [(half, F32)],
                        name="conf_glu", tl=512)[0]
    c_conv = conv_fwd(c_glu, dw_w_o, dw_b_o, act=False, name="conf_conv")
    c_out = rowwise_fwd(_ln_silu, [c_conv], [], [ln_g, ln_b], [], [(half, BF16)], name="conf_ln", tl=512)[0]
    u_seg = _seg_interleave(proj1[:, 2 * half:])
    bu_re = matmul(u_seg, b_re_bd, name="s5_bu_re")
    bu_im = matmul(u_seg, b_im_bd, name="s5_bu_im")
    xs_re, xs_im = s5_scan(bu_re, bu_im, a_re_row, a_im_row, name="s5_scan")
    y_im = matmul(xs_im, c_im_neg_bd, tb=True, name="s5_y_im")
    y_s5 = _seg_deinterleave(matmul(xs_re, c_re_bd, tb=True, res=y_im, name="s5_y_re"))
    s_out = rowwise_fwd(_s5_post, [y_s5, Cols(proj1, half, 2)], [], [d_o, glu_w], [], [(half, BF16)],
                        name="s5_post", tl=512)[0]
    mix1 = jnp.concatenate([c_out, s_out], axis=1)
    x3 = matmul(mix1, w_out_o, res=x2, name="odd_out")
    x4, ffn1_saved = ffn_fwd(1, x3)

    loss, dx4, dfinal = loss_head(x4, tgt, w["final_norm"].reshape(1, D_MODEL))
    grads["final_norm"] = dfinal.reshape(D_MODEL)

    dx3, g_ffn1 = ffn_bwd(1, x3, ffn1_saved, dx4)
    dmix1 = matmul(dx3, w_out_o, tb=True, name="odd_out_dx")
    grads["o_w_out"] = [matmul(mix1, dx3, ta=True, name="odd_out_dw")]
    dc_conv, dln_g, dln_b = rowwise_bwd(_ln_silu, [c_conv], [], [ln_g, ln_b], [], [Cols(dmix1, half, 0)], [F32],
                                        name="conf_ln_bwd", tl=512)
    dc_glu, ddw_w_o, ddw_b_o = conv_bwd(c_glu, dw_w_o, dw_b_o, dc_conv, act=False, name="conf_conv_bwd")
    d_cacg = rowwise_bwd(_glu, [Cols(proj1, half, 0), Cols(proj1, half, 1)], [], [], [], [dc_glu], [BF16],
                         name="conf_glu_bwd", tl=512, merge=True)[0]
    dy_s5, du_post, dd_o, dglu_w = rowwise_bwd(
        _s5_post, [y_s5, Cols(proj1, half, 2)], [], [d_o, glu_w], [], [Cols(dmix1, half, 1)], [F32, F32],
        name="s5_post_bwd", tl=512)
    dy_seg = _seg_interleave(dy_s5)
    dxs_re = matmul(dy_seg, c_re_bd, name="s5_dx_re")
    dxs_im = matmul(dy_seg, c_im_neg_bd, name="s5_dx_im")
    dc_re_bd = matmul(dy_seg, xs_re, ta=True, name="s5_dc_re")
    dc_im_neg_bd = matmul(dy_seg, xs_im, ta=True, name="s5_dc_im")
    g_re, g_im, dab_re, dab_im = s5_scan(dxs_re, dxs_im, a_re_row, -a_im_row, reverse=True, states=(xs_re, xs_im),
                                         name="s5_scan_bwd", lw=LANE)
    dbb_re = _block_diag_take(matmul(u_seg, g_re, ta=True, name="s5_db_re"))
    dbb_im = _block_diag_take(matmul(u_seg, g_im, ta=True, name="s5_db_im"))
    du_im = matmul(g_im, b_im_bd, tb=True, name="s5_du_im")
    du = _seg_deinterleave(matmul(g_re, b_re_bd, tb=True, res=du_im, name="s5_du_re")) + du_post
    da_re, da_im, dlog_step, db_re, db_im = whole_bwd(
        _s5_prep, prep_in, 5,
        [dab_re.reshape(S5_GROUPS, S5_STATE), dab_im.reshape(S5_GROUPS, S5_STATE), dbb_re, dbb_im], name="s5_prep_bwd")
    gcn = (S5_GROUPS, S5_GROUP, S5_STATE)
    grads.update(
        o_a_re=da_re[None], o_a_im=da_im[None], o_log_step=dlog_step.reshape(1, S5_GROUPS),
        o_b_re=db_re.reshape(gcn).transpose(0, 2, 1)[None], o_b_im=db_im.reshape(gcn).transpose(0, 2, 1)[None],
        o_c_re=_block_diag_take(dc_re_bd).reshape(gcn)[None], o_c_im=-_block_diag_take(dc_im_neg_bd).reshape(gcn)[None],
        o_d=dd_o, o_glu_w=[dglu_w], o_dw_w=ddw_w_o[None], o_dw_b=ddw_b_o, o_ln_g=dln_g, o_ln_b=dln_b)
    dproj1 = jnp.concatenate([d_cacg, du.astype(BF16)], axis=1)
    grads["o_w_in"] = [matmul(hn1, dproj1, ta=True, name="odd_in_dw")]
    dhn1 = matmul(dproj1, w_in_o, tb=True, name="odd_in_dx")
    dx2, dmix_norm1 = rms_bwd(x2, w["mix_norm"][1:2], dhn1, dx3, "mix1_norm_bwd")

    if early_reduce is not None:
        zero = early_reduce({("o_w_in", 0): grads["o_w_in"][0], ("o_glu_w", 0): grads["o_glu_w"][0],
                             ("o_w_out", 0): grads["o_w_out"][0], ("ffn_w_up", 1): g_ffn1["ffn_w_up"],
                             ("ffn_w_down", 1): g_ffn1["ffn_w_down"]})
        w["ffn_dw_b"] = w["ffn_dw_b"] + zero
    dx1, g_ffn0 = ffn_bwd(0, x1, ffn0_saved, dx2)
    if early_reduce is not None:
        dt_bias = dt_bias + early_reduce({("ffn_w_up", 0): g_ffn0["ffn_w_up"], ("ffn_w_down", 0): g_ffn0["ffn_w_down"]})
    for k in g_ffn0:
        per_layer = [g_ffn0[k], g_ffn1[k]]
        grads[k] = per_layer if k in ("ffn_w_up", "ffn_w_down") else jnp.stack(per_layer).reshape(w[k].shape)
    dmix0 = matmul(dx1, w_out_e, tb=True, name="even_out_dx")
    grads["e_w_out"] = [matmul(mix0, dx1, ta=True, name="even_out_dw")]
    if early_reduce is not None:
        a_log = a_log + early_reduce({("e_w_out", 0): grads["e_w_out"][0]})
    dq, dk, dv, dg = retention_bwd(proj0, cos2, sin2, ret_states, dmix0)
    dxbc_c, ddt, dz, ddt_bias, da_log, dd_skip, dssm_norm = ssd_bwd(
        xbc, dt_raw, proj0, dt_bias, a_log, d_skip, w["e_ssm_norm"], ssd_states, dmix0)
    dxbc, dconv_w, dconv_b = conv_bwd(proj0, conv_w_e, conv_b_e, dxbc_c, act=True, off=xbc_off,
                                      name="ssd_conv_bwd", dx_dtype=BF16)
    dproj0 = jnp.concatenate([dq, dk, dv, dg, dz, dxbc, ddt], axis=1)
    grads["e_w_in"] = [matmul(hn0, dproj0, ta=True, name="even_in_dw")[:, :EVEN_IN]]
    dhn0 = matmul(dproj0, w_in_e, tb=True, name="even_in_dx")
    dx, dmix_norm0 = rms_bwd(x, w["mix_norm"][0:1], dhn0, dx1, "mix0_norm_bwd")
    grads.update(
        mix_norm=jnp.concatenate([dmix_norm0, dmix_norm1], axis=0), e_conv_w=dconv_w[None], e_conv_b=dconv_b,
        e_dt_bias=ddt_bias[:, :SSM_HEADS], e_a_log=da_log[:, :SSM_HEADS], e_d=dd_skip[:, :SSM_HEADS],
        e_ssm_norm=dssm_norm)
    return loss, dx, grads


def adamw(w, g, m, v, *, name):
    shape = w.shape
    cols = shape[-1]
    rows = w.size // cols
    tr = _tile(rows, max(8, (512 * 1024 // cols) // 8 * 8), unit=8)

    def body(w_ref, g_ref, m_ref, v_ref, d_ref, nm_ref, nv_ref):
        gv = g_ref[...]
        nm = ADAM_B1 * m_ref[...] + (1.0 - ADAM_B1) * gv
        nv = ADAM_B2 * v_ref[...] + (1.0 - ADAM_B2) * jnp.square(gv)
        m_hat = nm / (1.0 - ADAM_B1 ** ADAM_STEP)
        v_hat = nv / (1.0 - ADAM_B2 ** ADAM_STEP)
        d_ref[...] = -ADAM_LR * (m_hat / (jnp.sqrt(v_hat) + ADAM_EPS) + ADAM_WD * w_ref[...])
        nm_ref[...] = nm
        nv_ref[...] = nv

    spec = pl.BlockSpec((tr, cols), lambda i: (i, 0))
    outs = pl.pallas_call(
        body, name=name, grid=(rows // tr,), in_specs=[spec] * 4, out_specs=[spec] * 3,
        out_shape=[jax.ShapeDtypeStruct((rows, cols), F32)] * 3, compiler_params=_params(("parallel",)),
    )(*[t.reshape(rows, cols) for t in (w, g, m, v)])
    return [o.reshape(shape) for o in outs]


OTHER_CHIPS = ((1, 0), (0, 1), (1, 1))
ANY = pl.BlockSpec(memory_space=pl.ANY)


def _position():
    return lax.axis_index("x"), lax.axis_index("y"), lax.axis_index("c")


def _flip(v, f):
    return 1 - v if f else v


def _remote(src, dst, send_sem, recv_sem, device):
    return pltpu.make_async_remote_copy(src_ref=src, dst_ref=dst, send_sem=send_sem, recv_sem=recv_sem,
                                        device_id=device, device_id_type=MESH)


def gather_shards(big, small):
    n_big, n_small = len(big), len(small)
    halves = [a.shape[0] // 2 for a in big]

    def body(*refs):
        big_refs, small_refs = refs[:n_big], refs[n_big:n_big + n_small]
        obig_refs = refs[n_big + n_small:2 * n_big + n_small]
        osmall_refs = refs[2 * n_big + n_small:2 * (n_big + n_small)]
        ici_send, ici_recv, d2d_send, d2d_recv, small_send, small_recv = refs[2 * (n_big + n_small):]
        x, y, c = _position()
        mine = 2 * x + y

        def half(k, core):
            return pl.ds(pl.multiple_of(core * halves[k], 16), halves[k])

        sends = []
        for j, (fx, fy) in enumerate(OTHER_CHIPS):
            peer = (_flip(x, fx), _flip(y, fy), c)
            for k in range(n_big):
                sends.append(_remote(big_refs[k].at[half(k, c)], obig_refs[k].at[mine, half(k, c)],
                                     ici_send.at[j, k], ici_recv.at[j, k], peer))
            for k in range(n_small):
                sends.append(_remote(small_refs[k], osmall_refs[k].at[mine], small_send.at[j, k], small_recv.at[j, k], peer))
        for cp in sends:
            cp.start()
        for j, (fx, fy) in enumerate(OTHER_CHIPS):
            px, py = _flip(x, fx), _flip(y, fy)
            src_chip = 2 * px + py
            for k in range(n_big):
                landed = obig_refs[k].at[src_chip, half(k, c)]
                _remote(landed, landed, ici_send.at[j, k], ici_recv.at[j, k], (px, py, c)).wait_recv()
                fwd = _remote(landed, landed, d2d_send.at[j, k], d2d_recv.at[j, k], (x, y, 1 - c))
                fwd.start()
                sends.append(fwd)
        for j, (fx, fy) in enumerate(OTHER_CHIPS):
            px, py = _flip(x, fx), _flip(y, fy)
            src_chip = 2 * px + py
            for k in range(n_big):
                other = obig_refs[k].at[src_chip, half(k, 1 - c)]
                _remote(other, other, d2d_send.at[j, k], d2d_recv.at[j, k], (x, y, 1 - c)).wait_recv()
            for k in range(n_small):
                dst = osmall_refs[k].at[src_chip]
                _remote(small_refs[k], dst, small_send.at[j, k], small_recv.at[j, k], (px, py, c)).wait_recv()
        for cp in sends:
            cp.wait_send()

    arrays = list(big) + list(small)
    dma = pltpu.SemaphoreType.DMA
    return pl.pallas_call(
        body, name="gather_shards", in_specs=[ANY] * len(arrays), out_specs=[ANY] * len(arrays),
        out_shape=[jax.ShapeDtypeStruct((4,) + a.shape, a.dtype) for a in arrays],
        scratch_shapes=[dma((3, n_big)), dma((3, n_big)), dma((3, n_big)), dma((3, n_big)),
                        dma((3, n_small)), dma((3, n_small))],
        compiler_params=_params(),
    )(*arrays)


def allreduce_small(pack):
    rows = pack.shape[0]

    def body(p_ref, o_ref, slots, send_sems, recv_sems):
        x, y, c = _position()
        me = 4 * x + 2 * y + c
        slots[me] = p_ref[...]
        flips = [((k >> 2) & 1, (k >> 1) & 1, k & 1) for k in range(1, 8)]
        sends = []
        for k, (fx, fy, fc) in enumerate(flips):
            peer = (_flip(x, fx), _flip(y, fy), _flip(c, fc))
            sends.append(_remote(p_ref, slots.at[me], send_sems.at[k], recv_sems.at[k], peer))
        for cp in sends:
            cp.start()
        for k, (fx, fy, fc) in enumerate(flips):
            px, py, pc = _flip(x, fx), _flip(y, fy), _flip(c, fc)
            _remote(p_ref, slots.at[4 * px + 2 * py + pc], send_sems.at[k], recv_sems.at[k], (px, py, pc)).wait_recv()
        for cp in sends:
            cp.wait_send()
        acc = slots[0]
        for d in range(1, 8):
            acc = acc + slots[d]
        o_ref[...] = acc

    vmem = pl.BlockSpec(memory_space=pltpu.VMEM)
    return pl.pallas_call(
        body, name="allreduce_small", in_specs=[vmem], out_specs=vmem,
        out_shape=jax.ShapeDtypeStruct(pack.shape, F32),
        scratch_shapes=[pltpu.VMEM((8, rows, LANE), F32), pltpu.SemaphoreType.DMA((7,)), pltpu.SemaphoreType.DMA((7,))],
        compiler_params=_params(),
    )(pack)


def exchange_halves(gs, *, name):
    n = len(gs)

    def body(*refs):
        g_refs, o_refs, (send_sems, recv_sems) = refs[:n], refs[n:2 * n], refs[2 * n:]
        x, y, c = _position()
        copies = [_remote(g_refs[k].at[:, 1 - c], o_refs[k], send_sems.at[k], recv_sems.at[k], (x, y, 1 - c)) for k in range(n)]
        for cp in copies:
            cp.start()
        for cp in copies:
            cp.wait()

    return pl.pallas_call(
        body, name=name, in_specs=[ANY] * n, out_specs=[ANY] * n,
        out_shape=[jax.ShapeDtypeStruct((4,) + g.shape[2:], g.dtype) for g in gs],
        scratch_shapes=[pltpu.SemaphoreType.DMA((n,)), pltpu.SemaphoreType.DMA((n,))],
        compiler_params=_params(),
    )(*gs)


def scatter_to_chips(parts):
    n = len(parts)

    def body(*refs):
        a_refs, o_refs, (send_sems, recv_sems) = refs[:n], refs[n:2 * n], refs[2 * n:]
        x, y, c = _position()
        copies = []
        for j, (fx, fy) in enumerate(OTHER_CHIPS):
            px, py = _flip(x, fx), _flip(y, fy)
            for k in range(n):
                copies.append(_remote(a_refs[k].at[2 * px + py], o_refs[k].at[j], send_sems.at[j, k], recv_sems.at[j, k], (px, py, c)))
        for cp in copies:
            cp.start()
        for cp in copies:
            cp.wait()

    return pl.pallas_call(
        body, name="scatter_to_chips", in_specs=[ANY] * n, out_specs=[ANY] * n,
        out_shape=[jax.ShapeDtypeStruct((3,) + a.shape[1:], a.dtype) for a in parts],
        scratch_shapes=[pltpu.SemaphoreType.DMA((3, n)), pltpu.SemaphoreType.DMA((3, n))],
        compiler_params=_params(),
    )(*parts)


def swap_halves(rs):
    n = len(rs)

    def body(*refs):
        r_refs, o_refs, (send_sems, recv_sems) = refs[:n], refs[n:2 * n], refs[2 * n:]
        x, y, c = _position()
        copies = [_remote(r_refs[k], o_refs[k], send_sems.at[k], recv_sems.at[k], (x, y, 1 - c)) for k in range(n)]
        for cp in copies:
            cp.start()
        for cp in copies:
            cp.wait()

    dma = pltpu.SemaphoreType.DMA
    return pl.pallas_call(
        body, name="swap_halves", in_specs=[ANY] * n, out_specs=[ANY] * n,
        out_shape=[jax.ShapeDtypeStruct(r.shape, r.dtype) for r in rs],
        scratch_shapes=[dma((n,)), dma((n,))],
        compiler_params=_params(),
    )(*rs)


HBM = pl.BlockSpec(memory_space=pltpu.HBM)
SEM = pl.BlockSpec(memory_space=pltpu.SEMAPHORE)
SIDE_EFFECT = pltpu.SideEffectType.DATAFLOW_SIDE_EFFECTING


def _gather_plan(halves):
    def plan(v_refs, land_refs, x, y, c):
        copies = []
        for fx, fy in OTHER_CHIPS:
            for k in range(len(v_refs)):
                rows = pl.ds(pl.multiple_of(c * halves[k], 16), halves[k])
                copies.append((v_refs[k].at[rows], land_refs[k].at[2 * x + y, rows], (_flip(x, fx), _flip(y, fy), c)))
        return copies
    return plan


def _scatter_plan(v_refs, land_refs, x, y, c):
    copies = []
    for j, (fx, fy) in enumerate(OTHER_CHIPS):
        px, py = _flip(x, fx), _flip(y, fy)
        for k in range(len(v_refs)):
            copies.append((v_refs[k].at[2 * px + py], land_refs[k].at[j], (px, py, c)))
    return copies


def chip_exchange_start(srcs, land_shapes, plan, after, *, name):
    n = len(srcs)
    n_cp = 3 * n

    def body(*refs):
        v_refs, land_refs = refs[:n], refs[n:2 * n]
        outs = refs[2 * n + 1:]
        sends, recvs, token = outs[:n_cp], outs[n_cp:2 * n_cp], outs[-1]
        x, y, c = _position()
        for (src, dst, device), send, recv in zip(plan(v_refs, land_refs, x, y, c), sends, recvs, strict=True):
            _remote(src, dst, send, recv, device).start()
        token[...] = jnp.zeros_like(token)

    lands = [lax.empty(shape, v.dtype) for shape, v in zip(land_shapes, srcs)]
    arrays = [pltpu.with_memory_space_constraint(a, pltpu.HBM) for a in list(srcs) + lands]
    outs = pl.pallas_call(
        body, name=name,
        out_shape=tuple(pltpu.SemaphoreType.DMA(()) for _ in range(2 * n_cp))
        + tuple(pltpu.HBM(a.shape, a.dtype) for a in arrays) + (jax.ShapeDtypeStruct((8, LANE), F32),),
        in_specs=[HBM] * (2 * n) + [ANY],
        out_specs=(SEM,) * (2 * n_cp) + (HBM,) * (2 * n) + (pl.BlockSpec(memory_space=pltpu.VMEM),),
        input_output_aliases={i: 2 * n_cp + i for i in range(2 * n)},
        compiler_params=pltpu.CompilerParams(has_side_effects=SIDE_EFFECT),
    )(*arrays, after)
    handle = (outs[:n_cp], outs[n_cp:2 * n_cp], outs[2 * n_cp:2 * n_cp + n], outs[2 * n_cp + n:2 * n_cp + 2 * n])
    return handle, outs[-1]


def chip_exchange_wait(handle, plan, after, *, name):
    sends, recvs, v_thru, land_thru = handle
    n = len(v_thru)
    n_cp = 3 * n

    def body(*refs):
        v_refs, land_refs = refs[:n], refs[n:2 * n]
        sends, recvs = refs[2 * n:2 * n + n_cp], refs[2 * n + n_cp:2 * n + 2 * n_cp]
        x, y, c = _position()
        for (src, dst, device), send, recv in zip(plan(v_refs, land_refs, x, y, c), sends, recvs, strict=True):
            copy = _remote(src, dst, send, recv, device)
            copy.wait_send()
            copy.wait_recv()

    outs = pl.pallas_call(
        body, name=name,
        out_shape=tuple(pltpu.HBM(a.shape, a.dtype) for a in list(v_thru) + list(land_thru)),
        in_specs=[HBM] * (2 * n) + [SEM] * (2 * n_cp) + [ANY], out_specs=(HBM,) * (2 * n),
        input_output_aliases={i: i for i in range(2 * n)},
        compiler_params=pltpu.CompilerParams(has_side_effects=SIDE_EFFECT),
    )(*v_thru, *land_thru, *sends, *recvs, after)
    return outs[:n], outs[n:]


def finish_gather(lands):
    n = len(lands)
    halves = [a.shape[1] // 2 for a in lands]

    def body(*refs):
        o_refs, (send_sems, recv_sems) = refs[n:2 * n], refs[2 * n:]
        x, y, c = _position()

        def half(k, core):
            return pl.ds(pl.multiple_of(core * halves[k], 16), halves[k])

        sends = []
        for j, (fx, fy) in enumerate(OTHER_CHIPS):
            src_chip = 2 * _flip(x, fx) + _flip(y, fy)
            for k in range(n):
                held = o_refs[k].at[src_chip, half(k, c)]
                sends.append(_remote(held, held, send_sems.at[j, k], recv_sems.at[j, k], (x, y, 1 - c)))
        for cp in sends:
            cp.start()
        for j, (fx, fy) in enumerate(OTHER_CHIPS):
            src_chip = 2 * _flip(x, fx) + _flip(y, fy)
            for k in range(n):
                other = o_refs[k].at[src_chip, half(k, 1 - c)]
                _remote(other, other, send_sems.at[j, k], recv_sems.at[j, k], (x, y, 1 - c)).wait_recv()
        for cp in sends:
            cp.wait_send()

    dma = pltpu.SemaphoreType.DMA
    return pl.pallas_call(
        body, name="finish_gather", in_specs=[ANY] * n, out_specs=[ANY] * n,
        out_shape=[jax.ShapeDtypeStruct(a.shape, a.dtype) for a in lands],
        input_output_aliases={k: k for k in range(n)},
        scratch_shapes=[dma((3, n)), dma((3, n))],
        compiler_params=_params(),
    )(*lands)


def add_own_half(g, r, c_idx, *, name):
    _, _, h, cols = g.shape

    def body(c_ref, g_ref, r_ref, o_ref):
        o_ref[...] = (g_ref[0] + r_ref[...]).astype(o_ref.dtype)

    return pl.pallas_call(
        body, name=name,
        grid_spec=pltpu.PrefetchScalarGridSpec(
            num_scalar_prefetch=1, grid=(4,),
            in_specs=[pl.BlockSpec((1, 1, h, cols), lambda s, c: (s, c[0], 0, 0)),
                      pl.BlockSpec((1, h, cols), lambda s, c: (s, 0, 0))],
            out_specs=pl.BlockSpec((1, h, cols), lambda s, c: (s, 0, 0))),
        out_shape=jax.ShapeDtypeStruct(r.shape, BF16), compiler_params=_params(("parallel",)),
    )(c_idx, g, r)


def add_chip_parts(a, parts, chip_idx, *, name):
    _, h, cols = a.shape
    th = h // 2

    def body(s_ref, a_ref, p0_ref, p1_ref, p2_ref, o_ref):
        f = lambda r: r[0].astype(F32)
        o_ref[...] = ((f(a_ref) + f(p0_ref)) + f(p1_ref)) + f(p2_ref)

    part = lambda j: pl.BlockSpec((1, th, cols), lambda i, s, j=j: (j, i, 0))
    return pl.pallas_call(
        body, name=name,
        grid_spec=pltpu.PrefetchScalarGridSpec(
            num_scalar_prefetch=1, grid=(2,),
            in_specs=[pl.BlockSpec((1, th, cols), lambda i, s: (s[0], i, 0)), part(0), part(1), part(2)],
            out_specs=pl.BlockSpec((th, cols), lambda i, s: (i, 0))),
        out_shape=jax.ShapeDtypeStruct((h, cols), F32), compiler_params=_params(("parallel",)),
    )(chip_idx, a, parts, parts, parts)


WEIGHTS = ("mix_norm", "e_w_in", "e_conv_w", "e_conv_b", "e_dt_bias", "e_a_log", "e_d", "e_ssm_norm", "e_w_out",
           "o_w_in", "o_dw_w", "o_dw_b", "o_ln_g", "o_ln_b", "o_a_re", "o_a_im", "o_b_re", "o_b_im", "o_c_re",
           "o_c_im", "o_d", "o_log_step", "o_glu_w", "o_w_out", "ffn_norm", "ffn_w_up", "ffn_dw_w", "ffn_dw_b",
           "ffn_w_down", "final_norm")
BIG = (("e_w_in", 2), ("e_w_out", 1), ("o_w_in", 2), ("o_glu_w", 1), ("o_w_out", 1), ("ffn_w_up", 2), ("ffn_w_down", 1))
SMALL_SHARDED = (("e_conv_w", 2), ("o_dw_w", 2), ("o_dw_b", 1), ("o_ln_g", 1), ("o_ln_b", 1), ("o_d", 1), ("ffn_dw_w", 2))
REPLICATED = tuple(n for n in WEIGHTS if n not in dict(BIG + SMALL_SHARDED))
PACK_ROWS = 8


def _pack(arrays, dtype, row_unit=PACK_ROWS):
    flat = jnp.concatenate([a.astype(dtype).reshape(-1) for a in arrays])
    rows = -(-flat.size // (LANE * row_unit)) * row_unit
    return jnp.pad(flat, (0, rows * LANE - flat.size)).reshape(rows, LANE)


def _unpack(flat, shapes, lead=()):
    out, off = [], 0
    for shape in shapes:
        size = int(np.prod(shape))
        out.append(flat[..., off:off + size].reshape(lead + tuple(shape)))
        off += size
    return out


def _join_shards(parts, axis):
    return jnp.concatenate([parts[s] for s in range(4)], axis=axis)


def _split_shards(full, axis):
    return jnp.stack(jnp.split(full, 4, axis=axis))


def _rows2d(a):
    return a.reshape(-1, a.shape[-1])


def _layer_shards(g, axis):
    if g.ndim == 3:
        return g.reshape(4, 2, g.shape[1] // 2, g.shape[2])
    rows, cols = g.shape
    if axis == 0:
        return g.reshape(4, 2, rows // 8, cols)
    return g.reshape(rows, 4, cols // 4).transpose(1, 0, 2).reshape(4, 2, rows // 2, cols // 4)


def kernel(x, mix_norm, e_w_in, e_conv_w, e_conv_b, e_dt_bias, e_a_log, e_d, e_ssm_norm, e_w_out, o_w_in, o_dw_w, o_dw_b, o_ln_g, o_ln_b, o_a_re, o_a_im, o_b_re, o_b_im, o_c_re, o_c_im, o_d, o_log_step, o_glu_w, o_w_out, ffn_norm, ffn_w_up, ffn_dw_w, ffn_dw_b, ffn_w_down, final_norm, loss_target, m_mix_norm, m_e_w_in, m_e_conv_w, m_e_conv_b, m_e_dt_bias, m_e_a_log, m_e_d, m_e_ssm_norm, m_e_w_out, m_o_w_in, m_o_dw_w, m_o_dw_b, m_o_ln_g, m_o_ln_b, m_o_a_re, m_o_a_im, m_o_b_re, m_o_b_im, m_o_c_re, m_o_c_im, m_o_d, m_o_log_step, m_o_glu_w, m_o_w_out, m_ffn_norm, m_ffn_w_up, m_ffn_dw_w, m_ffn_dw_b, m_ffn_w_down, m_final_norm, v_mix_norm, v_e_w_in, v_e_conv_w, v_e_conv_b, v_e_dt_bias, v_e_a_log, v_e_d, v_e_ssm_norm, v_e_w_out, v_o_w_in, v_o_dw_w, v_o_dw_b, v_o_ln_g, v_o_ln_b, v_o_a_re, v_o_a_im, v_o_b_re, v_o_b_im, v_o_c_re, v_o_c_im, v_o_d, v_o_log_step, v_o_glu_w, v_o_w_out, v_ffn_norm, v_ffn_w_up, v_ffn_dw_w, v_ffn_dw_b, v_ffn_w_down, v_final_norm):
    given = dict(locals())
    chip = 2 * lax.axis_index("x") + lax.axis_index("y")
    core = lax.axis_index("c")

    core_idx, chip_idx = core.reshape(1).astype(jnp.int32), chip.reshape(1).astype(jnp.int32)

    def whole(n, axis, parts):
        shape = given[n].shape
        own = given[n].astype(parts.dtype)
        return _join_shards(lax.dynamic_update_index_in_dim(parts.reshape((4,) + shape), own, chip, 0), axis)

    first, later = BIG[:1], BIG[1:]
    shards = {n: _rows2d(given[n]).astype(BF16) for n, _ in BIG}
    gathered = gather_shards([shards[n] for n, _ in first], [_rows2d(given[n]) for n, _ in SMALL_SHARDED])
    w = {n: given[n] for n in REPLICATED}
    for (n, axis), parts in zip(first + SMALL_SHARDED, gathered):
        w[n] = whole(n, axis, parts)
    later_keys = [(n, layer) for n, _ in later for layer in (range(2) if n.startswith("ffn_") else [None])]
    later_shards = [shards[n] if layer is None else given[n][layer].astype(BF16) for n, layer in later_keys]
    gather_plan = _gather_plan([a.shape[0] // 2 for a in later_shards])
    gather_handle, token = chip_exchange_start(later_shards, [(4,) + a.shape for a in later_shards], gather_plan,
                                               gathered[0], name="gather_start")
    w["mix_norm"] = w["mix_norm"] + token[0, 0]

    def late_weights(after):
        _, lands = chip_exchange_wait(gather_handle, gather_plan, after, name="gather_wait")
        out = {"ffn_w_up_pairs": [], "ffn_w_down": []}
        for (n, layer), own, parts in zip(later_keys, later_shards, finish_gather(lands)):
            if layer is None:
                out[n] = whole(n, dict(BIG)[n], parts)
                continue
            parts = lax.dynamic_update_index_in_dim(parts, own, chip, 0)
            if n == "ffn_w_up":
                out["ffn_w_up_pairs"].append(ffn_shards_to_pairs(parts, name=f"ffn{layer}_up_pairs"))
            else:
                out[n].append(parts.reshape(-1, parts.shape[-1]))
        return out

    groups = []

    def finish_group(after):
        group = groups[-1]
        group["sums"], group["parts"] = chip_exchange_wait(group.pop("handle"), _scatter_plan, after,
                                                           name=f"scatter_wait_{len(groups) - 1}")

    def early_reduce(layer_grads):
        keys = list(layer_grads)
        if groups:
            finish_group(layer_grads[keys[0]])
        tag = len(groups)
        parts = [_layer_shards(layer_grads[k], dict(BIG)[k[0]] - 1) for k in keys]
        sums = [add_own_half(g, r, core_idx, name=f"add_own_half_{n}{layer}")
                for g, r, (n, layer) in zip(parts, exchange_halves(parts, name=f"exchange_halves_{tag}"), keys)]
        handle, zeros = chip_exchange_start(sums, [(3,) + a.shape[1:] for a in sums], _scatter_plan, sums[0],
                                            name=f"scatter_start_{tag}")
        groups.append(dict(keys=keys, handle=handle))
        return zeros[0, 0]

    loss, dx, grads = local_step(x[0], loss_target[0], w, late_weights, early_reduce)
    finish_group(dx)
    early_keys = [k for group in groups for k in group["keys"]]
    early_sums = [a for group in groups for a in group["sums"]]
    early_parts = [a for group in groups for a in group["parts"]]

    small_names = REPLICATED + tuple(n for n, _ in SMALL_SHARDED)
    small_sum = allreduce_small(_pack([grads[n] for n in small_names], F32))
    reduced = dict(zip(small_names, _unpack(small_sum.reshape(-1), [grads[n].shape for n in small_names])))
    for n, axis in SMALL_SHARDED:
        width = given[n].shape[axis]
        reduced[n] = lax.dynamic_slice_in_dim(reduced[n], chip * width, width, axis=axis)

    keys, parts = [], []
    for n, axis in BIG:
        for layer, g in enumerate(grads[n]):
            if (n, layer) not in early_keys:
                keys.append((n, layer))
                parts.append(_layer_shards(g, axis - 1))
    core_sums = [add_own_half(g, r, core_idx, name=f"add_own_half_{n}{layer}")
                 for g, r, (n, layer) in zip(parts, exchange_halves(parts, name="exchange_halves_last"), keys)]
    chip_parts = scatter_to_chips(core_sums)
    keys, core_sums, chip_parts = early_keys + keys, early_sums + core_sums, early_parts + list(chip_parts)
    mine = [add_chip_parts(a, p, chip_idx, name=f"add_chip_parts_{n}{layer}")
            for a, p, (n, layer) in zip(core_sums, chip_parts, keys)]
    layers = {}
    for (n, layer), own, other in zip(keys, mine, swap_halves(mine)):
        both = jnp.where(core == 0, jnp.stack([own, other]), jnp.stack([other, own]))
        layers.setdefault(n, {})[layer] = both.reshape(given[n].shape[1:])
    for n, _ in BIG:
        reduced[n] = jnp.stack([layers[n][layer] for layer in sorted(layers[n])])

    delta, new_m, new_v = {}, {}, {}
    for n in WEIGHTS:
        delta[n], new_m[n], new_v[n] = adamw(given[n], reduced[n], given["m_" + n], given["v_" + n], name="adamw_" + n)

    total = lax.psum(loss[0, 0], ("x", "y", "c"))
    return (total, dx[None], *[reduced[n] for n in WEIGHTS], *[delta[n] for n in WEIGHTS],
            *[new_m[n] for n in WEIGHTS], *[new_v[n] for n in WEIGHTS])
```

```python
import functools
import math
from typing import NamedTuple

import numpy as np
import jax
import jax.numpy as jnp
from jax import lax
from jax.experimental import pallas as pl
from jax.experimental.pallas import tpu as pltpu

F32 = jnp.float32
BF16 = jnp.bfloat16
HIGHEST = lax.Precision.HIGHEST
MESH = pl.DeviceIdType.MESH

D_MODEL = 1024
EPS = 1e-6
RET_HEADS, RET_DK, RET_DV, CHUNK = 4, 128, 256, 128
ROPE_BASE = 10000.0
SSM_HEADS, SSM_P, SSM_N, SSM_GROUPS = 16, 64, 128, 2
SSM_DINNER = SSM_HEADS * SSM_P
EVEN_IN, EVEN_IN_PAD = 5648, 5760
S5_GROUPS, S5_GROUP, S5_STATE = 32, 16, 64
S5_LANES = S5_GROUPS * S5_STATE
SCAN_SEG = 32
D_FF = 2816
ADAM_LR, ADAM_B1, ADAM_B2, ADAM_EPS, ADAM_WD, ADAM_STEP = 0.001, 0.9, 0.999, 1e-08, 0.01, 10

LANE = 128
VMEM_LIMIT = 56 * 1024 * 1024


def _params(sem=None, **kw):
    return pltpu.CompilerParams(dimension_semantics=sem, vmem_limit_bytes=VMEM_LIMIT, **kw)


def _tile(n, target, unit=LANE):
    if n <= target:
        return n
    t = (target // unit) * unit
    while t >= unit:
        if n % t == 0:
            return t
        t -= unit
    return n


def _silu(x):
    return x * jax.nn.sigmoid(x)


def _mm(a, b):
    return jnp.dot(a.astype(BF16), b.astype(BF16), preferred_element_type=F32)


def _mm_nt(a, b):
    return lax.dot_general(a.astype(BF16), b.astype(BF16), (((1,), (1,)), ((), ())), preferred_element_type=F32)


def _mm_tn(a, b):
    return lax.dot_general(a.astype(BF16), b.astype(BF16), (((0,), (0,)), ((), ())), preferred_element_type=F32)


def _dot_hi(a, b):
    return jnp.dot(a, b, precision=HIGHEST, preferred_element_type=F32)


def _dot_hi_tn(a, b):
    return lax.dot_general(a, b, (((0,), (0,)), ((), ())), precision=HIGHEST, preferred_element_type=F32)


def _bf16_parts(v):
    hi = v.astype(BF16)
    rest = v - hi.astype(F32)
    mid = rest.astype(BF16)
    return hi, mid, (rest - mid.astype(F32)).astype(BF16)


def _dot_parts(v, fixed, dims, v_first):
    fixed = fixed.astype(BF16)
    out = None
    for part in _bf16_parts(v):
        ops = (part, fixed) if v_first else (fixed, part)
        p = lax.dot_general(*ops, (dims, ((), ())), preferred_element_type=F32)
        out = p if out is None else out + p
    return out


@jax.custom_vjp
def _times_01(v, ones):
    return _dot_parts(v, ones, ((1,), (0,)), True)


_times_01.defvjp(lambda v, ones: (_times_01(v, ones), ones),
                 lambda ones, g: (_dot_parts(g, ones, ((1,), (1,)), True), jnp.zeros_like(ones)))


@jax.custom_vjp
def _01_times(ones, v):
    return _dot_parts(v, ones, ((1,), (0,)), False)


_01_times.defvjp(lambda ones, v: (_01_times(ones, v), ones),
                 lambda ones, g: (jnp.zeros_like(ones), _dot_parts(g, ones, ((0,), (0,)), False)))


MATMUL_VMEM = 44 * 1024 * 1024


def matmul(a, b, *, ta=False, tb=False, res=None, out_dtype=F32, name):
    m, k = (a.shape[1], a.shape[0]) if ta else a.shape
    n = b.shape[0] if tb else b.shape[1]
    assert (b.shape[1] if tb else b.shape[0]) == k, (a.shape, b.shape, ta, tb)
    tm = _tile(m, 1536)
    tn = _tile(n, 640)
    if tn < 384:
        tn = _tile(n, 1536)
    res_bytes = 0 if res is None else res.dtype.itemsize

    def vmem(tm, tn):
        return 2 * (tm * k * a.dtype.itemsize + tn * k * b.dtype.itemsize + tm * tn * (jnp.dtype(out_dtype).itemsize + res_bytes))

    while vmem(tm, tn) > MATMUL_VMEM and tm % (2 * LANE) == 0:
        tm //= 2
    assert vmem(tm, tn) <= MATMUL_VMEM, (name, tm, tn, k)
    a_spec = pl.BlockSpec((k, tm), lambda i, j: (0, i)) if ta else pl.BlockSpec((tm, k), lambda i, j: (i, 0))
    b_spec = pl.BlockSpec((tn, k), lambda i, j: (j, 0)) if tb else pl.BlockSpec((k, tn), lambda i, j: (0, j))
    o_spec = pl.BlockSpec((tm, tn), lambda i, j: (i, j))
    dims = (((0 if ta else 1,), (1 if tb else 0,)), ((), ()))
    has_res = res is not None

    def body(a_ref, b_ref, *rest):
        o_ref = rest[-1]
        out = lax.dot_general(a_ref[...].astype(BF16), b_ref[...].astype(BF16), dims, preferred_element_type=F32)
        if has_res:
            out = out + rest[0][...].astype(F32)
        o_ref[...] = out.astype(o_ref.dtype)

    ins = [a, b] + ([res] if has_res else [])
    specs = [a_spec, b_spec] + ([o_spec] if has_res else [])
    return pl.pallas_call(
        body, name=name, grid=(m // tm, n // tn), in_specs=specs, out_specs=o_spec,
        out_shape=jax.ShapeDtypeStruct((m, n), out_dtype), compiler_params=_params(("parallel", "parallel")),
    )(*ins)


class Cols(NamedTuple):
    arr: jax.Array
    w: int
    j: int


def _cols(a):
    return a if isinstance(a, Cols) else Cols(a, a.shape[1], 0)


def _row_spec(c, tl):
    return pl.BlockSpec((tl, c.w), lambda i, j=c.j: (i, j))


def _whole_spec(p):
    return pl.BlockSpec(p.shape, lambda i, nd=p.ndim: (0,) * nd)


def rowwise_fwd(fn, rows, aux, pars, consts, outs, *, name, tl):
    rows = [_cols(r) for r in rows + aux]
    whole = list(pars) + list(consts)
    n_rows = len(rows)
    n_whole = len(whole)
    length = rows[0].arr.shape[0]
    tl = min(tl, length)

    def body(*refs):
        vals = [r[...].astype(F32) for r in refs[:n_rows]] + [r[...] for r in refs[n_rows:n_rows + n_whole]]
        res = fn(*vals)
        for o_ref, v in zip(refs[n_rows + n_whole:], res, strict=True):
            o_ref[...] = v.astype(o_ref.dtype)

    return pl.pallas_call(
        body, name=name, grid=(length // tl,),
        in_specs=[_row_spec(r, tl) for r in rows] + [_whole_spec(p) for p in whole],
        out_specs=[pl.BlockSpec((tl, w), lambda i: (i, 0)) for w, _ in outs],
        out_shape=[jax.ShapeDtypeStruct((length, w), dt) for w, dt in outs],
        compiler_params=_params(("parallel",)),
    )(*[r.arr for r in rows], *whole)


def rowwise_bwd(fn, rows, aux, pars, consts, cots, drow_dtypes, *, name, tl, add=None, merge=False):
    rows = [_cols(r) for r in rows]
    aux = [_cols(r) for r in aux]
    cots = [_cols(r) for r in cots]
    n_r, n_a, n_p, n_c, n_t = len(rows), len(aux), len(pars), len(consts), len(cots)
    length = rows[0].arr.shape[0]
    tl = min(tl, length)
    has_add = add is not None
    widths = [r.w for r in rows]

    def body(*refs):
        pos = 0
        r_vals = [r[...].astype(F32) for r in refs[pos:pos + n_r]]; pos += n_r
        a_vals = [r[...].astype(F32) for r in refs[pos:pos + n_a]]; pos += n_a
        p_vals = [r[...].astype(F32) for r in refs[pos:pos + n_p]]; pos += n_p
        c_vals = [r[...] for r in refs[pos:pos + n_c]]; pos += n_c
        t_vals = [r[...].astype(F32) for r in refs[pos:pos + n_t]]; pos += n_t
        add_val = None
        if has_add:
            add_val = refs[pos][...].astype(F32); pos += 1
        n_dr = 1 if merge else n_r
        dr_refs = refs[pos:pos + n_dr]; pos += n_dr
        dp_refs = refs[pos:pos + n_p]

        def f(*rp):
            return fn(*rp[:n_r], *a_vals, *rp[n_r:], *c_vals)

        _, vjp = jax.vjp(f, *r_vals, *p_vals)
        grads = vjp(tuple(t_vals))
        drows = list(grads[:n_r])
        if has_add:
            drows[0] = drows[0] + add_val
        if merge:
            off = 0
            for w, d in zip(widths, drows):
                dr_refs[0][:, off:off + w] = d.astype(dr_refs[0].dtype)
                off += w
        else:
            for r, d in zip(dr_refs, drows):
                r[...] = d.astype(r.dtype)
        i = pl.program_id(0)
        for r, d in zip(dp_refs, grads[n_r:]):
            @pl.when(i == 0)
            def _(r=r, d=d):
                r[...] = d

            @pl.when(i > 0)
            def _(r=r, d=d):
                r[...] += d

    if merge:
        dr_specs = [pl.BlockSpec((tl, sum(widths)), lambda i: (i, 0))]
        dr_shapes = [jax.ShapeDtypeStruct((length, sum(widths)), drow_dtypes[0])]
    else:
        dr_specs = [pl.BlockSpec((tl, w), lambda i: (i, 0)) for w in widths]
        dr_shapes = [jax.ShapeDtypeStruct((length, w), dt) for w, dt in zip(widths, drow_dtypes)]
    ins = [r.arr for r in rows + aux] + list(pars) + list(consts) + [r.arr for r in cots] + ([add] if has_add else [])
    specs = ([_row_spec(r, tl) for r in rows + aux] + [_whole_spec(p) for p in list(pars) + list(consts)]
             + [_row_spec(r, tl) for r in cots] + ([pl.BlockSpec((tl, add.shape[1]), lambda i: (i, 0))] if has_add else []))
    return pl.pallas_call(
        body, name=name, grid=(length // tl,), in_specs=specs,
        out_specs=dr_specs + [_whole_spec(p) for p in pars],
        out_shape=dr_shapes + [jax.ShapeDtypeStruct(p.shape, F32) for p in pars],
        compiler_params=_params(("arbitrary",)),
    )(*ins)


def whole_fwd(fn, ins, out_shapes, *, name):
    n_in = len(ins)

    def body(*refs):
        res = fn(*[r[...] for r in refs[:n_in]])
        for o_ref, v in zip(refs[n_in:], res, strict=True):
            o_ref[...] = v

    return pl.pallas_call(body, name=name, out_shape=[jax.ShapeDtypeStruct(s, F32) for s in out_shapes],
                          compiler_params=_params())(*ins)


def whole_bwd(fn, ins, n_diff, cots, *, name):
    n_in, n_t = len(ins), len(cots)

    def body(*refs):
        vals = [r[...] for r in refs[:n_in]]
        t_vals = [r[...] for r in refs[n_in:n_in + n_t]]
        _, vjp = jax.vjp(lambda *d: fn(*d, *vals[n_diff:]), *vals[:n_diff])
        for o_ref, g in zip(refs[n_in + n_t:], vjp(tuple(t_vals)), strict=True):
            o_ref[...] = g

    return pl.pallas_call(body, name=name, out_shape=[jax.ShapeDtypeStruct(a.shape, F32) for a in ins[:n_diff]],
                          compiler_params=_params())(*ins, *cots)


CONV_ROWS = 256


def _conv_geometry(x, w, cw, off):
    width = w.shape[1]
    x = Cols(x, width, 0)
    length = x.arr.shape[0]
    taps = w.shape[0]
    pad = -(-(taps - 1) // 8) * 8
    assert off % cw == 0 and width % cw == 0, (off, width, cw)
    return x, length, taps, pad, off // cw


def _conv_taps(xp_ref, w_ref, base, taps, pad, init, lanes=slice(None)):
    acc = init
    for k in range(taps):
        acc = acc + w_ref[k:k + 1, lanes] * xp_ref[pl.ds(base + pad - (taps - 1) + k, init.shape[0]), :]
    return acc


def conv_fwd(x, w, b, *, act, name, off=0, cw=LANE, out_dtype=F32):
    x, length, taps, pad, jb = _conv_geometry(x, w, cw, off)
    rc = min(CONV_ROWS, length)

    def body(x_ref, w_ref, b_ref, o_ref, xp_ref):
        xp_ref[0:pad, :] = jnp.zeros((pad, cw), F32)
        xp_ref[pad:pad + length, :] = x_ref[...].astype(F32)

        def chunk(r, carry):
            base = pl.multiple_of(r * rc, rc)
            acc = _conv_taps(xp_ref, w_ref, base, taps, pad, jnp.broadcast_to(b_ref[...], (rc, cw)))
            if act:
                acc = _silu(acc)
            o_ref[pl.ds(base, rc), :] = acc.astype(o_ref.dtype)
            return carry

        lax.fori_loop(0, length // rc, chunk, 0)

    return pl.pallas_call(
        body, name=name, grid=(x.w // cw,),
        in_specs=[pl.BlockSpec((length, cw), lambda j: (0, jb + j)), pl.BlockSpec((taps, cw), lambda j: (0, j)),
                  pl.BlockSpec((1, cw), lambda j: (0, j))],
        out_specs=pl.BlockSpec((length, cw), lambda j: (0, j)),
        out_shape=jax.ShapeDtypeStruct((length, x.w), out_dtype),
        scratch_shapes=[pltpu.VMEM((pad + length, cw), F32)],
        compiler_params=_params(("parallel",)),
    )(x.arr, w, b)


def conv_bwd(x, w, b, dy, *, act, name, off=0, cw=LANE, dx_dtype=F32):
    x, length, taps, pad, jb = _conv_geometry(x, w, cw, off)
    rc = min(CONV_ROWS, length)

    def body(x_ref, w_ref, b_ref, dy_ref, dx_ref, dw_ref, db_ref, xp_ref, gp_ref):
        xp_ref[0:pad, :] = jnp.zeros((pad, cw), F32)
        xp_ref[pad:pad + length, :] = x_ref[...].astype(F32)
        gp_ref[length:length + pad, :] = jnp.zeros((pad, cw), F32)
        if act:
            def pre_chunk(r, carry):
                base = pl.multiple_of(r * rc, rc)
                pre = _conv_taps(xp_ref, w_ref, base, taps, pad, jnp.broadcast_to(b_ref[...], (rc, cw)))
                sig = jax.nn.sigmoid(pre)
                gp_ref[pl.ds(base, rc), :] = dy_ref[pl.ds(base, rc), :].astype(F32) * (sig * (1.0 + pre * (1.0 - sig)))
                return carry

            lax.fori_loop(0, length // rc, pre_chunk, 0)
        else:
            gp_ref[0:length, :] = dy_ref[...].astype(F32)
        dw_ref[...] = jnp.zeros((taps, cw), F32)
        db_ref[...] = jnp.zeros((1, cw), F32)

        def chunk(r, carry):
            base = pl.multiple_of(r * rc, rc)
            acc = jnp.zeros((rc, cw), F32)
            g = gp_ref[pl.ds(base, rc), :]
            for k in range(taps):
                acc = acc + w_ref[k:k + 1, :] * gp_ref[pl.ds(base + (taps - 1) - k, rc), :]
                xs = xp_ref[pl.ds(base + pad - (taps - 1) + k, rc), :]
                dw_ref[k:k + 1, :] += jnp.sum(g * xs, axis=0, keepdims=True)
            db_ref[...] += jnp.sum(g, axis=0, keepdims=True)
            dx_ref[pl.ds(base, rc), :] = acc.astype(dx_ref.dtype)
            return carry

        lax.fori_loop(0, length // rc, chunk, 0)

    dy = _cols(dy)
    assert dy.j == 0 and dy.w == x.w
    return pl.pallas_call(
        body, name=name, grid=(x.w // cw,),
        in_specs=[pl.BlockSpec((length, cw), lambda j: (0, jb + j)), pl.BlockSpec((taps, cw), lambda j: (0, j)),
                  pl.BlockSpec((1, cw), lambda j: (0, j)), pl.BlockSpec((length, cw), lambda j: (0, j))],
        out_specs=[pl.BlockSpec((length, cw), lambda j: (0, j)), pl.BlockSpec((taps, cw), lambda j: (0, j)),
                   pl.BlockSpec((1, cw), lambda j: (0, j))],
        out_shape=[jax.ShapeDtypeStruct((length, x.w), dx_dtype), jax.ShapeDtypeStruct((taps, x.w), F32),
                   jax.ShapeDtypeStruct((1, x.w), F32)],
        scratch_shapes=[pltpu.VMEM((pad + length, cw), F32), pltpu.VMEM((length + pad, cw), F32)],
        compiler_params=_params(("parallel",)),
    )(x.arr, w, b, dy.arr)


def _conv_transpose(xp_ref, gp_ref, w_ref, dx_ref, dw_ref, db_ref, lanes, length, taps, pad, rc):
    dw_ref[:, lanes] = jnp.zeros((taps, LANE), F32)
    db_ref[:, lanes] = jnp.zeros((1, LANE), F32)

    def chunk(r, carry):
        base = pl.multiple_of(r * rc, rc)
        acc = jnp.zeros((rc, LANE), F32)
        g = gp_ref[pl.ds(base, rc), :]
        for k in range(taps):
            acc = acc + w_ref[k:k + 1, lanes] * gp_ref[pl.ds(base + (taps - 1) - k, rc), :]
            xs = xp_ref[pl.ds(base + pad - (taps - 1) + k, rc), :]
            dw_ref[k:k + 1, lanes] += jnp.sum(g * xs, axis=0, keepdims=True)
        db_ref[:, lanes] += jnp.sum(g, axis=0, keepdims=True)
        dx_ref[pl.ds(base, rc), lanes] = acc.astype(dx_ref.dtype)
        return carry

    lax.fori_loop(0, length // rc, chunk, 0)


LANE_PAIR_ROWS = 1024


def ffn_interleave(a, name=None):
    rows, width = a.shape
    nb = width // (2 * LANE)
    if rows < LANE_PAIR_ROWS:
        return a.reshape(rows, 2, nb, LANE).swapaxes(1, 2).reshape(a.shape)

    def body(g_ref, u_ref, o_ref):
        o_ref[:, 0:LANE] = g_ref[...]
        o_ref[:, LANE:2 * LANE] = u_ref[...]

    tr = LANE_PAIR_ROWS
    return pl.pallas_call(
        body, name=name, grid=(rows // tr, nb),
        in_specs=[pl.BlockSpec((tr, LANE), lambda i, j: (i, j)), pl.BlockSpec((tr, LANE), lambda i, j: (i, nb + j))],
        out_specs=pl.BlockSpec((tr, 2 * LANE), lambda i, j: (i, j)),
        out_shape=jax.ShapeDtypeStruct(a.shape, a.dtype), compiler_params=_params(("parallel", "parallel")),
    )(a, a)


def ffn_deinterleave(a):
    rows, width = a.shape
    return a.reshape(rows, width // (2 * LANE), 2, LANE).swapaxes(1, 2).reshape(a.shape)


PAIR_COPY_ROWS = 512


def ffn_pairs_to_shards(a, *, name):
    rows, width = a.shape
    cols = width // 4
    per = cols // LANE
    tr = min(rows, PAIR_COPY_ROWS)

    def body(a_ref, o_ref):
        is_up = pl.program_id(1) >= 2
        for parity, chosen in ((0, jnp.logical_not(is_up)), (1, is_up)):
            @pl.when(chosen)
            def _(parity=parity):
                for t in range(per):
                    o_ref[0, :, t * LANE:(t + 1) * LANE] = a_ref[:, (2 * t + parity) * LANE:(2 * t + parity + 1) * LANE]

    return pl.pallas_call(
        body, name=name, grid=(rows // tr, 4),
        in_specs=[pl.BlockSpec((tr, 2 * cols), lambda i, s: (i, jnp.where(s >= 2, s - 2, s)))],
        out_specs=pl.BlockSpec((1, tr, cols), lambda i, s: (s, i, 0)),
        out_shape=jax.ShapeDtypeStruct((4, rows, cols), a.dtype), compiler_params=_params(("parallel", "parallel")),
    )(a)


def ffn_shards_to_pairs(parts, *, name):
    _, rows, cols = parts.shape
    per = cols // LANE
    tr = min(rows, PAIR_COPY_ROWS)

    def body(gate_ref, up_ref, o_ref):
        for t in range(per):
            o_ref[:, 2 * t * LANE:(2 * t + 1) * LANE] = gate_ref[0, :, t * LANE:(t + 1) * LANE]
            o_ref[:, (2 * t + 1) * LANE:(2 * t + 2) * LANE] = up_ref[0, :, t * LANE:(t + 1) * LANE]

    return pl.pallas_call(
        body, name=name, grid=(rows // tr, 2),
        in_specs=[pl.BlockSpec((1, tr, cols), lambda i, j: (j, i, 0)), pl.BlockSpec((1, tr, cols), lambda i, j: (2 + j, i, 0))],
        out_specs=pl.BlockSpec((tr, 2 * cols), lambda i, j: (i, j)),
        out_shape=jax.ShapeDtypeStruct((rows, 4 * cols), parts.dtype), compiler_params=_params(("parallel", "parallel")),
    )(parts, parts)


GATE, UP = slice(0, LANE), slice(LANE, 2 * LANE)


def _ffn_geometry(a, w):
    length, width = a.shape
    taps = w.shape[0]
    return length, width, width // (2 * LANE), taps, -(-(taps - 1) // 8) * 8, min(CONV_ROWS, length)


def _ffn_pre(xg_ref, xu_ref, w_ref, b_ref, base, taps, pad, rc):
    gate = _conv_taps(xg_ref, w_ref, base, taps, pad, jnp.broadcast_to(b_ref[:, GATE], (rc, LANE)), GATE)
    up = _conv_taps(xu_ref, w_ref, base, taps, pad, jnp.broadcast_to(b_ref[:, UP], (rc, LANE)), UP)
    return gate, up


def ffn_conv_act(a, w, b, *, name):
    length, width, nb, taps, pad, rc = _ffn_geometry(a, w)

    def body(a_ref, w_ref, b_ref, o_ref, xg_ref, xu_ref):
        for xp_ref, lanes in ((xg_ref, GATE), (xu_ref, UP)):
            xp_ref[0:pad, :] = jnp.zeros((pad, LANE), F32)
            xp_ref[pad:pad + length, :] = a_ref[:, lanes].astype(F32)

        def chunk(r, carry):
            base = pl.multiple_of(r * rc, rc)
            gate, up = _ffn_pre(xg_ref, xu_ref, w_ref, b_ref, base, taps, pad, rc)
            o_ref[pl.ds(base, rc), :] = (_silu(gate) * up).astype(o_ref.dtype)
            return carry

        lax.fori_loop(0, length // rc, chunk, 0)

    pair = lambda rows: pl.BlockSpec((rows, 2 * LANE), lambda j: (0, j))
    return pl.pallas_call(
        body, name=name, grid=(nb,), in_specs=[pair(length), pair(taps), pair(1)],
        out_specs=pl.BlockSpec((length, LANE), lambda j: (0, j)),
        out_shape=jax.ShapeDtypeStruct((length, width // 2), BF16),
        scratch_shapes=[pltpu.VMEM((pad + length, LANE), F32), pltpu.VMEM((pad + length, LANE), F32)],
        compiler_params=_params(("parallel",)),
    )(a, w, b)


def ffn_conv_act_bwd(a, w, b, dact, *, name):
    length, width, nb, taps, pad, rc = _ffn_geometry(a, w)

    def body(a_ref, w_ref, b_ref, dy_ref, da_ref, dw_ref, db_ref, xg_ref, xu_ref, gg_ref, gu_ref):
        for xp_ref, lanes in ((xg_ref, GATE), (xu_ref, UP)):
            xp_ref[0:pad, :] = jnp.zeros((pad, LANE), F32)
            xp_ref[pad:pad + length, :] = a_ref[:, lanes].astype(F32)
        for gp_ref in (gg_ref, gu_ref):
            gp_ref[length:length + pad, :] = jnp.zeros((pad, LANE), F32)

        def pre_chunk(r, carry):
            base = pl.multiple_of(r * rc, rc)
            gate, up = _ffn_pre(xg_ref, xu_ref, w_ref, b_ref, base, taps, pad, rc)
            sig = jax.nn.sigmoid(gate)
            dy = dy_ref[pl.ds(base, rc), :]
            gg_ref[pl.ds(base, rc), :] = dy * up * (sig * (1.0 + gate * (1.0 - sig)))
            gu_ref[pl.ds(base, rc), :] = dy * (gate * sig)
            return carry

        lax.fori_loop(0, length // rc, pre_chunk, 0)
        _conv_transpose(xg_ref, gg_ref, w_ref, da_ref, dw_ref, db_ref, GATE, length, taps, pad, rc)
        _conv_transpose(xu_ref, gu_ref, w_ref, da_ref, dw_ref, db_ref, UP, length, taps, pad, rc)

    pair = lambda rows: pl.BlockSpec((rows, 2 * LANE), lambda j: (0, j))
    return pl.pallas_call(
        body, name=name, grid=(nb,),
        in_specs=[pair(length), pair(taps), pair(1), pl.BlockSpec((length, LANE), lambda j: (0, j))],
        out_specs=[pair(length), pair(taps), pair(1)],
        out_shape=[jax.ShapeDtypeStruct((length, width), BF16), jax.ShapeDtypeStruct((taps, width), F32),
                   jax.ShapeDtypeStruct((1, width), F32)],
        scratch_shapes=[pltpu.VMEM((pad + length, LANE), F32), pltpu.VMEM((pad + length, LANE), F32),
                        pltpu.VMEM((length + pad, LANE), F32), pltpu.VMEM((length + pad, LANE), F32)],
        compiler_params=_params(("parallel",)),
    )(a, w, b, dact)


def _retention_consts():
    h = np.arange(RET_HEADS, dtype=np.float32)
    log_g = np.log1p(-(2.0 ** (-5.0 - h))).astype(np.float32)
    idx = np.arange(CHUNK, dtype=np.float32)
    diff = idx[:, None] - idx[None, :]
    intra = np.where(diff[None] >= 0, np.exp(np.maximum(diff, 0.0)[None] * log_g[:, None, None]), 0.0)
    zeta = np.exp((CHUNK - 1 - idx)[None, :] * log_g[:, None])
    xi = np.exp((idx + 1)[None, :] * log_g[:, None])
    decay = np.exp(CHUNK * log_g)
    zeta = np.broadcast_to(zeta[:, :, None], (RET_HEADS, CHUNK, RET_DK))
    xi = np.broadcast_to(xi[:, :, None], (RET_HEADS, CHUNK, RET_DV))
    return (jnp.asarray(intra, F32), jnp.asarray(zeta, F32), jnp.asarray(xi, F32), [float(d) for d in decay])


def _rotary_tables(length):
    inv = ROPE_BASE ** (-jnp.arange(0, RET_DK, 2, dtype=F32) / RET_DK)
    ang = jnp.arange(length).astype(F32)[:, None] * inv[None, :]
    cos, sin = jnp.cos(ang), jnp.sin(ang)
    return jnp.concatenate([cos, cos], axis=1), jnp.concatenate([-sin, sin], axis=1)


def _rot(x, cos2, sin2):
    return x * cos2 + pltpu.roll(x, RET_DK // 2, 1) * sin2


def _rot_t(y, cos2, sin2):
    return y * cos2 + pltpu.roll(y * sin2, RET_DK // 2, 1)


def _ret_chunk(q, k, v, g, state, intra, zeta, xi, decay):
    s = _mm_nt(q, k) * intra
    kv = _mm_tn(k * zeta, v)
    o = _mm(s, v) + _mm(q, state) * xi
    oc = o - jnp.mean(o, axis=-1, keepdims=True)
    r = oc * lax.rsqrt(jnp.mean(oc * oc, axis=-1, keepdims=True) + EPS)
    return _silu(g) * r, state * decay + kv


RET_QK, RET_V = RET_HEADS * RET_DK, RET_HEADS * RET_DV


def _ret_specs(rev, nc):
    def cidx(c):
        return nc - 1 - c if rev else c
    whole = lambda shape: pl.BlockSpec(shape, lambda c: (0,) * len(shape))
    return [
        pl.BlockSpec((CHUNK, RET_QK), lambda c: (cidx(c), 0)),
        pl.BlockSpec((CHUNK, RET_QK), lambda c: (cidx(c), 1)),
        pl.BlockSpec((CHUNK, RET_V), lambda c: (cidx(c), 1)),
        pl.BlockSpec((CHUNK, RET_V), lambda c: (cidx(c), 2)),
        pl.BlockSpec((CHUNK, RET_DK), lambda c: (cidx(c), 0)),
        pl.BlockSpec((CHUNK, RET_DK), lambda c: (cidx(c), 0)),
        whole((RET_HEADS, CHUNK, CHUNK)), whole((RET_HEADS, CHUNK, RET_DK)), whole((RET_HEADS, CHUNK, RET_DV)),
    ], cidx


def _head(ref, h, width):
    return ref[:, h * width:(h + 1) * width].astype(F32)


def retention_fwd(proj, cos2, sin2):
    length = proj.shape[0]
    nc = length // CHUNK
    intra, zeta, xi, decays = _retention_consts()
    specs, _ = _ret_specs(False, nc)
    scale = RET_DK ** -0.5

    def body(q_ref, k_ref, v_ref, g_ref, cos_ref, sin_ref, intra_ref, zeta_ref, xi_ref, y_ref, st_ref, state):
        @pl.when(pl.program_id(0) == 0)
        def _():
            state[...] = jnp.zeros_like(state)

        cos2v, sin2v = cos_ref[...], sin_ref[...]
        for h in range(RET_HEADS):
            q = _rot(_head(q_ref, h, RET_DK), cos2v, sin2v)
            k = _rot(_head(k_ref, h, RET_DK), cos2v, sin2v) * scale
            st_ref[h, 0] = state[h]
            y, new_state = _ret_chunk(q, k, _head(v_ref, h, RET_DV), _head(g_ref, h, RET_DV), state[h],
                                      intra_ref[h], zeta_ref[h], xi_ref[h], decays[h])
            y_ref[:, h * RET_DV:(h + 1) * RET_DV] = y.astype(y_ref.dtype)
            state[h] = new_state

    return pl.pallas_call(
        body, name="retention_fwd", grid=(nc,), in_specs=specs,
        out_specs=[pl.BlockSpec((CHUNK, RET_V), lambda c: (c, 0)),
                   pl.BlockSpec((RET_HEADS, 1, RET_DK, RET_DV), lambda c: (0, c, 0, 0))],
        out_shape=[jax.ShapeDtypeStruct((length, RET_V), BF16),
                   jax.ShapeDtypeStruct((RET_HEADS, nc, RET_DK, RET_DV), F32)],
        scratch_shapes=[pltpu.VMEM((RET_HEADS, RET_DK, RET_DV), F32)],
        compiler_params=_params(("arbitrary",)),
    )(proj, proj, proj, proj, cos2, sin2, intra, zeta, xi)


def retention_bwd(proj, cos2, sin2, states, dmix):
    length = proj.shape[0]
    nc = length // CHUNK
    intra, zeta, xi, decays = _retention_consts()
    specs, cidx = _ret_specs(True, nc)
    scale = RET_DK ** -0.5

    def body(q_ref, k_ref, v_ref, g_ref, cos_ref, sin_ref, intra_ref, zeta_ref, xi_ref, st_ref, dy_ref,
             dq_ref, dk_ref, dv_ref, dg_ref, dstate):
        @pl.when(pl.program_id(0) == 0)
        def _():
            dstate[...] = jnp.zeros_like(dstate)

        cos2v, sin2v = cos_ref[...], sin_ref[...]
        for h in range(RET_HEADS):
            q = _rot(_head(q_ref, h, RET_DK), cos2v, sin2v)
            k = _rot(_head(k_ref, h, RET_DK), cos2v, sin2v) * scale
            intra_v, zeta_v, xi_v, decay = intra_ref[h], zeta_ref[h], xi_ref[h], decays[h]
            _, vjp = jax.vjp(lambda q, k, v, g, s: _ret_chunk(q, k, v, g, s, intra_v, zeta_v, xi_v, decay),
                             q, k, _head(v_ref, h, RET_DV), _head(g_ref, h, RET_DV), st_ref[h, 0])
            dq, dk, dv, dg, ds = vjp((_head(dy_ref, h, RET_DV).astype(F32), dstate[h]))
            dq_ref[:, h * RET_DK:(h + 1) * RET_DK] = _rot_t(dq, cos2v, sin2v).astype(dq_ref.dtype)
            dk_ref[:, h * RET_DK:(h + 1) * RET_DK] = _rot_t(dk * scale, cos2v, sin2v).astype(dk_ref.dtype)
            dv_ref[:, h * RET_DV:(h + 1) * RET_DV] = dv.astype(dv_ref.dtype)
            dg_ref[:, h * RET_DV:(h + 1) * RET_DV] = dg.astype(dg_ref.dtype)
            dstate[h] = ds

    specs = specs + [pl.BlockSpec((RET_HEADS, 1, RET_DK, RET_DV), lambda c: (0, cidx(c), 0, 0)),
                     pl.BlockSpec((CHUNK, RET_V), lambda c: (cidx(c), 0))]
    row = lambda width: pl.BlockSpec((CHUNK, width), lambda c: (cidx(c), 0))
    return pl.pallas_call(
        body, name="retention_bwd", grid=(nc,), in_specs=specs,
        out_specs=[row(RET_QK), row(RET_QK), row(RET_V), row(RET_V)],
        out_shape=[jax.ShapeDtypeStruct((length, RET_QK), BF16), jax.ShapeDtypeStruct((length, RET_QK), BF16),
                   jax.ShapeDtypeStruct((length, RET_V), BF16), jax.ShapeDtypeStruct((length, RET_V), BF16)],
        scratch_shapes=[pltpu.VMEM((RET_HEADS, RET_DK, RET_DV), F32)],
        compiler_params=_params(("arbitrary",)),
    )(proj, proj, proj, proj, cos2, sin2, intra, zeta, xi, states, dmix)


def _ssd_consts():
    tri = np.tril(np.ones((CHUNK, CHUNK), np.float32))
    expand = np.zeros((LANE, SSM_DINNER), np.float32)
    for h in range(SSM_HEADS):
        expand[h, h * SSM_P:(h + 1) * SSM_P] = 1.0
    return jnp.asarray(tri), jnp.asarray(tri.T.copy()), jnp.asarray(expand)


def _ssd_chunk(xs, bm, cm, dtr, z, state, dt_bias, a_log, d_skip, norm_w, tri, tri_t, expand):
    gw = SSM_DINNER // SSM_GROUPS
    dt = jax.nn.softplus(dtr + dt_bias)
    a_neg = -jnp.exp(a_log)
    da = dt * a_neg
    acs = _dot_hi(tri, da)
    acs_t = _dot_hi_tn(da, tri_t)
    dt_x = _times_01(dt, expand)
    a_x = jnp.mean(_dot_hi(jnp.broadcast_to(a_neg, (8, LANE)), expand), axis=0, keepdims=True)
    da_x = dt_x * a_x
    acs_x = _01_times(tri, da_x)
    tot_x = jnp.sum(da_x, axis=0, keepdims=True)
    x_dt = xs * dt_x
    x_dec = x_dt * jnp.exp(tot_x - acs_x)
    e_acs = jnp.exp(acs_x)
    e_tot = jnp.exp(tot_x)
    lane = lax.broadcasted_iota(jnp.int32, (CHUNK, LANE), 1)
    sub = lax.broadcasted_iota(jnp.int32, (CHUNK, LANE), 0)
    causal = sub >= lane
    ys, new_states = [], []
    for g in range(SSM_GROUPS):
        bg = bm[:, g * SSM_N:(g + 1) * SSM_N]
        cg = cm[:, g * SSM_N:(g + 1) * SSM_N]
        sg = state[:, g * gw:(g + 1) * gw]
        cb = _mm_nt(cg, bg)
        y_off = _mm(cg, sg) * e_acs[:, g * gw:(g + 1) * gw]
        new_states.append(sg * e_tot[:, g * gw:(g + 1) * gw] + _mm_tn(bg, x_dec[:, g * gw:(g + 1) * gw]))
        pairs = []
        for p in range(gw // LANE):
            hp = g * (gw // LANE) + p
            xp = x_dt[:, hp * LANE:(hp + 1) * LANE]
            halves = []
            for head in (2 * hp, 2 * hp + 1):
                col = jnp.sum(jnp.where(lane == head, acs, 0.0), axis=1, keepdims=True)
                row = jnp.sum(jnp.where(sub == head, acs_t, 0.0), axis=0, keepdims=True)
                decay = jnp.exp(jnp.where(causal, col - row, -1e30))
                halves.append(_mm(cb * decay, xp))
            pairs.append(jnp.where(lane < SSM_P, halves[0], halves[1]))
        ys.append(jnp.concatenate(pairs, axis=1) + y_off)
    d_x = jnp.mean(_dot_hi(jnp.broadcast_to(d_skip, (8, LANE)), expand), axis=0, keepdims=True)
    y = (jnp.concatenate(ys, axis=1) + d_x * xs) * _silu(z)
    normed = []
    for g in range(SSM_GROUPS):
        yg = y[:, g * gw:(g + 1) * gw]
        normed.append(yg * lax.rsqrt(jnp.mean(yg * yg, axis=-1, keepdims=True) + EPS))
    return jnp.concatenate(normed, axis=1) * norm_w, jnp.concatenate(new_states, axis=1)


XBC = SSM_DINNER + 2 * SSM_GROUPS * SSM_N


def _ssd_specs(rev, nc):
    def cidx(c):
        return nc - 1 - c if rev else c
    row = lambda w, j: pl.BlockSpec((CHUNK, w), lambda c: (cidx(c), j))
    whole = lambda shape: pl.BlockSpec(shape, lambda c: (0,) * len(shape))
    return [row(XBC, 0), row(LANE, 0), row(SSM_DINNER, 3),
            whole((1, LANE)), whole((1, LANE)), whole((1, LANE)), whole((1, SSM_DINNER)),
            whole((CHUNK, CHUNK)), whole((CHUNK, CHUNK)), whole((LANE, SSM_DINNER))], cidx


def ssd_fwd(xbc, dt_raw, proj, dt_bias, a_log, d_skip, norm_w):
    length = proj.shape[0]
    nc = length // CHUNK
    tri, tri_t, expand = _ssd_consts()
    specs, _ = _ssd_specs(False, nc)

    def body(xbc_ref, dt_ref, z_ref, dtb_ref, alog_ref, d_ref, nw_ref, tri_ref, trit_ref, e_ref, y_ref, st_ref, state):
        @pl.when(pl.program_id(0) == 0)
        def _():
            state[...] = jnp.zeros_like(state)

        st_ref[0] = state[...]
        y, new_state = _ssd_chunk(
            xbc_ref[:, 0:SSM_DINNER], xbc_ref[:, SSM_DINNER:SSM_DINNER + 256], xbc_ref[:, SSM_DINNER + 256:XBC],
            dt_ref[...], z_ref[...].astype(F32), state[...], dtb_ref[...], alog_ref[...], d_ref[...], nw_ref[...],
            tri_ref[...], trit_ref[...], e_ref[...])
        y_ref[...] = y.astype(y_ref.dtype)
        state[...] = new_state

    return pl.pallas_call(
        body, name="ssd_fwd", grid=(nc,), in_specs=specs,
        out_specs=[pl.BlockSpec((CHUNK, SSM_DINNER), lambda c: (c, 0)),
                   pl.BlockSpec((1, SSM_N, SSM_DINNER), lambda c: (c, 0, 0))],
        out_shape=[jax.ShapeDtypeStruct((length, SSM_DINNER), BF16),
                   jax.ShapeDtypeStruct((nc, SSM_N, SSM_DINNER), F32)],
        scratch_shapes=[pltpu.VMEM((SSM_N, SSM_DINNER), F32)],
        compiler_params=_params(("arbitrary",)),
    )(xbc, dt_raw, proj, dt_bias, a_log, d_skip, norm_w, tri, tri_t, expand)


def ssd_bwd(xbc, dt_raw, proj, dt_bias, a_log, d_skip, norm_w, states, dmix):
    length = proj.shape[0]
    nc = length // CHUNK
    tri, tri_t, expand = _ssd_consts()
    specs, cidx = _ssd_specs(True, nc)

    def body(xbc_ref, dt_ref, z_ref, dtb_ref, alog_ref, d_ref, nw_ref, tri_ref, trit_ref, e_ref, st_ref, dy_ref,
             dxbc_ref, ddt_ref, dz_ref, ddtb_ref, dalog_ref, dd_ref, dnw_ref, dstate):
        c = pl.program_id(0)

        @pl.when(c == 0)
        def _():
            dstate[...] = jnp.zeros_like(dstate)

        tri_v, trit_v, e_v = tri_ref[...], trit_ref[...], e_ref[...]
        _, vjp = jax.vjp(
            lambda *a: _ssd_chunk(*a, tri_v, trit_v, e_v),
            xbc_ref[:, 0:SSM_DINNER], xbc_ref[:, SSM_DINNER:SSM_DINNER + 256], xbc_ref[:, SSM_DINNER + 256:XBC],
            dt_ref[...], z_ref[...].astype(F32), st_ref[0], dtb_ref[...], alog_ref[...], d_ref[...], nw_ref[...])
        dxs, dbm, dcm, ddt, dz, ds, ddtb, dalog, dd, dnw = vjp((dy_ref[...].astype(F32), dstate[...]))
        dxbc_ref[:, 0:SSM_DINNER] = dxs
        dxbc_ref[:, SSM_DINNER:SSM_DINNER + 256] = dbm
        dxbc_ref[:, SSM_DINNER + 256:XBC] = dcm
        ddt_ref[...] = ddt.astype(ddt_ref.dtype)
        dz_ref[...] = dz.astype(dz_ref.dtype)
        dstate[...] = ds
        for r, d in ((ddtb_ref, ddtb), (dalog_ref, dalog), (dd_ref, dd), (dnw_ref, dnw)):
            @pl.when(c == 0)
            def _(r=r, d=d):
                r[...] = d

            @pl.when(c > 0)
            def _(r=r, d=d):
                r[...] += d

    whole = lambda shape: pl.BlockSpec(shape, lambda c: (0,) * len(shape))
    specs = specs + [pl.BlockSpec((1, SSM_N, SSM_DINNER), lambda c: (cidx(c), 0, 0)),
                     pl.BlockSpec((CHUNK, SSM_DINNER), lambda c: (cidx(c), 1))]
    return pl.pallas_call(
        body, name="ssd_bwd", grid=(nc,), in_specs=specs,
        out_specs=[pl.BlockSpec((CHUNK, XBC), lambda c: (cidx(c), 0)), pl.BlockSpec((CHUNK, LANE), lambda c: (cidx(c), 0)),
                   pl.BlockSpec((CHUNK, SSM_DINNER), lambda c: (cidx(c), 0)),
                   whole((1, LANE)), whole((1, LANE)), whole((1, LANE)), whole((1, SSM_DINNER))],
        out_shape=[jax.ShapeDtypeStruct((length, XBC), F32), jax.ShapeDtypeStruct((length, LANE), BF16),
                   jax.ShapeDtypeStruct((length, SSM_DINNER), BF16),
                   jax.ShapeDtypeStruct((1, LANE), F32), jax.ShapeDtypeStruct((1, LANE), F32),
                   jax.ShapeDtypeStruct((1, LANE), F32), jax.ShapeDtypeStruct((1, SSM_DINNER), F32)],
        scratch_shapes=[pltpu.VMEM((SSM_N, SSM_DINNER), F32)],
        compiler_params=_params(("arbitrary",)),
    )(xbc, dt_raw, proj, dt_bias, a_log, d_skip, norm_w, tri, tri_t, expand, states, dmix)


def _cmul(ar, ai, br, bi):
    return ar * br - ai * bi, ar * bi + ai * br


def s5_scan(b_re, b_im, a_re, a_im, *, reverse=False, states=None, name, lw=256):
    length, lanes = b_re.shape
    nk = length // SCAN_SEG
    with_da = states is not None
    assert reverse or not with_da

    def shift(v):
        sub = lax.broadcasted_iota(jnp.int32, v.shape, 0)
        if reverse:
            return jnp.where(sub == SCAN_SEG - 1, 0.0, pltpu.roll(v, SCAN_SEG - 1, 0))
        return jnp.where(sub == 0, 0.0, pltpu.roll(v, 1, 0))

    def body(*refs):
        if with_da:
            bre_ref, bim_ref, are_ref, aim_ref, sre_ref, sim_ref, xre_ref, xim_ref, dare_ref, daim_ref = refs
        else:
            bre_ref, bim_ref, are_ref, aim_ref, xre_ref, xim_ref = refs
        ar = jnp.broadcast_to(are_ref[...], (SCAN_SEG, lw))
        ai = jnp.broadcast_to(aim_ref[...], (SCAN_SEG, lw))

        def tile(i):
            k = (nk - 1 - i) if reverse else i
            return pl.ds(pl.multiple_of(k * SCAN_SEG, SCAN_SEG), SCAN_SEG)

        def local(i, carry):
            xr, xi, pr, pi = carry
            rows = tile(i)
            mr, mi = _cmul(ar, ai, xr, xi)
            xr, xi = mr + bre_ref[rows, :], mi + bim_ref[rows, :]
            xre_ref[rows, :] = xr
            xim_ref[rows, :] = xi
            pr, pi = _cmul(ar, ai, pr, pi)
            return xr, xi, pr, pi

        zero = jnp.zeros((SCAN_SEG, lw), F32)
        one = jnp.ones((SCAN_SEG, lw), F32)
        er, ei, pr, pi = lax.fori_loop(0, nk, local, (zero, zero, one, zero))
        cr, ci = zero, zero
        for _ in range(SCAN_SEG - 1):
            mr, mi = _cmul(pr, pi, cr, ci)
            cr, ci = shift(er + mr), shift(ei + mi)

        def fix(i, carry):
            pr, pi, dr, di = carry
            rows = tile(i)
            pr, pi = _cmul(ar, ai, pr, pi)
            mr, mi = _cmul(pr, pi, cr, ci)
            xr, xi = xre_ref[rows, :] + mr, xim_ref[rows, :] + mi
            xre_ref[rows, :] = xr
            xim_ref[rows, :] = xi
            if with_da:
                k = nk - 1 - i
                prev = pl.ds(pl.multiple_of(jnp.maximum(k - 1, 0) * SCAN_SEG, SCAN_SEG), SCAN_SEG)
                last = pl.ds((nk - 1) * SCAN_SEG, SCAN_SEG)
                sub = lax.broadcasted_iota(jnp.int32, (SCAN_SEG, lw), 0)
                wr = jnp.where(sub == 0, 0.0, pltpu.roll(sre_ref[last, :], 1, 0))
                wi = jnp.where(sub == 0, 0.0, pltpu.roll(sim_ref[last, :], 1, 0))
                sr = jnp.where(k == 0, wr, sre_ref[prev, :])
                si = jnp.where(k == 0, wi, sim_ref[prev, :])
                dr, di = dr + xr * sr + xi * si, di + xi * sr - xr * si
            return pr, pi, dr, di

        _, _, dr, di = lax.fori_loop(0, nk, fix, (one, zero, zero, zero))
        if with_da:
            dare_ref[...] = jnp.sum(dr, axis=0, keepdims=True)
            daim_ref[...] = jnp.sum(di, axis=0, keepdims=True)

    col = pl.BlockSpec((length, lw), lambda j: (0, j))
    vec = pl.BlockSpec((1, lw), lambda j: (0, j))
    ins = [b_re, b_im, a_re, a_im] + (list(states) if with_da else [])
    in_specs = [col, col, vec, vec] + ([col, col] if with_da else [])
    out_specs = [col, col] + ([vec, vec] if with_da else [])
    out_shape = [jax.ShapeDtypeStruct((length, lanes), F32)] * 2 + ([jax.ShapeDtypeStruct((1, lanes), F32)] * 2 if with_da else [])
    return pl.pallas_call(
        body, name=name, grid=(lanes // lw,), in_specs=in_specs, out_specs=out_specs, out_shape=out_shape,
        compiler_params=_params(("parallel",)),
    )(*ins)


def _seg_interleave(v):
    length = v.shape[0]
    return v.reshape(SCAN_SEG, length // SCAN_SEG, -1).transpose(1, 0, 2).reshape(length, -1)


def _seg_deinterleave(v):
    length = v.shape[0]
    return v.reshape(length // SCAN_SEG, SCAN_SEG, -1).transpose(1, 0, 2).reshape(length, -1)


def _block_diag(m):
    eye = jnp.eye(S5_GROUPS, dtype=m.dtype)
    return (m.reshape(S5_GROUPS, S5_GROUP, 1, S5_STATE) * eye[:, None, :, None]).reshape(S5_GROUPS * S5_GROUP, S5_LANES)


def _block_diag_take(full):
    idx = jnp.arange(S5_GROUPS)
    blocks = full.reshape(S5_GROUPS, S5_GROUP, S5_GROUPS, S5_STATE)[idx, :, idx, :]
    return blocks.reshape(S5_GROUPS * S5_GROUP, S5_STATE)


def _s5_prep(a_re, a_im, log_step, b_re, b_im, rep):
    step = jnp.exp(log_step)
    mag = jnp.exp(a_re * step)
    ab_re = mag * jnp.cos(a_im * step)
    ab_im = mag * jnp.sin(a_im * step)
    den = a_re * a_re + a_im * a_im
    f_re = ((ab_re - 1.0) * a_re + ab_im * a_im) / den
    f_im = (ab_im * a_re - (ab_re - 1.0) * a_im) / den
    fr, fi = _dot_hi(rep, f_re), _dot_hi(rep, f_im)
    return ab_re, ab_im, fr * b_re - fi * b_im, fr * b_im + fi * b_re


def _rms(x, g):
    return (x * lax.rsqrt(jnp.mean(x * x, axis=-1, keepdims=True) + EPS) * g,)


def _ffn_act(gate, up):
    return (_silu(gate) * up,)


def _glu(a, g):
    return (a * jax.nn.sigmoid(g),)


def _ln_silu(x, g, b):
    xc = x - jnp.mean(x, axis=-1, keepdims=True)
    var = jnp.mean(xc * xc, axis=-1, keepdims=True)
    return (_silu(xc * lax.rsqrt(var + EPS) * g + b),)


def _s5_post(y, u, d_skip, glu_w):
    s = jax.nn.gelu(y + d_skip * u)
    return (s * jax.nn.sigmoid(_mm(s, glu_w)),)


def loss_head(x, tgt, g, *, tl=512):
    length, d = x.shape
    tl = min(tl, length)

    def body(x_ref, t_ref, g_ref, loss_ref, dx_ref, dg_ref):
        i = pl.program_id(0)
        y, vjp = jax.vjp(lambda x, g: _rms(x, g)[0], x_ref[...], g_ref[...])
        err = y - t_ref[...]
        dx, dg = vjp(err * (1.0 / d))
        dx_ref[...] = dx
        part = jnp.broadcast_to(0.5 * jnp.sum(jnp.mean(err * err, axis=-1, keepdims=True), axis=0, keepdims=True), (1, LANE))

        @pl.when(i == 0)
        def _():
            loss_ref[...] = part
            dg_ref[...] = dg

        @pl.when(i > 0)
        def _():
            loss_ref[...] += part
            dg_ref[...] += dg

    row = pl.BlockSpec((tl, d), lambda i: (i, 0))
    return pl.pallas_call(
        body, name="loss_head", grid=(length // tl,),
        in_specs=[row, row, pl.BlockSpec((1, d), lambda i: (0, 0))],
        out_specs=[pl.BlockSpec((1, LANE), lambda i: (0, 0)), row, pl.BlockSpec((1, d), lambda i: (0, 0))],
        out_shape=[jax.ShapeDtypeStruct((1, LANE), F32), jax.ShapeDtypeStruct((length, d), F32),
                   jax.ShapeDtypeStruct((1, d), F32)],
        compiler_params=_params(("arbitrary",)),
    )(x, tgt, g)


def _pad_heads(v):
    return jnp.pad(v, ((0, 0), (0, LANE - v.shape[1])))


def local_step(x, tgt, w, late_weights=None, early_reduce=None):
    length = x.shape[0]
    cos2, sin2 = _rotary_tables(length)
    grads = {}
    w = dict(w)

    def rms_fwd(xin, g, name):
        return rowwise_fwd(_rms, [xin], [], [g], [], [(D_MODEL, BF16)], name=name, tl=512)[0]

    def rms_bwd(xin, g, dh, dxo, name):
        return rowwise_bwd(_rms, [xin], [], [g], [], [dh], [F32], name=name, tl=512, add=dxo)

    def ffn_fwd(i, xin):
        hf = rms_fwd(xin, w["ffn_norm"][i:i + 1], f"ffn{i}_norm")
        w_up = w["ffn_w_up_pairs"][i] if "ffn_w_up_pairs" in w else ffn_interleave(w["ffn_w_up"][i], name=f"ffn{i}_up_pairs")
        a = matmul(hf, w_up, out_dtype=BF16, name=f"ffn{i}_up")
        act = ffn_conv_act(a, ffn_interleave(w["ffn_dw_w"][i]), ffn_interleave(w["ffn_dw_b"][i:i + 1]),
                           name=f"ffn{i}_conv_act")
        return matmul(act, w["ffn_w_down"][i], res=xin, name=f"ffn{i}_down"), (hf, a, act, w_up)

    def ffn_bwd(i, xin, saved, dxo):
        hf, a, act, w_up = saved
        dact = matmul(dxo, w["ffn_w_down"][i], tb=True, name=f"ffn{i}_down_dx")
        dw_down = matmul(act, dxo, ta=True, name=f"ffn{i}_down_dw")
        da, ddw_w, ddw_b = ffn_conv_act_bwd(a, ffn_interleave(w["ffn_dw_w"][i]), ffn_interleave(w["ffn_dw_b"][i:i + 1]),
                                            dact, name=f"ffn{i}_conv_act_bwd")
        dw_up = ffn_pairs_to_shards(matmul(hf, da, ta=True, name=f"ffn{i}_up_dw"), name=f"ffn{i}_up_dw_shards")
        dhf = matmul(da, w_up, tb=True, name=f"ffn{i}_up_dx")
        dxin, dnorm = rms_bwd(xin, w["ffn_norm"][i:i + 1], dhf, dxo, f"ffn{i}_norm_bwd")
        return dxin, dict(ffn_norm=dnorm, ffn_w_up=dw_up, ffn_dw_w=ffn_deinterleave(ddw_w),
                          ffn_dw_b=ffn_deinterleave(ddw_b), ffn_w_down=dw_down)

    w_in_e = jnp.pad(w["e_w_in"][0], ((0, 0), (0, EVEN_IN_PAD - EVEN_IN)))
    conv_w_e, conv_b_e = w["e_conv_w"][0], w["e_conv_b"]
    dt_bias, a_log, d_skip = _pad_heads(w["e_dt_bias"]), _pad_heads(w["e_a_log"]), _pad_heads(w["e_d"])
    xbc_off = 4 * D_MODEL

    hn0 = rms_fwd(x, w["mix_norm"][0:1], "mix0_norm")
    proj0 = matmul(hn0, w_in_e, out_dtype=BF16, name="even_in")
    dt_raw = matmul(hn0, w_in_e[:, EVEN_IN_PAD - LANE:], name="even_in_dt")
    y_ret, ret_states = retention_fwd(proj0, cos2, sin2)
    xbc = conv_fwd(proj0, conv_w_e, conv_b_e, act=True, off=xbc_off, name="ssd_conv")
    y_ssm, ssd_states = ssd_fwd(xbc, dt_raw, proj0, dt_bias, a_log, d_skip, w["e_ssm_norm"])
    mix0 = jnp.concatenate([y_ret, y_ssm], axis=1)
    if late_weights is not None:
        w.update(late_weights(y_ssm))
    w_out_e = w["e_w_out"][0]
    x1 = matmul(mix0, w_out_e, res=x, name="even_out")
    x2, ffn0_saved = ffn_fwd(0, x1)

    w_in_o, w_out_o, glu_w = w["o_w_in"][0], w["o_w_out"][0], w["o_glu_w"][0]
    dw_w_o, dw_b_o, ln_g, ln_b, d_o = w["o_dw_w"][0], w["o_dw_b"], w["o_ln_g"], w["o_ln_b"], w["o_d"]
    rep = jnp.asarray(np.repeat(np.eye(S5_GROUPS, dtype=np.float32), S5_GROUP, axis=0))
    rows_gc = (S5_GROUPS * S5_GROUP, S5_STATE)
    prep_in = [w["o_a_re"][0], w["o_a_im"][0], w["o_log_step"].reshape(S5_GROUPS, 1),
               w["o_b_re"][0].transpose(0, 2, 1).reshape(rows_gc), w["o_b_im"][0].transpose(0, 2, 1).reshape(rows_gc), rep]
    ab_re, ab_im, bb_re, bb_im = whole_fwd(
        _s5_prep, prep_in, [(S5_GROUPS, S5_STATE)] * 2 + [rows_gc] * 2, name="s5_prep")
    a_re_row, a_im_row = ab_re.reshape(1, S5_LANES), ab_im.reshape(1, S5_LANES)
    b_re_bd, b_im_bd = _block_diag(bb_re).astype(BF16), _block_diag(bb_im).astype(BF16)
    c_re_bd = _block_diag(w["o_c_re"][0].reshape(rows_gc)).astype(BF16)
    c_im_neg_bd = _block_diag(-w["o_c_im"][0].reshape(rows_gc)).astype(BF16)

    hn1 = rms_fwd(x2, w["mix_norm"][1:2], "mix1_norm")
    proj1 = matmul(hn1, w_in_o, name="odd_in")
    half = D_MODEL // 2
    c_glu = rowwise_fwd(_glu, [Cols(proj1, half, 0), Cols(proj1, half, 1)], [], [], [], [(half, F32)],
                        name="conf_glu", tl=512)[0]
    c_conv = conv_fwd(c_glu, dw_w_o, dw_b_o, act=False, name="conf_conv")
    c_out = rowwise_fwd(_ln_silu, [c_conv], [], [ln_g, ln_b], [], [(half, BF16)], name="conf_ln", tl=512)[0]
    u_seg = _seg_interleave(proj1[:, 2 * half:])
    bu_re = matmul(u_seg, b_re_bd, name="s5_bu_re")
    bu_im = matmul(u_seg, b_im_bd, name="s5_bu_im")
    xs_re, xs_im = s5_scan(bu_re, bu_im, a_re_row, a_im_row, name="s5_scan")
    y_im = matmul(xs_im, c_im_neg_bd, tb=True, name="s5_y_im")
    y_s5 = _seg_deinterleave(matmul(xs_re, c_re_bd, tb=True, res=y_im, name="s5_y_re"))
    s_out = rowwise_fwd(_s5_post, [y_s5, Cols(proj1, half, 2)], [], [d_o, glu_w], [], [(half, BF16)],
                        name="s5_post", tl=512)[0]
    mix1 = jnp.concatenate([c_out, s_out], axis=1)
    x3 = matmul(mix1, w_out_o, res=x2, name="odd_out")
    x4, ffn1_saved = ffn_fwd(1, x3)

    loss, dx4, dfinal = loss_head(x4, tgt, w["final_norm"].reshape(1, D_MODEL))
    grads["final_norm"] = dfinal.reshape(D_MODEL)

    dx3, g_ffn1 = ffn_bwd(1, x3, ffn1_saved, dx4)
    dmix1 = matmul(dx3, w_out_o, tb=True, name="odd_out_dx")
    grads["o_w_out"] = [matmul(mix1, dx3, ta=True, name="odd_out_dw")]
    dc_conv, dln_g, dln_b = rowwise_bwd(_ln_silu, [c_conv], [], [ln_g, ln_b], [], [Cols(dmix1, half, 0)], [F32],
                                        name="conf_ln_bwd", tl=512)
    dc_glu, ddw_w_o, ddw_b_o = conv_bwd(c_glu, dw_w_o, dw_b_o, dc_conv, act=False, name="conf_conv_bwd")
    d_cacg = rowwise_bwd(_glu, [Cols(proj1, half, 0), Cols(proj1, half, 1)], [], [], [], [dc_glu], [BF16],
                         name="conf_glu_bwd", tl=512, merge=True)[0]
    dy_s5, du_post, dd_o, dglu_w = rowwise_bwd(
        _s5_post, [y_s5, Cols(proj1, half, 2)], [], [d_o, glu_w], [], [Cols(dmix1, half, 1)], [F32, F32],
        name="s5_post_bwd", tl=512)
    dy_seg = _seg_interleave(dy_s5)
    dxs_re = matmul(dy_seg, c_re_bd, name="s5_dx_re")
    dxs_im = matmul(dy_seg, c_im_neg_bd, name="s5_dx_im")
    dc_re_bd = matmul(dy_seg, xs_re, ta=True, name="s5_dc_re")
    dc_im_neg_bd = matmul(dy_seg, xs_im, ta=True, name="s5_dc_im")
    g_re, g_im, dab_re, dab_im = s5_scan(dxs_re, dxs_im, a_re_row, -a_im_row, reverse=True, states=(xs_re, xs_im),
                                         name="s5_scan_bwd", lw=LANE)
    dbb_re = _block_diag_take(matmul(u_seg, g_re, ta=True, name="s5_db_re"))
    dbb_im = _block_diag_take(matmul(u_seg, g_im, ta=True, name="s5_db_im"))
    du_im = matmul(g_im, b_im_bd, tb=True, name="s5_du_im")
    du = _seg_deinterleave(matmul(g_re, b_re_bd, tb=True, res=du_im, name="s5_du_re")) + du_post
    da_re, da_im, dlog_step, db_re, db_im = whole_bwd(
        _s5_prep, prep_in, 5,
        [dab_re.reshape(S5_GROUPS, S5_STATE), dab_im.reshape(S5_GROUPS, S5_STATE), dbb_re, dbb_im], name="s5_prep_bwd")
    gcn = (S5_GROUPS, S5_GROUP, S5_STATE)
    grads.update(
        o_a_re=da_re[None], o_a_im=da_im[None], o_log_step=dlog_step.reshape(1, S5_GROUPS),
        o_b_re=db_re.reshape(gcn).transpose(0, 2, 1)[None], o_b_im=db_im.reshape(gcn).transpose(0, 2, 1)[None],
        o_c_re=_block_diag_take(dc_re_bd).reshape(gcn)[None], o_c_im=-_block_diag_take(dc_im_neg_bd).reshape(gcn)[None],
        o_d=dd_o, o_glu_w=[dglu_w], o_dw_w=ddw_w_o[None], o_dw_b=ddw_b_o, o_ln_g=dln_g, o_ln_b=dln_b)
    dproj1 = jnp.concatenate([d_cacg, du.astype(BF16)], axis=1)
    grads["o_w_in"] = [matmul(hn1, dproj1, ta=True, name="odd_in_dw")]
    dhn1 = matmul(dproj1, w_in_o, tb=True, name="odd_in_dx")
    dx2, dmix_norm1 = rms_bwd(x2, w["mix_norm"][1:2], dhn1, dx3, "mix1_norm_bwd")

    if early_reduce is not None:
        zero = early_reduce({("o_w_in", 0): grads["o_w_in"][0], ("o_glu_w", 0): grads["o_glu_w"][0],
                             ("o_w_out", 0): grads["o_w_out"][0], ("ffn_w_up", 1): g_ffn1["ffn_w_up"],
                             ("ffn_w_down", 1): g_ffn1["ffn_w_down"]})
        w["ffn_dw_b"] = w["ffn_dw_b"] + zero
    dx1, g_ffn0 = ffn_bwd(0, x1, ffn0_saved, dx2)
    if early_reduce is not None:
        dt_bias = dt_bias + early_reduce({("ffn_w_up", 0): g_ffn0["ffn_w_up"], ("ffn_w_down", 0): g_ffn0["ffn_w_down"]})
    for k in g_ffn0:
        per_layer = [g_ffn0[k], g_ffn1[k]]
        grads[k] = per_layer if k in ("ffn_w_up", "ffn_w_down") else jnp.stack(per_layer).reshape(w[k].shape)
    dmix0 = matmul(dx1, w_out_e, tb=True, name="even_out_dx")
    grads["e_w_out"] = [matmul(mix0, dx1, ta=True, name="even_out_dw")]
    if early_reduce is not None:
        a_log = a_log + early_reduce({("e_w_out", 0): grads["e_w_out"][0]})
    dq, dk, dv, dg = retention_bwd(proj0, cos2, sin2, ret_states, dmix0)
    dxbc_c, ddt, dz, ddt_bias, da_log, dd_skip, dssm_norm = ssd_bwd(
        xbc, dt_raw, proj0, dt_bias, a_log, d_skip, w["e_ssm_norm"], ssd_states, dmix0)
    dxbc, dconv_w, dconv_b = conv_bwd(proj0, conv_w_e, conv_b_e, dxbc_c, act=True, off=xbc_off,
                                      name="ssd_conv_bwd", dx_dtype=BF16)
    dproj0 = jnp.concatenate([dq, dk, dv, dg, dz, dxbc, ddt], axis=1)
    grads["e_w_in"] = [matmul(hn0, dproj0, ta=True, name="even_in_dw")[:, :EVEN_IN]]
    dhn0 = matmul(dproj0, w_in_e, tb=True, name="even_in_dx")
    dx, dmix_norm0 = rms_bwd(x, w["mix_norm"][0:1], dhn0, dx1, "mix0_norm_bwd")
    grads.update(
        mix_norm=jnp.concatenate([dmix_norm0, dmix_norm1], axis=0), e_conv_w=dconv_w[None], e_conv_b=dconv_b,
        e_dt_bias=ddt_bias[:, :SSM_HEADS], e_a_log=da_log[:, :SSM_HEADS], e_d=dd_skip[:, :SSM_HEADS],
        e_ssm_norm=dssm_norm)
    return loss, dx, grads


def adamw(w, g, m, v, *, name):
    shape = w.shape
    cols = shape[-1]
    rows = w.size // cols
    tr = _tile(rows, max(8, (512 * 1024 // cols) // 8 * 8), unit=8)

    def body(w_ref, g_ref, m_ref, v_ref, d_ref, nm_ref, nv_ref):
        gv = g_ref[...]
        nm = ADAM_B1 * m_ref[...] + (1.0 - ADAM_B1) * gv
        nv = ADAM_B2 * v_ref[...] + (1.0 - ADAM_B2) * jnp.square(gv)
        m_hat = nm / (1.0 - ADAM_B1 ** ADAM_STEP)
        v_hat = nv / (1.0 - ADAM_B2 ** ADAM_STEP)
        d_ref[...] = -ADAM_LR * (m_hat / (jnp.sqrt(v_hat) + ADAM_EPS) + ADAM_WD * w_ref[...])
        nm_ref[...] = nm
        nv_ref[...] = nv

    spec = pl.BlockSpec((tr, cols), lambda i: (i, 0))
    outs = pl.pallas_call(
        body, name=name, grid=(rows // tr,), in_specs=[spec] * 4, out_specs=[spec] * 3,
        out_shape=[jax.ShapeDtypeStruct((rows, cols), F32)] * 3, compiler_params=_params(("parallel",)),
    )(*[t.reshape(rows, cols) for t in (w, g, m, v)])
    return [o.reshape(shape) for o in outs]


OTHER_CHIPS = ((1, 0), (0, 1), (1, 1))
ANY = pl.BlockSpec(memory_space=pl.ANY)


def _position():
    return lax.axis_index("x"), lax.axis_index("y"), lax.axis_index("c")


def _flip(v, f):
    return 1 - v if f else v


def _remote(src, dst, send_sem, recv_sem, device):
    return pltpu.make_async_remote_copy(src_ref=src, dst_ref=dst, send_sem=send_sem, recv_sem=recv_sem,
                                        device_id=device, device_id_type=MESH)


def gather_shards(big, small):
    n_big, n_small = len(big), len(small)
    halves = [a.shape[0] // 2 for a in big]

    def body(*refs):
        big_refs, small_refs = refs[:n_big], refs[n_big:n_big + n_small]
        obig_refs = refs[n_big + n_small:2 * n_big + n_small]
        osmall_refs = refs[2 * n_big + n_small:2 * (n_big + n_small)]
        ici_send, ici_recv, d2d_send, d2d_recv, small_send, small_recv = refs[2 * (n_big + n_small):]
        x, y, c = _position()
        mine = 2 * x + y

        def half(k, core):
            return pl.ds(pl.multiple_of(core * halves[k], 16), halves[k])

        sends = []
        for j, (fx, fy) in enumerate(OTHER_CHIPS):
            peer = (_flip(x, fx), _flip(y, fy), c)
            for k in range(n_big):
                sends.append(_remote(big_refs[k].at[half(k, c)], obig_refs[k].at[mine, half(k, c)],
                                     ici_send.at[j, k], ici_recv.at[j, k], peer))
            for k in range(n_small):
                sends.append(_remote(small_refs[k], osmall_refs[k].at[mine], small_send.at[j, k], small_recv.at[j, k], peer))
        for cp in sends:
            cp.start()
        for j, (fx, fy) in enumerate(OTHER_CHIPS):
            px, py = _flip(x, fx), _flip(y, fy)
            src_chip = 2 * px + py
            for k in range(n_big):
                landed = obig_refs[k].at[src_chip, half(k, c)]
                _remote(landed, landed, ici_send.at[j, k], ici_recv.at[j, k], (px, py, c)).wait_recv()
                fwd = _remote(landed, landed, d2d_send.at[j, k], d2d_recv.at[j, k], (x, y, 1 - c))
                fwd.start()
                sends.append(fwd)
        for j, (fx, fy) in enumerate(OTHER_CHIPS):
            px, py = _flip(x, fx), _flip(y, fy)
            src_chip = 2 * px + py
            for k in range(n_big):
                other = obig_refs[k].at[src_chip, half(k, 1 - c)]
                _remote(other, other, d2d_send.at[j, k], d2d_recv.at[j, k], (x, y, 1 - c)).wait_recv()
            for k in range(n_small):
                dst = osmall_refs[k].at[src_chip]
                _remote(small_refs[k], dst, small_send.at[j, k], small_recv.at[j, k], (px, py, c)).wait_recv()
        for cp in sends:
            cp.wait_send()

    arrays = list(big) + list(small)
    dma = pltpu.SemaphoreType.DMA
    return pl.pallas_call(
        body, name="gather_shards", in_specs=[ANY] * len(arrays), out_specs=[ANY] * len(arrays),
        out_shape=[jax.ShapeDtypeStruct((4,) + a.shape, a.dtype) for a in arrays],
        scratch_shapes=[dma((3, n_big)), dma((3, n_big)), dma((3, n_big)), dma((3, n_big)),
                        dma((3, n_small)), dma((3, n_small))],
        compiler_params=_params(),
    )(*arrays)


def allreduce_small(pack):
    rows = pack.shape[0]
    half = rows // 2

    def body(p_ref, o_ref, sibling_pack, chip_sum, chip_halves, total, sems):
        x, y, c = _position()
        sibling = (x, y, 1 - c)
        swap = _remote(p_ref, sibling_pack, sems.at[0, 0], sems.at[1, 0], sibling)
        swap.start()
        swap.wait()
        chip_sum[...] = p_ref[...] + sibling_pack[...]
        mine = pl.ds(pl.multiple_of(c * half, 8), half)
        other = pl.ds(pl.multiple_of((1 - c) * half, 8), half)
        chip = 2 * x + y
        chip_halves[chip] = chip_sum[mine, :]
        sends = []
        for j, (fx, fy) in enumerate(OTHER_CHIPS):
            sends.append(_remote(chip_sum.at[mine], chip_halves.at[chip], sems.at[0, 1 + j], sems.at[1, 1 + j],
                                 (_flip(x, fx), _flip(y, fy), c)))
        for cp in sends:
            cp.start()
        for j, (fx, fy) in enumerate(OTHER_CHIPS):
            px, py = _flip(x, fx), _flip(y, fy)
            _remote(chip_sum.at[mine], chip_halves.at[2 * px + py], sems.at[0, 1 + j], sems.at[1, 1 + j], (px, py, c)).wait_recv()
        for cp in sends:
            cp.wait_send()
        total[...] = ((chip_halves[0] + chip_halves[1]) + chip_halves[2]) + chip_halves[3]
        o_ref[mine, :] = total[...]
        share = _remote(total, o_ref.at[mine], sems.at[0, 4], sems.at[1, 4], sibling)
        share.start()
        _remote(total, o_ref.at[other], sems.at[0, 4], sems.at[1, 4], sibling).wait_recv()
        share.wait_send()

    vmem = pl.BlockSpec(memory_space=pltpu.VMEM)
    return pl.pallas_call(
        body, name="allreduce_small", in_specs=[vmem], out_specs=vmem,
        out_shape=jax.ShapeDtypeStruct(pack.shape, F32),
        scratch_shapes=[pltpu.VMEM((rows, LANE), F32), pltpu.VMEM((rows, LANE), F32), pltpu.VMEM((4, half, LANE), F32),
                        pltpu.VMEM((half, LANE), F32), pltpu.SemaphoreType.DMA((2, 5))],
        compiler_params=_params(),
    )(pack)


def exchange_halves(gs, *, name):
    n = len(gs)

    def body(*refs):
        g_refs, o_refs, (send_sems, recv_sems) = refs[:n], refs[n:2 * n], refs[2 * n:]
        x, y, c = _position()
        copies = [_remote(g_refs[k].at[:, 1 - c], o_refs[k], send_sems.at[k], recv_sems.at[k], (x, y, 1 - c)) for k in range(n)]
        for cp in copies:
            cp.start()
        for cp in copies:
            cp.wait()

    return pl.pallas_call(
        body, name=name, in_specs=[ANY] * n, out_specs=[ANY] * n,
        out_shape=[jax.ShapeDtypeStruct((4,) + g.shape[2:], g.dtype) for g in gs],
        scratch_shapes=[pltpu.SemaphoreType.DMA((n,)), pltpu.SemaphoreType.DMA((n,))],
        compiler_params=_params(),
    )(*gs)


def scatter_to_chips(parts):
    n = len(parts)

    def body(*refs):
        a_refs, o_refs, (send_sems, recv_sems) = refs[:n], refs[n:2 * n], refs[2 * n:]
        x, y, c = _position()
        copies = []
        for j, (fx, fy) in enumerate(OTHER_CHIPS):
            px, py = _flip(x, fx), _flip(y, fy)
            for k in range(n):
                copies.append(_remote(a_refs[k].at[2 * px + py], o_refs[k].at[j], send_sems.at[j, k], recv_sems.at[j, k], (px, py, c)))
        for cp in copies:
            cp.start()
        for cp in copies:
            cp.wait()

    return pl.pallas_call(
        body, name="scatter_to_chips", in_specs=[ANY] * n, out_specs=[ANY] * n,
        out_shape=[jax.ShapeDtypeStruct((3,) + a.shape[1:], a.dtype) for a in parts],
        scratch_shapes=[pltpu.SemaphoreType.DMA((3, n)), pltpu.SemaphoreType.DMA((3, n))],
        compiler_params=_params(),
    )(*parts)


def swap_halves(rs):
    n = len(rs)

    def body(*refs):
        r_refs, o_refs, (send_sems, recv_sems) = refs[:n], refs[n:2 * n], refs[2 * n:]
        x, y, c = _position()
        copies = [_remote(r_refs[k], o_refs[k], send_sems.at[k], recv_sems.at[k], (x, y, 1 - c)) for k in range(n)]
        for cp in copies:
            cp.start()
        for cp in copies:
            cp.wait()

    dma = pltpu.SemaphoreType.DMA
    return pl.pallas_call(
        body, name="swap_halves", in_specs=[ANY] * n, out_specs=[ANY] * n,
        out_shape=[jax.ShapeDtypeStruct(r.shape, r.dtype) for r in rs],
        scratch_shapes=[dma((n,)), dma((n,))],
        compiler_params=_params(),
    )(*rs)


HBM = pl.BlockSpec(memory_space=pltpu.HBM)
SEM = pl.BlockSpec(memory_space=pltpu.SEMAPHORE)
SIDE_EFFECT = pltpu.SideEffectType.DATAFLOW_SIDE_EFFECTING


def _gather_plan(halves):
    def plan(v_refs, land_refs, x, y, c):
        copies = []
        for fx, fy in OTHER_CHIPS:
            for k in range(len(v_refs)):
                rows = pl.ds(pl.multiple_of(c * halves[k], 16), halves[k])
                copies.append((v_refs[k].at[rows], land_refs[k].at[2 * x + y, rows], (_flip(x, fx), _flip(y, fy), c)))
        return copies
    return plan


def _scatter_plan(v_refs, land_refs, x, y, c):
    copies = []
    for j, (fx, fy) in enumerate(OTHER_CHIPS):
        px, py = _flip(x, fx), _flip(y, fy)
        for k in range(len(v_refs)):
            copies.append((v_refs[k].at[2 * px + py], land_refs[k].at[j], (px, py, c)))
    return copies


def chip_exchange_start(srcs, land_shapes, plan, after, *, name):
    n = len(srcs)
    n_cp = 3 * n

    def body(*refs):
        v_refs, land_refs = refs[:n], refs[n:2 * n]
        outs = refs[2 * n + 1:]
        sends, recvs, token = outs[:n_cp], outs[n_cp:2 * n_cp], outs[-1]
        x, y, c = _position()
        for (src, dst, device), send, recv in zip(plan(v_refs, land_refs, x, y, c), sends, recvs, strict=True):
            _remote(src, dst, send, recv, device).start()
        token[...] = jnp.zeros_like(token)

    lands = [lax.empty(shape, v.dtype) for shape, v in zip(land_shapes, srcs)]
    arrays = [pltpu.with_memory_space_constraint(a, pltpu.HBM) for a in list(srcs) + lands]
    outs = pl.pallas_call(
        body, name=name,
        out_shape=tuple(pltpu.SemaphoreType.DMA(()) for _ in range(2 * n_cp))
        + tuple(pltpu.HBM(a.shape, a.dtype) for a in arrays) + (jax.ShapeDtypeStruct((8, LANE), F32),),
        in_specs=[HBM] * (2 * n) + [ANY],
        out_specs=(SEM,) * (2 * n_cp) + (HBM,) * (2 * n) + (pl.BlockSpec(memory_space=pltpu.VMEM),),
        input_output_aliases={i: 2 * n_cp + i for i in range(2 * n)},
        compiler_params=pltpu.CompilerParams(has_side_effects=SIDE_EFFECT),
    )(*arrays, after)
    handle = (outs[:n_cp], outs[n_cp:2 * n_cp], outs[2 * n_cp:2 * n_cp + n], outs[2 * n_cp + n:2 * n_cp + 2 * n])
    return handle, outs[-1]


def chip_exchange_wait(handle, plan, after, *, name):
    sends, recvs, v_thru, land_thru = handle
    n = len(v_thru)
    n_cp = 3 * n

    def body(*refs):
        v_refs, land_refs = refs[:n], refs[n:2 * n]
        sends, recvs = refs[2 * n:2 * n + n_cp], refs[2 * n + n_cp:2 * n + 2 * n_cp]
        x, y, c = _position()
        for (src, dst, device), send, recv in zip(plan(v_refs, land_refs, x, y, c), sends, recvs, strict=True):
            copy = _remote(src, dst, send, recv, device)
            copy.wait_send()
            copy.wait_recv()

    outs = pl.pallas_call(
        body, name=name,
        out_shape=tuple(pltpu.HBM(a.shape, a.dtype) for a in list(v_thru) + list(land_thru)),
        in_specs=[HBM] * (2 * n) + [SEM] * (2 * n_cp) + [ANY], out_specs=(HBM,) * (2 * n),
        input_output_aliases={i: i for i in range(2 * n)},
        compiler_params=pltpu.CompilerParams(has_side_effects=SIDE_EFFECT),
    )(*v_thru, *land_thru, *sends, *recvs, after)
    return outs[:n], outs[n:]


def finish_gather(lands):
    n = len(lands)
    halves = [a.shape[1] // 2 for a in lands]

    def body(*refs):
        o_refs, (send_sems, recv_sems) = refs[n:2 * n], refs[2 * n:]
        x, y, c = _position()

        def half(k, core):
            return pl.ds(pl.multiple_of(core * halves[k], 16), halves[k])

        sends = []
        for j, (fx, fy) in enumerate(OTHER_CHIPS):
            src_chip = 2 * _flip(x, fx) + _flip(y, fy)
            for k in range(n):
                held = o_refs[k].at[src_chip, half(k, c)]
                sends.append(_remote(held, held, send_sems.at[j, k], recv_sems.at[j, k], (x, y, 1 - c)))
        for cp in sends:
            cp.start()
        for j, (fx, fy) in enumerate(OTHER_CHIPS):
            src_chip = 2 * _flip(x, fx) + _flip(y, fy)
            for k in range(n):
                other = o_refs[k].at[src_chip, half(k, 1 - c)]
                _remote(other, other, send_sems.at[j, k], recv_sems.at[j, k], (x, y, 1 - c)).wait_recv()
        for cp in sends:
            cp.wait_send()

    dma = pltpu.SemaphoreType.DMA
    return pl.pallas_call(
        body, name="finish_gather", in_specs=[ANY] * n, out_specs=[ANY] * n,
        out_shape=[jax.ShapeDtypeStruct(a.shape, a.dtype) for a in lands],
        input_output_aliases={k: k for k in range(n)},
        scratch_shapes=[dma((3, n)), dma((3, n))],
        compiler_params=_params(),
    )(*lands)


def add_own_half(g, r, c_idx, *, name):
    _, _, h, cols = g.shape

    def body(c_ref, g_ref, r_ref, o_ref):
        o_ref[...] = (g_ref[0] + r_ref[...]).astype(o_ref.dtype)

    return pl.pallas_call(
        body, name=name,
        grid_spec=pltpu.PrefetchScalarGridSpec(
            num_scalar_prefetch=1, grid=(4,),
            in_specs=[pl.BlockSpec((1, 1, h, cols), lambda s, c: (s, c[0], 0, 0)),
                      pl.BlockSpec((1, h, cols), lambda s, c: (s, 0, 0))],
            out_specs=pl.BlockSpec((1, h, cols), lambda s, c: (s, 0, 0))),
        out_shape=jax.ShapeDtypeStruct(r.shape, BF16), compiler_params=_params(("parallel",)),
    )(c_idx, g, r)


def add_chip_parts(a, parts, chip_idx, *, name):
    _, h, cols = a.shape
    th = h // 2

    def body(s_ref, a_ref, p0_ref, p1_ref, p2_ref, o_ref):
        f = lambda r: r[0].astype(F32)
        o_ref[...] = ((f(a_ref) + f(p0_ref)) + f(p1_ref)) + f(p2_ref)

    part = lambda j: pl.BlockSpec((1, th, cols), lambda i, s, j=j: (j, i, 0))
    return pl.pallas_call(
        body, name=name,
        grid_spec=pltpu.PrefetchScalarGridSpec(
            num_scalar_prefetch=1, grid=(2,),
            in_specs=[pl.BlockSpec((1, th, cols), lambda i, s: (s[0], i, 0)), part(0), part(1), part(2)],
            out_specs=pl.BlockSpec((th, cols), lambda i, s: (i, 0))),
        out_shape=jax.ShapeDtypeStruct((h, cols), F32), compiler_params=_params(("parallel",)),
    )(chip_idx, a, parts, parts, parts)


WEIGHTS = ("mix_norm", "e_w_in", "e_conv_w", "e_conv_b", "e_dt_bias", "e_a_log", "e_d", "e_ssm_norm", "e_w_out",
           "o_w_in", "o_dw_w", "o_dw_b", "o_ln_g", "o_ln_b", "o_a_re", "o_a_im", "o_b_re", "o_b_im", "o_c_re",
           "o_c_im", "o_d", "o_log_step", "o_glu_w", "o_w_out", "ffn_norm", "ffn_w_up", "ffn_dw_w", "ffn_dw_b",
           "ffn_w_down", "final_norm")
BIG = (("e_w_in", 2), ("e_w_out", 1), ("o_w_in", 2), ("o_glu_w", 1), ("o_w_out", 1), ("ffn_w_up", 2), ("ffn_w_down", 1))
SMALL_SHARDED = (("e_conv_w", 2), ("o_dw_w", 2), ("o_dw_b", 1), ("o_ln_g", 1), ("o_ln_b", 1), ("o_d", 1), ("ffn_dw_w", 2))
REPLICATED = tuple(n for n in WEIGHTS if n not in dict(BIG + SMALL_SHARDED))
PACK_ROWS = 8


def _pack(arrays, dtype, row_unit=PACK_ROWS):
    flat = jnp.concatenate([a.astype(dtype).reshape(-1) for a in arrays])
    rows = -(-flat.size // (LANE * row_unit)) * row_unit
    return jnp.pad(flat, (0, rows * LANE - flat.size)).reshape(rows, LANE)


def _unpack(flat, shapes, lead=()):
    out, off = [], 0
    for shape in shapes:
        size = int(np.prod(shape))
        out.append(flat[..., off:off + size].reshape(lead + tuple(shape)))
        off += size
    return out


def _join_shards(parts, axis):
    return jnp.concatenate([parts[s] for s in range(4)], axis=axis)


def _split_shards(full, axis):
    return jnp.stack(jnp.split(full, 4, axis=axis))


def _rows2d(a):
    return a.reshape(-1, a.shape[-1])


def _layer_shards(g, axis):
    if g.ndim == 3:
        return g.reshape(4, 2, g.shape[1] // 2, g.shape[2])
    rows, cols = g.shape
    if axis == 0:
        return g.reshape(4, 2, rows // 8, cols)
    return g.reshape(rows, 4, cols // 4).transpose(1, 0, 2).reshape(4, 2, rows // 2, cols // 4)


def kernel(x, mix_norm, e_w_in, e_conv_w, e_conv_b, e_dt_bias, e_a_log, e_d, e_ssm_norm, e_w_out, o_w_in, o_dw_w, o_dw_b, o_ln_g, o_ln_b, o_a_re, o_a_im, o_b_re, o_b_im, o_c_re, o_c_im, o_d, o_log_step, o_glu_w, o_w_out, ffn_norm, ffn_w_up, ffn_dw_w, ffn_dw_b, ffn_w_down, final_norm, loss_target, m_mix_norm, m_e_w_in, m_e_conv_w, m_e_conv_b, m_e_dt_bias, m_e_a_log, m_e_d, m_e_ssm_norm, m_e_w_out, m_o_w_in, m_o_dw_w, m_o_dw_b, m_o_ln_g, m_o_ln_b, m_o_a_re, m_o_a_im, m_o_b_re, m_o_b_im, m_o_c_re, m_o_c_im, m_o_d, m_o_log_step, m_o_glu_w, m_o_w_out, m_ffn_norm, m_ffn_w_up, m_ffn_dw_w, m_ffn_dw_b, m_ffn_w_down, m_final_norm, v_mix_norm, v_e_w_in, v_e_conv_w, v_e_conv_b, v_e_dt_bias, v_e_a_log, v_e_d, v_e_ssm_norm, v_e_w_out, v_o_w_in, v_o_dw_w, v_o_dw_b, v_o_ln_g, v_o_ln_b, v_o_a_re, v_o_a_im, v_o_b_re, v_o_b_im, v_o_c_re, v_o_c_im, v_o_d, v_o_log_step, v_o_glu_w, v_o_w_out, v_ffn_norm, v_ffn_w_up, v_ffn_dw_w, v_ffn_dw_b, v_ffn_w_down, v_final_norm):
    given = dict(locals())
    chip = 2 * lax.axis_index("x") + lax.axis_index("y")
    core = lax.axis_index("c")

    core_idx, chip_idx = core.reshape(1).astype(jnp.int32), chip.reshape(1).astype(jnp.int32)

    def whole(n, axis, parts):
        shape = given[n].shape
        own = given[n].astype(parts.dtype)
        return _join_shards(lax.dynamic_update_index_in_dim(parts.reshape((4,) + shape), own, chip, 0), axis)

    first, later = BIG[:1], BIG[1:]
    shards = {n: _rows2d(given[n]).astype(BF16) for n, _ in BIG}
    gathered = gather_shards([shards[n] for n, _ in first], [_rows2d(given[n]) for n, _ in SMALL_SHARDED])
    w = {n: given[n] for n in REPLICATED}
    for (n, axis), parts in zip(first + SMALL_SHARDED, gathered):
        w[n] = whole(n, axis, parts)
    later_keys = [(n, layer) for n, _ in later for layer in (range(2) if n.startswith("ffn_") else [None])]
    later_shards = [shards[n] if layer is None else given[n][layer].astype(BF16) for n, layer in later_keys]
    gather_plan = _gather_plan([a.shape[0] // 2 for a in later_shards])
    gather_handle, token = chip_exchange_start(later_shards, [(4,) + a.shape for a in later_shards], gather_plan,
                                               gathered[0], name="gather_start")
    w["mix_norm"] = w["mix_norm"] + token[0, 0]

    def late_weights(after):
        _, lands = chip_exchange_wait(gather_handle, gather_plan, after, name="gather_wait")
        out = {"ffn_w_up_pairs": [], "ffn_w_down": []}
        for (n, layer), own, parts in zip(later_keys, later_shards, finish_gather(lands)):
            if layer is None:
                out[n] = whole(n, dict(BIG)[n], parts)
                continue
            parts = lax.dynamic_update_index_in_dim(parts, own, chip, 0)
            if n == "ffn_w_up":
                out["ffn_w_up_pairs"].append(ffn_shards_to_pairs(parts, name=f"ffn{layer}_up_pairs"))
            else:
                out[n].append(parts.reshape(-1, parts.shape[-1]))
        return out

    groups = []

    def finish_group(after):
        group = groups[-1]
        group["sums"], group["parts"] = chip_exchange_wait(group.pop("handle"), _scatter_plan, after,
                                                           name=f"scatter_wait_{len(groups) - 1}")

    def early_reduce(layer_grads):
        keys = list(layer_grads)
        if groups:
            finish_group(layer_grads[keys[0]])
        tag = len(groups)
        parts = [_layer_shards(layer_grads[k], dict(BIG)[k[0]] - 1) for k in keys]
        sums = [add_own_half(g, r, core_idx, name=f"add_own_half_{n}{layer}")
                for g, r, (n, layer) in zip(parts, exchange_halves(parts, name=f"exchange_halves_{tag}"), keys)]
        handle, zeros = chip_exchange_start(sums, [(3,) + a.shape[1:] for a in sums], _scatter_plan, sums[0],
                                            name=f"scatter_start_{tag}")
        groups.append(dict(keys=keys, handle=handle))
        return zeros[0, 0]

    loss, dx, grads = local_step(x[0], loss_target[0], w, late_weights, early_reduce)
    finish_group(dx)
    early_keys = [k for group in groups for k in group["keys"]]
    early_sums = [a for group in groups for a in group["sums"]]
    early_parts = [a for group in groups for a in group["parts"]]

    small_names = REPLICATED + tuple(n for n, _ in SMALL_SHARDED)
    small_sum = allreduce_small(_pack([grads[n] for n in small_names], F32, row_unit=16))
    reduced = dict(zip(small_names, _unpack(small_sum.reshape(-1), [grads[n].shape for n in small_names])))
    for n, axis in SMALL_SHARDED:
        width = given[n].shape[axis]
        reduced[n] = lax.dynamic_slice_in_dim(reduced[n], chip * width, width, axis=axis)

    keys, parts = [], []
    for n, axis in BIG:
        for layer, g in enumerate(grads[n]):
            if (n, layer) not in early_keys:
                keys.append((n, layer))
                parts.append(_layer_shards(g, axis - 1))
    core_sums = [add_own_half(g, r, core_idx, name=f"add_own_half_{n}{layer}")
                 for g, r, (n, layer) in zip(parts, exchange_halves(parts, name="exchange_halves_last"), keys)]
    chip_parts = scatter_to_chips(core_sums)
    keys, core_sums, chip_parts = early_keys + keys, early_sums + core_sums, early_parts + list(chip_parts)
    mine = [add_chip_parts(a, p, chip_idx, name=f"add_chip_parts_{n}{layer}")
            for a, p, (n, layer) in zip(core_sums, chip_parts, keys)]
    layers = {}
    for (n, layer), own, other in zip(keys, mine, swap_halves(mine)):
        both = jnp.where(core == 0, jnp.stack([own, other]), jnp.stack([other, own]))
        layers.setdefault(n, {})[layer] = both.reshape(given[n].shape[1:])
    for n, _ in BIG:
        reduced[n] = jnp.stack([layers[n][layer] for layer in sorted(layers[n])])

    delta, new_m, new_v = {}, {}, {}
    for n in WEIGHTS:
        delta[n], new_m[n], new_v[n] = adamw(given[n], reduced[n], given["m_" + n], given["v_" + n], name="adamw_" + n)

    total = lax.psum(loss[0, 0], ("x", "y", "c"))
    return (total, dx[None], *[reduced[n] for n in WEIGHTS], *[delta[n] for n in WEIGHTS],
            *[new_m[n] for n in WEIGHTS], *[new_v[n] for n in WEIGHTS])
```

```python
import functools
import math
from typing import NamedTuple

import numpy as np
import jax
import jax.numpy as jnp
from jax import lax
from jax.experimental import pallas as pl
from jax.experimental.pallas import tpu as pltpu

F32 = jnp.float32
BF16 = jnp.bfloat16
HIGHEST = lax.Precision.HIGHEST
MESH = pl.DeviceIdType.MESH

D_MODEL = 1024
EPS = 1e-6
RET_HEADS, RET_DK, RET_DV, CHUNK = 4, 128, 256, 128
ROPE_BASE = 10000.0
SSM_HEADS, SSM_P, SSM_N, SSM_GROUPS = 16, 64, 128, 2
SSM_DINNER = SSM_HEADS * SSM_P
EVEN_IN, EVEN_IN_PAD = 5648, 5760
S5_GROUPS, S5_GROUP, S5_STATE = 32, 16, 64
S5_LANES = S5_GROUPS * S5_STATE
SCAN_SEG = 32
D_FF = 2816
ADAM_LR, ADAM_B1, ADAM_B2, ADAM_EPS, ADAM_WD, ADAM_STEP = 0.001, 0.9, 0.999, 1e-08, 0.01, 10

LANE = 128
VMEM_LIMIT = 56 * 1024 * 1024


def _params(sem=None, **kw):
    return pltpu.CompilerParams(dimension_semantics=sem, vmem_limit_bytes=VMEM_LIMIT, **kw)


def _tile(n, target, unit=LANE):
    if n <= target:
        return n
    t = (target // unit) * unit
    while t >= unit:
        if n % t == 0:
            return t
        t -= unit
    return n


def _silu(x):
    return x * jax.nn.sigmoid(x)


def _mm(a, b):
    return jnp.dot(a.astype(BF16), b.astype(BF16), preferred_element_type=F32)


def _mm_nt(a, b):
    return lax.dot_general(a.astype(BF16), b.astype(BF16), (((1,), (1,)), ((), ())), preferred_element_type=F32)


def _mm_tn(a, b):
    return lax.dot_general(a.astype(BF16), b.astype(BF16), (((0,), (0,)), ((), ())), preferred_element_type=F32)


def _dot_hi(a, b):
    return jnp.dot(a, b, precision=HIGHEST, preferred_element_type=F32)


def _dot_hi_tn(a, b):
    return lax.dot_general(a, b, (((0,), (0,)), ((), ())), precision=HIGHEST, preferred_element_type=F32)


def _bf16_parts(v):
    hi = v.astype(BF16)
    rest = v - hi.astype(F32)
    mid = rest.astype(BF16)
    return hi, mid, (rest - mid.astype(F32)).astype(BF16)


def _dot_parts(v, fixed, dims, v_first):
    fixed = fixed.astype(BF16)
    out = None
    for part in _bf16_parts(v):
        ops = (part, fixed) if v_first else (fixed, part)
        p = lax.dot_general(*ops, (dims, ((), ())), preferred_element_type=F32)
        out = p if out is None else out + p
    return out


@jax.custom_vjp
def _times_01(v, ones):
    return _dot_parts(v, ones, ((1,), (0,)), True)


_times_01.defvjp(lambda v, ones: (_times_01(v, ones), ones),
                 lambda ones, g: (_dot_parts(g, ones, ((1,), (1,)), True), jnp.zeros_like(ones)))


@jax.custom_vjp
def _01_times(ones, v):
    return _dot_parts(v, ones, ((1,), (0,)), False)


_01_times.defvjp(lambda ones, v: (_01_times(ones, v), ones),
                 lambda ones, g: (jnp.zeros_like(ones), _dot_parts(g, ones, ((0,), (0,)), False)))


MATMUL_VMEM = 44 * 1024 * 1024


def matmul(a, b, *, ta=False, tb=False, res=None, out_dtype=F32, name):
    m, k = (a.shape[1], a.shape[0]) if ta else a.shape
    n = b.shape[0] if tb else b.shape[1]
    assert (b.shape[1] if tb else b.shape[0]) == k, (a.shape, b.shape, ta, tb)
    tm = _tile(m, 1536)
    tn = _tile(n, 640)
    if tn < 384:
        tn = _tile(n, 1536)
    res_bytes = 0 if res is None else res.dtype.itemsize

    def vmem(tm, tn):
        return 2 * (tm * k * a.dtype.itemsize + tn * k * b.dtype.itemsize + tm * tn * (jnp.dtype(out_dtype).itemsize + res_bytes))

    while vmem(tm, tn) > MATMUL_VMEM and tm % (2 * LANE) == 0:
        tm //= 2
    assert vmem(tm, tn) <= MATMUL_VMEM, (name, tm, tn, k)
    a_spec = pl.BlockSpec((k, tm), lambda i, j: (0, i)) if ta else pl.BlockSpec((tm, k), lambda i, j: (i, 0))
    b_spec = pl.BlockSpec((tn, k), lambda i, j: (j, 0)) if tb else pl.BlockSpec((k, tn), lambda i, j: (0, j))
    o_spec = pl.BlockSpec((tm, tn), lambda i, j: (i, j))
    dims = (((0 if ta else 1,), (1 if tb else 0,)), ((), ()))
    has_res = res is not None

    def body(a_ref, b_ref, *rest):
        o_ref = rest[-1]
        out = lax.dot_general(a_ref[...].astype(BF16), b_ref[...].astype(BF16), dims, preferred_element_type=F32)
        if has_res:
            out = out + rest[0][...].astype(F32)
        o_ref[...] = out.astype(o_ref.dtype)

    ins = [a, b] + ([res] if has_res else [])
    specs = [a_spec, b_spec] + ([o_spec] if has_res else [])
    return pl.pallas_call(
        body, name=name, grid=(m // tm, n // tn), in_specs=specs, out_specs=o_spec,
        out_shape=jax.ShapeDtypeStruct((m, n), out_dtype), compiler_params=_params(("parallel", "parallel")),
    )(*ins)


class Cols(NamedTuple):
    arr: jax.Array
    w: int
    j: int


def _cols(a):
    return a if isinstance(a, Cols) else Cols(a, a.shape[1], 0)


def _row_spec(c, tl):
    return pl.BlockSpec((tl, c.w), lambda i, j=c.j: (i, j))


def _whole_spec(p):
    return pl.BlockSpec(p.shape, lambda i, nd=p.ndim: (0,) * nd)


def rowwise_fwd(fn, rows, aux, pars, consts, outs, *, name, tl):
    rows = [_cols(r) for r in rows + aux]
    whole = list(pars) + list(consts)
    n_rows = len(rows)
    n_whole = len(whole)
    length = rows[0].arr.shape[0]
    tl = min(tl, length)

    def body(*refs):
        vals = [r[...].astype(F32) for r in refs[:n_rows]] + [r[...] for r in refs[n_rows:n_rows + n_whole]]
        res = fn(*vals)
        for o_ref, v in zip(refs[n_rows + n_whole:], res, strict=True):
            o_ref[...] = v.astype(o_ref.dtype)

    return pl.pallas_call(
        body, name=name, grid=(length // tl,),
        in_specs=[_row_spec(r, tl) for r in rows] + [_whole_spec(p) for p in whole],
        out_specs=[pl.BlockSpec((tl, w), lambda i: (i, 0)) for w, _ in outs],
        out_shape=[jax.ShapeDtypeStruct((length, w), dt) for w, dt in outs],
        compiler_params=_params(("parallel",)),
    )(*[r.arr for r in rows], *whole)


def rowwise_bwd(fn, rows, aux, pars, consts, cots, drow_dtypes, *, name, tl, add=None, merge=False):
    rows = [_cols(r) for r in rows]
    aux = [_cols(r) for r in aux]
    cots = [_cols(r) for r in cots]
    n_r, n_a, n_p, n_c, n_t = len(rows), len(aux), len(pars), len(consts), len(cots)
    length = rows[0].arr.shape[0]
    tl = min(tl, length)
    has_add = add is not None
    widths = [r.w for r in rows]

    def body(*refs):
        pos = 0
        r_vals = [r[...].astype(F32) for r in refs[pos:pos + n_r]]; pos += n_r
        a_vals = [r[...].astype(F32) for r in refs[pos:pos + n_a]]; pos += n_a
        p_vals = [r[...].astype(F32) for r in refs[pos:pos + n_p]]; pos += n_p
        c_vals = [r[...] for r in refs[pos:pos + n_c]]; pos += n_c
        t_vals = [r[...].astype(F32) for r in refs[pos:pos + n_t]]; pos += n_t
        add_val = None
        if has_add:
            add_val = refs[pos][...].astype(F32); pos += 1
        n_dr = 1 if merge else n_r
        dr_refs = refs[pos:pos + n_dr]; pos += n_dr
        dp_refs = refs[pos:pos + n_p]

        def f(*rp):
            return fn(*rp[:n_r], *a_vals, *rp[n_r:], *c_vals)

        _, vjp = jax.vjp(f, *r_vals, *p_vals)
        grads = vjp(tuple(t_vals))
        drows = list(grads[:n_r])
        if has_add:
            drows[0] = drows[0] + add_val
        if merge:
            off = 0
            for w, d in zip(widths, drows):
                dr_refs[0][:, off:off + w] = d.astype(dr_refs[0].dtype)
                off += w
        else:
            for r, d in zip(dr_refs, drows):
                r[...] = d.astype(r.dtype)
        i = pl.program_id(0)
        for r, d in zip(dp_refs, grads[n_r:]):
            @pl.when(i == 0)
            def _(r=r, d=d):
                r[...] = d

            @pl.when(i > 0)
            def _(r=r, d=d):
                r[...] += d

    if merge:
        dr_specs = [pl.BlockSpec((tl, sum(widths)), lambda i: (i, 0))]
        dr_shapes = [jax.ShapeDtypeStruct((length, sum(widths)), drow_dtypes[0])]
    else:
        dr_specs = [pl.BlockSpec((tl, w), lambda i: (i, 0)) for w in widths]
        dr_shapes = [jax.ShapeDtypeStruct((length, w), dt) for w, dt in zip(widths, drow_dtypes)]
    ins = [r.arr for r in rows + aux] + list(pars) + list(consts) + [r.arr for r in cots] + ([add] if has_add else [])
    specs = ([_row_spec(r, tl) for r in rows + aux] + [_whole_spec(p) for p in list(pars) + list(consts)]
             + [_row_spec(r, tl) for r in cots] + ([pl.BlockSpec((tl, add.shape[1]), lambda i: (i, 0))] if has_add else []))
    return pl.pallas_call(
        body, name=name, grid=(length // tl,), in_specs=specs,
        out_specs=dr_specs + [_whole_spec(p) for p in pars],
        out_shape=dr_shapes + [jax.ShapeDtypeStruct(p.shape, F32) for p in pars],
        compiler_params=_params(("arbitrary",)),
    )(*ins)


def whole_fwd(fn, ins, out_shapes, *, name):
    n_in = len(ins)

    def body(*refs):
        res = fn(*[r[...] for r in refs[:n_in]])
        for o_ref, v in zip(refs[n_in:], res, strict=True):
            o_ref[...] = v

    return pl.pallas_call(body, name=name, out_shape=[jax.ShapeDtypeStruct(s, F32) for s in out_shapes],
                          compiler_params=_params())(*ins)


def whole_bwd(fn, ins, n_diff, cots, *, name):
    n_in, n_t = len(ins), len(cots)

    def body(*refs):
        vals = [r[...] for r in refs[:n_in]]
        t_vals = [r[...] for r in refs[n_in:n_in + n_t]]
        _, vjp = jax.vjp(lambda *d: fn(*d, *vals[n_diff:]), *vals[:n_diff])
        for o_ref, g in zip(refs[n_in + n_t:], vjp(tuple(t_vals)), strict=True):
            o_ref[...] = g

    return pl.pallas_call(body, name=name, out_shape=[jax.ShapeDtypeStruct(a.shape, F32) for a in ins[:n_diff]],
                          compiler_params=_params())(*ins, *cots)


CONV_ROWS = 256


def _conv_geometry(x, w, cw, off):
    width = w.shape[1]
    x = Cols(x, width, 0)
    length = x.arr.shape[0]
    taps = w.shape[0]
    pad = -(-(taps - 1) // 8) * 8
    assert off % cw == 0 and width % cw == 0, (off, width, cw)
    return x, length, taps, pad, off // cw


def _conv_taps(xp_ref, w_ref, base, taps, pad, init, lanes=slice(None)):
    acc = init
    for k in range(taps):
        acc = acc + w_ref[k:k + 1, lanes] * xp_ref[pl.ds(base + pad - (taps - 1) + k, init.shape[0]), :]
    return acc


def conv_fwd(x, w, b, *, act, name, off=0, cw=LANE, out_dtype=F32):
    x, length, taps, pad, jb = _conv_geometry(x, w, cw, off)
    rc = min(CONV_ROWS, length)

    def body(x_ref, w_ref, b_ref, o_ref, xp_ref):
        xp_ref[0:pad, :] = jnp.zeros((pad, cw), F32)
        xp_ref[pad:pad + length, :] = x_ref[...].astype(F32)

        def chunk(r, carry):
            base = pl.multiple_of(r * rc, rc)
            acc = _conv_taps(xp_ref, w_ref, base, taps, pad, jnp.broadcast_to(b_ref[...], (rc, cw)))
            if act:
                acc = _silu(acc)
            o_ref[pl.ds(base, rc), :] = acc.astype(o_ref.dtype)
            return carry

        lax.fori_loop(0, length // rc, chunk, 0)

    return pl.pallas_call(
        body, name=name, grid=(x.w // cw,),
        in_specs=[pl.BlockSpec((length, cw), lambda j: (0, jb + j)), pl.BlockSpec((taps, cw), lambda j: (0, j)),
                  pl.BlockSpec((1, cw), lambda j: (0, j))],
        out_specs=pl.BlockSpec((length, cw), lambda j: (0, j)),
        out_shape=jax.ShapeDtypeStruct((length, x.w), out_dtype),
        scratch_shapes=[pltpu.VMEM((pad + length, cw), F32)],
        compiler_params=_params(("parallel",)),
    )(x.arr, w, b)


def conv_bwd(x, w, b, dy, *, act, name, off=0, cw=LANE, dx_dtype=F32):
    x, length, taps, pad, jb = _conv_geometry(x, w, cw, off)
    rc = min(CONV_ROWS, length)

    def body(x_ref, w_ref, b_ref, dy_ref, dx_ref, dw_ref, db_ref, xp_ref, gp_ref):
        xp_ref[0:pad, :] = jnp.zeros((pad, cw), F32)
        xp_ref[pad:pad + length, :] = x_ref[...].astype(F32)
        gp_ref[length:length + pad, :] = jnp.zeros((pad, cw), F32)
        if act:
            def pre_chunk(r, carry):
                base = pl.multiple_of(r * rc, rc)
                pre = _conv_taps(xp_ref, w_ref, base, taps, pad, jnp.broadcast_to(b_ref[...], (rc, cw)))
                sig = jax.nn.sigmoid(pre)
                gp_ref[pl.ds(base, rc), :] = dy_ref[pl.ds(base, rc), :].astype(F32) * (sig * (1.0 + pre * (1.0 - sig)))
                return carry

            lax.fori_loop(0, length // rc, pre_chunk, 0)
        else:
            gp_ref[0:length, :] = dy_ref[...].astype(F32)
        dw_ref[...] = jnp.zeros((taps, cw), F32)
        db_ref[...] = jnp.zeros((1, cw), F32)

        def chunk(r, carry):
            base = pl.multiple_of(r * rc, rc)
            acc = jnp.zeros((rc, cw), F32)
            g = gp_ref[pl.ds(base, rc), :]
            for k in range(taps):
                acc = acc + w_ref[k:k + 1, :] * gp_ref[pl.ds(base + (taps - 1) - k, rc), :]
                xs = xp_ref[pl.ds(base + pad - (taps - 1) + k, rc), :]
                dw_ref[k:k + 1, :] += jnp.sum(g * xs, axis=0, keepdims=True)
            db_ref[...] += jnp.sum(g, axis=0, keepdims=True)
            dx_ref[pl.ds(base, rc), :] = acc.astype(dx_ref.dtype)
            return carry

        lax.fori_loop(0, length // rc, chunk, 0)

    dy = _cols(dy)
    assert dy.j == 0 and dy.w == x.w
    return pl.pallas_call(
        body, name=name, grid=(x.w // cw,),
        in_specs=[pl.BlockSpec((length, cw), lambda j: (0, jb + j)), pl.BlockSpec((taps, cw), lambda j: (0, j)),
                  pl.BlockSpec((1, cw), lambda j: (0, j)), pl.BlockSpec((length, cw), lambda j: (0, j))],
        out_specs=[pl.BlockSpec((length, cw), lambda j: (0, j)), pl.BlockSpec((taps, cw), lambda j: (0, j)),
                   pl.BlockSpec((1, cw), lambda j: (0, j))],
        out_shape=[jax.ShapeDtypeStruct((length, x.w), dx_dtype), jax.ShapeDtypeStruct((taps, x.w), F32),
                   jax.ShapeDtypeStruct((1, x.w), F32)],
        scratch_shapes=[pltpu.VMEM((pad + length, cw), F32), pltpu.VMEM((length + pad, cw), F32)],
        compiler_params=_params(("parallel",)),
    )(x.arr, w, b, dy.arr)


def _conv_transpose(xp_ref, gp_ref, w_ref, dx_ref, dw_ref, db_ref, lanes, length, taps, pad, rc):
    dw_ref[:, lanes] = jnp.zeros((taps, LANE), F32)
    db_ref[:, lanes] = jnp.zeros((1, LANE), F32)

    def chunk(r, carry):
        base = pl.multiple_of(r * rc, rc)
        acc = jnp.zeros((rc, LANE), F32)
        g = gp_ref[pl.ds(base, rc), :]
        for k in range(taps):
            acc = acc + w_ref[k:k + 1, lanes] * gp_ref[pl.ds(base + (taps - 1) - k, rc), :]
            xs = xp_ref[pl.ds(base + pad - (taps - 1) + k, rc), :]
            dw_ref[k:k + 1, lanes] += jnp.sum(g * xs, axis=0, keepdims=True)
        db_ref[:, lanes] += jnp.sum(g, axis=0, keepdims=True)
        dx_ref[pl.ds(base, rc), lanes] = acc.astype(dx_ref.dtype)
        return carry

    lax.fori_loop(0, length // rc, chunk, 0)


LANE_PAIR_ROWS = 1024


def ffn_interleave(a, name=None):
    rows, width = a.shape
    nb = width // (2 * LANE)
    if rows < LANE_PAIR_ROWS:
        return a.reshape(rows, 2, nb, LANE).swapaxes(1, 2).reshape(a.shape)

    def body(g_ref, u_ref, o_ref):
        o_ref[:, 0:LANE] = g_ref[...]
        o_ref[:, LANE:2 * LANE] = u_ref[...]

    tr = LANE_PAIR_ROWS
    return pl.pallas_call(
        body, name=name, grid=(rows // tr, nb),
        in_specs=[pl.BlockSpec((tr, LANE), lambda i, j: (i, j)), pl.BlockSpec((tr, LANE), lambda i, j: (i, nb + j))],
        out_specs=pl.BlockSpec((tr, 2 * LANE), lambda i, j: (i, j)),
        out_shape=jax.ShapeDtypeStruct(a.shape, a.dtype), compiler_params=_params(("parallel", "parallel")),
    )(a, a)


def ffn_deinterleave(a):
    rows, width = a.shape
    return a.reshape(rows, width // (2 * LANE), 2, LANE).swapaxes(1, 2).reshape(a.shape)


PAIR_COPY_ROWS = 512


def ffn_pairs_to_shards(a, *, name):
    rows, width = a.shape
    cols = width // 4
    per = cols // LANE
    tr = min(rows, PAIR_COPY_ROWS)

    def body(a_ref, o_ref):
        is_up = pl.program_id(1) >= 2
        for parity, chosen in ((0, jnp.logical_not(is_up)), (1, is_up)):
            @pl.when(chosen)
            def _(parity=parity):
                for t in range(per):
                    o_ref[0, :, t * LANE:(t + 1) * LANE] = a_ref[:, (2 * t + parity) * LANE:(2 * t + parity + 1) * LANE]

    return pl.pallas_call(
        body, name=name, grid=(rows // tr, 4),
        in_specs=[pl.BlockSpec((tr, 2 * cols), lambda i, s: (i, jnp.where(s >= 2, s - 2, s)))],
        out_specs=pl.BlockSpec((1, tr, cols), lambda i, s: (s, i, 0)),
        out_shape=jax.ShapeDtypeStruct((4, rows, cols), a.dtype), compiler_params=_params(("parallel", "parallel")),
    )(a)


def ffn_shards_to_pairs(parts, *, name):
    _, rows, cols = parts.shape
    per = cols // LANE
    tr = min(rows, PAIR_COPY_ROWS)

    def body(gate_ref, up_ref, o_ref):
        for t in range(per):
            o_ref[:, 2 * t * LANE:(2 * t + 1) * LANE] = gate_ref[0, :, t * LANE:(t + 1) * LANE]
            o_ref[:, (2 * t + 1) * LANE:(2 * t + 2) * LANE] = up_ref[0, :, t * LANE:(t + 1) * LANE]

    return pl.pallas_call(
        body, name=name, grid=(rows // tr, 2),
        in_specs=[pl.BlockSpec((1, tr, cols), lambda i, j: (j, i, 0)), pl.BlockSpec((1, tr, cols), lambda i, j: (2 + j, i, 0))],
        out_specs=pl.BlockSpec((tr, 2 * cols), lambda i, j: (i, j)),
        out_shape=jax.ShapeDtypeStruct((rows, 4 * cols), parts.dtype), compiler_params=_params(("parallel", "parallel")),
    )(parts, parts)


GATE, UP = slice(0, LANE), slice(LANE, 2 * LANE)


def _ffn_geometry(a, w):
    length, width = a.shape
    taps = w.shape[0]
    return length, width, width // (2 * LANE), taps, -(-(taps - 1) // 8) * 8, min(CONV_ROWS, length)


def _ffn_pre(xg_ref, xu_ref, w_ref, b_ref, base, taps, pad, rc):
    gate = _conv_taps(xg_ref, w_ref, base, taps, pad, jnp.broadcast_to(b_ref[:, GATE], (rc, LANE)), GATE)
    up = _conv_taps(xu_ref, w_ref, base, taps, pad, jnp.broadcast_to(b_ref[:, UP], (rc, LANE)), UP)
    return gate, up


def ffn_conv_act(a, w, b, *, name):
    length, width, nb, taps, pad, rc = _ffn_geometry(a, w)

    def body(a_ref, w_ref, b_ref, o_ref, xg_ref, xu_ref):
        for xp_ref, lanes in ((xg_ref, GATE), (xu_ref, UP)):
            xp_ref[0:pad, :] = jnp.zeros((pad, LANE), F32)
            xp_ref[pad:pad + length, :] = a_ref[:, lanes].astype(F32)

        def chunk(r, carry):
            base = pl.multiple_of(r * rc, rc)
            gate, up = _ffn_pre(xg_ref, xu_ref, w_ref, b_ref, base, taps, pad, rc)
            o_ref[pl.ds(base, rc), :] = (_silu(gate) * up).astype(o_ref.dtype)
            return carry

        lax.fori_loop(0, length // rc, chunk, 0)

    pair = lambda rows: pl.BlockSpec((rows, 2 * LANE), lambda j: (0, j))
    return pl.pallas_call(
        body, name=name, grid=(nb,), in_specs=[pair(length), pair(taps), pair(1)],
        out_specs=pl.BlockSpec((length, LANE), lambda j: (0, j)),
        out_shape=jax.ShapeDtypeStruct((length, width // 2), BF16),
        scratch_shapes=[pltpu.VMEM((pad + length, LANE), F32), pltpu.VMEM((pad + length, LANE), F32)],
        compiler_params=_params(("parallel",)),
    )(a, w, b)


def ffn_conv_act_bwd(a, w, b, dact, *, name):
    length, width, nb, taps, pad, rc = _ffn_geometry(a, w)

    def body(a_ref, w_ref, b_ref, dy_ref, da_ref, dw_ref, db_ref, xg_ref, xu_ref, gg_ref, gu_ref):
        for xp_ref, lanes in ((xg_ref, GATE), (xu_ref, UP)):
            xp_ref[0:pad, :] = jnp.zeros((pad, LANE), F32)
            xp_ref[pad:pad + length, :] = a_ref[:, lanes].astype(F32)
        for gp_ref in (gg_ref, gu_ref):
            gp_ref[length:length + pad, :] = jnp.zeros((pad, LANE), F32)

        def pre_chunk(r, carry):
            base = pl.multiple_of(r * rc, rc)
            gate, up = _ffn_pre(xg_ref, xu_ref, w_ref, b_ref, base, taps, pad, rc)
            sig = jax.nn.sigmoid(gate)
            dy = dy_ref[pl.ds(base, rc), :]
            gg_ref[pl.ds(base, rc), :] = dy * up * (sig * (1.0 + gate * (1.0 - sig)))
            gu_ref[pl.ds(base, rc), :] = dy * (gate * sig)
            return carry

        lax.fori_loop(0, length // rc, pre_chunk, 0)
        _conv_transpose(xg_ref, gg_ref, w_ref, da_ref, dw_ref, db_ref, GATE, length, taps, pad, rc)
        _conv_transpose(xu_ref, gu_ref, w_ref, da_ref, dw_ref, db_ref, UP, length, taps, pad, rc)

    pair = lambda rows: pl.BlockSpec((rows, 2 * LANE), lambda j: (0, j))
    return pl.pallas_call(
        body, name=name, grid=(nb,),
        in_specs=[pair(length), pair(taps), pair(1), pl.BlockSpec((length, LANE), lambda j: (0, j))],
        out_specs=[pair(length), pair(taps), pair(1)],
        out_shape=[jax.ShapeDtypeStruct((length, width), BF16), jax.ShapeDtypeStruct((taps, width), F32),
                   jax.ShapeDtypeStruct((1, width), F32)],
        scratch_shapes=[pltpu.VMEM((pad + length, LANE), F32), pltpu.VMEM((pad + length, LANE), F32),
                        pltpu.VMEM((length + pad, LANE), F32), pltpu.VMEM((length + pad, LANE), F32)],
        compiler_params=_params(("parallel",)),
    )(a, w, b, dact)


def _retention_consts():
    h = np.arange(RET_HEADS, dtype=np.float32)
    log_g = np.log1p(-(2.0 ** (-5.0 - h))).astype(np.float32)
    idx = np.arange(CHUNK, dtype=np.float32)
    diff = idx[:, None] - idx[None, :]
    intra = np.where(diff[None] >= 0, np.exp(np.maximum(diff, 0.0)[None] * log_g[:, None, None]), 0.0)
    zeta = np.exp((CHUNK - 1 - idx)[None, :] * log_g[:, None])
    xi = np.exp((idx + 1)[None, :] * log_g[:, None])
    decay = np.exp(CHUNK * log_g)
    zeta = np.broadcast_to(zeta[:, :, None], (RET_HEADS, CHUNK, RET_DK))
    xi = np.broadcast_to(xi[:, :, None], (RET_HEADS, CHUNK, RET_DV))
    return (jnp.asarray(intra, F32), jnp.asarray(zeta, F32), jnp.asarray(xi, F32), [float(d) for d in decay])


def _rotary_tables(length):
    inv = ROPE_BASE ** (-jnp.arange(0, RET_DK, 2, dtype=F32) / RET_DK)
    ang = jnp.arange(length).astype(F32)[:, None] * inv[None, :]
    cos, sin = jnp.cos(ang), jnp.sin(ang)
    return jnp.concatenate([cos, cos], axis=1), jnp.concatenate([-sin, sin], axis=1)


def _rot(x, cos2, sin2):
    return x * cos2 + pltpu.roll(x, RET_DK // 2, 1) * sin2


def _rot_t(y, cos2, sin2):
    return y * cos2 + pltpu.roll(y * sin2, RET_DK // 2, 1)


def _ret_chunk(q, k, v, g, state, intra, zeta, xi, decay):
    s = _mm_nt(q, k) * intra
    kv = _mm_tn(k * zeta, v)
    o = _mm(s, v) + _mm(q, state) * xi
    oc = o - jnp.mean(o, axis=-1, keepdims=True)
    r = oc * lax.rsqrt(jnp.mean(oc * oc, axis=-1, keepdims=True) + EPS)
    return _silu(g) * r, state * decay + kv


RET_QK, RET_V = RET_HEADS * RET_DK, RET_HEADS * RET_DV


def _ret_specs(rev, nc):
    def cidx(c):
        return nc - 1 - c if rev else c
    whole = lambda shape: pl.BlockSpec(shape, lambda c: (0,) * len(shape))
    return [
        pl.BlockSpec((CHUNK, RET_QK), lambda c: (cidx(c), 0)),
        pl.BlockSpec((CHUNK, RET_QK), lambda c: (cidx(c), 1)),
        pl.BlockSpec((CHUNK, RET_V), lambda c: (cidx(c), 1)),
        pl.BlockSpec((CHUNK, RET_V), lambda c: (cidx(c), 2)),
        pl.BlockSpec((CHUNK, RET_DK), lambda c: (cidx(c), 0)),
        pl.BlockSpec((CHUNK, RET_DK), lambda c: (cidx(c), 0)),
        whole((RET_HEADS, CHUNK, CHUNK)), whole((RET_HEADS, CHUNK, RET_DK)), whole((RET_HEADS, CHUNK, RET_DV)),
    ], cidx


def _head(ref, h, width):
    return ref[:, h * width:(h + 1) * width].astype(F32)


def retention_fwd(proj, cos2, sin2):
    length = proj.shape[0]
    nc = length // CHUNK
    intra, zeta, xi, decays = _retention_consts()
    specs, _ = _ret_specs(False, nc)
    scale = RET_DK ** -0.5

    def body(q_ref, k_ref, v_ref, g_ref, cos_ref, sin_ref, intra_ref, zeta_ref, xi_ref, y_ref, st_ref, state):
        @pl.when(pl.program_id(0) == 0)
        def _():
            state[...] = jnp.zeros_like(state)

        cos2v, sin2v = cos_ref[...], sin_ref[...]
        for h in range(RET_HEADS):
            q = _rot(_head(q_ref, h, RET_DK), cos2v, sin2v)
            k = _rot(_head(k_ref, h, RET_DK), cos2v, sin2v) * scale
            st_ref[h, 0] = state[h]
            y, new_state = _ret_chunk(q, k, _head(v_ref, h, RET_DV), _head(g_ref, h, RET_DV), state[h],
                                      intra_ref[h], zeta_ref[h], xi_ref[h], decays[h])
            y_ref[:, h * RET_DV:(h + 1) * RET_DV] = y.astype(y_ref.dtype)
            state[h] = new_state

    return pl.pallas_call(
        body, name="retention_fwd", grid=(nc,), in_specs=specs,
        out_specs=[pl.BlockSpec((CHUNK, RET_V), lambda c: (c, 0)),
                   pl.BlockSpec((RET_HEADS, 1, RET_DK, RET_DV), lambda c: (0, c, 0, 0))],
        out_shape=[jax.ShapeDtypeStruct((length, RET_V), BF16),
                   jax.ShapeDtypeStruct((RET_HEADS, nc, RET_DK, RET_DV), F32)],
        scratch_shapes=[pltpu.VMEM((RET_HEADS, RET_DK, RET_DV), F32)],
        compiler_params=_params(("arbitrary",)),
    )(proj, proj, proj, proj, cos2, sin2, intra, zeta, xi)


def retention_bwd(proj, cos2, sin2, states, dmix):
    length = proj.shape[0]
    nc = length // CHUNK
    intra, zeta, xi, decays = _retention_consts()
    specs, cidx = _ret_specs(True, nc)
    scale = RET_DK ** -0.5

    def body(q_ref, k_ref, v_ref, g_ref, cos_ref, sin_ref, intra_ref, zeta_ref, xi_ref, st_ref, dy_ref,
             dq_ref, dk_ref, dv_ref, dg_ref, dstate):
        @pl.when(pl.program_id(0) == 0)
        def _():
            dstate[...] = jnp.zeros_like(dstate)

        cos2v, sin2v = cos_ref[...], sin_ref[...]
        for h in range(RET_HEADS):
            q = _rot(_head(q_ref, h, RET_DK), cos2v, sin2v)
            k = _rot(_head(k_ref, h, RET_DK), cos2v, sin2v) * scale
            intra_v, zeta_v, xi_v, decay = intra_ref[h], zeta_ref[h], xi_ref[h], decays[h]
            _, vjp = jax.vjp(lambda q, k, v, g, s: _ret_chunk(q, k, v, g, s, intra_v, zeta_v, xi_v, decay),
                             q, k, _head(v_ref, h, RET_DV), _head(g_ref, h, RET_DV), st_ref[h, 0])
            dq, dk, dv, dg, ds = vjp((_head(dy_ref, h, RET_DV).astype(F32), dstate[h]))
            dq_ref[:, h * RET_DK:(h + 1) * RET_DK] = _rot_t(dq, cos2v, sin2v).astype(dq_ref.dtype)
            dk_ref[:, h * RET_DK:(h + 1) * RET_DK] = _rot_t(dk * scale, cos2v, sin2v).astype(dk_ref.dtype)
            dv_ref[:, h * RET_DV:(h + 1) * RET_DV] = dv.astype(dv_ref.dtype)
            dg_ref[:, h * RET_DV:(h + 1) * RET_DV] = dg.astype(dg_ref.dtype)
            dstate[h] = ds

    specs = specs + [pl.BlockSpec((RET_HEADS, 1, RET_DK, RET_DV), lambda c: (0, cidx(c), 0, 0)),
                     pl.BlockSpec((CHUNK, RET_V), lambda c: (cidx(c), 0))]
    row = lambda width: pl.BlockSpec((CHUNK, width), lambda c: (cidx(c), 0))
    return pl.pallas_call(
        body, name="retention_bwd", grid=(nc,), in_specs=specs,
        out_specs=[row(RET_QK), row(RET_QK), row(RET_V), row(RET_V)],
        out_shape=[jax.ShapeDtypeStruct((length, RET_QK), BF16), jax.ShapeDtypeStruct((length, RET_QK), BF16),
                   jax.ShapeDtypeStruct((length, RET_V), BF16), jax.ShapeDtypeStruct((length, RET_V), BF16)],
        scratch_shapes=[pltpu.VMEM((RET_HEADS, RET_DK, RET_DV), F32)],
        compiler_params=_params(("arbitrary",)),
    )(proj, proj, proj, proj, cos2, sin2, intra, zeta, xi, states, dmix)


def _ssd_consts():
    tri = np.tril(np.ones((CHUNK, CHUNK), np.float32))
    expand = np.zeros((LANE, SSM_DINNER), np.float32)
    for h in range(SSM_HEADS):
        expand[h, h * SSM_P:(h + 1) * SSM_P] = 1.0
    return jnp.asarray(tri), jnp.asarray(tri.T.copy()), jnp.asarray(expand)


def _ssd_chunk(xs, bm, cm, dtr, z, state, dt_bias, a_log, d_skip, norm_w, tri, tri_t, expand):
    gw = SSM_DINNER // SSM_GROUPS
    dt = jax.nn.softplus(dtr + dt_bias)
    a_neg = -jnp.exp(a_log)
    da = dt * a_neg
    acs = _dot_hi(tri, da)
    acs_t = _dot_hi_tn(da, tri_t)
    dt_x = _times_01(dt, expand)
    a_x = jnp.mean(_dot_hi(jnp.broadcast_to(a_neg, (8, LANE)), expand), axis=0, keepdims=True)
    da_x = dt_x * a_x
    acs_x = _01_times(tri, da_x)
    tot_x = jnp.sum(da_x, axis=0, keepdims=True)
    x_dt = xs * dt_x
    x_dec = x_dt * jnp.exp(tot_x - acs_x)
    e_acs = jnp.exp(acs_x)
    e_tot = jnp.exp(tot_x)
    lane = lax.broadcasted_iota(jnp.int32, (CHUNK, LANE), 1)
    sub = lax.broadcasted_iota(jnp.int32, (CHUNK, LANE), 0)
    causal = sub >= lane
    ys, new_states = [], []
    for g in range(SSM_GROUPS):
        bg = bm[:, g * SSM_N:(g + 1) * SSM_N]
        cg = cm[:, g * SSM_N:(g + 1) * SSM_N]
        sg = state[:, g * gw:(g + 1) * gw]
        cb = _mm_nt(cg, bg)
        y_off = _mm(cg, sg) * e_acs[:, g * gw:(g + 1) * gw]
        new_states.append(sg * e_tot[:, g * gw:(g + 1) * gw] + _mm_tn(bg, x_dec[:, g * gw:(g + 1) * gw]))
        pairs = []
        for p in range(gw // LANE):
            hp = g * (gw // LANE) + p
            xp = x_dt[:, hp * LANE:(hp + 1) * LANE]
            halves = []
            for head in (2 * hp, 2 * hp + 1):
                col = jnp.sum(jnp.where(lane == head, acs, 0.0), axis=1, keepdims=True)
                row = jnp.sum(jnp.where(sub == head, acs_t, 0.0), axis=0, keepdims=True)
                decay = jnp.exp(jnp.where(causal, col - row, -1e30))
                halves.append(_mm(cb * decay, xp))
            pairs.append(jnp.where(lane < SSM_P, halves[0], halves[1]))
        ys.append(jnp.concatenate(pairs, axis=1) + y_off)
    d_x = jnp.mean(_dot_hi(jnp.broadcast_to(d_skip, (8, LANE)), expand), axis=0, keepdims=True)
    y = (jnp.concatenate(ys, axis=1) + d_x * xs) * _silu(z)
    normed = []
    for g in range(SSM_GROUPS):
        yg = y[:, g * gw:(g + 1) * gw]
        normed.append(yg * lax.rsqrt(jnp.mean(yg * yg, axis=-1, keepdims=True) + EPS))
    return jnp.concatenate(normed, axis=1) * norm_w, jnp.concatenate(new_states, axis=1)


XBC = SSM_DINNER + 2 * SSM_GROUPS * SSM_N


def _ssd_specs(rev, nc):
    def cidx(c):
        return nc - 1 - c if rev else c
    row = lambda w, j: pl.BlockSpec((CHUNK, w), lambda c: (cidx(c), j))
    whole = lambda shape: pl.BlockSpec(shape, lambda c: (0,) * len(shape))
    return [row(XBC, 0), row(LANE, 0), row(SSM_DINNER, 3),
            whole((1, LANE)), whole((1, LANE)), whole((1, LANE)), whole((1, SSM_DINNER)),
            whole((CHUNK, CHUNK)), whole((CHUNK, CHUNK)), whole((LANE, SSM_DINNER))], cidx


def ssd_fwd(xbc, dt_raw, proj, dt_bias, a_log, d_skip, norm_w):
    length = proj.shape[0]
    nc = length // CHUNK
    tri, tri_t, expand = _ssd_consts()
    specs, _ = _ssd_specs(False, nc)

    def body(xbc_ref, dt_ref, z_ref, dtb_ref, alog_ref, d_ref, nw_ref, tri_ref, trit_ref, e_ref, y_ref, st_ref, state):
        @pl.when(pl.program_id(0) == 0)
        def _():
            state[...] = jnp.zeros_like(state)

        st_ref[0] = state[...]
        y, new_state = _ssd_chunk(
            xbc_ref[:, 0:SSM_DINNER], xbc_ref[:, SSM_DINNER:SSM_DINNER + 256], xbc_ref[:, SSM_DINNER + 256:XBC],
            dt_ref[...], z_ref[...].astype(F32), state[...], dtb_ref[...], alog_ref[...], d_ref[...], nw_ref[...],
            tri_ref[...], trit_ref[...], e_ref[...])
        y_ref[...] = y.astype(y_ref.dtype)
        state[...] = new_state

    return pl.pallas_call(
        body, name="ssd_fwd", grid=(nc,), in_specs=specs,
        out_specs=[pl.BlockSpec((CHUNK, SSM_DINNER), lambda c: (c, 0)),
                   pl.BlockSpec((1, SSM_N, SSM_DINNER), lambda c: (c, 0, 0))],
        out_shape=[jax.ShapeDtypeStruct((length, SSM_DINNER), BF16),
                   jax.ShapeDtypeStruct((nc, SSM_N, SSM_DINNER), F32)],
        scratch_shapes=[pltpu.VMEM((SSM_N, SSM_DINNER), F32)],
        compiler_params=_params(("arbitrary",)),
    )(xbc, dt_raw, proj, dt_bias, a_log, d_skip, norm_w, tri, tri_t, expand)


def ssd_bwd(xbc, dt_raw, proj, dt_bias, a_log, d_skip, norm_w, states, dmix):
    length = proj.shape[0]
    nc = length // CHUNK
    tri, tri_t, expand = _ssd_consts()
    specs, cidx = _ssd_specs(True, nc)

    def body(xbc_ref, dt_ref, z_ref, dtb_ref, alog_ref, d_ref, nw_ref, tri_ref, trit_ref, e_ref, st_ref, dy_ref,
             dxbc_ref, ddt_ref, dz_ref, ddtb_ref, dalog_ref, dd_ref, dnw_ref, dstate):
        c = pl.program_id(0)

        @pl.when(c == 0)
        def _():
            dstate[...] = jnp.zeros_like(dstate)

        tri_v, trit_v, e_v = tri_ref[...], trit_ref[...], e_ref[...]
        _, vjp = jax.vjp(
            lambda *a: _ssd_chunk(*a, tri_v, trit_v, e_v),
            xbc_ref[:, 0:SSM_DINNER], xbc_ref[:, SSM_DINNER:SSM_DINNER + 256], xbc_ref[:, SSM_DINNER + 256:XBC],
            dt_ref[...], z_ref[...].astype(F32), st_ref[0], dtb_ref[...], alog_ref[...], d_ref[...], nw_ref[...])
        dxs, dbm, dcm, ddt, dz, ds, ddtb, dalog, dd, dnw = vjp((dy_ref[...].astype(F32), dstate[...]))
        dxbc_ref[:, 0:SSM_DINNER] = dxs
        dxbc_ref[:, SSM_DINNER:SSM_DINNER + 256] = dbm
        dxbc_ref[:, SSM_DINNER + 256:XBC] = dcm
        ddt_ref[...] = ddt.astype(ddt_ref.dtype)
        dz_ref[...] = dz.astype(dz_ref.dtype)
        dstate[...] = ds
        for r, d in ((ddtb_ref, ddtb), (dalog_ref, dalog), (dd_ref, dd), (dnw_ref, dnw)):
            @pl.when(c == 0)
            def _(r=r, d=d):
                r[...] = d

            @pl.when(c > 0)
            def _(r=r, d=d):
                r[...] += d

    whole = lambda shape: pl.BlockSpec(shape, lambda c: (0,) * len(shape))
    specs = specs + [pl.BlockSpec((1, SSM_N, SSM_DINNER), lambda c: (cidx(c), 0, 0)),
                     pl.BlockSpec((CHUNK, SSM_DINNER), lambda c: (cidx(c), 1))]
    return pl.pallas_call(
        body, name="ssd_bwd", grid=(nc,), in_specs=specs,
        out_specs=[pl.BlockSpec((CHUNK, XBC), lambda c: (cidx(c), 0)), pl.BlockSpec((CHUNK, LANE), lambda c: (cidx(c), 0)),
                   pl.BlockSpec((CHUNK, SSM_DINNER), lambda c: (cidx(c), 0)),
                   whole((1, LANE)), whole((1, LANE)), whole((1, LANE)), whole((1, SSM_DINNER))],
        out_shape=[jax.ShapeDtypeStruct((length, XBC), F32), jax.ShapeDtypeStruct((length, LANE), BF16),
                   jax.ShapeDtypeStruct((length, SSM_DINNER), BF16),
                   jax.ShapeDtypeStruct((1, LANE), F32), jax.ShapeDtypeStruct((1, LANE), F32),
                   jax.ShapeDtypeStruct((1, LANE), F32), jax.ShapeDtypeStruct((1, SSM_DINNER), F32)],
        scratch_shapes=[pltpu.VMEM((SSM_N, SSM_DINNER), F32)],
        compiler_params=_params(("arbitrary",)),
    )(xbc, dt_raw, proj, dt_bias, a_log, d_skip, norm_w, tri, tri_t, expand, states, dmix)


def _cmul(ar, ai, br, bi):
    return ar * br - ai * bi, ar * bi + ai * br


def s5_scan(b_re, b_im, a_re, a_im, *, reverse=False, states=None, name, lw=256):
    length, lanes = b_re.shape
    nk = length // SCAN_SEG
    with_da = states is not None
    assert reverse or not with_da

    def shift(v):
        sub = lax.broadcasted_iota(jnp.int32, v.shape, 0)
        if reverse:
            return jnp.where(sub == SCAN_SEG - 1, 0.0, pltpu.roll(v, SCAN_SEG - 1, 0))
        return jnp.where(sub == 0, 0.0, pltpu.roll(v, 1, 0))

    def body(*refs):
        if with_da:
            bre_ref, bim_ref, are_ref, aim_ref, sre_ref, sim_ref, xre_ref, xim_ref, dare_ref, daim_ref = refs
        else:
            bre_ref, bim_ref, are_ref, aim_ref, xre_ref, xim_ref = refs
        ar = jnp.broadcast_to(are_ref[...], (SCAN_SEG, lw))
        ai = jnp.broadcast_to(aim_ref[...], (SCAN_SEG, lw))

        def tile(i):
            k = (nk - 1 - i) if reverse else i
            return pl.ds(pl.multiple_of(k * SCAN_SEG, SCAN_SEG), SCAN_SEG)

        def local(i, carry):
            xr, xi, pr, pi = carry
            rows = tile(i)
            mr, mi = _cmul(ar, ai, xr, xi)
            xr, xi = mr + bre_ref[rows, :], mi + bim_ref[rows, :]
            xre_ref[rows, :] = xr
            xim_ref[rows, :] = xi
            pr, pi = _cmul(ar, ai, pr, pi)
            return xr, xi, pr, pi

        zero = jnp.zeros((SCAN_SEG, lw), F32)
        one = jnp.ones((SCAN_SEG, lw), F32)
        er, ei, pr, pi = lax.fori_loop(0, nk, local, (zero, zero, one, zero))
        cr, ci = zero, zero
        for _ in range(SCAN_SEG - 1):
            mr, mi = _cmul(pr, pi, cr, ci)
            cr, ci = shift(er + mr), shift(ei + mi)

        def fix(i, carry):
            pr, pi, dr, di = carry
            rows = tile(i)
            pr, pi = _cmul(ar, ai, pr, pi)
            mr, mi = _cmul(pr, pi, cr, ci)
            xr, xi = xre_ref[rows, :] + mr, xim_ref[rows, :] + mi
            xre_ref[rows, :] = xr
            xim_ref[rows, :] = xi
            if with_da:
                k = nk - 1 - i
                prev = pl.ds(pl.multiple_of(jnp.maximum(k - 1, 0) * SCAN_SEG, SCAN_SEG), SCAN_SEG)
                last = pl.ds((nk - 1) * SCAN_SEG, SCAN_SEG)
                sub = lax.broadcasted_iota(jnp.int32, (SCAN_SEG, lw), 0)
                wr = jnp.where(sub == 0, 0.0, pltpu.roll(sre_ref[last, :], 1, 0))
                wi = jnp.where(sub == 0, 0.0, pltpu.roll(sim_ref[last, :], 1, 0))
                sr = jnp.where(k == 0, wr, sre_ref[prev, :])
                si = jnp.where(k == 0, wi, sim_ref[prev, :])
                dr, di = dr + xr * sr + xi * si, di + xi * sr - xr * si
            return pr, pi, dr, di

        _, _, dr, di = lax.fori_loop(0, nk, fix, (one, zero, zero, zero))
        if with_da:
            dare_ref[...] = jnp.sum(dr, axis=0, keepdims=True)
            daim_ref[...] = jnp.sum(di, axis=0, keepdims=True)

    col = pl.BlockSpec((length, lw), lambda j: (0, j))
    vec = pl.BlockSpec((1, lw), lambda j: (0, j))
    ins = [b_re, b_im, a_re, a_im] + (list(states) if with_da else [])
    in_specs = [col, col, vec, vec] + ([col, col] if with_da else [])
    out_specs = [col, col] + ([vec, vec] if with_da else [])
    out_shape = [jax.ShapeDtypeStruct((length, lanes), F32)] * 2 + ([jax.ShapeDtypeStruct((1, lanes), F32)] * 2 if with_da else [])
    return pl.pallas_call(
        body, name=name, grid=(lanes // lw,), in_specs=in_specs, out_specs=out_specs, out_shape=out_shape,
        compiler_params=_params(("parallel",)),
    )(*ins)


def _seg_interleave(v):
    length = v.shape[0]
    return v.reshape(SCAN_SEG, length // SCAN_SEG, -1).transpose(1, 0, 2).reshape(length, -1)


def _seg_deinterleave(v):
    length = v.shape[0]
    return v.reshape(length // SCAN_SEG, SCAN_SEG, -1).transpose(1, 0, 2).reshape(length, -1)


def _block_diag(m):
    eye = jnp.eye(S5_GROUPS, dtype=m.dtype)
    return (m.reshape(S5_GROUPS, S5_GROUP, 1, S5_STATE) * eye[:, None, :, None]).reshape(S5_GROUPS * S5_GROUP, S5_LANES)


def _block_diag_take(full):
    idx = jnp.arange(S5_GROUPS)
    blocks = full.reshape(S5_GROUPS, S5_GROUP, S5_GROUPS, S5_STATE)[idx, :, idx, :]
    return blocks.reshape(S5_GROUPS * S5_GROUP, S5_STATE)


def _s5_prep(a_re, a_im, log_step, b_re, b_im, rep):
    step = jnp.exp(log_step)
    mag = jnp.exp(a_re * step)
    ab_re = mag * jnp.cos(a_im * step)
    ab_im = mag * jnp.sin(a_im * step)
    den = a_re * a_re + a_im * a_im
    f_re = ((ab_re - 1.0) * a_re + ab_im * a_im) / den
    f_im = (ab_im * a_re - (ab_re - 1.0) * a_im) / den
    fr, fi = _dot_hi(rep, f_re), _dot_hi(rep, f_im)
    return ab_re, ab_im, fr * b_re - fi * b_im, fr * b_im + fi * b_re


def _rms(x, g):
    return (x * lax.rsqrt(jnp.mean(x * x, axis=-1, keepdims=True) + EPS) * g,)


def _ffn_act(gate, up):
    return (_silu(gate) * up,)


def _glu(a, g):
    return (a * jax.nn.sigmoid(g),)


def _ln_silu(x, g, b):
    xc = x - jnp.mean(x, axis=-1, keepdims=True)
    var = jnp.mean(xc * xc, axis=-1, keepdims=True)
    return (_silu(xc * lax.rsqrt(var + EPS) * g + b),)


def _s5_post(y, u, d_skip, glu_w):
    s = jax.nn.gelu(y + d_skip * u)
    return (s * jax.nn.sigmoid(_mm(s, glu_w)),)


def loss_head(x, tgt, g, *, tl=512):
    length, d = x.shape
    tl = min(tl, length)

    def body(x_ref, t_ref, g_ref, loss_ref, dx_ref, dg_ref):
        i = pl.program_id(0)
        y, vjp = jax.vjp(lambda x, g: _rms(x, g)[0], x_ref[...], g_ref[...])
        err = y - t_ref[...]
        dx, dg = vjp(err * (1.0 / d))
        dx_ref[...] = dx
        part = jnp.broadcast_to(0.5 * jnp.sum(jnp.mean(err * err, axis=-1, keepdims=True), axis=0, keepdims=True), (1, LANE))

        @pl.when(i == 0)
        def _():
            loss_ref[...] = part
            dg_ref[...] = dg

        @pl.when(i > 0)
        def _():
            loss_ref[...] += part
            dg_ref[...] += dg

    row = pl.BlockSpec((tl, d), lambda i: (i, 0))
    return pl.pallas_call(
        body, name="loss_head", grid=(length // tl,),
        in_specs=[row, row, pl.BlockSpec((1, d), lambda i: (0, 0))],
        out_specs=[pl.BlockSpec((1, LANE), lambda i: (0, 0)), row, pl.BlockSpec((1, d), lambda i: (0, 0))],
        out_shape=[jax.ShapeDtypeStruct((1, LANE), F32), jax.ShapeDtypeStruct((length, d), F32),
                   jax.ShapeDtypeStruct((1, d), F32)],
        compiler_params=_params(("arbitrary",)),
    )(x, tgt, g)


def _pad_heads(v):
    return jnp.pad(v, ((0, 0), (0, LANE - v.shape[1])))


def local_step(x, tgt, w, late_weights=None, early_reduce=None):
    length = x.shape[0]
    cos2, sin2 = _rotary_tables(length)
    grads = {}
    w = dict(w)

    def rms_fwd(xin, g, name):
        return rowwise_fwd(_rms, [xin], [], [g], [], [(D_MODEL, BF16)], name=name, tl=512)[0]

    def rms_bwd(xin, g, dh, dxo, name):
        return rowwise_bwd(_rms, [xin], [], [g], [], [dh], [F32], name=name, tl=512, add=dxo)

    def ffn_fwd(i, xin):
        hf = rms_fwd(xin, w["ffn_norm"][i:i + 1], f"ffn{i}_norm")
        w_up = w["ffn_w_up_pairs"][i] if "ffn_w_up_pairs" in w else ffn_interleave(w["ffn_w_up"][i], name=f"ffn{i}_up_pairs")
        a = matmul(hf, w_up, out_dtype=BF16, name=f"ffn{i}_up")
        act = ffn_conv_act(a, ffn_interleave(w["ffn_dw_w"][i]), ffn_interleave(w["ffn_dw_b"][i:i + 1]),
                           name=f"ffn{i}_conv_act")
        return matmul(act, w["ffn_w_down"][i], res=xin, name=f"ffn{i}_down"), (hf, a, act, w_up)

    def ffn_bwd(i, xin, saved, dxo):
        hf, a, act, w_up = saved
        dact = matmul(dxo, w["ffn_w_down"][i], tb=True, name=f"ffn{i}_down_dx")
        dw_down = matmul(act, dxo, ta=True, name=f"ffn{i}_down_dw")
        da, ddw_w, ddw_b = ffn_conv_act_bwd(a, ffn_interleave(w["ffn_dw_w"][i]), ffn_interleave(w["ffn_dw_b"][i:i + 1]),
                                            dact, name=f"ffn{i}_conv_act_bwd")
        dw_up = ffn_pairs_to_shards(matmul(hf, da, ta=True, name=f"ffn{i}_up_dw"), name=f"ffn{i}_up_dw_shards")
        dhf = matmul(da, w_up, tb=True, name=f"ffn{i}_up_dx")
        dxin, dnorm = rms_bwd(xin, w["ffn_norm"][i:i + 1], dhf, dxo, f"ffn{i}_norm_bwd")
        return dxin, dict(ffn_norm=dnorm, ffn_w_up=dw_up, ffn_dw_w=ffn_deinterleave(ddw_w),
                          ffn_dw_b=ffn_deinterleave(ddw_b), ffn_w_down=dw_down)

    w_in_e = jnp.pad(w["e_w_in"][0], ((0, 0), (0, EVEN_IN_PAD - EVEN_IN)))
    conv_w_e, conv_b_e = w["e_conv_w"][0], w["e_conv_b"]
    dt_bias, a_log, d_skip = _pad_heads(w["e_dt_bias"]), _pad_heads(w["e_a_log"]), _pad_heads(w["e_d"])
    xbc_off = 4 * D_MODEL

    hn0 = rms_fwd(x, w["mix_norm"][0:1], "mix0_norm")
    proj0 = matmul(hn0, w_in_e, out_dtype=BF16, name="even_in")
    dt_raw = matmul(hn0, w_in_e[:, EVEN_IN_PAD - LANE:], name="even_in_dt")
    y_ret, ret_states = retention_fwd(proj0, cos2, sin2)
    xbc = conv_fwd(proj0, conv_w_e, conv_b_e, act=True, off=xbc_off, name="ssd_conv")
    y_ssm, ssd_states = ssd_fwd(xbc, dt_raw, proj0, dt_bias, a_log, d_skip, w["e_ssm_norm"])
    mix0 = jnp.concatenate([y_ret, y_ssm], axis=1)
    if late_weights is not None:
        w.update(late_weights(y_ssm))
    w_out_e = w["e_w_out"][0]
    x1 = matmul(mix0, w_out_e, res=x, name="even_out")
    x2, ffn0_saved = ffn_fwd(0, x1)

    w_in_o, w_out_o, glu_w = w["o_w_in"][0], w["o_w_out"][0], w["o_glu_w"][0]
    dw_w_o, dw_b_o, ln_g, ln_b, d_o = w["o_dw_w"][0], w["o_dw_b"], w["o_ln_g"], w["o_ln_b"], w["o_d"]
    rep = jnp.asarray(np.repeat(np.eye(S5_GROUPS, dtype=np.float32), S5_GROUP, axis=0))
    rows_gc = (S5_GROUPS * S5_GROUP, S5_STATE)
    prep_in = [w["o_a_re"][0], w["o_a_im"][0], w["o_log_step"].reshape(S5_GROUPS, 1),
               w["o_b_re"][0].transpose(0, 2, 1).reshape(rows_gc), w["o_b_im"][0].transpose(0, 2, 1).reshape(rows_gc), rep]
    ab_re, ab_im, bb_re, bb_im = whole_fwd(
        _s5_prep, prep_in, [(S5_GROUPS, S5_STATE)] * 2 + [rows_gc] * 2, name="s5_prep")
    a_re_row, a_im_row = ab_re.reshape(1, S5_LANES), ab_im.reshape(1, S5_LANES)
    b_re_bd, b_im_bd = _block_diag(bb_re).astype(BF16), _block_diag(bb_im).astype(BF16)
    c_re_bd = _block_diag(w["o_c_re"][0].reshape(rows_gc)).astype(BF16)
    c_im_neg_bd = _block_diag(-w["o_c_im"][0].reshape(rows_gc)).astype(BF16)

    hn1 = rms_fwd(x2, w["mix_norm"][1:2], "mix1_norm")
    proj1 = matmul(hn1, w_in_o, name="odd_in")
    half = D_MODEL // 2
    c_glu = rowwise_fwd(_glu, [Cols(proj1, half, 0), Cols(proj1, half, 1)], [], [], [], [(half, F32)],
                        name="conf_glu", tl=512)[0]
    c_conv = conv_fwd(c_glu, dw_w_o, dw_b_o, act=False, name="conf_conv")
    c_out = rowwise_fwd(_ln_silu, [c_conv], [], [ln_g, ln_b], [], [(half, BF16)], name="conf_ln", tl=512)[0]
    u_seg = _seg_interleave(proj1[:, 2 * half:])
    bu_re = matmul(u_seg, b_re_bd, name="s5_bu_re")
    bu_im = matmul(u_seg, b_im_bd, name="s5_bu_im")
    xs_re, xs_im = s5_scan(bu_re, bu_im, a_re_row, a_im_row, name="s5_scan")
    y_im = matmul(xs_im, c_im_neg_bd, tb=True, name="s5_y_im")
    y_s5 = _seg_deinterleave(matmul(xs_re, c_re_bd, tb=True, res=y_im, name="s5_y_re"))
    s_out = rowwise_fwd(_s5_post, [y_s5, Cols(proj1, half, 2)], [], [d_o, glu_w], [], [(half, BF16)],
                        name="s5_post", tl=512)[0]
    mix1 = jnp.concatenate([c_out, s_out], axis=1)
    x3 = matmul(mix1, w_out_o, res=x2, name="odd_out")
    x4, ffn1_saved = ffn_fwd(1, x3)

    loss, dx4, dfinal = loss_head(x4, tgt, w["final_norm"].reshape(1, D_MODEL))
    grads["final_norm"] = dfinal.reshape(D_MODEL)

    dx3, g_ffn1 = ffn_bwd(1, x3, ffn1_saved, dx4)
    dmix1 = matmul(dx3, w_out_o, tb=True, name="odd_out_dx")
    grads["o_w_out"] = [matmul(mix1, dx3, ta=True, name="odd_out_dw")]
    dc_conv, dln_g, dln_b = rowwise_bwd(_ln_silu, [c_conv], [], [ln_g, ln_b], [], [Cols(dmix1, half, 0)], [F32],
                                        name="conf_ln_bwd", tl=512)
    dc_glu, ddw_w_o, ddw_b_o = conv_bwd(c_glu, dw_w_o, dw_b_o, dc_conv, act=False, name="conf_conv_bwd")
    d_cacg = rowwise_bwd(_glu, [Cols(proj1, half, 0), Cols(proj1, half, 1)], [], [], [], [dc_glu], [BF16],
                         name="conf_glu_bwd", tl=512, merge=True)[0]
    dy_s5, du_post, dd_o, dglu_w = rowwise_bwd(
        _s5_post, [y_s5, Cols(proj1, half, 2)], [], [d_o, glu_w], [], [Cols(dmix1, half, 1)], [F32, F32],
        name="s5_post_bwd", tl=512)
    dy_seg = _seg_interleave(dy_s5)
    dxs_re = matmul(dy_seg, c_re_bd, name="s5_dx_re")
    dxs_im = matmul(dy_seg, c_im_neg_bd, name="s5_dx_im")
    dc_re_bd = matmul(dy_seg, xs_re, ta=True, name="s5_dc_re")
    dc_im_neg_bd = matmul(dy_seg, xs_im, ta=True, name="s5_dc_im")
    g_re, g_im, dab_re, dab_im = s5_scan(dxs_re, dxs_im, a_re_row, -a_im_row, reverse=True, states=(xs_re, xs_im),
                                         name="s5_scan_bwd", lw=LANE)
    dbb_re = _block_diag_take(matmul(u_seg, g_re, ta=True, name="s5_db_re"))
    dbb_im = _block_diag_take(matmul(u_seg, g_im, ta=True, name="s5_db_im"))
    du_im = matmul(g_im, b_im_bd, tb=True, name="s5_du_im")
    du = _seg_deinterleave(matmul(g_re, b_re_bd, tb=True, res=du_im, name="s5_du_re")) + du_post
    da_re, da_im, dlog_step, db_re, db_im = whole_bwd(
        _s5_prep, prep_in, 5,
        [dab_re.reshape(S5_GROUPS, S5_STATE), dab_im.reshape(S5_GROUPS, S5_STATE), dbb_re, dbb_im], name="s5_prep_bwd")
    gcn = (S5_GROUPS, S5_GROUP, S5_STATE)
    grads.update(
        o_a_re=da_re[None], o_a_im=da_im[None], o_log_step=dlog_step.reshape(1, S5_GROUPS),
        o_b_re=db_re.reshape(gcn).transpose(0, 2, 1)[None], o_b_im=db_im.reshape(gcn).transpose(0, 2, 1)[None],
        o_c_re=_block_diag_take(dc_re_bd).reshape(gcn)[None], o_c_im=-_block_diag_take(dc_im_neg_bd).reshape(gcn)[None],
        o_d=dd_o, o_glu_w=[dglu_w], o_dw_w=ddw_w_o[None], o_dw_b=ddw_b_o, o_ln_g=dln_g, o_ln_b=dln_b)
    dproj1 = jnp.concatenate([d_cacg, du.astype(BF16)], axis=1)
    grads["o_w_in"] = [matmul(hn1, dproj1, ta=True, name="odd_in_dw")]
    dhn1 = matmul(dproj1, w_in_o, tb=True, name="odd_in_dx")
    dx2, dmix_norm1 = rms_bwd(x2, w["mix_norm"][1:2], dhn1, dx3, "mix1_norm_bwd")

    if early_reduce is not None:
        zero = early_reduce({("o_w_in", 0): grads["o_w_in"][0], ("o_glu_w", 0): grads["o_glu_w"][0],
                             ("o_w_out", 0): grads["o_w_out"][0], ("ffn_w_up", 1): g_ffn1["ffn_w_up"],
                             ("ffn_w_down", 1): g_ffn1["ffn_w_down"]})
        w["ffn_dw_b"] = w["ffn_dw_b"] + zero
    dx1, g_ffn0 = ffn_bwd(0, x1, ffn0_saved, dx2)
    if early_reduce is not None:
        dt_bias = dt_bias + early_reduce({("ffn_w_up", 0): g_ffn0["ffn_w_up"], ("ffn_w_down", 0): g_ffn0["ffn_w_down"]})
    for k in g_ffn0:
        per_layer = [g_ffn0[k], g_ffn1[k]]
        grads[k] = per_layer if k in ("ffn_w_up", "ffn_w_down") else jnp.stack(per_layer).reshape(w[k].shape)
    dmix0 = matmul(dx1, w_out_e, tb=True, name="even_out_dx")
    grads["e_w_out"] = [matmul(mix0, dx1, ta=True, name="even_out_dw")]
    if early_reduce is not None:
        a_log = a_log + early_reduce({("e_w_out", 0): grads["e_w_out"][0]})
    dq, dk, dv, dg = retention_bwd(proj0, cos2, sin2, ret_states, dmix0)
    dxbc_c, ddt, dz, ddt_bias, da_log, dd_skip, dssm_norm = ssd_bwd(
        xbc, dt_raw, proj0, dt_bias, a_log, d_skip, w["e_ssm_norm"], ssd_states, dmix0)
    dxbc, dconv_w, dconv_b = conv_bwd(proj0, conv_w_e, conv_b_e, dxbc_c, act=True, off=xbc_off,
                                      name="ssd_conv_bwd", dx_dtype=BF16)
    dproj0 = jnp.concatenate([dq, dk, dv, dg, dz, dxbc, ddt], axis=1)
    grads["e_w_in"] = [matmul(hn0, dproj0, ta=True, name="even_in_dw")[:, :EVEN_IN]]
    norm_w0 = w["mix_norm"][0:1]
    if early_reduce is not None:
        norm_w0 = norm_w0 + early_reduce({("e_w_in", 0): grads["e_w_in"][0]})
    dhn0 = matmul(dproj0, w_in_e, tb=True, name="even_in_dx")
    dx, dmix_norm0 = rms_bwd(x, norm_w0, dhn0, dx1, "mix0_norm_bwd")
    grads.update(
        mix_norm=jnp.concatenate([dmix_norm0, dmix_norm1], axis=0), e_conv_w=dconv_w[None], e_conv_b=dconv_b,
        e_dt_bias=ddt_bias[:, :SSM_HEADS], e_a_log=da_log[:, :SSM_HEADS], e_d=dd_skip[:, :SSM_HEADS],
        e_ssm_norm=dssm_norm)
    return loss, dx, grads


def adamw(w, g, m, v, *, name):
    shape = w.shape
    cols = shape[-1]
    rows = w.size // cols
    tr = _tile(rows, max(8, (512 * 1024 // cols) // 8 * 8), unit=8)

    def body(w_ref, g_ref, m_ref, v_ref, d_ref, nm_ref, nv_ref):
        gv = g_ref[...]
        nm = ADAM_B1 * m_ref[...] + (1.0 - ADAM_B1) * gv
        nv = ADAM_B2 * v_ref[...] + (1.0 - ADAM_B2) * jnp.square(gv)
        m_hat = nm / (1.0 - ADAM_B1 ** ADAM_STEP)
        v_hat = nv / (1.0 - ADAM_B2 ** ADAM_STEP)
        d_ref[...] = -ADAM_LR * (m_hat / (jnp.sqrt(v_hat) + ADAM_EPS) + ADAM_WD * w_ref[...])
        nm_ref[...] = nm
        nv_ref[...] = nv

    spec = pl.BlockSpec((tr, cols), lambda i: (i, 0))
    outs = pl.pallas_call(
        body, name=name, grid=(rows // tr,), in_specs=[spec] * 4, out_specs=[spec] * 3,
        out_shape=[jax.ShapeDtypeStruct((rows, cols), F32)] * 3, compiler_params=_params(("parallel",)),
    )(*[t.reshape(rows, cols) for t in (w, g, m, v)])
    return [o.reshape(shape) for o in outs]


OTHER_CHIPS = ((1, 0), (0, 1), (1, 1))
ANY = pl.BlockSpec(memory_space=pl.ANY)


def _position():
    return lax.axis_index("x"), lax.axis_index("y"), lax.axis_index("c")


def _flip(v, f):
    return 1 - v if f else v


def _remote(src, dst, send_sem, recv_sem, device):
    return pltpu.make_async_remote_copy(src_ref=src, dst_ref=dst, send_sem=send_sem, recv_sem=recv_sem,
                                        device_id=device, device_id_type=MESH)


def gather_shards(big, small):
    n_big, n_small = len(big), len(small)
    halves = [a.shape[0] // 2 for a in big]

    def body(*refs):
        big_refs, small_refs = refs[:n_big], refs[n_big:n_big + n_small]
        obig_refs = refs[n_big + n_small:2 * n_big + n_small]
        osmall_refs = refs[2 * n_big + n_small:2 * (n_big + n_small)]
        ici_send, ici_recv, d2d_send, d2d_recv, small_send, small_recv = refs[2 * (n_big + n_small):]
        x, y, c = _position()
        mine = 2 * x + y

        def half(k, core):
            return pl.ds(pl.multiple_of(core * halves[k], 16), halves[k])

        sends = []
        for j, (fx, fy) in enumerate(OTHER_CHIPS):
            peer = (_flip(x, fx), _flip(y, fy), c)
            for k in range(n_big):
                sends.append(_remote(big_refs[k].at[half(k, c)], obig_refs[k].at[mine, half(k, c)],
                                     ici_send.at[j, k], ici_recv.at[j, k], peer))
            for k in range(n_small):
                sends.append(_remote(small_refs[k], osmall_refs[k].at[mine], small_send.at[j, k], small_recv.at[j, k], peer))
        for cp in sends:
            cp.start()
        for j, (fx, fy) in enumerate(OTHER_CHIPS):
            px, py = _flip(x, fx), _flip(y, fy)
            src_chip = 2 * px + py
            for k in range(n_big):
                landed = obig_refs[k].at[src_chip, half(k, c)]
                _remote(landed, landed, ici_send.at[j, k], ici_recv.at[j, k], (px, py, c)).wait_recv()
                fwd = _remote(landed, landed, d2d_send.at[j, k], d2d_recv.at[j, k], (x, y, 1 - c))
                fwd.start()
                sends.append(fwd)
        for j, (fx, fy) in enumerate(OTHER_CHIPS):
            px, py = _flip(x, fx), _flip(y, fy)
            src_chip = 2 * px + py
            for k in range(n_big):
                other = obig_refs[k].at[src_chip, half(k, 1 - c)]
                _remote(other, other, d2d_send.at[j, k], d2d_recv.at[j, k], (x, y, 1 - c)).wait_recv()
            for k in range(n_small):
                dst = osmall_refs[k].at[src_chip]
                _remote(small_refs[k], dst, small_send.at[j, k], small_recv.at[j, k], (px, py, c)).wait_recv()
        for cp in sends:
            cp.wait_send()

    arrays = list(big) + list(small)
    dma = pltpu.SemaphoreType.DMA
    return pl.pallas_call(
        body, name="gather_shards", in_specs=[ANY] * len(arrays), out_specs=[ANY] * len(arrays),
        out_shape=[jax.ShapeDtypeStruct((4,) + a.shape, a.dtype) for a in arrays],
        scratch_shapes=[dma((3, n_big)), dma((3, n_big)), dma((3, n_big)), dma((3, n_big)),
                        dma((3, n_small)), dma((3, n_small))],
        compiler_params=_params(),
    )(*arrays)


def allreduce_small(pack):
    rows = pack.shape[0]
    half = rows // 2

    def body(p_ref, o_ref, sibling_pack, chip_sum, chip_halves, total, sems):
        x, y, c = _position()
        sibling = (x, y, 1 - c)
        swap = _remote(p_ref, sibling_pack, sems.at[0, 0], sems.at[1, 0], sibling)
        swap.start()
        swap.wait()
        chip_sum[...] = p_ref[...] + sibling_pack[...]
        mine = pl.ds(pl.multiple_of(c * half, 8), half)
        other = pl.ds(pl.multiple_of((1 - c) * half, 8), half)
        chip = 2 * x + y
        chip_halves[chip] = chip_sum[mine, :]
        sends = []
        for j, (fx, fy) in enumerate(OTHER_CHIPS):
            sends.append(_remote(chip_sum.at[mine], chip_halves.at[chip], sems.at[0, 1 + j], sems.at[1, 1 + j],
                                 (_flip(x, fx), _flip(y, fy), c)))
        for cp in sends:
            cp.start()
        for j, (fx, fy) in enumerate(OTHER_CHIPS):
            px, py = _flip(x, fx), _flip(y, fy)
            _remote(chip_sum.at[mine], chip_halves.at[2 * px + py], sems.at[0, 1 + j], sems.at[1, 1 + j], (px, py, c)).wait_recv()
        for cp in sends:
            cp.wait_send()
        total[...] = ((chip_halves[0] + chip_halves[1]) + chip_halves[2]) + chip_halves[3]
        o_ref[mine, :] = total[...]
        share = _remote(total, o_ref.at[mine], sems.at[0, 4], sems.at[1, 4], sibling)
        share.start()
        _remote(total, o_ref.at[other], sems.at[0, 4], sems.at[1, 4], sibling).wait_recv()
        share.wait_send()

    vmem = pl.BlockSpec(memory_space=pltpu.VMEM)
    return pl.pallas_call(
        body, name="allreduce_small", in_specs=[vmem], out_specs=vmem,
        out_shape=jax.ShapeDtypeStruct(pack.shape, F32),
        scratch_shapes=[pltpu.VMEM((rows, LANE), F32), pltpu.VMEM((rows, LANE), F32), pltpu.VMEM((4, half, LANE), F32),
                        pltpu.VMEM((half, LANE), F32), pltpu.SemaphoreType.DMA((2, 5))],
        compiler_params=_params(),
    )(pack)


def exchange_halves(gs, *, name):
    n = len(gs)

    def body(*refs):
        g_refs, o_refs, (send_sems, recv_sems) = refs[:n], refs[n:2 * n], refs[2 * n:]
        x, y, c = _position()
        copies = [_remote(g_refs[k].at[:, 1 - c], o_refs[k], send_sems.at[k], recv_sems.at[k], (x, y, 1 - c)) for k in range(n)]
        for cp in copies:
            cp.start()
        for cp in copies:
            cp.wait()

    return pl.pallas_call(
        body, name=name, in_specs=[ANY] * n, out_specs=[ANY] * n,
        out_shape=[jax.ShapeDtypeStruct((4,) + g.shape[2:], g.dtype) for g in gs],
        scratch_shapes=[pltpu.SemaphoreType.DMA((n,)), pltpu.SemaphoreType.DMA((n,))],
        compiler_params=_params(),
    )(*gs)


def scatter_to_chips(parts):
    n = len(parts)

    def body(*refs):
        a_refs, o_refs, (send_sems, recv_sems) = refs[:n], refs[n:2 * n], refs[2 * n:]
        x, y, c = _position()
        copies = []
        for j, (fx, fy) in enumerate(OTHER_CHIPS):
            px, py = _flip(x, fx), _flip(y, fy)
            for k in range(n):
                copies.append(_remote(a_refs[k].at[2 * px + py], o_refs[k].at[j], send_sems.at[j, k], recv_sems.at[j, k], (px, py, c)))
        for cp in copies:
            cp.start()
        for cp in copies:
            cp.wait()

    return pl.pallas_call(
        body, name="scatter_to_chips", in_specs=[ANY] * n, out_specs=[ANY] * n,
        out_shape=[jax.ShapeDtypeStruct((3,) + a.shape[1:], a.dtype) for a in parts],
        scratch_shapes=[pltpu.SemaphoreType.DMA((3, n)), pltpu.SemaphoreType.DMA((3, n))],
        compiler_params=_params(),
    )(*parts)


def swap_halves(rs):
    n = len(rs)

    def body(*refs):
        r_refs, o_refs, (send_sems, recv_sems) = refs[:n], refs[n:2 * n], refs[2 * n:]
        x, y, c = _position()
        copies = [_remote(r_refs[k], o_refs[k], send_sems.at[k], recv_sems.at[k], (x, y, 1 - c)) for k in range(n)]
        for cp in copies:
            cp.start()
        for cp in copies:
            cp.wait()

    dma = pltpu.SemaphoreType.DMA
    return pl.pallas_call(
        body, name="swap_halves", in_specs=[ANY] * n, out_specs=[ANY] * n,
        out_shape=[jax.ShapeDtypeStruct(r.shape, r.dtype) for r in rs],
        scratch_shapes=[dma((n,)), dma((n,))],
        compiler_params=_params(),
    )(*rs)


HBM = pl.BlockSpec(memory_space=pltpu.HBM)
SEM = pl.BlockSpec(memory_space=pltpu.SEMAPHORE)
SIDE_EFFECT = pltpu.SideEffectType.DATAFLOW_SIDE_EFFECTING


def _gather_plan(halves):
    def plan(v_refs, land_refs, x, y, c):
        copies = []
        for fx, fy in OTHER_CHIPS:
            for k in range(len(v_refs)):
                rows = pl.ds(pl.multiple_of(c * halves[k], 16), halves[k])
                copies.append((v_refs[k].at[rows], land_refs[k].at[2 * x + y, rows], (_flip(x, fx), _flip(y, fy), c)))
        return copies
    return plan


def _scatter_plan(v_refs, land_refs, x, y, c):
    copies = []
    for j, (fx, fy) in enumerate(OTHER_CHIPS):
        px, py = _flip(x, fx), _flip(y, fy)
        for k in range(len(v_refs)):
            copies.append((v_refs[k].at[2 * px + py], land_refs[k].at[j], (px, py, c)))
    return copies


def chip_exchange_start(srcs, land_shapes, plan, after, *, name):
    n = len(srcs)
    n_cp = 3 * n

    def body(*refs):
        v_refs, land_refs = refs[:n], refs[n:2 * n]
        outs = refs[2 * n + 1:]
        sends, recvs, token = outs[:n_cp], outs[n_cp:2 * n_cp], outs[-1]
        x, y, c = _position()
        for (src, dst, device), send, recv in zip(plan(v_refs, land_refs, x, y, c), sends, recvs, strict=True):
            _remote(src, dst, send, recv, device).start()
        token[...] = jnp.zeros_like(token)

    lands = [lax.empty(shape, v.dtype) for shape, v in zip(land_shapes, srcs)]
    arrays = [pltpu.with_memory_space_constraint(a, pltpu.HBM) for a in list(srcs) + lands]
    outs = pl.pallas_call(
        body, name=name,
        out_shape=tuple(pltpu.SemaphoreType.DMA(()) for _ in range(2 * n_cp))
        + tuple(pltpu.HBM(a.shape, a.dtype) for a in arrays) + (jax.ShapeDtypeStruct((8, LANE), F32),),
        in_specs=[HBM] * (2 * n) + [ANY],
        out_specs=(SEM,) * (2 * n_cp) + (HBM,) * (2 * n) + (pl.BlockSpec(memory_space=pltpu.VMEM),),
        input_output_aliases={i: 2 * n_cp + i for i in range(2 * n)},
        compiler_params=pltpu.CompilerParams(has_side_effects=SIDE_EFFECT),
    )(*arrays, after)
    handle = (outs[:n_cp], outs[n_cp:2 * n_cp], outs[2 * n_cp:2 * n_cp + n], outs[2 * n_cp + n:2 * n_cp + 2 * n])
    return handle, outs[-1]


def chip_exchange_wait(handle, plan, after, *, name):
    sends, recvs, v_thru, land_thru = handle
    n = len(v_thru)
    n_cp = 3 * n

    def body(*refs):
        v_refs, land_refs = refs[:n], refs[n:2 * n]
        sends, recvs = refs[2 * n:2 * n + n_cp], refs[2 * n + n_cp:2 * n + 2 * n_cp]
        x, y, c = _position()
        for (src, dst, device), send, recv in zip(plan(v_refs, land_refs, x, y, c), sends, recvs, strict=True):
            copy = _remote(src, dst, send, recv, device)
            copy.wait_send()
            copy.wait_recv()

    outs = pl.pallas_call(
        body, name=name,
        out_shape=tuple(pltpu.HBM(a.shape, a.dtype) for a in list(v_thru) + list(land_thru)),
        in_specs=[HBM] * (2 * n) + [SEM] * (2 * n_cp) + [ANY], out_specs=(HBM,) * (2 * n),
        input_output_aliases={i: i for i in range(2 * n)},
        compiler_params=pltpu.CompilerParams(has_side_effects=SIDE_EFFECT),
    )(*v_thru, *land_thru, *sends, *recvs, after)
    return outs[:n], outs[n:]


def finish_gather(lands):
    n = len(lands)
    halves = [a.shape[1] // 2 for a in lands]

    def body(*refs):
        o_refs, (send_sems, recv_sems) = refs[n:2 * n], refs[2 * n:]
        x, y, c = _position()

        def half(k, core):
            return pl.ds(pl.multiple_of(core * halves[k], 16), halves[k])

        sends = []
        for j, (fx, fy) in enumerate(OTHER_CHIPS):
            src_chip = 2 * _flip(x, fx) + _flip(y, fy)
            for k in range(n):
                held = o_refs[k].at[src_chip, half(k, c)]
                sends.append(_remote(held, held, send_sems.at[j, k], recv_sems.at[j, k], (x, y, 1 - c)))
        for cp in sends:
            cp.start()
        for j, (fx, fy) in enumerate(OTHER_CHIPS):
            src_chip = 2 * _flip(x, fx) + _flip(y, fy)
            for k in range(n):
                other = o_refs[k].at[src_chip, half(k, 1 - c)]
                _remote(other, other, send_sems.at[j, k], recv_sems.at[j, k], (x, y, 1 - c)).wait_recv()
        for cp in sends:
            cp.wait_send()

    dma = pltpu.SemaphoreType.DMA
    return pl.pallas_call(
        body, name="finish_gather", in_specs=[ANY] * n, out_specs=[ANY] * n,
        out_shape=[jax.ShapeDtypeStruct(a.shape, a.dtype) for a in lands],
        input_output_aliases={k: k for k in range(n)},
        scratch_shapes=[dma((3, n)), dma((3, n))],
        compiler_params=_params(),
    )(*lands)


def add_own_half(g, r, c_idx, *, name):
    _, _, h, cols = g.shape

    def body(c_ref, g_ref, r_ref, o_ref):
        o_ref[...] = (g_ref[0] + r_ref[...]).astype(o_ref.dtype)

    return pl.pallas_call(
        body, name=name,
        grid_spec=pltpu.PrefetchScalarGridSpec(
            num_scalar_prefetch=1, grid=(4,),
            in_specs=[pl.BlockSpec((1, 1, h, cols), lambda s, c: (s, c[0], 0, 0)),
                      pl.BlockSpec((1, h, cols), lambda s, c: (s, 0, 0))],
            out_specs=pl.BlockSpec((1, h, cols), lambda s, c: (s, 0, 0))),
        out_shape=jax.ShapeDtypeStruct(r.shape, BF16), compiler_params=_params(("parallel",)),
    )(c_idx, g, r)


def add_chip_parts(a, parts, chip_idx, *, name):
    _, h, cols = a.shape
    th = h // 2

    def body(s_ref, a_ref, p0_ref, p1_ref, p2_ref, o_ref):
        f = lambda r: r[0].astype(F32)
        o_ref[...] = ((f(a_ref) + f(p0_ref)) + f(p1_ref)) + f(p2_ref)

    part = lambda j: pl.BlockSpec((1, th, cols), lambda i, s, j=j: (j, i, 0))
    return pl.pallas_call(
        body, name=name,
        grid_spec=pltpu.PrefetchScalarGridSpec(
            num_scalar_prefetch=1, grid=(2,),
            in_specs=[pl.BlockSpec((1, th, cols), lambda i, s: (s[0], i, 0)), part(0), part(1), part(2)],
            out_specs=pl.BlockSpec((th, cols), lambda i, s: (i, 0))),
        out_shape=jax.ShapeDtypeStruct((h, cols), F32), compiler_params=_params(("parallel",)),
    )(chip_idx, a, parts, parts, parts)


WEIGHTS = ("mix_norm", "e_w_in", "e_conv_w", "e_conv_b", "e_dt_bias", "e_a_log", "e_d", "e_ssm_norm", "e_w_out",
           "o_w_in", "o_dw_w", "o_dw_b", "o_ln_g", "o_ln_b", "o_a_re", "o_a_im", "o_b_re", "o_b_im", "o_c_re",
           "o_c_im", "o_d", "o_log_step", "o_glu_w", "o_w_out", "ffn_norm", "ffn_w_up", "ffn_dw_w", "ffn_dw_b",
           "ffn_w_down", "final_norm")
BIG = (("e_w_in", 2), ("e_w_out", 1), ("o_w_in", 2), ("o_glu_w", 1), ("o_w_out", 1), ("ffn_w_up", 2), ("ffn_w_down", 1))
SMALL_SHARDED = (("e_conv_w", 2), ("o_dw_w", 2), ("o_dw_b", 1), ("o_ln_g", 1), ("o_ln_b", 1), ("o_d", 1), ("ffn_dw_w", 2))
REPLICATED = tuple(n for n in WEIGHTS if n not in dict(BIG + SMALL_SHARDED))
PACK_ROWS = 8


def _pack(arrays, dtype, row_unit=PACK_ROWS):
    flat = jnp.concatenate([a.astype(dtype).reshape(-1) for a in arrays])
    rows = -(-flat.size // (LANE * row_unit)) * row_unit
    return jnp.pad(flat, (0, rows * LANE - flat.size)).reshape(rows, LANE)


def _unpack(flat, shapes, lead=()):
    out, off = [], 0
    for shape in shapes:
        size = int(np.prod(shape))
        out.append(flat[..., off:off + size].reshape(lead + tuple(shape)))
        off += size
    return out


def _join_shards(parts, axis):
    return jnp.concatenate([parts[s] for s in range(4)], axis=axis)


def _split_shards(full, axis):
    return jnp.stack(jnp.split(full, 4, axis=axis))


def _rows2d(a):
    return a.reshape(-1, a.shape[-1])


def _layer_shards(g, axis):
    if g.ndim == 3:
        return g.reshape(4, 2, g.shape[1] // 2, g.shape[2])
    rows, cols = g.shape
    if axis == 0:
        return g.reshape(4, 2, rows // 8, cols)
    return g.reshape(rows, 4, cols // 4).transpose(1, 0, 2).reshape(4, 2, rows // 2, cols // 4)


def kernel(x, mix_norm, e_w_in, e_conv_w, e_conv_b, e_dt_bias, e_a_log, e_d, e_ssm_norm, e_w_out, o_w_in, o_dw_w, o_dw_b, o_ln_g, o_ln_b, o_a_re, o_a_im, o_b_re, o_b_im, o_c_re, o_c_im, o_d, o_log_step, o_glu_w, o_w_out, ffn_norm, ffn_w_up, ffn_dw_w, ffn_dw_b, ffn_w_down, final_norm, loss_target, m_mix_norm, m_e_w_in, m_e_conv_w, m_e_conv_b, m_e_dt_bias, m_e_a_log, m_e_d, m_e_ssm_norm, m_e_w_out, m_o_w_in, m_o_dw_w, m_o_dw_b, m_o_ln_g, m_o_ln_b, m_o_a_re, m_o_a_im, m_o_b_re, m_o_b_im, m_o_c_re, m_o_c_im, m_o_d, m_o_log_step, m_o_glu_w, m_o_w_out, m_ffn_norm, m_ffn_w_up, m_ffn_dw_w, m_ffn_dw_b, m_ffn_w_down, m_final_norm, v_mix_norm, v_e_w_in, v_e_conv_w, v_e_conv_b, v_e_dt_bias, v_e_a_log, v_e_d, v_e_ssm_norm, v_e_w_out, v_o_w_in, v_o_dw_w, v_o_dw_b, v_o_ln_g, v_o_ln_b, v_o_a_re, v_o_a_im, v_o_b_re, v_o_b_im, v_o_c_re, v_o_c_im, v_o_d, v_o_log_step, v_o_glu_w, v_o_w_out, v_ffn_norm, v_ffn_w_up, v_ffn_dw_w, v_ffn_dw_b, v_ffn_w_down, v_final_norm):
    given = dict(locals())
    chip = 2 * lax.axis_index("x") + lax.axis_index("y")
    core = lax.axis_index("c")

    core_idx, chip_idx = core.reshape(1).astype(jnp.int32), chip.reshape(1).astype(jnp.int32)

    def whole(n, axis, parts):
        shape = given[n].shape
        own = given[n].astype(parts.dtype)
        return _join_shards(lax.dynamic_update_index_in_dim(parts.reshape((4,) + shape), own, chip, 0), axis)

    first, later = BIG[:1], BIG[1:]
    shards = {n: _rows2d(given[n]).astype(BF16) for n, _ in BIG}
    gathered = gather_shards([shards[n] for n, _ in first], [_rows2d(given[n]) for n, _ in SMALL_SHARDED])
    w = {n: given[n] for n in REPLICATED}
    for (n, axis), parts in zip(first + SMALL_SHARDED, gathered):
        w[n] = whole(n, axis, parts)
    later_keys = [(n, layer) for n, _ in later for layer in (range(2) if n.startswith("ffn_") else [None])]
    later_shards = [shards[n] if layer is None else given[n][layer].astype(BF16) for n, layer in later_keys]
    gather_plan = _gather_plan([a.shape[0] // 2 for a in later_shards])
    gather_handle, token = chip_exchange_start(later_shards, [(4,) + a.shape for a in later_shards], gather_plan,
                                               gathered[0], name="gather_start")
    w["mix_norm"] = w["mix_norm"] + token[0, 0]

    def late_weights(after):
        _, lands = chip_exchange_wait(gather_handle, gather_plan, after, name="gather_wait")
        out = {"ffn_w_up_pairs": [], "ffn_w_down": []}
        for (n, layer), own, parts in zip(later_keys, later_shards, finish_gather(lands)):
            if layer is None:
                out[n] = whole(n, dict(BIG)[n], parts)
                continue
            parts = lax.dynamic_update_index_in_dim(parts, own, chip, 0)
            if n == "ffn_w_up":
                out["ffn_w_up_pairs"].append(ffn_shards_to_pairs(parts, name=f"ffn{layer}_up_pairs"))
            else:
                out[n].append(parts.reshape(-1, parts.shape[-1]))
        return out

    groups = []

    def finish_group(after):
        group = groups[-1]
        group["sums"], group["parts"] = chip_exchange_wait(group.pop("handle"), _scatter_plan, after,
                                                           name=f"scatter_wait_{len(groups) - 1}")

    def early_reduce(layer_grads):
        keys = list(layer_grads)
        if groups:
            finish_group(layer_grads[keys[0]])
        tag = len(groups)
        parts = [_layer_shards(layer_grads[k], dict(BIG)[k[0]] - 1) for k in keys]
        sums = [add_own_half(g, r, core_idx, name=f"add_own_half_{n}{layer}")
                for g, r, (n, layer) in zip(parts, exchange_halves(parts, name=f"exchange_halves_{tag}"), keys)]
        handle, zeros = chip_exchange_start(sums, [(3,) + a.shape[1:] for a in sums], _scatter_plan, sums[0],
                                            name=f"scatter_start_{tag}")
        groups.append(dict(keys=keys, handle=handle))
        return zeros[0, 0]

    loss, dx, grads = local_step(x[0], loss_target[0], w, late_weights, early_reduce)
    finish_group(dx)
    early_keys = [k for group in groups for k in group["keys"]]
    early_sums = [a for group in groups for a in group["sums"]]
    early_parts = [a for group in groups for a in group["parts"]]

    small_names = REPLICATED + tuple(n for n, _ in SMALL_SHARDED)
    small_sum = allreduce_small(_pack([grads[n] for n in small_names], F32, row_unit=16))
    reduced = dict(zip(small_names, _unpack(small_sum.reshape(-1), [grads[n].shape for n in small_names])))
    for n, axis in SMALL_SHARDED:
        width = given[n].shape[axis]
        reduced[n] = lax.dynamic_slice_in_dim(reduced[n], chip * width, width, axis=axis)

    keys, parts = [], []
    for n, axis in BIG:
        for layer, g in enumerate(grads[n]):
            if (n, layer) not in early_keys:
                keys.append((n, layer))
                parts.append(_layer_shards(g, axis - 1))
    core_sums, chip_parts = [], []
    if parts:
        core_sums = [add_own_half(g, r, core_idx, name=f"add_own_half_{n}{layer}")
                     for g, r, (n, layer) in zip(parts, exchange_halves(parts, name="exchange_halves_last"), keys)]
        chip_parts = list(scatter_to_chips(core_sums))
    keys, core_sums, chip_parts = early_keys + keys, early_sums + core_sums, early_parts + chip_parts
    mine = [add_chip_parts(a, p, chip_idx, name=f"add_chip_parts_{n}{layer}")
            for a, p, (n, layer) in zip(core_sums, chip_parts, keys)]
    layers = {}
    for (n, layer), own, other in zip(keys, mine, swap_halves(mine)):
        both = jnp.where(core == 0, jnp.stack([own, other]), jnp.stack([other, own]))
        layers.setdefault(n, {})[layer] = both.reshape(given[n].shape[1:])
    for n, _ in BIG:
        reduced[n] = jnp.stack([layers[n][layer] for layer in sorted(layers[n])])

    delta, new_m, new_v = {}, {}, {}
    for n in WEIGHTS:
        delta[n], new_m[n], new_v[n] = adamw(given[n], reduced[n], given["m_" + n], given["v_" + n], name="adamw_" + n)

    total = lax.psum(loss[0, 0], ("x", "y", "c"))
    return (total, dx[None], *[reduced[n] for n in WEIGHTS], *[delta[n] for n in WEIGHTS],
            *[new_m[n] for n in WEIGHTS], *[new_v[n] for n in WEIGHTS])
```

```python
import functools
import math
from typing import NamedTuple

import numpy as np
import jax
import jax.numpy as jnp
from jax import lax
from jax.experimental import pallas as pl
from jax.experimental.pallas import tpu as pltpu

F32 = jnp.float32
BF16 = jnp.bfloat16
HIGHEST = lax.Precision.HIGHEST
MESH = pl.DeviceIdType.MESH

D_MODEL = 1024
EPS = 1e-6
RET_HEADS, RET_DK, RET_DV, CHUNK = 4, 128, 256, 128
ROPE_BASE = 10000.0
SSM_HEADS, SSM_P, SSM_N, SSM_GROUPS = 16, 64, 128, 2
SSM_DINNER = SSM_HEADS * SSM_P
EVEN_IN, EVEN_IN_PAD = 5648, 5760
S5_GROUPS, S5_GROUP, S5_STATE = 32, 16, 64
S5_LANES = S5_GROUPS * S5_STATE
SCAN_SEG = 32
D_FF = 2816
ADAM_LR, ADAM_B1, ADAM_B2, ADAM_EPS, ADAM_WD, ADAM_STEP = 0.001, 0.9, 0.999, 1e-08, 0.01, 10

LANE = 128
VMEM_LIMIT = 56 * 1024 * 1024


def _params(sem=None, **kw):
    return pltpu.CompilerParams(dimension_semantics=sem, vmem_limit_bytes=VMEM_LIMIT, **kw)


def _tile(n, target, unit=LANE):
    if n <= target:
        return n
    t = (target // unit) * unit
    while t >= unit:
        if n % t == 0:
            return t
        t -= unit
    return n


def _silu(x):
    return x * jax.nn.sigmoid(x)


def _mm(a, b):
    return jnp.dot(a.astype(BF16), b.astype(BF16), preferred_element_type=F32)


def _mm_nt(a, b):
    return lax.dot_general(a.astype(BF16), b.astype(BF16), (((1,), (1,)), ((), ())), preferred_element_type=F32)


def _mm_tn(a, b):
    return lax.dot_general(a.astype(BF16), b.astype(BF16), (((0,), (0,)), ((), ())), preferred_element_type=F32)


def _dot_hi(a, b):
    return jnp.dot(a, b, precision=HIGHEST, preferred_element_type=F32)


def _dot_hi_tn(a, b):
    return lax.dot_general(a, b, (((0,), (0,)), ((), ())), precision=HIGHEST, preferred_element_type=F32)


def _bf16_parts(v):
    hi = v.astype(BF16)
    rest = v - hi.astype(F32)
    mid = rest.astype(BF16)
    return hi, mid, (rest - mid.astype(F32)).astype(BF16)


def _dot_parts(v, fixed, dims, v_first):
    fixed = fixed.astype(BF16)
    out = None
    for part in _bf16_parts(v):
        ops = (part, fixed) if v_first else (fixed, part)
        p = lax.dot_general(*ops, (dims, ((), ())), preferred_element_type=F32)
        out = p if out is None else out + p
    return out


@jax.custom_vjp
def _times_01(v, ones):
    return _dot_parts(v, ones, ((1,), (0,)), True)


_times_01.defvjp(lambda v, ones: (_times_01(v, ones), ones),
                 lambda ones, g: (_dot_parts(g, ones, ((1,), (1,)), True), jnp.zeros_like(ones)))


@jax.custom_vjp
def _01_times(ones, v):
    return _dot_parts(v, ones, ((1,), (0,)), False)


_01_times.defvjp(lambda ones, v: (_01_times(ones, v), ones),
                 lambda ones, g: (jnp.zeros_like(ones), _dot_parts(g, ones, ((0,), (0,)), False)))


MATMUL_VMEM = 44 * 1024 * 1024


def matmul(a, b, *, ta=False, tb=False, res=None, out_dtype=F32, name):
    m, k = (a.shape[1], a.shape[0]) if ta else a.shape
    n = b.shape[0] if tb else b.shape[1]
    assert (b.shape[1] if tb else b.shape[0]) == k, (a.shape, b.shape, ta, tb)
    tm = _tile(m, 1536)
    tn = _tile(n, 640)
    if tn < 384:
        tn = _tile(n, 1536)
    res_bytes = 0 if res is None else res.dtype.itemsize

    def vmem(tm, tn):
        return 2 * (tm * k * a.dtype.itemsize + tn * k * b.dtype.itemsize + tm * tn * (jnp.dtype(out_dtype).itemsize + res_bytes))

    while vmem(tm, tn) > MATMUL_VMEM and tm % (2 * LANE) == 0:
        tm //= 2
    assert vmem(tm, tn) <= MATMUL_VMEM, (name, tm, tn, k)
    a_spec = pl.BlockSpec((k, tm), lambda i, j: (0, i)) if ta else pl.BlockSpec((tm, k), lambda i, j: (i, 0))
    b_spec = pl.BlockSpec((tn, k), lambda i, j: (j, 0)) if tb else pl.BlockSpec((k, tn), lambda i, j: (0, j))
    o_spec = pl.BlockSpec((tm, tn), lambda i, j: (i, j))
    dims = (((0 if ta else 1,), (1 if tb else 0,)), ((), ()))
    has_res = res is not None

    def body(a_ref, b_ref, *rest):
        o_ref = rest[-1]
        out = lax.dot_general(a_ref[...].astype(BF16), b_ref[...].astype(BF16), dims, preferred_element_type=F32)
        if has_res:
            out = out + rest[0][...].astype(F32)
        o_ref[...] = out.astype(o_ref.dtype)

    ins = [a, b] + ([res] if has_res else [])
    specs = [a_spec, b_spec] + ([o_spec] if has_res else [])
    return pl.pallas_call(
        body, name=name, grid=(m // tm, n // tn), in_specs=specs, out_specs=o_spec,
        out_shape=jax.ShapeDtypeStruct((m, n), out_dtype), compiler_params=_params(("parallel", "parallel")),
    )(*ins)


class Cols(NamedTuple):
    arr: jax.Array
    w: int
    j: int


def _cols(a):
    return a if isinstance(a, Cols) else Cols(a, a.shape[1], 0)


def _row_spec(c, tl):
    return pl.BlockSpec((tl, c.w), lambda i, j=c.j: (i, j))


def _whole_spec(p):
    return pl.BlockSpec(p.shape, lambda i, nd=p.ndim: (0,) * nd)


def rowwise_fwd(fn, rows, aux, pars, consts, outs, *, name, tl):
    rows = [_cols(r) for r in rows + aux]
    whole = list(pars) + list(consts)
    n_rows = len(rows)
    n_whole = len(whole)
    length = rows[0].arr.shape[0]
    tl = min(tl, length)

    def body(*refs):
        vals = [r[...].astype(F32) for r in refs[:n_rows]] + [r[...] for r in refs[n_rows:n_rows + n_whole]]
        res = fn(*vals)
        for o_ref, v in zip(refs[n_rows + n_whole:], res, strict=True):
            o_ref[...] = v.astype(o_ref.dtype)

    return pl.pallas_call(
        body, name=name, grid=(length // tl,),
        in_specs=[_row_spec(r, tl) for r in rows] + [_whole_spec(p) for p in whole],
        out_specs=[pl.BlockSpec((tl, w), lambda i: (i, 0)) for w, _ in outs],
        out_shape=[jax.ShapeDtypeStruct((length, w), dt) for w, dt in outs],
        compiler_params=_params(("parallel",)),
    )(*[r.arr for r in rows], *whole)


def rowwise_bwd(fn, rows, aux, pars, consts, cots, drow_dtypes, *, name, tl, add=None, merge=False):
    rows = [_cols(r) for r in rows]
    aux = [_cols(r) for r in aux]
    cots = [_cols(r) for r in cots]
    n_r, n_a, n_p, n_c, n_t = len(rows), len(aux), len(pars), len(consts), len(cots)
    length = rows[0].arr.shape[0]
    tl = min(tl, length)
    has_add = add is not None
    widths = [r.w for r in rows]

    def body(*refs):
        pos = 0
        r_vals = [r[...].astype(F32) for r in refs[pos:pos + n_r]]; pos += n_r
        a_vals = [r[...].astype(F32) for r in refs[pos:pos + n_a]]; pos += n_a
        p_vals = [r[...].astype(F32) for r in refs[pos:pos + n_p]]; pos += n_p
        c_vals = [r[...] for r in refs[pos:pos + n_c]]; pos += n_c
        t_vals = [r[...].astype(F32) for r in refs[pos:pos + n_t]]; pos += n_t
        add_val = None
        if has_add:
            add_val = refs[pos][...].astype(F32); pos += 1
        n_dr = 1 if merge else n_r
        dr_refs = refs[pos:pos + n_dr]; pos += n_dr
        dp_refs = refs[pos:pos + n_p]

        def f(*rp):
            return fn(*rp[:n_r], *a_vals, *rp[n_r:], *c_vals)

        _, vjp = jax.vjp(f, *r_vals, *p_vals)
        grads = vjp(tuple(t_vals))
        drows = list(grads[:n_r])
        if has_add:
            drows[0] = drows[0] + add_val
        if merge:
            off = 0
            for w, d in zip(widths, drows):
                dr_refs[0][:, off:off + w] = d.astype(dr_refs[0].dtype)
                off += w
        else:
            for r, d in zip(dr_refs, drows):
                r[...] = d.astype(r.dtype)
        i = pl.program_id(0)
        for r, d in zip(dp_refs, grads[n_r:]):
            @pl.when(i == 0)
            def _(r=r, d=d):
                r[...] = d

            @pl.when(i > 0)
            def _(r=r, d=d):
                r[...] += d

    if merge:
        dr_specs = [pl.BlockSpec((tl, sum(widths)), lambda i: (i, 0))]
        dr_shapes = [jax.ShapeDtypeStruct((length, sum(widths)), drow_dtypes[0])]
    else:
        dr_specs = [pl.BlockSpec((tl, w), lambda i: (i, 0)) for w in widths]
        dr_shapes = [jax.ShapeDtypeStruct((length, w), dt) for w, dt in zip(widths, drow_dtypes)]
    ins = [r.arr for r in rows + aux] + list(pars) + list(consts) + [r.arr for r in cots] + ([add] if has_add else [])
    specs = ([_row_spec(r, tl) for r in rows + aux] + [_whole_spec(p) for p in list(pars) + list(consts)]
             + [_row_spec(r, tl) for r in cots] + ([pl.BlockSpec((tl, add.shape[1]), lambda i: (i, 0))] if has_add else []))
    return pl.pallas_call(
        body, name=name, grid=(length // tl,), in_specs=specs,
        out_specs=dr_specs + [_whole_spec(p) for p in pars],
        out_shape=dr_shapes + [jax.ShapeDtypeStruct(p.shape, F32) for p in pars],
        compiler_params=_params(("arbitrary",)),
    )(*ins)


def whole_fwd(fn, ins, out_shapes, *, name):
    n_in = len(ins)

    def body(*refs):
        res = fn(*[r[...] for r in refs[:n_in]])
        for o_ref, v in zip(refs[n_in:], res, strict=True):
            o_ref[...] = v

    return pl.pallas_call(body, name=name, out_shape=[jax.ShapeDtypeStruct(s, F32) for s in out_shapes],
                          compiler_params=_params())(*ins)


def whole_bwd(fn, ins, n_diff, cots, *, name):
    n_in, n_t = len(ins), len(cots)

    def body(*refs):
        vals = [r[...] for r in refs[:n_in]]
        t_vals = [r[...] for r in refs[n_in:n_in + n_t]]
        _, vjp = jax.vjp(lambda *d: fn(*d, *vals[n_diff:]), *vals[:n_diff])
        for o_ref, g in zip(refs[n_in + n_t:], vjp(tuple(t_vals)), strict=True):
            o_ref[...] = g

    return pl.pallas_call(body, name=name, out_shape=[jax.ShapeDtypeStruct(a.shape, F32) for a in ins[:n_diff]],
                          compiler_params=_params())(*ins, *cots)


CONV_ROWS = 256


def _conv_geometry(x, w, cw, off):
    width = w.shape[1]
    x = Cols(x, width, 0)
    length = x.arr.shape[0]
    taps = w.shape[0]
    pad = -(-(taps - 1) // 8) * 8
    assert off % cw == 0 and width % cw == 0, (off, width, cw)
    return x, length, taps, pad, off // cw


def _conv_taps(xp_ref, w_ref, base, taps, pad, init, lanes=slice(None)):
    acc = init
    for k in range(taps):
        acc = acc + w_ref[k:k + 1, lanes] * xp_ref[pl.ds(base + pad - (taps - 1) + k, init.shape[0]), :]
    return acc


def conv_fwd(x, w, b, *, act, name, off=0, cw=LANE, out_dtype=F32):
    x, length, taps, pad, jb = _conv_geometry(x, w, cw, off)
    rc = min(CONV_ROWS, length)

    def body(x_ref, w_ref, b_ref, o_ref, xp_ref):
        xp_ref[0:pad, :] = jnp.zeros((pad, cw), F32)
        xp_ref[pad:pad + length, :] = x_ref[...].astype(F32)

        def chunk(r, carry):
            base = pl.multiple_of(r * rc, rc)
            acc = _conv_taps(xp_ref, w_ref, base, taps, pad, jnp.broadcast_to(b_ref[...], (rc, cw)))
            if act:
                acc = _silu(acc)
            o_ref[pl.ds(base, rc), :] = acc.astype(o_ref.dtype)
            return carry

        lax.fori_loop(0, length // rc, chunk, 0)

    return pl.pallas_call(
        body, name=name, grid=(x.w // cw,),
        in_specs=[pl.BlockSpec((length, cw), lambda j: (0, jb + j)), pl.BlockSpec((taps, cw), lambda j: (0, j)),
                  pl.BlockSpec((1, cw), lambda j: (0, j))],
        out_specs=pl.BlockSpec((length, cw), lambda j: (0, j)),
        out_shape=jax.ShapeDtypeStruct((length, x.w), out_dtype),
        scratch_shapes=[pltpu.VMEM((pad + length, cw), F32)],
        compiler_params=_params(("parallel",)),
    )(x.arr, w, b)


def conv_bwd(x, w, b, dy, *, act, name, off=0, cw=LANE, dx_dtype=F32):
    x, length, taps, pad, jb = _conv_geometry(x, w, cw, off)
    rc = min(CONV_ROWS, length)

    def body(x_ref, w_ref, b_ref, dy_ref, dx_ref, dw_ref, db_ref, xp_ref, gp_ref):
        xp_ref[0:pad, :] = jnp.zeros((pad, cw), F32)
        xp_ref[pad:pad + length, :] = x_ref[...].astype(F32)
        gp_ref[length:length + pad, :] = jnp.zeros((pad, cw), F32)
        if act:
            def pre_chunk(r, carry):
                base = pl.multiple_of(r * rc, rc)
                pre = _conv_taps(xp_ref, w_ref, base, taps, pad, jnp.broadcast_to(b_ref[...], (rc, cw)))
                sig = jax.nn.sigmoid(pre)
                gp_ref[pl.ds(base, rc), :] = dy_ref[pl.ds(base, rc), :].astype(F32) * (sig * (1.0 + pre * (1.0 - sig)))
                return carry

            lax.fori_loop(0, length // rc, pre_chunk, 0)
        else:
            gp_ref[0:length, :] = dy_ref[...].astype(F32)
        dw_ref[...] = jnp.zeros((taps, cw), F32)
        db_ref[...] = jnp.zeros((1, cw), F32)

        def chunk(r, carry):
            base = pl.multiple_of(r * rc, rc)
            acc = jnp.zeros((rc, cw), F32)
            g = gp_ref[pl.ds(base, rc), :]
            for k in range(taps):
                acc = acc + w_ref[k:k + 1, :] * gp_ref[pl.ds(base + (taps - 1) - k, rc), :]
                xs = xp_ref[pl.ds(base + pad - (taps - 1) + k, rc), :]
                dw_ref[k:k + 1, :] += jnp.sum(g * xs, axis=0, keepdims=True)
            db_ref[...] += jnp.sum(g, axis=0, keepdims=True)
            dx_ref[pl.ds(base, rc), :] = acc.astype(dx_ref.dtype)
            return carry

        lax.fori_loop(0, length // rc, chunk, 0)

    dy = _cols(dy)
    assert dy.j == 0 and dy.w == x.w
    return pl.pallas_call(
        body, name=name, grid=(x.w // cw,),
        in_specs=[pl.BlockSpec((length, cw), lambda j: (0, jb + j)), pl.BlockSpec((taps, cw), lambda j: (0, j)),
                  pl.BlockSpec((1, cw), lambda j: (0, j)), pl.BlockSpec((length, cw), lambda j: (0, j))],
        out_specs=[pl.BlockSpec((length, cw), lambda j: (0, j)), pl.BlockSpec((taps, cw), lambda j: (0, j)),
                   pl.BlockSpec((1, cw), lambda j: (0, j))],
        out_shape=[jax.ShapeDtypeStruct((length, x.w), dx_dtype), jax.ShapeDtypeStruct((taps, x.w), F32),
                   jax.ShapeDtypeStruct((1, x.w), F32)],
        scratch_shapes=[pltpu.VMEM((pad + length, cw), F32), pltpu.VMEM((length + pad, cw), F32)],
        compiler_params=_params(("parallel",)),
    )(x.arr, w, b, dy.arr)


def _conv_transpose(xp_ref, gp_ref, w_ref, dx_ref, dw_ref, db_ref, lanes, length, taps, pad, rc):
    dw_ref[:, lanes] = jnp.zeros((taps, LANE), F32)
    db_ref[:, lanes] = jnp.zeros((1, LANE), F32)

    def chunk(r, carry):
        base = pl.multiple_of(r * rc, rc)
        acc = jnp.zeros((rc, LANE), F32)
        g = gp_ref[pl.ds(base, rc), :]
        for k in range(taps):
            acc = acc + w_ref[k:k + 1, lanes] * gp_ref[pl.ds(base + (taps - 1) - k, rc), :]
            xs = xp_ref[pl.ds(base + pad - (taps - 1) + k, rc), :]
            dw_ref[k:k + 1, lanes] += jnp.sum(g * xs, axis=0, keepdims=True)
        db_ref[:, lanes] += jnp.sum(g, axis=0, keepdims=True)
        dx_ref[pl.ds(base, rc), lanes] = acc.astype(dx_ref.dtype)
        return carry

    lax.fori_loop(0, length // rc, chunk, 0)


LANE_PAIR_ROWS = 1024


def ffn_interleave(a, name=None):
    rows, width = a.shape
    nb = width // (2 * LANE)
    if rows < LANE_PAIR_ROWS:
        return a.reshape(rows, 2, nb, LANE).swapaxes(1, 2).reshape(a.shape)

    def body(g_ref, u_ref, o_ref):
        o_ref[:, 0:LANE] = g_ref[...]
        o_ref[:, LANE:2 * LANE] = u_ref[...]

    tr = LANE_PAIR_ROWS
    return pl.pallas_call(
        body, name=name, grid=(rows // tr, nb),
        in_specs=[pl.BlockSpec((tr, LANE), lambda i, j: (i, j)), pl.BlockSpec((tr, LANE), lambda i, j: (i, nb + j))],
        out_specs=pl.BlockSpec((tr, 2 * LANE), lambda i, j: (i, j)),
        out_shape=jax.ShapeDtypeStruct(a.shape, a.dtype), compiler_params=_params(("parallel", "parallel")),
    )(a, a)


def ffn_deinterleave(a):
    rows, width = a.shape
    return a.reshape(rows, width // (2 * LANE), 2, LANE).swapaxes(1, 2).reshape(a.shape)


PAIR_COPY_ROWS = 512


def ffn_pairs_to_shards(a, *, name):
    rows, width = a.shape
    cols = width // 4
    per = cols // LANE
    tr = min(rows, PAIR_COPY_ROWS)

    def body(a_ref, o_ref):
        is_up = pl.program_id(1) >= 2
        for parity, chosen in ((0, jnp.logical_not(is_up)), (1, is_up)):
            @pl.when(chosen)
            def _(parity=parity):
                for t in range(per):
                    o_ref[0, :, t * LANE:(t + 1) * LANE] = a_ref[:, (2 * t + parity) * LANE:(2 * t + parity + 1) * LANE]

    return pl.pallas_call(
        body, name=name, grid=(rows // tr, 4),
        in_specs=[pl.BlockSpec((tr, 2 * cols), lambda i, s: (i, jnp.where(s >= 2, s - 2, s)))],
        out_specs=pl.BlockSpec((1, tr, cols), lambda i, s: (s, i, 0)),
        out_shape=jax.ShapeDtypeStruct((4, rows, cols), a.dtype), compiler_params=_params(("parallel", "parallel")),
    )(a)


def ffn_shards_to_pairs(parts, *, name):
    _, rows, cols = parts.shape
    per = cols // LANE
    tr = min(rows, PAIR_COPY_ROWS)

    def body(gate_ref, up_ref, o_ref):
        for t in range(per):
            o_ref[:, 2 * t * LANE:(2 * t + 1) * LANE] = gate_ref[0, :, t * LANE:(t + 1) * LANE]
            o_ref[:, (2 * t + 1) * LANE:(2 * t + 2) * LANE] = up_ref[0, :, t * LANE:(t + 1) * LANE]

    return pl.pallas_call(
        body, name=name, grid=(rows // tr, 2),
        in_specs=[pl.BlockSpec((1, tr, cols), lambda i, j: (j, i, 0)), pl.BlockSpec((1, tr, cols), lambda i, j: (2 + j, i, 0))],
        out_specs=pl.BlockSpec((tr, 2 * cols), lambda i, j: (i, j)),
        out_shape=jax.ShapeDtypeStruct((rows, 4 * cols), parts.dtype), compiler_params=_params(("parallel", "parallel")),
    )(parts, parts)


GATE, UP = slice(0, LANE), slice(LANE, 2 * LANE)


def _ffn_geometry(a, w):
    length, width = a.shape
    taps = w.shape[0]
    return length, width, width // (2 * LANE), taps, -(-(taps - 1) // 8) * 8, min(CONV_ROWS, length)


def _ffn_pre(xg_ref, xu_ref, w_ref, b_ref, base, taps, pad, rc):
    gate = _conv_taps(xg_ref, w_ref, base, taps, pad, jnp.broadcast_to(b_ref[:, GATE], (rc, LANE)), GATE)
    up = _conv_taps(xu_ref, w_ref, base, taps, pad, jnp.broadcast_to(b_ref[:, UP], (rc, LANE)), UP)
    return gate, up


def ffn_conv_act(a, w, b, *, name):
    length, width, nb, taps, pad, rc = _ffn_geometry(a, w)

    def body(a_ref, w_ref, b_ref, o_ref, xg_ref, xu_ref):
        for xp_ref, lanes in ((xg_ref, GATE), (xu_ref, UP)):
            xp_ref[0:pad, :] = jnp.zeros((pad, LANE), F32)
            xp_ref[pad:pad + length, :] = a_ref[:, lanes].astype(F32)

        def chunk(r, carry):
            base = pl.multiple_of(r * rc, rc)
            gate, up = _ffn_pre(xg_ref, xu_ref, w_ref, b_ref, base, taps, pad, rc)
            o_ref[pl.ds(base, rc), :] = (_silu(gate) * up).astype(o_ref.dtype)
            return carry

        lax.fori_loop(0, length // rc, chunk, 0)

    pair = lambda rows: pl.BlockSpec((rows, 2 * LANE), lambda j: (0, j))
    return pl.pallas_call(
        body, name=name, grid=(nb,), in_specs=[pair(length), pair(taps), pair(1)],
        out_specs=pl.BlockSpec((length, LANE), lambda j: (0, j)),
        out_shape=jax.ShapeDtypeStruct((length, width // 2), BF16),
        scratch_shapes=[pltpu.VMEM((pad + length, LANE), F32), pltpu.VMEM((pad + length, LANE), F32)],
        compiler_params=_params(("parallel",)),
    )(a, w, b)


def ffn_conv_act_bwd(a, w, b, dact, *, name):
    length, width, nb, taps, pad, rc = _ffn_geometry(a, w)

    def body(a_ref, w_ref, b_ref, dy_ref, da_ref, dw_ref, db_ref, xg_ref, xu_ref, gg_ref, gu_ref):
        for xp_ref, lanes in ((xg_ref, GATE), (xu_ref, UP)):
            xp_ref[0:pad, :] = jnp.zeros((pad, LANE), F32)
            xp_ref[pad:pad + length, :] = a_ref[:, lanes].astype(F32)
        for gp_ref in (gg_ref, gu_ref):
            gp_ref[length:length + pad, :] = jnp.zeros((pad, LANE), F32)

        def pre_chunk(r, carry):
            base = pl.multiple_of(r * rc, rc)
            gate, up = _ffn_pre(xg_ref, xu_ref, w_ref, b_ref, base, taps, pad, rc)
            sig = jax.nn.sigmoid(gate)
            dy = dy_ref[pl.ds(base, rc), :]
            gg_ref[pl.ds(base, rc), :] = dy * up * (sig * (1.0 + gate * (1.0 - sig)))
            gu_ref[pl.ds(base, rc), :] = dy * (gate * sig)
            return carry

        lax.fori_loop(0, length // rc, pre_chunk, 0)
        _conv_transpose(xg_ref, gg_ref, w_ref, da_ref, dw_ref, db_ref, GATE, length, taps, pad, rc)
        _conv_transpose(xu_ref, gu_ref, w_ref, da_ref, dw_ref, db_ref, UP, length, taps, pad, rc)

    pair = lambda rows: pl.BlockSpec((rows, 2 * LANE), lambda j: (0, j))
    return pl.pallas_call(
        body, name=name, grid=(nb,),
        in_specs=[pair(length), pair(taps), pair(1), pl.BlockSpec((length, LANE), lambda j: (0, j))],
        out_specs=[pair(length), pair(taps), pair(1)],
        out_shape=[jax.ShapeDtypeStruct((length, width), BF16), jax.ShapeDtypeStruct((taps, width), F32),
                   jax.ShapeDtypeStruct((1, width), F32)],
        scratch_shapes=[pltpu.VMEM((pad + length, LANE), F32), pltpu.VMEM((pad + length, LANE), F32),
                        pltpu.VMEM((length + pad, LANE), F32), pltpu.VMEM((length + pad, LANE), F32)],
        compiler_params=_params(("parallel",)),
    )(a, w, b, dact)


def _retention_consts():
    h = np.arange(RET_HEADS, dtype=np.float32)
    log_g = np.log1p(-(2.0 ** (-5.0 - h))).astype(np.float32)
    idx = np.arange(CHUNK, dtype=np.float32)
    diff = idx[:, None] - idx[None, :]
    intra = np.where(diff[None] >= 0, np.exp(np.maximum(diff, 0.0)[None] * log_g[:, None, None]), 0.0)
    zeta = np.exp((CHUNK - 1 - idx)[None, :] * log_g[:, None])
    xi = np.exp((idx + 1)[None, :] * log_g[:, None])
    decay = np.exp(CHUNK * log_g)
    zeta = np.broadcast_to(zeta[:, :, None], (RET_HEADS, CHUNK, RET_DK))
    xi = np.broadcast_to(xi[:, :, None], (RET_HEADS, CHUNK, RET_DV))
    return (jnp.asarray(intra, F32), jnp.asarray(zeta, F32), jnp.asarray(xi, F32), [float(d) for d in decay])


def _rotary_tables(length):
    inv = ROPE_BASE ** (-jnp.arange(0, RET_DK, 2, dtype=F32) / RET_DK)
    ang = jnp.arange(length).astype(F32)[:, None] * inv[None, :]
    cos, sin = jnp.cos(ang), jnp.sin(ang)
    return jnp.concatenate([cos, cos], axis=1), jnp.concatenate([-sin, sin], axis=1)


def _rot(x, cos2, sin2):
    return x * cos2 + pltpu.roll(x, RET_DK // 2, 1) * sin2


def _rot_t(y, cos2, sin2):
    return y * cos2 + pltpu.roll(y * sin2, RET_DK // 2, 1)


def _ret_chunk(q, k, v, g, state, intra, zeta, xi, decay):
    s = _mm_nt(q, k) * intra
    kv = _mm_tn(k * zeta, v)
    o = _mm(s, v) + _mm(q, state) * xi
    oc = o - jnp.mean(o, axis=-1, keepdims=True)
    r = oc * lax.rsqrt(jnp.mean(oc * oc, axis=-1, keepdims=True) + EPS)
    return _silu(g) * r, state * decay + kv


RET_QK, RET_V = RET_HEADS * RET_DK, RET_HEADS * RET_DV


def _ret_specs(rev, nc):
    def cidx(c):
        return nc - 1 - c if rev else c
    whole = lambda shape: pl.BlockSpec(shape, lambda c: (0,) * len(shape))
    return [
        pl.BlockSpec((CHUNK, RET_QK), lambda c: (cidx(c), 0)),
        pl.BlockSpec((CHUNK, RET_QK), lambda c: (cidx(c), 1)),
        pl.BlockSpec((CHUNK, RET_V), lambda c: (cidx(c), 1)),
        pl.BlockSpec((CHUNK, RET_V), lambda c: (cidx(c), 2)),
        pl.BlockSpec((CHUNK, RET_DK), lambda c: (cidx(c), 0)),
        pl.BlockSpec((CHUNK, RET_DK), lambda c: (cidx(c), 0)),
        whole((RET_HEADS, CHUNK, CHUNK)), whole((RET_HEADS, CHUNK, RET_DK)), whole((RET_HEADS, CHUNK, RET_DV)),
    ], cidx


def _head(ref, h, width):
    return ref[:, h * width:(h + 1) * width].astype(F32)


def retention_fwd(proj, cos2, sin2):
    length = proj.shape[0]
    nc = length // CHUNK
    intra, zeta, xi, decays = _retention_consts()
    specs, _ = _ret_specs(False, nc)
    scale = RET_DK ** -0.5

    def body(q_ref, k_ref, v_ref, g_ref, cos_ref, sin_ref, intra_ref, zeta_ref, xi_ref, y_ref, st_ref, state):
        @pl.when(pl.program_id(0) == 0)
        def _():
            state[...] = jnp.zeros_like(state)

        cos2v, sin2v = cos_ref[...], sin_ref[...]
        for h in range(RET_HEADS):
            q = _rot(_head(q_ref, h, RET_DK), cos2v, sin2v)
            k = _rot(_head(k_ref, h, RET_DK), cos2v, sin2v) * scale
            st_ref[h, 0] = state[h]
            y, new_state = _ret_chunk(q, k, _head(v_ref, h, RET_DV), _head(g_ref, h, RET_DV), state[h],
                                      intra_ref[h], zeta_ref[h], xi_ref[h], decays[h])
            y_ref[:, h * RET_DV:(h + 1) * RET_DV] = y.astype(y_ref.dtype)
            state[h] = new_state

    return pl.pallas_call(
        body, name="retention_fwd", grid=(nc,), in_specs=specs,
        out_specs=[pl.BlockSpec((CHUNK, RET_V), lambda c: (c, 0)),
                   pl.BlockSpec((RET_HEADS, 1, RET_DK, RET_DV), lambda c: (0, c, 0, 0))],
        out_shape=[jax.ShapeDtypeStruct((length, RET_V), BF16),
                   jax.ShapeDtypeStruct((RET_HEADS, nc, RET_DK, RET_DV), F32)],
        scratch_shapes=[pltpu.VMEM((RET_HEADS, RET_DK, RET_DV), F32)],
        compiler_params=_params(("arbitrary",)),
    )(proj, proj, proj, proj, cos2, sin2, intra, zeta, xi)


def retention_bwd(proj, cos2, sin2, states, dmix):
    length = proj.shape[0]
    nc = length // CHUNK
    intra, zeta, xi, decays = _retention_consts()
    specs, cidx = _ret_specs(True, nc)
    scale = RET_DK ** -0.5

    def body(q_ref, k_ref, v_ref, g_ref, cos_ref, sin_ref, intra_ref, zeta_ref, xi_ref, st_ref, dy_ref,
             dq_ref, dk_ref, dv_ref, dg_ref, dstate):
        @pl.when(pl.program_id(0) == 0)
        def _():
            dstate[...] = jnp.zeros_like(dstate)

        cos2v, sin2v = cos_ref[...], sin_ref[...]
        for h in range(RET_HEADS):
            q = _rot(_head(q_ref, h, RET_DK), cos2v, sin2v)
            k = _rot(_head(k_ref, h, RET_DK), cos2v, sin2v) * scale
            intra_v, zeta_v, xi_v, decay = intra_ref[h], zeta_ref[h], xi_ref[h], decays[h]
            _, vjp = jax.vjp(lambda q, k, v, g, s: _ret_chunk(q, k, v, g, s, intra_v, zeta_v, xi_v, decay),
                             q, k, _head(v_ref, h, RET_DV), _head(g_ref, h, RET_DV), st_ref[h, 0])
            dq, dk, dv, dg, ds = vjp((_head(dy_ref, h, RET_DV).astype(F32), dstate[h]))
            dq_ref[:, h * RET_DK:(h + 1) * RET_DK] = _rot_t(dq, cos2v, sin2v).astype(dq_ref.dtype)
            dk_ref[:, h * RET_DK:(h + 1) * RET_DK] = _rot_t(dk * scale, cos2v, sin2v).astype(dk_ref.dtype)
            dv_ref[:, h * RET_DV:(h + 1) * RET_DV] = dv.astype(dv_ref.dtype)
            dg_ref[:, h * RET_DV:(h + 1) * RET_DV] = dg.astype(dg_ref.dtype)
            dstate[h] = ds

    specs = specs + [pl.BlockSpec((RET_HEADS, 1, RET_DK, RET_DV), lambda c: (0, cidx(c), 0, 0)),
                     pl.BlockSpec((CHUNK, RET_V), lambda c: (cidx(c), 0))]
    row = lambda width: pl.BlockSpec((CHUNK, width), lambda c: (cidx(c), 0))
    return pl.pallas_call(
        body, name="retention_bwd", grid=(nc,), in_specs=specs,
        out_specs=[row(RET_QK), row(RET_QK), row(RET_V), row(RET_V)],
        out_shape=[jax.ShapeDtypeStruct((length, RET_QK), BF16), jax.ShapeDtypeStruct((length, RET_QK), BF16),
                   jax.ShapeDtypeStruct((length, RET_V), BF16), jax.ShapeDtypeStruct((length, RET_V), BF16)],
        scratch_shapes=[pltpu.VMEM((RET_HEADS, RET_DK, RET_DV), F32)],
        compiler_params=_params(("arbitrary",)),
    )(proj, proj, proj, proj, cos2, sin2, intra, zeta, xi, states, dmix)


def _ssd_consts():
    tri = np.tril(np.ones((CHUNK, CHUNK), np.float32))
    expand = np.zeros((LANE, SSM_DINNER), np.float32)
    for h in range(SSM_HEADS):
        expand[h, h * SSM_P:(h + 1) * SSM_P] = 1.0
    return jnp.asarray(tri), jnp.asarray(tri.T.copy()), jnp.asarray(expand)


def _ssd_chunk(xs, bm, cm, dtr, z, state, dt_bias, a_log, d_skip, norm_w, tri, tri_t, expand):
    gw = SSM_DINNER // SSM_GROUPS
    dt = jax.nn.softplus(dtr + dt_bias)
    a_neg = -jnp.exp(a_log)
    da = dt * a_neg
    acs = _dot_hi(tri, da)
    acs_t = _dot_hi_tn(da, tri_t)
    dt_x = _times_01(dt, expand)
    a_x = jnp.mean(_dot_hi(jnp.broadcast_to(a_neg, (8, LANE)), expand), axis=0, keepdims=True)
    da_x = dt_x * a_x
    acs_x = _01_times(tri, da_x)
    tot_x = jnp.sum(da_x, axis=0, keepdims=True)
    x_dt = xs * dt_x
    x_dec = x_dt * jnp.exp(tot_x - acs_x)
    e_acs = jnp.exp(acs_x)
    e_tot = jnp.exp(tot_x)
    lane = lax.broadcasted_iota(jnp.int32, (CHUNK, LANE), 1)
    sub = lax.broadcasted_iota(jnp.int32, (CHUNK, LANE), 0)
    causal = sub >= lane
    ys, new_states = [], []
    for g in range(SSM_GROUPS):
        bg = bm[:, g * SSM_N:(g + 1) * SSM_N]
        cg = cm[:, g * SSM_N:(g + 1) * SSM_N]
        sg = state[:, g * gw:(g + 1) * gw]
        cb = _mm_nt(cg, bg)
        y_off = _mm(cg, sg) * e_acs[:, g * gw:(g + 1) * gw]
        new_states.append(sg * e_tot[:, g * gw:(g + 1) * gw] + _mm_tn(bg, x_dec[:, g * gw:(g + 1) * gw]))
        pairs = []
        for p in range(gw // LANE):
            hp = g * (gw // LANE) + p
            xp = x_dt[:, hp * LANE:(hp + 1) * LANE]
            halves = []
            for head in (2 * hp, 2 * hp + 1):
                col = jnp.sum(jnp.where(lane == head, acs, 0.0), axis=1, keepdims=True)
                row = jnp.sum(jnp.where(sub == head, acs_t, 0.0), axis=0, keepdims=True)
                decay = jnp.exp(jnp.where(causal, col - row, -1e30))
                halves.append(_mm(cb * decay, xp))
            pairs.append(jnp.where(lane < SSM_P, halves[0], halves[1]))
        ys.append(jnp.concatenate(pairs, axis=1) + y_off)
    d_x = jnp.mean(_dot_hi(jnp.broadcast_to(d_skip, (8, LANE)), expand), axis=0, keepdims=True)
    y = (jnp.concatenate(ys, axis=1) + d_x * xs) * _silu(z)
    normed = []
    for g in range(SSM_GROUPS):
        yg = y[:, g * gw:(g + 1) * gw]
        normed.append(yg * lax.rsqrt(jnp.mean(yg * yg, axis=-1, keepdims=True) + EPS))
    return jnp.concatenate(normed, axis=1) * norm_w, jnp.concatenate(new_states, axis=1)


XBC = SSM_DINNER + 2 * SSM_GROUPS * SSM_N


def _ssd_specs(rev, nc):
    def cidx(c):
        return nc - 1 - c if rev else c
    row = lambda w, j: pl.BlockSpec((CHUNK, w), lambda c: (cidx(c), j))
    whole = lambda shape: pl.BlockSpec(shape, lambda c: (0,) * len(shape))
    return [row(XBC, 0), row(LANE, 0), row(SSM_DINNER, 3),
            whole((1, LANE)), whole((1, LANE)), whole((1, LANE)), whole((1, SSM_DINNER)),
            whole((CHUNK, CHUNK)), whole((CHUNK, CHUNK)), whole((LANE, SSM_DINNER))], cidx


def ssd_fwd(xbc, dt_raw, proj, dt_bias, a_log, d_skip, norm_w):
    length = proj.shape[0]
    nc = length // CHUNK
    tri, tri_t, expand = _ssd_consts()
    specs, _ = _ssd_specs(False, nc)

    def body(xbc_ref, dt_ref, z_ref, dtb_ref, alog_ref, d_ref, nw_ref, tri_ref, trit_ref, e_ref, y_ref, st_ref, state):
        @pl.when(pl.program_id(0) == 0)
        def _():
            state[...] = jnp.zeros_like(state)

        st_ref[0] = state[...]
        y, new_state = _ssd_chunk(
            xbc_ref[:, 0:SSM_DINNER], xbc_ref[:, SSM_DINNER:SSM_DINNER + 256], xbc_ref[:, SSM_DINNER + 256:XBC],
            dt_ref[...], z_ref[...].astype(F32), state[...], dtb_ref[...], alog_ref[...], d_ref[...], nw_ref[...],
            tri_ref[...], trit_ref[...], e_ref[...])
        y_ref[...] = y.astype(y_ref.dtype)
        state[...] = new_state

    return pl.pallas_call(
        body, name="ssd_fwd", grid=(nc,), in_specs=specs,
        out_specs=[pl.BlockSpec((CHUNK, SSM_DINNER), lambda c: (c, 0)),
                   pl.BlockSpec((1, SSM_N, SSM_DINNER), lambda c: (c, 0, 0))],
        out_shape=[jax.ShapeDtypeStruct((length, SSM_DINNER), BF16),
                   jax.ShapeDtypeStruct((nc, SSM_N, SSM_DINNER), F32)],
        scratch_shapes=[pltpu.VMEM((SSM_N, SSM_DINNER), F32)],
        compiler_params=_params(("arbitrary",)),
    )(xbc, dt_raw, proj, dt_bias, a_log, d_skip, norm_w, tri, tri_t, expand)


def ssd_bwd(xbc, dt_raw, proj, dt_bias, a_log, d_skip, norm_w, states, dmix):
    length = proj.shape[0]
    nc = length // CHUNK
    tri, tri_t, expand = _ssd_consts()
    specs, cidx = _ssd_specs(True, nc)

    def body(xbc_ref, dt_ref, z_ref, dtb_ref, alog_ref, d_ref, nw_ref, tri_ref, trit_ref, e_ref, st_ref, dy_ref,
             dxbc_ref, ddt_ref, dz_ref, ddtb_ref, dalog_ref, dd_ref, dnw_ref, dstate):
        c = pl.program_id(0)

        @pl.when(c == 0)
        def _():
            dstate[...] = jnp.zeros_like(dstate)

        tri_v, trit_v, e_v = tri_ref[...], trit_ref[...], e_ref[...]
        _, vjp = jax.vjp(
            lambda *a: _ssd_chunk(*a, tri_v, trit_v, e_v),
            xbc_ref[:, 0:SSM_DINNER], xbc_ref[:, SSM_DINNER:SSM_DINNER + 256], xbc_ref[:, SSM_DINNER + 256:XBC],
            dt_ref[...], z_ref[...].astype(F32), st_ref[0], dtb_ref[...], alog_ref[...], d_ref[...], nw_ref[...])
        dxs, dbm, dcm, ddt, dz, ds, ddtb, dalog, dd, dnw = vjp((dy_ref[...].astype(F32), dstate[...]))
        dxbc_ref[:, 0:SSM_DINNER] = dxs
        dxbc_ref[:, SSM_DINNER:SSM_DINNER + 256] = dbm
        dxbc_ref[:, SSM_DINNER + 256:XBC] = dcm
        ddt_ref[...] = ddt.astype(ddt_ref.dtype)
        dz_ref[...] = dz.astype(dz_ref.dtype)
        dstate[...] = ds
        for r, d in ((ddtb_ref, ddtb), (dalog_ref, dalog), (dd_ref, dd), (dnw_ref, dnw)):
            @pl.when(c == 0)
            def _(r=r, d=d):
                r[...] = d

            @pl.when(c > 0)
            def _(r=r, d=d):
                r[...] += d

    whole = lambda shape: pl.BlockSpec(shape, lambda c: (0,) * len(shape))
    specs = specs + [pl.BlockSpec((1, SSM_N, SSM_DINNER), lambda c: (cidx(c), 0, 0)),
                     pl.BlockSpec((CHUNK, SSM_DINNER), lambda c: (cidx(c), 1))]
    return pl.pallas_call(
        body, name="ssd_bwd", grid=(nc,), in_specs=specs,
        out_specs=[pl.BlockSpec((CHUNK, XBC), lambda c: (cidx(c), 0)), pl.BlockSpec((CHUNK, LANE), lambda c: (cidx(c), 0)),
                   pl.BlockSpec((CHUNK, SSM_DINNER), lambda c: (cidx(c), 0)),
                   whole((1, LANE)), whole((1, LANE)), whole((1, LANE)), whole((1, SSM_DINNER))],
        out_shape=[jax.ShapeDtypeStruct((length, XBC), F32), jax.ShapeDtypeStruct((length, LANE), BF16),
                   jax.ShapeDtypeStruct((length, SSM_DINNER), BF16),
                   jax.ShapeDtypeStruct((1, LANE), F32), jax.ShapeDtypeStruct((1, LANE), F32),
                   jax.ShapeDtypeStruct((1, LANE), F32), jax.ShapeDtypeStruct((1, SSM_DINNER), F32)],
        scratch_shapes=[pltpu.VMEM((SSM_N, SSM_DINNER), F32)],
        compiler_params=_params(("arbitrary",)),
    )(xbc, dt_raw, proj, dt_bias, a_log, d_skip, norm_w, tri, tri_t, expand, states, dmix)


def _cmul(ar, ai, br, bi):
    return ar * br - ai * bi, ar * bi + ai * br


def s5_scan(b_re, b_im, a_re, a_im, *, reverse=False, states=None, name, lw=256):
    length, lanes = b_re.shape
    nk = length // SCAN_SEG
    with_da = states is not None
    assert reverse or not with_da

    def shift(v):
        sub = lax.broadcasted_iota(jnp.int32, v.shape, 0)
        if reverse:
            return jnp.where(sub == SCAN_SEG - 1, 0.0, pltpu.roll(v, SCAN_SEG - 1, 0))
        return jnp.where(sub == 0, 0.0, pltpu.roll(v, 1, 0))

    def body(*refs):
        if with_da:
            bre_ref, bim_ref, are_ref, aim_ref, sre_ref, sim_ref, xre_ref, xim_ref, dare_ref, daim_ref = refs
        else:
            bre_ref, bim_ref, are_ref, aim_ref, xre_ref, xim_ref = refs
        ar = jnp.broadcast_to(are_ref[...], (SCAN_SEG, lw))
        ai = jnp.broadcast_to(aim_ref[...], (SCAN_SEG, lw))

        def tile(i):
            k = (nk - 1 - i) if reverse else i
            return pl.ds(pl.multiple_of(k * SCAN_SEG, SCAN_SEG), SCAN_SEG)

        def local(i, carry):
            xr, xi, pr, pi = carry
            rows = tile(i)
            mr, mi = _cmul(ar, ai, xr, xi)
            xr, xi = mr + bre_ref[rows, :], mi + bim_ref[rows, :]
            xre_ref[rows, :] = xr
            xim_ref[rows, :] = xi
            pr, pi = _cmul(ar, ai, pr, pi)
            return xr, xi, pr, pi

        zero = jnp.zeros((SCAN_SEG, lw), F32)
        one = jnp.ones((SCAN_SEG, lw), F32)
        er, ei, pr, pi = lax.fori_loop(0, nk, local, (zero, zero, one, zero))
        cr, ci = zero, zero
        for _ in range(SCAN_SEG - 1):
            mr, mi = _cmul(pr, pi, cr, ci)
            cr, ci = shift(er + mr), shift(ei + mi)

        def fix(i, carry):
            pr, pi, dr, di = carry
            rows = tile(i)
            pr, pi = _cmul(ar, ai, pr, pi)
            mr, mi = _cmul(pr, pi, cr, ci)
            xr, xi = xre_ref[rows, :] + mr, xim_ref[rows, :] + mi
            xre_ref[rows, :] = xr
            xim_ref[rows, :] = xi
            if with_da:
                k = nk - 1 - i
                prev = pl.ds(pl.multiple_of(jnp.maximum(k - 1, 0) * SCAN_SEG, SCAN_SEG), SCAN_SEG)
                last = pl.ds((nk - 1) * SCAN_SEG, SCAN_SEG)
                sub = lax.broadcasted_iota(jnp.int32, (SCAN_SEG, lw), 0)
                wr = jnp.where(sub == 0, 0.0, pltpu.roll(sre_ref[last, :], 1, 0))
                wi = jnp.where(sub == 0, 0.0, pltpu.roll(sim_ref[last, :], 1, 0))
                sr = jnp.where(k == 0, wr, sre_ref[prev, :])
                si = jnp.where(k == 0, wi, sim_ref[prev, :])
                dr, di = dr + xr * sr + xi * si, di + xi * sr - xr * si
            return pr, pi, dr, di

        _, _, dr, di = lax.fori_loop(0, nk, fix, (one, zero, zero, zero))
        if with_da:
            dare_ref[...] = jnp.sum(dr, axis=0, keepdims=True)
            daim_ref[...] = jnp.sum(di, axis=0, keepdims=True)

    col = pl.BlockSpec((length, lw), lambda j: (0, j))
    vec = pl.BlockSpec((1, lw), lambda j: (0, j))
    ins = [b_re, b_im, a_re, a_im] + (list(states) if with_da else [])
    in_specs = [col, col, vec, vec] + ([col, col] if with_da else [])
    out_specs = [col, col] + ([vec, vec] if with_da else [])
    out_shape = [jax.ShapeDtypeStruct((length, lanes), F32)] * 2 + ([jax.ShapeDtypeStruct((1, lanes), F32)] * 2 if with_da else [])
    return pl.pallas_call(
        body, name=name, grid=(lanes // lw,), in_specs=in_specs, out_specs=out_specs, out_shape=out_shape,
        compiler_params=_params(("parallel",)),
    )(*ins)


def _seg_interleave(v):
    length = v.shape[0]
    return v.reshape(SCAN_SEG, length // SCAN_SEG, -1).transpose(1, 0, 2).reshape(length, -1)


def _seg_deinterleave(v):
    length = v.shape[0]
    return v.reshape(length // SCAN_SEG, SCAN_SEG, -1).transpose(1, 0, 2).reshape(length, -1)


def _block_diag(m):
    eye = jnp.eye(S5_GROUPS, dtype=m.dtype)
    return (m.reshape(S5_GROUPS, S5_GROUP, 1, S5_STATE) * eye[:, None, :, None]).reshape(S5_GROUPS * S5_GROUP, S5_LANES)


def _block_diag_take(full, *, name):
    pairs = S5_GROUPS // 2
    rows, lanes = 2 * S5_GROUP, 2 * S5_STATE

    def body(f_ref, o_ref):
        o_ref[...] = f_ref[...]

    tiles = pl.pallas_call(
        body, name=name, grid=(pairs,), in_specs=[pl.BlockSpec((rows, lanes), lambda p: (p, p))],
        out_specs=pl.BlockSpec((rows, lanes), lambda p: (p, 0)),
        out_shape=jax.ShapeDtypeStruct((pairs * rows, lanes), full.dtype), compiler_params=_params(("parallel",)),
    )(full)
    tiles = tiles.reshape(pairs, 2, S5_GROUP, 2, S5_STATE)
    blocks = jnp.stack([tiles[:, 0, :, 0, :], tiles[:, 1, :, 1, :]], axis=1)
    return blocks.reshape(S5_GROUPS * S5_GROUP, S5_STATE)


def _s5_prep(a_re, a_im, log_step, b_re, b_im, rep):
    step = jnp.exp(log_step)
    mag = jnp.exp(a_re * step)
    ab_re = mag * jnp.cos(a_im * step)
    ab_im = mag * jnp.sin(a_im * step)
    den = a_re * a_re + a_im * a_im
    f_re = ((ab_re - 1.0) * a_re + ab_im * a_im) / den
    f_im = (ab_im * a_re - (ab_re - 1.0) * a_im) / den
    fr, fi = _dot_hi(rep, f_re), _dot_hi(rep, f_im)
    return ab_re, ab_im, fr * b_re - fi * b_im, fr * b_im + fi * b_re


def _rms(x, g):
    return (x * lax.rsqrt(jnp.mean(x * x, axis=-1, keepdims=True) + EPS) * g,)


def _ffn_act(gate, up):
    return (_silu(gate) * up,)


def _glu(a, g):
    return (a * jax.nn.sigmoid(g),)


def _ln_silu(x, g, b):
    xc = x - jnp.mean(x, axis=-1, keepdims=True)
    var = jnp.mean(xc * xc, axis=-1, keepdims=True)
    return (_silu(xc * lax.rsqrt(var + EPS) * g + b),)


def _s5_post(y, u, d_skip, glu_w):
    s = jax.nn.gelu(y + d_skip * u)
    return (s * jax.nn.sigmoid(_mm(s, glu_w)),)


def loss_head(x, tgt, g, *, tl=512):
    length, d = x.shape
    tl = min(tl, length)

    def body(x_ref, t_ref, g_ref, loss_ref, dx_ref, dg_ref):
        i = pl.program_id(0)
        y, vjp = jax.vjp(lambda x, g: _rms(x, g)[0], x_ref[...], g_ref[...])
        err = y - t_ref[...]
        dx, dg = vjp(err * (1.0 / d))
        dx_ref[...] = dx
        part = jnp.broadcast_to(0.5 * jnp.sum(jnp.mean(err * err, axis=-1, keepdims=True), axis=0, keepdims=True), (1, LANE))

        @pl.when(i == 0)
        def _():
            loss_ref[...] = part
            dg_ref[...] = dg

        @pl.when(i > 0)
        def _():
            loss_ref[...] += part
            dg_ref[...] += dg

    row = pl.BlockSpec((tl, d), lambda i: (i, 0))
    return pl.pallas_call(
        body, name="loss_head", grid=(length // tl,),
        in_specs=[row, row, pl.BlockSpec((1, d), lambda i: (0, 0))],
        out_specs=[pl.BlockSpec((1, LANE), lambda i: (0, 0)), row, pl.BlockSpec((1, d), lambda i: (0, 0))],
        out_shape=[jax.ShapeDtypeStruct((1, LANE), F32), jax.ShapeDtypeStruct((length, d), F32),
                   jax.ShapeDtypeStruct((1, d), F32)],
        compiler_params=_params(("arbitrary",)),
    )(x, tgt, g)


def _pad_heads(v):
    return jnp.pad(v, ((0, 0), (0, LANE - v.shape[1])))


def local_step(x, tgt, w, late_weights=None, early_reduce=None):
    length = x.shape[0]
    cos2, sin2 = _rotary_tables(length)
    grads = {}
    w = dict(w)

    def rms_fwd(xin, g, name):
        return rowwise_fwd(_rms, [xin], [], [g], [], [(D_MODEL, BF16)], name=name, tl=512)[0]

    def rms_bwd(xin, g, dh, dxo, name):
        return rowwise_bwd(_rms, [xin], [], [g], [], [dh], [F32], name=name, tl=512, add=dxo)

    def ffn_fwd(i, xin):
        hf = rms_fwd(xin, w["ffn_norm"][i:i + 1], f"ffn{i}_norm")
        w_up = w["ffn_w_up_pairs"][i] if "ffn_w_up_pairs" in w else ffn_interleave(w["ffn_w_up"][i], name=f"ffn{i}_up_pairs")
        a = matmul(hf, w_up, out_dtype=BF16, name=f"ffn{i}_up")
        act = ffn_conv_act(a, ffn_interleave(w["ffn_dw_w"][i]), ffn_interleave(w["ffn_dw_b"][i:i + 1]),
                           name=f"ffn{i}_conv_act")
        return matmul(act, w["ffn_w_down"][i], res=xin, name=f"ffn{i}_down"), (hf, a, act, w_up)

    def ffn_bwd(i, xin, saved, dxo):
        hf, a, act, w_up = saved
        dact = matmul(dxo, w["ffn_w_down"][i], tb=True, out_dtype=BF16, name=f"ffn{i}_down_dx")
        dw_down = matmul(act, dxo, ta=True, name=f"ffn{i}_down_dw")
        da, ddw_w, ddw_b = ffn_conv_act_bwd(a, ffn_interleave(w["ffn_dw_w"][i]), ffn_interleave(w["ffn_dw_b"][i:i + 1]),
                                            dact, name=f"ffn{i}_conv_act_bwd")
        dw_up = ffn_pairs_to_shards(matmul(hf, da, ta=True, name=f"ffn{i}_up_dw"), name=f"ffn{i}_up_dw_shards")
        dhf = matmul(da, w_up, tb=True, name=f"ffn{i}_up_dx")
        dxin, dnorm = rms_bwd(xin, w["ffn_norm"][i:i + 1], dhf, dxo, f"ffn{i}_norm_bwd")
        return dxin, dict(ffn_norm=dnorm, ffn_w_up=dw_up, ffn_dw_w=ffn_deinterleave(ddw_w),
                          ffn_dw_b=ffn_deinterleave(ddw_b), ffn_w_down=dw_down)

    w_in_e = jnp.pad(w["e_w_in"][0], ((0, 0), (0, EVEN_IN_PAD - EVEN_IN)))
    conv_w_e, conv_b_e = w["e_conv_w"][0], w["e_conv_b"]
    dt_bias, a_log, d_skip = _pad_heads(w["e_dt_bias"]), _pad_heads(w["e_a_log"]), _pad_heads(w["e_d"])
    xbc_off = 4 * D_MODEL

    hn0 = rms_fwd(x, w["mix_norm"][0:1], "mix0_norm")
    proj0 = matmul(hn0, w_in_e, out_dtype=BF16, name="even_in")
    dt_raw = matmul(hn0, w_in_e[:, EVEN_IN_PAD - LANE:], name="even_in_dt")
    y_ret, ret_states = retention_fwd(proj0, cos2, sin2)
    xbc = conv_fwd(proj0, conv_w_e, conv_b_e, act=True, off=xbc_off, name="ssd_conv")
    y_ssm, ssd_states = ssd_fwd(xbc, dt_raw, proj0, dt_bias, a_log, d_skip, w["e_ssm_norm"])
    mix0 = jnp.concatenate([y_ret, y_ssm], axis=1)
    if late_weights is not None:
        w.update(late_weights(y_ssm))
    w_out_e = w["e_w_out"][0]
    x1 = matmul(mix0, w_out_e, res=x, name="even_out")
    x2, ffn0_saved = ffn_fwd(0, x1)

    w_in_o, w_out_o, glu_w = w["o_w_in"][0], w["o_w_out"][0], w["o_glu_w"][0]
    dw_w_o, dw_b_o, ln_g, ln_b, d_o = w["o_dw_w"][0], w["o_dw_b"], w["o_ln_g"], w["o_ln_b"], w["o_d"]
    rep = jnp.asarray(np.repeat(np.eye(S5_GROUPS, dtype=np.float32), S5_GROUP, axis=0))
    rows_gc = (S5_GROUPS * S5_GROUP, S5_STATE)
    prep_in = [w["o_a_re"][0], w["o_a_im"][0], w["o_log_step"].reshape(S5_GROUPS, 1),
               w["o_b_re"][0].transpose(0, 2, 1).reshape(rows_gc), w["o_b_im"][0].transpose(0, 2, 1).reshape(rows_gc), rep]
    ab_re, ab_im, bb_re, bb_im = whole_fwd(
        _s5_prep, prep_in, [(S5_GROUPS, S5_STATE)] * 2 + [rows_gc] * 2, name="s5_prep")
    a_re_row, a_im_row = ab_re.reshape(1, S5_LANES), ab_im.reshape(1, S5_LANES)
    b_re_bd, b_im_bd = _block_diag(bb_re).astype(BF16), _block_diag(bb_im).astype(BF16)
    c_re_bd = _block_diag(w["o_c_re"][0].reshape(rows_gc)).astype(BF16)
    c_im_neg_bd = _block_diag(-w["o_c_im"][0].reshape(rows_gc)).astype(BF16)

    hn1 = rms_fwd(x2, w["mix_norm"][1:2], "mix1_norm")
    proj1 = matmul(hn1, w_in_o, name="odd_in")
    half = D_MODEL // 2
    c_glu = rowwise_fwd(_glu, [Cols(proj1, half, 0), Cols(proj1, half, 1)], [], [], [], [(half, F32)],
                        name="conf_glu", tl=512)[0]
    c_conv = conv_fwd(c_glu, dw_w_o, dw_b_o, act=False, name="conf_conv")
    c_out = rowwise_fwd(_ln_silu, [c_conv], [], [ln_g, ln_b], [], [(half, BF16)], name="conf_ln", tl=512)[0]
    u_seg = _seg_interleave(proj1[:, 2 * half:])
    bu_re = matmul(u_seg, b_re_bd, name="s5_bu_re")
    bu_im = matmul(u_seg, b_im_bd, name="s5_bu_im")
    xs_re, xs_im = s5_scan(bu_re, bu_im, a_re_row, a_im_row, name="s5_scan")
    y_im = matmul(xs_im, c_im_neg_bd, tb=True, name="s5_y_im")
    y_s5 = _seg_deinterleave(matmul(xs_re, c_re_bd, tb=True, res=y_im, name="s5_y_re"))
    s_out = rowwise_fwd(_s5_post, [y_s5, Cols(proj1, half, 2)], [], [d_o, glu_w], [], [(half, BF16)],
                        name="s5_post", tl=512)[0]
    mix1 = jnp.concatenate([c_out, s_out], axis=1)
    x3 = matmul(mix1, w_out_o, res=x2, name="odd_out")
    x4, ffn1_saved = ffn_fwd(1, x3)

    loss, dx4, dfinal = loss_head(x4, tgt, w["final_norm"].reshape(1, D_MODEL))
    grads["final_norm"] = dfinal.reshape(D_MODEL)

    dx3, g_ffn1 = ffn_bwd(1, x3, ffn1_saved, dx4)
    dmix1 = matmul(dx3, w_out_o, tb=True, name="odd_out_dx")
    grads["o_w_out"] = [matmul(mix1, dx3, ta=True, name="odd_out_dw")]
    dc_conv, dln_g, dln_b = rowwise_bwd(_ln_silu, [c_conv], [], [ln_g, ln_b], [], [Cols(dmix1, half, 0)], [F32],
                                        name="conf_ln_bwd", tl=512)
    dc_glu, ddw_w_o, ddw_b_o = conv_bwd(c_glu, dw_w_o, dw_b_o, dc_conv, act=False, name="conf_conv_bwd")
    d_cacg = rowwise_bwd(_glu, [Cols(proj1, half, 0), Cols(proj1, half, 1)], [], [], [], [dc_glu], [BF16],
                         name="conf_glu_bwd", tl=512, merge=True)[0]
    dy_s5, du_post, dd_o, dglu_w = rowwise_bwd(
        _s5_post, [y_s5, Cols(proj1, half, 2)], [], [d_o, glu_w], [], [Cols(dmix1, half, 1)], [F32, F32],
        name="s5_post_bwd", tl=512)
    dy_seg = _seg_interleave(dy_s5)
    dxs_re = matmul(dy_seg, c_re_bd, name="s5_dx_re")
    dxs_im = matmul(dy_seg, c_im_neg_bd, name="s5_dx_im")
    dc_re_bd = matmul(dy_seg, xs_re, ta=True, name="s5_dc_re")
    dc_im_neg_bd = matmul(dy_seg, xs_im, ta=True, name="s5_dc_im")
    g_re, g_im, dab_re, dab_im = s5_scan(dxs_re, dxs_im, a_re_row, -a_im_row, reverse=True, states=(xs_re, xs_im),
                                         name="s5_scan_bwd", lw=LANE)
    dbb_re = _block_diag_take(matmul(u_seg, g_re, ta=True, name="s5_db_re"), name="s5_db_re_diag")
    dbb_im = _block_diag_take(matmul(u_seg, g_im, ta=True, name="s5_db_im"), name="s5_db_im_diag")
    du_im = matmul(g_im, b_im_bd, tb=True, name="s5_du_im")
    du = _seg_deinterleave(matmul(g_re, b_re_bd, tb=True, res=du_im, name="s5_du_re")) + du_post
    da_re, da_im, dlog_step, db_re, db_im = whole_bwd(
        _s5_prep, prep_in, 5,
        [dab_re.reshape(S5_GROUPS, S5_STATE), dab_im.reshape(S5_GROUPS, S5_STATE), dbb_re, dbb_im], name="s5_prep_bwd")
    gcn = (S5_GROUPS, S5_GROUP, S5_STATE)
    grads.update(
        o_a_re=da_re[None], o_a_im=da_im[None], o_log_step=dlog_step.reshape(1, S5_GROUPS),
        o_b_re=db_re.reshape(gcn).transpose(0, 2, 1)[None], o_b_im=db_im.reshape(gcn).transpose(0, 2, 1)[None],
        o_c_re=_block_diag_take(dc_re_bd, name="s5_dc_re_diag").reshape(gcn)[None],
        o_c_im=-_block_diag_take(dc_im_neg_bd, name="s5_dc_im_diag").reshape(gcn)[None],
        o_d=dd_o, o_glu_w=[dglu_w], o_dw_w=ddw_w_o[None], o_dw_b=ddw_b_o, o_ln_g=dln_g, o_ln_b=dln_b)
    dproj1 = jnp.concatenate([d_cacg, du.astype(BF16)], axis=1)
    grads["o_w_in"] = [matmul(hn1, dproj1, ta=True, name="odd_in_dw")]
    dhn1 = matmul(dproj1, w_in_o, tb=True, name="odd_in_dx")
    dx2, dmix_norm1 = rms_bwd(x2, w["mix_norm"][1:2], dhn1, dx3, "mix1_norm_bwd")

    if early_reduce is not None:
        zero = early_reduce({("o_w_in", 0): grads["o_w_in"][0], ("o_glu_w", 0): grads["o_glu_w"][0],
                             ("o_w_out", 0): grads["o_w_out"][0], ("ffn_w_up", 1): g_ffn1["ffn_w_up"],
                             ("ffn_w_down", 1): g_ffn1["ffn_w_down"]})
        w["ffn_dw_b"] = w["ffn_dw_b"] + zero
    dx1, g_ffn0 = ffn_bwd(0, x1, ffn0_saved, dx2)
    if early_reduce is not None:
        dt_bias = dt_bias + early_reduce({("ffn_w_up", 0): g_ffn0["ffn_w_up"], ("ffn_w_down", 0): g_ffn0["ffn_w_down"]})
    for k in g_ffn0:
        per_layer = [g_ffn0[k], g_ffn1[k]]
        grads[k] = per_layer if k in ("ffn_w_up", "ffn_w_down") else jnp.stack(per_layer).reshape(w[k].shape)
    dmix0 = matmul(dx1, w_out_e, tb=True, name="even_out_dx")
    grads["e_w_out"] = [matmul(mix0, dx1, ta=True, name="even_out_dw")]
    if early_reduce is not None:
        a_log = a_log + early_reduce({("e_w_out", 0): grads["e_w_out"][0]})
    dq, dk, dv, dg = retention_bwd(proj0, cos2, sin2, ret_states, dmix0)
    dxbc_c, ddt, dz, ddt_bias, da_log, dd_skip, dssm_norm = ssd_bwd(
        xbc, dt_raw, proj0, dt_bias, a_log, d_skip, w["e_ssm_norm"], ssd_states, dmix0)
    dxbc, dconv_w, dconv_b = conv_bwd(proj0, conv_w_e, conv_b_e, dxbc_c, act=True, off=xbc_off,
                                      name="ssd_conv_bwd", dx_dtype=BF16)
    dproj0 = jnp.concatenate([dq, dk, dv, dg, dz, dxbc, ddt], axis=1)
    grads["e_w_in"] = [matmul(hn0, dproj0, ta=True, name="even_in_dw")[:, :EVEN_IN]]
    norm_w0 = w["mix_norm"][0:1]
    if early_reduce is not None:
        norm_w0 = norm_w0 + early_reduce({("e_w_in", 0): grads["e_w_in"][0]})
    dhn0 = matmul(dproj0, w_in_e, tb=True, name="even_in_dx")
    dx, dmix_norm0 = rms_bwd(x, norm_w0, dhn0, dx1, "mix0_norm_bwd")
    grads.update(
        mix_norm=jnp.concatenate([dmix_norm0, dmix_norm1], axis=0), e_conv_w=dconv_w[None], e_conv_b=dconv_b,
        e_dt_bias=ddt_bias[:, :SSM_HEADS], e_a_log=da_log[:, :SSM_HEADS], e_d=dd_skip[:, :SSM_HEADS],
        e_ssm_norm=dssm_norm)
    return loss, dx, grads


def adamw(w, g, m, v, *, name):
    shape = w.shape
    cols = shape[-1]
    rows = w.size // cols
    tr = _tile(rows, max(8, (512 * 1024 // cols) // 8 * 8), unit=8)

    def body(w_ref, g_ref, m_ref, v_ref, d_ref, nm_ref, nv_ref):
        gv = g_ref[...]
        nm = ADAM_B1 * m_ref[...] + (1.0 - ADAM_B1) * gv
        nv = ADAM_B2 * v_ref[...] + (1.0 - ADAM_B2) * jnp.square(gv)
        m_hat = nm / (1.0 - ADAM_B1 ** ADAM_STEP)
        v_hat = nv / (1.0 - ADAM_B2 ** ADAM_STEP)
        d_ref[...] = -ADAM_LR * (m_hat / (jnp.sqrt(v_hat) + ADAM_EPS) + ADAM_WD * w_ref[...])
        nm_ref[...] = nm
        nv_ref[...] = nv

    spec = pl.BlockSpec((tr, cols), lambda i: (i, 0))
    outs = pl.pallas_call(
        body, name=name, grid=(rows // tr,), in_specs=[spec] * 4, out_specs=[spec] * 3,
        out_shape=[jax.ShapeDtypeStruct((rows, cols), F32)] * 3, compiler_params=_params(("parallel",)),
    )(*[t.reshape(rows, cols) for t in (w, g, m, v)])
    return [o.reshape(shape) for o in outs]


OTHER_CHIPS = ((1, 0), (0, 1), (1, 1))
ANY = pl.BlockSpec(memory_space=pl.ANY)


def _position():
    return lax.axis_index("x"), lax.axis_index("y"), lax.axis_index("c")


def _flip(v, f):
    return 1 - v if f else v


def _remote(src, dst, send_sem, recv_sem, device):
    return pltpu.make_async_remote_copy(src_ref=src, dst_ref=dst, send_sem=send_sem, recv_sem=recv_sem,
                                        device_id=device, device_id_type=MESH)


def gather_shards(big, small):
    n_big, n_small = len(big), len(small)
    halves = [a.shape[0] // 2 for a in big]

    def body(*refs):
        big_refs, small_refs = refs[:n_big], refs[n_big:n_big + n_small]
        obig_refs = refs[n_big + n_small:2 * n_big + n_small]
        osmall_refs = refs[2 * n_big + n_small:2 * (n_big + n_small)]
        ici_send, ici_recv, d2d_send, d2d_recv, small_send, small_recv = refs[2 * (n_big + n_small):]
        x, y, c = _position()
        mine = 2 * x + y

        def half(k, core):
            return pl.ds(pl.multiple_of(core * halves[k], 16), halves[k])

        sends = []
        for j, (fx, fy) in enumerate(OTHER_CHIPS):
            peer = (_flip(x, fx), _flip(y, fy), c)
            for k in range(n_big):
                sends.append(_remote(big_refs[k].at[half(k, c)], obig_refs[k].at[mine, half(k, c)],
                                     ici_send.at[j, k], ici_recv.at[j, k], peer))
            for k in range(n_small):
                sends.append(_remote(small_refs[k], osmall_refs[k].at[mine], small_send.at[j, k], small_recv.at[j, k], peer))
        for cp in sends:
            cp.start()
        for j, (fx, fy) in enumerate(OTHER_CHIPS):
            px, py = _flip(x, fx), _flip(y, fy)
            src_chip = 2 * px + py
            for k in range(n_big):
                landed = obig_refs[k].at[src_chip, half(k, c)]
                _remote(landed, landed, ici_send.at[j, k], ici_recv.at[j, k], (px, py, c)).wait_recv()
                fwd = _remote(landed, landed, d2d_send.at[j, k], d2d_recv.at[j, k], (x, y, 1 - c))
                fwd.start()
                sends.append(fwd)
        for j, (fx, fy) in enumerate(OTHER_CHIPS):
            px, py = _flip(x, fx), _flip(y, fy)
            src_chip = 2 * px + py
            for k in range(n_big):
                other = obig_refs[k].at[src_chip, half(k, 1 - c)]
                _remote(other, other, d2d_send.at[j, k], d2d_recv.at[j, k], (x, y, 1 - c)).wait_recv()
            for k in range(n_small):
                dst = osmall_refs[k].at[src_chip]
                _remote(small_refs[k], dst, small_send.at[j, k], small_recv.at[j, k], (px, py, c)).wait_recv()
        for cp in sends:
            cp.wait_send()

    arrays = list(big) + list(small)
    dma = pltpu.SemaphoreType.DMA
    return pl.pallas_call(
        body, name="gather_shards", in_specs=[ANY] * len(arrays), out_specs=[ANY] * len(arrays),
        out_shape=[jax.ShapeDtypeStruct((4,) + a.shape, a.dtype) for a in arrays],
        scratch_shapes=[dma((3, n_big)), dma((3, n_big)), dma((3, n_big)), dma((3, n_big)),
                        dma((3, n_small)), dma((3, n_small))],
        compiler_params=_params(),
    )(*arrays)


def allreduce_small(pack):
    rows = pack.shape[0]
    half = rows // 2

    def body(p_ref, o_ref, sibling_pack, chip_sum, chip_halves, total, sems):
        x, y, c = _position()
        sibling = (x, y, 1 - c)
        swap = _remote(p_ref, sibling_pack, sems.at[0, 0], sems.at[1, 0], sibling)
        swap.start()
        swap.wait()
        chip_sum[...] = p_ref[...] + sibling_pack[...]
        mine = pl.ds(pl.multiple_of(c * half, 8), half)
        other = pl.ds(pl.multiple_of((1 - c) * half, 8), half)
        chip = 2 * x + y
        chip_halves[chip] = chip_sum[mine, :]
        sends = []
        for j, (fx, fy) in enumerate(OTHER_CHIPS):
            sends.append(_remote(chip_sum.at[mine], chip_halves.at[chip], sems.at[0, 1 + j], sems.at[1, 1 + j],
                                 (_flip(x, fx), _flip(y, fy), c)))
        for cp in sends:
            cp.start()
        for j, (fx, fy) in enumerate(OTHER_CHIPS):
            px, py = _flip(x, fx), _flip(y, fy)
            _remote(chip_sum.at[mine], chip_halves.at[2 * px + py], sems.at[0, 1 + j], sems.at[1, 1 + j], (px, py, c)).wait_recv()
        for cp in sends:
            cp.wait_send()
        total[...] = ((chip_halves[0] + chip_halves[1]) + chip_halves[2]) + chip_halves[3]
        o_ref[mine, :] = total[...]
        share = _remote(total, o_ref.at[mine], sems.at[0, 4], sems.at[1, 4], sibling)
        share.start()
        _remote(total, o_ref.at[other], sems.at[0, 4], sems.at[1, 4], sibling).wait_recv()
        share.wait_send()

    vmem = pl.BlockSpec(memory_space=pltpu.VMEM)
    return pl.pallas_call(
        body, name="allreduce_small", in_specs=[vmem], out_specs=vmem,
        out_shape=jax.ShapeDtypeStruct(pack.shape, F32),
        scratch_shapes=[pltpu.VMEM((rows, LANE), F32), pltpu.VMEM((rows, LANE), F32), pltpu.VMEM((4, half, LANE), F32),
                        pltpu.VMEM((half, LANE), F32), pltpu.SemaphoreType.DMA((2, 5))],
        compiler_params=_params(),
    )(pack)


def exchange_halves(gs, *, name):
    n = len(gs)

    def body(*refs):
        g_refs, o_refs, (send_sems, recv_sems) = refs[:n], refs[n:2 * n], refs[2 * n:]
        x, y, c = _position()
        copies = [_remote(g_refs[k].at[:, 1 - c], o_refs[k], send_sems.at[k], recv_sems.at[k], (x, y, 1 - c)) for k in range(n)]
        for cp in copies:
            cp.start()
        for cp in copies:
            cp.wait()

    return pl.pallas_call(
        body, name=name, in_specs=[ANY] * n, out_specs=[ANY] * n,
        out_shape=[jax.ShapeDtypeStruct((4,) + g.shape[2:], g.dtype) for g in gs],
        scratch_shapes=[pltpu.SemaphoreType.DMA((n,)), pltpu.SemaphoreType.DMA((n,))],
        compiler_params=_params(),
    )(*gs)


def scatter_to_chips(parts):
    n = len(parts)

    def body(*refs):
        a_refs, o_refs, (send_sems, recv_sems) = refs[:n], refs[n:2 * n], refs[2 * n:]
        x, y, c = _position()
        copies = []
        for j, (fx, fy) in enumerate(OTHER_CHIPS):
            px, py = _flip(x, fx), _flip(y, fy)
            for k in range(n):
                copies.append(_remote(a_refs[k].at[2 * px + py], o_refs[k].at[j], send_sems.at[j, k], recv_sems.at[j, k], (px, py, c)))
        for cp in copies:
            cp.start()
        for cp in copies:
            cp.wait()

    return pl.pallas_call(
        body, name="scatter_to_chips", in_specs=[ANY] * n, out_specs=[ANY] * n,
        out_shape=[jax.ShapeDtypeStruct((3,) + a.shape[1:], a.dtype) for a in parts],
        scratch_shapes=[pltpu.SemaphoreType.DMA((3, n)), pltpu.SemaphoreType.DMA((3, n))],
        compiler_params=_params(),
    )(*parts)


def swap_halves(rs):
    n = len(rs)

    def body(*refs):
        r_refs, o_refs, (send_sems, recv_sems) = refs[:n], refs[n:2 * n], refs[2 * n:]
        x, y, c = _position()
        copies = [_remote(r_refs[k], o_refs[k], send_sems.at[k], recv_sems.at[k], (x, y, 1 - c)) for k in range(n)]
        for cp in copies:
            cp.start()
        for cp in copies:
            cp.wait()

    dma = pltpu.SemaphoreType.DMA
    return pl.pallas_call(
        body, name="swap_halves", in_specs=[ANY] * n, out_specs=[ANY] * n,
        out_shape=[jax.ShapeDtypeStruct(r.shape, r.dtype) for r in rs],
        scratch_shapes=[dma((n,)), dma((n,))],
        compiler_params=_params(),
    )(*rs)


HBM = pl.BlockSpec(memory_space=pltpu.HBM)
SEM = pl.BlockSpec(memory_space=pltpu.SEMAPHORE)
SIDE_EFFECT = pltpu.SideEffectType.DATAFLOW_SIDE_EFFECTING


def _gather_plan(halves):
    def plan(v_refs, land_refs, x, y, c):
        copies = []
        for fx, fy in OTHER_CHIPS:
            for k in range(len(v_refs)):
                rows = pl.ds(pl.multiple_of(c * halves[k], 16), halves[k])
                copies.append((v_refs[k].at[rows], land_refs[k].at[2 * x + y, rows], (_flip(x, fx), _flip(y, fy), c)))
        return copies
    return plan


def _scatter_plan(v_refs, land_refs, x, y, c):
    copies = []
    for j, (fx, fy) in enumerate(OTHER_CHIPS):
        px, py = _flip(x, fx), _flip(y, fy)
        for k in range(len(v_refs)):
            copies.append((v_refs[k].at[2 * px + py], land_refs[k].at[j], (px, py, c)))
    return copies


def chip_exchange_start(srcs, land_shapes, plan, after, *, name):
    n = len(srcs)
    n_cp = 3 * n

    def body(*refs):
        v_refs, land_refs = refs[:n], refs[n:2 * n]
        outs = refs[2 * n + 1:]
        sends, recvs, token = outs[:n_cp], outs[n_cp:2 * n_cp], outs[-1]
        x, y, c = _position()
        for (src, dst, device), send, recv in zip(plan(v_refs, land_refs, x, y, c), sends, recvs, strict=True):
            _remote(src, dst, send, recv, device).start()
        token[...] = jnp.zeros_like(token)

    lands = [lax.empty(shape, v.dtype) for shape, v in zip(land_shapes, srcs)]
    arrays = [pltpu.with_memory_space_constraint(a, pltpu.HBM) for a in list(srcs) + lands]
    outs = pl.pallas_call(
        body, name=name,
        out_shape=tuple(pltpu.SemaphoreType.DMA(()) for _ in range(2 * n_cp))
        + tuple(pltpu.HBM(a.shape, a.dtype) for a in arrays) + (jax.ShapeDtypeStruct((8, LANE), F32),),
        in_specs=[HBM] * (2 * n) + [ANY],
        out_specs=(SEM,) * (2 * n_cp) + (HBM,) * (2 * n) + (pl.BlockSpec(memory_space=pltpu.VMEM),),
        input_output_aliases={i: 2 * n_cp + i for i in range(2 * n)},
        compiler_params=pltpu.CompilerParams(has_side_effects=SIDE_EFFECT),
    )(*arrays, after)
    handle = (outs[:n_cp], outs[n_cp:2 * n_cp], outs[2 * n_cp:2 * n_cp + n], outs[2 * n_cp + n:2 * n_cp + 2 * n])
    return handle, outs[-1]


def chip_exchange_wait(handle, plan, after, *, name):
    sends, recvs, v_thru, land_thru = handle
    n = len(v_thru)
    n_cp = 3 * n

    def body(*refs):
        v_refs, land_refs = refs[:n], refs[n:2 * n]
        sends, recvs = refs[2 * n:2 * n + n_cp], refs[2 * n + n_cp:2 * n + 2 * n_cp]
        x, y, c = _position()
        for (src, dst, device), send, recv in zip(plan(v_refs, land_refs, x, y, c), sends, recvs, strict=True):
            copy = _remote(src, dst, send, recv, device)
            copy.wait_send()
            copy.wait_recv()

    outs = pl.pallas_call(
        body, name=name,
        out_shape=tuple(pltpu.HBM(a.shape, a.dtype) for a in list(v_thru) + list(land_thru)),
        in_specs=[HBM] * (2 * n) + [SEM] * (2 * n_cp) + [ANY], out_specs=(HBM,) * (2 * n),
        input_output_aliases={i: i for i in range(2 * n)},
        compiler_params=pltpu.CompilerParams(has_side_effects=SIDE_EFFECT),
    )(*v_thru, *land_thru, *sends, *recvs, after)
    return outs[:n], outs[n:]


def finish_gather(lands):
    n = len(lands)
    halves = [a.shape[1] // 2 for a in lands]

    def body(*refs):
        o_refs, (send_sems, recv_sems) = refs[n:2 * n], refs[2 * n:]
        x, y, c = _position()

        def half(k, core):
            return pl.ds(pl.multiple_of(core * halves[k], 16), halves[k])

        sends = []
        for j, (fx, fy) in enumerate(OTHER_CHIPS):
            src_chip = 2 * _flip(x, fx) + _flip(y, fy)
            for k in range(n):
                held = o_refs[k].at[src_chip, half(k, c)]
                sends.append(_remote(held, held, send_sems.at[j, k], recv_sems.at[j, k], (x, y, 1 - c)))
        for cp in sends:
            cp.start()
        for j, (fx, fy) in enumerate(OTHER_CHIPS):
            src_chip = 2 * _flip(x, fx) + _flip(y, fy)
            for k in range(n):
                other = o_refs[k].at[src_chip, half(k, 1 - c)]
                _remote(other, other, send_sems.at[j, k], recv_sems.at[j, k], (x, y, 1 - c)).wait_recv()
        for cp in sends:
            cp.wait_send()

    dma = pltpu.SemaphoreType.DMA
    return pl.pallas_call(
        body, name="finish_gather", in_specs=[ANY] * n, out_specs=[ANY] * n,
        out_shape=[jax.ShapeDtypeStruct(a.shape, a.dtype) for a in lands],
        input_output_aliases={k: k for k in range(n)},
        scratch_shapes=[dma((3, n)), dma((3, n))],
        compiler_params=_params(),
    )(*lands)


def add_own_half(g, r, c_idx, *, name):
    _, _, h, cols = g.shape

    def body(c_ref, g_ref, r_ref, o_ref):
        o_ref[...] = (g_ref[0] + r_ref[...]).astype(o_ref.dtype)

    return pl.pallas_call(
        body, name=name,
        grid_spec=pltpu.PrefetchScalarGridSpec(
            num_scalar_prefetch=1, grid=(4,),
            in_specs=[pl.BlockSpec((1, 1, h, cols), lambda s, c: (s, c[0], 0, 0)),
                      pl.BlockSpec((1, h, cols), lambda s, c: (s, 0, 0))],
            out_specs=pl.BlockSpec((1, h, cols), lambda s, c: (s, 0, 0))),
        out_shape=jax.ShapeDtypeStruct(r.shape, BF16), compiler_params=_params(("parallel",)),
    )(c_idx, g, r)


def add_chip_parts(a, parts, chip_idx, *, name):
    _, h, cols = a.shape
    th = h // 2

    def body(s_ref, a_ref, p0_ref, p1_ref, p2_ref, o_ref):
        f = lambda r: r[0].astype(F32)
        o_ref[...] = ((f(a_ref) + f(p0_ref)) + f(p1_ref)) + f(p2_ref)

    part = lambda j: pl.BlockSpec((1, th, cols), lambda i, s, j=j: (j, i, 0))
    return pl.pallas_call(
        body, name=name,
        grid_spec=pltpu.PrefetchScalarGridSpec(
            num_scalar_prefetch=1, grid=(2,),
            in_specs=[pl.BlockSpec((1, th, cols), lambda i, s: (s[0], i, 0)), part(0), part(1), part(2)],
            out_specs=pl.BlockSpec((th, cols), lambda i, s: (i, 0))),
        out_shape=jax.ShapeDtypeStruct((h, cols), F32), compiler_params=_params(("parallel",)),
    )(chip_idx, a, parts, parts, parts)


WEIGHTS = ("mix_norm", "e_w_in", "e_conv_w", "e_conv_b", "e_dt_bias", "e_a_log", "e_d", "e_ssm_norm", "e_w_out",
           "o_w_in", "o_dw_w", "o_dw_b", "o_ln_g", "o_ln_b", "o_a_re", "o_a_im", "o_b_re", "o_b_im", "o_c_re",
           "o_c_im", "o_d", "o_log_step", "o_glu_w", "o_w_out", "ffn_norm", "ffn_w_up", "ffn_dw_w", "ffn_dw_b",
           "ffn_w_down", "final_norm")
BIG = (("e_w_in", 2), ("e_w_out", 1), ("o_w_in", 2), ("o_glu_w", 1), ("o_w_out", 1), ("ffn_w_up", 2), ("ffn_w_down", 1))
SMALL_SHARDED = (("e_conv_w", 2), ("o_dw_w", 2), ("o_dw_b", 1), ("o_ln_g", 1), ("o_ln_b", 1), ("o_d", 1), ("ffn_dw_w", 2))
REPLICATED = tuple(n for n in WEIGHTS if n not in dict(BIG + SMALL_SHARDED))
PACK_ROWS = 8


def _pack(arrays, dtype, row_unit=PACK_ROWS):
    flat = jnp.concatenate([a.astype(dtype).reshape(-1) for a in arrays])
    rows = -(-flat.size // (LANE * row_unit)) * row_unit
    return jnp.pad(flat, (0, rows * LANE - flat.size)).reshape(rows, LANE)


def _unpack(flat, shapes, lead=()):
    out, off = [], 0
    for shape in shapes:
        size = int(np.prod(shape))
        out.append(flat[..., off:off + size].reshape(lead + tuple(shape)))
        off += size
    return out


def _join_shards(parts, axis):
    return jnp.concatenate([parts[s] for s in range(4)], axis=axis)


def _split_shards(full, axis):
    return jnp.stack(jnp.split(full, 4, axis=axis))


def _rows2d(a):
    return a.reshape(-1, a.shape[-1])


def _layer_shards(g, axis):
    if g.ndim == 3:
        return g.reshape(4, 2, g.shape[1] // 2, g.shape[2])
    rows, cols = g.shape
    if axis == 0:
        return g.reshape(4, 2, rows // 8, cols)
    return g.reshape(rows, 4, cols // 4).transpose(1, 0, 2).reshape(4, 2, rows // 2, cols // 4)


def kernel(x, mix_norm, e_w_in, e_conv_w, e_conv_b, e_dt_bias, e_a_log, e_d, e_ssm_norm, e_w_out, o_w_in, o_dw_w, o_dw_b, o_ln_g, o_ln_b, o_a_re, o_a_im, o_b_re, o_b_im, o_c_re, o_c_im, o_d, o_log_step, o_glu_w, o_w_out, ffn_norm, ffn_w_up, ffn_dw_w, ffn_dw_b, ffn_w_down, final_norm, loss_target, m_mix_norm, m_e_w_in, m_e_conv_w, m_e_conv_b, m_e_dt_bias, m_e_a_log, m_e_d, m_e_ssm_norm, m_e_w_out, m_o_w_in, m_o_dw_w, m_o_dw_b, m_o_ln_g, m_o_ln_b, m_o_a_re, m_o_a_im, m_o_b_re, m_o_b_im, m_o_c_re, m_o_c_im, m_o_d, m_o_log_step, m_o_glu_w, m_o_w_out, m_ffn_norm, m_ffn_w_up, m_ffn_dw_w, m_ffn_dw_b, m_ffn_w_down, m_final_norm, v_mix_norm, v_e_w_in, v_e_conv_w, v_e_conv_b, v_e_dt_bias, v_e_a_log, v_e_d, v_e_ssm_norm, v_e_w_out, v_o_w_in, v_o_dw_w, v_o_dw_b, v_o_ln_g, v_o_ln_b, v_o_a_re, v_o_a_im, v_o_b_re, v_o_b_im, v_o_c_re, v_o_c_im, v_o_d, v_o_log_step, v_o_glu_w, v_o_w_out, v_ffn_norm, v_ffn_w_up, v_ffn_dw_w, v_ffn_dw_b, v_ffn_w_down, v_final_norm):
    given = dict(locals())
    chip = 2 * lax.axis_index("x") + lax.axis_index("y")
    core = lax.axis_index("c")

    core_idx, chip_idx = core.reshape(1).astype(jnp.int32), chip.reshape(1).astype(jnp.int32)

    def whole(n, axis, parts):
        shape = given[n].shape
        own = given[n].astype(parts.dtype)
        return _join_shards(lax.dynamic_update_index_in_dim(parts.reshape((4,) + shape), own, chip, 0), axis)

    first, later = BIG[:1], BIG[1:]
    shards = {n: _rows2d(given[n]).astype(BF16) for n, _ in BIG}
    gathered = gather_shards([shards[n] for n, _ in first], [_rows2d(given[n]) for n, _ in SMALL_SHARDED])
    w = {n: given[n] for n in REPLICATED}
    for (n, axis), parts in zip(first + SMALL_SHARDED, gathered):
        w[n] = whole(n, axis, parts)
    later_keys = [(n, layer) for n, _ in later for layer in (range(2) if n.startswith("ffn_") else [None])]
    later_shards = [shards[n] if layer is None else given[n][layer].astype(BF16) for n, layer in later_keys]
    gather_plan = _gather_plan([a.shape[0] // 2 for a in later_shards])
    gather_handle, token = chip_exchange_start(later_shards, [(4,) + a.shape for a in later_shards], gather_plan,
                                               gathered[0], name="gather_start")
    w["mix_norm"] = w["mix_norm"] + token[0, 0]

    def late_weights(after):
        _, lands = chip_exchange_wait(gather_handle, gather_plan, after, name="gather_wait")
        out = {"ffn_w_up_pairs": [], "ffn_w_down": []}
        for (n, layer), own, parts in zip(later_keys, later_shards, finish_gather(lands)):
            if layer is None:
                out[n] = whole(n, dict(BIG)[n], parts)
                continue
            parts = lax.dynamic_update_index_in_dim(parts, own, chip, 0)
            if n == "ffn_w_up":
                out["ffn_w_up_pairs"].append(ffn_shards_to_pairs(parts, name=f"ffn{layer}_up_pairs"))
            else:
                out[n].append(parts.reshape(-1, parts.shape[-1]))
        return out

    groups = []

    def finish_group(after):
        group = groups[-1]
        group["sums"], group["parts"] = chip_exchange_wait(group.pop("handle"), _scatter_plan, after,
                                                           name=f"scatter_wait_{len(groups) - 1}")

    def early_reduce(layer_grads):
        keys = list(layer_grads)
        if groups:
            finish_group(layer_grads[keys[0]])
        tag = len(groups)
        parts = [_layer_shards(layer_grads[k], dict(BIG)[k[0]] - 1) for k in keys]
        sums = [add_own_half(g, r, core_idx, name=f"add_own_half_{n}{layer}")
                for g, r, (n, layer) in zip(parts, exchange_halves(parts, name=f"exchange_halves_{tag}"), keys)]
        handle, zeros = chip_exchange_start(sums, [(3,) + a.shape[1:] for a in sums], _scatter_plan, sums[0],
                                            name=f"scatter_start_{tag}")
        groups.append(dict(keys=keys, handle=handle))
        return zeros[0, 0]

    loss, dx, grads = local_step(x[0], loss_target[0], w, late_weights, early_reduce)
    finish_group(dx)
    early_keys = [k for group in groups for k in group["keys"]]
    early_sums = [a for group in groups for a in group["sums"]]
    early_parts = [a for group in groups for a in group["parts"]]

    small_names = REPLICATED + tuple(n for n, _ in SMALL_SHARDED)
    small_sum = allreduce_small(_pack([grads[n] for n in small_names], F32, row_unit=16))
    reduced = dict(zip(small_names, _unpack(small_sum.reshape(-1), [grads[n].shape for n in small_names])))
    for n, axis in SMALL_SHARDED:
        width = given[n].shape[axis]
        reduced[n] = lax.dynamic_slice_in_dim(reduced[n], chip * width, width, axis=axis)

    keys, parts = [], []
    for n, axis in BIG:
        for layer, g in enumerate(grads[n]):
            if (n, layer) not in early_keys:
                keys.append((n, layer))
                parts.append(_layer_shards(g, axis - 1))
    core_sums, chip_parts = [], []
    if parts:
        core_sums = [add_own_half(g, r, core_idx, name=f"add_own_half_{n}{layer}")
                     for g, r, (n, layer) in zip(parts, exchange_halves(parts, name="exchange_halves_last"), keys)]
        chip_parts = list(scatter_to_chips(core_sums))
    keys, core_sums, chip_parts = early_keys + keys, early_sums + core_sums, early_parts + chip_parts
    mine = [add_chip_parts(a, p, chip_idx, name=f"add_chip_parts_{n}{layer}")
            for a, p, (n, layer) in zip(core_sums, chip_parts, keys)]
    layers = {}
    for (n, layer), own, other in zip(keys, mine, swap_halves(mine)):
        both = jnp.where(core == 0, jnp.stack([own, other]), jnp.stack([other, own]))
        layers.setdefault(n, {})[layer] = both.reshape(given[n].shape[1:])
    for n, _ in BIG:
        reduced[n] = jnp.stack([layers[n][layer] for layer in sorted(layers[n])])

    delta, new_m, new_v = {}, {}, {}
    for n in WEIGHTS:
        delta[n], new_m[n], new_v[n] = adamw(given[n], reduced[n], given["m_" + n], given["v_" + n], name="adamw_" + n)

    total = lax.psum(loss[0, 0], ("x", "y", "c"))
    return (total, dx[None], *[reduced[n] for n in WEIGHTS], *[delta[n] for n in WEIGHTS],
            *[new_m[n] for n in WEIGHTS], *[new_v[n] for n in WEIGHTS])
```

```python
import functools
import math
from typing import NamedTuple

import numpy as np
import jax
import jax.numpy as jnp
from jax import lax
from jax.experimental import pallas as pl
from jax.experimental.pallas import tpu as pltpu

F32 = jnp.float32
BF16 = jnp.bfloat16
HIGHEST = lax.Precision.HIGHEST
MESH = pl.DeviceIdType.MESH

D_MODEL = 1024
EPS = 1e-6
RET_HEADS, RET_DK, RET_DV, CHUNK = 4, 128, 256, 128
ROPE_BASE = 10000.0
SSM_HEADS, SSM_P, SSM_N, SSM_GROUPS = 16, 64, 128, 2
SSM_DINNER = SSM_HEADS * SSM_P
EVEN_IN, EVEN_IN_PAD = 5648, 5760
S5_GROUPS, S5_GROUP, S5_STATE = 32, 16, 64
S5_LANES = S5_GROUPS * S5_STATE
SCAN_SEG = 32
D_FF = 2816
ADAM_LR, ADAM_B1, ADAM_B2, ADAM_EPS, ADAM_WD, ADAM_STEP = 0.001, 0.9, 0.999, 1e-08, 0.01, 10

LANE = 128
VMEM_LIMIT = 56 * 1024 * 1024


def _params(sem=None, **kw):
    return pltpu.CompilerParams(dimension_semantics=sem, vmem_limit_bytes=VMEM_LIMIT, **kw)


def _tile(n, target, unit=LANE):
    if n <= target:
        return n
    t = (target // unit) * unit
    while t >= unit:
        if n % t == 0:
            return t
        t -= unit
    return n


def _silu(x):
    return x * jax.nn.sigmoid(x)


def _mm(a, b):
    return jnp.dot(a.astype(BF16), b.astype(BF16), preferred_element_type=F32)


def _mm_nt(a, b):
    return lax.dot_general(a.astype(BF16), b.astype(BF16), (((1,), (1,)), ((), ())), preferred_element_type=F32)


def _mm_tn(a, b):
    return lax.dot_general(a.astype(BF16), b.astype(BF16), (((0,), (0,)), ((), ())), preferred_element_type=F32)


def _dot_hi(a, b):
    return jnp.dot(a, b, precision=HIGHEST, preferred_element_type=F32)


def _dot_hi_tn(a, b):
    return lax.dot_general(a, b, (((0,), (0,)), ((), ())), precision=HIGHEST, preferred_element_type=F32)


def _bf16_parts(v):
    hi = v.astype(BF16)
    rest = v - hi.astype(F32)
    mid = rest.astype(BF16)
    return hi, mid, (rest - mid.astype(F32)).astype(BF16)


def _dot_parts(v, fixed, dims, v_first):
    fixed = fixed.astype(BF16)
    out = None
    for part in _bf16_parts(v):
        ops = (part, fixed) if v_first else (fixed, part)
        p = lax.dot_general(*ops, (dims, ((), ())), preferred_element_type=F32)
        out = p if out is None else out + p
    return out


@jax.custom_vjp
def _times_01(v, ones):
    return _dot_parts(v, ones, ((1,), (0,)), True)


_times_01.defvjp(lambda v, ones: (_times_01(v, ones), ones),
                 lambda ones, g: (_dot_parts(g, ones, ((1,), (1,)), True), jnp.zeros_like(ones)))


@jax.custom_vjp
def _01_times(ones, v):
    return _dot_parts(v, ones, ((1,), (0,)), False)


_01_times.defvjp(lambda ones, v: (_01_times(ones, v), ones),
                 lambda ones, g: (jnp.zeros_like(ones), _dot_parts(g, ones, ((0,), (0,)), False)))


MATMUL_VMEM = 44 * 1024 * 1024


def matmul(a, b, *, ta=False, tb=False, res=None, out_dtype=F32, name):
    m, k = (a.shape[1], a.shape[0]) if ta else a.shape
    n = b.shape[0] if tb else b.shape[1]
    assert (b.shape[1] if tb else b.shape[0]) == k, (a.shape, b.shape, ta, tb)
    tm = _tile(m, 1536)
    tn = _tile(n, 640)
    if tn < 384:
        tn = _tile(n, 1536)
    res_bytes = 0 if res is None else res.dtype.itemsize

    def vmem(tm, tn):
        return 2 * (tm * k * a.dtype.itemsize + tn * k * b.dtype.itemsize + tm * tn * (jnp.dtype(out_dtype).itemsize + res_bytes))

    while vmem(tm, tn) > MATMUL_VMEM and tm % (2 * LANE) == 0:
        tm //= 2
    assert vmem(tm, tn) <= MATMUL_VMEM, (name, tm, tn, k)
    a_spec = pl.BlockSpec((k, tm), lambda i, j: (0, i)) if ta else pl.BlockSpec((tm, k), lambda i, j: (i, 0))
    b_spec = pl.BlockSpec((tn, k), lambda i, j: (j, 0)) if tb else pl.BlockSpec((k, tn), lambda i, j: (0, j))
    o_spec = pl.BlockSpec((tm, tn), lambda i, j: (i, j))
    dims = (((0 if ta else 1,), (1 if tb else 0,)), ((), ()))
    has_res = res is not None

    def body(a_ref, b_ref, *rest):
        o_ref = rest[-1]
        out = lax.dot_general(a_ref[...].astype(BF16), b_ref[...].astype(BF16), dims, preferred_element_type=F32)
        if has_res:
            out = out + rest[0][...].astype(F32)
        o_ref[...] = out.astype(o_ref.dtype)

    ins = [a, b] + ([res] if has_res else [])
    specs = [a_spec, b_spec] + ([o_spec] if has_res else [])
    return pl.pallas_call(
        body, name=name, grid=(m // tm, n // tn), in_specs=specs, out_specs=o_spec,
        out_shape=jax.ShapeDtypeStruct((m, n), out_dtype), compiler_params=_params(("parallel", "parallel")),
    )(*ins)


class Cols(NamedTuple):
    arr: jax.Array
    w: int
    j: int


def _cols(a):
    return a if isinstance(a, Cols) else Cols(a, a.shape[1], 0)


def _row_spec(c, tl):
    return pl.BlockSpec((tl, c.w), lambda i, j=c.j: (i, j))


def _whole_spec(p):
    return pl.BlockSpec(p.shape, lambda i, nd=p.ndim: (0,) * nd)


def rowwise_fwd(fn, rows, aux, pars, consts, outs, *, name, tl):
    rows = [_cols(r) for r in rows + aux]
    whole = list(pars) + list(consts)
    n_rows = len(rows)
    n_whole = len(whole)
    length = rows[0].arr.shape[0]
    tl = min(tl, length)

    def body(*refs):
        vals = [r[...].astype(F32) for r in refs[:n_rows]] + [r[...] for r in refs[n_rows:n_rows + n_whole]]
        res = fn(*vals)
        for o_ref, v in zip(refs[n_rows + n_whole:], res, strict=True):
            o_ref[...] = v.astype(o_ref.dtype)

    return pl.pallas_call(
        body, name=name, grid=(length // tl,),
        in_specs=[_row_spec(r, tl) for r in rows] + [_whole_spec(p) for p in whole],
        out_specs=[pl.BlockSpec((tl, w), lambda i: (i, 0)) for w, _ in outs],
        out_shape=[jax.ShapeDtypeStruct((length, w), dt) for w, dt in outs],
        compiler_params=_params(("parallel",)),
    )(*[r.arr for r in rows], *whole)


def rowwise_bwd(fn, rows, aux, pars, consts, cots, drow_dtypes, *, name, tl, add=None, merge=False):
    rows = [_cols(r) for r in rows]
    aux = [_cols(r) for r in aux]
    cots = [_cols(r) for r in cots]
    n_r, n_a, n_p, n_c, n_t = len(rows), len(aux), len(pars), len(consts), len(cots)
    length = rows[0].arr.shape[0]
    tl = min(tl, length)
    has_add = add is not None
    widths = [r.w for r in rows]

    def body(*refs):
        pos = 0
        r_vals = [r[...].astype(F32) for r in refs[pos:pos + n_r]]; pos += n_r
        a_vals = [r[...].astype(F32) for r in refs[pos:pos + n_a]]; pos += n_a
        p_vals = [r[...].astype(F32) for r in refs[pos:pos + n_p]]; pos += n_p
        c_vals = [r[...] for r in refs[pos:pos + n_c]]; pos += n_c
        t_vals = [r[...].astype(F32) for r in refs[pos:pos + n_t]]; pos += n_t
        add_val = None
        if has_add:
            add_val = refs[pos][...].astype(F32); pos += 1
        n_dr = 1 if merge else n_r
        dr_refs = refs[pos:pos + n_dr]; pos += n_dr
        dp_refs = refs[pos:pos + n_p]

        def f(*rp):
            return fn(*rp[:n_r], *a_vals, *rp[n_r:], *c_vals)

        _, vjp = jax.vjp(f, *r_vals, *p_vals)
        grads = vjp(tuple(t_vals))
        drows = list(grads[:n_r])
        if has_add:
            drows[0] = drows[0] + add_val
        if merge:
            off = 0
            for w, d in zip(widths, drows):
                dr_refs[0][:, off:off + w] = d.astype(dr_refs[0].dtype)
                off += w
        else:
            for r, d in zip(dr_refs, drows):
                r[...] = d.astype(r.dtype)
        i = pl.program_id(0)
        for r, d in zip(dp_refs, grads[n_r:]):
            @pl.when(i == 0)
            def _(r=r, d=d):
                r[...] = d

            @pl.when(i > 0)
            def _(r=r, d=d):
                r[...] += d

    if merge:
        dr_specs = [pl.BlockSpec((tl, sum(widths)), lambda i: (i, 0))]
        dr_shapes = [jax.ShapeDtypeStruct((length, sum(widths)), drow_dtypes[0])]
    else:
        dr_specs = [pl.BlockSpec((tl, w), lambda i: (i, 0)) for w in widths]
        dr_shapes = [jax.ShapeDtypeStruct((length, w), dt) for w, dt in zip(widths, drow_dtypes)]
    ins = [r.arr for r in rows + aux] + list(pars) + list(consts) + [r.arr for r in cots] + ([add] if has_add else [])
    specs = ([_row_spec(r, tl) for r in rows + aux] + [_whole_spec(p) for p in list(pars) + list(consts)]
             + [_row_spec(r, tl) for r in cots] + ([pl.BlockSpec((tl, add.shape[1]), lambda i: (i, 0))] if has_add else []))
    return pl.pallas_call(
        body, name=name, grid=(length // tl,), in_specs=specs,
        out_specs=dr_specs + [_whole_spec(p) for p in pars],
        out_shape=dr_shapes + [jax.ShapeDtypeStruct(p.shape, F32) for p in pars],
        compiler_params=_params(("arbitrary",)),
    )(*ins)


def whole_fwd(fn, ins, out_shapes, *, name):
    n_in = len(ins)

    def body(*refs):
        res = fn(*[r[...] for r in refs[:n_in]])
        for o_ref, v in zip(refs[n_in:], res, strict=True):
            o_ref[...] = v

    return pl.pallas_call(body, name=name, out_shape=[jax.ShapeDtypeStruct(s, F32) for s in out_shapes],
                          compiler_params=_params())(*ins)


def whole_bwd(fn, ins, n_diff, cots, *, name):
    n_in, n_t = len(ins), len(cots)

    def body(*refs):
        vals = [r[...] for r in refs[:n_in]]
        t_vals = [r[...] for r in refs[n_in:n_in + n_t]]
        _, vjp = jax.vjp(lambda *d: fn(*d, *vals[n_diff:]), *vals[:n_diff])
        for o_ref, g in zip(refs[n_in + n_t:], vjp(tuple(t_vals)), strict=True):
            o_ref[...] = g

    return pl.pallas_call(body, name=name, out_shape=[jax.ShapeDtypeStruct(a.shape, F32) for a in ins[:n_diff]],
                          compiler_params=_params())(*ins, *cots)


CONV_ROWS = 256


def _conv_geometry(x, w, cw, off):
    width = w.shape[1]
    x = Cols(x, width, 0)
    length = x.arr.shape[0]
    taps = w.shape[0]
    pad = -(-(taps - 1) // 8) * 8
    assert off % cw == 0 and width % cw == 0, (off, width, cw)
    return x, length, taps, pad, off // cw


def _conv_taps(xp_ref, w_ref, base, taps, pad, init, lanes=slice(None)):
    acc = init
    for k in range(taps):
        acc = acc + w_ref[k:k + 1, lanes] * xp_ref[pl.ds(base + pad - (taps - 1) + k, init.shape[0]), :]
    return acc


def conv_fwd(x, w, b, *, act, name, off=0, cw=LANE, out_dtype=F32):
    x, length, taps, pad, jb = _conv_geometry(x, w, cw, off)
    rc = min(CONV_ROWS, length)

    def body(x_ref, w_ref, b_ref, o_ref, xp_ref):
        xp_ref[0:pad, :] = jnp.zeros((pad, cw), F32)
        xp_ref[pad:pad + length, :] = x_ref[...].astype(F32)

        def chunk(r, carry):
            base = pl.multiple_of(r * rc, rc)
            acc = _conv_taps(xp_ref, w_ref, base, taps, pad, jnp.broadcast_to(b_ref[...], (rc, cw)))
            if act:
                acc = _silu(acc)
            o_ref[pl.ds(base, rc), :] = acc.astype(o_ref.dtype)
            return carry

        lax.fori_loop(0, length // rc, chunk, 0)

    return pl.pallas_call(
        body, name=name, grid=(x.w // cw,),
        in_specs=[pl.BlockSpec((length, cw), lambda j: (0, jb + j)), pl.BlockSpec((taps, cw), lambda j: (0, j)),
                  pl.BlockSpec((1, cw), lambda j: (0, j))],
        out_specs=pl.BlockSpec((length, cw), lambda j: (0, j)),
        out_shape=jax.ShapeDtypeStruct((length, x.w), out_dtype),
        scratch_shapes=[pltpu.VMEM((pad + length, cw), F32)],
        compiler_params=_params(("parallel",)),
    )(x.arr, w, b)


def conv_bwd(x, w, b, dy, *, act, name, off=0, cw=LANE, dx_dtype=F32):
    x, length, taps, pad, jb = _conv_geometry(x, w, cw, off)
    rc = min(CONV_ROWS, length)

    def body(x_ref, w_ref, b_ref, dy_ref, dx_ref, dw_ref, db_ref, xp_ref, gp_ref):
        xp_ref[0:pad, :] = jnp.zeros((pad, cw), F32)
        xp_ref[pad:pad + length, :] = x_ref[...].astype(F32)
        gp_ref[length:length + pad, :] = jnp.zeros((pad, cw), F32)
        if act:
            def pre_chunk(r, carry):
                base = pl.multiple_of(r * rc, rc)
                pre = _conv_taps(xp_ref, w_ref, base, taps, pad, jnp.broadcast_to(b_ref[...], (rc, cw)))
                sig = jax.nn.sigmoid(pre)
                gp_ref[pl.ds(base, rc), :] = dy_ref[pl.ds(base, rc), :].astype(F32) * (sig * (1.0 + pre * (1.0 - sig)))
                return carry

            lax.fori_loop(0, length // rc, pre_chunk, 0)
        else:
            gp_ref[0:length, :] = dy_ref[...].astype(F32)
        dw_ref[...] = jnp.zeros((taps, cw), F32)
        db_ref[...] = jnp.zeros((1, cw), F32)

        def chunk(r, carry):
            base = pl.multiple_of(r * rc, rc)
            acc = jnp.zeros((rc, cw), F32)
            g = gp_ref[pl.ds(base, rc), :]
            for k in range(taps):
                acc = acc + w_ref[k:k + 1, :] * gp_ref[pl.ds(base + (taps - 1) - k, rc), :]
                xs = xp_ref[pl.ds(base + pad - (taps - 1) + k, rc), :]
                dw_ref[k:k + 1, :] += jnp.sum(g * xs, axis=0, keepdims=True)
            db_ref[...] += jnp.sum(g, axis=0, keepdims=True)
            dx_ref[pl.ds(base, rc), :] = acc.astype(dx_ref.dtype)
            return carry

        lax.fori_loop(0, length // rc, chunk, 0)

    dy = _cols(dy)
    assert dy.j == 0 and dy.w == x.w
    return pl.pallas_call(
        body, name=name, grid=(x.w // cw,),
        in_specs=[pl.BlockSpec((length, cw), lambda j: (0, jb + j)), pl.BlockSpec((taps, cw), lambda j: (0, j)),
                  pl.BlockSpec((1, cw), lambda j: (0, j)), pl.BlockSpec((length, cw), lambda j: (0, j))],
        out_specs=[pl.BlockSpec((length, cw), lambda j: (0, j)), pl.BlockSpec((taps, cw), lambda j: (0, j)),
                   pl.BlockSpec((1, cw), lambda j: (0, j))],
        out_shape=[jax.ShapeDtypeStruct((length, x.w), dx_dtype), jax.ShapeDtypeStruct((taps, x.w), F32),
                   jax.ShapeDtypeStruct((1, x.w), F32)],
        scratch_shapes=[pltpu.VMEM((pad + length, cw), F32), pltpu.VMEM((length + pad, cw), F32)],
        compiler_params=_params(("parallel",)),
    )(x.arr, w, b, dy.arr)


def _conv_transpose(xp_ref, gp_ref, w_ref, dx_ref, dw_ref, db_ref, lanes, length, taps, pad, rc):
    dw_ref[:, lanes] = jnp.zeros((taps, LANE), F32)
    db_ref[:, lanes] = jnp.zeros((1, LANE), F32)

    def chunk(r, carry):
        base = pl.multiple_of(r * rc, rc)
        acc = jnp.zeros((rc, LANE), F32)
        g = gp_ref[pl.ds(base, rc), :]
        for k in range(taps):
            acc = acc + w_ref[k:k + 1, lanes] * gp_ref[pl.ds(base + (taps - 1) - k, rc), :]
            xs = xp_ref[pl.ds(base + pad - (taps - 1) + k, rc), :]
            dw_ref[k:k + 1, lanes] += jnp.sum(g * xs, axis=0, keepdims=True)
        db_ref[:, lanes] += jnp.sum(g, axis=0, keepdims=True)
        dx_ref[pl.ds(base, rc), lanes] = acc.astype(dx_ref.dtype)
        return carry

    lax.fori_loop(0, length // rc, chunk, 0)


LANE_PAIR_ROWS = 1024


def ffn_interleave(a, name=None):
    rows, width = a.shape
    nb = width // (2 * LANE)
    if rows < LANE_PAIR_ROWS:
        return a.reshape(rows, 2, nb, LANE).swapaxes(1, 2).reshape(a.shape)

    def body(g_ref, u_ref, o_ref):
        o_ref[:, 0:LANE] = g_ref[...]
        o_ref[:, LANE:2 * LANE] = u_ref[...]

    tr = LANE_PAIR_ROWS
    return pl.pallas_call(
        body, name=name, grid=(rows // tr, nb),
        in_specs=[pl.BlockSpec((tr, LANE), lambda i, j: (i, j)), pl.BlockSpec((tr, LANE), lambda i, j: (i, nb + j))],
        out_specs=pl.BlockSpec((tr, 2 * LANE), lambda i, j: (i, j)),
        out_shape=jax.ShapeDtypeStruct(a.shape, a.dtype), compiler_params=_params(("parallel", "parallel")),
    )(a, a)


def ffn_deinterleave(a):
    rows, width = a.shape
    return a.reshape(rows, width // (2 * LANE), 2, LANE).swapaxes(1, 2).reshape(a.shape)


PAIR_COPY_ROWS = 512


def ffn_pairs_to_shards(a, *, name):
    rows, width = a.shape
    cols = width // 4
    per = cols // LANE
    tr = min(rows, PAIR_COPY_ROWS)

    def body(a_ref, o_ref):
        is_up = pl.program_id(1) >= 2
        for parity, chosen in ((0, jnp.logical_not(is_up)), (1, is_up)):
            @pl.when(chosen)
            def _(parity=parity):
                for t in range(per):
                    o_ref[0, :, t * LANE:(t + 1) * LANE] = a_ref[:, (2 * t + parity) * LANE:(2 * t + parity + 1) * LANE]

    return pl.pallas_call(
        body, name=name, grid=(rows // tr, 4),
        in_specs=[pl.BlockSpec((tr, 2 * cols), lambda i, s: (i, jnp.where(s >= 2, s - 2, s)))],
        out_specs=pl.BlockSpec((1, tr, cols), lambda i, s: (s, i, 0)),
        out_shape=jax.ShapeDtypeStruct((4, rows, cols), a.dtype), compiler_params=_params(("parallel", "parallel")),
    )(a)


def ffn_shards_to_pairs(parts, *, name):
    _, rows, cols = parts.shape
    per = cols // LANE
    tr = min(rows, PAIR_COPY_ROWS)

    def body(gate_ref, up_ref, o_ref):
        for t in range(per):
            o_ref[:, 2 * t * LANE:(2 * t + 1) * LANE] = gate_ref[0, :, t * LANE:(t + 1) * LANE]
            o_ref[:, (2 * t + 1) * LANE:(2 * t + 2) * LANE] = up_ref[0, :, t * LANE:(t + 1) * LANE]

    return pl.pallas_call(
        body, name=name, grid=(rows // tr, 2),
        in_specs=[pl.BlockSpec((1, tr, cols), lambda i, j: (j, i, 0)), pl.BlockSpec((1, tr, cols), lambda i, j: (2 + j, i, 0))],
        out_specs=pl.BlockSpec((tr, 2 * cols), lambda i, j: (i, j)),
        out_shape=jax.ShapeDtypeStruct((rows, 4 * cols), parts.dtype), compiler_params=_params(("parallel", "parallel")),
    )(parts, parts)


GATE, UP = slice(0, LANE), slice(LANE, 2 * LANE)


def _ffn_geometry(a, w):
    length, width = a.shape
    taps = w.shape[0]
    return length, width, width // (2 * LANE), taps, -(-(taps - 1) // 8) * 8, min(CONV_ROWS, length)


def _ffn_pre(xg_ref, xu_ref, w_ref, b_ref, base, taps, pad, rc):
    gate = _conv_taps(xg_ref, w_ref, base, taps, pad, jnp.broadcast_to(b_ref[:, GATE], (rc, LANE)), GATE)
    up = _conv_taps(xu_ref, w_ref, base, taps, pad, jnp.broadcast_to(b_ref[:, UP], (rc, LANE)), UP)
    return gate, up


def ffn_conv_act(a, w, b, *, name):
    length, width, nb, taps, pad, rc = _ffn_geometry(a, w)

    def body(a_ref, w_ref, b_ref, o_ref, xg_ref, xu_ref):
        for xp_ref, lanes in ((xg_ref, GATE), (xu_ref, UP)):
            xp_ref[0:pad, :] = jnp.zeros((pad, LANE), F32)
            xp_ref[pad:pad + length, :] = a_ref[:, lanes].astype(F32)

        def chunk(r, carry):
            base = pl.multiple_of(r * rc, rc)
            gate, up = _ffn_pre(xg_ref, xu_ref, w_ref, b_ref, base, taps, pad, rc)
            o_ref[pl.ds(base, rc), :] = (_silu(gate) * up).astype(o_ref.dtype)
            return carry

        lax.fori_loop(0, length // rc, chunk, 0)

    pair = lambda rows: pl.BlockSpec((rows, 2 * LANE), lambda j: (0, j))
    return pl.pallas_call(
        body, name=name, grid=(nb,), in_specs=[pair(length), pair(taps), pair(1)],
        out_specs=pl.BlockSpec((length, LANE), lambda j: (0, j)),
        out_shape=jax.ShapeDtypeStruct((length, width // 2), BF16),
        scratch_shapes=[pltpu.VMEM((pad + length, LANE), F32), pltpu.VMEM((pad + length, LANE), F32)],
        compiler_params=_params(("parallel",)),
    )(a, w, b)


def ffn_conv_act_bwd(a, w, b, dact, *, name):
    length, width, nb, taps, pad, rc = _ffn_geometry(a, w)

    def body(a_ref, w_ref, b_ref, dy_ref, da_ref, dw_ref, db_ref, xg_ref, xu_ref, gg_ref, gu_ref):
        for xp_ref, lanes in ((xg_ref, GATE), (xu_ref, UP)):
            xp_ref[0:pad, :] = jnp.zeros((pad, LANE), F32)
            xp_ref[pad:pad + length, :] = a_ref[:, lanes].astype(F32)
        for gp_ref in (gg_ref, gu_ref):
            gp_ref[length:length + pad, :] = jnp.zeros((pad, LANE), F32)

        def pre_chunk(r, carry):
            base = pl.multiple_of(r * rc, rc)
            gate, up = _ffn_pre(xg_ref, xu_ref, w_ref, b_ref, base, taps, pad, rc)
            sig = jax.nn.sigmoid(gate)
            dy = dy_ref[pl.ds(base, rc), :]
            gg_ref[pl.ds(base, rc), :] = dy * up * (sig * (1.0 + gate * (1.0 - sig)))
            gu_ref[pl.ds(base, rc), :] = dy * (gate * sig)
            return carry

        lax.fori_loop(0, length // rc, pre_chunk, 0)
        _conv_transpose(xg_ref, gg_ref, w_ref, da_ref, dw_ref, db_ref, GATE, length, taps, pad, rc)
        _conv_transpose(xu_ref, gu_ref, w_ref, da_ref, dw_ref, db_ref, UP, length, taps, pad, rc)

    pair = lambda rows: pl.BlockSpec((rows, 2 * LANE), lambda j: (0, j))
    return pl.pallas_call(
        body, name=name, grid=(nb,),
        in_specs=[pair(length), pair(taps), pair(1), pl.BlockSpec((length, LANE), lambda j: (0, j))],
        out_specs=[pair(length), pair(taps), pair(1)],
        out_shape=[jax.ShapeDtypeStruct((length, width), BF16), jax.ShapeDtypeStruct((taps, width), F32),
                   jax.ShapeDtypeStruct((1, width), F32)],
        scratch_shapes=[pltpu.VMEM((pad + length, LANE), F32), pltpu.VMEM((pad + length, LANE), F32),
                        pltpu.VMEM((length + pad, LANE), F32), pltpu.VMEM((length + pad, LANE), F32)],
        compiler_params=_params(("parallel",)),
    )(a, w, b, dact)


def _retention_consts():
    h = np.arange(RET_HEADS, dtype=np.float32)
    log_g = np.log1p(-(2.0 ** (-5.0 - h))).astype(np.float32)
    idx = np.arange(CHUNK, dtype=np.float32)
    diff = idx[:, None] - idx[None, :]
    intra = np.where(diff[None] >= 0, np.exp(np.maximum(diff, 0.0)[None] * log_g[:, None, None]), 0.0)
    zeta = np.exp((CHUNK - 1 - idx)[None, :] * log_g[:, None])
    xi = np.exp((idx + 1)[None, :] * log_g[:, None])
    decay = np.exp(CHUNK * log_g)
    zeta = np.broadcast_to(zeta[:, :, None], (RET_HEADS, CHUNK, RET_DK))
    xi = np.broadcast_to(xi[:, :, None], (RET_HEADS, CHUNK, RET_DV))
    return (jnp.asarray(intra, F32), jnp.asarray(zeta, F32), jnp.asarray(xi, F32), [float(d) for d in decay])


def _rotary_tables(length):
    inv = ROPE_BASE ** (-jnp.arange(0, RET_DK, 2, dtype=F32) / RET_DK)
    ang = jnp.arange(length).astype(F32)[:, None] * inv[None, :]
    cos, sin = jnp.cos(ang), jnp.sin(ang)
    return jnp.concatenate([cos, cos], axis=1), jnp.concatenate([-sin, sin], axis=1)


def _rot(x, cos2, sin2):
    return x * cos2 + pltpu.roll(x, RET_DK // 2, 1) * sin2


def _rot_t(y, cos2, sin2):
    return y * cos2 + pltpu.roll(y * sin2, RET_DK // 2, 1)


def _ret_chunk(q, k, v, g, state, intra, zeta, xi, decay):
    s = _mm_nt(q, k) * intra
    kv = _mm_tn(k * zeta, v)
    o = _mm(s, v) + _mm(q, state) * xi
    oc = o - jnp.mean(o, axis=-1, keepdims=True)
    r = oc * lax.rsqrt(jnp.mean(oc * oc, axis=-1, keepdims=True) + EPS)
    return _silu(g) * r, state * decay + kv


RET_QK, RET_V = RET_HEADS * RET_DK, RET_HEADS * RET_DV


def _ret_specs(rev, nc):
    def cidx(c):
        return nc - 1 - c if rev else c
    whole = lambda shape: pl.BlockSpec(shape, lambda c: (0,) * len(shape))
    return [
        pl.BlockSpec((CHUNK, RET_QK), lambda c: (cidx(c), 0)),
        pl.BlockSpec((CHUNK, RET_QK), lambda c: (cidx(c), 1)),
        pl.BlockSpec((CHUNK, RET_V), lambda c: (cidx(c), 1)),
        pl.BlockSpec((CHUNK, RET_V), lambda c: (cidx(c), 2)),
        pl.BlockSpec((CHUNK, RET_DK), lambda c: (cidx(c), 0)),
        pl.BlockSpec((CHUNK, RET_DK), lambda c: (cidx(c), 0)),
        whole((RET_HEADS, CHUNK, CHUNK)), whole((RET_HEADS, CHUNK, RET_DK)), whole((RET_HEADS, CHUNK, RET_DV)),
    ], cidx


def _head(ref, h, width):
    return ref[:, h * width:(h + 1) * width].astype(F32)


def retention_fwd(proj, cos2, sin2):
    length = proj.shape[0]
    nc = length // CHUNK
    intra, zeta, xi, decays = _retention_consts()
    specs, _ = _ret_specs(False, nc)
    scale = RET_DK ** -0.5

    def body(q_ref, k_ref, v_ref, g_ref, cos_ref, sin_ref, intra_ref, zeta_ref, xi_ref, y_ref, st_ref, state):
        @pl.when(pl.program_id(0) == 0)
        def _():
            state[...] = jnp.zeros_like(state)

        cos2v, sin2v = cos_ref[...], sin_ref[...]
        for h in range(RET_HEADS):
            q = _rot(_head(q_ref, h, RET_DK), cos2v, sin2v)
            k = _rot(_head(k_ref, h, RET_DK), cos2v, sin2v) * scale
            st_ref[h, 0] = state[h]
            y, new_state = _ret_chunk(q, k, _head(v_ref, h, RET_DV), _head(g_ref, h, RET_DV), state[h],
                                      intra_ref[h], zeta_ref[h], xi_ref[h], decays[h])
            y_ref[:, h * RET_DV:(h + 1) * RET_DV] = y.astype(y_ref.dtype)
            state[h] = new_state

    return pl.pallas_call(
        body, name="retention_fwd", grid=(nc,), in_specs=specs,
        out_specs=[pl.BlockSpec((CHUNK, RET_V), lambda c: (c, 0)),
                   pl.BlockSpec((RET_HEADS, 1, RET_DK, RET_DV), lambda c: (0, c, 0, 0))],
        out_shape=[jax.ShapeDtypeStruct((length, RET_V), BF16),
                   jax.ShapeDtypeStruct((RET_HEADS, nc, RET_DK, RET_DV), F32)],
        scratch_shapes=[pltpu.VMEM((RET_HEADS, RET_DK, RET_DV), F32)],
        compiler_params=_params(("arbitrary",)),
    )(proj, proj, proj, proj, cos2, sin2, intra, zeta, xi)


def retention_bwd(proj, cos2, sin2, states, dmix):
    length = proj.shape[0]
    nc = length // CHUNK
    intra, zeta, xi, decays = _retention_consts()
    specs, cidx = _ret_specs(True, nc)
    scale = RET_DK ** -0.5

    def body(q_ref, k_ref, v_ref, g_ref, cos_ref, sin_ref, intra_ref, zeta_ref, xi_ref, st_ref, dy_ref,
             dq_ref, dk_ref, dv_ref, dg_ref, dstate):
        @pl.when(pl.program_id(0) == 0)
        def _():
            dstate[...] = jnp.zeros_like(dstate)

        cos2v, sin2v = cos_ref[...], sin_ref[...]
        for h in range(RET_HEADS):
            q = _rot(_head(q_ref, h, RET_DK), cos2v, sin2v)
            k = _rot(_head(k_ref, h, RET_DK), cos2v, sin2v) * scale
            intra_v, zeta_v, xi_v, decay = intra_ref[h], zeta_ref[h], xi_ref[h], decays[h]
            _, vjp = jax.vjp(lambda q, k, v, g, s: _ret_chunk(q, k, v, g, s, intra_v, zeta_v, xi_v, decay),
                             q, k, _head(v_ref, h, RET_DV), _head(g_ref, h, RET_DV), st_ref[h, 0])
            dq, dk, dv, dg, ds = vjp((_head(dy_ref, h, RET_DV).astype(F32), dstate[h]))
            dq_ref[:, h * RET_DK:(h + 1) * RET_DK] = _rot_t(dq, cos2v, sin2v).astype(dq_ref.dtype)
            dk_ref[:, h * RET_DK:(h + 1) * RET_DK] = _rot_t(dk * scale, cos2v, sin2v).astype(dk_ref.dtype)
            dv_ref[:, h * RET_DV:(h + 1) * RET_DV] = dv.astype(dv_ref.dtype)
            dg_ref[:, h * RET_DV:(h + 1) * RET_DV] = dg.astype(dg_ref.dtype)
            dstate[h] = ds

    specs = specs + [pl.BlockSpec((RET_HEADS, 1, RET_DK, RET_DV), lambda c: (0, cidx(c), 0, 0)),
                     pl.BlockSpec((CHUNK, RET_V), lambda c: (cidx(c), 0))]
    row = lambda width: pl.BlockSpec((CHUNK, width), lambda c: (cidx(c), 0))
    return pl.pallas_call(
        body, name="retention_bwd", grid=(nc,), in_specs=specs,
        out_specs=[row(RET_QK), row(RET_QK), row(RET_V), row(RET_V)],
        out_shape=[jax.ShapeDtypeStruct((length, RET_QK), BF16), jax.ShapeDtypeStruct((length, RET_QK), BF16),
                   jax.ShapeDtypeStruct((length, RET_V), BF16), jax.ShapeDtypeStruct((length, RET_V), BF16)],
        scratch_shapes=[pltpu.VMEM((RET_HEADS, RET_DK, RET_DV), F32)],
        compiler_params=_params(("arbitrary",)),
    )(proj, proj, proj, proj, cos2, sin2, intra, zeta, xi, states, dmix)


def _ssd_consts():
    tri = np.tril(np.ones((CHUNK, CHUNK), np.float32))
    expand = np.zeros((LANE, SSM_DINNER), np.float32)
    for h in range(SSM_HEADS):
        expand[h, h * SSM_P:(h + 1) * SSM_P] = 1.0
    return jnp.asarray(tri), jnp.asarray(tri.T.copy()), jnp.asarray(expand)


def _ssd_chunk(xs, bm, cm, dtr, z, state, dt_bias, a_log, d_skip, norm_w, tri, tri_t, expand):
    gw = SSM_DINNER // SSM_GROUPS
    dt = jax.nn.softplus(dtr + dt_bias)
    a_neg = -jnp.exp(a_log)
    da = dt * a_neg
    acs = _dot_hi(tri, da)
    acs_t = _dot_hi_tn(da, tri_t)
    dt_x = _times_01(dt, expand)
    a_x = jnp.mean(_dot_hi(jnp.broadcast_to(a_neg, (8, LANE)), expand), axis=0, keepdims=True)
    da_x = dt_x * a_x
    acs_x = _01_times(tri, da_x)
    tot_x = jnp.sum(da_x, axis=0, keepdims=True)
    x_dt = xs * dt_x
    x_dec = x_dt * jnp.exp(tot_x - acs_x)
    e_acs = jnp.exp(acs_x)
    e_tot = jnp.exp(tot_x)
    lane = lax.broadcasted_iota(jnp.int32, (CHUNK, LANE), 1)
    sub = lax.broadcasted_iota(jnp.int32, (CHUNK, LANE), 0)
    causal = sub >= lane
    ys, new_states = [], []
    for g in range(SSM_GROUPS):
        bg = bm[:, g * SSM_N:(g + 1) * SSM_N]
        cg = cm[:, g * SSM_N:(g + 1) * SSM_N]
        sg = state[:, g * gw:(g + 1) * gw]
        cb = _mm_nt(cg, bg)
        y_off = _mm(cg, sg) * e_acs[:, g * gw:(g + 1) * gw]
        new_states.append(sg * e_tot[:, g * gw:(g + 1) * gw] + _mm_tn(bg, x_dec[:, g * gw:(g + 1) * gw]))
        pairs = []
        for p in range(gw // LANE):
            hp = g * (gw // LANE) + p
            xp = x_dt[:, hp * LANE:(hp + 1) * LANE]
            halves = []
            for head in (2 * hp, 2 * hp + 1):
                col = jnp.sum(jnp.where(lane == head, acs, 0.0), axis=1, keepdims=True)
                row = jnp.sum(jnp.where(sub == head, acs_t, 0.0), axis=0, keepdims=True)
                decay = jnp.exp(jnp.where(causal, col - row, -1e30))
                halves.append(_mm(cb * decay, xp))
            pairs.append(jnp.where(lane < SSM_P, halves[0], halves[1]))
        ys.append(jnp.concatenate(pairs, axis=1) + y_off)
    d_x = jnp.mean(_dot_hi(jnp.broadcast_to(d_skip, (8, LANE)), expand), axis=0, keepdims=True)
    y = (jnp.concatenate(ys, axis=1) + d_x * xs) * _silu(z)
    normed = []
    for g in range(SSM_GROUPS):
        yg = y[:, g * gw:(g + 1) * gw]
        normed.append(yg * lax.rsqrt(jnp.mean(yg * yg, axis=-1, keepdims=True) + EPS))
    return jnp.concatenate(normed, axis=1) * norm_w, jnp.concatenate(new_states, axis=1)


XBC = SSM_DINNER + 2 * SSM_GROUPS * SSM_N


def _ssd_specs(rev, nc):
    def cidx(c):
        return nc - 1 - c if rev else c
    row = lambda w, j: pl.BlockSpec((CHUNK, w), lambda c: (cidx(c), j))
    whole = lambda shape: pl.BlockSpec(shape, lambda c: (0,) * len(shape))
    return [row(XBC, 0), row(LANE, 0), row(SSM_DINNER, 3),
            whole((1, LANE)), whole((1, LANE)), whole((1, LANE)), whole((1, SSM_DINNER)),
            whole((CHUNK, CHUNK)), whole((CHUNK, CHUNK)), whole((LANE, SSM_DINNER))], cidx


def ssd_fwd(xbc, dt_raw, proj, dt_bias, a_log, d_skip, norm_w):
    length = proj.shape[0]
    nc = length // CHUNK
    tri, tri_t, expand = _ssd_consts()
    specs, _ = _ssd_specs(False, nc)

    def body(xbc_ref, dt_ref, z_ref, dtb_ref, alog_ref, d_ref, nw_ref, tri_ref, trit_ref, e_ref, y_ref, st_ref, state):
        @pl.when(pl.program_id(0) == 0)
        def _():
            state[...] = jnp.zeros_like(state)

        st_ref[0] = state[...]
        y, new_state = _ssd_chunk(
            xbc_ref[:, 0:SSM_DINNER], xbc_ref[:, SSM_DINNER:SSM_DINNER + 256], xbc_ref[:, SSM_DINNER + 256:XBC],
            dt_ref[...], z_ref[...].astype(F32), state[...], dtb_ref[...], alog_ref[...], d_ref[...], nw_ref[...],
            tri_ref[...], trit_ref[...], e_ref[...])
        y_ref[...] = y.astype(y_ref.dtype)
        state[...] = new_state

    return pl.pallas_call(
        body, name="ssd_fwd", grid=(nc,), in_specs=specs,
        out_specs=[pl.BlockSpec((CHUNK, SSM_DINNER), lambda c: (c, 0)),
                   pl.BlockSpec((1, SSM_N, SSM_DINNER), lambda c: (c, 0, 0))],
        out_shape=[jax.ShapeDtypeStruct((length, SSM_DINNER), BF16),
                   jax.ShapeDtypeStruct((nc, SSM_N, SSM_DINNER), F32)],
        scratch_shapes=[pltpu.VMEM((SSM_N, SSM_DINNER), F32)],
        compiler_params=_params(("arbitrary",)),
    )(xbc, dt_raw, proj, dt_bias, a_log, d_skip, norm_w, tri, tri_t, expand)


def ssd_bwd(xbc, dt_raw, proj, dt_bias, a_log, d_skip, norm_w, states, dmix):
    length = proj.shape[0]
    nc = length // CHUNK
    tri, tri_t, expand = _ssd_consts()
    specs, cidx = _ssd_specs(True, nc)

    def body(xbc_ref, dt_ref, z_ref, dtb_ref, alog_ref, d_ref, nw_ref, tri_ref, trit_ref, e_ref, st_ref, dy_ref,
             dxbc_ref, ddt_ref, dz_ref, ddtb_ref, dalog_ref, dd_ref, dnw_ref, dstate):
        c = pl.program_id(0)

        @pl.when(c == 0)
        def _():
            dstate[...] = jnp.zeros_like(dstate)

        tri_v, trit_v, e_v = tri_ref[...], trit_ref[...], e_ref[...]
        _, vjp = jax.vjp(
            lambda *a: _ssd_chunk(*a, tri_v, trit_v, e_v),
            xbc_ref[:, 0:SSM_DINNER], xbc_ref[:, SSM_DINNER:SSM_DINNER + 256], xbc_ref[:, SSM_DINNER + 256:XBC],
            dt_ref[...], z_ref[...].astype(F32), st_ref[0], dtb_ref[...], alog_ref[...], d_ref[...], nw_ref[...])
        dxs, dbm, dcm, ddt, dz, ds, ddtb, dalog, dd, dnw = vjp((dy_ref[...].astype(F32), dstate[...]))
        dxbc_ref[:, 0:SSM_DINNER] = dxs
        dxbc_ref[:, SSM_DINNER:SSM_DINNER + 256] = dbm
        dxbc_ref[:, SSM_DINNER + 256:XBC] = dcm
        ddt_ref[...] = ddt.astype(ddt_ref.dtype)
        dz_ref[...] = dz.astype(dz_ref.dtype)
        dstate[...] = ds
        for r, d in ((ddtb_ref, ddtb), (dalog_ref, dalog), (dd_ref, dd), (dnw_ref, dnw)):
            @pl.when(c == 0)
            def _(r=r, d=d):
                r[...] = d

            @pl.when(c > 0)
            def _(r=r, d=d):
                r[...] += d

    whole = lambda shape: pl.BlockSpec(shape, lambda c: (0,) * len(shape))
    specs = specs + [pl.BlockSpec((1, SSM_N, SSM_DINNER), lambda c: (cidx(c), 0, 0)),
                     pl.BlockSpec((CHUNK, SSM_DINNER), lambda c: (cidx(c), 1))]
    return pl.pallas_call(
        body, name="ssd_bwd", grid=(nc,), in_specs=specs,
        out_specs=[pl.BlockSpec((CHUNK, XBC), lambda c: (cidx(c), 0)), pl.BlockSpec((CHUNK, LANE), lambda c: (cidx(c), 0)),
                   pl.BlockSpec((CHUNK, SSM_DINNER), lambda c: (cidx(c), 0)),
                   whole((1, LANE)), whole((1, LANE)), whole((1, LANE)), whole((1, SSM_DINNER))],
        out_shape=[jax.ShapeDtypeStruct((length, XBC), F32), jax.ShapeDtypeStruct((length, LANE), BF16),
                   jax.ShapeDtypeStruct((length, SSM_DINNER), BF16),
                   jax.ShapeDtypeStruct((1, LANE), F32), jax.ShapeDtypeStruct((1, LANE), F32),
                   jax.ShapeDtypeStruct((1, LANE), F32), jax.ShapeDtypeStruct((1, SSM_DINNER), F32)],
        scratch_shapes=[pltpu.VMEM((SSM_N, SSM_DINNER), F32)],
        compiler_params=_params(("arbitrary",)),
    )(xbc, dt_raw, proj, dt_bias, a_log, d_skip, norm_w, tri, tri_t, expand, states, dmix)


def _cmul(ar, ai, br, bi):
    return ar * br - ai * bi, ar * bi + ai * br


def s5_scan(b_re, b_im, a_re, a_im, *, reverse=False, states=None, name, lw=256):
    length, lanes = b_re.shape
    nk = length // SCAN_SEG
    with_da = states is not None
    assert reverse or not with_da

    def shift(v):
        sub = lax.broadcasted_iota(jnp.int32, v.shape, 0)
        if reverse:
            return jnp.where(sub == SCAN_SEG - 1, 0.0, pltpu.roll(v, SCAN_SEG - 1, 0))
        return jnp.where(sub == 0, 0.0, pltpu.roll(v, 1, 0))

    def body(*refs):
        if with_da:
            bre_ref, bim_ref, are_ref, aim_ref, sre_ref, sim_ref, xre_ref, xim_ref, dare_ref, daim_ref = refs
        else:
            bre_ref, bim_ref, are_ref, aim_ref, xre_ref, xim_ref = refs
        ar = jnp.broadcast_to(are_ref[...], (SCAN_SEG, lw))
        ai = jnp.broadcast_to(aim_ref[...], (SCAN_SEG, lw))

        def tile(i):
            k = (nk - 1 - i) if reverse else i
            return pl.ds(pl.multiple_of(k * SCAN_SEG, SCAN_SEG), SCAN_SEG)

        def local(i, carry):
            xr, xi, pr, pi = carry
            rows = tile(i)
            mr, mi = _cmul(ar, ai, xr, xi)
            xr, xi = mr + bre_ref[rows, :], mi + bim_ref[rows, :]
            xre_ref[rows, :] = xr
            xim_ref[rows, :] = xi
            pr, pi = _cmul(ar, ai, pr, pi)
            return xr, xi, pr, pi

        zero = jnp.zeros((SCAN_SEG, lw), F32)
        one = jnp.ones((SCAN_SEG, lw), F32)
        er, ei, pr, pi = lax.fori_loop(0, nk, local, (zero, zero, one, zero))
        cr, ci = zero, zero
        for _ in range(SCAN_SEG - 1):
            mr, mi = _cmul(pr, pi, cr, ci)
            cr, ci = shift(er + mr), shift(ei + mi)

        def fix(i, carry):
            pr, pi, dr, di = carry
            rows = tile(i)
            pr, pi = _cmul(ar, ai, pr, pi)
            mr, mi = _cmul(pr, pi, cr, ci)
            xr, xi = xre_ref[rows, :] + mr, xim_ref[rows, :] + mi
            xre_ref[rows, :] = xr
            xim_ref[rows, :] = xi
            if with_da:
                k = nk - 1 - i
                prev = pl.ds(pl.multiple_of(jnp.maximum(k - 1, 0) * SCAN_SEG, SCAN_SEG), SCAN_SEG)
                last = pl.ds((nk - 1) * SCAN_SEG, SCAN_SEG)
                sub = lax.broadcasted_iota(jnp.int32, (SCAN_SEG, lw), 0)
                wr = jnp.where(sub == 0, 0.0, pltpu.roll(sre_ref[last, :], 1, 0))
                wi = jnp.where(sub == 0, 0.0, pltpu.roll(sim_ref[last, :], 1, 0))
                sr = jnp.where(k == 0, wr, sre_ref[prev, :])
                si = jnp.where(k == 0, wi, sim_ref[prev, :])
                dr, di = dr + xr * sr + xi * si, di + xi * sr - xr * si
            return pr, pi, dr, di

        _, _, dr, di = lax.fori_loop(0, nk, fix, (one, zero, zero, zero))
        if with_da:
            dare_ref[...] = jnp.sum(dr, axis=0, keepdims=True)
            daim_ref[...] = jnp.sum(di, axis=0, keepdims=True)

    col = pl.BlockSpec((length, lw), lambda j: (0, j))
    vec = pl.BlockSpec((1, lw), lambda j: (0, j))
    ins = [b_re, b_im, a_re, a_im] + (list(states) if with_da else [])
    in_specs = [col, col, vec, vec] + ([col, col] if with_da else [])
    out_specs = [col, col] + ([vec, vec] if with_da else [])
    out_shape = [jax.ShapeDtypeStruct((length, lanes), F32)] * 2 + ([jax.ShapeDtypeStruct((1, lanes), F32)] * 2 if with_da else [])
    return pl.pallas_call(
        body, name=name, grid=(lanes // lw,), in_specs=in_specs, out_specs=out_specs, out_shape=out_shape,
        compiler_params=_params(("parallel",)),
    )(*ins)


def _seg_interleave(v):
    length = v.shape[0]
    return v.reshape(SCAN_SEG, length // SCAN_SEG, -1).transpose(1, 0, 2).reshape(length, -1)


def _seg_deinterleave(v):
    length = v.shape[0]
    return v.reshape(length // SCAN_SEG, SCAN_SEG, -1).transpose(1, 0, 2).reshape(length, -1)


def _block_diag(m):
    eye = jnp.eye(S5_GROUPS, dtype=m.dtype)
    return (m.reshape(S5_GROUPS, S5_GROUP, 1, S5_STATE) * eye[:, None, :, None]).reshape(S5_GROUPS * S5_GROUP, S5_LANES)


def _block_diag_take(full, *, name):
    pairs = S5_GROUPS // 2
    rows, lanes = 2 * S5_GROUP, 2 * S5_STATE

    def body(f_ref, o_ref):
        o_ref[...] = f_ref[...]

    tiles = pl.pallas_call(
        body, name=name, grid=(pairs,), in_specs=[pl.BlockSpec((rows, lanes), lambda p: (p, p))],
        out_specs=pl.BlockSpec((rows, lanes), lambda p: (p, 0)),
        out_shape=jax.ShapeDtypeStruct((pairs * rows, lanes), full.dtype), compiler_params=_params(("parallel",)),
    )(full)
    tiles = tiles.reshape(pairs, 2, S5_GROUP, 2, S5_STATE)
    blocks = jnp.stack([tiles[:, 0, :, 0, :], tiles[:, 1, :, 1, :]], axis=1)
    return blocks.reshape(S5_GROUPS * S5_GROUP, S5_STATE)


def _s5_prep(a_re, a_im, log_step, b_re, b_im, rep):
    step = jnp.exp(log_step)
    mag = jnp.exp(a_re * step)
    ab_re = mag * jnp.cos(a_im * step)
    ab_im = mag * jnp.sin(a_im * step)
    den = a_re * a_re + a_im * a_im
    f_re = ((ab_re - 1.0) * a_re + ab_im * a_im) / den
    f_im = (ab_im * a_re - (ab_re - 1.0) * a_im) / den
    fr, fi = _dot_hi(rep, f_re), _dot_hi(rep, f_im)
    return ab_re, ab_im, fr * b_re - fi * b_im, fr * b_im + fi * b_re


def _rms(x, g):
    return (x * lax.rsqrt(jnp.mean(x * x, axis=-1, keepdims=True) + EPS) * g,)


def _glu(a, g):
    return (a * jax.nn.sigmoid(g),)


def _ln_silu(x, g, b):
    xc = x - jnp.mean(x, axis=-1, keepdims=True)
    var = jnp.mean(xc * xc, axis=-1, keepdims=True)
    return (_silu(xc * lax.rsqrt(var + EPS) * g + b),)


def _s5_post(y, u, d_skip, glu_w):
    s = jax.nn.gelu(y + d_skip * u)
    return (s * jax.nn.sigmoid(_mm(s, glu_w)),)


def loss_head(x, tgt, g, *, tl=512):
    length, d = x.shape
    tl = min(tl, length)

    def body(x_ref, t_ref, g_ref, loss_ref, dx_ref, dg_ref):
        i = pl.program_id(0)
        y, vjp = jax.vjp(lambda x, g: _rms(x, g)[0], x_ref[...], g_ref[...])
        err = y - t_ref[...]
        dx, dg = vjp(err * (1.0 / d))
        dx_ref[...] = dx
        part = jnp.broadcast_to(0.5 * jnp.sum(jnp.mean(err * err, axis=-1, keepdims=True), axis=0, keepdims=True), (1, LANE))

        @pl.when(i == 0)
        def _():
            loss_ref[...] = part
            dg_ref[...] = dg

        @pl.when(i > 0)
        def _():
            loss_ref[...] += part
            dg_ref[...] += dg

    row = pl.BlockSpec((tl, d), lambda i: (i, 0))
    return pl.pallas_call(
        body, name="loss_head", grid=(length // tl,),
        in_specs=[row, row, pl.BlockSpec((1, d), lambda i: (0, 0))],
        out_specs=[pl.BlockSpec((1, LANE), lambda i: (0, 0)), row, pl.BlockSpec((1, d), lambda i: (0, 0))],
        out_shape=[jax.ShapeDtypeStruct((1, LANE), F32), jax.ShapeDtypeStruct((length, d), F32),
                   jax.ShapeDtypeStruct((1, d), F32)],
        compiler_params=_params(("arbitrary",)),
    )(x, tgt, g)


def _pad_heads(v):
    return jnp.pad(v, ((0, 0), (0, LANE - v.shape[1])))


def local_step(x, tgt, w, late_weights=None, early_reduce=None):
    length = x.shape[0]
    cos2, sin2 = _rotary_tables(length)
    grads = {}
    w = dict(w)

    def rms_fwd(xin, g, name):
        return rowwise_fwd(_rms, [xin], [], [g], [], [(D_MODEL, BF16)], name=name, tl=512)[0]

    def rms_bwd(xin, g, dh, dxo, name):
        return rowwise_bwd(_rms, [xin], [], [g], [], [dh], [F32], name=name, tl=512, add=dxo)

    def ffn_fwd(i, xin):
        hf = rms_fwd(xin, w["ffn_norm"][i:i + 1], f"ffn{i}_norm")
        w_up = w["ffn_w_up_pairs"][i] if "ffn_w_up_pairs" in w else ffn_interleave(w["ffn_w_up"][i], name=f"ffn{i}_up_pairs")
        a = matmul(hf, w_up, out_dtype=BF16, name=f"ffn{i}_up")
        act = ffn_conv_act(a, ffn_interleave(w["ffn_dw_w"][i]), ffn_interleave(w["ffn_dw_b"][i:i + 1]),
                           name=f"ffn{i}_conv_act")
        return matmul(act, w["ffn_w_down"][i], res=xin, name=f"ffn{i}_down"), (hf, a, act, w_up)

    def ffn_bwd(i, xin, saved, dxo):
        hf, a, act, w_up = saved
        dact = matmul(dxo, w["ffn_w_down"][i], tb=True, out_dtype=BF16, name=f"ffn{i}_down_dx")
        dw_down = matmul(act, dxo, ta=True, name=f"ffn{i}_down_dw")
        da, ddw_w, ddw_b = ffn_conv_act_bwd(a, ffn_interleave(w["ffn_dw_w"][i]), ffn_interleave(w["ffn_dw_b"][i:i + 1]),
                                            dact, name=f"ffn{i}_conv_act_bwd")
        dw_up = ffn_pairs_to_shards(matmul(hf, da, ta=True, name=f"ffn{i}_up_dw"), name=f"ffn{i}_up_dw_shards")
        dhf = matmul(da, w_up, tb=True, name=f"ffn{i}_up_dx")
        dxin, dnorm = rms_bwd(xin, w["ffn_norm"][i:i + 1], dhf, dxo, f"ffn{i}_norm_bwd")
        return dxin, dict(ffn_norm=dnorm, ffn_w_up=dw_up, ffn_dw_w=ffn_deinterleave(ddw_w),
                          ffn_dw_b=ffn_deinterleave(ddw_b), ffn_w_down=dw_down)

    w_in_e = jnp.pad(w["e_w_in"][0], ((0, 0), (0, EVEN_IN_PAD - EVEN_IN)))
    conv_w_e, conv_b_e = w["e_conv_w"][0], w["e_conv_b"]
    dt_bias, a_log, d_skip = _pad_heads(w["e_dt_bias"]), _pad_heads(w["e_a_log"]), _pad_heads(w["e_d"])
    xbc_off = 4 * D_MODEL

    hn0 = rms_fwd(x, w["mix_norm"][0:1], "mix0_norm")
    proj0 = matmul(hn0, w_in_e, out_dtype=BF16, name="even_in")
    dt_raw = matmul(hn0, w_in_e[:, EVEN_IN_PAD - LANE:], name="even_in_dt")
    y_ret, ret_states = retention_fwd(proj0, cos2, sin2)
    xbc = conv_fwd(proj0, conv_w_e, conv_b_e, act=True, off=xbc_off, name="ssd_conv")
    y_ssm, ssd_states = ssd_fwd(xbc, dt_raw, proj0, dt_bias, a_log, d_skip, w["e_ssm_norm"])
    mix0 = jnp.concatenate([y_ret, y_ssm], axis=1)
    if late_weights is not None:
        w.update(late_weights(y_ssm))
    w_out_e = w["e_w_out"][0]
    x1 = matmul(mix0, w_out_e, res=x, name="even_out")
    x2, ffn0_saved = ffn_fwd(0, x1)

    w_in_o, w_out_o, glu_w = w["o_w_in"][0], w["o_w_out"][0], w["o_glu_w"][0]
    dw_w_o, dw_b_o, ln_g, ln_b, d_o = w["o_dw_w"][0], w["o_dw_b"], w["o_ln_g"], w["o_ln_b"], w["o_d"]
    rep = jnp.asarray(np.repeat(np.eye(S5_GROUPS, dtype=np.float32), S5_GROUP, axis=0))
    rows_gc = (S5_GROUPS * S5_GROUP, S5_STATE)
    prep_in = [w["o_a_re"][0], w["o_a_im"][0], w["o_log_step"].reshape(S5_GROUPS, 1),
               w["o_b_re"][0].transpose(0, 2, 1).reshape(rows_gc), w["o_b_im"][0].transpose(0, 2, 1).reshape(rows_gc), rep]
    ab_re, ab_im, bb_re, bb_im = whole_fwd(
        _s5_prep, prep_in, [(S5_GROUPS, S5_STATE)] * 2 + [rows_gc] * 2, name="s5_prep")
    a_re_row, a_im_row = ab_re.reshape(1, S5_LANES), ab_im.reshape(1, S5_LANES)
    b_re_bd, b_im_bd = _block_diag(bb_re).astype(BF16), _block_diag(bb_im).astype(BF16)
    c_re_bd = _block_diag(w["o_c_re"][0].reshape(rows_gc)).astype(BF16)
    c_im_neg_bd = _block_diag(-w["o_c_im"][0].reshape(rows_gc)).astype(BF16)

    hn1 = rms_fwd(x2, w["mix_norm"][1:2], "mix1_norm")
    proj1 = matmul(hn1, w_in_o, name="odd_in")
    half = D_MODEL // 2
    c_glu = rowwise_fwd(_glu, [Cols(proj1, half, 0), Cols(proj1, half, 1)], [], [], [], [(half, F32)],
                        name="conf_glu", tl=512)[0]
    c_conv = conv_fwd(c_glu, dw_w_o, dw_b_o, act=False, name="conf_conv")
    c_out = rowwise_fwd(_ln_silu, [c_conv], [], [ln_g, ln_b], [], [(half, BF16)], name="conf_ln", tl=512)[0]
    u_seg = _seg_interleave(proj1[:, 2 * half:])
    bu_re = matmul(u_seg, b_re_bd, name="s5_bu_re")
    bu_im = matmul(u_seg, b_im_bd, name="s5_bu_im")
    xs_re, xs_im = s5_scan(bu_re, bu_im, a_re_row, a_im_row, name="s5_scan")
    y_im = matmul(xs_im, c_im_neg_bd, tb=True, name="s5_y_im")
    y_s5 = _seg_deinterleave(matmul(xs_re, c_re_bd, tb=True, res=y_im, name="s5_y_re"))
    s_out = rowwise_fwd(_s5_post, [y_s5, Cols(proj1, half, 2)], [], [d_o, glu_w], [], [(half, BF16)],
                        name="s5_post", tl=512)[0]
    mix1 = jnp.concatenate([c_out, s_out], axis=1)
    x3 = matmul(mix1, w_out_o, res=x2, name="odd_out")
    x4, ffn1_saved = ffn_fwd(1, x3)

    loss, dx4, dfinal = loss_head(x4, tgt, w["final_norm"].reshape(1, D_MODEL))
    grads["final_norm"] = dfinal.reshape(D_MODEL)

    dx3, g_ffn1 = ffn_bwd(1, x3, ffn1_saved, dx4)
    dmix1 = matmul(dx3, w_out_o, tb=True, out_dtype=BF16, name="odd_out_dx")
    grads["o_w_out"] = [matmul(mix1, dx3, ta=True, name="odd_out_dw")]
    dc_conv, dln_g, dln_b = rowwise_bwd(_ln_silu, [c_conv], [], [ln_g, ln_b], [], [Cols(dmix1, half, 0)], [F32],
                                        name="conf_ln_bwd", tl=512)
    dc_glu, ddw_w_o, ddw_b_o = conv_bwd(c_glu, dw_w_o, dw_b_o, dc_conv, act=False, name="conf_conv_bwd")
    d_cacg = rowwise_bwd(_glu, [Cols(proj1, half, 0), Cols(proj1, half, 1)], [], [], [], [dc_glu], [BF16],
                         name="conf_glu_bwd", tl=512, merge=True)[0]
    dy_s5, du_post, dd_o, dglu_w = rowwise_bwd(
        _s5_post, [y_s5, Cols(proj1, half, 2)], [], [d_o, glu_w], [], [Cols(dmix1, half, 1)], [F32, F32],
        name="s5_post_bwd", tl=512)
    dy_seg = _seg_interleave(dy_s5)
    dxs_re = matmul(dy_seg, c_re_bd, name="s5_dx_re")
    dxs_im = matmul(dy_seg, c_im_neg_bd, name="s5_dx_im")
    dc_re_bd = matmul(dy_seg, xs_re, ta=True, name="s5_dc_re")
    dc_im_neg_bd = matmul(dy_seg, xs_im, ta=True, name="s5_dc_im")
    g_re, g_im, dab_re, dab_im = s5_scan(dxs_re, dxs_im, a_re_row, -a_im_row, reverse=True, states=(xs_re, xs_im),
                                         name="s5_scan_bwd", lw=LANE)
    dbb_re = _block_diag_take(matmul(u_seg, g_re, ta=True, name="s5_db_re"), name="s5_db_re_diag")
    dbb_im = _block_diag_take(matmul(u_seg, g_im, ta=True, name="s5_db_im"), name="s5_db_im_diag")
    du_im = matmul(g_im, b_im_bd, tb=True, name="s5_du_im")
    du = _seg_deinterleave(matmul(g_re, b_re_bd, tb=True, res=du_im, name="s5_du_re")) + du_post
    da_re, da_im, dlog_step, db_re, db_im = whole_bwd(
        _s5_prep, prep_in, 5,
        [dab_re.reshape(S5_GROUPS, S5_STATE), dab_im.reshape(S5_GROUPS, S5_STATE), dbb_re, dbb_im], name="s5_prep_bwd")
    gcn = (S5_GROUPS, S5_GROUP, S5_STATE)
    grads.update(
        o_a_re=da_re[None], o_a_im=da_im[None], o_log_step=dlog_step.reshape(1, S5_GROUPS),
        o_b_re=db_re.reshape(gcn).transpose(0, 2, 1)[None], o_b_im=db_im.reshape(gcn).transpose(0, 2, 1)[None],
        o_c_re=_block_diag_take(dc_re_bd, name="s5_dc_re_diag").reshape(gcn)[None],
        o_c_im=-_block_diag_take(dc_im_neg_bd, name="s5_dc_im_diag").reshape(gcn)[None],
        o_d=dd_o, o_glu_w=[dglu_w], o_dw_w=ddw_w_o[None], o_dw_b=ddw_b_o, o_ln_g=dln_g, o_ln_b=dln_b)
    dproj1 = jnp.concatenate([d_cacg, du.astype(BF16)], axis=1)
    grads["o_w_in"] = [matmul(hn1, dproj1, ta=True, name="odd_in_dw")]
    dhn1 = matmul(dproj1, w_in_o, tb=True, name="odd_in_dx")
    dx2, dmix_norm1 = rms_bwd(x2, w["mix_norm"][1:2], dhn1, dx3, "mix1_norm_bwd")

    if early_reduce is not None:
        zero = early_reduce({("o_w_in", 0): grads["o_w_in"][0], ("o_glu_w", 0): grads["o_glu_w"][0],
                             ("o_w_out", 0): grads["o_w_out"][0], ("ffn_w_up", 1): g_ffn1["ffn_w_up"],
                             ("ffn_w_down", 1): g_ffn1["ffn_w_down"]})
        w["ffn_dw_b"] = w["ffn_dw_b"] + zero
    dx1, g_ffn0 = ffn_bwd(0, x1, ffn0_saved, dx2)
    if early_reduce is not None:
        dt_bias = dt_bias + early_reduce({("ffn_w_up", 0): g_ffn0["ffn_w_up"], ("ffn_w_down", 0): g_ffn0["ffn_w_down"]})
    for k in g_ffn0:
        per_layer = [g_ffn0[k], g_ffn1[k]]
        grads[k] = per_layer if k in ("ffn_w_up", "ffn_w_down") else jnp.stack(per_layer).reshape(w[k].shape)
    dmix0 = matmul(dx1, w_out_e, tb=True, out_dtype=BF16, name="even_out_dx")
    grads["e_w_out"] = [matmul(mix0, dx1, ta=True, name="even_out_dw")]
    if early_reduce is not None:
        a_log = a_log + early_reduce({("e_w_out", 0): grads["e_w_out"][0]})
    dq, dk, dv, dg = retention_bwd(proj0, cos2, sin2, ret_states, dmix0)
    dxbc_c, ddt, dz, ddt_bias, da_log, dd_skip, dssm_norm = ssd_bwd(
        xbc, dt_raw, proj0, dt_bias, a_log, d_skip, w["e_ssm_norm"], ssd_states, dmix0)
    dxbc, dconv_w, dconv_b = conv_bwd(proj0, conv_w_e, conv_b_e, dxbc_c, act=True, off=xbc_off,
                                      name="ssd_conv_bwd", dx_dtype=BF16)
    dproj0 = jnp.concatenate([dq, dk, dv, dg, dz, dxbc, ddt], axis=1)
    grads["e_w_in"] = [matmul(hn0, dproj0, ta=True, name="even_in_dw")[:, :EVEN_IN]]
    norm_w0 = w["mix_norm"][0:1]
    if early_reduce is not None:
        norm_w0 = norm_w0 + early_reduce({("e_w_in", 0): grads["e_w_in"][0]})
    dhn0 = matmul(dproj0, w_in_e, tb=True, name="even_in_dx")
    dx, dmix_norm0 = rms_bwd(x, norm_w0, dhn0, dx1, "mix0_norm_bwd")
    grads.update(
        mix_norm=jnp.concatenate([dmix_norm0, dmix_norm1], axis=0), e_conv_w=dconv_w[None], e_conv_b=dconv_b,
        e_dt_bias=ddt_bias[:, :SSM_HEADS], e_a_log=da_log[:, :SSM_HEADS], e_d=dd_skip[:, :SSM_HEADS],
        e_ssm_norm=dssm_norm)
    return loss, dx, grads


def adamw(w, g, m, v, *, name):
    shape = w.shape
    cols = shape[-1]
    rows = w.size // cols
    tr = _tile(rows, max(8, (512 * 1024 // cols) // 8 * 8), unit=8)

    def body(w_ref, g_ref, m_ref, v_ref, d_ref, nm_ref, nv_ref):
        gv = g_ref[...]
        nm = ADAM_B1 * m_ref[...] + (1.0 - ADAM_B1) * gv
        nv = ADAM_B2 * v_ref[...] + (1.0 - ADAM_B2) * jnp.square(gv)
        m_hat = nm / (1.0 - ADAM_B1 ** ADAM_STEP)
        v_hat = nv / (1.0 - ADAM_B2 ** ADAM_STEP)
        d_ref[...] = -ADAM_LR * (m_hat / (jnp.sqrt(v_hat) + ADAM_EPS) + ADAM_WD * w_ref[...])
        nm_ref[...] = nm
        nv_ref[...] = nv

    spec = pl.BlockSpec((tr, cols), lambda i: (i, 0))
    outs = pl.pallas_call(
        body, name=name, grid=(rows // tr,), in_specs=[spec] * 4, out_specs=[spec] * 3,
        out_shape=[jax.ShapeDtypeStruct((rows, cols), F32)] * 3, compiler_params=_params(("parallel",)),
    )(*[t.reshape(rows, cols) for t in (w, g, m, v)])
    return [o.reshape(shape) for o in outs]


OTHER_CHIPS = ((1, 0), (0, 1), (1, 1))
ANY = pl.BlockSpec(memory_space=pl.ANY)


def _position():
    return lax.axis_index("x"), lax.axis_index("y"), lax.axis_index("c")


def _flip(v, f):
    return 1 - v if f else v


def _remote(src, dst, send_sem, recv_sem, device):
    return pltpu.make_async_remote_copy(src_ref=src, dst_ref=dst, send_sem=send_sem, recv_sem=recv_sem,
                                        device_id=device, device_id_type=MESH)


def gather_shards(big, small):
    n_big, n_small = len(big), len(small)
    halves = [a.shape[0] // 2 for a in big]

    def body(*refs):
        big_refs, small_refs = refs[:n_big], refs[n_big:n_big + n_small]
        obig_refs = refs[n_big + n_small:2 * n_big + n_small]
        osmall_refs = refs[2 * n_big + n_small:2 * (n_big + n_small)]
        ici_send, ici_recv, d2d_send, d2d_recv, small_send, small_recv = refs[2 * (n_big + n_small):]
        x, y, c = _position()
        mine = 2 * x + y

        def half(k, core):
            return pl.ds(pl.multiple_of(core * halves[k], 16), halves[k])

        sends = []
        for j, (fx, fy) in enumerate(OTHER_CHIPS):
            peer = (_flip(x, fx), _flip(y, fy), c)
            for k in range(n_big):
                sends.append(_remote(big_refs[k].at[half(k, c)], obig_refs[k].at[mine, half(k, c)],
                                     ici_send.at[j, k], ici_recv.at[j, k], peer))
            for k in range(n_small):
                sends.append(_remote(small_refs[k], osmall_refs[k].at[mine], small_send.at[j, k], small_recv.at[j, k], peer))
        for cp in sends:
            cp.start()
        for j, (fx, fy) in enumerate(OTHER_CHIPS):
            px, py = _flip(x, fx), _flip(y, fy)
            src_chip = 2 * px + py
            for k in range(n_big):
                landed = obig_refs[k].at[src_chip, half(k, c)]
                _remote(landed, landed, ici_send.at[j, k], ici_recv.at[j, k], (px, py, c)).wait_recv()
                fwd = _remote(landed, landed, d2d_send.at[j, k], d2d_recv.at[j, k], (x, y, 1 - c))
                fwd.start()
                sends.append(fwd)
        for j, (fx, fy) in enumerate(OTHER_CHIPS):
            px, py = _flip(x, fx), _flip(y, fy)
            src_chip = 2 * px + py
            for k in range(n_big):
                other = obig_refs[k].at[src_chip, half(k, 1 - c)]
                _remote(other, other, d2d_send.at[j, k], d2d_recv.at[j, k], (x, y, 1 - c)).wait_recv()
            for k in range(n_small):
                dst = osmall_refs[k].at[src_chip]
                _remote(small_refs[k], dst, small_send.at[j, k], small_recv.at[j, k], (px, py, c)).wait_recv()
        for cp in sends:
            cp.wait_send()

    arrays = list(big) + list(small)
    dma = pltpu.SemaphoreType.DMA
    return pl.pallas_call(
        body, name="gather_shards", in_specs=[ANY] * len(arrays), out_specs=[ANY] * len(arrays),
        out_shape=[jax.ShapeDtypeStruct((4,) + a.shape, a.dtype) for a in arrays],
        scratch_shapes=[dma((3, n_big)), dma((3, n_big)), dma((3, n_big)), dma((3, n_big)),
                        dma((3, n_small)), dma((3, n_small))],
        compiler_params=_params(),
    )(*arrays)


def allreduce_small(pack):
    rows = pack.shape[0]
    half = rows // 2

    def body(p_ref, o_ref, sibling_pack, chip_sum, chip_halves, total, sems):
        x, y, c = _position()
        sibling = (x, y, 1 - c)
        swap = _remote(p_ref, sibling_pack, sems.at[0, 0], sems.at[1, 0], sibling)
        swap.start()
        swap.wait()
        chip_sum[...] = p_ref[...] + sibling_pack[...]
        mine = pl.ds(pl.multiple_of(c * half, 8), half)
        other = pl.ds(pl.multiple_of((1 - c) * half, 8), half)
        chip = 2 * x + y
        chip_halves[chip] = chip_sum[mine, :]
        sends = []
        for j, (fx, fy) in enumerate(OTHER_CHIPS):
            sends.append(_remote(chip_sum.at[mine], chip_halves.at[chip], sems.at[0, 1 + j], sems.at[1, 1 + j],
                                 (_flip(x, fx), _flip(y, fy), c)))
        for cp in sends:
            cp.start()
        for j, (fx, fy) in enumerate(OTHER_CHIPS):
            px, py = _flip(x, fx), _flip(y, fy)
            _remote(chip_sum.at[mine], chip_halves.at[2 * px + py], sems.at[0, 1 + j], sems.at[1, 1 + j], (px, py, c)).wait_recv()
        for cp in sends:
            cp.wait_send()
        total[...] = ((chip_halves[0] + chip_halves[1]) + chip_halves[2]) + chip_halves[3]
        o_ref[mine, :] = total[...]
        share = _remote(total, o_ref.at[mine], sems.at[0, 4], sems.at[1, 4], sibling)
        share.start()
        _remote(total, o_ref.at[other], sems.at[0, 4], sems.at[1, 4], sibling).wait_recv()
        share.wait_send()

    vmem = pl.BlockSpec(memory_space=pltpu.VMEM)
    return pl.pallas_call(
        body, name="allreduce_small", in_specs=[vmem], out_specs=vmem,
        out_shape=jax.ShapeDtypeStruct(pack.shape, F32),
        scratch_shapes=[pltpu.VMEM((rows, LANE), F32), pltpu.VMEM((rows, LANE), F32), pltpu.VMEM((4, half, LANE), F32),
                        pltpu.VMEM((half, LANE), F32), pltpu.SemaphoreType.DMA((2, 5))],
        compiler_params=_params(),
    )(pack)


def exchange_halves(gs, *, name):
    n = len(gs)

    def body(*refs):
        g_refs, o_refs, (send_sems, recv_sems) = refs[:n], refs[n:2 * n], refs[2 * n:]
        x, y, c = _position()
        copies = [_remote(g_refs[k].at[:, 1 - c], o_refs[k], send_sems.at[k], recv_sems.at[k], (x, y, 1 - c)) for k in range(n)]
        for cp in copies:
            cp.start()
        for cp in copies:
            cp.wait()

    return pl.pallas_call(
        body, name=name, in_specs=[ANY] * n, out_specs=[ANY] * n,
        out_shape=[jax.ShapeDtypeStruct((4,) + g.shape[2:], g.dtype) for g in gs],
        scratch_shapes=[pltpu.SemaphoreType.DMA((n,)), pltpu.SemaphoreType.DMA((n,))],
        compiler_params=_params(),
    )(*gs)


def swap_halves(rs):
    n = len(rs)

    def body(*refs):
        r_refs, o_refs, (send_sems, recv_sems) = refs[:n], refs[n:2 * n], refs[2 * n:]
        x, y, c = _position()
        copies = [_remote(r_refs[k], o_refs[k], send_sems.at[k], recv_sems.at[k], (x, y, 1 - c)) for k in range(n)]
        for cp in copies:
            cp.start()
        for cp in copies:
            cp.wait()

    dma = pltpu.SemaphoreType.DMA
    return pl.pallas_call(
        body, name="swap_halves", in_specs=[ANY] * n, out_specs=[ANY] * n,
        out_shape=[jax.ShapeDtypeStruct(r.shape, r.dtype) for r in rs],
        scratch_shapes=[dma((n,)), dma((n,))],
        compiler_params=_params(),
    )(*rs)


HBM = pl.BlockSpec(memory_space=pltpu.HBM)
SEM = pl.BlockSpec(memory_space=pltpu.SEMAPHORE)
SIDE_EFFECT = pltpu.SideEffectType.DATAFLOW_SIDE_EFFECTING


def _gather_plan(halves):
    def plan(v_refs, land_refs, x, y, c):
        copies = []
        for fx, fy in OTHER_CHIPS:
            for k in range(len(v_refs)):
                rows = pl.ds(pl.multiple_of(c * halves[k], 16), halves[k])
                copies.append((v_refs[k].at[rows], land_refs[k].at[2 * x + y, rows], (_flip(x, fx), _flip(y, fy), c)))
        return copies
    return plan


def _scatter_plan(v_refs, land_refs, x, y, c):
    copies = []
    for j, (fx, fy) in enumerate(OTHER_CHIPS):
        px, py = _flip(x, fx), _flip(y, fy)
        for k in range(len(v_refs)):
            copies.append((v_refs[k].at[2 * px + py], land_refs[k].at[j], (px, py, c)))
    return copies


def chip_exchange_start(srcs, land_shapes, plan, after, *, name):
    n = len(srcs)
    n_cp = 3 * n

    def body(*refs):
        v_refs, land_refs = refs[:n], refs[n:2 * n]
        outs = refs[2 * n + 1:]
        sends, recvs, token = outs[:n_cp], outs[n_cp:2 * n_cp], outs[-1]
        x, y, c = _position()
        for (src, dst, device), send, recv in zip(plan(v_refs, land_refs, x, y, c), sends, recvs, strict=True):
            _remote(src, dst, send, recv, device).start()
        token[...] = jnp.zeros_like(token)

    lands = [lax.empty(shape, v.dtype) for shape, v in zip(land_shapes, srcs)]
    arrays = [pltpu.with_memory_space_constraint(a, pltpu.HBM) for a in list(srcs) + lands]
    outs = pl.pallas_call(
        body, name=name,
        out_shape=tuple(pltpu.SemaphoreType.DMA(()) for _ in range(2 * n_cp))
        + tuple(pltpu.HBM(a.shape, a.dtype) for a in arrays) + (jax.ShapeDtypeStruct((8, LANE), F32),),
        in_specs=[HBM] * (2 * n) + [ANY],
        out_specs=(SEM,) * (2 * n_cp) + (HBM,) * (2 * n) + (pl.BlockSpec(memory_space=pltpu.VMEM),),
        input_output_aliases={i: 2 * n_cp + i for i in range(2 * n)},
        compiler_params=pltpu.CompilerParams(has_side_effects=SIDE_EFFECT),
    )(*arrays, after)
    handle = (outs[:n_cp], outs[n_cp:2 * n_cp], outs[2 * n_cp:2 * n_cp + n], outs[2 * n_cp + n:2 * n_cp + 2 * n])
    return handle, outs[-1]


def chip_exchange_wait(handle, plan, after, *, name):
    sends, recvs, v_thru, land_thru = handle
    n = len(v_thru)
    n_cp = 3 * n

    def body(*refs):
        v_refs, land_refs = refs[:n], refs[n:2 * n]
        sends, recvs = refs[2 * n:2 * n + n_cp], refs[2 * n + n_cp:2 * n + 2 * n_cp]
        x, y, c = _position()
        for (src, dst, device), send, recv in zip(plan(v_refs, land_refs, x, y, c), sends, recvs, strict=True):
            copy = _remote(src, dst, send, recv, device)
            copy.wait_send()
            copy.wait_recv()

    outs = pl.pallas_call(
        body, name=name,
        out_shape=tuple(pltpu.HBM(a.shape, a.dtype) for a in list(v_thru) + list(land_thru)),
        in_specs=[HBM] * (2 * n) + [SEM] * (2 * n_cp) + [ANY], out_specs=(HBM,) * (2 * n),
        input_output_aliases={i: i for i in range(2 * n)},
        compiler_params=pltpu.CompilerParams(has_side_effects=SIDE_EFFECT),
    )(*v_thru, *land_thru, *sends, *recvs, after)
    return outs[:n], outs[n:]


def finish_gather(lands):
    n = len(lands)
    halves = [a.shape[1] // 2 for a in lands]

    def body(*refs):
        o_refs, (send_sems, recv_sems) = refs[n:2 * n], refs[2 * n:]
        x, y, c = _position()

        def half(k, core):
            return pl.ds(pl.multiple_of(core * halves[k], 16), halves[k])

        sends = []
        for j, (fx, fy) in enumerate(OTHER_CHIPS):
            src_chip = 2 * _flip(x, fx) + _flip(y, fy)
            for k in range(n):
                held = o_refs[k].at[src_chip, half(k, c)]
                sends.append(_remote(held, held, send_sems.at[j, k], recv_sems.at[j, k], (x, y, 1 - c)))
        for cp in sends:
            cp.start()
        for j, (fx, fy) in enumerate(OTHER_CHIPS):
            src_chip = 2 * _flip(x, fx) + _flip(y, fy)
            for k in range(n):
                other = o_refs[k].at[src_chip, half(k, 1 - c)]
                _remote(other, other, send_sems.at[j, k], recv_sems.at[j, k], (x, y, 1 - c)).wait_recv()
        for cp in sends:
            cp.wait_send()

    dma = pltpu.SemaphoreType.DMA
    return pl.pallas_call(
        body, name="finish_gather", in_specs=[ANY] * n, out_specs=[ANY] * n,
        out_shape=[jax.ShapeDtypeStruct(a.shape, a.dtype) for a in lands],
        input_output_aliases={k: k for k in range(n)},
        scratch_shapes=[dma((3, n)), dma((3, n))],
        compiler_params=_params(),
    )(*lands)


def add_own_half(g, r, c_idx, *, name):
    _, _, h, cols = g.shape

    def body(c_ref, g_ref, r_ref, o_ref):
        o_ref[...] = (g_ref[0] + r_ref[...]).astype(o_ref.dtype)

    return pl.pallas_call(
        body, name=name,
        grid_spec=pltpu.PrefetchScalarGridSpec(
            num_scalar_prefetch=1, grid=(4,),
            in_specs=[pl.BlockSpec((1, 1, h, cols), lambda s, c: (s, c[0], 0, 0)),
                      pl.BlockSpec((1, h, cols), lambda s, c: (s, 0, 0))],
            out_specs=pl.BlockSpec((1, h, cols), lambda s, c: (s, 0, 0))),
        out_shape=jax.ShapeDtypeStruct(r.shape, BF16), compiler_params=_params(("parallel",)),
    )(c_idx, g, r)


def add_chip_parts(a, parts, chip_idx, *, name):
    _, h, cols = a.shape
    th = h // 2

    def body(s_ref, a_ref, p0_ref, p1_ref, p2_ref, o_ref):
        f = lambda r: r[0].astype(F32)
        o_ref[...] = ((f(a_ref) + f(p0_ref)) + f(p1_ref)) + f(p2_ref)

    part = lambda j: pl.BlockSpec((1, th, cols), lambda i, s, j=j: (j, i, 0))
    return pl.pallas_call(
        body, name=name,
        grid_spec=pltpu.PrefetchScalarGridSpec(
            num_scalar_prefetch=1, grid=(2,),
            in_specs=[pl.BlockSpec((1, th, cols), lambda i, s: (s[0], i, 0)), part(0), part(1), part(2)],
            out_specs=pl.BlockSpec((th, cols), lambda i, s: (i, 0))),
        out_shape=jax.ShapeDtypeStruct((h, cols), F32), compiler_params=_params(("parallel",)),
    )(chip_idx, a, parts, parts, parts)


WEIGHTS = ("mix_norm", "e_w_in", "e_conv_w", "e_conv_b", "e_dt_bias", "e_a_log", "e_d", "e_ssm_norm", "e_w_out",
           "o_w_in", "o_dw_w", "o_dw_b", "o_ln_g", "o_ln_b", "o_a_re", "o_a_im", "o_b_re", "o_b_im", "o_c_re",
           "o_c_im", "o_d", "o_log_step", "o_glu_w", "o_w_out", "ffn_norm", "ffn_w_up", "ffn_dw_w", "ffn_dw_b",
           "ffn_w_down", "final_norm")
BIG = (("e_w_in", 2), ("e_w_out", 1), ("o_w_in", 2), ("o_glu_w", 1), ("o_w_out", 1), ("ffn_w_up", 2), ("ffn_w_down", 1))
SMALL_SHARDED = (("e_conv_w", 2), ("o_dw_w", 2), ("o_dw_b", 1), ("o_ln_g", 1), ("o_ln_b", 1), ("o_d", 1), ("ffn_dw_w", 2))
REPLICATED = tuple(n for n in WEIGHTS if n not in dict(BIG + SMALL_SHARDED))
PACK_ROWS = 8


def _pack(arrays, dtype, row_unit=PACK_ROWS):
    flat = jnp.concatenate([a.astype(dtype).reshape(-1) for a in arrays])
    rows = -(-flat.size // (LANE * row_unit)) * row_unit
    return jnp.pad(flat, (0, rows * LANE - flat.size)).reshape(rows, LANE)


def _unpack(flat, shapes, lead=()):
    out, off = [], 0
    for shape in shapes:
        size = int(np.prod(shape))
        out.append(flat[..., off:off + size].reshape(lead + tuple(shape)))
        off += size
    return out


def _join_shards(parts, axis):
    return jnp.concatenate([parts[s] for s in range(4)], axis=axis)


def _rows2d(a):
    return a.reshape(-1, a.shape[-1])


def _layer_shards(g, axis):
    if g.ndim == 3:
        return g.reshape(4, 2, g.shape[1] // 2, g.shape[2])
    rows, cols = g.shape
    if axis == 0:
        return g.reshape(4, 2, rows // 8, cols)
    return g.reshape(rows, 4, cols // 4).transpose(1, 0, 2).reshape(4, 2, rows // 2, cols // 4)


def kernel(x, mix_norm, e_w_in, e_conv_w, e_conv_b, e_dt_bias, e_a_log, e_d, e_ssm_norm, e_w_out, o_w_in, o_dw_w, o_dw_b, o_ln_g, o_ln_b, o_a_re, o_a_im, o_b_re, o_b_im, o_c_re, o_c_im, o_d, o_log_step, o_glu_w, o_w_out, ffn_norm, ffn_w_up, ffn_dw_w, ffn_dw_b, ffn_w_down, final_norm, loss_target, m_mix_norm, m_e_w_in, m_e_conv_w, m_e_conv_b, m_e_dt_bias, m_e_a_log, m_e_d, m_e_ssm_norm, m_e_w_out, m_o_w_in, m_o_dw_w, m_o_dw_b, m_o_ln_g, m_o_ln_b, m_o_a_re, m_o_a_im, m_o_b_re, m_o_b_im, m_o_c_re, m_o_c_im, m_o_d, m_o_log_step, m_o_glu_w, m_o_w_out, m_ffn_norm, m_ffn_w_up, m_ffn_dw_w, m_ffn_dw_b, m_ffn_w_down, m_final_norm, v_mix_norm, v_e_w_in, v_e_conv_w, v_e_conv_b, v_e_dt_bias, v_e_a_log, v_e_d, v_e_ssm_norm, v_e_w_out, v_o_w_in, v_o_dw_w, v_o_dw_b, v_o_ln_g, v_o_ln_b, v_o_a_re, v_o_a_im, v_o_b_re, v_o_b_im, v_o_c_re, v_o_c_im, v_o_d, v_o_log_step, v_o_glu_w, v_o_w_out, v_ffn_norm, v_ffn_w_up, v_ffn_dw_w, v_ffn_dw_b, v_ffn_w_down, v_final_norm):
    given = dict(locals())
    chip = 2 * lax.axis_index("x") + lax.axis_index("y")
    core = lax.axis_index("c")

    core_idx, chip_idx = core.reshape(1).astype(jnp.int32), chip.reshape(1).astype(jnp.int32)

    def whole(n, axis, parts):
        shape = given[n].shape
        own = given[n].astype(parts.dtype)
        return _join_shards(lax.dynamic_update_index_in_dim(parts.reshape((4,) + shape), own, chip, 0), axis)

    first, later = BIG[:1], BIG[1:]
    shards = {n: _rows2d(given[n]).astype(BF16) for n, _ in BIG}
    gathered = gather_shards([shards[n] for n, _ in first], [_rows2d(given[n]) for n, _ in SMALL_SHARDED])
    w = {n: given[n] for n in REPLICATED}
    for (n, axis), parts in zip(first + SMALL_SHARDED, gathered):
        w[n] = whole(n, axis, parts)
    later_keys = [(n, layer) for n, _ in later for layer in (range(2) if n.startswith("ffn_") else [None])]
    later_shards = [shards[n] if layer is None else given[n][layer].astype(BF16) for n, layer in later_keys]
    gather_plan = _gather_plan([a.shape[0] // 2 for a in later_shards])
    gather_handle, token = chip_exchange_start(later_shards, [(4,) + a.shape for a in later_shards], gather_plan,
                                               gathered[0], name="gather_start")
    w["mix_norm"] = w["mix_norm"] + token[0, 0]

    def late_weights(after):
        _, lands = chip_exchange_wait(gather_handle, gather_plan, after, name="gather_wait")
        out = {"ffn_w_up_pairs": [], "ffn_w_down": []}
        for (n, layer), own, parts in zip(later_keys, later_shards, finish_gather(lands)):
            if layer is None:
                out[n] = whole(n, dict(BIG)[n], parts)
                continue
            parts = lax.dynamic_update_index_in_dim(parts, own, chip, 0)
            if n == "ffn_w_up":
                out["ffn_w_up_pairs"].append(ffn_shards_to_pairs(parts, name=f"ffn{layer}_up_pairs"))
            else:
                out[n].append(parts.reshape(-1, parts.shape[-1]))
        return out

    groups = []

    def finish_group(after):
        group = groups[-1]
        group["sums"], group["parts"] = chip_exchange_wait(group.pop("handle"), _scatter_plan, after,
                                                           name=f"scatter_wait_{len(groups) - 1}")

    def early_reduce(layer_grads):
        keys = list(layer_grads)
        if groups:
            finish_group(layer_grads[keys[0]])
        tag = len(groups)
        parts = [_layer_shards(layer_grads[k], dict(BIG)[k[0]] - 1) for k in keys]
        sums = [add_own_half(g, r, core_idx, name=f"add_own_half_{n}{layer}")
                for g, r, (n, layer) in zip(parts, exchange_halves(parts, name=f"exchange_halves_{tag}"), keys)]
        handle, zeros = chip_exchange_start(sums, [(3,) + a.shape[1:] for a in sums], _scatter_plan, sums[0],
                                            name=f"scatter_start_{tag}")
        groups.append(dict(keys=keys, handle=handle))
        return zeros[0, 0]

    loss, dx, grads = local_step(x[0], loss_target[0], w, late_weights, early_reduce)
    finish_group(dx)
    keys = [k for group in groups for k in group["keys"]]
    core_sums = [a for group in groups for a in group["sums"]]
    chip_parts = [a for group in groups for a in group["parts"]]
    assert sorted(keys) == sorted((n, layer) for n, _ in BIG for layer in range(len(grads[n]))), keys

    small_names = REPLICATED + tuple(n for n, _ in SMALL_SHARDED)
    small_sum = allreduce_small(_pack([grads[n] for n in small_names], F32, row_unit=16))
    reduced = dict(zip(small_names, _unpack(small_sum.reshape(-1), [grads[n].shape for n in small_names])))
    for n, axis in SMALL_SHARDED:
        width = given[n].shape[axis]
        reduced[n] = lax.dynamic_slice_in_dim(reduced[n], chip * width, width, axis=axis)

    mine = [add_chip_parts(a, p, chip_idx, name=f"add_chip_parts_{n}{layer}")
            for a, p, (n, layer) in zip(core_sums, chip_parts, keys)]
    layers = {}
    for (n, layer), own, other in zip(keys, mine, swap_halves(mine)):
        both = jnp.where(core == 0, jnp.stack([own, other]), jnp.stack([other, own]))
        layers.setdefault(n, {})[layer] = both.reshape(given[n].shape[1:])
    for n, _ in BIG:
        reduced[n] = jnp.stack([layers[n][layer] for layer in sorted(layers[n])])

    delta, new_m, new_v = {}, {}, {}
    for n in WEIGHTS:
        delta[n], new_m[n], new_v[n] = adamw(given[n], reduced[n], given["m_" + n], given["v_" + n], name="adamw_" + n)

    total = lax.psum(loss[0, 0], ("x", "y", "c"))
    return (total, dx[None], *[reduced[n] for n in WEIGHTS], *[delta[n] for n in WEIGHTS],
            *[new_m[n] for n in WEIGHTS], *[new_v[n] for n in WEIGHTS])
```

```python
import functools
import math
from typing import NamedTuple

import numpy as np
import jax
import jax.numpy as jnp
from jax import lax
from jax.experimental import pallas as pl
from jax.experimental.pallas import tpu as pltpu

F32 = jnp.float32
BF16 = jnp.bfloat16
HIGHEST = lax.Precision.HIGHEST
MESH = pl.DeviceIdType.MESH

D_MODEL = 1024
EPS = 1e-6
RET_HEADS, RET_DK, RET_DV, CHUNK = 4, 128, 256, 128
ROPE_BASE = 10000.0
SSM_HEADS, SSM_P, SSM_N, SSM_GROUPS = 16, 64, 128, 2
SSM_DINNER = SSM_HEADS * SSM_P
EVEN_IN, EVEN_IN_PAD = 5648, 5760
S5_GROUPS, S5_GROUP, S5_STATE = 32, 16, 64
S5_LANES = S5_GROUPS * S5_STATE
SCAN_SEG = 32
D_FF = 2816
ADAM_LR, ADAM_B1, ADAM_B2, ADAM_EPS, ADAM_WD, ADAM_STEP = 0.001, 0.9, 0.999, 1e-08, 0.01, 10

LANE = 128
VMEM_LIMIT = 56 * 1024 * 1024


def _params(sem=None, **kw):
    return pltpu.CompilerParams(dimension_semantics=sem, vmem_limit_bytes=VMEM_LIMIT, **kw)


def _tile(n, target, unit=LANE):
    if n <= target:
        return n
    t = (target // unit) * unit
    while t >= unit:
        if n % t == 0:
            return t
        t -= unit
    return n


def _silu(x):
    return x * jax.nn.sigmoid(x)


def _mm(a, b):
    return jnp.dot(a.astype(BF16), b.astype(BF16), preferred_element_type=F32)


def _mm_nt(a, b):
    return lax.dot_general(a.astype(BF16), b.astype(BF16), (((1,), (1,)), ((), ())), preferred_element_type=F32)


def _mm_tn(a, b):
    return lax.dot_general(a.astype(BF16), b.astype(BF16), (((0,), (0,)), ((), ())), preferred_element_type=F32)


def _dot_hi(a, b):
    return jnp.dot(a, b, precision=HIGHEST, preferred_element_type=F32)


def _dot_hi_tn(a, b):
    return lax.dot_general(a, b, (((0,), (0,)), ((), ())), precision=HIGHEST, preferred_element_type=F32)


def _bf16_parts(v):
    hi = v.astype(BF16)
    rest = v - hi.astype(F32)
    mid = rest.astype(BF16)
    return hi, mid, (rest - mid.astype(F32)).astype(BF16)


def _dot_parts(v, fixed, dims, v_first):
    fixed = fixed.astype(BF16)
    out = None
    for part in _bf16_parts(v):
        ops = (part, fixed) if v_first else (fixed, part)
        p = lax.dot_general(*ops, (dims, ((), ())), preferred_element_type=F32)
        out = p if out is None else out + p
    return out


@jax.custom_vjp
def _times_01(v, ones):
    return _dot_parts(v, ones, ((1,), (0,)), True)


_times_01.defvjp(lambda v, ones: (_times_01(v, ones), ones),
                 lambda ones, g: (_dot_parts(g, ones, ((1,), (1,)), True), jnp.zeros_like(ones)))


@jax.custom_vjp
def _01_times(ones, v):
    return _dot_parts(v, ones, ((1,), (0,)), False)


_01_times.defvjp(lambda ones, v: (_01_times(ones, v), ones),
                 lambda ones, g: (jnp.zeros_like(ones), _dot_parts(g, ones, ((0,), (0,)), False)))


MATMUL_VMEM = 44 * 1024 * 1024


def matmul(a, b, *, ta=False, tb=False, res=None, out_dtype=F32, name):
    m, k = (a.shape[1], a.shape[0]) if ta else a.shape
    n = b.shape[0] if tb else b.shape[1]
    assert (b.shape[1] if tb else b.shape[0]) == k, (a.shape, b.shape, ta, tb)
    tm = _tile(m, 1536)
    tn = _tile(n, 1536)
    res_bytes = 0 if res is None else res.dtype.itemsize

    def vmem(tm, tn):
        return 2 * (tm * k * a.dtype.itemsize + tn * k * b.dtype.itemsize + tm * tn * (jnp.dtype(out_dtype).itemsize + res_bytes))

    while vmem(tm, tn) > MATMUL_VMEM:
        if tm % (2 * LANE) == 0 and (tm >= tn or tn % (2 * LANE) != 0):
            tm //= 2
        elif tn % (2 * LANE) == 0:
            tn //= 2
        else:
            break
    assert vmem(tm, tn) <= MATMUL_VMEM, (name, tm, tn, k)
    a_spec = pl.BlockSpec((k, tm), lambda i, j: (0, i)) if ta else pl.BlockSpec((tm, k), lambda i, j: (i, 0))
    b_spec = pl.BlockSpec((tn, k), lambda i, j: (j, 0)) if tb else pl.BlockSpec((k, tn), lambda i, j: (0, j))
    o_spec = pl.BlockSpec((tm, tn), lambda i, j: (i, j))
    dims = (((0 if ta else 1,), (1 if tb else 0,)), ((), ()))
    has_res = res is not None

    def body(a_ref, b_ref, *rest):
        o_ref = rest[-1]
        out = lax.dot_general(a_ref[...].astype(BF16), b_ref[...].astype(BF16), dims, preferred_element_type=F32)
        if has_res:
            out = out + rest[0][...].astype(F32)
        o_ref[...] = out.astype(o_ref.dtype)

    ins = [a, b] + ([res] if has_res else [])
    specs = [a_spec, b_spec] + ([o_spec] if has_res else [])
    return pl.pallas_call(
        body, name=name, grid=(m // tm, n // tn), in_specs=specs, out_specs=o_spec,
        out_shape=jax.ShapeDtypeStruct((m, n), out_dtype), compiler_params=_params(("parallel", "parallel")),
    )(*ins)


class Cols(NamedTuple):
    arr: jax.Array
    w: int
    j: int


def _cols(a):
    return a if isinstance(a, Cols) else Cols(a, a.shape[1], 0)


def _row_spec(c, tl):
    return pl.BlockSpec((tl, c.w), lambda i, j=c.j: (i, j))


def _whole_spec(p):
    return pl.BlockSpec(p.shape, lambda i, nd=p.ndim: (0,) * nd)


def rowwise_fwd(fn, rows, aux, pars, consts, outs, *, name, tl):
    rows = [_cols(r) for r in rows + aux]
    whole = list(pars) + list(consts)
    n_rows = len(rows)
    n_whole = len(whole)
    length = rows[0].arr.shape[0]
    tl = min(tl, length)

    def body(*refs):
        vals = [r[...].astype(F32) for r in refs[:n_rows]] + [r[...] for r in refs[n_rows:n_rows + n_whole]]
        res = fn(*vals)
        for o_ref, v in zip(refs[n_rows + n_whole:], res, strict=True):
            o_ref[...] = v.astype(o_ref.dtype)

    return pl.pallas_call(
        body, name=name, grid=(length // tl,),
        in_specs=[_row_spec(r, tl) for r in rows] + [_whole_spec(p) for p in whole],
        out_specs=[pl.BlockSpec((tl, w), lambda i: (i, 0)) for w, _ in outs],
        out_shape=[jax.ShapeDtypeStruct((length, w), dt) for w, dt in outs],
        compiler_params=_params(("parallel",)),
    )(*[r.arr for r in rows], *whole)


def rowwise_bwd(fn, rows, aux, pars, consts, cots, drow_dtypes, *, name, tl, add=None, merge=False):
    rows = [_cols(r) for r in rows]
    aux = [_cols(r) for r in aux]
    cots = [_cols(r) for r in cots]
    n_r, n_a, n_p, n_c, n_t = len(rows), len(aux), len(pars), len(consts), len(cots)
    length = rows[0].arr.shape[0]
    tl = min(tl, length)
    has_add = add is not None
    widths = [r.w for r in rows]

    def body(*refs):
        pos = 0
        r_vals = [r[...].astype(F32) for r in refs[pos:pos + n_r]]; pos += n_r
        a_vals = [r[...].astype(F32) for r in refs[pos:pos + n_a]]; pos += n_a
        p_vals = [r[...].astype(F32) for r in refs[pos:pos + n_p]]; pos += n_p
        c_vals = [r[...] for r in refs[pos:pos + n_c]]; pos += n_c
        t_vals = [r[...].astype(F32) for r in refs[pos:pos + n_t]]; pos += n_t
        add_val = None
        if has_add:
            add_val = refs[pos][...].astype(F32); pos += 1
        n_dr = 1 if merge else n_r
        dr_refs = refs[pos:pos + n_dr]; pos += n_dr
        dp_refs = refs[pos:pos + n_p]

        def f(*rp):
            return fn(*rp[:n_r], *a_vals, *rp[n_r:], *c_vals)

        _, vjp = jax.vjp(f, *r_vals, *p_vals)
        grads = vjp(tuple(t_vals))
        drows = list(grads[:n_r])
        if has_add:
            drows[0] = drows[0] + add_val
        if merge:
            off = 0
            for w, d in zip(widths, drows):
                dr_refs[0][:, off:off + w] = d.astype(dr_refs[0].dtype)
                off += w
        else:
            for r, d in zip(dr_refs, drows):
                r[...] = d.astype(r.dtype)
        i = pl.program_id(0)
        for r, d in zip(dp_refs, grads[n_r:]):
            @pl.when(i == 0)
            def _(r=r, d=d):
                r[...] = d

            @pl.when(i > 0)
            def _(r=r, d=d):
                r[...] += d

    if merge:
        dr_specs = [pl.BlockSpec((tl, sum(widths)), lambda i: (i, 0))]
        dr_shapes = [jax.ShapeDtypeStruct((length, sum(widths)), drow_dtypes[0])]
    else:
        dr_specs = [pl.BlockSpec((tl, w), lambda i: (i, 0)) for w in widths]
        dr_shapes = [jax.ShapeDtypeStruct((length, w), dt) for w, dt in zip(widths, drow_dtypes)]
    ins = [r.arr for r in rows + aux] + list(pars) + list(consts) + [r.arr for r in cots] + ([add] if has_add else [])
    specs = ([_row_spec(r, tl) for r in rows + aux] + [_whole_spec(p) for p in list(pars) + list(consts)]
             + [_row_spec(r, tl) for r in cots] + ([pl.BlockSpec((tl, add.shape[1]), lambda i: (i, 0))] if has_add else []))
    return pl.pallas_call(
        body, name=name, grid=(length // tl,), in_specs=specs,
        out_specs=dr_specs + [_whole_spec(p) for p in pars],
        out_shape=dr_shapes + [jax.ShapeDtypeStruct(p.shape, F32) for p in pars],
        compiler_params=_params(("arbitrary",)),
    )(*ins)


def whole_fwd(fn, ins, out_shapes, *, name):
    n_in = len(ins)

    def body(*refs):
        res = fn(*[r[...] for r in refs[:n_in]])
        for o_ref, v in zip(refs[n_in:], res, strict=True):
            o_ref[...] = v

    return pl.pallas_call(body, name=name, out_shape=[jax.ShapeDtypeStruct(s, F32) for s in out_shapes],
                          compiler_params=_params())(*ins)


def whole_bwd(fn, ins, n_diff, cots, *, name):
    n_in, n_t = len(ins), len(cots)

    def body(*refs):
        vals = [r[...] for r in refs[:n_in]]
        t_vals = [r[...] for r in refs[n_in:n_in + n_t]]
        _, vjp = jax.vjp(lambda *d: fn(*d, *vals[n_diff:]), *vals[:n_diff])
        for o_ref, g in zip(refs[n_in + n_t:], vjp(tuple(t_vals)), strict=True):
            o_ref[...] = g

    return pl.pallas_call(body, name=name, out_shape=[jax.ShapeDtypeStruct(a.shape, F32) for a in ins[:n_diff]],
                          compiler_params=_params())(*ins, *cots)


CONV_ROWS = 256


def _conv_geometry(x, w, cw, off):
    width = w.shape[1]
    x = Cols(x, width, 0)
    length = x.arr.shape[0]
    taps = w.shape[0]
    pad = -(-(taps - 1) // 8) * 8
    assert off % cw == 0 and width % cw == 0, (off, width, cw)
    return x, length, taps, pad, off // cw


def _conv_taps(xp_ref, w_ref, base, taps, pad, init, lanes=slice(None)):
    acc = init
    for k in range(taps):
        acc = acc + w_ref[k:k + 1, lanes] * xp_ref[pl.ds(base + pad - (taps - 1) + k, init.shape[0]), :]
    return acc


def conv_fwd(x, w, b, *, act, name, off=0, cw=LANE, out_dtype=F32):
    x, length, taps, pad, jb = _conv_geometry(x, w, cw, off)
    rc = min(CONV_ROWS, length)

    def body(x_ref, w_ref, b_ref, o_ref, xp_ref):
        xp_ref[0:pad, :] = jnp.zeros((pad, cw), F32)
        xp_ref[pad:pad + length, :] = x_ref[...].astype(F32)

        def chunk(r, carry):
            base = pl.multiple_of(r * rc, rc)
            acc = _conv_taps(xp_ref, w_ref, base, taps, pad, jnp.broadcast_to(b_ref[...], (rc, cw)))
            if act:
                acc = _silu(acc)
            o_ref[pl.ds(base, rc), :] = acc.astype(o_ref.dtype)
            return carry

        lax.fori_loop(0, length // rc, chunk, 0)

    return pl.pallas_call(
        body, name=name, grid=(x.w // cw,),
        in_specs=[pl.BlockSpec((length, cw), lambda j: (0, jb + j)), pl.BlockSpec((taps, cw), lambda j: (0, j)),
                  pl.BlockSpec((1, cw), lambda j: (0, j))],
        out_specs=pl.BlockSpec((length, cw), lambda j: (0, j)),
        out_shape=jax.ShapeDtypeStruct((length, x.w), out_dtype),
        scratch_shapes=[pltpu.VMEM((pad + length, cw), F32)],
        compiler_params=_params(("parallel",)),
    )(x.arr, w, b)


def conv_bwd(x, w, b, dy, *, act, name, off=0, cw=LANE, dx_dtype=F32):
    x, length, taps, pad, jb = _conv_geometry(x, w, cw, off)
    rc = min(CONV_ROWS, length)

    def body(x_ref, w_ref, b_ref, dy_ref, dx_ref, dw_ref, db_ref, xp_ref, gp_ref):
        xp_ref[0:pad, :] = jnp.zeros((pad, cw), F32)
        xp_ref[pad:pad + length, :] = x_ref[...].astype(F32)
        gp_ref[length:length + pad, :] = jnp.zeros((pad, cw), F32)
        if act:
            def pre_chunk(r, carry):
                base = pl.multiple_of(r * rc, rc)
                pre = _conv_taps(xp_ref, w_ref, base, taps, pad, jnp.broadcast_to(b_ref[...], (rc, cw)))
                sig = jax.nn.sigmoid(pre)
                gp_ref[pl.ds(base, rc), :] = dy_ref[pl.ds(base, rc), :].astype(F32) * (sig * (1.0 + pre * (1.0 - sig)))
                return carry

            lax.fori_loop(0, length // rc, pre_chunk, 0)
        else:
            gp_ref[0:length, :] = dy_ref[...].astype(F32)
        dw_ref[...] = jnp.zeros((taps, cw), F32)
        db_ref[...] = jnp.zeros((1, cw), F32)

        def chunk(r, carry):
            base = pl.multiple_of(r * rc, rc)
            acc = jnp.zeros((rc, cw), F32)
            g = gp_ref[pl.ds(base, rc), :]
            for k in range(taps):
                acc = acc + w_ref[k:k + 1, :] * gp_ref[pl.ds(base + (taps - 1) - k, rc), :]
                xs = xp_ref[pl.ds(base + pad - (taps - 1) + k, rc), :]
                dw_ref[k:k + 1, :] += jnp.sum(g * xs, axis=0, keepdims=True)
            db_ref[...] += jnp.sum(g, axis=0, keepdims=True)
            dx_ref[pl.ds(base, rc), :] = acc.astype(dx_ref.dtype)
            return carry

        lax.fori_loop(0, length // rc, chunk, 0)

    dy = _cols(dy)
    assert dy.j == 0 and dy.w == x.w
    return pl.pallas_call(
        body, name=name, grid=(x.w // cw,),
        in_specs=[pl.BlockSpec((length, cw), lambda j: (0, jb + j)), pl.BlockSpec((taps, cw), lambda j: (0, j)),
                  pl.BlockSpec((1, cw), lambda j: (0, j)), pl.BlockSpec((length, cw), lambda j: (0, j))],
        out_specs=[pl.BlockSpec((length, cw), lambda j: (0, j)), pl.BlockSpec((taps, cw), lambda j: (0, j)),
                   pl.BlockSpec((1, cw), lambda j: (0, j))],
        out_shape=[jax.ShapeDtypeStruct((length, x.w), dx_dtype), jax.ShapeDtypeStruct((taps, x.w), F32),
                   jax.ShapeDtypeStruct((1, x.w), F32)],
        scratch_shapes=[pltpu.VMEM((pad + length, cw), F32), pltpu.VMEM((length + pad, cw), F32)],
        compiler_params=_params(("parallel",)),
    )(x.arr, w, b, dy.arr)


def _conv_transpose(xp_ref, gp_ref, w_ref, dx_ref, dw_ref, db_ref, lanes, length, taps, pad, rc):
    dw_ref[:, lanes] = jnp.zeros((taps, LANE), F32)
    db_ref[:, lanes] = jnp.zeros((1, LANE), F32)

    def chunk(r, carry):
        base = pl.multiple_of(r * rc, rc)
        acc = jnp.zeros((rc, LANE), F32)
        g = gp_ref[pl.ds(base, rc), :]
        for k in range(taps):
            acc = acc + w_ref[k:k + 1, lanes] * gp_ref[pl.ds(base + (taps - 1) - k, rc), :]
            xs = xp_ref[pl.ds(base + pad - (taps - 1) + k, rc), :]
            dw_ref[k:k + 1, lanes] += jnp.sum(g * xs, axis=0, keepdims=True)
        db_ref[:, lanes] += jnp.sum(g, axis=0, keepdims=True)
        dx_ref[pl.ds(base, rc), lanes] = acc.astype(dx_ref.dtype)
        return carry

    lax.fori_loop(0, length // rc, chunk, 0)


LANE_PAIR_ROWS = 1024


def ffn_interleave(a, name=None):
    rows, width = a.shape
    nb = width // (2 * LANE)
    if rows < LANE_PAIR_ROWS:
        return a.reshape(rows, 2, nb, LANE).swapaxes(1, 2).reshape(a.shape)

    def body(g_ref, u_ref, o_ref):
        o_ref[:, 0:LANE] = g_ref[...]
        o_ref[:, LANE:2 * LANE] = u_ref[...]

    tr = LANE_PAIR_ROWS
    return pl.pallas_call(
        body, name=name, grid=(rows // tr, nb),
        in_specs=[pl.BlockSpec((tr, LANE), lambda i, j: (i, j)), pl.BlockSpec((tr, LANE), lambda i, j: (i, nb + j))],
        out_specs=pl.BlockSpec((tr, 2 * LANE), lambda i, j: (i, j)),
        out_shape=jax.ShapeDtypeStruct(a.shape, a.dtype), compiler_params=_params(("parallel", "parallel")),
    )(a, a)


def ffn_deinterleave(a):
    rows, width = a.shape
    return a.reshape(rows, width // (2 * LANE), 2, LANE).swapaxes(1, 2).reshape(a.shape)


PAIR_COPY_ROWS = 512


def ffn_pairs_to_shards(a, *, name):
    rows, width = a.shape
    cols = width // 4
    per = cols // LANE
    tr = min(rows, PAIR_COPY_ROWS)

    def body(a_ref, o_ref):
        is_up = pl.program_id(1) >= 2
        for parity, chosen in ((0, jnp.logical_not(is_up)), (1, is_up)):
            @pl.when(chosen)
            def _(parity=parity):
                for t in range(per):
                    o_ref[0, :, t * LANE:(t + 1) * LANE] = a_ref[:, (2 * t + parity) * LANE:(2 * t + parity + 1) * LANE]

    return pl.pallas_call(
        body, name=name, grid=(rows // tr, 4),
        in_specs=[pl.BlockSpec((tr, 2 * cols), lambda i, s: (i, jnp.where(s >= 2, s - 2, s)))],
        out_specs=pl.BlockSpec((1, tr, cols), lambda i, s: (s, i, 0)),
        out_shape=jax.ShapeDtypeStruct((4, rows, cols), a.dtype), compiler_params=_params(("parallel", "parallel")),
    )(a)


def ffn_shards_to_pairs(parts, *, name):
    _, rows, cols = parts.shape
    per = cols // LANE
    tr = min(rows, PAIR_COPY_ROWS)

    def body(gate_ref, up_ref, o_ref):
        for t in range(per):
            o_ref[:, 2 * t * LANE:(2 * t + 1) * LANE] = gate_ref[0, :, t * LANE:(t + 1) * LANE]
            o_ref[:, (2 * t + 1) * LANE:(2 * t + 2) * LANE] = up_ref[0, :, t * LANE:(t + 1) * LANE]

    return pl.pallas_call(
        body, name=name, grid=(rows // tr, 2),
        in_specs=[pl.BlockSpec((1, tr, cols), lambda i, j: (j, i, 0)), pl.BlockSpec((1, tr, cols), lambda i, j: (2 + j, i, 0))],
        out_specs=pl.BlockSpec((tr, 2 * cols), lambda i, j: (i, j)),
        out_shape=jax.ShapeDtypeStruct((rows, 4 * cols), parts.dtype), compiler_params=_params(("parallel", "parallel")),
    )(parts, parts)


GATE, UP = slice(0, LANE), slice(LANE, 2 * LANE)


def _ffn_geometry(a, w):
    length, width = a.shape
    taps = w.shape[0]
    return length, width, width // (2 * LANE), taps, -(-(taps - 1) // 8) * 8, min(CONV_ROWS, length)


def _ffn_pre(xg_ref, xu_ref, w_ref, b_ref, base, taps, pad, rc):
    gate = _conv_taps(xg_ref, w_ref, base, taps, pad, jnp.broadcast_to(b_ref[:, GATE], (rc, LANE)), GATE)
    up = _conv_taps(xu_ref, w_ref, base, taps, pad, jnp.broadcast_to(b_ref[:, UP], (rc, LANE)), UP)
    return gate, up


def ffn_conv_act(a, w, b, *, name):
    length, width, nb, taps, pad, rc = _ffn_geometry(a, w)

    def body(a_ref, w_ref, b_ref, o_ref, xg_ref, xu_ref):
        for xp_ref, lanes in ((xg_ref, GATE), (xu_ref, UP)):
            xp_ref[0:pad, :] = jnp.zeros((pad, LANE), F32)
            xp_ref[pad:pad + length, :] = a_ref[:, lanes].astype(F32)

        def chunk(r, carry):
            base = pl.multiple_of(r * rc, rc)
            gate, up = _ffn_pre(xg_ref, xu_ref, w_ref, b_ref, base, taps, pad, rc)
            o_ref[pl.ds(base, rc), :] = (_silu(gate) * up).astype(o_ref.dtype)
            return carry

        lax.fori_loop(0, length // rc, chunk, 0)

    pair = lambda rows: pl.BlockSpec((rows, 2 * LANE), lambda j: (0, j))
    return pl.pallas_call(
        body, name=name, grid=(nb,), in_specs=[pair(length), pair(taps), pair(1)],
        out_specs=pl.BlockSpec((length, LANE), lambda j: (0, j)),
        out_shape=jax.ShapeDtypeStruct((length, width // 2), BF16),
        scratch_shapes=[pltpu.VMEM((pad + length, LANE), F32), pltpu.VMEM((pad + length, LANE), F32)],
        compiler_params=_params(("parallel",)),
    )(a, w, b)


def ffn_conv_act_bwd(a, w, b, dact, *, name):
    length, width, nb, taps, pad, rc = _ffn_geometry(a, w)

    def body(a_ref, w_ref, b_ref, dy_ref, da_ref, dw_ref, db_ref, xg_ref, xu_ref, gg_ref, gu_ref):
        for xp_ref, lanes in ((xg_ref, GATE), (xu_ref, UP)):
            xp_ref[0:pad, :] = jnp.zeros((pad, LANE), F32)
            xp_ref[pad:pad + length, :] = a_ref[:, lanes].astype(F32)
        for gp_ref in (gg_ref, gu_ref):
            gp_ref[length:length + pad, :] = jnp.zeros((pad, LANE), F32)

        def pre_chunk(r, carry):
            base = pl.multiple_of(r * rc, rc)
            gate, up = _ffn_pre(xg_ref, xu_ref, w_ref, b_ref, base, taps, pad, rc)
            sig = jax.nn.sigmoid(gate)
            dy = dy_ref[pl.ds(base, rc), :]
            gg_ref[pl.ds(base, rc), :] = dy * up * (sig * (1.0 + gate * (1.0 - sig)))
            gu_ref[pl.ds(base, rc), :] = dy * (gate * sig)
            return carry

        lax.fori_loop(0, length // rc, pre_chunk, 0)
        _conv_transpose(xg_ref, gg_ref, w_ref, da_ref, dw_ref, db_ref, GATE, length, taps, pad, rc)
        _conv_transpose(xu_ref, gu_ref, w_ref, da_ref, dw_ref, db_ref, UP, length, taps, pad, rc)

    pair = lambda rows: pl.BlockSpec((rows, 2 * LANE), lambda j: (0, j))
    return pl.pallas_call(
        body, name=name, grid=(nb,),
        in_specs=[pair(length), pair(taps), pair(1), pl.BlockSpec((length, LANE), lambda j: (0, j))],
        out_specs=[pair(length), pair(taps), pair(1)],
        out_shape=[jax.ShapeDtypeStruct((length, width), BF16), jax.ShapeDtypeStruct((taps, width), F32),
                   jax.ShapeDtypeStruct((1, width), F32)],
        scratch_shapes=[pltpu.VMEM((pad + length, LANE), F32), pltpu.VMEM((pad + length, LANE), F32),
                        pltpu.VMEM((length + pad, LANE), F32), pltpu.VMEM((length + pad, LANE), F32)],
        compiler_params=_params(("parallel",)),
    )(a, w, b, dact)


def _retention_consts():
    h = np.arange(RET_HEADS, dtype=np.float32)
    log_g = np.log1p(-(2.0 ** (-5.0 - h))).astype(np.float32)
    idx = np.arange(CHUNK, dtype=np.float32)
    diff = idx[:, None] - idx[None, :]
    intra = np.where(diff[None] >= 0, np.exp(np.maximum(diff, 0.0)[None] * log_g[:, None, None]), 0.0)
    zeta = np.exp((CHUNK - 1 - idx)[None, :] * log_g[:, None])
    xi = np.exp((idx + 1)[None, :] * log_g[:, None])
    decay = np.exp(CHUNK * log_g)
    zeta = np.broadcast_to(zeta[:, :, None], (RET_HEADS, CHUNK, RET_DK))
    xi = np.broadcast_to(xi[:, :, None], (RET_HEADS, CHUNK, RET_DV))
    return (jnp.asarray(intra, F32), jnp.asarray(zeta, F32), jnp.asarray(xi, F32), [float(d) for d in decay])


def _rotary_tables(length):
    inv = ROPE_BASE ** (-jnp.arange(0, RET_DK, 2, dtype=F32) / RET_DK)
    ang = jnp.arange(length).astype(F32)[:, None] * inv[None, :]
    cos, sin = jnp.cos(ang), jnp.sin(ang)
    return jnp.concatenate([cos, cos], axis=1), jnp.concatenate([-sin, sin], axis=1)


def _rot(x, cos2, sin2):
    return x * cos2 + pltpu.roll(x, RET_DK // 2, 1) * sin2


def _rot_t(y, cos2, sin2):
    return y * cos2 + pltpu.roll(y * sin2, RET_DK // 2, 1)


def _ret_chunk(q, k, v, g, state, intra, zeta, xi, decay):
    s = _mm_nt(q, k) * intra
    kv = _mm_tn(k * zeta, v)
    o = _mm(s, v) + _mm(q, state) * xi
    oc = o - jnp.mean(o, axis=-1, keepdims=True)
    r = oc * lax.rsqrt(jnp.mean(oc * oc, axis=-1, keepdims=True) + EPS)
    return _silu(g) * r, state * decay + kv


RET_QK, RET_V = RET_HEADS * RET_DK, RET_HEADS * RET_DV


def _ret_specs(rev, nc):
    def cidx(c):
        return nc - 1 - c if rev else c
    whole = lambda shape: pl.BlockSpec(shape, lambda c: (0,) * len(shape))
    return [
        pl.BlockSpec((CHUNK, RET_QK), lambda c: (cidx(c), 0)),
        pl.BlockSpec((CHUNK, RET_QK), lambda c: (cidx(c), 1)),
        pl.BlockSpec((CHUNK, RET_V), lambda c: (cidx(c), 1)),
        pl.BlockSpec((CHUNK, RET_V), lambda c: (cidx(c), 2)),
        pl.BlockSpec((CHUNK, RET_DK), lambda c: (cidx(c), 0)),
        pl.BlockSpec((CHUNK, RET_DK), lambda c: (cidx(c), 0)),
        whole((RET_HEADS, CHUNK, CHUNK)), whole((RET_HEADS, CHUNK, RET_DK)), whole((RET_HEADS, CHUNK, RET_DV)),
    ], cidx


def _head(ref, h, width):
    return ref[:, h * width:(h + 1) * width].astype(F32)


def retention_fwd(proj, cos2, sin2):
    length = proj.shape[0]
    nc = length // CHUNK
    intra, zeta, xi, decays = _retention_consts()
    specs, _ = _ret_specs(False, nc)
    scale = RET_DK ** -0.5

    def body(q_ref, k_ref, v_ref, g_ref, cos_ref, sin_ref, intra_ref, zeta_ref, xi_ref, y_ref, st_ref, state):
        @pl.when(pl.program_id(0) == 0)
        def _():
            state[...] = jnp.zeros_like(state)

        cos2v, sin2v = cos_ref[...], sin_ref[...]
        for h in range(RET_HEADS):
            q = _rot(_head(q_ref, h, RET_DK), cos2v, sin2v)
            k = _rot(_head(k_ref, h, RET_DK), cos2v, sin2v) * scale
            st_ref[h, 0] = state[h]
            y, new_state = _ret_chunk(q, k, _head(v_ref, h, RET_DV), _head(g_ref, h, RET_DV), state[h],
                                      intra_ref[h], zeta_ref[h], xi_ref[h], decays[h])
            y_ref[:, h * RET_DV:(h + 1) * RET_DV] = y.astype(y_ref.dtype)
            state[h] = new_state

    return pl.pallas_call(
        body, name="retention_fwd", grid=(nc,), in_specs=specs,
        out_specs=[pl.BlockSpec((CHUNK, RET_V), lambda c: (c, 0)),
                   pl.BlockSpec((RET_HEADS, 1, RET_DK, RET_DV), lambda c: (0, c, 0, 0))],
        out_shape=[jax.ShapeDtypeStruct((length, RET_V), BF16),
                   jax.ShapeDtypeStruct((RET_HEADS, nc, RET_DK, RET_DV), F32)],
        scratch_shapes=[pltpu.VMEM((RET_HEADS, RET_DK, RET_DV), F32)],
        compiler_params=_params(("arbitrary",)),
    )(proj, proj, proj, proj, cos2, sin2, intra, zeta, xi)


def retention_bwd(proj, cos2, sin2, states, dmix):
    length = proj.shape[0]
    nc = length // CHUNK
    intra, zeta, xi, decays = _retention_consts()
    specs, cidx = _ret_specs(True, nc)
    scale = RET_DK ** -0.5

    def body(q_ref, k_ref, v_ref, g_ref, cos_ref, sin_ref, intra_ref, zeta_ref, xi_ref, st_ref, dy_ref,
             dq_ref, dk_ref, dv_ref, dg_ref, dstate):
        @pl.when(pl.program_id(0) == 0)
        def _():
            dstate[...] = jnp.zeros_like(dstate)

        cos2v, sin2v = cos_ref[...], sin_ref[...]
        for h in range(RET_HEADS):
            q = _rot(_head(q_ref, h, RET_DK), cos2v, sin2v)
            k = _rot(_head(k_ref, h, RET_DK), cos2v, sin2v) * scale
            intra_v, zeta_v, xi_v, decay = intra_ref[h], zeta_ref[h], xi_ref[h], decays[h]
            _, vjp = jax.vjp(lambda q, k, v, g, s: _ret_chunk(q, k, v, g, s, intra_v, zeta_v, xi_v, decay),
                             q, k, _head(v_ref, h, RET_DV), _head(g_ref, h, RET_DV), st_ref[h, 0])
            dq, dk, dv, dg, ds = vjp((_head(dy_ref, h, RET_DV).astype(F32), dstate[h]))
            dq_ref[:, h * RET_DK:(h + 1) * RET_DK] = _rot_t(dq, cos2v, sin2v).astype(dq_ref.dtype)
            dk_ref[:, h * RET_DK:(h + 1) * RET_DK] = _rot_t(dk * scale, cos2v, sin2v).astype(dk_ref.dtype)
            dv_ref[:, h * RET_DV:(h + 1) * RET_DV] = dv.astype(dv_ref.dtype)
            dg_ref[:, h * RET_DV:(h + 1) * RET_DV] = dg.astype(dg_ref.dtype)
            dstate[h] = ds

    specs = specs + [pl.BlockSpec((RET_HEADS, 1, RET_DK, RET_DV), lambda c: (0, cidx(c), 0, 0)),
                     pl.BlockSpec((CHUNK, RET_V), lambda c: (cidx(c), 0))]
    row = lambda width: pl.BlockSpec((CHUNK, width), lambda c: (cidx(c), 0))
    return pl.pallas_call(
        body, name="retention_bwd", grid=(nc,), in_specs=specs,
        out_specs=[row(RET_QK), row(RET_QK), row(RET_V), row(RET_V)],
        out_shape=[jax.ShapeDtypeStruct((length, RET_QK), BF16), jax.ShapeDtypeStruct((length, RET_QK), BF16),
                   jax.ShapeDtypeStruct((length, RET_V), BF16), jax.ShapeDtypeStruct((length, RET_V), BF16)],
        scratch_shapes=[pltpu.VMEM((RET_HEADS, RET_DK, RET_DV), F32)],
        compiler_params=_params(("arbitrary",)),
    )(proj, proj, proj, proj, cos2, sin2, intra, zeta, xi, states, dmix)


def _ssd_consts():
    tri = np.tril(np.ones((CHUNK, CHUNK), np.float32))
    expand = np.zeros((LANE, SSM_DINNER), np.float32)
    for h in range(SSM_HEADS):
        expand[h, h * SSM_P:(h + 1) * SSM_P] = 1.0
    return jnp.asarray(tri), jnp.asarray(tri.T.copy()), jnp.asarray(expand)


def _ssd_chunk(xs, bm, cm, dtr, z, state, dt_bias, a_log, d_skip, norm_w, tri, tri_t, expand):
    gw = SSM_DINNER // SSM_GROUPS
    dt = jax.nn.softplus(dtr + dt_bias)
    a_neg = -jnp.exp(a_log)
    da = dt * a_neg
    acs = _dot_hi(tri, da)
    acs_t = _dot_hi_tn(da, tri_t)
    dt_x = _times_01(dt, expand)
    a_x = jnp.mean(_dot_hi(jnp.broadcast_to(a_neg, (8, LANE)), expand), axis=0, keepdims=True)
    da_x = dt_x * a_x
    acs_x = _01_times(tri, da_x)
    tot_x = jnp.sum(da_x, axis=0, keepdims=True)
    x_dt = xs * dt_x
    x_dec = x_dt * jnp.exp(tot_x - acs_x)
    e_acs = jnp.exp(acs_x)
    e_tot = jnp.exp(tot_x)
    lane = lax.broadcasted_iota(jnp.int32, (CHUNK, LANE), 1)
    sub = lax.broadcasted_iota(jnp.int32, (CHUNK, LANE), 0)
    causal = sub >= lane
    ys, new_states = [], []
    for g in range(SSM_GROUPS):
        bg = bm[:, g * SSM_N:(g + 1) * SSM_N]
        cg = cm[:, g * SSM_N:(g + 1) * SSM_N]
        sg = state[:, g * gw:(g + 1) * gw]
        cb = _mm_nt(cg, bg)
        y_off = _mm(cg, sg) * e_acs[:, g * gw:(g + 1) * gw]
        new_states.append(sg * e_tot[:, g * gw:(g + 1) * gw] + _mm_tn(bg, x_dec[:, g * gw:(g + 1) * gw]))
        pairs = []
        for p in range(gw // LANE):
            hp = g * (gw // LANE) + p
            xp = x_dt[:, hp * LANE:(hp + 1) * LANE]
            halves = []
            for head in (2 * hp, 2 * hp + 1):
                col = jnp.sum(jnp.where(lane == head, acs, 0.0), axis=1, keepdims=True)
                row = jnp.sum(jnp.where(sub == head, acs_t, 0.0), axis=0, keepdims=True)
                decay = jnp.exp(jnp.where(causal, col - row, -1e30))
                halves.append(_mm(cb * decay, xp))
            pairs.append(jnp.where(lane < SSM_P, halves[0], halves[1]))
        ys.append(jnp.concatenate(pairs, axis=1) + y_off)
    d_x = jnp.mean(_dot_hi(jnp.broadcast_to(d_skip, (8, LANE)), expand), axis=0, keepdims=True)
    y = (jnp.concatenate(ys, axis=1) + d_x * xs) * _silu(z)
    normed = []
    for g in range(SSM_GROUPS):
        yg = y[:, g * gw:(g + 1) * gw]
        normed.append(yg * lax.rsqrt(jnp.mean(yg * yg, axis=-1, keepdims=True) + EPS))
    return jnp.concatenate(normed, axis=1) * norm_w, jnp.concatenate(new_states, axis=1)


XBC = SSM_DINNER + 2 * SSM_GROUPS * SSM_N


def _ssd_specs(rev, nc):
    def cidx(c):
        return nc - 1 - c if rev else c
    row = lambda w, j: pl.BlockSpec((CHUNK, w), lambda c: (cidx(c), j))
    whole = lambda shape: pl.BlockSpec(shape, lambda c: (0,) * len(shape))
    return [row(XBC, 0), row(LANE, 0), row(SSM_DINNER, 3),
            whole((1, LANE)), whole((1, LANE)), whole((1, LANE)), whole((1, SSM_DINNER)),
            whole((CHUNK, CHUNK)), whole((CHUNK, CHUNK)), whole((LANE, SSM_DINNER))], cidx


def ssd_fwd(xbc, dt_raw, proj, dt_bias, a_log, d_skip, norm_w):
    length = proj.shape[0]
    nc = length // CHUNK
    tri, tri_t, expand = _ssd_consts()
    specs, _ = _ssd_specs(False, nc)

    def body(xbc_ref, dt_ref, z_ref, dtb_ref, alog_ref, d_ref, nw_ref, tri_ref, trit_ref, e_ref, y_ref, st_ref, state):
        @pl.when(pl.program_id(0) == 0)
        def _():
            state[...] = jnp.zeros_like(state)

        st_ref[0] = state[...]
        y, new_state = _ssd_chunk(
            xbc_ref[:, 0:SSM_DINNER], xbc_ref[:, SSM_DINNER:SSM_DINNER + 256], xbc_ref[:, SSM_DINNER + 256:XBC],
            dt_ref[...], z_ref[...].astype(F32), state[...], dtb_ref[...], alog_ref[...], d_ref[...], nw_ref[...],
            tri_ref[...], trit_ref[...], e_ref[...])
        y_ref[...] = y.astype(y_ref.dtype)
        state[...] = new_state

    return pl.pallas_call(
        body, name="ssd_fwd", grid=(nc,), in_specs=specs,
        out_specs=[pl.BlockSpec((CHUNK, SSM_DINNER), lambda c: (c, 0)),
                   pl.BlockSpec((1, SSM_N, SSM_DINNER), lambda c: (c, 0, 0))],
        out_shape=[jax.ShapeDtypeStruct((length, SSM_DINNER), BF16),
                   jax.ShapeDtypeStruct((nc, SSM_N, SSM_DINNER), F32)],
        scratch_shapes=[pltpu.VMEM((SSM_N, SSM_DINNER), F32)],
        compiler_params=_params(("arbitrary",)),
    )(xbc, dt_raw, proj, dt_bias, a_log, d_skip, norm_w, tri, tri_t, expand)


def ssd_bwd(xbc, dt_raw, proj, dt_bias, a_log, d_skip, norm_w, states, dmix):
    length = proj.shape[0]
    nc = length // CHUNK
    tri, tri_t, expand = _ssd_consts()
    specs, cidx = _ssd_specs(True, nc)

    def body(xbc_ref, dt_ref, z_ref, dtb_ref, alog_ref, d_ref, nw_ref, tri_ref, trit_ref, e_ref, st_ref, dy_ref,
             dxbc_ref, ddt_ref, dz_ref, ddtb_ref, dalog_ref, dd_ref, dnw_ref, dstate):
        c = pl.program_id(0)

        @pl.when(c == 0)
        def _():
            dstate[...] = jnp.zeros_like(dstate)

        tri_v, trit_v, e_v = tri_ref[...], trit_ref[...], e_ref[...]
        _, vjp = jax.vjp(
            lambda *a: _ssd_chunk(*a, tri_v, trit_v, e_v),
            xbc_ref[:, 0:SSM_DINNER], xbc_ref[:, SSM_DINNER:SSM_DINNER + 256], xbc_ref[:, SSM_DINNER + 256:XBC],
            dt_ref[...], z_ref[...].astype(F32), st_ref[0], dtb_ref[...], alog_ref[...], d_ref[...], nw_ref[...])
        dxs, dbm, dcm, ddt, dz, ds, ddtb, dalog, dd, dnw = vjp((dy_ref[...].astype(F32), dstate[...]))
        dxbc_ref[:, 0:SSM_DINNER] = dxs
        dxbc_ref[:, SSM_DINNER:SSM_DINNER + 256] = dbm
        dxbc_ref[:, SSM_DINNER + 256:XBC] = dcm
        ddt_ref[...] = ddt.astype(ddt_ref.dtype)
        dz_ref[...] = dz.astype(dz_ref.dtype)
        dstate[...] = ds
        for r, d in ((ddtb_ref, ddtb), (dalog_ref, dalog), (dd_ref, dd), (dnw_ref, dnw)):
            @pl.when(c == 0)
            def _(r=r, d=d):
                r[...] = d

            @pl.when(c > 0)
            def _(r=r, d=d):
                r[...] += d

    whole = lambda shape: pl.BlockSpec(shape, lambda c: (0,) * len(shape))
    specs = specs + [pl.BlockSpec((1, SSM_N, SSM_DINNER), lambda c: (cidx(c), 0, 0)),
                     pl.BlockSpec((CHUNK, SSM_DINNER), lambda c: (cidx(c), 1))]
    return pl.pallas_call(
        body, name="ssd_bwd", grid=(nc,), in_specs=specs,
        out_specs=[pl.BlockSpec((CHUNK, XBC), lambda c: (cidx(c), 0)), pl.BlockSpec((CHUNK, LANE), lambda c: (cidx(c), 0)),
                   pl.BlockSpec((CHUNK, SSM_DINNER), lambda c: (cidx(c), 0)),
                   whole((1, LANE)), whole((1, LANE)), whole((1, LANE)), whole((1, SSM_DINNER))],
        out_shape=[jax.ShapeDtypeStruct((length, XBC), F32), jax.ShapeDtypeStruct((length, LANE), BF16),
                   jax.ShapeDtypeStruct((length, SSM_DINNER), BF16),
                   jax.ShapeDtypeStruct((1, LANE), F32), jax.ShapeDtypeStruct((1, LANE), F32),
                   jax.ShapeDtypeStruct((1, LANE), F32), jax.ShapeDtypeStruct((1, SSM_DINNER), F32)],
        scratch_shapes=[pltpu.VMEM((SSM_N, SSM_DINNER), F32)],
        compiler_params=_params(("arbitrary",)),
    )(xbc, dt_raw, proj, dt_bias, a_log, d_skip, norm_w, tri, tri_t, expand, states, dmix)


def _cmul(ar, ai, br, bi):
    return ar * br - ai * bi, ar * bi + ai * br


def s5_scan(b_re, b_im, a_re, a_im, *, reverse=False, states=None, name, lw=256):
    length, lanes = b_re.shape
    nk = length // SCAN_SEG
    with_da = states is not None
    assert reverse or not with_da

    def shift(v):
        sub = lax.broadcasted_iota(jnp.int32, v.shape, 0)
        if reverse:
            return jnp.where(sub == SCAN_SEG - 1, 0.0, pltpu.roll(v, SCAN_SEG - 1, 0))
        return jnp.where(sub == 0, 0.0, pltpu.roll(v, 1, 0))

    def body(*refs):
        if with_da:
            bre_ref, bim_ref, are_ref, aim_ref, sre_ref, sim_ref, xre_ref, xim_ref, dare_ref, daim_ref = refs
        else:
            bre_ref, bim_ref, are_ref, aim_ref, xre_ref, xim_ref = refs
        ar = jnp.broadcast_to(are_ref[...], (SCAN_SEG, lw))
        ai = jnp.broadcast_to(aim_ref[...], (SCAN_SEG, lw))

        def tile(i):
            k = (nk - 1 - i) if reverse else i
            return pl.ds(pl.multiple_of(k * SCAN_SEG, SCAN_SEG), SCAN_SEG)

        def local(i, carry):
            xr, xi, pr, pi = carry
            rows = tile(i)
            mr, mi = _cmul(ar, ai, xr, xi)
            xr, xi = mr + bre_ref[rows, :], mi + bim_ref[rows, :]
            xre_ref[rows, :] = xr
            xim_ref[rows, :] = xi
            pr, pi = _cmul(ar, ai, pr, pi)
            return xr, xi, pr, pi

        zero = jnp.zeros((SCAN_SEG, lw), F32)
        one = jnp.ones((SCAN_SEG, lw), F32)
        er, ei, pr, pi = lax.fori_loop(0, nk, local, (zero, zero, one, zero))
        cr, ci = zero, zero
        for _ in range(SCAN_SEG - 1):
            mr, mi = _cmul(pr, pi, cr, ci)
            cr, ci = shift(er + mr), shift(ei + mi)

        def fix(i, carry):
            pr, pi, dr, di = carry
            rows = tile(i)
            pr, pi = _cmul(ar, ai, pr, pi)
            mr, mi = _cmul(pr, pi, cr, ci)
            xr, xi = xre_ref[rows, :] + mr, xim_ref[rows, :] + mi
            xre_ref[rows, :] = xr
            xim_ref[rows, :] = xi
            if with_da:
                k = nk - 1 - i
                prev = pl.ds(pl.multiple_of(jnp.maximum(k - 1, 0) * SCAN_SEG, SCAN_SEG), SCAN_SEG)
                last = pl.ds((nk - 1) * SCAN_SEG, SCAN_SEG)
                sub = lax.broadcasted_iota(jnp.int32, (SCAN_SEG, lw), 0)
                wr = jnp.where(sub == 0, 0.0, pltpu.roll(sre_ref[last, :], 1, 0))
                wi = jnp.where(sub == 0, 0.0, pltpu.roll(sim_ref[last, :], 1, 0))
                sr = jnp.where(k == 0, wr, sre_ref[prev, :])
                si = jnp.where(k == 0, wi, sim_ref[prev, :])
                dr, di = dr + xr * sr + xi * si, di + xi * sr - xr * si
            return pr, pi, dr, di

        _, _, dr, di = lax.fori_loop(0, nk, fix, (one, zero, zero, zero))
        if with_da:
            dare_ref[...] = jnp.sum(dr, axis=0, keepdims=True)
            daim_ref[...] = jnp.sum(di, axis=0, keepdims=True)

    col = pl.BlockSpec((length, lw), lambda j: (0, j))
    vec = pl.BlockSpec((1, lw), lambda j: (0, j))
    ins = [b_re, b_im, a_re, a_im] + (list(states) if with_da else [])
    in_specs = [col, col, vec, vec] + ([col, col] if with_da else [])
    out_specs = [col, col] + ([vec, vec] if with_da else [])
    out_shape = [jax.ShapeDtypeStruct((length, lanes), F32)] * 2 + ([jax.ShapeDtypeStruct((1, lanes), F32)] * 2 if with_da else [])
    return pl.pallas_call(
        body, name=name, grid=(lanes // lw,), in_specs=in_specs, out_specs=out_specs, out_shape=out_shape,
        compiler_params=_params(("parallel",)),
    )(*ins)


def _seg_interleave(v):
    length = v.shape[0]
    return v.reshape(SCAN_SEG, length // SCAN_SEG, -1).transpose(1, 0, 2).reshape(length, -1)


def _seg_deinterleave(v):
    length = v.shape[0]
    return v.reshape(length // SCAN_SEG, SCAN_SEG, -1).transpose(1, 0, 2).reshape(length, -1)


def _block_diag(m):
    eye = jnp.eye(S5_GROUPS, dtype=m.dtype)
    return (m.reshape(S5_GROUPS, S5_GROUP, 1, S5_STATE) * eye[:, None, :, None]).reshape(S5_GROUPS * S5_GROUP, S5_LANES)


def _block_diag_take(full, *, name):
    pairs = S5_GROUPS // 2
    rows, lanes = 2 * S5_GROUP, 2 * S5_STATE

    def body(f_ref, o_ref):
        o_ref[...] = f_ref[...]

    tiles = pl.pallas_call(
        body, name=name, grid=(pairs,), in_specs=[pl.BlockSpec((rows, lanes), lambda p: (p, p))],
        out_specs=pl.BlockSpec((rows, lanes), lambda p: (p, 0)),
        out_shape=jax.ShapeDtypeStruct((pairs * rows, lanes), full.dtype), compiler_params=_params(("parallel",)),
    )(full)
    tiles = tiles.reshape(pairs, 2, S5_GROUP, 2, S5_STATE)
    blocks = jnp.stack([tiles[:, 0, :, 0, :], tiles[:, 1, :, 1, :]], axis=1)
    return blocks.reshape(S5_GROUPS * S5_GROUP, S5_STATE)


def _s5_prep(a_re, a_im, log_step, b_re, b_im, rep):
    step = jnp.exp(log_step)
    mag = jnp.exp(a_re * step)
    ab_re = mag * jnp.cos(a_im * step)
    ab_im = mag * jnp.sin(a_im * step)
    den = a_re * a_re + a_im * a_im
    f_re = ((ab_re - 1.0) * a_re + ab_im * a_im) / den
    f_im = (ab_im * a_re - (ab_re - 1.0) * a_im) / den
    fr, fi = _dot_hi(rep, f_re), _dot_hi(rep, f_im)
    return ab_re, ab_im, fr * b_re - fi * b_im, fr * b_im + fi * b_re


def _rms(x, g):
    return (x * lax.rsqrt(jnp.mean(x * x, axis=-1, keepdims=True) + EPS) * g,)


def _glu(a, g):
    return (a * jax.nn.sigmoid(g),)


def _ln_silu(x, g, b):
    xc = x - jnp.mean(x, axis=-1, keepdims=True)
    var = jnp.mean(xc * xc, axis=-1, keepdims=True)
    return (_silu(xc * lax.rsqrt(var + EPS) * g + b),)


def _s5_post(y, u, d_skip, glu_w):
    s = jax.nn.gelu(y + d_skip * u)
    return (s * jax.nn.sigmoid(_mm(s, glu_w)),)


def loss_head(x, tgt, g, *, tl=512):
    length, d = x.shape
    tl = min(tl, length)

    def body(x_ref, t_ref, g_ref, loss_ref, dx_ref, dg_ref):
        i = pl.program_id(0)
        y, vjp = jax.vjp(lambda x, g: _rms(x, g)[0], x_ref[...], g_ref[...])
        err = y - t_ref[...]
        dx, dg = vjp(err * (1.0 / d))
        dx_ref[...] = dx
        part = jnp.broadcast_to(0.5 * jnp.sum(jnp.mean(err * err, axis=-1, keepdims=True), axis=0, keepdims=True), (1, LANE))

        @pl.when(i == 0)
        def _():
            loss_ref[...] = part
            dg_ref[...] = dg

        @pl.when(i > 0)
        def _():
            loss_ref[...] += part
            dg_ref[...] += dg

    row = pl.BlockSpec((tl, d), lambda i: (i, 0))
    return pl.pallas_call(
        body, name="loss_head", grid=(length // tl,),
        in_specs=[row, row, pl.BlockSpec((1, d), lambda i: (0, 0))],
        out_specs=[pl.BlockSpec((1, LANE), lambda i: (0, 0)), row, pl.BlockSpec((1, d), lambda i: (0, 0))],
        out_shape=[jax.ShapeDtypeStruct((1, LANE), F32), jax.ShapeDtypeStruct((length, d), F32),
                   jax.ShapeDtypeStruct((1, d), F32)],
        compiler_params=_params(("arbitrary",)),
    )(x, tgt, g)


def _pad_heads(v):
    return jnp.pad(v, ((0, 0), (0, LANE - v.shape[1])))


def local_step(x, tgt, w, late_weights=None, early_reduce=None):
    length = x.shape[0]
    cos2, sin2 = _rotary_tables(length)
    grads = {}
    w = dict(w)

    def rms_fwd(xin, g, name):
        return rowwise_fwd(_rms, [xin], [], [g], [], [(D_MODEL, BF16)], name=name, tl=512)[0]

    def rms_bwd(xin, g, dh, dxo, name):
        return rowwise_bwd(_rms, [xin], [], [g], [], [dh], [F32], name=name, tl=512, add=dxo)

    def ffn_fwd(i, xin):
        hf = rms_fwd(xin, w["ffn_norm"][i:i + 1], f"ffn{i}_norm")
        w_up = w["ffn_w_up_pairs"][i] if "ffn_w_up_pairs" in w else ffn_interleave(w["ffn_w_up"][i], name=f"ffn{i}_up_pairs")
        a = matmul(hf, w_up, out_dtype=BF16, name=f"ffn{i}_up")
        act = ffn_conv_act(a, ffn_interleave(w["ffn_dw_w"][i]), ffn_interleave(w["ffn_dw_b"][i:i + 1]),
                           name=f"ffn{i}_conv_act")
        return matmul(act, w["ffn_w_down"][i], res=xin, name=f"ffn{i}_down"), (hf, a, act, w_up)

    def ffn_bwd(i, xin, saved, dxo):
        hf, a, act, w_up = saved
        dact = matmul(dxo, w["ffn_w_down"][i], tb=True, out_dtype=BF16, name=f"ffn{i}_down_dx")
        dw_down = matmul(act, dxo, ta=True, name=f"ffn{i}_down_dw")
        da, ddw_w, ddw_b = ffn_conv_act_bwd(a, ffn_interleave(w["ffn_dw_w"][i]), ffn_interleave(w["ffn_dw_b"][i:i + 1]),
                                            dact, name=f"ffn{i}_conv_act_bwd")
        dw_up = ffn_pairs_to_shards(matmul(hf, da, ta=True, name=f"ffn{i}_up_dw"), name=f"ffn{i}_up_dw_shards")
        dhf = matmul(da, w_up, tb=True, name=f"ffn{i}_up_dx")
        dxin, dnorm = rms_bwd(xin, w["ffn_norm"][i:i + 1], dhf, dxo, f"ffn{i}_norm_bwd")
        return dxin, dict(ffn_norm=dnorm, ffn_w_up=dw_up, ffn_dw_w=ffn_deinterleave(ddw_w),
                          ffn_dw_b=ffn_deinterleave(ddw_b), ffn_w_down=dw_down)

    w_in_e = jnp.pad(w["e_w_in"][0], ((0, 0), (0, EVEN_IN_PAD - EVEN_IN)))
    conv_w_e, conv_b_e = w["e_conv_w"][0], w["e_conv_b"]
    dt_bias, a_log, d_skip = _pad_heads(w["e_dt_bias"]), _pad_heads(w["e_a_log"]), _pad_heads(w["e_d"])
    xbc_off = 4 * D_MODEL

    hn0 = rms_fwd(x, w["mix_norm"][0:1], "mix0_norm")
    proj0 = matmul(hn0, w_in_e, out_dtype=BF16, name="even_in")
    dt_raw = matmul(hn0, w_in_e[:, EVEN_IN_PAD - LANE:], name="even_in_dt")
    y_ret, ret_states = retention_fwd(proj0, cos2, sin2)
    xbc = conv_fwd(proj0, conv_w_e, conv_b_e, act=True, off=xbc_off, name="ssd_conv")
    y_ssm, ssd_states = ssd_fwd(xbc, dt_raw, proj0, dt_bias, a_log, d_skip, w["e_ssm_norm"])
    mix0 = jnp.concatenate([y_ret, y_ssm], axis=1)
    if late_weights is not None:
        w.update(late_weights(y_ssm))
    w_out_e = w["e_w_out"][0]
    x1 = matmul(mix0, w_out_e, res=x, name="even_out")
    x2, ffn0_saved = ffn_fwd(0, x1)

    w_in_o, w_out_o, glu_w = w["o_w_in"][0], w["o_w_out"][0], w["o_glu_w"][0]
    dw_w_o, dw_b_o, ln_g, ln_b, d_o = w["o_dw_w"][0], w["o_dw_b"], w["o_ln_g"], w["o_ln_b"], w["o_d"]
    rep = jnp.asarray(np.repeat(np.eye(S5_GROUPS, dtype=np.float32), S5_GROUP, axis=0))
    rows_gc = (S5_GROUPS * S5_GROUP, S5_STATE)
    prep_in = [w["o_a_re"][0], w["o_a_im"][0], w["o_log_step"].reshape(S5_GROUPS, 1),
               w["o_b_re"][0].transpose(0, 2, 1).reshape(rows_gc), w["o_b_im"][0].transpose(0, 2, 1).reshape(rows_gc), rep]
    ab_re, ab_im, bb_re, bb_im = whole_fwd(
        _s5_prep, prep_in, [(S5_GROUPS, S5_STATE)] * 2 + [rows_gc] * 2, name="s5_prep")
    a_re_row, a_im_row = ab_re.reshape(1, S5_LANES), ab_im.reshape(1, S5_LANES)
    b_re_bd, b_im_bd = _block_diag(bb_re).astype(BF16), _block_diag(bb_im).astype(BF16)
    c_re_bd = _block_diag(w["o_c_re"][0].reshape(rows_gc)).astype(BF16)
    c_im_neg_bd = _block_diag(-w["o_c_im"][0].reshape(rows_gc)).astype(BF16)

    hn1 = rms_fwd(x2, w["mix_norm"][1:2], "mix1_norm")
    proj1 = matmul(hn1, w_in_o, name="odd_in")
    half = D_MODEL // 2
    c_glu = rowwise_fwd(_glu, [Cols(proj1, half, 0), Cols(proj1, half, 1)], [], [], [], [(half, F32)],
                        name="conf_glu", tl=512)[0]
    c_conv = conv_fwd(c_glu, dw_w_o, dw_b_o, act=False, name="conf_conv")
    c_out = rowwise_fwd(_ln_silu, [c_conv], [], [ln_g, ln_b], [], [(half, BF16)], name="conf_ln", tl=512)[0]
    u_seg = _seg_interleave(proj1[:, 2 * half:])
    bu_re = matmul(u_seg, b_re_bd, name="s5_bu_re")
    bu_im = matmul(u_seg, b_im_bd, name="s5_bu_im")
    xs_re, xs_im = s5_scan(bu_re, bu_im, a_re_row, a_im_row, name="s5_scan")
    y_im = matmul(xs_im, c_im_neg_bd, tb=True, name="s5_y_im")
    y_s5 = _seg_deinterleave(matmul(xs_re, c_re_bd, tb=True, res=y_im, name="s5_y_re"))
    s_out = rowwise_fwd(_s5_post, [y_s5, Cols(proj1, half, 2)], [], [d_o, glu_w], [], [(half, BF16)],
                        name="s5_post", tl=512)[0]
    mix1 = jnp.concatenate([c_out, s_out], axis=1)
    x3 = matmul(mix1, w_out_o, res=x2, name="odd_out")
    x4, ffn1_saved = ffn_fwd(1, x3)

    loss, dx4, dfinal = loss_head(x4, tgt, w["final_norm"].reshape(1, D_MODEL))
    grads["final_norm"] = dfinal.reshape(D_MODEL)

    dx3, g_ffn1 = ffn_bwd(1, x3, ffn1_saved, dx4)
    dmix1 = matmul(dx3, w_out_o, tb=True, out_dtype=BF16, name="odd_out_dx")
    grads["o_w_out"] = [matmul(mix1, dx3, ta=True, name="odd_out_dw")]
    dc_conv, dln_g, dln_b = rowwise_bwd(_ln_silu, [c_conv], [], [ln_g, ln_b], [], [Cols(dmix1, half, 0)], [F32],
                                        name="conf_ln_bwd", tl=512)
    dc_glu, ddw_w_o, ddw_b_o = conv_bwd(c_glu, dw_w_o, dw_b_o, dc_conv, act=False, name="conf_conv_bwd")
    d_cacg = rowwise_bwd(_glu, [Cols(proj1, half, 0), Cols(proj1, half, 1)], [], [], [], [dc_glu], [BF16],
                         name="conf_glu_bwd", tl=512, merge=True)[0]
    dy_s5, du_post, dd_o, dglu_w = rowwise_bwd(
        _s5_post, [y_s5, Cols(proj1, half, 2)], [], [d_o, glu_w], [], [Cols(dmix1, half, 1)], [F32, F32],
        name="s5_post_bwd", tl=512)
    dy_seg = _seg_interleave(dy_s5)
    dxs_re = matmul(dy_seg, c_re_bd, name="s5_dx_re")
    dxs_im = matmul(dy_seg, c_im_neg_bd, name="s5_dx_im")
    dc_re_bd = matmul(dy_seg, xs_re, ta=True, name="s5_dc_re")
    dc_im_neg_bd = matmul(dy_seg, xs_im, ta=True, name="s5_dc_im")
    g_re, g_im, dab_re, dab_im = s5_scan(dxs_re, dxs_im, a_re_row, -a_im_row, reverse=True, states=(xs_re, xs_im),
                                         name="s5_scan_bwd", lw=LANE)
    dbb_re = _block_diag_take(matmul(u_seg, g_re, ta=True, name="s5_db_re"), name="s5_db_re_diag")
    dbb_im = _block_diag_take(matmul(u_seg, g_im, ta=True, name="s5_db_im"), name="s5_db_im_diag")
    du_im = matmul(g_im, b_im_bd, tb=True, name="s5_du_im")
    du = _seg_deinterleave(matmul(g_re, b_re_bd, tb=True, res=du_im, name="s5_du_re")) + du_post
    da_re, da_im, dlog_step, db_re, db_im = whole_bwd(
        _s5_prep, prep_in, 5,
        [dab_re.reshape(S5_GROUPS, S5_STATE), dab_im.reshape(S5_GROUPS, S5_STATE), dbb_re, dbb_im], name="s5_prep_bwd")
    gcn = (S5_GROUPS, S5_GROUP, S5_STATE)
    grads.update(
        o_a_re=da_re[None], o_a_im=da_im[None], o_log_step=dlog_step.reshape(1, S5_GROUPS),
        o_b_re=db_re.reshape(gcn).transpose(0, 2, 1)[None], o_b_im=db_im.reshape(gcn).transpose(0, 2, 1)[None],
        o_c_re=_block_diag_take(dc_re_bd, name="s5_dc_re_diag").reshape(gcn)[None],
        o_c_im=-_block_diag_take(dc_im_neg_bd, name="s5_dc_im_diag").reshape(gcn)[None],
        o_d=dd_o, o_glu_w=[dglu_w], o_dw_w=ddw_w_o[None], o_dw_b=ddw_b_o, o_ln_g=dln_g, o_ln_b=dln_b)
    dproj1 = jnp.concatenate([d_cacg, du.astype(BF16)], axis=1)
    grads["o_w_in"] = [matmul(hn1, dproj1, ta=True, name="odd_in_dw")]
    dhn1 = matmul(dproj1, w_in_o, tb=True, name="odd_in_dx")
    dx2, dmix_norm1 = rms_bwd(x2, w["mix_norm"][1:2], dhn1, dx3, "mix1_norm_bwd")

    if early_reduce is not None:
        zero = early_reduce({("o_w_in", 0): grads["o_w_in"][0], ("o_glu_w", 0): grads["o_glu_w"][0],
                             ("o_w_out", 0): grads["o_w_out"][0], ("ffn_w_up", 1): g_ffn1["ffn_w_up"],
                             ("ffn_w_down", 1): g_ffn1["ffn_w_down"]})
        w["ffn_dw_b"] = w["ffn_dw_b"] + zero
    dx1, g_ffn0 = ffn_bwd(0, x1, ffn0_saved, dx2)
    if early_reduce is not None:
        dt_bias = dt_bias + early_reduce({("ffn_w_up", 0): g_ffn0["ffn_w_up"], ("ffn_w_down", 0): g_ffn0["ffn_w_down"]})
    for k in g_ffn0:
        per_layer = [g_ffn0[k], g_ffn1[k]]
        grads[k] = per_layer if k in ("ffn_w_up", "ffn_w_down") else jnp.stack(per_layer).reshape(w[k].shape)
    dmix0 = matmul(dx1, w_out_e, tb=True, out_dtype=BF16, name="even_out_dx")
    grads["e_w_out"] = [matmul(mix0, dx1, ta=True, name="even_out_dw")]
    if early_reduce is not None:
        a_log = a_log + early_reduce({("e_w_out", 0): grads["e_w_out"][0]})
    dq, dk, dv, dg = retention_bwd(proj0, cos2, sin2, ret_states, dmix0)
    dxbc_c, ddt, dz, ddt_bias, da_log, dd_skip, dssm_norm = ssd_bwd(
        xbc, dt_raw, proj0, dt_bias, a_log, d_skip, w["e_ssm_norm"], ssd_states, dmix0)
    dxbc, dconv_w, dconv_b = conv_bwd(proj0, conv_w_e, conv_b_e, dxbc_c, act=True, off=xbc_off,
                                      name="ssd_conv_bwd", dx_dtype=BF16)
    dproj0 = jnp.concatenate([dq, dk, dv, dg, dz, dxbc, ddt], axis=1)
    grads["e_w_in"] = [matmul(hn0, dproj0, ta=True, name="even_in_dw")[:, :EVEN_IN]]
    norm_w0 = w["mix_norm"][0:1]
    if early_reduce is not None:
        norm_w0 = norm_w0 + early_reduce({("e_w_in", 0): grads["e_w_in"][0]})
    dhn0 = matmul(dproj0, w_in_e, tb=True, name="even_in_dx")
    dx, dmix_norm0 = rms_bwd(x, norm_w0, dhn0, dx1, "mix0_norm_bwd")
    grads.update(
        mix_norm=jnp.concatenate([dmix_norm0, dmix_norm1], axis=0), e_conv_w=dconv_w[None], e_conv_b=dconv_b,
        e_dt_bias=ddt_bias[:, :SSM_HEADS], e_a_log=da_log[:, :SSM_HEADS], e_d=dd_skip[:, :SSM_HEADS],
        e_ssm_norm=dssm_norm)
    return loss, dx, grads


def adamw(w, g, m, v, *, name):
    shape = w.shape
    cols = shape[-1]
    rows = w.size // cols
    tr = _tile(rows, max(8, (512 * 1024 // cols) // 8 * 8), unit=8)

    def body(w_ref, g_ref, m_ref, v_ref, d_ref, nm_ref, nv_ref):
        gv = g_ref[...]
        nm = ADAM_B1 * m_ref[...] + (1.0 - ADAM_B1) * gv
        nv = ADAM_B2 * v_ref[...] + (1.0 - ADAM_B2) * jnp.square(gv)
        m_hat = nm / (1.0 - ADAM_B1 ** ADAM_STEP)
        v_hat = nv / (1.0 - ADAM_B2 ** ADAM_STEP)
        d_ref[...] = -ADAM_LR * (m_hat / (jnp.sqrt(v_hat) + ADAM_EPS) + ADAM_WD * w_ref[...])
        nm_ref[...] = nm
        nv_ref[...] = nv

    spec = pl.BlockSpec((tr, cols), lambda i: (i, 0))
    outs = pl.pallas_call(
        body, name=name, grid=(rows // tr,), in_specs=[spec] * 4, out_specs=[spec] * 3,
        out_shape=[jax.ShapeDtypeStruct((rows, cols), F32)] * 3, compiler_params=_params(("parallel",)),
    )(*[t.reshape(rows, cols) for t in (w, g, m, v)])
    return [o.reshape(shape) for o in outs]


OTHER_CHIPS = ((1, 0), (0, 1), (1, 1))
ANY = pl.BlockSpec(memory_space=pl.ANY)


def _position():
    return lax.axis_index("x"), lax.axis_index("y"), lax.axis_index("c")


def _flip(v, f):
    return 1 - v if f else v


def _remote(src, dst, send_sem, recv_sem, device):
    return pltpu.make_async_remote_copy(src_ref=src, dst_ref=dst, send_sem=send_sem, recv_sem=recv_sem,
                                        device_id=device, device_id_type=MESH)


def gather_shards(big, small):
    n_big, n_small = len(big), len(small)
    halves = [a.shape[0] // 2 for a in big]

    def body(*refs):
        big_refs, small_refs = refs[:n_big], refs[n_big:n_big + n_small]
        obig_refs = refs[n_big + n_small:2 * n_big + n_small]
        osmall_refs = refs[2 * n_big + n_small:2 * (n_big + n_small)]
        ici_send, ici_recv, d2d_send, d2d_recv, small_send, small_recv = refs[2 * (n_big + n_small):]
        x, y, c = _position()
        mine = 2 * x + y

        def half(k, core):
            return pl.ds(pl.multiple_of(core * halves[k], 16), halves[k])

        sends = []
        for j, (fx, fy) in enumerate(OTHER_CHIPS):
            peer = (_flip(x, fx), _flip(y, fy), c)
            for k in range(n_big):
                sends.append(_remote(big_refs[k].at[half(k, c)], obig_refs[k].at[mine, half(k, c)],
                                     ici_send.at[j, k], ici_recv.at[j, k], peer))
            for k in range(n_small):
                sends.append(_remote(small_refs[k], osmall_refs[k].at[mine], small_send.at[j, k], small_recv.at[j, k], peer))
        for cp in sends:
            cp.start()
        for j, (fx, fy) in enumerate(OTHER_CHIPS):
            px, py = _flip(x, fx), _flip(y, fy)
            src_chip = 2 * px + py
            for k in range(n_big):
                landed = obig_refs[k].at[src_chip, half(k, c)]
                _remote(landed, landed, ici_send.at[j, k], ici_recv.at[j, k], (px, py, c)).wait_recv()
                fwd = _remote(landed, landed, d2d_send.at[j, k], d2d_recv.at[j, k], (x, y, 1 - c))
                fwd.start()
                sends.append(fwd)
        for j, (fx, fy) in enumerate(OTHER_CHIPS):
            px, py = _flip(x, fx), _flip(y, fy)
            src_chip = 2 * px + py
            for k in range(n_big):
                other = obig_refs[k].at[src_chip, half(k, 1 - c)]
                _remote(other, other, d2d_send.at[j, k], d2d_recv.at[j, k], (x, y, 1 - c)).wait_recv()
            for k in range(n_small):
                dst = osmall_refs[k].at[src_chip]
                _remote(small_refs[k], dst, small_send.at[j, k], small_recv.at[j, k], (px, py, c)).wait_recv()
        for cp in sends:
            cp.wait_send()

    arrays = list(big) + list(small)
    dma = pltpu.SemaphoreType.DMA
    return pl.pallas_call(
        body, name="gather_shards", in_specs=[ANY] * len(arrays), out_specs=[ANY] * len(arrays),
        out_shape=[jax.ShapeDtypeStruct((4,) + a.shape, a.dtype) for a in arrays],
        scratch_shapes=[dma((3, n_big)), dma((3, n_big)), dma((3, n_big)), dma((3, n_big)),
                        dma((3, n_small)), dma((3, n_small))],
        compiler_params=_params(),
    )(*arrays)


def allreduce_small(pack):
    rows = pack.shape[0]
    half = rows // 2

    def body(p_ref, o_ref, sibling_pack, chip_sum, chip_halves, total, sems):
        x, y, c = _position()
        sibling = (x, y, 1 - c)
        swap = _remote(p_ref, sibling_pack, sems.at[0, 0], sems.at[1, 0], sibling)
        swap.start()
        swap.wait()
        chip_sum[...] = p_ref[...] + sibling_pack[...]
        mine = pl.ds(pl.multiple_of(c * half, 8), half)
        other = pl.ds(pl.multiple_of((1 - c) * half, 8), half)
        chip = 2 * x + y
        chip_halves[chip] = chip_sum[mine, :]
        sends = []
        for j, (fx, fy) in enumerate(OTHER_CHIPS):
            sends.append(_remote(chip_sum.at[mine], chip_halves.at[chip], sems.at[0, 1 + j], sems.at[1, 1 + j],
                                 (_flip(x, fx), _flip(y, fy), c)))
        for cp in sends:
            cp.start()
        for j, (fx, fy) in enumerate(OTHER_CHIPS):
            px, py = _flip(x, fx), _flip(y, fy)
            _remote(chip_sum.at[mine], chip_halves.at[2 * px + py], sems.at[0, 1 + j], sems.at[1, 1 + j], (px, py, c)).wait_recv()
        for cp in sends:
            cp.wait_send()
        total[...] = ((chip_halves[0] + chip_halves[1]) + chip_halves[2]) + chip_halves[3]
        o_ref[mine, :] = total[...]
        share = _remote(total, o_ref.at[mine], sems.at[0, 4], sems.at[1, 4], sibling)
        share.start()
        _remote(total, o_ref.at[other], sems.at[0, 4], sems.at[1, 4], sibling).wait_recv()
        share.wait_send()

    vmem = pl.BlockSpec(memory_space=pltpu.VMEM)
    return pl.pallas_call(
        body, name="allreduce_small", in_specs=[vmem], out_specs=vmem,
        out_shape=jax.ShapeDtypeStruct(pack.shape, F32),
        scratch_shapes=[pltpu.VMEM((rows, LANE), F32), pltpu.VMEM((rows, LANE), F32), pltpu.VMEM((4, half, LANE), F32),
                        pltpu.VMEM((half, LANE), F32), pltpu.SemaphoreType.DMA((2, 5))],
        compiler_params=_params(),
    )(pack)


def exchange_halves(gs, *, name):
    n = len(gs)

    def body(*refs):
        g_refs, o_refs, (send_sems, recv_sems) = refs[:n], refs[n:2 * n], refs[2 * n:]
        x, y, c = _position()
        copies = [_remote(g_refs[k].at[:, 1 - c], o_refs[k], send_sems.at[k], recv_sems.at[k], (x, y, 1 - c)) for k in range(n)]
        for cp in copies:
            cp.start()
        for cp in copies:
            cp.wait()

    return pl.pallas_call(
        body, name=name, in_specs=[ANY] * n, out_specs=[ANY] * n,
        out_shape=[jax.ShapeDtypeStruct((4,) + g.shape[2:], g.dtype) for g in gs],
        scratch_shapes=[pltpu.SemaphoreType.DMA((n,)), pltpu.SemaphoreType.DMA((n,))],
        compiler_params=_params(),
    )(*gs)


def swap_halves(rs):
    n = len(rs)

    def body(*refs):
        r_refs, o_refs, (send_sems, recv_sems) = refs[:n], refs[n:2 * n], refs[2 * n:]
        x, y, c = _position()
        copies = [_remote(r_refs[k], o_refs[k], send_sems.at[k], recv_sems.at[k], (x, y, 1 - c)) for k in range(n)]
        for cp in copies:
            cp.start()
        for cp in copies:
            cp.wait()

    dma = pltpu.SemaphoreType.DMA
    return pl.pallas_call(
        body, name="swap_halves", in_specs=[ANY] * n, out_specs=[ANY] * n,
        out_shape=[jax.ShapeDtypeStruct(r.shape, r.dtype) for r in rs],
        scratch_shapes=[dma((n,)), dma((n,))],
        compiler_params=_params(),
    )(*rs)


HBM = pl.BlockSpec(memory_space=pltpu.HBM)
SEM = pl.BlockSpec(memory_space=pltpu.SEMAPHORE)
SIDE_EFFECT = pltpu.SideEffectType.DATAFLOW_SIDE_EFFECTING


def _gather_plan(halves):
    def plan(v_refs, land_refs, x, y, c):
        copies = []
        for fx, fy in OTHER_CHIPS:
            for k in range(len(v_refs)):
                rows = pl.ds(pl.multiple_of(c * halves[k], 16), halves[k])
                copies.append((v_refs[k].at[rows], land_refs[k].at[2 * x + y, rows], (_flip(x, fx), _flip(y, fy), c)))
        return copies
    return plan


def _scatter_plan(v_refs, land_refs, x, y, c):
    copies = []
    for j, (fx, fy) in enumerate(OTHER_CHIPS):
        px, py = _flip(x, fx), _flip(y, fy)
        for k in range(len(v_refs)):
            copies.append((v_refs[k].at[2 * px + py], land_refs[k].at[j], (px, py, c)))
    return copies


def chip_exchange_start(srcs, land_shapes, plan, after, *, name):
    n = len(srcs)
    n_cp = 3 * n

    def body(*refs):
        v_refs, land_refs = refs[:n], refs[n:2 * n]
        outs = refs[2 * n + 1:]
        sends, recvs, token = outs[:n_cp], outs[n_cp:2 * n_cp], outs[-1]
        x, y, c = _position()
        for (src, dst, device), send, recv in zip(plan(v_refs, land_refs, x, y, c), sends, recvs, strict=True):
            _remote(src, dst, send, recv, device).start()
        token[...] = jnp.zeros_like(token)

    lands = [lax.empty(shape, v.dtype) for shape, v in zip(land_shapes, srcs)]
    arrays = [pltpu.with_memory_space_constraint(a, pltpu.HBM) for a in list(srcs) + lands]
    outs = pl.pallas_call(
        body, name=name,
        out_shape=tuple(pltpu.SemaphoreType.DMA(()) for _ in range(2 * n_cp))
        + tuple(pltpu.HBM(a.shape, a.dtype) for a in arrays) + (jax.ShapeDtypeStruct((8, LANE), F32),),
        in_specs=[HBM] * (2 * n) + [ANY],
        out_specs=(SEM,) * (2 * n_cp) + (HBM,) * (2 * n) + (pl.BlockSpec(memory_space=pltpu.VMEM),),
        input_output_aliases={i: 2 * n_cp + i for i in range(2 * n)},
        compiler_params=pltpu.CompilerParams(has_side_effects=SIDE_EFFECT),
    )(*arrays, after)
    handle = (outs[:n_cp], outs[n_cp:2 * n_cp], outs[2 * n_cp:2 * n_cp + n], outs[2 * n_cp + n:2 * n_cp + 2 * n])
    return handle, outs[-1]


def chip_exchange_wait(handle, plan, after, *, name):
    sends, recvs, v_thru, land_thru = handle
    n = len(v_thru)
    n_cp = 3 * n

    def body(*refs):
        v_refs, land_refs = refs[:n], refs[n:2 * n]
        sends, recvs = refs[2 * n:2 * n + n_cp], refs[2 * n + n_cp:2 * n + 2 * n_cp]
        x, y, c = _position()
        for (src, dst, device), send, recv in zip(plan(v_refs, land_refs, x, y, c), sends, recvs, strict=True):
            copy = _remote(src, dst, send, recv, device)
            copy.wait_send()
            copy.wait_recv()

    outs = pl.pallas_call(
        body, name=name,
        out_shape=tuple(pltpu.HBM(a.shape, a.dtype) for a in list(v_thru) + list(land_thru)),
        in_specs=[HBM] * (2 * n) + [SEM] * (2 * n_cp) + [ANY], out_specs=(HBM,) * (2 * n),
        input_output_aliases={i: i for i in range(2 * n)},
        compiler_params=pltpu.CompilerParams(has_side_effects=SIDE_EFFECT),
    )(*v_thru, *land_thru, *sends, *recvs, after)
    return outs[:n], outs[n:]


def finish_gather(lands):
    n = len(lands)
    halves = [a.shape[1] // 2 for a in lands]

    def body(*refs):
        o_refs, (send_sems, recv_sems) = refs[n:2 * n], refs[2 * n:]
        x, y, c = _position()

        def half(k, core):
            return pl.ds(pl.multiple_of(core * halves[k], 16), halves[k])

        sends = []
        for j, (fx, fy) in enumerate(OTHER_CHIPS):
            src_chip = 2 * _flip(x, fx) + _flip(y, fy)
            for k in range(n):
                held = o_refs[k].at[src_chip, half(k, c)]
                sends.append(_remote(held, held, send_sems.at[j, k], recv_sems.at[j, k], (x, y, 1 - c)))
        for cp in sends:
            cp.start()
        for j, (fx, fy) in enumerate(OTHER_CHIPS):
            src_chip = 2 * _flip(x, fx) + _flip(y, fy)
            for k in range(n):
                other = o_refs[k].at[src_chip, half(k, 1 - c)]
                _remote(other, other, send_sems.at[j, k], recv_sems.at[j, k], (x, y, 1 - c)).wait_recv()
        for cp in sends:
            cp.wait_send()

    dma = pltpu.SemaphoreType.DMA
    return pl.pallas_call(
        body, name="finish_gather", in_specs=[ANY] * n, out_specs=[ANY] * n,
        out_shape=[jax.ShapeDtypeStruct(a.shape, a.dtype) for a in lands],
        input_output_aliases={k: k for k in range(n)},
        scratch_shapes=[dma((3, n)), dma((3, n))],
        compiler_params=_params(),
    )(*lands)


def add_own_half(g, r, c_idx, *, name):
    _, _, h, cols = g.shape

    def body(c_ref, g_ref, r_ref, o_ref):
        o_ref[...] = (g_ref[0] + r_ref[...]).astype(o_ref.dtype)

    return pl.pallas_call(
        body, name=name,
        grid_spec=pltpu.PrefetchScalarGridSpec(
            num_scalar_prefetch=1, grid=(4,),
            in_specs=[pl.BlockSpec((1, 1, h, cols), lambda s, c: (s, c[0], 0, 0)),
                      pl.BlockSpec((1, h, cols), lambda s, c: (s, 0, 0))],
            out_specs=pl.BlockSpec((1, h, cols), lambda s, c: (s, 0, 0))),
        out_shape=jax.ShapeDtypeStruct(r.shape, BF16), compiler_params=_params(("parallel",)),
    )(c_idx, g, r)


def add_chip_parts(a, parts, chip_idx, *, name):
    _, h, cols = a.shape
    th = h // 2

    def body(s_ref, a_ref, p0_ref, p1_ref, p2_ref, o_ref):
        f = lambda r: r[0].astype(F32)
        o_ref[...] = ((f(a_ref) + f(p0_ref)) + f(p1_ref)) + f(p2_ref)

    part = lambda j: pl.BlockSpec((1, th, cols), lambda i, s, j=j: (j, i, 0))
    return pl.pallas_call(
        body, name=name,
        grid_spec=pltpu.PrefetchScalarGridSpec(
            num_scalar_prefetch=1, grid=(2,),
            in_specs=[pl.BlockSpec((1, th, cols), lambda i, s: (s[0], i, 0)), part(0), part(1), part(2)],
            out_specs=pl.BlockSpec((th, cols), lambda i, s: (i, 0))),
        out_shape=jax.ShapeDtypeStruct((h, cols), F32), compiler_params=_params(("parallel",)),
    )(chip_idx, a, parts, parts, parts)


WEIGHTS = ("mix_norm", "e_w_in", "e_conv_w", "e_conv_b", "e_dt_bias", "e_a_log", "e_d", "e_ssm_norm", "e_w_out",
           "o_w_in", "o_dw_w", "o_dw_b", "o_ln_g", "o_ln_b", "o_a_re", "o_a_im", "o_b_re", "o_b_im", "o_c_re",
           "o_c_im", "o_d", "o_log_step", "o_glu_w", "o_w_out", "ffn_norm", "ffn_w_up", "ffn_dw_w", "ffn_dw_b",
           "ffn_w_down", "final_norm")
BIG = (("e_w_in", 2), ("e_w_out", 1), ("o_w_in", 2), ("o_glu_w", 1), ("o_w_out", 1), ("ffn_w_up", 2), ("ffn_w_down", 1))
SMALL_SHARDED = (("e_conv_w", 2), ("o_dw_w", 2), ("o_dw_b", 1), ("o_ln_g", 1), ("o_ln_b", 1), ("o_d", 1), ("ffn_dw_w", 2))
REPLICATED = tuple(n for n in WEIGHTS if n not in dict(BIG + SMALL_SHARDED))
PACK_ROWS = 8


def _pack(arrays, dtype, row_unit=PACK_ROWS):
    flat = jnp.concatenate([a.astype(dtype).reshape(-1) for a in arrays])
    rows = -(-flat.size // (LANE * row_unit)) * row_unit
    return jnp.pad(flat, (0, rows * LANE - flat.size)).reshape(rows, LANE)


def _unpack(flat, shapes, lead=()):
    out, off = [], 0
    for shape in shapes:
        size = int(np.prod(shape))
        out.append(flat[..., off:off + size].reshape(lead + tuple(shape)))
        off += size
    return out


def _join_shards(parts, axis):
    return jnp.concatenate([parts[s] for s in range(4)], axis=axis)


def _rows2d(a):
    return a.reshape(-1, a.shape[-1])


def _layer_shards(g, axis):
    if g.ndim == 3:
        return g.reshape(4, 2, g.shape[1] // 2, g.shape[2])
    rows, cols = g.shape
    if axis == 0:
        return g.reshape(4, 2, rows // 8, cols)
    return g.reshape(rows, 4, cols // 4).transpose(1, 0, 2).reshape(4, 2, rows // 2, cols // 4)


def kernel(x, mix_norm, e_w_in, e_conv_w, e_conv_b, e_dt_bias, e_a_log, e_d, e_ssm_norm, e_w_out, o_w_in, o_dw_w, o_dw_b, o_ln_g, o_ln_b, o_a_re, o_a_im, o_b_re, o_b_im, o_c_re, o_c_im, o_d, o_log_step, o_glu_w, o_w_out, ffn_norm, ffn_w_up, ffn_dw_w, ffn_dw_b, ffn_w_down, final_norm, loss_target, m_mix_norm, m_e_w_in, m_e_conv_w, m_e_conv_b, m_e_dt_bias, m_e_a_log, m_e_d, m_e_ssm_norm, m_e_w_out, m_o_w_in, m_o_dw_w, m_o_dw_b, m_o_ln_g, m_o_ln_b, m_o_a_re, m_o_a_im, m_o_b_re, m_o_b_im, m_o_c_re, m_o_c_im, m_o_d, m_o_log_step, m_o_glu_w, m_o_w_out, m_ffn_norm, m_ffn_w_up, m_ffn_dw_w, m_ffn_dw_b, m_ffn_w_down, m_final_norm, v_mix_norm, v_e_w_in, v_e_conv_w, v_e_conv_b, v_e_dt_bias, v_e_a_log, v_e_d, v_e_ssm_norm, v_e_w_out, v_o_w_in, v_o_dw_w, v_o_dw_b, v_o_ln_g, v_o_ln_b, v_o_a_re, v_o_a_im, v_o_b_re, v_o_b_im, v_o_c_re, v_o_c_im, v_o_d, v_o_log_step, v_o_glu_w, v_o_w_out, v_ffn_norm, v_ffn_w_up, v_ffn_dw_w, v_ffn_dw_b, v_ffn_w_down, v_final_norm):
    given = dict(locals())
    chip = 2 * lax.axis_index("x") + lax.axis_index("y")
    core = lax.axis_index("c")

    core_idx, chip_idx = core.reshape(1).astype(jnp.int32), chip.reshape(1).astype(jnp.int32)

    def whole(n, axis, parts):
        shape = given[n].shape
        own = given[n].astype(parts.dtype)
        return _join_shards(lax.dynamic_update_index_in_dim(parts.reshape((4,) + shape), own, chip, 0), axis)

    first, later = BIG[:1], BIG[1:]
    shards = {n: _rows2d(given[n]).astype(BF16) for n, _ in BIG}
    gathered = gather_shards([shards[n] for n, _ in first], [_rows2d(given[n]) for n, _ in SMALL_SHARDED])
    w = {n: given[n] for n in REPLICATED}
    for (n, axis), parts in zip(first + SMALL_SHARDED, gathered):
        w[n] = whole(n, axis, parts)
    later_keys = [(n, layer) for n, _ in later for layer in (range(2) if n.startswith("ffn_") else [None])]
    later_shards = [shards[n] if layer is None else given[n][layer].astype(BF16) for n, layer in later_keys]
    gather_plan = _gather_plan([a.shape[0] // 2 for a in later_shards])
    gather_handle, token = chip_exchange_start(later_shards, [(4,) + a.shape for a in later_shards], gather_plan,
                                               gathered[0], name="gather_start")
    w["mix_norm"] = w["mix_norm"] + token[0, 0]

    def late_weights(after):
        _, lands = chip_exchange_wait(gather_handle, gather_plan, after, name="gather_wait")
        out = {"ffn_w_up_pairs": [], "ffn_w_down": []}
        for (n, layer), own, parts in zip(later_keys, later_shards, finish_gather(lands)):
            if layer is None:
                out[n] = whole(n, dict(BIG)[n], parts)
                continue
            parts = lax.dynamic_update_index_in_dim(parts, own, chip, 0)
            if n == "ffn_w_up":
                out["ffn_w_up_pairs"].append(ffn_shards_to_pairs(parts, name=f"ffn{layer}_up_pairs"))
            else:
                out[n].append(parts.reshape(-1, parts.shape[-1]))
        return out

    groups = []

    def finish_group(after):
        group = groups[-1]
        group["sums"], group["parts"] = chip_exchange_wait(group.pop("handle"), _scatter_plan, after,
                                                           name=f"scatter_wait_{len(groups) - 1}")

    def early_reduce(layer_grads):
        keys = list(layer_grads)
        if groups:
            finish_group(layer_grads[keys[0]])
        tag = len(groups)
        parts = [_layer_shards(layer_grads[k], dict(BIG)[k[0]] - 1) for k in keys]
        sums = [add_own_half(g, r, core_idx, name=f"add_own_half_{n}{layer}")
                for g, r, (n, layer) in zip(parts, exchange_halves(parts, name=f"exchange_halves_{tag}"), keys)]
        handle, zeros = chip_exchange_start(sums, [(3,) + a.shape[1:] for a in sums], _scatter_plan, sums[0],
                                            name=f"scatter_start_{tag}")
        groups.append(dict(keys=keys, handle=handle))
        return zeros[0, 0]

    loss, dx, grads = local_step(x[0], loss_target[0], w, late_weights, early_reduce)
    finish_group(dx)
    keys = [k for group in groups for k in group["keys"]]
    core_sums = [a for group in groups for a in group["sums"]]
    chip_parts = [a for group in groups for a in group["parts"]]
    assert sorted(keys) == sorted((n, layer) for n, _ in BIG for layer in range(len(grads[n]))), keys

    small_names = REPLICATED + tuple(n for n, _ in SMALL_SHARDED)
    small_sum = allreduce_small(_pack([grads[n] for n in small_names], F32, row_unit=16))
    reduced = dict(zip(small_names, _unpack(small_sum.reshape(-1), [grads[n].shape for n in small_names])))
    for n, axis in SMALL_SHARDED:
        width = given[n].shape[axis]
        reduced[n] = lax.dynamic_slice_in_dim(reduced[n], chip * width, width, axis=axis)

    mine = [add_chip_parts(a, p, chip_idx, name=f"add_chip_parts_{n}{layer}")
            for a, p, (n, layer) in zip(core_sums, chip_parts, keys)]
    layers = {}
    for (n, layer), own, other in zip(keys, mine, swap_halves(mine)):
        both = jnp.where(core == 0, jnp.stack([own, other]), jnp.stack([other, own]))
        layers.setdefault(n, {})[layer] = both.reshape(given[n].shape[1:])
    for n, _ in BIG:
        reduced[n] = jnp.stack([layers[n][layer] for layer in sorted(layers[n])])

    delta, new_m, new_v = {}, {}, {}
    for n in WEIGHTS:
        delta[n], new_m[n], new_v[n] = adamw(given[n], reduced[n], given["m_" + n], given["v_" + n], name="adamw_" + n)

    total = lax.psum(loss[0, 0], ("x", "y", "c"))
    return (total, dx[None], *[reduced[n] for n in WEIGHTS], *[delta[n] for n in WEIGHTS],
            *[new_m[n] for n in WEIGHTS], *[new_v[n] for n in WEIGHTS])
```
